```python
import jax, jax.numpy as jnp
from jax import lax
import numpy as np

D_MODEL = 1024
BATCH = 4
SEQ = 4096
DEPTH = 2

N_MIXERS = 2
N_POOL_LAYERS = (DEPTH + 1) // 2
N_SGU_LAYERS = DEPTH // 2
POOL_WINDOWS = (2, 4, 8, 16)
POOL_GROUPS = 4
POOL_WIDTH = D_MODEL
POOL_GROUP_DIM = POOL_WIDTH // POOL_GROUPS
SGU_CHUNK = 128
SGU_HEADS = 4
SGU_WIDTH = 2 * D_MODEL
SGU_HEAD_DIM = SGU_WIDTH // SGU_HEADS
N_EXPERTS = 64
TOP_K = 8
N_GROUPS = 8
TOPK_GROUPS = 4
EXPERT_DIM = D_MODEL // 4
SHARED_DIM = EXPERT_DIM
ROUTED_SCALE = 2.5
EXPERT_BLOCK = 128
DEEPNORM_ALPHA = (2 * DEPTH) ** 0.25
DEEPNORM_BETA = (8 * DEPTH) ** -0.25
LN_EPS = 1e-5

kernel_name = "hybrid_pool_sgu_moe_deepnorm"


def layer_norm(x, g, b):
    xf = x.astype(jnp.float32)
    mu = jnp.mean(xf, axis=-1, keepdims=True)
    xc = xf - mu
    var = jnp.mean(jnp.square(xc), axis=-1, keepdims=True)
    y = xc * lax.rsqrt(var + LN_EPS) * g.astype(jnp.float32) + b.astype(jnp.float32)
    return y.astype(x.dtype)


def pool_mixer(x, w_in, w_grp, scale, w_out):
    bsz, s, _ = x.shape
    z = (x @ w_in).astype(jnp.float32).reshape(bsz, s, POOL_GROUPS, POOL_GROUP_DIM)
    cs = jnp.cumsum(z, axis=1)
    t = jnp.arange(1, s + 1, dtype=jnp.float32)
    pooled = []
    for g, w in enumerate(POOL_WINDOWS):
        c = cs[:, :, g]
        prev = jnp.pad(c[:, : s - w], ((0, 0), (w, 0), (0, 0)))
        mean = (c - prev) / jnp.minimum(t, float(w))[None, :, None]
        pooled.append(mean - z[:, :, g])
    pooled = jnp.stack(pooled, axis=2)
    y = jnp.einsum('bsgc,gcd->bsgd', pooled, w_grp.astype(jnp.float32))
    y = y.reshape(bsz, s, POOL_WIDTH) * scale.astype(jnp.float32)
    return y.astype(x.dtype) @ w_out


def sgu_mixer(x, w_in, b_in, ln_g, ln_b, w_s, b_s, w_out):
    bsz, s, _ = x.shape
    h = jax.nn.gelu(x @ w_in + b_in)
    u, v = h[..., :SGU_WIDTH], h[..., SGU_WIDTH:]
    v = layer_norm(v, ln_g, ln_b)
    v = v.reshape(bsz, s // SGU_CHUNK, SGU_CHUNK, SGU_HEADS, SGU_HEAD_DIM)
    causal = jnp.tril(jnp.ones((SGU_CHUNK, SGU_CHUNK), dtype=w_s.dtype))
    ws = w_s * causal[None]
    sv = jnp.einsum('hij,bnjhc->bnihc', ws, v) + b_s.T[:, :, None]
    gated = u * sv.reshape(bsz, s, SGU_WIDTH)
    return gated @ w_out


def moe(x, w_router, router_bias, w_gate, w_up, w_down, sh_gate, sh_up, sh_down):
    bsz, s, d = x.shape
    n_tok = bsz * s
    xf = x.reshape(n_tok, d)
    scores = jax.nn.sigmoid((xf @ w_router).astype(jnp.float32))
    biased = scores + router_bias.astype(jnp.float32)
    grouped = biased.reshape(n_tok, N_GROUPS, N_EXPERTS // N_GROUPS)
    group_score = lax.top_k(grouped, 2)[0].sum(-1)
    _, top_groups = lax.top_k(group_score, TOPK_GROUPS)
    group_mask = jax.nn.one_hot(top_groups, N_GROUPS, dtype=jnp.float32).sum(1) > 0
    expert_mask = jnp.repeat(group_mask, N_EXPERTS // N_GROUPS, axis=1)
    masked = jnp.where(expert_mask, biased, -jnp.inf)
    _, idx = lax.top_k(masked, TOP_K)
    gate = jnp.take_along_axis(scores, idx, axis=1)
    gate = gate / jnp.sum(gate, axis=-1, keepdims=True) * ROUTED_SCALE

    n_assign = n_tok * TOP_K
    flat_e = idx.reshape(n_assign)
    flat_tok = jnp.repeat(jnp.arange(n_tok, dtype=jnp.int32), TOP_K)
    order = jnp.argsort(flat_e)
    sorted_e = flat_e[order]
    counts = jnp.bincount(flat_e, length=N_EXPERTS)
    starts = jnp.cumsum(counts) - counts
    padded = (counts + EXPERT_BLOCK - 1) // EXPERT_BLOCK * EXPERT_BLOCK
    padded_end = jnp.cumsum(padded)
    padded_start = padded_end - padded
    dest_sorted = (padded_start[sorted_e] + jnp.arange(n_assign, dtype=jnp.int32) - starts[sorted_e]).astype(jnp.int32)
    n_rows = n_assign + N_EXPERTS * EXPERT_BLOCK
    n_blocks = n_rows // EXPERT_BLOCK
    row_tok = jnp.zeros((n_rows,), jnp.int32).at[dest_sorted].set(flat_tok[order])
    block_start = jnp.arange(n_blocks, dtype=jnp.int32) * EXPERT_BLOCK
    block_e = jnp.minimum(jnp.searchsorted(padded_end, block_start, side='right'), N_EXPERTS - 1)

    def expert_block(args):
        tok, e = args
        xb = xf[tok]
        hb = jax.nn.silu(xb @ w_gate[e]) * (xb @ w_up[e])
        return hb @ w_down[e]

    rows = lax.map(expert_block, (row_tok.reshape(n_blocks, EXPERT_BLOCK), block_e))
    rows = rows.reshape(n_rows, d)
    dest = jnp.zeros((n_assign,), jnp.int32).at[order].set(dest_sorted)
    picked = rows[dest].reshape(n_tok, TOP_K, d).astype(jnp.float32)
    routed = jnp.einsum('tkd,tk->td', picked, gate)
    shared = (jax.nn.silu(xf @ sh_gate) * (xf @ sh_up)) @ sh_down
    return (routed.astype(x.dtype) + shared).reshape(bsz, s, d)


def setup_inputs(seed: int = 0) -> dict:
    key = jax.random.key(seed)
    keys = iter(jax.random.split(key, 32))

    def nrm(shape, scale):
        return jax.random.normal(next(keys), shape, jnp.float32) * scale

    beta = DEEPNORM_BETA
    return {
        "x": nrm((BATCH, SEQ, D_MODEL), 1.0),
        "pool_w_in": nrm((N_POOL_LAYERS, D_MODEL, POOL_WIDTH), D_MODEL ** -0.5),
        "pool_w_grp": nrm((N_POOL_LAYERS, POOL_GROUPS, POOL_GROUP_DIM, POOL_GROUP_DIM), POOL_GROUP_DIM ** -0.5),
        "pool_scale": 1.0 + nrm((N_POOL_LAYERS, POOL_WIDTH), 0.1),
        "pool_w_out": nrm((N_POOL_LAYERS, POOL_WIDTH, D_MODEL), beta * POOL_WIDTH ** -0.5),
        "sgu_w_in": nrm((N_SGU_LAYERS, D_MODEL, 2 * SGU_WIDTH), D_MODEL ** -0.5),
        "sgu_b_in": nrm((N_SGU_LAYERS, 2 * SGU_WIDTH), 0.02),
        "sgu_ln_g": 1.0 + nrm((N_SGU_LAYERS, SGU_WIDTH), 0.1),
        "sgu_ln_b": nrm((N_SGU_LAYERS, SGU_WIDTH), 0.02),
        "sgu_w_s": nrm((N_SGU_LAYERS, SGU_HEADS, SGU_CHUNK, SGU_CHUNK), SGU_CHUNK ** -0.5),
        "sgu_b_s": 1.0 + nrm((N_SGU_LAYERS, SGU_HEADS, SGU_CHUNK), 0.1),
        "sgu_w_out": nrm((N_SGU_LAYERS, SGU_WIDTH, D_MODEL), beta * SGU_WIDTH ** -0.5),
        "ln_mix_g": 1.0 + nrm((DEPTH, D_MODEL), 0.1),
        "ln_mix_b": nrm((DEPTH, D_MODEL), 0.02),
        "moe_w_router": nrm((DEPTH, D_MODEL, N_EXPERTS), D_MODEL ** -0.5),
        "moe_router_bias": nrm((DEPTH, N_EXPERTS), 0.01),
        "moe_w_gate": nrm((DEPTH, N_EXPERTS, D_MODEL, EXPERT_DIM), D_MODEL ** -0.5),
        "moe_w_up": nrm((DEPTH, N_EXPERTS, D_MODEL, EXPERT_DIM), D_MODEL ** -0.5),
        "moe_w_down": nrm((DEPTH, N_EXPERTS, EXPERT_DIM, D_MODEL), beta * EXPERT_DIM ** -0.5),
        "moe_sh_gate": nrm((DEPTH, D_MODEL, SHARED_DIM), D_MODEL ** -0.5),
        "moe_sh_up": nrm((DEPTH, D_MODEL, SHARED_DIM), D_MODEL ** -0.5),
        "moe_sh_down": nrm((DEPTH, SHARED_DIM, D_MODEL), beta * SHARED_DIM ** -0.5),
        "ln_ffn_g": 1.0 + nrm((DEPTH, D_MODEL), 0.1),
        "ln_ffn_b": nrm((DEPTH, D_MODEL), 0.02),
    }


def reference(x, pool_w_in, pool_w_grp, pool_scale, pool_w_out, sgu_w_in, sgu_b_in, sgu_ln_g, sgu_ln_b, sgu_w_s, sgu_b_s, sgu_w_out, ln_mix_g, ln_mix_b, moe_w_router, moe_router_bias, moe_w_gate, moe_w_up, moe_w_down, moe_sh_gate, moe_sh_up, moe_sh_down, ln_ffn_g, ln_ffn_b):
    for i in range(DEPTH):
        j = i // N_MIXERS
        if i % N_MIXERS == 0:
            mix = pool_mixer(x, pool_w_in[j], pool_w_grp[j], pool_scale[j], pool_w_out[j])
        else:
            mix = sgu_mixer(x, sgu_w_in[j], sgu_b_in[j], sgu_ln_g[j], sgu_ln_b[j], sgu_w_s[j], sgu_b_s[j], sgu_w_out[j])
        x = layer_norm(DEEPNORM_ALPHA * x + mix, ln_mix_g[i], ln_mix_b[i])
        ffn = moe(x, moe_w_router[i], moe_router_bias[i], moe_w_gate[i], moe_w_up[i], moe_w_down[i], moe_sh_gate[i], moe_sh_up[i], moe_sh_down[i])
        x = layer_norm(DEEPNORM_ALPHA * x + ffn, ln_ffn_g[i], ln_ffn_b[i])
    return x
```

```python
import functools

import jax
import jax.numpy as jnp
from jax import lax
from jax.experimental import pallas as pl
from jax.experimental.pallas import tpu as pltpu

D_MODEL = 1024
DEPTH = 2
POOL_WINDOWS = (2, 4, 8, 16)
POOL_GROUP_DIM = D_MODEL // len(POOL_WINDOWS)
POOL_HALO = 16
SGU_CHUNK = 128
SGU_HEADS = 4
SGU_WIDTH = 2 * D_MODEL
SGU_HEAD_DIM = SGU_WIDTH // SGU_HEADS
N_EXPERTS = 64
TOP_K = 8
N_GROUPS = 8
GROUP_SIZE = N_EXPERTS // N_GROUPS
TOPK_GROUPS = 4
EXPERT_DIM = D_MODEL // 4
ROUTED_SCALE = 2.5
DEEPNORM_ALPHA = (2 * DEPTH) ** 0.25
LN_EPS = 1e-5

POOL_TILE = 512
SGU_TILE = 256
ROUTE_TILE = 512
EXPERT_ROWS = 256
COMBINE_TILE = 256
VMEM_LIMIT = 56 * 1024 * 1024

_F32 = jnp.float32
_BF16 = jnp.bfloat16


def _dot(a, b):
    return jnp.dot(a, b, preferred_element_type=_F32)


def _layer_norm(h, g, b):
    mu = jnp.mean(h, axis=-1, keepdims=True)
    hc = h - mu
    var = jnp.mean(hc * hc, axis=-1, keepdims=True)
    return hc * lax.rsqrt(var + LN_EPS) * g + b


def _silu(x):
    return x * jax.nn.sigmoid(x)


def _gelu_tanh(x):
    c = 0.7978845608028654
    return 0.5 * x * (1.0 + jnp.tanh(c * (x + 0.044715 * (x * x * x))))


def _pool_kernel(x_ref, win_ref, wgrp_ref, scale_ref, wout_ref, g_ref, b_ref,
                 o_ref, obf_ref, zs_ref, y_ref):
    s = pl.program_id(1)
    ts = x_ref.shape[1]
    x = x_ref[0]
    z = _dot(x.astype(_BF16), win_ref[...])

    @pl.when(s == 0)
    def _():
        zs_ref[0:POOL_HALO, :] = jnp.zeros((POOL_HALO, D_MODEL), _F32)

    zs_ref[POOL_HALO:POOL_HALO + ts, :] = z
    pos = s * ts + lax.broadcasted_iota(jnp.int32, (ts, 1), 0)
    for g, w in enumerate(POOL_WINDOWS):
        c0 = g * POOL_GROUP_DIM
        c1 = c0 + POOL_GROUP_DIM
        zg = zs_ref[POOL_HALO:POOL_HALO + ts, c0:c1]
        acc = zg
        for k in range(1, w):
            acc = acc + zs_ref[POOL_HALO - k:POOL_HALO - k + ts, c0:c1]
        cnt = jnp.minimum(pos + 1, w).astype(_F32)
        pooled = acc / cnt - zg
        yg = _dot(pooled.astype(_BF16), wgrp_ref[g]) * scale_ref[:, c0:c1]
        y_ref[:, c0:c1] = yg.astype(_BF16)
    zs_ref[0:POOL_HALO, :] = zs_ref[ts:ts + POOL_HALO, :]
    mix = _dot(y_ref[...], wout_ref[...])
    out = _layer_norm(DEEPNORM_ALPHA * x + mix, g_ref[...], b_ref[...])
    o_ref[0] = out
    obf_ref[0] = out.astype(_BF16)


def _pool_layer(x, w_in, w_grp, scale, w_out, ln_g, ln_b):
    bsz, seq, d = x.shape
    ts = POOL_TILE
    const2 = lambda b, s: (0, 0)
    out_shape = (jax.ShapeDtypeStruct((bsz, seq, d), _F32),
                 jax.ShapeDtypeStruct((bsz, seq, d), _BF16))
    tile = pl.BlockSpec((1, ts, d), lambda b, s: (b, s, 0))
    return pl.pallas_call(
        _pool_kernel,
        grid=(bsz, seq // ts),
        in_specs=[
            tile,
            pl.BlockSpec((d, d), const2),
            pl.BlockSpec((len(POOL_WINDOWS), POOL_GROUP_DIM, POOL_GROUP_DIM), lambda b, s: (0, 0, 0)),
            pl.BlockSpec((1, d), const2),
            pl.BlockSpec((d, d), const2),
            pl.BlockSpec((1, d), const2),
            pl.BlockSpec((1, d), const2),
        ],
        out_specs=(tile, tile),
        out_shape=out_shape,
        scratch_shapes=[pltpu.VMEM((POOL_HALO + ts, d), _F32),
                        pltpu.VMEM((ts, d), _BF16)],
        compiler_params=pltpu.CompilerParams(
            dimension_semantics=("arbitrary", "arbitrary"),
            vmem_limit_bytes=VMEM_LIMIT),
        name="pool_mixer",
    )(x, w_in.astype(_BF16), w_grp.astype(_BF16), scale.reshape(1, d),
      w_out.astype(_BF16), ln_g.reshape(1, d), ln_b.reshape(1, d))


def _sgu_kernel(x_ref, win_ref, bin_ref, lng_ref, lnb_ref, ws_ref, bs_ref, wout_ref,
                g_ref, b_ref, o_ref, obf_ref, gated_ref):
    ts = x_ref.shape[0]
    x = x_ref[...]
    xb = x.astype(_BF16)
    v = _gelu_tanh(_dot(xb, win_ref[:, SGU_WIDTH:]) + bin_ref[:, SGU_WIDTH:])
    v = _layer_norm(v, lng_ref[...], lnb_ref[...]).astype(_BF16)
    u = _gelu_tanh(_dot(xb, win_ref[:, :SGU_WIDTH]) + bin_ref[:, :SGU_WIDTH])
    for c in range(ts // SGU_CHUNK):
        r0 = c * SGU_CHUNK
        for h in range(SGU_HEADS):
            c0 = h * SGU_HEAD_DIM
            sv = _dot(ws_ref[h], v[r0:r0 + SGU_CHUNK, c0:c0 + SGU_HEAD_DIM]) + bs_ref[h]
            gated_ref[r0:r0 + SGU_CHUNK, c0:c0 + SGU_HEAD_DIM] = (
                u[r0:r0 + SGU_CHUNK, c0:c0 + SGU_HEAD_DIM] * sv).astype(_BF16)
    mix = _dot(gated_ref[...], wout_ref[...])
    out = _layer_norm(DEEPNORM_ALPHA * x + mix, g_ref[...], b_ref[...])
    o_ref[...] = out
    obf_ref[...] = out.astype(_BF16)


def _sgu_layer(x, w_in, b_in, ln_g, ln_b, w_s, b_s, w_out, mix_g, mix_b):
    n_tok, d = x.shape
    ts = SGU_TILE
    const2 = lambda i: (0, 0)
    const3 = lambda i: (0, 0, 0)
    causal = jnp.tril(jnp.ones((SGU_CHUNK, SGU_CHUNK), w_s.dtype))
    ws = (w_s * causal[None]).astype(_BF16)
    tile = pl.BlockSpec((ts, d), lambda i: (i, 0))
    out_shape = (jax.ShapeDtypeStruct((n_tok, d), _F32),
                 jax.ShapeDtypeStruct((n_tok, d), _BF16))
    return pl.pallas_call(
        _sgu_kernel,
        grid=(n_tok // ts,),
        in_specs=[
            tile,
            pl.BlockSpec((d, 2 * SGU_WIDTH), const2),
            pl.BlockSpec((1, 2 * SGU_WIDTH), const2),
            pl.BlockSpec((1, SGU_WIDTH), const2),
            pl.BlockSpec((1, SGU_WIDTH), const2),
            pl.BlockSpec((SGU_HEADS, SGU_CHUNK, SGU_CHUNK), const3),
            pl.BlockSpec((SGU_HEADS, SGU_CHUNK, 1), const3),
            pl.BlockSpec((SGU_WIDTH, d), const2),
            pl.BlockSpec((1, d), const2),
            pl.BlockSpec((1, d), const2),
        ],
        out_specs=(tile, tile),
        out_shape=out_shape,
        scratch_shapes=[pltpu.VMEM((ts, SGU_WIDTH), _BF16)],
        compiler_params=pltpu.CompilerParams(
            dimension_semantics=("arbitrary",),
            vmem_limit_bytes=VMEM_LIMIT),
        name="sgu_mixer",
    )(x, w_in.astype(_BF16), b_in.reshape(1, -1), ln_g.reshape(1, -1), ln_b.reshape(1, -1),
      ws, b_s.reshape(SGU_HEADS, SGU_CHUNK, 1), w_out.astype(_BF16),
      mix_g.reshape(1, d), mix_b.reshape(1, d))


def _route_kernel(x_ref, wrt_ref, bias_ref, ek_ref, rk_ref, gk_ref, cnt_ref, carry_ref):
    i = pl.program_id(0)
    ts = x_ref.shape[0]

    @pl.when(i == 0)
    def _():
        carry_ref[...] = jnp.zeros_like(carry_ref)

    logits = lax.dot_general(wrt_ref[...], x_ref[...], (((1,), (1,)), ((), ())),
                             precision=lax.Precision.HIGHEST,
                             preferred_element_type=_F32)
    scores = jax.nn.sigmoid(logits).reshape(N_GROUPS, GROUP_SIZE, ts)
    biased = scores + bias_ref[...]
    neg_inf = jnp.float32(-jnp.inf)
    shape3 = (N_GROUPS, GROUP_SIZE, ts)
    in_grp = lax.broadcasted_iota(jnp.int32, shape3, 1)
    grp = lax.broadcasted_iota(jnp.int32, shape3, 0)
    eid = grp * GROUP_SIZE + in_grp

    m1 = jnp.max(biased, axis=1, keepdims=True)
    first1 = jnp.min(jnp.where(biased == m1, in_grp, GROUP_SIZE), axis=1, keepdims=True)
    m2 = jnp.max(jnp.where(in_grp == first1, neg_inf, biased), axis=1, keepdims=True)
    gscore = m1 + m2

    gid = lax.broadcasted_iota(jnp.int32, (N_GROUPS, 1, ts), 0)
    gsel = jnp.zeros((N_GROUPS, 1, ts), jnp.bool_)
    for _ in range(TOPK_GROUPS):
        m = jnp.max(gscore, axis=0, keepdims=True)
        first = jnp.min(jnp.where(gscore == m, gid, N_GROUPS), axis=0, keepdims=True)
        pick = gid == first
        gsel = jnp.logical_or(gsel, pick)
        gscore = jnp.where(pick, neg_inf, gscore)

    masked = jnp.where(gsel, biased, neg_inf)
    picked_any = jnp.zeros(shape3, jnp.bool_)
    e_k, s_k = [], []
    for _ in range(TOP_K):
        m = jnp.max(jnp.max(masked, axis=1, keepdims=True), axis=0, keepdims=True)
        first = jnp.min(jnp.min(jnp.where(masked == m, eid, N_EXPERTS), axis=1, keepdims=True),
                        axis=0, keepdims=True)
        pick = eid == first
        picked_any = jnp.logical_or(picked_any, pick)
        masked = jnp.where(pick, neg_inf, masked)
        sc = jnp.sum(jnp.sum(jnp.where(pick, scores, 0.0), axis=1, keepdims=True),
                     axis=0, keepdims=True)
        e_k.append(first)
        s_k.append(sc)
    denom = s_k[0]
    for k in range(1, TOP_K):
        denom = denom + s_k[k]

    sel = picked_any.astype(_F32).reshape(N_EXPERTS, ts)
    row = lax.broadcasted_iota(jnp.int32, (ts, ts), 0)
    col = lax.broadcasted_iota(jnp.int32, (ts, ts), 1)
    earlier = (row < col).astype(_BF16)
    rank = _dot(sel.astype(_BF16), earlier) + carry_ref[...]
    rank3 = rank.reshape(shape3)
    carry_ref[...] += jnp.sum(sel, axis=1, keepdims=True)
    cnt_ref[...] = carry_ref[...]

    for k in range(TOP_K):
        r = jnp.sum(jnp.sum(jnp.where(eid == e_k[k], rank3, 0.0), axis=1, keepdims=True),
                    axis=0, keepdims=True)
        ek_ref[k:k + 1, :] = e_k[k].reshape(1, ts)
        rk_ref[k:k + 1, :] = r.reshape(1, ts).astype(jnp.int32)
        gk_ref[k:k + 1, :] = (s_k[k] / denom * ROUTED_SCALE).reshape(1, ts)


def _route(x, w_router, router_bias):
    n_tok, d = x.shape
    ts = ROUTE_TILE
    out_shape = (jax.ShapeDtypeStruct((TOP_K, n_tok), jnp.int32),
                 jax.ShapeDtypeStruct((TOP_K, n_tok), jnp.int32),
                 jax.ShapeDtypeStruct((TOP_K, n_tok), _F32),
                 jax.ShapeDtypeStruct((N_EXPERTS, 1), _F32))
    kspec = pl.BlockSpec((TOP_K, ts), lambda i: (0, i))
    return pl.pallas_call(
        _route_kernel,
        grid=(n_tok // ts,),
        in_specs=[
            pl.BlockSpec((ts, d), lambda i: (i, 0)),
            pl.BlockSpec((N_EXPERTS, d), lambda i: (0, 0)),
            pl.BlockSpec((N_GROUPS, GROUP_SIZE, 1), lambda i: (0, 0, 0)),
        ],
        out_specs=(kspec, kspec, kspec, pl.BlockSpec((N_EXPERTS, 1), lambda i: (0, 0))),
        out_shape=out_shape,
        scratch_shapes=[pltpu.VMEM((N_EXPERTS, 1), _F32)],
        compiler_params=pltpu.CompilerParams(
            dimension_semantics=("arbitrary",),
            vmem_limit_bytes=VMEM_LIMIT),
        name="moe_route",
    )(x, w_router.T, router_bias.reshape(N_GROUPS, GROUP_SIZE, 1))


def _expert_kernel(be_ref, nb_ref, xs_ref, gate_ref, wg_ref, wu_ref, wd_ref, o_ref):
    i = pl.program_id(0)

    @pl.when(i < nb_ref[0])
    def _():
        xs = xs_ref[...]
        h = _silu(_dot(xs, wg_ref[0].astype(_BF16))) * _dot(xs, wu_ref[0].astype(_BF16))
        y = _dot(h.astype(_BF16), wd_ref[0].astype(_BF16))
        o_ref[...] = (y * gate_ref[...]).astype(_BF16)


def _experts(block_e, n_blocks_used, xs, row_gate, w_gate, w_up, w_down):
    n_rows, d = xs.shape
    bm = EXPERT_ROWS
    wspec_in = pl.BlockSpec((1, d, EXPERT_DIM), lambda i, be, nb: (be[i], 0, 0))
    grid_spec = pltpu.PrefetchScalarGridSpec(
        num_scalar_prefetch=2,
        grid=(n_rows // bm,),
        in_specs=[
            pl.BlockSpec((bm, d), lambda i, be, nb: (i, 0)),
            pl.BlockSpec((bm, 1), lambda i, be, nb: (i, 0)),
            wspec_in,
            wspec_in,
            pl.BlockSpec((1, EXPERT_DIM, d), lambda i, be, nb: (be[i], 0, 0)),
        ],
        out_specs=pl.BlockSpec((bm, d), lambda i, be, nb: (i, 0)),
    )
    return pl.pallas_call(
        _expert_kernel,
        grid_spec=grid_spec,
        out_shape=jax.ShapeDtypeStruct((n_rows, d), _BF16),
        compiler_params=pltpu.CompilerParams(
            dimension_semantics=("arbitrary",),
            vmem_limit_bytes=VMEM_LIMIT),
        name="moe_experts",
    )(block_e, n_blocks_used, xs, row_gate, w_gate, w_up, w_down)


def _combine_kernel(x_ref, p_ref, sg_ref, su_ref, sd_ref, g_ref, b_ref, o_ref):
    x = x_ref[...]
    xb = x.astype(_BF16)
    routed = p_ref[0].astype(_F32)
    for k in range(1, TOP_K):
        routed = routed + p_ref[k].astype(_F32)
    h = _silu(_dot(xb, sg_ref[...])) * _dot(xb, su_ref[...])
    shared = _dot(h.astype(_BF16), sd_ref[...])
    o_ref[...] = _layer_norm(DEEPNORM_ALPHA * x + (routed + shared), g_ref[...], b_ref[...])


def _combine(x, picked, sh_gate, sh_up, sh_down, ln_g, ln_b):
    n_tok, d = x.shape
    ts = COMBINE_TILE
    const2 = lambda i: (0, 0)
    return pl.pallas_call(
        _combine_kernel,
        grid=(n_tok // ts,),
        in_specs=[
            pl.BlockSpec((ts, d), lambda i: (i, 0)),
            pl.BlockSpec((TOP_K, ts, d), lambda i: (0, i, 0)),
            pl.BlockSpec((d, EXPERT_DIM), const2),
            pl.BlockSpec((d, EXPERT_DIM), const2),
            pl.BlockSpec((EXPERT_DIM, d), const2),
            pl.BlockSpec((1, d), const2),
            pl.BlockSpec((1, d), const2),
        ],
        out_specs=pl.BlockSpec((ts, d), lambda i: (i, 0)),
        out_shape=jax.ShapeDtypeStruct((n_tok, d), _F32),
        compiler_params=pltpu.CompilerParams(
            dimension_semantics=("arbitrary",),
            vmem_limit_bytes=VMEM_LIMIT),
        name="moe_combine",
    )(x, picked, sh_gate.astype(_BF16), sh_up.astype(_BF16), sh_down.astype(_BF16),
      ln_g.reshape(1, d), ln_b.reshape(1, d))


def _moe_layer(x, x_bf, w_router, router_bias, w_gate, w_up, w_down,
               sh_gate, sh_up, sh_down, ln_g, ln_b):
    n_tok, d = x.shape
    bm = EXPERT_ROWS
    e_k, r_k, g_k, counts = _route(x, w_router, router_bias)
    counts = counts.reshape(N_EXPERTS).astype(jnp.int32)
    padded = (counts + bm - 1) // bm * bm
    padded_end = jnp.cumsum(padded)
    padded_start = padded_end - padded
    dest = padded_start[e_k] + r_k
    n_rows = n_tok * TOP_K + N_EXPERTS * bm
    n_blocks = n_rows // bm
    tok = jnp.broadcast_to(jnp.arange(n_tok, dtype=jnp.int32)[None, :], (TOP_K, n_tok))
    flat_dest = dest.reshape(-1)
    row_tok = jnp.zeros((n_rows,), jnp.int32).at[flat_dest].set(tok.reshape(-1))
    row_gate = jnp.zeros((n_rows,), _F32).at[flat_dest].set(g_k.reshape(-1))
    block_start = jnp.arange(n_blocks, dtype=jnp.int32) * bm
    block_e = jnp.minimum(jnp.searchsorted(padded_end, block_start, side='right'),
                          N_EXPERTS - 1).astype(jnp.int32)
    n_used = (padded_end[-1] // bm).astype(jnp.int32).reshape(1)
    xs = x_bf[row_tok]
    rows = _experts(block_e, n_used, xs, row_gate.reshape(n_rows, 1), w_gate, w_up, w_down)
    picked = rows[dest]
    return _combine(x, picked, sh_gate, sh_up, sh_down, ln_g, ln_b)


def kernel(x, pool_w_in, pool_w_grp, pool_scale, pool_w_out, sgu_w_in, sgu_b_in, sgu_ln_g, sgu_ln_b, sgu_w_s, sgu_b_s, sgu_w_out, ln_mix_g, ln_mix_b, moe_w_router, moe_router_bias, moe_w_gate, moe_w_up, moe_w_down, moe_sh_gate, moe_sh_up, moe_sh_down, ln_ffn_g, ln_ffn_b):
    bsz, seq, d = x.shape
    n_tok = bsz * seq

    def moe(i, h, h_bf):
        return _moe_layer(h, h_bf, moe_w_router[i], moe_router_bias[i], moe_w_gate[i],
                          moe_w_up[i], moe_w_down[i], moe_sh_gate[i], moe_sh_up[i],
                          moe_sh_down[i], ln_ffn_g[i], ln_ffn_b[i])

    h, h_bf = _pool_layer(x, pool_w_in[0], pool_w_grp[0], pool_scale[0], pool_w_out[0],
                          ln_mix_g[0], ln_mix_b[0])
    h = moe(0, h.reshape(n_tok, d), h_bf.reshape(n_tok, d))
    h, h_bf = _sgu_layer(h, sgu_w_in[0], sgu_b_in[0], sgu_ln_g[0], sgu_ln_b[0], sgu_w_s[0],
                         sgu_b_s[0], sgu_w_out[0], ln_mix_g[1], ln_mix_b[1])
    h = moe(1, h, h_bf)
    return h.reshape(bsz, seq, d)
```

```python
import jax
import jax.numpy as jnp
from jax import lax
from jax.experimental import pallas as pl
from jax.experimental.pallas import tpu as pltpu
from jax.experimental.pallas import tpu_sc as plsc

D_MODEL = 1024
DEPTH = 2
POOL_WINDOWS = (2, 4, 8, 16)
POOL_GROUP_DIM = D_MODEL // len(POOL_WINDOWS)
POOL_HALO = 16
SGU_CHUNK = 128
SGU_HEADS = 4
SGU_WIDTH = 2 * D_MODEL
SGU_HEAD_DIM = SGU_WIDTH // SGU_HEADS
N_EXPERTS = 64
TOP_K = 8
N_GROUPS = 8
GROUP_SIZE = N_EXPERTS // N_GROUPS
TOPK_GROUPS = 4
EXPERT_DIM = D_MODEL // 4
ROUTED_SCALE = 2.5
DEEPNORM_ALPHA = (2 * DEPTH) ** 0.25
LN_EPS = 1e-5

SUBLANES = 8
LANES = 128
PACKED = D_MODEL // 2
ROW_TILES = PACKED // LANES

POOL_TILE = 512
SGU_TILE = 256
ROUTE_TILE = 512
EXPERT_ROWS = 256
COMBINE_TILE = 256
SC_WINDOW = 128
VMEM_LIMIT = 56 * 1024 * 1024

_F32 = jnp.float32
_BF16 = jnp.bfloat16
_U32 = jnp.uint32


def _dot(a, b):
    return jnp.dot(a, b, preferred_element_type=_F32)


def _layer_norm(h, g, b):
    mu = jnp.mean(h, axis=-1, keepdims=True)
    hc = h - mu
    var = jnp.mean(hc * hc, axis=-1, keepdims=True)
    return hc * lax.rsqrt(var + LN_EPS) * g + b


def _silu(x):
    return x * jax.nn.sigmoid(x)


def _gelu_tanh(x):
    c = 0.7978845608028654
    return 0.5 * x * (1.0 + jnp.tanh(c * (x + 0.044715 * (x * x * x))))


def _pack_halves(v):
    half = v.shape[1] // 2
    lo = lax.bitcast_convert_type(v[:, :half].astype(_BF16).astype(_F32), _U32)
    hi = lax.bitcast_convert_type(v[:, half:].astype(_BF16).astype(_F32), _U32)
    return (hi & _U32(0xFFFF0000)) | (lo >> 16)


def _unpack_halves(w):
    lo = lax.bitcast_convert_type(w << 16, _F32)
    hi = lax.bitcast_convert_type(w & _U32(0xFFFF0000), _F32)
    return lo, hi


def _store_tiles(ref, w):
    n = w.shape[0]
    w3 = w.reshape(n // SUBLANES, SUBLANES, PACKED)
    for c in range(ROW_TILES):
        ref[:, c] = w3[:, :, c * LANES:(c + 1) * LANES]


def _load_tiles(ref):
    n = ref.shape[0] * SUBLANES
    return jnp.concatenate([ref[:, c].reshape(n, LANES) for c in range(ROW_TILES)], axis=1)


def _pool_kernel(x_ref, win_ref, wgrp_ref, scale_ref, wout_ref, g_ref, b_ref,
                 o_ref, op_ref, zs_ref, y_ref):
    s = pl.program_id(1)
    ts = x_ref.shape[1]
    x = x_ref[0]
    z = _dot(x.astype(_BF16), win_ref[...])

    @pl.when(s == 0)
    def _():
        zs_ref[0:POOL_HALO, :] = jnp.zeros((POOL_HALO, D_MODEL), _F32)

    zs_ref[POOL_HALO:POOL_HALO + ts, :] = z
    pos = s * ts + lax.broadcasted_iota(jnp.int32, (ts, 1), 0)
    for g, w in enumerate(POOL_WINDOWS):
        c0 = g * POOL_GROUP_DIM
        c1 = c0 + POOL_GROUP_DIM
        zg = zs_ref[POOL_HALO:POOL_HALO + ts, c0:c1]
        acc = zg
        for k in range(1, w):
            acc = acc + zs_ref[POOL_HALO - k:POOL_HALO - k + ts, c0:c1]
        cnt = jnp.minimum(pos + 1, w).astype(_F32)
        pooled = acc / cnt - zg
        yg = _dot(pooled.astype(_BF16), wgrp_ref[g]) * scale_ref[:, c0:c1]
        y_ref[:, c0:c1] = yg.astype(_BF16)
    zs_ref[0:POOL_HALO, :] = zs_ref[ts:ts + POOL_HALO, :]
    mix = _dot(y_ref[...], wout_ref[...])
    out = _layer_norm(DEEPNORM_ALPHA * x + mix, g_ref[...], b_ref[...])
    o_ref[0] = out
    _store_tiles(op_ref, _pack_halves(out))


def _pool_layer(x, w_in, w_grp, scale, w_out, ln_g, ln_b):
    bsz, seq, d = x.shape
    ts = POOL_TILE
    steps = seq // ts
    const2 = lambda b, s: (0, 0)
    out_shape = (jax.ShapeDtypeStruct((bsz, seq, d), _F32),
                 jax.ShapeDtypeStruct((bsz * seq // SUBLANES, ROW_TILES, SUBLANES, LANES), _U32))
    tile = pl.BlockSpec((1, ts, d), lambda b, s: (b, s, 0))
    ptile = pl.BlockSpec((ts // SUBLANES, ROW_TILES, SUBLANES, LANES),
                         lambda b, s: (b * steps + s, 0, 0, 0))
    return pl.pallas_call(
        _pool_kernel,
        grid=(bsz, steps),
        in_specs=[
            tile,
            pl.BlockSpec((d, d), const2),
            pl.BlockSpec((len(POOL_WINDOWS), POOL_GROUP_DIM, POOL_GROUP_DIM), lambda b, s: (0, 0, 0)),
            pl.BlockSpec((1, d), const2),
            pl.BlockSpec((d, d), const2),
            pl.BlockSpec((1, d), const2),
            pl.BlockSpec((1, d), const2),
        ],
        out_specs=(tile, ptile),
        out_shape=out_shape,
        scratch_shapes=[pltpu.VMEM((POOL_HALO + ts, d), _F32),
                        pltpu.VMEM((ts, d), _BF16)],
        compiler_params=pltpu.CompilerParams(
            dimension_semantics=("arbitrary", "arbitrary"),
            vmem_limit_bytes=VMEM_LIMIT),
        name="pool_mixer",
    )(x, w_in.astype(_BF16), w_grp.astype(_BF16), scale.reshape(1, d),
      w_out.astype(_BF16), ln_g.reshape(1, d), ln_b.reshape(1, d))


def _sgu_kernel(x_ref, win_ref, bin_ref, lng_ref, lnb_ref, ws_ref, bs_ref, wout_ref,
                g_ref, b_ref, o_ref, op_ref, gated_ref):
    ts = x_ref.shape[0]
    x = x_ref[...]
    xb = x.astype(_BF16)
    v = _gelu_tanh(_dot(xb, win_ref[:, SGU_WIDTH:]) + bin_ref[:, SGU_WIDTH:])
    v = _layer_norm(v, lng_ref[...], lnb_ref[...]).astype(_BF16)
    u = _gelu_tanh(_dot(xb, win_ref[:, :SGU_WIDTH]) + bin_ref[:, :SGU_WIDTH])
    for c in range(ts // SGU_CHUNK):
        r0 = c * SGU_CHUNK
        for h in range(SGU_HEADS):
            c0 = h * SGU_HEAD_DIM
            sv = _dot(ws_ref[h], v[r0:r0 + SGU_CHUNK, c0:c0 + SGU_HEAD_DIM]) + bs_ref[h]
            gated_ref[r0:r0 + SGU_CHUNK, c0:c0 + SGU_HEAD_DIM] = (
                u[r0:r0 + SGU_CHUNK, c0:c0 + SGU_HEAD_DIM] * sv).astype(_BF16)
    mix = _dot(gated_ref[...], wout_ref[...])
    out = _layer_norm(DEEPNORM_ALPHA * x + mix, g_ref[...], b_ref[...])
    o_ref[...] = out
    _store_tiles(op_ref, _pack_halves(out))


def _sgu_layer(x, w_in, b_in, ln_g, ln_b, w_s, b_s, w_out, mix_g, mix_b):
    n_tok, d = x.shape
    ts = SGU_TILE
    const2 = lambda i: (0, 0)
    const3 = lambda i: (0, 0, 0)
    causal = jnp.tril(jnp.ones((SGU_CHUNK, SGU_CHUNK), w_s.dtype))
    ws = (w_s * causal[None]).astype(_BF16)
    tile = pl.BlockSpec((ts, d), lambda i: (i, 0))
    ptile = pl.BlockSpec((ts // SUBLANES, ROW_TILES, SUBLANES, LANES), lambda i: (i, 0, 0, 0))
    out_shape = (jax.ShapeDtypeStruct((n_tok, d), _F32),
                 jax.ShapeDtypeStruct((n_tok // SUBLANES, ROW_TILES, SUBLANES, LANES), _U32))
    return pl.pallas_call(
        _sgu_kernel,
        grid=(n_tok // ts,),
        in_specs=[
            tile,
            pl.BlockSpec((d, 2 * SGU_WIDTH), const2),
            pl.BlockSpec((1, 2 * SGU_WIDTH), const2),
            pl.BlockSpec((1, SGU_WIDTH), const2),
            pl.BlockSpec((1, SGU_WIDTH), const2),
            pl.BlockSpec((SGU_HEADS, SGU_CHUNK, SGU_CHUNK), const3),
            pl.BlockSpec((SGU_HEADS, SGU_CHUNK, 1), const3),
            pl.BlockSpec((SGU_WIDTH, d), const2),
            pl.BlockSpec((1, d), const2),
            pl.BlockSpec((1, d), const2),
        ],
        out_specs=(tile, ptile),
        out_shape=out_shape,
        scratch_shapes=[pltpu.VMEM((ts, SGU_WIDTH), _BF16)],
        compiler_params=pltpu.CompilerParams(
            dimension_semantics=("arbitrary",),
            vmem_limit_bytes=VMEM_LIMIT),
        name="sgu_mixer",
    )(x, w_in.astype(_BF16), b_in.reshape(1, -1), ln_g.reshape(1, -1), ln_b.reshape(1, -1),
      ws, b_s.reshape(SGU_HEADS, SGU_CHUNK, 1), w_out.astype(_BF16),
      mix_g.reshape(1, d), mix_b.reshape(1, d))


def _route_kernel(x_ref, wrt_ref, bias_ref, ek_ref, rk_ref, gk_ref, cnt_ref, carry_ref):
    i = pl.program_id(0)
    ts = x_ref.shape[0]

    @pl.when(i == 0)
    def _():
        carry_ref[...] = jnp.zeros_like(carry_ref)

    logits = lax.dot_general(wrt_ref[...], x_ref[...], (((1,), (1,)), ((), ())),
                             precision=lax.Precision.HIGHEST,
                             preferred_element_type=_F32)
    scores = jax.nn.sigmoid(logits).reshape(N_GROUPS, GROUP_SIZE, ts)
    biased = scores + bias_ref[...]
    neg_inf = jnp.float32(-jnp.inf)
    shape3 = (N_GROUPS, GROUP_SIZE, ts)
    in_grp = lax.broadcasted_iota(jnp.int32, shape3, 1)
    grp = lax.broadcasted_iota(jnp.int32, shape3, 0)
    eid = grp * GROUP_SIZE + in_grp

    m1 = jnp.max(biased, axis=1, keepdims=True)
    first1 = jnp.min(jnp.where(biased == m1, in_grp, GROUP_SIZE), axis=1, keepdims=True)
    m2 = jnp.max(jnp.where(in_grp == first1, neg_inf, biased), axis=1, keepdims=True)
    gscore = m1 + m2

    gid = lax.broadcasted_iota(jnp.int32, (N_GROUPS, 1, ts), 0)
    gsel = jnp.zeros((N_GROUPS, 1, ts), jnp.bool_)
    for _ in range(TOPK_GROUPS):
        m = jnp.max(gscore, axis=0, keepdims=True)
        first = jnp.min(jnp.where(gscore == m, gid, N_GROUPS), axis=0, keepdims=True)
        pick = gid == first
        gsel = jnp.logical_or(gsel, pick)
        gscore = jnp.where(pick, neg_inf, gscore)

    masked = jnp.where(gsel, biased, neg_inf)
    picked_any = jnp.zeros(shape3, jnp.bool_)
    e_k, s_k = [], []
    for _ in range(TOP_K):
        m = jnp.max(jnp.max(masked, axis=1, keepdims=True), axis=0, keepdims=True)
        first = jnp.min(jnp.min(jnp.where(masked == m, eid, N_EXPERTS), axis=1, keepdims=True),
                        axis=0, keepdims=True)
        pick = eid == first
        picked_any = jnp.logical_or(picked_any, pick)
        masked = jnp.where(pick, neg_inf, masked)
        sc = jnp.sum(jnp.sum(jnp.where(pick, scores, 0.0), axis=1, keepdims=True),
                     axis=0, keepdims=True)
        e_k.append(first)
        s_k.append(sc)
    denom = s_k[0]
    for k in range(1, TOP_K):
        denom = denom + s_k[k]

    sel = picked_any.astype(_F32).reshape(N_EXPERTS, ts)
    row = lax.broadcasted_iota(jnp.int32, (ts, ts), 0)
    col = lax.broadcasted_iota(jnp.int32, (ts, ts), 1)
    earlier = (row < col).astype(_BF16)
    rank = _dot(sel.astype(_BF16), earlier) + carry_ref[...]
    rank3 = rank.reshape(shape3)
    carry_ref[...] += jnp.sum(sel, axis=1, keepdims=True)
    cnt_ref[...] = carry_ref[...]

    for k in range(TOP_K):
        r = jnp.sum(jnp.sum(jnp.where(eid == e_k[k], rank3, 0.0), axis=1, keepdims=True),
                    axis=0, keepdims=True)
        ek_ref[k:k + 1, :] = e_k[k].reshape(1, ts)
        rk_ref[k:k + 1, :] = r.reshape(1, ts).astype(jnp.int32)
        gk_ref[k:k + 1, :] = (s_k[k] / denom * ROUTED_SCALE).reshape(1, ts)


def _route(x, w_router, router_bias):
    n_tok, d = x.shape
    ts = ROUTE_TILE
    out_shape = (jax.ShapeDtypeStruct((TOP_K, n_tok), jnp.int32),
                 jax.ShapeDtypeStruct((TOP_K, n_tok), jnp.int32),
                 jax.ShapeDtypeStruct((TOP_K, n_tok), _F32),
                 jax.ShapeDtypeStruct((N_EXPERTS, 1), _F32))
    kspec = pl.BlockSpec((TOP_K, ts), lambda i: (0, i))
    return pl.pallas_call(
        _route_kernel,
        grid=(n_tok // ts,),
        in_specs=[
            pl.BlockSpec((ts, d), lambda i: (i, 0)),
            pl.BlockSpec((N_EXPERTS, d), lambda i: (0, 0)),
            pl.BlockSpec((N_GROUPS, GROUP_SIZE, 1), lambda i: (0, 0, 0)),
        ],
        out_specs=(kspec, kspec, kspec, pl.BlockSpec((N_EXPERTS, 1), lambda i: (0, 0))),
        out_shape=out_shape,
        scratch_shapes=[pltpu.VMEM((N_EXPERTS, 1), _F32)],
        compiler_params=pltpu.CompilerParams(
            dimension_semantics=("arbitrary",),
            vmem_limit_bytes=VMEM_LIMIT),
        name="moe_route",
    )(x, w_router.T, router_bias.reshape(N_GROUPS, GROUP_SIZE, 1))


def _sc_mesh():
    return plsc.VectorSubcoreMesh(core_axis_name="core", subcore_axis_name="subcore")


def _sc_scatter_rows(src, idx, n_out):
    m, d = src.shape
    nk = idx.shape[0]

    def body(src_hbm, idx_hbm, out_hbm):
        def step(src_vmem, idx_vmem):
            for k in range(nk):
                pltpu.sync_copy(src_vmem, out_hbm.at[idx_vmem.at[k]])

        pltpu.emit_pipeline(
            step,
            grid=(m // SC_WINDOW,),
            in_specs=[pl.BlockSpec((SC_WINDOW, d), index_map=lambda i: (i, 0)),
                      pl.BlockSpec((nk, SC_WINDOW), index_map=lambda i: (0, i))],
            out_specs=[],
            core_axis_name=("core", "subcore"),
            dimension_semantics=(pltpu.PARALLEL,),
        )(src_hbm, idx_hbm)

    return pl.kernel(body, out_type=jax.ShapeDtypeStruct((n_out, d), src.dtype),
                     mesh=_sc_mesh(), scratch_types=[], name="sc_scatter_rows")(src, idx)


def _sc_gather_rows(table, idx):
    d = table.shape[1]
    m = idx.shape[1]

    def body(table_hbm, idx_hbm, out_hbm):
        def step(idx_vmem, out_vmem):
            pltpu.sync_copy(table_hbm.at[idx_vmem.at[0]], out_vmem)

        pltpu.emit_pipeline(
            step,
            grid=(m // SC_WINDOW,),
            in_specs=[pl.BlockSpec((1, SC_WINDOW), index_map=lambda i: (0, i))],
            out_specs=[pl.BlockSpec((SC_WINDOW, d), index_map=lambda i: (i, 0))],
            core_axis_name=("core", "subcore"),
            dimension_semantics=(pltpu.PARALLEL,),
        )(idx_hbm, out_hbm)

    return pl.kernel(body, out_type=jax.ShapeDtypeStruct((m, d), table.dtype),
                     mesh=_sc_mesh(), scratch_types=[], name="sc_gather_rows")(table, idx)


def _expert_kernel(be_ref, nb_ref, xs_ref, wg_ref, wu_ref, wd_ref, o_ref):
    i = pl.program_id(0)

    @pl.when(i < nb_ref[0])
    def _():
        lo, hi = _unpack_halves(_load_tiles(xs_ref))
        lo = lo.astype(_BF16)
        hi = hi.astype(_BF16)
        wg = wg_ref[0].astype(_BF16)
        wu = wu_ref[0].astype(_BF16)
        gate = _dot(lo, wg[:PACKED]) + _dot(hi, wg[PACKED:])
        up = _dot(lo, wu[:PACKED]) + _dot(hi, wu[PACKED:])
        h = _silu(gate) * up
        y = _dot(h.astype(_BF16), wd_ref[0].astype(_BF16))
        _store_tiles(o_ref, _pack_halves(y))


def _experts(block_e, n_blocks_used, xs, w_gate, w_up, w_down):
    n_rows = xs.shape[0] * SUBLANES
    d = D_MODEL
    bm = EXPERT_ROWS
    wspec_in = pl.BlockSpec((1, d, EXPERT_DIM), lambda i, be, nb: (be[i], 0, 0))
    rows_spec = pl.BlockSpec((bm // SUBLANES, ROW_TILES, SUBLANES, LANES),
                             lambda i, be, nb: (i, 0, 0, 0))
    grid_spec = pltpu.PrefetchScalarGridSpec(
        num_scalar_prefetch=2,
        grid=(n_rows // bm,),
        in_specs=[
            rows_spec,
            wspec_in,
            wspec_in,
            pl.BlockSpec((1, EXPERT_DIM, d), lambda i, be, nb: (be[i], 0, 0)),
        ],
        out_specs=rows_spec,
    )
    return pl.pallas_call(
        _expert_kernel,
        grid_spec=grid_spec,
        out_shape=jax.ShapeDtypeStruct(xs.shape, _U32),
        compiler_params=pltpu.CompilerParams(
            dimension_semantics=("arbitrary",),
            vmem_limit_bytes=VMEM_LIMIT),
        name="moe_experts",
    )(block_e, n_blocks_used, xs, w_gate, w_up, w_down)


def _combine_kernel(x_ref, p_ref, gate_ref, sg_ref, su_ref, sd_ref, g_ref, b_ref, o_ref):
    x = x_ref[...]
    xb = x.astype(_BF16)
    gates = gate_ref[...]
    r_lo = r_hi = None
    for k in range(TOP_K):
        lo, hi = _unpack_halves(_load_tiles(p_ref.at[k]))
        gk = gates[:, k:k + 1]
        r_lo = gk * lo if r_lo is None else r_lo + gk * lo
        r_hi = gk * hi if r_hi is None else r_hi + gk * hi
    routed = jnp.concatenate([r_lo, r_hi], axis=1)
    h = _silu(_dot(xb, sg_ref[...])) * _dot(xb, su_ref[...])
    shared = _dot(h.astype(_BF16), sd_ref[...])
    o_ref[...] = _layer_norm(DEEPNORM_ALPHA * x + (routed + shared), g_ref[...], b_ref[...])


def _combine(x, picked, gates, sh_gate, sh_up, sh_down, ln_g, ln_b):
    n_tok, d = x.shape
    ts = COMBINE_TILE
    const2 = lambda i: (0, 0)
    return pl.pallas_call(
        _combine_kernel,
        grid=(n_tok // ts,),
        in_specs=[
            pl.BlockSpec((ts, d), lambda i: (i, 0)),
            pl.BlockSpec((TOP_K, ts // SUBLANES, ROW_TILES, SUBLANES, LANES),
                         lambda i: (0, i, 0, 0, 0)),
            pl.BlockSpec((ts, TOP_K), lambda i: (i, 0)),
            pl.BlockSpec((d, EXPERT_DIM), const2),
            pl.BlockSpec((d, EXPERT_DIM), const2),
            pl.BlockSpec((EXPERT_DIM, d), const2),
            pl.BlockSpec((1, d), const2),
            pl.BlockSpec((1, d), const2),
        ],
        out_specs=pl.BlockSpec((ts, d), lambda i: (i, 0)),
        out_shape=jax.ShapeDtypeStruct((n_tok, d), _F32),
        compiler_params=pltpu.CompilerParams(
            dimension_semantics=("arbitrary",),
            vmem_limit_bytes=VMEM_LIMIT),
        name="moe_combine",
    )(x, picked, gates, sh_gate.astype(_BF16), sh_up.astype(_BF16), sh_down.astype(_BF16),
      ln_g.reshape(1, d), ln_b.reshape(1, d))


def _flat_row_index(rows):
    lead = rows.shape[:-1]
    n = rows.shape[-1]
    r = rows.reshape(lead + (n // SUBLANES, 1, SUBLANES))
    piece = jnp.arange(ROW_TILES, dtype=jnp.int32).reshape(ROW_TILES, 1)
    flat = (r // SUBLANES) * (ROW_TILES * SUBLANES) + piece * SUBLANES + r % SUBLANES
    return flat.reshape(lead + (n * ROW_TILES,))


def _moe_layer(x, x_packed, w_router, router_bias, w_gate, w_up, w_down,
               sh_gate, sh_up, sh_down, ln_g, ln_b):
    n_tok, d = x.shape
    bm = EXPERT_ROWS
    e_k, r_k, g_k, counts = _route(x, w_router, router_bias)
    counts = counts.reshape(N_EXPERTS).astype(jnp.int32)
    padded = (counts + bm - 1) // bm * bm
    padded_end = jnp.cumsum(padded)
    padded_start = padded_end - padded
    experts = jnp.arange(N_EXPERTS, dtype=jnp.int32)
    start_k = jnp.sum(jnp.where(e_k[..., None] == experts, padded_start, 0), axis=-1)
    dest = start_k + r_k
    n_rows = n_tok * TOP_K + N_EXPERTS * bm
    n_blocks = n_rows // bm
    block_start = jnp.arange(n_blocks, dtype=jnp.int32) * bm
    block_e = jnp.minimum(jnp.sum(padded_end[None, :] <= block_start[:, None], axis=1),
                          N_EXPERTS - 1).astype(jnp.int32)
    n_used = (padded_end[-1] // bm).astype(jnp.int32).reshape(1)

    flat_dest = _flat_row_index(dest)
    xs = _sc_scatter_rows(x_packed.reshape(n_tok * ROW_TILES, LANES), flat_dest,
                          n_rows * ROW_TILES)
    rows = _experts(block_e, n_used,
                    xs.reshape(n_rows // SUBLANES, ROW_TILES, SUBLANES, LANES),
                    w_gate, w_up, w_down)
    picked = _sc_gather_rows(rows.reshape(n_rows * ROW_TILES, LANES),
                             flat_dest.reshape(1, TOP_K * n_tok * ROW_TILES))
    picked = picked.reshape(TOP_K, n_tok // SUBLANES, ROW_TILES, SUBLANES, LANES)
    return _combine(x, picked, g_k.T, sh_gate, sh_up, sh_down, ln_g, ln_b)


def kernel(x, pool_w_in, pool_w_grp, pool_scale, pool_w_out, sgu_w_in, sgu_b_in, sgu_ln_g, sgu_ln_b, sgu_w_s, sgu_b_s, sgu_w_out, ln_mix_g, ln_mix_b, moe_w_router, moe_router_bias, moe_w_gate, moe_w_up, moe_w_down, moe_sh_gate, moe_sh_up, moe_sh_down, ln_ffn_g, ln_ffn_b):
    bsz, seq, d = x.shape
    n_tok = bsz * seq

    def moe(i, h, h_packed):
        return _moe_layer(h, h_packed, moe_w_router[i], moe_router_bias[i], moe_w_gate[i],
                          moe_w_up[i], moe_w_down[i], moe_sh_gate[i], moe_sh_up[i],
                          moe_sh_down[i], ln_ffn_g[i], ln_ffn_b[i])

    h, h_packed = _pool_layer(x, pool_w_in[0], pool_w_grp[0], pool_scale[0], pool_w_out[0],
                              ln_mix_g[0], ln_mix_b[0])
    h = moe(0, h.reshape(n_tok, d), h_packed)
    h, h_packed = _sgu_layer(h, sgu_w_in[0], sgu_b_in[0], sgu_ln_g[0], sgu_ln_b[0], sgu_w_s[0],
                             sgu_b_s[0], sgu_w_out[0], ln_mix_g[1], ln_mix_b[1])
    h = moe(1, h, h_packed)
    return h.reshape(bsz, seq, d)
```

```python
import functools

import jax
import jax.numpy as jnp
from jax import lax
from jax.experimental import pallas as pl
from jax.experimental.pallas import tpu as pltpu
from jax.experimental.pallas import tpu_sc as plsc

D_MODEL = 1024
DEPTH = 2
POOL_WINDOWS = (2, 4, 8, 16)
POOL_GROUP_DIM = D_MODEL // len(POOL_WINDOWS)
POOL_HALO = 16
SGU_CHUNK = 128
SGU_HEADS = 4
SGU_WIDTH = 2 * D_MODEL
SGU_HEAD_DIM = SGU_WIDTH // SGU_HEADS
N_EXPERTS = 64
TOP_K = 8
N_GROUPS = 8
GROUP_SIZE = N_EXPERTS // N_GROUPS
TOPK_GROUPS = 4
EXPERT_DIM = D_MODEL // 4
ROUTED_SCALE = 2.5
DEEPNORM_ALPHA = (2 * DEPTH) ** 0.25
LN_EPS = 1e-5

LANES = 128
PACKED = D_MODEL // 2
PIECES = PACKED // LANES

POOL_TILE = 512
SGU_TILE = 256
ROUTE_TILE = 512
INDEX_TILE = 2048
EXPERT_ROWS = 512
COMBINE_TILE = 256
SC_WINDOW = 128
VMEM_LIMIT = 56 * 1024 * 1024

_F32 = jnp.float32
_BF16 = jnp.bfloat16
_U32 = jnp.uint32


def _dot(a, b):
    return jnp.dot(a, b, preferred_element_type=_F32)


def _layer_norm(h, g, b):
    mu = jnp.mean(h, axis=-1, keepdims=True)
    hc = h - mu
    var = jnp.mean(hc * hc, axis=-1, keepdims=True)
    return hc * lax.rsqrt(var + LN_EPS) * g + b


def _silu(x):
    return x * jax.nn.sigmoid(x)


def _gelu_tanh(x):
    c = 0.7978845608028654
    return 0.5 * x * (1.0 + jnp.tanh(c * (x + 0.044715 * (x * x * x))))


def _pack_halves(v):
    half = v.shape[1] // 2
    lo = lax.bitcast_convert_type(v[:, :half].astype(_BF16).astype(_F32), _U32)
    hi = lax.bitcast_convert_type(v[:, half:].astype(_BF16).astype(_F32), _U32)
    return (hi & _U32(0xFFFF0000)) | (lo >> 16)


def _unpack_halves(w):
    lo = lax.bitcast_convert_type(w << 16, _F32)
    hi = lax.bitcast_convert_type(w & _U32(0xFFFF0000), _F32)
    return lo, hi


def _store_pieces(ref, w):
    for c in range(PIECES):
        ref[c] = w[:, c * LANES:(c + 1) * LANES]


def _load_pieces(ref):
    return jnp.concatenate([ref[c] for c in range(PIECES)], axis=1)


def _pool_kernel(x_ref, win_ref, wgrp_ref, scale_ref, wout_ref, g_ref, b_ref,
                 o_ref, op_ref, zs_ref, y_ref):
    s = pl.program_id(1)
    ts = x_ref.shape[1]
    x = x_ref[0]
    z = _dot(x.astype(_BF16), win_ref[...])

    @pl.when(s == 0)
    def _():
        zs_ref[0:POOL_HALO, :] = jnp.zeros((POOL_HALO, D_MODEL), _F32)

    zs_ref[POOL_HALO:POOL_HALO + ts, :] = z
    pos = s * ts + lax.broadcasted_iota(jnp.int32, (ts, 1), 0)
    for g, w in enumerate(POOL_WINDOWS):
        c0 = g * POOL_GROUP_DIM
        c1 = c0 + POOL_GROUP_DIM
        zg = zs_ref[POOL_HALO:POOL_HALO + ts, c0:c1]
        acc = zg
        for k in range(1, w):
            acc = acc + zs_ref[POOL_HALO - k:POOL_HALO - k + ts, c0:c1]
        cnt = jnp.minimum(pos + 1, w).astype(_F32)
        pooled = acc / cnt - zg
        yg = _dot(pooled.astype(_BF16), wgrp_ref[g]) * scale_ref[:, c0:c1]
        y_ref[:, c0:c1] = yg.astype(_BF16)
    zs_ref[0:POOL_HALO, :] = zs_ref[ts:ts + POOL_HALO, :]
    mix = _dot(y_ref[...], wout_ref[...])
    out = _layer_norm(DEEPNORM_ALPHA * x + mix, g_ref[...], b_ref[...])
    o_ref[0] = out
    _store_pieces(op_ref, _pack_halves(out))


def _pool_layer(x, w_in, w_grp, scale, w_out, ln_g, ln_b):
    bsz, seq, d = x.shape
    ts = POOL_TILE
    steps = seq // ts
    const2 = lambda b, s: (0, 0)
    out_shape = (jax.ShapeDtypeStruct((bsz, seq, d), _F32),
                 jax.ShapeDtypeStruct((PIECES, bsz * seq, LANES), _U32))
    tile = pl.BlockSpec((1, ts, d), lambda b, s: (b, s, 0))
    ptile = pl.BlockSpec((PIECES, ts, LANES), lambda b, s: (0, b * steps + s, 0))
    return pl.pallas_call(
        _pool_kernel,
        grid=(bsz, steps),
        in_specs=[
            tile,
            pl.BlockSpec((d, d), const2),
            pl.BlockSpec((len(POOL_WINDOWS), POOL_GROUP_DIM, POOL_GROUP_DIM), lambda b, s: (0, 0, 0)),
            pl.BlockSpec((1, d), const2),
            pl.BlockSpec((d, d), const2),
            pl.BlockSpec((1, d), const2),
            pl.BlockSpec((1, d), const2),
        ],
        out_specs=(tile, ptile),
        out_shape=out_shape,
        scratch_shapes=[pltpu.VMEM((POOL_HALO + ts, d), _F32),
                        pltpu.VMEM((ts, d), _BF16)],
        compiler_params=pltpu.CompilerParams(
            dimension_semantics=("arbitrary", "arbitrary"),
            vmem_limit_bytes=VMEM_LIMIT),
        name="pool_mixer",
    )(x, w_in.astype(_BF16), w_grp.astype(_BF16), scale.reshape(1, d),
      w_out.astype(_BF16), ln_g.reshape(1, d), ln_b.reshape(1, d))


def _sgu_kernel(x_ref, win_ref, bin_ref, lng_ref, lnb_ref, ws_ref, bs_ref, wout_ref,
                g_ref, b_ref, o_ref, op_ref, gated_ref):
    ts = x_ref.shape[0]
    x = x_ref[...]
    xb = x.astype(_BF16)
    v = _gelu_tanh(_dot(xb, win_ref[:, SGU_WIDTH:]) + bin_ref[:, SGU_WIDTH:])
    v = _layer_norm(v, lng_ref[...], lnb_ref[...]).astype(_BF16)
    u = _gelu_tanh(_dot(xb, win_ref[:, :SGU_WIDTH]) + bin_ref[:, :SGU_WIDTH])
    for c in range(ts // SGU_CHUNK):
        r0 = c * SGU_CHUNK
        for h in range(SGU_HEADS):
            c0 = h * SGU_HEAD_DIM
            sv = _dot(ws_ref[h], v[r0:r0 + SGU_CHUNK, c0:c0 + SGU_HEAD_DIM]) + bs_ref[h]
            gated_ref[r0:r0 + SGU_CHUNK, c0:c0 + SGU_HEAD_DIM] = (
                u[r0:r0 + SGU_CHUNK, c0:c0 + SGU_HEAD_DIM] * sv).astype(_BF16)
    mix = _dot(gated_ref[...], wout_ref[...])
    out = _layer_norm(DEEPNORM_ALPHA * x + mix, g_ref[...], b_ref[...])
    o_ref[...] = out
    _store_pieces(op_ref, _pack_halves(out))


def _sgu_layer(x, w_in, b_in, ln_g, ln_b, w_s, b_s, w_out, mix_g, mix_b):
    n_tok, d = x.shape
    ts = SGU_TILE
    const2 = lambda i: (0, 0)
    const3 = lambda i: (0, 0, 0)
    causal = jnp.tril(jnp.ones((SGU_CHUNK, SGU_CHUNK), w_s.dtype))
    ws = (w_s * causal[None]).astype(_BF16)
    tile = pl.BlockSpec((ts, d), lambda i: (i, 0))
    ptile = pl.BlockSpec((PIECES, ts, LANES), lambda i: (0, i, 0))
    out_shape = (jax.ShapeDtypeStruct((n_tok, d), _F32),
                 jax.ShapeDtypeStruct((PIECES, n_tok, LANES), _U32))
    return pl.pallas_call(
        _sgu_kernel,
        grid=(n_tok // ts,),
        in_specs=[
            tile,
            pl.BlockSpec((d, 2 * SGU_WIDTH), const2),
            pl.BlockSpec((1, 2 * SGU_WIDTH), const2),
            pl.BlockSpec((1, SGU_WIDTH), const2),
            pl.BlockSpec((1, SGU_WIDTH), const2),
            pl.BlockSpec((SGU_HEADS, SGU_CHUNK, SGU_CHUNK), const3),
            pl.BlockSpec((SGU_HEADS, SGU_CHUNK, 1), const3),
            pl.BlockSpec((SGU_WIDTH, d), const2),
            pl.BlockSpec((1, d), const2),
            pl.BlockSpec((1, d), const2),
        ],
        out_specs=(tile, ptile),
        out_shape=out_shape,
        scratch_shapes=[pltpu.VMEM((ts, SGU_WIDTH), _BF16)],
        compiler_params=pltpu.CompilerParams(
            dimension_semantics=("arbitrary",),
            vmem_limit_bytes=VMEM_LIMIT),
        name="sgu_mixer",
    )(x, w_in.astype(_BF16), b_in.reshape(1, -1), ln_g.reshape(1, -1), ln_b.reshape(1, -1),
      ws, b_s.reshape(SGU_HEADS, SGU_CHUNK, 1), w_out.astype(_BF16),
      mix_g.reshape(1, d), mix_b.reshape(1, d))


def _route_kernel(x_ref, wrt_ref, bias_ref, ek_ref, rk_ref, gk_ref, cnt_ref, carry_ref):
    i = pl.program_id(0)
    ts = x_ref.shape[0]

    @pl.when(i == 0)
    def _():
        carry_ref[...] = jnp.zeros_like(carry_ref)

    logits = lax.dot_general(wrt_ref[...], x_ref[...], (((1,), (1,)), ((), ())),
                             precision=lax.Precision.HIGHEST,
                             preferred_element_type=_F32)
    scores = jax.nn.sigmoid(logits).reshape(N_GROUPS, GROUP_SIZE, ts)
    biased = scores + bias_ref[...]
    neg_inf = jnp.float32(-jnp.inf)
    shape3 = (N_GROUPS, GROUP_SIZE, ts)
    in_grp = lax.broadcasted_iota(jnp.int32, shape3, 1)
    grp = lax.broadcasted_iota(jnp.int32, shape3, 0)
    eid = grp * GROUP_SIZE + in_grp

    m1 = jnp.max(biased, axis=1, keepdims=True)
    first1 = jnp.min(jnp.where(biased == m1, in_grp, GROUP_SIZE), axis=1, keepdims=True)
    m2 = jnp.max(jnp.where(in_grp == first1, neg_inf, biased), axis=1, keepdims=True)
    gscore = m1 + m2

    gid = lax.broadcasted_iota(jnp.int32, (N_GROUPS, 1, ts), 0)
    gsel = jnp.zeros((N_GROUPS, 1, ts), jnp.bool_)
    for _ in range(TOPK_GROUPS):
        m = jnp.max(gscore, axis=0, keepdims=True)
        first = jnp.min(jnp.where(gscore == m, gid, N_GROUPS), axis=0, keepdims=True)
        pick = gid == first
        gsel = jnp.logical_or(gsel, pick)
        gscore = jnp.where(pick, neg_inf, gscore)

    masked = jnp.where(gsel, biased, neg_inf)
    picked_any = jnp.zeros(shape3, jnp.bool_)
    e_k, s_k = [], []
    for _ in range(TOP_K):
        m = jnp.max(jnp.max(masked, axis=1, keepdims=True), axis=0, keepdims=True)
        first = jnp.min(jnp.min(jnp.where(masked == m, eid, N_EXPERTS), axis=1, keepdims=True),
                        axis=0, keepdims=True)
        pick = eid == first
        picked_any = jnp.logical_or(picked_any, pick)
        masked = jnp.where(pick, neg_inf, masked)
        sc = jnp.sum(jnp.sum(jnp.where(pick, scores, 0.0), axis=1, keepdims=True),
                     axis=0, keepdims=True)
        e_k.append(first)
        s_k.append(sc)
    denom = s_k[0]
    for k in range(1, TOP_K):
        denom = denom + s_k[k]

    sel = picked_any.astype(_F32).reshape(N_EXPERTS, ts)
    row = lax.broadcasted_iota(jnp.int32, (ts, ts), 0)
    col = lax.broadcasted_iota(jnp.int32, (ts, ts), 1)
    earlier = (row < col).astype(_BF16)
    rank = _dot(sel.astype(_BF16), earlier) + carry_ref[...]
    rank3 = rank.reshape(shape3)
    carry_ref[...] += jnp.sum(sel, axis=1, keepdims=True)
    cnt_ref[...] = carry_ref[...]

    for k in range(TOP_K):
        r = jnp.sum(jnp.sum(jnp.where(eid == e_k[k], rank3, 0.0), axis=1, keepdims=True),
                    axis=0, keepdims=True)
        ek_ref[k:k + 1, :] = e_k[k].reshape(1, ts)
        rk_ref[k:k + 1, :] = r.reshape(1, ts).astype(jnp.int32)
        gk_ref[k:k + 1, :] = (s_k[k] / denom * ROUTED_SCALE).reshape(1, ts)


def _route(x, w_router, router_bias):
    n_tok, d = x.shape
    ts = ROUTE_TILE
    out_shape = (jax.ShapeDtypeStruct((TOP_K, n_tok), jnp.int32),
                 jax.ShapeDtypeStruct((TOP_K, n_tok), jnp.int32),
                 jax.ShapeDtypeStruct((TOP_K, n_tok), _F32),
                 jax.ShapeDtypeStruct((N_EXPERTS, 1), _F32))
    kspec = pl.BlockSpec((TOP_K, ts), lambda i: (0, i))
    return pl.pallas_call(
        _route_kernel,
        grid=(n_tok // ts,),
        in_specs=[
            pl.BlockSpec((ts, d), lambda i: (i, 0)),
            pl.BlockSpec((N_EXPERTS, d), lambda i: (0, 0)),
            pl.BlockSpec((N_GROUPS, GROUP_SIZE, 1), lambda i: (0, 0, 0)),
        ],
        out_specs=(kspec, kspec, kspec, pl.BlockSpec((N_EXPERTS, 1), lambda i: (0, 0))),
        out_shape=out_shape,
        scratch_shapes=[pltpu.VMEM((N_EXPERTS, 1), _F32)],
        compiler_params=pltpu.CompilerParams(
            dimension_semantics=("arbitrary",),
            vmem_limit_bytes=VMEM_LIMIT),
        name="moe_route",
    )(x, w_router.T, router_bias.reshape(N_GROUPS, GROUP_SIZE, 1))


def _sc_mesh():
    return plsc.VectorSubcoreMesh(core_axis_name="core", subcore_axis_name="subcore")


def _sc_scatter_rows(src, idx, n_out):
    d = src.shape[1]
    n_tok = idx.shape[1]
    wins = n_tok // SC_WINDOW

    def body(src_hbm, idx_hbm, out_hbm):
        def step(src_vmem, idx_vmem):
            for k in range(TOP_K):
                pltpu.sync_copy(src_vmem, out_hbm.at[idx_vmem.at[k]])

        pltpu.emit_pipeline(
            step,
            grid=(src.shape[0] // SC_WINDOW,),
            in_specs=[pl.BlockSpec((SC_WINDOW, d), index_map=lambda i: (i, 0)),
                      pl.BlockSpec((TOP_K, SC_WINDOW), index_map=lambda i: (i // wins, i % wins))],
            out_specs=[],
            core_axis_name=("core", "subcore"),
            dimension_semantics=(pltpu.PARALLEL,),
        )(src_hbm, idx_hbm)

    return pl.kernel(body, out_type=jax.ShapeDtypeStruct((n_out, d), src.dtype),
                     mesh=_sc_mesh(), scratch_types=[], name="sc_scatter_rows")(src, idx)


def _sc_gather_rows(table, idx):
    d = table.shape[1]
    n_idx_rows, n_tok = idx.shape
    wins = n_tok // SC_WINDOW

    def body(table_hbm, idx_hbm, out_hbm):
        def step(idx_vmem, out_vmem):
            pltpu.sync_copy(table_hbm.at[idx_vmem.at[0]], out_vmem)

        pltpu.emit_pipeline(
            step,
            grid=(n_idx_rows * wins,),
            in_specs=[pl.BlockSpec((1, SC_WINDOW), index_map=lambda i: (i // wins, i % wins))],
            out_specs=[pl.BlockSpec((SC_WINDOW, d), index_map=lambda i: (i, 0))],
            core_axis_name=("core", "subcore"),
            dimension_semantics=(pltpu.PARALLEL,),
        )(idx_hbm, out_hbm)

    return pl.kernel(body, out_type=jax.ShapeDtypeStruct((n_idx_rows * n_tok, d), table.dtype),
                     mesh=_sc_mesh(), scratch_types=[], name="sc_gather_rows")(table, idx)


def _index_kernel(start_ref, ek_ref, rk_ref, idx_ref, *, n_rows):
    e = ek_ref[...]
    start = jnp.zeros(e.shape, jnp.int32)
    for ex in range(N_EXPERTS):
        start = jnp.where(e == ex, start_ref[ex], start)
    dest = start + rk_ref[...]
    for c in range(PIECES):
        idx_ref[c] = dest + c * n_rows


def _row_indices(padded_start, e_k, r_k, n_rows):
    n_tok = e_k.shape[1]
    ts = INDEX_TILE
    kspec = pl.BlockSpec((TOP_K, ts), lambda i, st: (0, i))
    grid_spec = pltpu.PrefetchScalarGridSpec(
        num_scalar_prefetch=1,
        grid=(n_tok // ts,),
        in_specs=[kspec, kspec],
        out_specs=pl.BlockSpec((PIECES, TOP_K, ts), lambda i, st: (0, 0, i)),
    )
    return pl.pallas_call(
        functools.partial(_index_kernel, n_rows=n_rows),
        grid_spec=grid_spec,
        out_shape=jax.ShapeDtypeStruct((PIECES, TOP_K, n_tok), jnp.int32),
        compiler_params=pltpu.CompilerParams(dimension_semantics=("arbitrary",)),
        name="moe_row_indices",
    )(padded_start, e_k, r_k)


def _expert_kernel(be_ref, nb_ref, xs_ref, wg_ref, wu_ref, wd_ref, o_ref, wgu_bf, wd_bf):
    i = pl.program_id(0)

    @pl.when(jnp.logical_or(i == 0, be_ref[i] != be_ref[jnp.maximum(i - 1, 0)]))
    def _():
        wgu_bf[:, :EXPERT_DIM] = wg_ref[0, 0].astype(_BF16)
        wgu_bf[:, EXPERT_DIM:] = wu_ref[0, 0].astype(_BF16)
        wd_bf[...] = wd_ref[0, 0].astype(_BF16)

    @pl.when(i < nb_ref[0])
    def _():
        lo, hi = _unpack_halves(_load_pieces(xs_ref))
        gu = (_dot(lo.astype(_BF16), wgu_bf[:PACKED]) +
              _dot(hi.astype(_BF16), wgu_bf[PACKED:]))
        h = _silu(gu[:, :EXPERT_DIM]) * gu[:, EXPERT_DIM:]
        y = _dot(h.astype(_BF16), wd_bf[...])
        _store_pieces(o_ref, _pack_halves(y))


def _experts(layer, block_e, n_blocks_used, xs, w_gate, w_up, w_down):
    n_rows = xs.shape[1]
    d = D_MODEL
    bm = EXPERT_ROWS
    wspec_in = pl.BlockSpec((1, 1, d, EXPERT_DIM), lambda i, be, nb: (layer, be[i], 0, 0))
    rows_spec = pl.BlockSpec((PIECES, bm, LANES), lambda i, be, nb: (0, i, 0))
    grid_spec = pltpu.PrefetchScalarGridSpec(
        num_scalar_prefetch=2,
        grid=(n_rows // bm,),
        in_specs=[
            rows_spec,
            wspec_in,
            wspec_in,
            pl.BlockSpec((1, 1, EXPERT_DIM, d), lambda i, be, nb: (layer, be[i], 0, 0)),
        ],
        out_specs=rows_spec,
        scratch_shapes=[pltpu.VMEM((d, 2 * EXPERT_DIM), _BF16),
                        pltpu.VMEM((EXPERT_DIM, d), _BF16)],
    )
    return pl.pallas_call(
        _expert_kernel,
        grid_spec=grid_spec,
        out_shape=jax.ShapeDtypeStruct(xs.shape, _U32),
        compiler_params=pltpu.CompilerParams(
            dimension_semantics=("arbitrary",),
            vmem_limit_bytes=VMEM_LIMIT),
        name="moe_experts",
    )(block_e, n_blocks_used, xs, w_gate, w_up, w_down)


def _combine_kernel(x_ref, p_ref, gate_ref, sg_ref, su_ref, sd_ref, g_ref, b_ref, o_ref):
    x = x_ref[...]
    xb = x.astype(_BF16)
    gates = gate_ref[...]
    r_lo = r_hi = None
    for k in range(TOP_K):
        w = jnp.concatenate([p_ref[c, k] for c in range(PIECES)], axis=1)
        lo, hi = _unpack_halves(w)
        gk = gates[:, k:k + 1]
        r_lo = gk * lo if r_lo is None else r_lo + gk * lo
        r_hi = gk * hi if r_hi is None else r_hi + gk * hi
    routed = jnp.concatenate([r_lo, r_hi], axis=1)
    h = _silu(_dot(xb, sg_ref[...])) * _dot(xb, su_ref[...])
    shared = _dot(h.astype(_BF16), sd_ref[...])
    o_ref[...] = _layer_norm(DEEPNORM_ALPHA * x + (routed + shared), g_ref[...], b_ref[...])


def _combine(x, picked, gates, sh_gate, sh_up, sh_down, ln_g, ln_b):
    n_tok, d = x.shape
    ts = COMBINE_TILE
    const2 = lambda i: (0, 0)
    return pl.pallas_call(
        _combine_kernel,
        grid=(n_tok // ts,),
        in_specs=[
            pl.BlockSpec((ts, d), lambda i: (i, 0)),
            pl.BlockSpec((PIECES, TOP_K, ts, LANES), lambda i: (0, 0, i, 0)),
            pl.BlockSpec((ts, TOP_K), lambda i: (i, 0)),
            pl.BlockSpec((d, EXPERT_DIM), const2),
            pl.BlockSpec((d, EXPERT_DIM), const2),
            pl.BlockSpec((EXPERT_DIM, d), const2),
            pl.BlockSpec((1, d), const2),
            pl.BlockSpec((1, d), const2),
        ],
        out_specs=pl.BlockSpec((ts, d), lambda i: (i, 0)),
        out_shape=jax.ShapeDtypeStruct((n_tok, d), _F32),
        compiler_params=pltpu.CompilerParams(
            dimension_semantics=("arbitrary",),
            vmem_limit_bytes=VMEM_LIMIT),
        name="moe_combine",
    )(x, picked, gates, sh_gate.astype(_BF16), sh_up.astype(_BF16), sh_down.astype(_BF16),
      ln_g.reshape(1, d), ln_b.reshape(1, d))


def _moe_layer(layer, x, x_packed, w_router, router_bias, w_gate, w_up, w_down,
               sh_gate, sh_up, sh_down, ln_g, ln_b):
    n_tok, d = x.shape
    bm = EXPERT_ROWS
    e_k, r_k, g_k, counts = _route(x, w_router, router_bias)
    counts = counts.reshape(N_EXPERTS).astype(jnp.int32)
    padded = (counts + bm - 1) // bm * bm
    padded_end = jnp.cumsum(padded)
    padded_start = padded_end - padded
    n_rows = n_tok * TOP_K + N_EXPERTS * bm
    n_blocks = n_rows // bm
    block_start = jnp.arange(n_blocks, dtype=jnp.int32) * bm
    block_e = jnp.minimum(jnp.sum(padded_end[None, :] <= block_start[:, None], axis=1),
                          N_EXPERTS - 1).astype(jnp.int32)
    n_used = (padded_end[-1] // bm).astype(jnp.int32).reshape(1)

    idx = _row_indices(padded_start, e_k, r_k, n_rows).reshape(PIECES * TOP_K, n_tok)
    xs = _sc_scatter_rows(x_packed.reshape(PIECES * n_tok, LANES), idx, PIECES * n_rows)
    rows = _experts(layer, block_e, n_used, xs.reshape(PIECES, n_rows, LANES),
                    w_gate, w_up, w_down)
    picked = _sc_gather_rows(rows.reshape(PIECES * n_rows, LANES), idx)
    picked = picked.reshape(PIECES, TOP_K, n_tok, LANES)
    return _combine(x, picked, g_k.T, sh_gate, sh_up, sh_down, ln_g, ln_b)


def kernel(x, pool_w_in, pool_w_grp, pool_scale, pool_w_out, sgu_w_in, sgu_b_in, sgu_ln_g, sgu_ln_b, sgu_w_s, sgu_b_s, sgu_w_out, ln_mix_g, ln_mix_b, moe_w_router, moe_router_bias, moe_w_gate, moe_w_up, moe_w_down, moe_sh_gate, moe_sh_up, moe_sh_down, ln_ffn_g, ln_ffn_b):
    bsz, seq, d = x.shape
    n_tok = bsz * seq

    def moe(i, h, h_packed):
        return _moe_layer(i, h, h_packed, moe_w_router[i], moe_router_bias[i], moe_w_gate,
                          moe_w_up, moe_w_down, moe_sh_gate[i], moe_sh_up[i],
                          moe_sh_down[i], ln_ffn_g[i], ln_ffn_b[i])

    h, h_packed = _pool_layer(x, pool_w_in[0], pool_w_grp[0], pool_scale[0], pool_w_out[0],
                              ln_mix_g[0], ln_mix_b[0])
    h = moe(0, h.reshape(n_tok, d), h_packed)
    h, h_packed = _sgu_layer(h, sgu_w_in[0], sgu_b_in[0], sgu_ln_g[0], sgu_ln_b[0], sgu_w_s[0],
                             sgu_b_s[0], sgu_w_out[0], ln_mix_g[1], ln_mix_b[1])
    h = moe(1, h, h_packed)
    return h.reshape(bsz, seq, d)
```

```python
import functools

import jax
import jax.numpy as jnp
from jax import lax
from jax.experimental import pallas as pl
from jax.experimental.pallas import tpu as pltpu
from jax.experimental.pallas import tpu_sc as plsc

D_MODEL = 1024
DEPTH = 2
POOL_WINDOWS = (2, 4, 8, 16)
POOL_GROUP_DIM = D_MODEL // len(POOL_WINDOWS)
POOL_HALO = 16
SGU_CHUNK = 128
SGU_HEADS = 4
SGU_WIDTH = 2 * D_MODEL
SGU_HEAD_DIM = SGU_WIDTH // SGU_HEADS
N_EXPERTS = 64
TOP_K = 8
N_GROUPS = 8
GROUP_SIZE = N_EXPERTS // N_GROUPS
TOPK_GROUPS = 4
EXPERT_DIM = D_MODEL // 4
ROUTED_SCALE = 2.5
DEEPNORM_ALPHA = (2 * DEPTH) ** 0.25
LN_EPS = 1e-5

LANES = 128
PACKED = D_MODEL // 2
PIECES = PACKED // LANES

POOL_TILE = 512
SGU_TILE = 256
ROUTE_TILE = 512
INDEX_TILE = 2048
EXPERT_ROWS = 1024
COMBINE_TILE = 256
SC_WINDOW = 128
VMEM_LIMIT = 56 * 1024 * 1024

_F32 = jnp.float32
_BF16 = jnp.bfloat16
_U32 = jnp.uint32


def _dot(a, b):
    return jnp.dot(a, b, preferred_element_type=_F32)


def _layer_norm(h, g, b):
    mu = jnp.mean(h, axis=-1, keepdims=True)
    hc = h - mu
    var = jnp.mean(hc * hc, axis=-1, keepdims=True)
    return hc * lax.rsqrt(var + LN_EPS) * g + b


def _silu(x):
    return x * jax.nn.sigmoid(x)


def _gelu_tanh(x):
    c = 0.7978845608028654
    return 0.5 * x * (1.0 + jnp.tanh(c * (x + 0.044715 * (x * x * x))))


def _pack_halves(v):
    half = v.shape[1] // 2
    lo = lax.bitcast_convert_type(v[:, :half].astype(_BF16).astype(_F32), _U32)
    hi = lax.bitcast_convert_type(v[:, half:].astype(_BF16).astype(_F32), _U32)
    return (hi & _U32(0xFFFF0000)) | (lo >> 16)


def _unpack_halves(w):
    lo = lax.bitcast_convert_type(w << 16, _F32)
    hi = lax.bitcast_convert_type(w & _U32(0xFFFF0000), _F32)
    return lo, hi


def _store_pieces(ref, w):
    for c in range(PIECES):
        ref[c] = w[:, c * LANES:(c + 1) * LANES]


def _load_pieces(ref):
    return jnp.concatenate([ref[c] for c in range(PIECES)], axis=1)


def _pool_kernel(x_ref, win_ref, wgrp_ref, scale_ref, wout_ref, g_ref, b_ref,
                 o_ref, op_ref, zs_ref, y_ref):
    s = pl.program_id(1)
    ts = x_ref.shape[1]
    x = x_ref[0]
    z = _dot(x.astype(_BF16), win_ref[...])

    @pl.when(s == 0)
    def _():
        zs_ref[0:POOL_HALO, :] = jnp.zeros((POOL_HALO, D_MODEL), _F32)

    zs_ref[POOL_HALO:POOL_HALO + ts, :] = z
    pos = s * ts + lax.broadcasted_iota(jnp.int32, (ts, 1), 0)
    for g, w in enumerate(POOL_WINDOWS):
        c0 = g * POOL_GROUP_DIM
        c1 = c0 + POOL_GROUP_DIM
        zg = zs_ref[POOL_HALO:POOL_HALO + ts, c0:c1]
        acc = zg
        for k in range(1, w):
            acc = acc + zs_ref[POOL_HALO - k:POOL_HALO - k + ts, c0:c1]
        cnt = jnp.minimum(pos + 1, w).astype(_F32)
        pooled = acc / cnt - zg
        yg = _dot(pooled.astype(_BF16), wgrp_ref[g]) * scale_ref[:, c0:c1]
        y_ref[:, c0:c1] = yg.astype(_BF16)
    zs_ref[0:POOL_HALO, :] = zs_ref[ts:ts + POOL_HALO, :]
    mix = _dot(y_ref[...], wout_ref[...])
    out = _layer_norm(DEEPNORM_ALPHA * x + mix, g_ref[...], b_ref[...])
    o_ref[0] = out
    _store_pieces(op_ref, _pack_halves(out))


def _pool_layer(x, w_in, w_grp, scale, w_out, ln_g, ln_b):
    bsz, seq, d = x.shape
    ts = POOL_TILE
    steps = seq // ts
    const2 = lambda b, s: (0, 0)
    out_shape = (jax.ShapeDtypeStruct((bsz, seq, d), _F32),
                 jax.ShapeDtypeStruct((PIECES, bsz * seq, LANES), _U32))
    tile = pl.BlockSpec((1, ts, d), lambda b, s: (b, s, 0))
    ptile = pl.BlockSpec((PIECES, ts, LANES), lambda b, s: (0, b * steps + s, 0))
    return pl.pallas_call(
        _pool_kernel,
        grid=(bsz, steps),
        in_specs=[
            tile,
            pl.BlockSpec((d, d), const2),
            pl.BlockSpec((len(POOL_WINDOWS), POOL_GROUP_DIM, POOL_GROUP_DIM), lambda b, s: (0, 0, 0)),
            pl.BlockSpec((1, d), const2),
            pl.BlockSpec((d, d), const2),
            pl.BlockSpec((1, d), const2),
            pl.BlockSpec((1, d), const2),
        ],
        out_specs=(tile, ptile),
        out_shape=out_shape,
        scratch_shapes=[pltpu.VMEM((POOL_HALO + ts, d), _F32),
                        pltpu.VMEM((ts, d), _BF16)],
        compiler_params=pltpu.CompilerParams(
            dimension_semantics=("arbitrary", "arbitrary"),
            vmem_limit_bytes=VMEM_LIMIT),
        name="pool_mixer",
    )(x, w_in.astype(_BF16), w_grp.astype(_BF16), scale.reshape(1, d),
      w_out.astype(_BF16), ln_g.reshape(1, d), ln_b.reshape(1, d))


def _sgu_kernel(x_ref, win_ref, bin_ref, lng_ref, lnb_ref, ws_ref, bs_ref, wout_ref,
                g_ref, b_ref, o_ref, op_ref, gated_ref):
    ts = x_ref.shape[0]
    x = x_ref[...]
    xb = x.astype(_BF16)
    v = _gelu_tanh(_dot(xb, win_ref[:, SGU_WIDTH:]) + bin_ref[:, SGU_WIDTH:])
    v = _layer_norm(v, lng_ref[...], lnb_ref[...]).astype(_BF16)
    u = _gelu_tanh(_dot(xb, win_ref[:, :SGU_WIDTH]) + bin_ref[:, :SGU_WIDTH])
    for c in range(ts // SGU_CHUNK):
        r0 = c * SGU_CHUNK
        for h in range(SGU_HEADS):
            c0 = h * SGU_HEAD_DIM
            sv = _dot(ws_ref[h], v[r0:r0 + SGU_CHUNK, c0:c0 + SGU_HEAD_DIM]) + bs_ref[h]
            gated_ref[r0:r0 + SGU_CHUNK, c0:c0 + SGU_HEAD_DIM] = (
                u[r0:r0 + SGU_CHUNK, c0:c0 + SGU_HEAD_DIM] * sv).astype(_BF16)
    mix = _dot(gated_ref[...], wout_ref[...])
    out = _layer_norm(DEEPNORM_ALPHA * x + mix, g_ref[...], b_ref[...])
    o_ref[...] = out
    _store_pieces(op_ref, _pack_halves(out))


def _sgu_layer(x, w_in, b_in, ln_g, ln_b, w_s, b_s, w_out, mix_g, mix_b):
    n_tok, d = x.shape
    ts = SGU_TILE
    const2 = lambda i: (0, 0)
    const3 = lambda i: (0, 0, 0)
    causal = jnp.tril(jnp.ones((SGU_CHUNK, SGU_CHUNK), w_s.dtype))
    ws = (w_s * causal[None]).astype(_BF16)
    tile = pl.BlockSpec((ts, d), lambda i: (i, 0))
    ptile = pl.BlockSpec((PIECES, ts, LANES), lambda i: (0, i, 0))
    out_shape = (jax.ShapeDtypeStruct((n_tok, d), _F32),
                 jax.ShapeDtypeStruct((PIECES, n_tok, LANES), _U32))
    return pl.pallas_call(
        _sgu_kernel,
        grid=(n_tok // ts,),
        in_specs=[
            tile,
            pl.BlockSpec((d, 2 * SGU_WIDTH), const2),
            pl.BlockSpec((1, 2 * SGU_WIDTH), const2),
            pl.BlockSpec((1, SGU_WIDTH), const2),
            pl.BlockSpec((1, SGU_WIDTH), const2),
            pl.BlockSpec((SGU_HEADS, SGU_CHUNK, SGU_CHUNK), const3),
            pl.BlockSpec((SGU_HEADS, SGU_CHUNK, 1), const3),
            pl.BlockSpec((SGU_WIDTH, d), const2),
            pl.BlockSpec((1, d), const2),
            pl.BlockSpec((1, d), const2),
        ],
        out_specs=(tile, ptile),
        out_shape=out_shape,
        scratch_shapes=[pltpu.VMEM((ts, SGU_WIDTH), _BF16)],
        compiler_params=pltpu.CompilerParams(
            dimension_semantics=("arbitrary",),
            vmem_limit_bytes=VMEM_LIMIT),
        name="sgu_mixer",
    )(x, w_in.astype(_BF16), b_in.reshape(1, -1), ln_g.reshape(1, -1), ln_b.reshape(1, -1),
      ws, b_s.reshape(SGU_HEADS, SGU_CHUNK, 1), w_out.astype(_BF16),
      mix_g.reshape(1, d), mix_b.reshape(1, d))


def _route_kernel(x_ref, wrt_ref, bias_ref, ek_ref, rk_ref, gk_ref, cnt_ref, carry_ref):
    i = pl.program_id(0)
    ts = x_ref.shape[0]

    @pl.when(i == 0)
    def _():
        carry_ref[...] = jnp.zeros_like(carry_ref)

    logits = lax.dot_general(wrt_ref[...], x_ref[...], (((1,), (1,)), ((), ())),
                             precision=lax.Precision.HIGHEST,
                             preferred_element_type=_F32)
    scores = jax.nn.sigmoid(logits).reshape(N_GROUPS, GROUP_SIZE, ts)
    biased = scores + bias_ref[...]
    neg_inf = jnp.float32(-jnp.inf)
    shape3 = (N_GROUPS, GROUP_SIZE, ts)
    in_grp = lax.broadcasted_iota(jnp.int32, shape3, 1)
    grp = lax.broadcasted_iota(jnp.int32, shape3, 0)
    eid = grp * GROUP_SIZE + in_grp

    m1 = jnp.max(biased, axis=1, keepdims=True)
    first1 = jnp.min(jnp.where(biased == m1, in_grp, GROUP_SIZE), axis=1, keepdims=True)
    m2 = jnp.max(jnp.where(in_grp == first1, neg_inf, biased), axis=1, keepdims=True)
    gscore = m1 + m2

    gid = lax.broadcasted_iota(jnp.int32, (N_GROUPS, 1, ts), 0)
    gsel = jnp.zeros((N_GROUPS, 1, ts), jnp.bool_)
    for _ in range(TOPK_GROUPS):
        m = jnp.max(gscore, axis=0, keepdims=True)
        first = jnp.min(jnp.where(gscore == m, gid, N_GROUPS), axis=0, keepdims=True)
        pick = gid == first
        gsel = jnp.logical_or(gsel, pick)
        gscore = jnp.where(pick, neg_inf, gscore)

    masked = jnp.where(gsel, biased, neg_inf)
    picked_any = jnp.zeros(shape3, jnp.bool_)
    e_k, s_k = [], []
    for _ in range(TOP_K):
        m = jnp.max(jnp.max(masked, axis=1, keepdims=True), axis=0, keepdims=True)
        first = jnp.min(jnp.min(jnp.where(masked == m, eid, N_EXPERTS), axis=1, keepdims=True),
                        axis=0, keepdims=True)
        pick = eid == first
        picked_any = jnp.logical_or(picked_any, pick)
        masked = jnp.where(pick, neg_inf, masked)
        sc = jnp.sum(jnp.sum(jnp.where(pick, scores, 0.0), axis=1, keepdims=True),
                     axis=0, keepdims=True)
        e_k.append(first)
        s_k.append(sc)
    denom = s_k[0]
    for k in range(1, TOP_K):
        denom = denom + s_k[k]

    sel = picked_any.astype(_F32).reshape(N_EXPERTS, ts)
    row = lax.broadcasted_iota(jnp.int32, (ts, ts), 0)
    col = lax.broadcasted_iota(jnp.int32, (ts, ts), 1)
    earlier = (row < col).astype(_BF16)
    rank = _dot(sel.astype(_BF16), earlier) + carry_ref[...]
    rank3 = rank.reshape(shape3)
    carry_ref[...] += jnp.sum(sel, axis=1, keepdims=True)
    cnt_ref[...] = carry_ref[...]

    for k in range(TOP_K):
        r = jnp.sum(jnp.sum(jnp.where(eid == e_k[k], rank3, 0.0), axis=1, keepdims=True),
                    axis=0, keepdims=True)
        ek_ref[k:k + 1, :] = e_k[k].reshape(1, ts)
        rk_ref[k:k + 1, :] = r.reshape(1, ts).astype(jnp.int32)
        gk_ref[k:k + 1, :] = (s_k[k] / denom * ROUTED_SCALE).reshape(1, ts)


def _route(x, w_router, router_bias):
    n_tok, d = x.shape
    ts = ROUTE_TILE
    out_shape = (jax.ShapeDtypeStruct((TOP_K, n_tok), jnp.int32),
                 jax.ShapeDtypeStruct((TOP_K, n_tok), jnp.int32),
                 jax.ShapeDtypeStruct((TOP_K, n_tok), _F32),
                 jax.ShapeDtypeStruct((N_EXPERTS, 1), _F32))
    kspec = pl.BlockSpec((TOP_K, ts), lambda i: (0, i))
    return pl.pallas_call(
        _route_kernel,
        grid=(n_tok // ts,),
        in_specs=[
            pl.BlockSpec((ts, d), lambda i: (i, 0)),
            pl.BlockSpec((N_EXPERTS, d), lambda i: (0, 0)),
            pl.BlockSpec((N_GROUPS, GROUP_SIZE, 1), lambda i: (0, 0, 0)),
        ],
        out_specs=(kspec, kspec, kspec, pl.BlockSpec((N_EXPERTS, 1), lambda i: (0, 0))),
        out_shape=out_shape,
        scratch_shapes=[pltpu.VMEM((N_EXPERTS, 1), _F32)],
        compiler_params=pltpu.CompilerParams(
            dimension_semantics=("arbitrary",),
            vmem_limit_bytes=VMEM_LIMIT),
        name="moe_route",
    )(x, w_router.T, router_bias.reshape(N_GROUPS, GROUP_SIZE, 1))


def _sc_mesh():
    return plsc.VectorSubcoreMesh(core_axis_name="core", subcore_axis_name="subcore")


def _sc_scatter_rows(src, idx, n_out):
    d = src.shape[1]
    n_tok = idx.shape[1]
    wins = n_tok // SC_WINDOW

    def body(src_hbm, idx_hbm, out_hbm):
        def step(src_vmem, idx_vmem):
            for k in range(TOP_K):
                pltpu.sync_copy(src_vmem, out_hbm.at[idx_vmem.at[k]])

        pltpu.emit_pipeline(
            step,
            grid=(src.shape[0] // SC_WINDOW,),
            in_specs=[pl.BlockSpec((SC_WINDOW, d), index_map=lambda i: (i, 0)),
                      pl.BlockSpec((TOP_K, SC_WINDOW), index_map=lambda i: (i // wins, i % wins))],
            out_specs=[],
            core_axis_name=("core", "subcore"),
            dimension_semantics=(pltpu.PARALLEL,),
        )(src_hbm, idx_hbm)

    return pl.kernel(body, out_type=jax.ShapeDtypeStruct((n_out, d), src.dtype),
                     mesh=_sc_mesh(), scratch_types=[], name="sc_scatter_rows")(src, idx)


def _sc_gather_rows(table, idx):
    d = table.shape[1]
    n_idx_rows, n_tok = idx.shape
    wins = n_tok // SC_WINDOW

    def body(table_hbm, idx_hbm, out_hbm):
        def step(idx_vmem, out_vmem):
            pltpu.sync_copy(table_hbm.at[idx_vmem.at[0]], out_vmem)

        pltpu.emit_pipeline(
            step,
            grid=(n_idx_rows * wins,),
            in_specs=[pl.BlockSpec((1, SC_WINDOW), index_map=lambda i: (i // wins, i % wins))],
            out_specs=[pl.BlockSpec((SC_WINDOW, d), index_map=lambda i: (i, 0))],
            core_axis_name=("core", "subcore"),
            dimension_semantics=(pltpu.PARALLEL,),
        )(idx_hbm, out_hbm)

    return pl.kernel(body, out_type=jax.ShapeDtypeStruct((n_idx_rows * n_tok, d), table.dtype),
                     mesh=_sc_mesh(), scratch_types=[], name="sc_gather_rows")(table, idx)


def _index_kernel(start_ref, ek_ref, rk_ref, idx_ref, *, n_rows):
    e = ek_ref[...]
    start = jnp.zeros(e.shape, jnp.int32)
    for ex in range(N_EXPERTS):
        start = jnp.where(e == ex, start_ref[ex], start)
    dest = start + rk_ref[...]
    for c in range(PIECES):
        idx_ref[c] = dest + c * n_rows


def _row_indices(padded_start, e_k, r_k, n_rows):
    n_tok = e_k.shape[1]
    ts = INDEX_TILE
    kspec = pl.BlockSpec((TOP_K, ts), lambda i, st: (0, i))
    grid_spec = pltpu.PrefetchScalarGridSpec(
        num_scalar_prefetch=1,
        grid=(n_tok // ts,),
        in_specs=[kspec, kspec],
        out_specs=pl.BlockSpec((PIECES, TOP_K, ts), lambda i, st: (0, 0, i)),
    )
    return pl.pallas_call(
        functools.partial(_index_kernel, n_rows=n_rows),
        grid_spec=grid_spec,
        out_shape=jax.ShapeDtypeStruct((PIECES, TOP_K, n_tok), jnp.int32),
        compiler_params=pltpu.CompilerParams(dimension_semantics=("arbitrary",)),
        name="moe_row_indices",
    )(padded_start, e_k, r_k)


def _expert_kernel(be_ref, nb_ref, xs_ref, wg_ref, wu_ref, wd_ref, o_ref, wgu_bf, wd_bf):
    i = pl.program_id(0)

    @pl.when(jnp.logical_or(i == 0, be_ref[i] != be_ref[jnp.maximum(i - 1, 0)]))
    def _():
        wgu_bf[:, :EXPERT_DIM] = wg_ref[0, 0].astype(_BF16)
        wgu_bf[:, EXPERT_DIM:] = wu_ref[0, 0].astype(_BF16)
        wd_bf[...] = wd_ref[0, 0].astype(_BF16)

    @pl.when(i < nb_ref[0])
    def _():
        lo, hi = _unpack_halves(_load_pieces(xs_ref))
        gu = (_dot(lo.astype(_BF16), wgu_bf[:PACKED]) +
              _dot(hi.astype(_BF16), wgu_bf[PACKED:]))
        h = _silu(gu[:, :EXPERT_DIM]) * gu[:, EXPERT_DIM:]
        y = _dot(h.astype(_BF16), wd_bf[...])
        _store_pieces(o_ref, _pack_halves(y))


def _experts(layer, block_e, n_blocks_used, xs, w_gate, w_up, w_down):
    n_rows = xs.shape[1]
    d = D_MODEL
    bm = EXPERT_ROWS
    wspec_in = pl.BlockSpec((1, 1, d, EXPERT_DIM), lambda i, be, nb: (layer, be[i], 0, 0))
    rows_spec = pl.BlockSpec((PIECES, bm, LANES), lambda i, be, nb: (0, i, 0))
    grid_spec = pltpu.PrefetchScalarGridSpec(
        num_scalar_prefetch=2,
        grid=(n_rows // bm,),
        in_specs=[
            rows_spec,
            wspec_in,
            wspec_in,
            pl.BlockSpec((1, 1, EXPERT_DIM, d), lambda i, be, nb: (layer, be[i], 0, 0)),
        ],
        out_specs=rows_spec,
        scratch_shapes=[pltpu.VMEM((d, 2 * EXPERT_DIM), _BF16),
                        pltpu.VMEM((EXPERT_DIM, d), _BF16)],
    )
    return pl.pallas_call(
        _expert_kernel,
        grid_spec=grid_spec,
        out_shape=jax.ShapeDtypeStruct(xs.shape, _U32),
        compiler_params=pltpu.CompilerParams(
            dimension_semantics=("arbitrary",),
            vmem_limit_bytes=VMEM_LIMIT),
        name="moe_experts",
    )(block_e, n_blocks_used, xs, w_gate, w_up, w_down)


def _combine_kernel(x_ref, p_ref, gate_ref, sg_ref, su_ref, sd_ref, g_ref, b_ref, o_ref):
    x = x_ref[...]
    xb = x.astype(_BF16)
    gates = gate_ref[...]
    r_lo = r_hi = None
    for k in range(TOP_K):
        w = jnp.concatenate([p_ref[c, k] for c in range(PIECES)], axis=1)
        lo, hi = _unpack_halves(w)
        gk = gates[:, k:k + 1]
        r_lo = gk * lo if r_lo is None else r_lo + gk * lo
        r_hi = gk * hi if r_hi is None else r_hi + gk * hi
    routed = jnp.concatenate([r_lo, r_hi], axis=1)
    h = _silu(_dot(xb, sg_ref[...])) * _dot(xb, su_ref[...])
    shared = _dot(h.astype(_BF16), sd_ref[...])
    o_ref[...] = _layer_norm(DEEPNORM_ALPHA * x + (routed + shared), g_ref[...], b_ref[...])


def _combine(x, picked, gates, sh_gate, sh_up, sh_down, ln_g, ln_b):
    n_tok, d = x.shape
    ts = COMBINE_TILE
    const2 = lambda i: (0, 0)
    return pl.pallas_call(
        _combine_kernel,
        grid=(n_tok // ts,),
        in_specs=[
            pl.BlockSpec((ts, d), lambda i: (i, 0)),
            pl.BlockSpec((PIECES, TOP_K, ts, LANES), lambda i: (0, 0, i, 0)),
            pl.BlockSpec((ts, TOP_K), lambda i: (i, 0)),
            pl.BlockSpec((d, EXPERT_DIM), const2),
            pl.BlockSpec((d, EXPERT_DIM), const2),
            pl.BlockSpec((EXPERT_DIM, d), const2),
            pl.BlockSpec((1, d), const2),
            pl.BlockSpec((1, d), const2),
        ],
        out_specs=pl.BlockSpec((ts, d), lambda i: (i, 0)),
        out_shape=jax.ShapeDtypeStruct((n_tok, d), _F32),
        compiler_params=pltpu.CompilerParams(
            dimension_semantics=("arbitrary",),
            vmem_limit_bytes=VMEM_LIMIT),
        name="moe_combine",
    )(x, picked, gates, sh_gate.astype(_BF16), sh_up.astype(_BF16), sh_down.astype(_BF16),
      ln_g.reshape(1, d), ln_b.reshape(1, d))


def _moe_layer(layer, x, x_packed, w_router, router_bias, w_gate, w_up, w_down,
               sh_gate, sh_up, sh_down, ln_g, ln_b):
    n_tok, d = x.shape
    bm = EXPERT_ROWS
    e_k, r_k, g_k, counts = _route(x, w_router, router_bias)
    counts = counts.reshape(N_EXPERTS).astype(jnp.int32)
    padded = (counts + bm - 1) // bm * bm
    padded_end = jnp.cumsum(padded)
    padded_start = padded_end - padded
    n_rows = n_tok * TOP_K + N_EXPERTS * bm
    n_blocks = n_rows // bm
    block_start = jnp.arange(n_blocks, dtype=jnp.int32) * bm
    block_e = jnp.minimum(jnp.sum(padded_end[None, :] <= block_start[:, None], axis=1),
                          N_EXPERTS - 1).astype(jnp.int32)
    n_used = (padded_end[-1] // bm).astype(jnp.int32).reshape(1)

    idx = _row_indices(padded_start, e_k, r_k, n_rows).reshape(PIECES * TOP_K, n_tok)
    xs = _sc_scatter_rows(x_packed.reshape(PIECES * n_tok, LANES), idx, PIECES * n_rows)
    rows = _experts(layer, block_e, n_used, xs.reshape(PIECES, n_rows, LANES),
                    w_gate, w_up, w_down)
    picked = _sc_gather_rows(rows.reshape(PIECES * n_rows, LANES), idx)
    picked = picked.reshape(PIECES, TOP_K, n_tok, LANES)
    return _combine(x, picked, g_k.T, sh_gate, sh_up, sh_down, ln_g, ln_b)


def kernel(x, pool_w_in, pool_w_grp, pool_scale, pool_w_out, sgu_w_in, sgu_b_in, sgu_ln_g, sgu_ln_b, sgu_w_s, sgu_b_s, sgu_w_out, ln_mix_g, ln_mix_b, moe_w_router, moe_router_bias, moe_w_gate, moe_w_up, moe_w_down, moe_sh_gate, moe_sh_up, moe_sh_down, ln_ffn_g, ln_ffn_b):
    bsz, seq, d = x.shape
    n_tok = bsz * seq

    def moe(i, h, h_packed):
        return _moe_layer(i, h, h_packed, moe_w_router[i], moe_router_bias[i], moe_w_gate,
                          moe_w_up, moe_w_down, moe_sh_gate[i], moe_sh_up[i],
                          moe_sh_down[i], ln_ffn_g[i], ln_ffn_b[i])

    h, h_packed = _pool_layer(x, pool_w_in[0], pool_w_grp[0], pool_scale[0], pool_w_out[0],
                              ln_mix_g[0], ln_mix_b[0])
    h = moe(0, h.reshape(n_tok, d), h_packed)
    h, h_packed = _sgu_layer(h, sgu_w_in[0], sgu_b_in[0], sgu_ln_g[0], sgu_ln_b[0], sgu_w_s[0],
                             sgu_b_s[0], sgu_w_out[0], ln_mix_g[1], ln_mix_b[1])
    h = moe(1, h, h_packed)
    return h.reshape(bsz, seq, d)
```

```python
import functools

import jax
import jax.numpy as jnp
from jax import lax
from jax.experimental import pallas as pl
from jax.experimental.pallas import tpu as pltpu
from jax.experimental.pallas import tpu_sc as plsc

D_MODEL = 1024
DEPTH = 2
POOL_WINDOWS = (2, 4, 8, 16)
POOL_GROUP_DIM = D_MODEL // len(POOL_WINDOWS)
POOL_HALO = 16
SGU_CHUNK = 128
SGU_HEADS = 4
SGU_WIDTH = 2 * D_MODEL
SGU_HEAD_DIM = SGU_WIDTH // SGU_HEADS
N_EXPERTS = 64
TOP_K = 8
N_GROUPS = 8
GROUP_SIZE = N_EXPERTS // N_GROUPS
TOPK_GROUPS = 4
EXPERT_DIM = D_MODEL // 4
ROUTED_SCALE = 2.5
DEEPNORM_ALPHA = (2 * DEPTH) ** 0.25
LN_EPS = 1e-5

LANES = 128
PACKED = D_MODEL // 2
PIECES = PACKED // LANES

POOL_TILE = 512
SGU_TILE = 256
ROUTE_TILE = 512
INDEX_TILE = 2048
EXPERT_ROWS = 512
EXPERT_SUB = 256
COMBINE_TILE = 256
SC_WINDOW = 128
VMEM_LIMIT = 56 * 1024 * 1024

_F32 = jnp.float32
_BF16 = jnp.bfloat16
_U32 = jnp.uint32


def _dot(a, b):
    return jnp.dot(a, b, preferred_element_type=_F32)


def _layer_norm(h, g, b):
    mu = jnp.mean(h, axis=-1, keepdims=True)
    hc = h - mu
    var = jnp.mean(hc * hc, axis=-1, keepdims=True)
    return hc * lax.rsqrt(var + LN_EPS) * g + b


def _silu(x):
    return x * jax.nn.sigmoid(x)


def _gelu_tanh(x):
    c = 0.7978845608028654
    return 0.5 * x * (1.0 + jnp.tanh(c * (x + 0.044715 * (x * x * x))))


def _pack_halves(v):
    half = v.shape[1] // 2
    lo = lax.bitcast_convert_type(v[:, :half].astype(_BF16).astype(_F32), _U32)
    hi = lax.bitcast_convert_type(v[:, half:].astype(_BF16).astype(_F32), _U32)
    return (hi & _U32(0xFFFF0000)) | (lo >> 16)


def _unpack_halves(w):
    lo = lax.bitcast_convert_type(w << 16, _F32)
    hi = lax.bitcast_convert_type(w & _U32(0xFFFF0000), _F32)
    return lo, hi


def _store_pieces(ref, w):
    for c in range(PIECES):
        ref[c] = w[:, c * LANES:(c + 1) * LANES]


def _load_pieces(ref):
    return jnp.concatenate([ref[c] for c in range(PIECES)], axis=1)


def _pool_kernel(x_ref, win_ref, wgrp_ref, scale_ref, wout_ref, g_ref, b_ref,
                 o_ref, op_ref, zs_ref, y_ref):
    s = pl.program_id(1)
    ts = x_ref.shape[1]
    x = x_ref[0]
    z = _dot(x.astype(_BF16), win_ref[...])

    @pl.when(s == 0)
    def _():
        zs_ref[0:POOL_HALO, :] = jnp.zeros((POOL_HALO, D_MODEL), _F32)

    zs_ref[POOL_HALO:POOL_HALO + ts, :] = z
    pos = s * ts + lax.broadcasted_iota(jnp.int32, (ts, 1), 0)
    for g, w in enumerate(POOL_WINDOWS):
        c0 = g * POOL_GROUP_DIM
        c1 = c0 + POOL_GROUP_DIM
        zg = zs_ref[POOL_HALO:POOL_HALO + ts, c0:c1]
        acc = zg
        for k in range(1, w):
            acc = acc + zs_ref[POOL_HALO - k:POOL_HALO - k + ts, c0:c1]
        cnt = jnp.minimum(pos + 1, w).astype(_F32)
        pooled = acc / cnt - zg
        yg = _dot(pooled.astype(_BF16), wgrp_ref[g]) * scale_ref[:, c0:c1]
        y_ref[:, c0:c1] = yg.astype(_BF16)
    zs_ref[0:POOL_HALO, :] = zs_ref[ts:ts + POOL_HALO, :]
    mix = _dot(y_ref[...], wout_ref[...])
    out = _layer_norm(DEEPNORM_ALPHA * x + mix, g_ref[...], b_ref[...])
    o_ref[0] = out
    _store_pieces(op_ref, _pack_halves(out))


def _pool_layer(x, w_in, w_grp, scale, w_out, ln_g, ln_b):
    bsz, seq, d = x.shape
    ts = POOL_TILE
    steps = seq // ts
    const2 = lambda b, s: (0, 0)
    out_shape = (jax.ShapeDtypeStruct((bsz, seq, d), _F32),
                 jax.ShapeDtypeStruct((PIECES, bsz * seq, LANES), _U32))
    tile = pl.BlockSpec((1, ts, d), lambda b, s: (b, s, 0))
    ptile = pl.BlockSpec((PIECES, ts, LANES), lambda b, s: (0, b * steps + s, 0))
    return pl.pallas_call(
        _pool_kernel,
        grid=(bsz, steps),
        in_specs=[
            tile,
            pl.BlockSpec((d, d), const2),
            pl.BlockSpec((len(POOL_WINDOWS), POOL_GROUP_DIM, POOL_GROUP_DIM), lambda b, s: (0, 0, 0)),
            pl.BlockSpec((1, d), const2),
            pl.BlockSpec((d, d), const2),
            pl.BlockSpec((1, d), const2),
            pl.BlockSpec((1, d), const2),
        ],
        out_specs=(tile, ptile),
        out_shape=out_shape,
        scratch_shapes=[pltpu.VMEM((POOL_HALO + ts, d), _F32),
                        pltpu.VMEM((ts, d), _BF16)],
        compiler_params=pltpu.CompilerParams(
            dimension_semantics=("arbitrary", "arbitrary"),
            vmem_limit_bytes=VMEM_LIMIT),
        name="pool_mixer",
    )(x, w_in.astype(_BF16), w_grp.astype(_BF16), scale.reshape(1, d),
      w_out.astype(_BF16), ln_g.reshape(1, d), ln_b.reshape(1, d))


def _sgu_kernel(x_ref, win_ref, bin_ref, lng_ref, lnb_ref, ws_ref, bs_ref, wout_ref,
                g_ref, b_ref, o_ref, op_ref, gated_ref):
    ts = x_ref.shape[0]
    x = x_ref[...]
    xb = x.astype(_BF16)
    v = _gelu_tanh(_dot(xb, win_ref[:, SGU_WIDTH:]) + bin_ref[:, SGU_WIDTH:])
    v = _layer_norm(v, lng_ref[...], lnb_ref[...]).astype(_BF16)
    u = _gelu_tanh(_dot(xb, win_ref[:, :SGU_WIDTH]) + bin_ref[:, :SGU_WIDTH])
    for c in range(ts // SGU_CHUNK):
        r0 = c * SGU_CHUNK
        for h in range(SGU_HEADS):
            c0 = h * SGU_HEAD_DIM
            sv = _dot(ws_ref[h], v[r0:r0 + SGU_CHUNK, c0:c0 + SGU_HEAD_DIM]) + bs_ref[h]
            gated_ref[r0:r0 + SGU_CHUNK, c0:c0 + SGU_HEAD_DIM] = (
                u[r0:r0 + SGU_CHUNK, c0:c0 + SGU_HEAD_DIM] * sv).astype(_BF16)
    mix = _dot(gated_ref[...], wout_ref[...])
    out = _layer_norm(DEEPNORM_ALPHA * x + mix, g_ref[...], b_ref[...])
    o_ref[...] = out
    _store_pieces(op_ref, _pack_halves(out))


def _sgu_layer(x, w_in, b_in, ln_g, ln_b, w_s, b_s, w_out, mix_g, mix_b):
    n_tok, d = x.shape
    ts = SGU_TILE
    const2 = lambda i: (0, 0)
    const3 = lambda i: (0, 0, 0)
    causal = jnp.tril(jnp.ones((SGU_CHUNK, SGU_CHUNK), w_s.dtype))
    ws = (w_s * causal[None]).astype(_BF16)
    tile = pl.BlockSpec((ts, d), lambda i: (i, 0))
    ptile = pl.BlockSpec((PIECES, ts, LANES), lambda i: (0, i, 0))
    out_shape = (jax.ShapeDtypeStruct((n_tok, d), _F32),
                 jax.ShapeDtypeStruct((PIECES, n_tok, LANES), _U32))
    return pl.pallas_call(
        _sgu_kernel,
        grid=(n_tok // ts,),
        in_specs=[
            tile,
            pl.BlockSpec((d, 2 * SGU_WIDTH), const2),
            pl.BlockSpec((1, 2 * SGU_WIDTH), const2),
            pl.BlockSpec((1, SGU_WIDTH), const2),
            pl.BlockSpec((1, SGU_WIDTH), const2),
            pl.BlockSpec((SGU_HEADS, SGU_CHUNK, SGU_CHUNK), const3),
            pl.BlockSpec((SGU_HEADS, SGU_CHUNK, 1), const3),
            pl.BlockSpec((SGU_WIDTH, d), const2),
            pl.BlockSpec((1, d), const2),
            pl.BlockSpec((1, d), const2),
        ],
        out_specs=(tile, ptile),
        out_shape=out_shape,
        scratch_shapes=[pltpu.VMEM((ts, SGU_WIDTH), _BF16)],
        compiler_params=pltpu.CompilerParams(
            dimension_semantics=("arbitrary",),
            vmem_limit_bytes=VMEM_LIMIT),
        name="sgu_mixer",
    )(x, w_in.astype(_BF16), b_in.reshape(1, -1), ln_g.reshape(1, -1), ln_b.reshape(1, -1),
      ws, b_s.reshape(SGU_HEADS, SGU_CHUNK, 1), w_out.astype(_BF16),
      mix_g.reshape(1, d), mix_b.reshape(1, d))


def _route_kernel(x_ref, wrt_ref, bias_ref, ek_ref, rk_ref, gk_ref, cnt_ref, carry_ref):
    i = pl.program_id(0)
    ts = x_ref.shape[0]

    @pl.when(i == 0)
    def _():
        carry_ref[...] = jnp.zeros_like(carry_ref)

    logits = lax.dot_general(wrt_ref[...], x_ref[...], (((1,), (1,)), ((), ())),
                             precision=lax.Precision.HIGHEST,
                             preferred_element_type=_F32)
    scores = jax.nn.sigmoid(logits).reshape(N_GROUPS, GROUP_SIZE, ts)
    biased = scores + bias_ref[...]
    neg_inf = jnp.float32(-jnp.inf)
    shape3 = (N_GROUPS, GROUP_SIZE, ts)
    in_grp = lax.broadcasted_iota(jnp.int32, shape3, 1)
    grp = lax.broadcasted_iota(jnp.int32, shape3, 0)
    eid = grp * GROUP_SIZE + in_grp

    m1 = jnp.max(biased, axis=1, keepdims=True)
    first1 = jnp.min(jnp.where(biased == m1, in_grp, GROUP_SIZE), axis=1, keepdims=True)
    m2 = jnp.max(jnp.where(in_grp == first1, neg_inf, biased), axis=1, keepdims=True)
    gscore = m1 + m2

    gid = lax.broadcasted_iota(jnp.int32, (N_GROUPS, 1, ts), 0)
    gsel = jnp.zeros((N_GROUPS, 1, ts), jnp.bool_)
    for _ in range(TOPK_GROUPS):
        m = jnp.max(gscore, axis=0, keepdims=True)
        first = jnp.min(jnp.where(gscore == m, gid, N_GROUPS), axis=0, keepdims=True)
        pick = gid == first
        gsel = jnp.logical_or(gsel, pick)
        gscore = jnp.where(pick, neg_inf, gscore)

    masked = jnp.where(gsel, biased, neg_inf)
    picked_any = jnp.zeros(shape3, jnp.bool_)
    e_k, s_k = [], []
    for _ in range(TOP_K):
        m = jnp.max(jnp.max(masked, axis=1, keepdims=True), axis=0, keepdims=True)
        first = jnp.min(jnp.min(jnp.where(masked == m, eid, N_EXPERTS), axis=1, keepdims=True),
                        axis=0, keepdims=True)
        pick = eid == first
        picked_any = jnp.logical_or(picked_any, pick)
        masked = jnp.where(pick, neg_inf, masked)
        sc = jnp.sum(jnp.sum(jnp.where(pick, scores, 0.0), axis=1, keepdims=True),
                     axis=0, keepdims=True)
        e_k.append(first)
        s_k.append(sc)
    denom = s_k[0]
    for k in range(1, TOP_K):
        denom = denom + s_k[k]

    sel = picked_any.astype(_F32).reshape(N_EXPERTS, ts)
    row = lax.broadcasted_iota(jnp.int32, (ts, ts), 0)
    col = lax.broadcasted_iota(jnp.int32, (ts, ts), 1)
    earlier = (row < col).astype(_BF16)
    rank = _dot(sel.astype(_BF16), earlier) + carry_ref[...]
    rank3 = rank.reshape(shape3)
    carry_ref[...] += jnp.sum(sel, axis=1, keepdims=True)
    cnt_ref[...] = carry_ref[...]

    for k in range(TOP_K):
        r = jnp.sum(jnp.sum(jnp.where(eid == e_k[k], rank3, 0.0), axis=1, keepdims=True),
                    axis=0, keepdims=True)
        ek_ref[k:k + 1, :] = e_k[k].reshape(1, ts)
        rk_ref[k:k + 1, :] = r.reshape(1, ts).astype(jnp.int32)
        gk_ref[k:k + 1, :] = (s_k[k] / denom * ROUTED_SCALE).reshape(1, ts)


def _route(x, w_router, router_bias):
    n_tok, d = x.shape
    ts = ROUTE_TILE
    out_shape = (jax.ShapeDtypeStruct((TOP_K, n_tok), jnp.int32),
                 jax.ShapeDtypeStruct((TOP_K, n_tok), jnp.int32),
                 jax.ShapeDtypeStruct((TOP_K, n_tok), _F32),
                 jax.ShapeDtypeStruct((N_EXPERTS, 1), _F32))
    kspec = pl.BlockSpec((TOP_K, ts), lambda i: (0, i))
    return pl.pallas_call(
        _route_kernel,
        grid=(n_tok // ts,),
        in_specs=[
            pl.BlockSpec((ts, d), lambda i: (i, 0)),
            pl.BlockSpec((N_EXPERTS, d), lambda i: (0, 0)),
            pl.BlockSpec((N_GROUPS, GROUP_SIZE, 1), lambda i: (0, 0, 0)),
        ],
        out_specs=(kspec, kspec, kspec, pl.BlockSpec((N_EXPERTS, 1), lambda i: (0, 0))),
        out_shape=out_shape,
        scratch_shapes=[pltpu.VMEM((N_EXPERTS, 1), _F32)],
        compiler_params=pltpu.CompilerParams(
            dimension_semantics=("arbitrary",),
            vmem_limit_bytes=VMEM_LIMIT),
        name="moe_route",
    )(x, w_router.T, router_bias.reshape(N_GROUPS, GROUP_SIZE, 1))


def _sc_mesh():
    return plsc.VectorSubcoreMesh(core_axis_name="core", subcore_axis_name="subcore")


def _sc_scatter_rows(src, idx, n_out):
    d = src.shape[1]
    n_tok = idx.shape[1]
    wins = n_tok // SC_WINDOW

    def body(src_hbm, idx_hbm, out_hbm):
        def step(src_vmem, idx_vmem):
            for k in range(TOP_K):
                pltpu.sync_copy(src_vmem, out_hbm.at[idx_vmem.at[k]])

        pltpu.emit_pipeline(
            step,
            grid=(src.shape[0] // SC_WINDOW,),
            in_specs=[pl.BlockSpec((SC_WINDOW, d), index_map=lambda i: (i, 0)),
                      pl.BlockSpec((TOP_K, SC_WINDOW), index_map=lambda i: (i // wins, i % wins))],
            out_specs=[],
            core_axis_name=("core", "subcore"),
            dimension_semantics=(pltpu.PARALLEL,),
        )(src_hbm, idx_hbm)

    return pl.kernel(body, out_type=jax.ShapeDtypeStruct((n_out, d), src.dtype),
                     mesh=_sc_mesh(), scratch_types=[], name="sc_scatter_rows")(src, idx)


def _sc_gather_rows(table, idx):
    d = table.shape[1]
    n_idx_rows, n_tok = idx.shape
    wins = n_tok // SC_WINDOW

    def body(table_hbm, idx_hbm, out_hbm):
        def step(idx_vmem, out_vmem):
            pltpu.sync_copy(table_hbm.at[idx_vmem.at[0]], out_vmem)

        pltpu.emit_pipeline(
            step,
            grid=(n_idx_rows * wins,),
            in_specs=[pl.BlockSpec((1, SC_WINDOW), index_map=lambda i: (i // wins, i % wins))],
            out_specs=[pl.BlockSpec((SC_WINDOW, d), index_map=lambda i: (i, 0))],
            core_axis_name=("core", "subcore"),
            dimension_semantics=(pltpu.PARALLEL,),
        )(idx_hbm, out_hbm)

    return pl.kernel(body, out_type=jax.ShapeDtypeStruct((n_idx_rows * n_tok, d), table.dtype),
                     mesh=_sc_mesh(), scratch_types=[], name="sc_gather_rows")(table, idx)


def _index_kernel(start_ref, ek_ref, rk_ref, idx_ref, *, n_rows):
    e = ek_ref[...]
    start = jnp.zeros(e.shape, jnp.int32)
    for ex in range(N_EXPERTS):
        start = jnp.where(e == ex, start_ref[ex], start)
    dest = start + rk_ref[...]
    for c in range(PIECES):
        idx_ref[c] = dest + c * n_rows


def _row_indices(padded_start, e_k, r_k, n_rows):
    n_tok = e_k.shape[1]
    ts = INDEX_TILE
    kspec = pl.BlockSpec((TOP_K, ts), lambda i, st: (0, i))
    grid_spec = pltpu.PrefetchScalarGridSpec(
        num_scalar_prefetch=1,
        grid=(n_tok // ts,),
        in_specs=[kspec, kspec],
        out_specs=pl.BlockSpec((PIECES, TOP_K, ts), lambda i, st: (0, 0, i)),
    )
    return pl.pallas_call(
        functools.partial(_index_kernel, n_rows=n_rows),
        grid_spec=grid_spec,
        out_shape=jax.ShapeDtypeStruct((PIECES, TOP_K, n_tok), jnp.int32),
        compiler_params=pltpu.CompilerParams(dimension_semantics=("arbitrary",)),
        name="moe_row_indices",
    )(padded_start, e_k, r_k)


def _expert_kernel(chunk0_ref, nchunk_ref, cnt_ref, total_ref,
                   xs_hbm, wg_ref, wu_ref, wd_ref, o_hbm,
                   xbuf, obuf, wgu_bf, wd_bf, sem_in, sem_out):
    e = pl.program_id(0)
    total = total_ref[0]
    ch = EXPERT_ROWS

    def in_copy(g, slot):
        return pltpu.make_async_copy(xs_hbm.at[:, pl.ds(g * ch, ch), :], xbuf.at[slot],
                                     sem_in.at[slot])

    def out_copy(g, slot):
        return pltpu.make_async_copy(obuf.at[slot], o_hbm.at[:, pl.ds(g * ch, ch), :],
                                     sem_out.at[slot])

    @pl.when(e == 0)
    def _():
        obuf[...] = jnp.zeros_like(obuf)

        @pl.when(total > 0)
        def _():
            in_copy(0, 0).start()

    wgu_bf[:, :EXPERT_DIM] = wg_ref[0, 0].astype(_BF16)
    wgu_bf[:, EXPERT_DIM:] = wu_ref[0, 0].astype(_BF16)
    wd_bf[...] = wd_ref[0, 0].astype(_BF16)

    def chunk(j, carry):
        g = chunk0_ref[e] + j
        slot = g % 2
        in_copy(g, slot).wait()

        @pl.when(g + 1 < total)
        def _():
            in_copy(g + 1, 1 - slot).start()

        @pl.when(g >= 2)
        def _():
            out_copy(g - 2, slot).wait()

        valid = cnt_ref[e] - j * ch
        for s in range(ch // EXPERT_SUB):
            r0 = s * EXPERT_SUB

            @pl.when(r0 < valid)
            def _():
                w = jnp.concatenate([xbuf[slot, c, r0:r0 + EXPERT_SUB, :] for c in range(PIECES)],
                                    axis=1)
                lo, hi = _unpack_halves(w)
                gu = (_dot(lo.astype(_BF16), wgu_bf[:PACKED]) +
                      _dot(hi.astype(_BF16), wgu_bf[PACKED:]))
                h = _silu(gu[:, :EXPERT_DIM]) * gu[:, EXPERT_DIM:]
                y = _pack_halves(_dot(h.astype(_BF16), wd_bf[...]))
                for c in range(PIECES):
                    obuf[slot, c, r0:r0 + EXPERT_SUB, :] = y[:, c * LANES:(c + 1) * LANES]

        out_copy(g, slot).start()
        return carry

    lax.fori_loop(0, nchunk_ref[e], chunk, 0)

    @pl.when(e == pl.num_programs(0) - 1)
    def _():
        for back in (2, 1):
            @pl.when(total >= back)
            def _():
                g = total - back
                out_copy(g, g % 2).wait()


def _experts(layer, chunk0, nchunk, counts, total, xs, w_gate, w_up, w_down):
    d = D_MODEL
    ch = EXPERT_ROWS
    wspec_in = pl.BlockSpec((1, 1, d, EXPERT_DIM), lambda e, *_: (layer, e, 0, 0))
    any_spec = pl.BlockSpec(memory_space=pl.ANY)
    grid_spec = pltpu.PrefetchScalarGridSpec(
        num_scalar_prefetch=4,
        grid=(N_EXPERTS,),
        in_specs=[
            any_spec,
            wspec_in,
            wspec_in,
            pl.BlockSpec((1, 1, EXPERT_DIM, d), lambda e, *_: (layer, e, 0, 0)),
        ],
        out_specs=any_spec,
        scratch_shapes=[pltpu.VMEM((2, PIECES, ch, LANES), _U32),
                        pltpu.VMEM((2, PIECES, ch, LANES), _U32),
                        pltpu.VMEM((d, 2 * EXPERT_DIM), _BF16),
                        pltpu.VMEM((EXPERT_DIM, d), _BF16),
                        pltpu.SemaphoreType.DMA((2,)),
                        pltpu.SemaphoreType.DMA((2,))],
    )
    return pl.pallas_call(
        _expert_kernel,
        grid_spec=grid_spec,
        out_shape=jax.ShapeDtypeStruct(xs.shape, _U32),
        compiler_params=pltpu.CompilerParams(
            dimension_semantics=("arbitrary",),
            vmem_limit_bytes=VMEM_LIMIT),
        name="moe_experts",
    )(chunk0, nchunk, counts, total, xs, w_gate, w_up, w_down)


def _combine_kernel(x_ref, p_ref, gate_ref, sg_ref, su_ref, sd_ref, g_ref, b_ref, o_ref):
    x = x_ref[...]
    xb = x.astype(_BF16)
    gates = gate_ref[...]
    r_lo = r_hi = None
    for k in range(TOP_K):
        w = jnp.concatenate([p_ref[c, k] for c in range(PIECES)], axis=1)
        lo, hi = _unpack_halves(w)
        gk = gates[:, k:k + 1]
        r_lo = gk * lo if r_lo is None else r_lo + gk * lo
        r_hi = gk * hi if r_hi is None else r_hi + gk * hi
    routed = jnp.concatenate([r_lo, r_hi], axis=1)
    h = _silu(_dot(xb, sg_ref[...])) * _dot(xb, su_ref[...])
    shared = _dot(h.astype(_BF16), sd_ref[...])
    o_ref[...] = _layer_norm(DEEPNORM_ALPHA * x + (routed + shared), g_ref[...], b_ref[...])


def _combine(x, picked, gates, sh_gate, sh_up, sh_down, ln_g, ln_b):
    n_tok, d = x.shape
    ts = COMBINE_TILE
    const2 = lambda i: (0, 0)
    return pl.pallas_call(
        _combine_kernel,
        grid=(n_tok // ts,),
        in_specs=[
            pl.BlockSpec((ts, d), lambda i: (i, 0)),
            pl.BlockSpec((PIECES, TOP_K, ts, LANES), lambda i: (0, 0, i, 0)),
            pl.BlockSpec((ts, TOP_K), lambda i: (i, 0)),
            pl.BlockSpec((d, EXPERT_DIM), const2),
            pl.BlockSpec((d, EXPERT_DIM), const2),
            pl.BlockSpec((EXPERT_DIM, d), const2),
            pl.BlockSpec((1, d), const2),
            pl.BlockSpec((1, d), const2),
        ],
        out_specs=pl.BlockSpec((ts, d), lambda i: (i, 0)),
        out_shape=jax.ShapeDtypeStruct((n_tok, d), _F32),
        compiler_params=pltpu.CompilerParams(
            dimension_semantics=("arbitrary",),
            vmem_limit_bytes=VMEM_LIMIT),
        name="moe_combine",
    )(x, picked, gates, sh_gate.astype(_BF16), sh_up.astype(_BF16), sh_down.astype(_BF16),
      ln_g.reshape(1, d), ln_b.reshape(1, d))


def _moe_layer(layer, x, x_packed, w_router, router_bias, w_gate, w_up, w_down,
               sh_gate, sh_up, sh_down, ln_g, ln_b):
    n_tok, d = x.shape
    bm = EXPERT_ROWS
    e_k, r_k, g_k, counts = _route(x, w_router, router_bias)
    counts = counts.reshape(N_EXPERTS).astype(jnp.int32)
    padded = (counts + bm - 1) // bm * bm
    padded_end = jnp.cumsum(padded)
    padded_start = padded_end - padded
    n_rows = n_tok * TOP_K + N_EXPERTS * bm

    idx = _row_indices(padded_start, e_k, r_k, n_rows).reshape(PIECES * TOP_K, n_tok)
    xs = _sc_scatter_rows(x_packed.reshape(PIECES * n_tok, LANES), idx, PIECES * n_rows)
    rows = _experts(layer, padded_start // bm, padded // bm, counts,
                    (padded_end[-1:] // bm), xs.reshape(PIECES, n_rows, LANES),
                    w_gate, w_up, w_down)
    picked = _sc_gather_rows(rows.reshape(PIECES * n_rows, LANES), idx)
    picked = picked.reshape(PIECES, TOP_K, n_tok, LANES)
    return _combine(x, picked, g_k.T, sh_gate, sh_up, sh_down, ln_g, ln_b)


def kernel(x, pool_w_in, pool_w_grp, pool_scale, pool_w_out, sgu_w_in, sgu_b_in, sgu_ln_g, sgu_ln_b, sgu_w_s, sgu_b_s, sgu_w_out, ln_mix_g, ln_mix_b, moe_w_router, moe_router_bias, moe_w_gate, moe_w_up, moe_w_down, moe_sh_gate, moe_sh_up, moe_sh_down, ln_ffn_g, ln_ffn_b):
    bsz, seq, d = x.shape
    n_tok = bsz * seq

    def moe(i, h, h_packed):
        return _moe_layer(i, h, h_packed, moe_w_router[i], moe_router_bias[i], moe_w_gate,
                          moe_w_up, moe_w_down, moe_sh_gate[i], moe_sh_up[i],
                          moe_sh_down[i], ln_ffn_g[i], ln_ffn_b[i])

    h, h_packed = _pool_layer(x, pool_w_in[0], pool_w_grp[0], pool_scale[0], pool_w_out[0],
                              ln_mix_g[0], ln_mix_b[0])
    h = moe(0, h.reshape(n_tok, d), h_packed)
    h, h_packed = _sgu_layer(h, sgu_w_in[0], sgu_b_in[0], sgu_ln_g[0], sgu_ln_b[0], sgu_w_s[0],
                             sgu_b_s[0], sgu_w_out[0], ln_mix_g[1], ln_mix_b[1])
    h = moe(1, h, h_packed)
    return h.reshape(bsz, seq, d)
```

```python
import functools

import jax
import jax.numpy as jnp
from jax import lax
from jax.experimental import pallas as pl
from jax.experimental.pallas import tpu as pltpu
from jax.experimental.pallas import tpu_sc as plsc

D_MODEL = 1024
DEPTH = 2
POOL_WINDOWS = (2, 4, 8, 16)
POOL_GROUP_DIM = D_MODEL // len(POOL_WINDOWS)
POOL_HALO = 16
SGU_CHUNK = 128
SGU_HEADS = 4
SGU_WIDTH = 2 * D_MODEL
SGU_HEAD_DIM = SGU_WIDTH // SGU_HEADS
N_EXPERTS = 64
TOP_K = 8
N_GROUPS = 8
GROUP_SIZE = N_EXPERTS // N_GROUPS
TOPK_GROUPS = 4
EXPERT_DIM = D_MODEL // 4
ROUTED_SCALE = 2.5
DEEPNORM_ALPHA = (2 * DEPTH) ** 0.25
LN_EPS = 1e-5

LANES = 128
PACKED = D_MODEL // 2
PIECES = PACKED // LANES

POOL_TILE = 512
SGU_TILE = 256
ROUTE_TILE = 512
INDEX_TILE = 2048
EXPERT_ROWS = 512
EXPERT_SUB = 256
COMBINE_TILE = 256
SC_WINDOW = 128
VMEM_LIMIT = 56 * 1024 * 1024

_F32 = jnp.float32
_BF16 = jnp.bfloat16
_U32 = jnp.uint32


def _dot(a, b):
    return jnp.dot(a, b, preferred_element_type=_F32)


def _layer_norm(h, g, b):
    mu = jnp.mean(h, axis=-1, keepdims=True)
    hc = h - mu
    var = jnp.mean(hc * hc, axis=-1, keepdims=True)
    return hc * lax.rsqrt(var + LN_EPS) * g + b


def _silu(x):
    return x * jax.nn.sigmoid(x)


def _gelu_tanh(x):
    c = 0.7978845608028654
    return 0.5 * x * (1.0 + jnp.tanh(c * (x + 0.044715 * (x * x * x))))


def _pack_halves(v):
    half = v.shape[1] // 2
    lo = lax.bitcast_convert_type(v[:, :half].astype(_BF16).astype(_F32), _U32)
    hi = lax.bitcast_convert_type(v[:, half:].astype(_BF16).astype(_F32), _U32)
    return (hi & _U32(0xFFFF0000)) | (lo >> 16)


def _unpack_halves(w):
    lo = lax.bitcast_convert_type(w << 16, _F32)
    hi = lax.bitcast_convert_type(w & _U32(0xFFFF0000), _F32)
    return lo, hi


def _store_pieces(ref, w):
    for c in range(PIECES):
        ref[c] = w[:, c * LANES:(c + 1) * LANES]


def _load_pieces(ref):
    return jnp.concatenate([ref[c] for c in range(PIECES)], axis=1)


def _pool_kernel(x_ref, win_ref, wgrp_ref, scale_ref, wout_ref, g_ref, b_ref,
                 o_ref, op_ref, zs_ref, y_ref):
    s = pl.program_id(1)
    ts = x_ref.shape[1]
    x = x_ref[0]
    z = _dot(x.astype(_BF16), win_ref[...])

    @pl.when(s == 0)
    def _():
        zs_ref[0:POOL_HALO, :] = jnp.zeros((POOL_HALO, D_MODEL), _F32)

    zs_ref[POOL_HALO:POOL_HALO + ts, :] = z
    pos = s * ts + lax.broadcasted_iota(jnp.int32, (ts, 1), 0)
    for g, w in enumerate(POOL_WINDOWS):
        c0 = g * POOL_GROUP_DIM
        c1 = c0 + POOL_GROUP_DIM
        zg = zs_ref[POOL_HALO:POOL_HALO + ts, c0:c1]
        acc = zg
        for k in range(1, w):
            acc = acc + zs_ref[POOL_HALO - k:POOL_HALO - k + ts, c0:c1]
        cnt = jnp.minimum(pos + 1, w).astype(_F32)
        pooled = acc / cnt - zg
        yg = _dot(pooled.astype(_BF16), wgrp_ref[g]) * scale_ref[:, c0:c1]
        y_ref[:, c0:c1] = yg.astype(_BF16)
    zs_ref[0:POOL_HALO, :] = zs_ref[ts:ts + POOL_HALO, :]
    mix = _dot(y_ref[...], wout_ref[...])
    out = _layer_norm(DEEPNORM_ALPHA * x + mix, g_ref[...], b_ref[...])
    o_ref[0] = out
    _store_pieces(op_ref, _pack_halves(out))


def _pool_layer(x, w_in, w_grp, scale, w_out, ln_g, ln_b):
    bsz, seq, d = x.shape
    ts = POOL_TILE
    steps = seq // ts
    const2 = lambda b, s: (0, 0)
    out_shape = (jax.ShapeDtypeStruct((bsz, seq, d), _F32),
                 jax.ShapeDtypeStruct((PIECES, bsz * seq, LANES), _U32))
    tile = pl.BlockSpec((1, ts, d), lambda b, s: (b, s, 0))
    ptile = pl.BlockSpec((PIECES, ts, LANES), lambda b, s: (0, b * steps + s, 0))
    return pl.pallas_call(
        _pool_kernel,
        grid=(bsz, steps),
        in_specs=[
            tile,
            pl.BlockSpec((d, d), const2),
            pl.BlockSpec((len(POOL_WINDOWS), POOL_GROUP_DIM, POOL_GROUP_DIM), lambda b, s: (0, 0, 0)),
            pl.BlockSpec((1, d), const2),
            pl.BlockSpec((d, d), const2),
            pl.BlockSpec((1, d), const2),
            pl.BlockSpec((1, d), const2),
        ],
        out_specs=(tile, ptile),
        out_shape=out_shape,
        scratch_shapes=[pltpu.VMEM((POOL_HALO + ts, d), _F32),
                        pltpu.VMEM((ts, d), _BF16)],
        compiler_params=pltpu.CompilerParams(
            dimension_semantics=("arbitrary", "arbitrary"),
            vmem_limit_bytes=VMEM_LIMIT),
        name="pool_mixer",
    )(x, w_in.astype(_BF16), w_grp.astype(_BF16), scale.reshape(1, d),
      w_out.astype(_BF16), ln_g.reshape(1, d), ln_b.reshape(1, d))


def _sgu_kernel(x_ref, win_ref, bin_ref, lng_ref, lnb_ref, ws_ref, bs_ref, wout_ref,
                g_ref, b_ref, o_ref, op_ref, gated_ref):
    ts = x_ref.shape[0]
    x = x_ref[...]
    xb = x.astype(_BF16)
    v = _gelu_tanh(_dot(xb, win_ref[:, SGU_WIDTH:]) + bin_ref[:, SGU_WIDTH:])
    v = _layer_norm(v, lng_ref[...], lnb_ref[...]).astype(_BF16)
    u = _gelu_tanh(_dot(xb, win_ref[:, :SGU_WIDTH]) + bin_ref[:, :SGU_WIDTH])
    for c in range(ts // SGU_CHUNK):
        r0 = c * SGU_CHUNK
        for h in range(SGU_HEADS):
            c0 = h * SGU_HEAD_DIM
            sv = _dot(ws_ref[h], v[r0:r0 + SGU_CHUNK, c0:c0 + SGU_HEAD_DIM]) + bs_ref[h]
            gated_ref[r0:r0 + SGU_CHUNK, c0:c0 + SGU_HEAD_DIM] = (
                u[r0:r0 + SGU_CHUNK, c0:c0 + SGU_HEAD_DIM] * sv).astype(_BF16)
    mix = _dot(gated_ref[...], wout_ref[...])
    out = _layer_norm(DEEPNORM_ALPHA * x + mix, g_ref[...], b_ref[...])
    o_ref[...] = out
    _store_pieces(op_ref, _pack_halves(out))


def _sgu_layer(x, w_in, b_in, ln_g, ln_b, w_s, b_s, w_out, mix_g, mix_b):
    n_tok, d = x.shape
    ts = SGU_TILE
    const2 = lambda i: (0, 0)
    const3 = lambda i: (0, 0, 0)
    causal = jnp.tril(jnp.ones((SGU_CHUNK, SGU_CHUNK), w_s.dtype))
    ws = (w_s * causal[None]).astype(_BF16)
    tile = pl.BlockSpec((ts, d), lambda i: (i, 0))
    ptile = pl.BlockSpec((PIECES, ts, LANES), lambda i: (0, i, 0))
    out_shape = (jax.ShapeDtypeStruct((n_tok, d), _F32),
                 jax.ShapeDtypeStruct((PIECES, n_tok, LANES), _U32))
    return pl.pallas_call(
        _sgu_kernel,
        grid=(n_tok // ts,),
        in_specs=[
            tile,
            pl.BlockSpec((d, 2 * SGU_WIDTH), const2),
            pl.BlockSpec((1, 2 * SGU_WIDTH), const2),
            pl.BlockSpec((1, SGU_WIDTH), const2),
            pl.BlockSpec((1, SGU_WIDTH), const2),
            pl.BlockSpec((SGU_HEADS, SGU_CHUNK, SGU_CHUNK), const3),
            pl.BlockSpec((SGU_HEADS, SGU_CHUNK, 1), const3),
            pl.BlockSpec((SGU_WIDTH, d), const2),
            pl.BlockSpec((1, d), const2),
            pl.BlockSpec((1, d), const2),
        ],
        out_specs=(tile, ptile),
        out_shape=out_shape,
        scratch_shapes=[pltpu.VMEM((ts, SGU_WIDTH), _BF16)],
        compiler_params=pltpu.CompilerParams(
            dimension_semantics=("arbitrary",),
            vmem_limit_bytes=VMEM_LIMIT),
        name="sgu_mixer",
    )(x, w_in.astype(_BF16), b_in.reshape(1, -1), ln_g.reshape(1, -1), ln_b.reshape(1, -1),
      ws, b_s.reshape(SGU_HEADS, SGU_CHUNK, 1), w_out.astype(_BF16),
      mix_g.reshape(1, d), mix_b.reshape(1, d))


def _route_kernel(x_ref, wrt_ref, bias_ref, ek_ref, rk_ref, gk_ref, cnt_ref, carry_ref):
    i = pl.program_id(0)
    ts = x_ref.shape[0]

    @pl.when(i == 0)
    def _():
        carry_ref[...] = jnp.zeros_like(carry_ref)

    logits = lax.dot_general(wrt_ref[...], x_ref[...], (((1,), (1,)), ((), ())),
                             precision=lax.Precision.HIGHEST,
                             preferred_element_type=_F32)
    scores = jax.nn.sigmoid(logits).reshape(N_GROUPS, GROUP_SIZE, ts)
    biased = scores + bias_ref[...]
    neg_inf = jnp.float32(-jnp.inf)
    shape3 = (N_GROUPS, GROUP_SIZE, ts)
    in_grp = lax.broadcasted_iota(jnp.int32, shape3, 1)
    grp = lax.broadcasted_iota(jnp.int32, shape3, 0)
    eid = grp * GROUP_SIZE + in_grp

    m1 = jnp.max(biased, axis=1, keepdims=True)
    first1 = jnp.min(jnp.where(biased == m1, in_grp, GROUP_SIZE), axis=1, keepdims=True)
    m2 = jnp.max(jnp.where(in_grp == first1, neg_inf, biased), axis=1, keepdims=True)
    gscore = m1 + m2

    gid = lax.broadcasted_iota(jnp.int32, (N_GROUPS, 1, ts), 0)
    gsel = jnp.zeros((N_GROUPS, 1, ts), jnp.bool_)
    for _ in range(TOPK_GROUPS):
        m = jnp.max(gscore, axis=0, keepdims=True)
        first = jnp.min(jnp.where(gscore == m, gid, N_GROUPS), axis=0, keepdims=True)
        pick = gid == first
        gsel = jnp.logical_or(gsel, pick)
        gscore = jnp.where(pick, neg_inf, gscore)

    masked = jnp.where(gsel, biased, neg_inf)
    picked_any = jnp.zeros(shape3, jnp.bool_)
    e_k, s_k = [], []
    for _ in range(TOP_K):
        m = jnp.max(jnp.max(masked, axis=1, keepdims=True), axis=0, keepdims=True)
        first = jnp.min(jnp.min(jnp.where(masked == m, eid, N_EXPERTS), axis=1, keepdims=True),
                        axis=0, keepdims=True)
        pick = eid == first
        picked_any = jnp.logical_or(picked_any, pick)
        masked = jnp.where(pick, neg_inf, masked)
        sc = jnp.sum(jnp.sum(jnp.where(pick, scores, 0.0), axis=1, keepdims=True),
                     axis=0, keepdims=True)
        e_k.append(first)
        s_k.append(sc)
    denom = s_k[0]
    for k in range(1, TOP_K):
        denom = denom + s_k[k]

    sel = picked_any.astype(_F32).reshape(N_EXPERTS, ts)
    row = lax.broadcasted_iota(jnp.int32, (ts, ts), 0)
    col = lax.broadcasted_iota(jnp.int32, (ts, ts), 1)
    earlier = (row < col).astype(_BF16)
    rank = _dot(sel.astype(_BF16), earlier) + carry_ref[...]
    rank3 = rank.reshape(shape3)
    carry_ref[...] += jnp.sum(sel, axis=1, keepdims=True)
    cnt_ref[...] = carry_ref[...]

    for k in range(TOP_K):
        r = jnp.sum(jnp.sum(jnp.where(eid == e_k[k], rank3, 0.0), axis=1, keepdims=True),
                    axis=0, keepdims=True)
        ek_ref[k:k + 1, :] = e_k[k].reshape(1, ts)
        rk_ref[k:k + 1, :] = r.reshape(1, ts).astype(jnp.int32)
        gk_ref[k:k + 1, :] = (s_k[k] / denom * ROUTED_SCALE).reshape(1, ts)


def _route(x, w_router, router_bias):
    n_tok, d = x.shape
    ts = ROUTE_TILE
    out_shape = (jax.ShapeDtypeStruct((TOP_K, n_tok), jnp.int32),
                 jax.ShapeDtypeStruct((TOP_K, n_tok), jnp.int32),
                 jax.ShapeDtypeStruct((TOP_K, n_tok), _F32),
                 jax.ShapeDtypeStruct((N_EXPERTS, 1), _F32))
    kspec = pl.BlockSpec((TOP_K, ts), lambda i: (0, i))
    return pl.pallas_call(
        _route_kernel,
        grid=(n_tok // ts,),
        in_specs=[
            pl.BlockSpec((ts, d), lambda i: (i, 0)),
            pl.BlockSpec((N_EXPERTS, d), lambda i: (0, 0)),
            pl.BlockSpec((N_GROUPS, GROUP_SIZE, 1), lambda i: (0, 0, 0)),
        ],
        out_specs=(kspec, kspec, kspec, pl.BlockSpec((N_EXPERTS, 1), lambda i: (0, 0))),
        out_shape=out_shape,
        scratch_shapes=[pltpu.VMEM((N_EXPERTS, 1), _F32)],
        compiler_params=pltpu.CompilerParams(
            dimension_semantics=("arbitrary",),
            vmem_limit_bytes=VMEM_LIMIT),
        name="moe_route",
    )(x, w_router.T, router_bias.reshape(N_GROUPS, GROUP_SIZE, 1))


def _sc_mesh():
    return plsc.VectorSubcoreMesh(core_axis_name="core", subcore_axis_name="subcore")


def _sc_scatter_rows(src, idx, n_out):
    d = src.shape[1]
    n_tok = idx.shape[1]
    wins = n_tok // SC_WINDOW

    def body(src_hbm, idx_hbm, out_hbm):
        def step(src_vmem, idx_vmem):
            for k in range(TOP_K):
                pltpu.sync_copy(src_vmem, out_hbm.at[idx_vmem.at[k]])

        pltpu.emit_pipeline(
            step,
            grid=(src.shape[0] // SC_WINDOW,),
            in_specs=[pl.BlockSpec((SC_WINDOW, d), index_map=lambda i: (i, 0)),
                      pl.BlockSpec((TOP_K, SC_WINDOW), index_map=lambda i: (i // wins, i % wins))],
            out_specs=[],
            core_axis_name=("core", "subcore"),
            dimension_semantics=(pltpu.PARALLEL,),
        )(src_hbm, idx_hbm)

    return pl.kernel(body, out_type=jax.ShapeDtypeStruct((n_out, d), src.dtype),
                     mesh=_sc_mesh(), scratch_types=[], name="sc_scatter_rows")(src, idx)


def _sc_gather_rows(table, idx):
    d = table.shape[1]
    n_idx_rows, n_tok = idx.shape
    wins = n_tok // SC_WINDOW

    def body(table_hbm, idx_hbm, out_hbm):
        def step(idx_vmem, out_vmem):
            pltpu.sync_copy(table_hbm.at[idx_vmem.at[0]], out_vmem)

        pltpu.emit_pipeline(
            step,
            grid=(n_idx_rows * wins,),
            in_specs=[pl.BlockSpec((1, SC_WINDOW), index_map=lambda i: (i // wins, i % wins))],
            out_specs=[pl.BlockSpec((SC_WINDOW, d), index_map=lambda i: (i, 0))],
            core_axis_name=("core", "subcore"),
            dimension_semantics=(pltpu.PARALLEL,),
        )(idx_hbm, out_hbm)

    return pl.kernel(body, out_type=jax.ShapeDtypeStruct((n_idx_rows * n_tok, d), table.dtype),
                     mesh=_sc_mesh(), scratch_types=[], name="sc_gather_rows")(table, idx)


def _index_kernel(start_ref, ek_ref, rk_ref, idx_ref, *, n_rows):
    e = ek_ref[...]
    start = jnp.zeros(e.shape, jnp.int32)
    for ex in range(N_EXPERTS):
        start = jnp.where(e == ex, start_ref[ex], start)
    dest = start + rk_ref[...]
    for c in range(PIECES):
        idx_ref[c] = dest + c * n_rows


def _row_indices(padded_start, e_k, r_k, n_rows):
    n_tok = e_k.shape[1]
    ts = INDEX_TILE
    kspec = pl.BlockSpec((TOP_K, ts), lambda i, st: (0, i))
    grid_spec = pltpu.PrefetchScalarGridSpec(
        num_scalar_prefetch=1,
        grid=(n_tok // ts,),
        in_specs=[kspec, kspec],
        out_specs=pl.BlockSpec((PIECES, TOP_K, ts), lambda i, st: (0, 0, i)),
    )
    return pl.pallas_call(
        functools.partial(_index_kernel, n_rows=n_rows),
        grid_spec=grid_spec,
        out_shape=jax.ShapeDtypeStruct((PIECES, TOP_K, n_tok), jnp.int32),
        compiler_params=pltpu.CompilerParams(dimension_semantics=("arbitrary",)),
        name="moe_row_indices",
    )(padded_start, e_k, r_k)


def _expert_kernel(chunk0_ref, nchunk_ref, cnt_ref, total_ref,
                   xs_hbm, wg_ref, wu_ref, wd_ref, o_hbm,
                   xbuf, obuf, wgu_bf, wd_bf, sem_in, sem_out):
    e = pl.program_id(0)
    total = total_ref[0]
    ch = EXPERT_ROWS

    class _Copies:
        def __init__(self, copies):
            self.copies = copies

        def start(self):
            for c in self.copies:
                c.start()

        def wait(self):
            for c in self.copies:
                c.wait()

    def in_copy(g, slot):
        return _Copies([pltpu.make_async_copy(xs_hbm.at[c, pl.ds(g * ch, ch), :],
                                              xbuf.at[slot, c], sem_in.at[slot, c])
                        for c in range(PIECES)])

    def out_copy(g, slot):
        return _Copies([pltpu.make_async_copy(obuf.at[slot, c],
                                              o_hbm.at[c, pl.ds(g * ch, ch), :],
                                              sem_out.at[slot, c])
                        for c in range(PIECES)])

    @pl.when(e == 0)
    def _():
        obuf[...] = jnp.zeros_like(obuf)

        @pl.when(total > 0)
        def _():
            in_copy(0, 0).start()

    wgu_bf[:, :EXPERT_DIM] = wg_ref[0, 0].astype(_BF16)
    wgu_bf[:, EXPERT_DIM:] = wu_ref[0, 0].astype(_BF16)
    wd_bf[...] = wd_ref[0, 0].astype(_BF16)

    def chunk(j, carry):
        g = chunk0_ref[e] + j
        slot = g % 2
        in_copy(g, slot).wait()

        @pl.when(g + 1 < total)
        def _():
            in_copy(g + 1, 1 - slot).start()

        @pl.when(g >= 2)
        def _():
            out_copy(g - 2, slot).wait()

        valid = cnt_ref[e] - j * ch
        for s in range(ch // EXPERT_SUB):
            r0 = s * EXPERT_SUB

            @pl.when(r0 < valid)
            def _():
                w = jnp.concatenate([xbuf[slot, c, r0:r0 + EXPERT_SUB, :] for c in range(PIECES)],
                                    axis=1)
                lo, hi = _unpack_halves(w)
                gu = (_dot(lo.astype(_BF16), wgu_bf[:PACKED]) +
                      _dot(hi.astype(_BF16), wgu_bf[PACKED:]))
                h = _silu(gu[:, :EXPERT_DIM]) * gu[:, EXPERT_DIM:]
                y = _pack_halves(_dot(h.astype(_BF16), wd_bf[...]))
                for c in range(PIECES):
                    obuf[slot, c, r0:r0 + EXPERT_SUB, :] = y[:, c * LANES:(c + 1) * LANES]

        out_copy(g, slot).start()
        return carry

    lax.fori_loop(0, nchunk_ref[e], chunk, 0)

    @pl.when(e == pl.num_programs(0) - 1)
    def _():
        for back in (2, 1):
            @pl.when(total >= back)
            def _():
                g = total - back
                out_copy(g, g % 2).wait()


def _experts(layer, chunk0, nchunk, counts, total, xs, w_gate, w_up, w_down):
    d = D_MODEL
    ch = EXPERT_ROWS
    wspec_in = pl.BlockSpec((1, 1, d, EXPERT_DIM), lambda e, *_: (layer, e, 0, 0))
    any_spec = pl.BlockSpec(memory_space=pl.ANY)
    grid_spec = pltpu.PrefetchScalarGridSpec(
        num_scalar_prefetch=4,
        grid=(N_EXPERTS,),
        in_specs=[
            any_spec,
            wspec_in,
            wspec_in,
            pl.BlockSpec((1, 1, EXPERT_DIM, d), lambda e, *_: (layer, e, 0, 0)),
        ],
        out_specs=any_spec,
        scratch_shapes=[pltpu.VMEM((2, PIECES, ch, LANES), _U32),
                        pltpu.VMEM((2, PIECES, ch, LANES), _U32),
                        pltpu.VMEM((d, 2 * EXPERT_DIM), _BF16),
                        pltpu.VMEM((EXPERT_DIM, d), _BF16),
                        pltpu.SemaphoreType.DMA((2, PIECES)),
                        pltpu.SemaphoreType.DMA((2, PIECES))],
    )
    return pl.pallas_call(
        _expert_kernel,
        grid_spec=grid_spec,
        out_shape=jax.ShapeDtypeStruct(xs.shape, _U32),
        compiler_params=pltpu.CompilerParams(
            dimension_semantics=("arbitrary",),
            vmem_limit_bytes=VMEM_LIMIT),
        name="moe_experts",
    )(chunk0, nchunk, counts, total, xs, w_gate, w_up, w_down)


def _combine_kernel(x_ref, p_ref, gate_ref, sg_ref, su_ref, sd_ref, g_ref, b_ref, o_ref):
    x = x_ref[...]
    xb = x.astype(_BF16)
    gates = gate_ref[...]
    r_lo = r_hi = None
    for k in range(TOP_K):
        w = jnp.concatenate([p_ref[c, k] for c in range(PIECES)], axis=1)
        lo, hi = _unpack_halves(w)
        gk = gates[:, k:k + 1]
        r_lo = gk * lo if r_lo is None else r_lo + gk * lo
        r_hi = gk * hi if r_hi is None else r_hi + gk * hi
    routed = jnp.concatenate([r_lo, r_hi], axis=1)
    h = _silu(_dot(xb, sg_ref[...])) * _dot(xb, su_ref[...])
    shared = _dot(h.astype(_BF16), sd_ref[...])
    o_ref[...] = _layer_norm(DEEPNORM_ALPHA * x + (routed + shared), g_ref[...], b_ref[...])


def _combine(x, picked, gates, sh_gate, sh_up, sh_down, ln_g, ln_b):
    n_tok, d = x.shape
    ts = COMBINE_TILE
    const2 = lambda i: (0, 0)
    return pl.pallas_call(
        _combine_kernel,
        grid=(n_tok // ts,),
        in_specs=[
            pl.BlockSpec((ts, d), lambda i: (i, 0)),
            pl.BlockSpec((PIECES, TOP_K, ts, LANES), lambda i: (0, 0, i, 0)),
            pl.BlockSpec((ts, TOP_K), lambda i: (i, 0)),
            pl.BlockSpec((d, EXPERT_DIM), const2),
            pl.BlockSpec((d, EXPERT_DIM), const2),
            pl.BlockSpec((EXPERT_DIM, d), const2),
            pl.BlockSpec((1, d), const2),
            pl.BlockSpec((1, d), const2),
        ],
        out_specs=pl.BlockSpec((ts, d), lambda i: (i, 0)),
        out_shape=jax.ShapeDtypeStruct((n_tok, d), _F32),
        compiler_params=pltpu.CompilerParams(
            dimension_semantics=("arbitrary",),
            vmem_limit_bytes=VMEM_LIMIT),
        name="moe_combine",
    )(x, picked, gates, sh_gate.astype(_BF16), sh_up.astype(_BF16), sh_down.astype(_BF16),
      ln_g.reshape(1, d), ln_b.reshape(1, d))


def _moe_layer(layer, x, x_packed, w_router, router_bias, w_gate, w_up, w_down,
               sh_gate, sh_up, sh_down, ln_g, ln_b):
    n_tok, d = x.shape
    bm = EXPERT_ROWS
    e_k, r_k, g_k, counts = _route(x, w_router, router_bias)
    counts = counts.reshape(N_EXPERTS).astype(jnp.int32)
    padded = (counts + bm - 1) // bm * bm
    padded_end = jnp.cumsum(padded)
    padded_start = padded_end - padded
    n_rows = n_tok * TOP_K + N_EXPERTS * bm

    idx = _row_indices(padded_start, e_k, r_k, n_rows).reshape(PIECES * TOP_K, n_tok)
    xs = _sc_scatter_rows(x_packed.reshape(PIECES * n_tok, LANES), idx, PIECES * n_rows)
    rows = _experts(layer, padded_start // bm, padded // bm, counts,
                    (padded_end[-1:] // bm), xs.reshape(PIECES, n_rows, LANES),
                    w_gate, w_up, w_down)
    picked = _sc_gather_rows(rows.reshape(PIECES * n_rows, LANES), idx)
    picked = picked.reshape(PIECES, TOP_K, n_tok, LANES)
    return _combine(x, picked, g_k.T, sh_gate, sh_up, sh_down, ln_g, ln_b)


def kernel(x, pool_w_in, pool_w_grp, pool_scale, pool_w_out, sgu_w_in, sgu_b_in, sgu_ln_g, sgu_ln_b, sgu_w_s, sgu_b_s, sgu_w_out, ln_mix_g, ln_mix_b, moe_w_router, moe_router_bias, moe_w_gate, moe_w_up, moe_w_down, moe_sh_gate, moe_sh_up, moe_sh_down, ln_ffn_g, ln_ffn_b):
    bsz, seq, d = x.shape
    n_tok = bsz * seq

    def moe(i, h, h_packed):
        return _moe_layer(i, h, h_packed, moe_w_router[i], moe_router_bias[i], moe_w_gate,
                          moe_w_up, moe_w_down, moe_sh_gate[i], moe_sh_up[i],
                          moe_sh_down[i], ln_ffn_g[i], ln_ffn_b[i])

    h, h_packed = _pool_layer(x, pool_w_in[0], pool_w_grp[0], pool_scale[0], pool_w_out[0],
                              ln_mix_g[0], ln_mix_b[0])
    h = moe(0, h.reshape(n_tok, d), h_packed)
    h, h_packed = _sgu_layer(h, sgu_w_in[0], sgu_b_in[0], sgu_ln_g[0], sgu_ln_b[0], sgu_w_s[0],
                             sgu_b_s[0], sgu_w_out[0], ln_mix_g[1], ln_mix_b[1])
    h = moe(1, h, h_packed)
    return h.reshape(bsz, seq, d)
```

```python
import functools

import jax
import jax.numpy as jnp
from jax import lax
from jax.experimental import pallas as pl
from jax.experimental.pallas import tpu as pltpu
from jax.experimental.pallas import tpu_sc as plsc

D_MODEL = 1024
DEPTH = 2
POOL_WINDOWS = (2, 4, 8, 16)
POOL_GROUP_DIM = D_MODEL // len(POOL_WINDOWS)
POOL_HALO = 16
SGU_CHUNK = 128
SGU_HEADS = 4
SGU_WIDTH = 2 * D_MODEL
SGU_HEAD_DIM = SGU_WIDTH // SGU_HEADS
N_EXPERTS = 64
TOP_K = 8
N_GROUPS = 8
GROUP_SIZE = N_EXPERTS // N_GROUPS
TOPK_GROUPS = 4
EXPERT_DIM = D_MODEL // 4
ROUTED_SCALE = 2.5
DEEPNORM_ALPHA = (2 * DEPTH) ** 0.25
LN_EPS = 1e-5

LANES = 128
PACKED = D_MODEL // 2
PIECES = PACKED // LANES

POOL_TILE = 512
SGU_TILE = 256
ROUTE_TILE = 512
INDEX_TILE = 2048
EXPERT_ROWS = 512
EXPERT_SUB = 256
IN_SLOTS = 4
OUT_SLOTS = 3
COMBINE_TILE = 256
SC_WINDOW = 128
VMEM_LIMIT = 56 * 1024 * 1024

_F32 = jnp.float32
_BF16 = jnp.bfloat16
_U32 = jnp.uint32


def _dot(a, b):
    return jnp.dot(a, b, preferred_element_type=_F32)


def _layer_norm(h, g, b):
    mu = jnp.mean(h, axis=-1, keepdims=True)
    hc = h - mu
    var = jnp.mean(hc * hc, axis=-1, keepdims=True)
    return hc * lax.rsqrt(var + LN_EPS) * g + b


def _silu(x):
    return x * jax.nn.sigmoid(x)


def _gelu_tanh(x):
    c = 0.7978845608028654
    return 0.5 * x * (1.0 + jnp.tanh(c * (x + 0.044715 * (x * x * x))))


def _pack_halves(v):
    half = v.shape[1] // 2
    lo = lax.bitcast_convert_type(v[:, :half].astype(_BF16).astype(_F32), _U32)
    hi = lax.bitcast_convert_type(v[:, half:].astype(_BF16).astype(_F32), _U32)
    return (hi & _U32(0xFFFF0000)) | (lo >> 16)


def _unpack_halves(w):
    lo = lax.bitcast_convert_type(w << 16, _F32)
    hi = lax.bitcast_convert_type(w & _U32(0xFFFF0000), _F32)
    return lo, hi


def _store_pieces(ref, w):
    for c in range(PIECES):
        ref[c] = w[:, c * LANES:(c + 1) * LANES]


def _load_pieces(ref):
    return jnp.concatenate([ref[c] for c in range(PIECES)], axis=1)


def _pool_kernel(x_ref, win_ref, wgrp_ref, scale_ref, wout_ref, g_ref, b_ref,
                 o_ref, op_ref, zs_ref, y_ref):
    s = pl.program_id(1)
    ts = x_ref.shape[1]
    x = x_ref[0]
    z = _dot(x.astype(_BF16), win_ref[...])

    @pl.when(s == 0)
    def _():
        zs_ref[0:POOL_HALO, :] = jnp.zeros((POOL_HALO, D_MODEL), _F32)

    zs_ref[POOL_HALO:POOL_HALO + ts, :] = z
    pos = s * ts + lax.broadcasted_iota(jnp.int32, (ts, 1), 0)
    for g, w in enumerate(POOL_WINDOWS):
        c0 = g * POOL_GROUP_DIM
        c1 = c0 + POOL_GROUP_DIM
        zg = zs_ref[POOL_HALO:POOL_HALO + ts, c0:c1]
        acc = zg
        for k in range(1, w):
            acc = acc + zs_ref[POOL_HALO - k:POOL_HALO - k + ts, c0:c1]
        cnt = jnp.minimum(pos + 1, w).astype(_F32)
        pooled = acc / cnt - zg
        yg = _dot(pooled.astype(_BF16), wgrp_ref[g]) * scale_ref[:, c0:c1]
        y_ref[:, c0:c1] = yg.astype(_BF16)
    zs_ref[0:POOL_HALO, :] = zs_ref[ts:ts + POOL_HALO, :]
    mix = _dot(y_ref[...], wout_ref[...])
    out = _layer_norm(DEEPNORM_ALPHA * x + mix, g_ref[...], b_ref[...])
    o_ref[0] = out
    _store_pieces(op_ref, _pack_halves(out))


def _pool_layer(x, w_in, w_grp, scale, w_out, ln_g, ln_b):
    bsz, seq, d = x.shape
    ts = POOL_TILE
    steps = seq // ts
    const2 = lambda b, s: (0, 0)
    out_shape = (jax.ShapeDtypeStruct((bsz, seq, d), _F32),
                 jax.ShapeDtypeStruct((PIECES, bsz * seq, LANES), _U32))
    tile = pl.BlockSpec((1, ts, d), lambda b, s: (b, s, 0))
    ptile = pl.BlockSpec((PIECES, ts, LANES), lambda b, s: (0, b * steps + s, 0))
    return pl.pallas_call(
        _pool_kernel,
        grid=(bsz, steps),
        in_specs=[
            tile,
            pl.BlockSpec((d, d), const2),
            pl.BlockSpec((len(POOL_WINDOWS), POOL_GROUP_DIM, POOL_GROUP_DIM), lambda b, s: (0, 0, 0)),
            pl.BlockSpec((1, d), const2),
            pl.BlockSpec((d, d), const2),
            pl.BlockSpec((1, d), const2),
            pl.BlockSpec((1, d), const2),
        ],
        out_specs=(tile, ptile),
        out_shape=out_shape,
        scratch_shapes=[pltpu.VMEM((POOL_HALO + ts, d), _F32),
                        pltpu.VMEM((ts, d), _BF16)],
        compiler_params=pltpu.CompilerParams(
            dimension_semantics=("arbitrary", "arbitrary"),
            vmem_limit_bytes=VMEM_LIMIT),
        name="pool_mixer",
    )(x, w_in.astype(_BF16), w_grp.astype(_BF16), scale.reshape(1, d),
      w_out.astype(_BF16), ln_g.reshape(1, d), ln_b.reshape(1, d))


def _sgu_kernel(x_ref, win_ref, bin_ref, lng_ref, lnb_ref, ws_ref, bs_ref, wout_ref,
                g_ref, b_ref, o_ref, op_ref, gated_ref):
    ts = x_ref.shape[0]
    x = x_ref[...]
    xb = x.astype(_BF16)
    v = _gelu_tanh(_dot(xb, win_ref[:, SGU_WIDTH:]) + bin_ref[:, SGU_WIDTH:])
    v = _layer_norm(v, lng_ref[...], lnb_ref[...]).astype(_BF16)
    u = _gelu_tanh(_dot(xb, win_ref[:, :SGU_WIDTH]) + bin_ref[:, :SGU_WIDTH])
    for c in range(ts // SGU_CHUNK):
        r0 = c * SGU_CHUNK
        for h in range(SGU_HEADS):
            c0 = h * SGU_HEAD_DIM
            sv = _dot(ws_ref[h], v[r0:r0 + SGU_CHUNK, c0:c0 + SGU_HEAD_DIM]) + bs_ref[h]
            gated_ref[r0:r0 + SGU_CHUNK, c0:c0 + SGU_HEAD_DIM] = (
                u[r0:r0 + SGU_CHUNK, c0:c0 + SGU_HEAD_DIM] * sv).astype(_BF16)
    mix = _dot(gated_ref[...], wout_ref[...])
    out = _layer_norm(DEEPNORM_ALPHA * x + mix, g_ref[...], b_ref[...])
    o_ref[...] = out
    _store_pieces(op_ref, _pack_halves(out))


def _sgu_layer(x, w_in, b_in, ln_g, ln_b, w_s, b_s, w_out, mix_g, mix_b):
    n_tok, d = x.shape
    ts = SGU_TILE
    const2 = lambda i: (0, 0)
    const3 = lambda i: (0, 0, 0)
    causal = jnp.tril(jnp.ones((SGU_CHUNK, SGU_CHUNK), w_s.dtype))
    ws = (w_s * causal[None]).astype(_BF16)
    tile = pl.BlockSpec((ts, d), lambda i: (i, 0))
    ptile = pl.BlockSpec((PIECES, ts, LANES), lambda i: (0, i, 0))
    out_shape = (jax.ShapeDtypeStruct((n_tok, d), _F32),
                 jax.ShapeDtypeStruct((PIECES, n_tok, LANES), _U32))
    return pl.pallas_call(
        _sgu_kernel,
        grid=(n_tok // ts,),
        in_specs=[
            tile,
            pl.BlockSpec((d, 2 * SGU_WIDTH), const2),
            pl.BlockSpec((1, 2 * SGU_WIDTH), const2),
            pl.BlockSpec((1, SGU_WIDTH), const2),
            pl.BlockSpec((1, SGU_WIDTH), const2),
            pl.BlockSpec((SGU_HEADS, SGU_CHUNK, SGU_CHUNK), const3),
            pl.BlockSpec((SGU_HEADS, SGU_CHUNK, 1), const3),
            pl.BlockSpec((SGU_WIDTH, d), const2),
            pl.BlockSpec((1, d), const2),
            pl.BlockSpec((1, d), const2),
        ],
        out_specs=(tile, ptile),
        out_shape=out_shape,
        scratch_shapes=[pltpu.VMEM((ts, SGU_WIDTH), _BF16)],
        compiler_params=pltpu.CompilerParams(
            dimension_semantics=("arbitrary",),
            vmem_limit_bytes=VMEM_LIMIT),
        name="sgu_mixer",
    )(x, w_in.astype(_BF16), b_in.reshape(1, -1), ln_g.reshape(1, -1), ln_b.reshape(1, -1),
      ws, b_s.reshape(SGU_HEADS, SGU_CHUNK, 1), w_out.astype(_BF16),
      mix_g.reshape(1, d), mix_b.reshape(1, d))


def _route_kernel(x_ref, wrt_ref, bias_ref, ek_ref, rk_ref, gk_ref, cnt_ref, carry_ref):
    i = pl.program_id(0)
    ts = x_ref.shape[0]

    @pl.when(i == 0)
    def _():
        carry_ref[...] = jnp.zeros_like(carry_ref)

    logits = lax.dot_general(wrt_ref[...], x_ref[...], (((1,), (1,)), ((), ())),
                             precision=lax.Precision.HIGHEST,
                             preferred_element_type=_F32)
    scores = jax.nn.sigmoid(logits).reshape(N_GROUPS, GROUP_SIZE, ts)
    biased = scores + bias_ref[...]
    neg_inf = jnp.float32(-jnp.inf)
    shape3 = (N_GROUPS, GROUP_SIZE, ts)
    in_grp = lax.broadcasted_iota(jnp.int32, shape3, 1)
    grp = lax.broadcasted_iota(jnp.int32, shape3, 0)
    eid = grp * GROUP_SIZE + in_grp

    m1 = jnp.max(biased, axis=1, keepdims=True)
    first1 = jnp.min(jnp.where(biased == m1, in_grp, GROUP_SIZE), axis=1, keepdims=True)
    m2 = jnp.max(jnp.where(in_grp == first1, neg_inf, biased), axis=1, keepdims=True)
    gscore = m1 + m2

    gid = lax.broadcasted_iota(jnp.int32, (N_GROUPS, 1, ts), 0)
    gsel = jnp.zeros((N_GROUPS, 1, ts), jnp.bool_)
    for _ in range(TOPK_GROUPS):
        m = jnp.max(gscore, axis=0, keepdims=True)
        first = jnp.min(jnp.where(gscore == m, gid, N_GROUPS), axis=0, keepdims=True)
        pick = gid == first
        gsel = jnp.logical_or(gsel, pick)
        gscore = jnp.where(pick, neg_inf, gscore)

    masked = jnp.where(gsel, biased, neg_inf)
    picked_any = jnp.zeros(shape3, jnp.bool_)
    e_k, s_k = [], []
    for _ in range(TOP_K):
        m = jnp.max(jnp.max(masked, axis=1, keepdims=True), axis=0, keepdims=True)
        first = jnp.min(jnp.min(jnp.where(masked == m, eid, N_EXPERTS), axis=1, keepdims=True),
                        axis=0, keepdims=True)
        pick = eid == first
        picked_any = jnp.logical_or(picked_any, pick)
        masked = jnp.where(pick, neg_inf, masked)
        sc = jnp.sum(jnp.sum(jnp.where(pick, scores, 0.0), axis=1, keepdims=True),
                     axis=0, keepdims=True)
        e_k.append(first)
        s_k.append(sc)
    denom = s_k[0]
    for k in range(1, TOP_K):
        denom = denom + s_k[k]

    sel = picked_any.astype(_F32).reshape(N_EXPERTS, ts)
    row = lax.broadcasted_iota(jnp.int32, (ts, ts), 0)
    col = lax.broadcasted_iota(jnp.int32, (ts, ts), 1)
    earlier = (row < col).astype(_BF16)
    rank = _dot(sel.astype(_BF16), earlier) + carry_ref[...]
    rank3 = rank.reshape(shape3)
    carry_ref[...] += jnp.sum(sel, axis=1, keepdims=True)
    cnt_ref[...] = carry_ref[...]

    for k in range(TOP_K):
        r = jnp.sum(jnp.sum(jnp.where(eid == e_k[k], rank3, 0.0), axis=1, keepdims=True),
                    axis=0, keepdims=True)
        ek_ref[k:k + 1, :] = e_k[k].reshape(1, ts)
        rk_ref[k:k + 1, :] = r.reshape(1, ts).astype(jnp.int32)
        gk_ref[k:k + 1, :] = (s_k[k] / denom * ROUTED_SCALE).reshape(1, ts)


def _route(x, w_router, router_bias):
    n_tok, d = x.shape
    ts = ROUTE_TILE
    out_shape = (jax.ShapeDtypeStruct((TOP_K, n_tok), jnp.int32),
                 jax.ShapeDtypeStruct((TOP_K, n_tok), jnp.int32),
                 jax.ShapeDtypeStruct((TOP_K, n_tok), _F32),
                 jax.ShapeDtypeStruct((N_EXPERTS, 1), _F32))
    kspec = pl.BlockSpec((TOP_K, ts), lambda i: (0, i))
    return pl.pallas_call(
        _route_kernel,
        grid=(n_tok // ts,),
        in_specs=[
            pl.BlockSpec((ts, d), lambda i: (i, 0)),
            pl.BlockSpec((N_EXPERTS, d), lambda i: (0, 0)),
            pl.BlockSpec((N_GROUPS, GROUP_SIZE, 1), lambda i: (0, 0, 0)),
        ],
        out_specs=(kspec, kspec, kspec, pl.BlockSpec((N_EXPERTS, 1), lambda i: (0, 0))),
        out_shape=out_shape,
        scratch_shapes=[pltpu.VMEM((N_EXPERTS, 1), _F32)],
        compiler_params=pltpu.CompilerParams(
            dimension_semantics=("arbitrary",),
            vmem_limit_bytes=VMEM_LIMIT),
        name="moe_route",
    )(x, w_router.T, router_bias.reshape(N_GROUPS, GROUP_SIZE, 1))


def _sc_mesh():
    return plsc.VectorSubcoreMesh(core_axis_name="core", subcore_axis_name="subcore")


def _sc_scatter_rows(src, idx, n_out):
    d = src.shape[1]
    n_tok = idx.shape[1]
    wins = n_tok // SC_WINDOW

    def body(src_hbm, idx_hbm, out_hbm):
        def step(src_vmem, idx_vmem):
            for k in range(TOP_K):
                pltpu.sync_copy(src_vmem, out_hbm.at[idx_vmem.at[k]])

        pltpu.emit_pipeline(
            step,
            grid=(src.shape[0] // SC_WINDOW,),
            in_specs=[pl.BlockSpec((SC_WINDOW, d), index_map=lambda i: (i, 0)),
                      pl.BlockSpec((TOP_K, SC_WINDOW), index_map=lambda i: (i // wins, i % wins))],
            out_specs=[],
            core_axis_name=("core", "subcore"),
            dimension_semantics=(pltpu.PARALLEL,),
        )(src_hbm, idx_hbm)

    return pl.kernel(body, out_type=jax.ShapeDtypeStruct((n_out, d), src.dtype),
                     mesh=_sc_mesh(), scratch_types=[], name="sc_scatter_rows")(src, idx)


def _sc_gather_rows(table, idx):
    d = table.shape[1]
    n_idx_rows, n_tok = idx.shape
    wins = n_tok // SC_WINDOW

    def body(table_hbm, idx_hbm, out_hbm):
        def step(idx_vmem, out_vmem):
            pltpu.sync_copy(table_hbm.at[idx_vmem.at[0]], out_vmem)

        pltpu.emit_pipeline(
            step,
            grid=(n_idx_rows * wins,),
            in_specs=[pl.BlockSpec((1, SC_WINDOW), index_map=lambda i: (i // wins, i % wins))],
            out_specs=[pl.BlockSpec((SC_WINDOW, d), index_map=lambda i: (i, 0))],
            core_axis_name=("core", "subcore"),
            dimension_semantics=(pltpu.PARALLEL,),
        )(idx_hbm, out_hbm)

    return pl.kernel(body, out_type=jax.ShapeDtypeStruct((n_idx_rows * n_tok, d), table.dtype),
                     mesh=_sc_mesh(), scratch_types=[], name="sc_gather_rows")(table, idx)


def _index_kernel(start_ref, ek_ref, rk_ref, idx_ref, *, n_rows):
    e = ek_ref[...]
    start = jnp.zeros(e.shape, jnp.int32)
    for ex in range(N_EXPERTS):
        start = jnp.where(e == ex, start_ref[ex], start)
    dest = start + rk_ref[...]
    for c in range(PIECES):
        idx_ref[c] = dest + c * n_rows


def _row_indices(padded_start, e_k, r_k, n_rows):
    n_tok = e_k.shape[1]
    ts = INDEX_TILE
    kspec = pl.BlockSpec((TOP_K, ts), lambda i, st: (0, i))
    grid_spec = pltpu.PrefetchScalarGridSpec(
        num_scalar_prefetch=1,
        grid=(n_tok // ts,),
        in_specs=[kspec, kspec],
        out_specs=pl.BlockSpec((PIECES, TOP_K, ts), lambda i, st: (0, 0, i)),
    )
    return pl.pallas_call(
        functools.partial(_index_kernel, n_rows=n_rows),
        grid_spec=grid_spec,
        out_shape=jax.ShapeDtypeStruct((PIECES, TOP_K, n_tok), jnp.int32),
        compiler_params=pltpu.CompilerParams(dimension_semantics=("arbitrary",)),
        name="moe_row_indices",
    )(padded_start, e_k, r_k)


def _expert_kernel(chunk0_ref, nchunk_ref, cnt_ref, total_ref,
                   xs_hbm, wg_ref, wu_ref, wd_ref, o_hbm,
                   xbuf, obuf, wgu_bf, wd_bf, sem_in, sem_out):
    e = pl.program_id(0)
    total = total_ref[0]
    ch = EXPERT_ROWS
    ahead = IN_SLOTS - 1

    def in_copy(g):
        slot = g % IN_SLOTS
        return pltpu.make_async_copy(xs_hbm.at[:, pl.ds(g * ch, ch), :], xbuf.at[slot],
                                     sem_in.at[slot])

    def out_copy(g):
        slot = g % OUT_SLOTS
        return pltpu.make_async_copy(obuf.at[slot], o_hbm.at[:, pl.ds(g * ch, ch), :],
                                     sem_out.at[slot])

    @pl.when(e == 0)
    def _():
        obuf[...] = jnp.zeros_like(obuf)
        for g in range(ahead):
            @pl.when(g < total)
            def _():
                in_copy(g).start()

    wgu_bf[:, :EXPERT_DIM] = wg_ref[0, 0].astype(_BF16)
    wgu_bf[:, EXPERT_DIM:] = wu_ref[0, 0].astype(_BF16)
    wd_bf[...] = wd_ref[0, 0].astype(_BF16)

    def chunk(j, carry):
        g = chunk0_ref[e] + j
        in_copy(g).wait()

        @pl.when(g + ahead < total)
        def _():
            in_copy(g + ahead).start()

        @pl.when(g >= OUT_SLOTS)
        def _():
            out_copy(g - OUT_SLOTS).wait()

        islot = g % IN_SLOTS
        oslot = g % OUT_SLOTS
        valid = cnt_ref[e] - j * ch
        for s in range(ch // EXPERT_SUB):
            r0 = s * EXPERT_SUB

            @pl.when(r0 < valid)
            def _():
                w = jnp.concatenate([xbuf[islot, c, r0:r0 + EXPERT_SUB, :] for c in range(PIECES)],
                                    axis=1)
                lo, hi = _unpack_halves(w)
                gu = (_dot(lo.astype(_BF16), wgu_bf[:PACKED]) +
                      _dot(hi.astype(_BF16), wgu_bf[PACKED:]))
                h = _silu(gu[:, :EXPERT_DIM]) * gu[:, EXPERT_DIM:]
                y = _pack_halves(_dot(h.astype(_BF16), wd_bf[...]))
                for c in range(PIECES):
                    obuf[oslot, c, r0:r0 + EXPERT_SUB, :] = y[:, c * LANES:(c + 1) * LANES]

        out_copy(g).start()
        return carry

    lax.fori_loop(0, nchunk_ref[e], chunk, 0)

    @pl.when(e == pl.num_programs(0) - 1)
    def _():
        for back in range(OUT_SLOTS, 0, -1):
            @pl.when(total >= back)
            def _():
                out_copy(total - back).wait()


def _experts(layer, chunk0, nchunk, counts, total, xs, w_gate, w_up, w_down):
    d = D_MODEL
    ch = EXPERT_ROWS
    wspec_in = pl.BlockSpec((1, 1, d, EXPERT_DIM), lambda e, *_: (layer, e, 0, 0))
    any_spec = pl.BlockSpec(memory_space=pl.ANY)
    grid_spec = pltpu.PrefetchScalarGridSpec(
        num_scalar_prefetch=4,
        grid=(N_EXPERTS,),
        in_specs=[
            any_spec,
            wspec_in,
            wspec_in,
            pl.BlockSpec((1, 1, EXPERT_DIM, d), lambda e, *_: (layer, e, 0, 0)),
        ],
        out_specs=any_spec,
        scratch_shapes=[pltpu.VMEM((IN_SLOTS, PIECES, ch, LANES), _U32),
                        pltpu.VMEM((OUT_SLOTS, PIECES, ch, LANES), _U32),
                        pltpu.VMEM((d, 2 * EXPERT_DIM), _BF16),
                        pltpu.VMEM((EXPERT_DIM, d), _BF16),
                        pltpu.SemaphoreType.DMA((IN_SLOTS,)),
                        pltpu.SemaphoreType.DMA((OUT_SLOTS,))],
    )
    return pl.pallas_call(
        _expert_kernel,
        grid_spec=grid_spec,
        out_shape=jax.ShapeDtypeStruct(xs.shape, _U32),
        compiler_params=pltpu.CompilerParams(
            dimension_semantics=("arbitrary",),
            vmem_limit_bytes=VMEM_LIMIT),
        name="moe_experts",
    )(chunk0, nchunk, counts, total, xs, w_gate, w_up, w_down)


def _combine_kernel(x_ref, p_ref, gate_ref, sg_ref, su_ref, sd_ref, g_ref, b_ref, o_ref):
    x = x_ref[...]
    xb = x.astype(_BF16)
    gates = gate_ref[...]
    r_lo = r_hi = None
    for k in range(TOP_K):
        w = jnp.concatenate([p_ref[c, k] for c in range(PIECES)], axis=1)
        lo, hi = _unpack_halves(w)
        gk = gates[:, k:k + 1]
        r_lo = gk * lo if r_lo is None else r_lo + gk * lo
        r_hi = gk * hi if r_hi is None else r_hi + gk * hi
    routed = jnp.concatenate([r_lo, r_hi], axis=1)
    h = _silu(_dot(xb, sg_ref[...])) * _dot(xb, su_ref[...])
    shared = _dot(h.astype(_BF16), sd_ref[...])
    o_ref[...] = _layer_norm(DEEPNORM_ALPHA * x + (routed + shared), g_ref[...], b_ref[...])


def _combine(x, picked, gates, sh_gate, sh_up, sh_down, ln_g, ln_b):
    n_tok, d = x.shape
    ts = COMBINE_TILE
    const2 = lambda i: (0, 0)
    return pl.pallas_call(
        _combine_kernel,
        grid=(n_tok // ts,),
        in_specs=[
            pl.BlockSpec((ts, d), lambda i: (i, 0)),
            pl.BlockSpec((PIECES, TOP_K, ts, LANES), lambda i: (0, 0, i, 0)),
            pl.BlockSpec((ts, TOP_K), lambda i: (i, 0)),
            pl.BlockSpec((d, EXPERT_DIM), const2),
            pl.BlockSpec((d, EXPERT_DIM), const2),
            pl.BlockSpec((EXPERT_DIM, d), const2),
            pl.BlockSpec((1, d), const2),
            pl.BlockSpec((1, d), const2),
        ],
        out_specs=pl.BlockSpec((ts, d), lambda i: (i, 0)),
        out_shape=jax.ShapeDtypeStruct((n_tok, d), _F32),
        compiler_params=pltpu.CompilerParams(
            dimension_semantics=("arbitrary",),
            vmem_limit_bytes=VMEM_LIMIT),
        name="moe_combine",
    )(x, picked, gates, sh_gate.astype(_BF16), sh_up.astype(_BF16), sh_down.astype(_BF16),
      ln_g.reshape(1, d), ln_b.reshape(1, d))


def _moe_layer(layer, x, x_packed, w_router, router_bias, w_gate, w_up, w_down,
               sh_gate, sh_up, sh_down, ln_g, ln_b):
    n_tok, d = x.shape
    bm = EXPERT_ROWS
    e_k, r_k, g_k, counts = _route(x, w_router, router_bias)
    counts = counts.reshape(N_EXPERTS).astype(jnp.int32)
    padded = (counts + bm - 1) // bm * bm
    padded_end = jnp.cumsum(padded)
    padded_start = padded_end - padded
    n_rows = n_tok * TOP_K + N_EXPERTS * bm

    idx = _row_indices(padded_start, e_k, r_k, n_rows).reshape(PIECES * TOP_K, n_tok)
    xs = _sc_scatter_rows(x_packed.reshape(PIECES * n_tok, LANES), idx, PIECES * n_rows)
    rows = _experts(layer, padded_start // bm, padded // bm, counts,
                    (padded_end[-1:] // bm), xs.reshape(PIECES, n_rows, LANES),
                    w_gate, w_up, w_down)
    picked = _sc_gather_rows(rows.reshape(PIECES * n_rows, LANES), idx)
    picked = picked.reshape(PIECES, TOP_K, n_tok, LANES)
    return _combine(x, picked, g_k.T, sh_gate, sh_up, sh_down, ln_g, ln_b)


def kernel(x, pool_w_in, pool_w_grp, pool_scale, pool_w_out, sgu_w_in, sgu_b_in, sgu_ln_g, sgu_ln_b, sgu_w_s, sgu_b_s, sgu_w_out, ln_mix_g, ln_mix_b, moe_w_router, moe_router_bias, moe_w_gate, moe_w_up, moe_w_down, moe_sh_gate, moe_sh_up, moe_sh_down, ln_ffn_g, ln_ffn_b):
    bsz, seq, d = x.shape
    n_tok = bsz * seq

    def moe(i, h, h_packed):
        return _moe_layer(i, h, h_packed, moe_w_router[i], moe_router_bias[i], moe_w_gate,
                          moe_w_up, moe_w_down, moe_sh_gate[i], moe_sh_up[i],
                          moe_sh_down[i], ln_ffn_g[i], ln_ffn_b[i])

    h, h_packed = _pool_layer(x, pool_w_in[0], pool_w_grp[0], pool_scale[0], pool_w_out[0],
                              ln_mix_g[0], ln_mix_b[0])
    h = moe(0, h.reshape(n_tok, d), h_packed)
    h, h_packed = _sgu_layer(h, sgu_w_in[0], sgu_b_in[0], sgu_ln_g[0], sgu_ln_b[0], sgu_w_s[0],
                             sgu_b_s[0], sgu_w_out[0], ln_mix_g[1], ln_mix_b[1])
    h = moe(1, h, h_packed)
    return h.reshape(bsz, seq, d)
```

```python
import functools

import jax
import jax.numpy as jnp
from jax import lax
from jax.experimental import pallas as pl
from jax.experimental.pallas import tpu as pltpu
from jax.experimental.pallas import tpu_sc as plsc

D_MODEL = 1024
DEPTH = 2
POOL_WINDOWS = (2, 4, 8, 16)
POOL_GROUP_DIM = D_MODEL // len(POOL_WINDOWS)
POOL_HALO = 16
SGU_CHUNK = 128
SGU_HEADS = 4
SGU_WIDTH = 2 * D_MODEL
SGU_HEAD_DIM = SGU_WIDTH // SGU_HEADS
N_EXPERTS = 64
TOP_K = 8
N_GROUPS = 8
GROUP_SIZE = N_EXPERTS // N_GROUPS
TOPK_GROUPS = 4
EXPERT_DIM = D_MODEL // 4
ROUTED_SCALE = 2.5
DEEPNORM_ALPHA = (2 * DEPTH) ** 0.25
LN_EPS = 1e-5

LANES = 128
PACKED = D_MODEL // 2
PIECES = PACKED // LANES

POOL_TILE = 512
SGU_TILE = 256
ROUTE_TILE = 512
INDEX_TILE = 2048
EXPERT_ROWS = 512
EXPERT_SUB = 256
IN_SLOTS = 4
OUT_SLOTS = 3
COMBINE_TILE = 256
SC_WINDOW = 128
VMEM_LIMIT = 56 * 1024 * 1024

_F32 = jnp.float32
_BF16 = jnp.bfloat16
_U32 = jnp.uint32


def _dot(a, b):
    return jnp.dot(a, b, preferred_element_type=_F32)


def _layer_norm(h, g, b):
    mu = jnp.mean(h, axis=-1, keepdims=True)
    hc = h - mu
    var = jnp.mean(hc * hc, axis=-1, keepdims=True)
    return hc * lax.rsqrt(var + LN_EPS) * g + b


def _silu(x):
    return x * jax.nn.sigmoid(x)


def _gelu_tanh(x):
    c = 0.7978845608028654
    return 0.5 * x * (1.0 + jnp.tanh(c * (x + 0.044715 * (x * x * x))))


def _pack_halves(v):
    half = v.shape[1] // 2
    lo = lax.bitcast_convert_type(v[:, :half].astype(_BF16).astype(_F32), _U32)
    hi = lax.bitcast_convert_type(v[:, half:].astype(_BF16).astype(_F32), _U32)
    return (hi & _U32(0xFFFF0000)) | (lo >> 16)


def _unpack_halves(w):
    lo = lax.bitcast_convert_type(w << 16, _F32)
    hi = lax.bitcast_convert_type(w & _U32(0xFFFF0000), _F32)
    return lo, hi


def _store_pieces(ref, w):
    for c in range(PIECES):
        ref[c] = w[:, c * LANES:(c + 1) * LANES]


def _load_pieces(ref):
    return jnp.concatenate([ref[c] for c in range(PIECES)], axis=1)


def _pool_kernel(x_ref, win_ref, wgrp_ref, scale_ref, wout_ref, g_ref, b_ref,
                 o_ref, op_ref, zs_ref, y_ref):
    s = pl.program_id(1)
    ts = x_ref.shape[1]
    x = x_ref[0]
    z = _dot(x.astype(_BF16), win_ref[...])

    @pl.when(s == 0)
    def _():
        zs_ref[0:POOL_HALO, :] = jnp.zeros((POOL_HALO, D_MODEL), _F32)

    zs_ref[POOL_HALO:POOL_HALO + ts, :] = z
    pos = s * ts + lax.broadcasted_iota(jnp.int32, (ts, 1), 0)
    for g, w in enumerate(POOL_WINDOWS):
        c0 = g * POOL_GROUP_DIM
        c1 = c0 + POOL_GROUP_DIM
        zg = zs_ref[POOL_HALO:POOL_HALO + ts, c0:c1]
        acc = zg
        for k in range(1, w):
            acc = acc + zs_ref[POOL_HALO - k:POOL_HALO - k + ts, c0:c1]
        cnt = jnp.minimum(pos + 1, w).astype(_F32)
        pooled = acc / cnt - zg
        yg = _dot(pooled.astype(_BF16), wgrp_ref[g]) * scale_ref[:, c0:c1]
        y_ref[:, c0:c1] = yg.astype(_BF16)
    zs_ref[0:POOL_HALO, :] = zs_ref[ts:ts + POOL_HALO, :]
    mix = _dot(y_ref[...], wout_ref[...])
    out = _layer_norm(DEEPNORM_ALPHA * x + mix, g_ref[...], b_ref[...])
    o_ref[0] = out
    _store_pieces(op_ref, _pack_halves(out))


def _pool_layer(x, w_in, w_grp, scale, w_out, ln_g, ln_b):
    bsz, seq, d = x.shape
    ts = POOL_TILE
    steps = seq // ts
    const2 = lambda b, s: (0, 0)
    out_shape = (jax.ShapeDtypeStruct((bsz, seq, d), _F32),
                 jax.ShapeDtypeStruct((PIECES, bsz * seq, LANES), _U32))
    tile = pl.BlockSpec((1, ts, d), lambda b, s: (b, s, 0))
    ptile = pl.BlockSpec((PIECES, ts, LANES), lambda b, s: (0, b * steps + s, 0))
    return pl.pallas_call(
        _pool_kernel,
        grid=(bsz, steps),
        in_specs=[
            tile,
            pl.BlockSpec((d, d), const2),
            pl.BlockSpec((len(POOL_WINDOWS), POOL_GROUP_DIM, POOL_GROUP_DIM), lambda b, s: (0, 0, 0)),
            pl.BlockSpec((1, d), const2),
            pl.BlockSpec((d, d), const2),
            pl.BlockSpec((1, d), const2),
            pl.BlockSpec((1, d), const2),
        ],
        out_specs=(tile, ptile),
        out_shape=out_shape,
        scratch_shapes=[pltpu.VMEM((POOL_HALO + ts, d), _F32),
                        pltpu.VMEM((ts, d), _BF16)],
        compiler_params=pltpu.CompilerParams(
            dimension_semantics=("arbitrary", "arbitrary"),
            vmem_limit_bytes=VMEM_LIMIT),
        name="pool_mixer",
    )(x, w_in.astype(_BF16), w_grp.astype(_BF16), scale.reshape(1, d),
      w_out.astype(_BF16), ln_g.reshape(1, d), ln_b.reshape(1, d))


def _sgu_kernel(x_ref, win_ref, bin_ref, lng_ref, lnb_ref, ws_ref, bs_ref, wout_ref,
                g_ref, b_ref, o_ref, op_ref, gated_ref):
    ts = x_ref.shape[0]
    x = x_ref[...]
    xb = x.astype(_BF16)
    v = _gelu_tanh(_dot(xb, win_ref[:, SGU_WIDTH:]) + bin_ref[:, SGU_WIDTH:])
    v = _layer_norm(v, lng_ref[...], lnb_ref[...]).astype(_BF16)
    u = _gelu_tanh(_dot(xb, win_ref[:, :SGU_WIDTH]) + bin_ref[:, :SGU_WIDTH])
    for c in range(ts // SGU_CHUNK):
        r0 = c * SGU_CHUNK
        for h in range(SGU_HEADS):
            c0 = h * SGU_HEAD_DIM
            sv = _dot(ws_ref[h], v[r0:r0 + SGU_CHUNK, c0:c0 + SGU_HEAD_DIM]) + bs_ref[h]
            gated_ref[r0:r0 + SGU_CHUNK, c0:c0 + SGU_HEAD_DIM] = (
                u[r0:r0 + SGU_CHUNK, c0:c0 + SGU_HEAD_DIM] * sv).astype(_BF16)
    mix = _dot(gated_ref[...], wout_ref[...])
    out = _layer_norm(DEEPNORM_ALPHA * x + mix, g_ref[...], b_ref[...])
    o_ref[...] = out
    _store_pieces(op_ref, _pack_halves(out))


def _sgu_layer(x, w_in, b_in, ln_g, ln_b, w_s, b_s, w_out, mix_g, mix_b):
    n_tok, d = x.shape
    ts = SGU_TILE
    const2 = lambda i: (0, 0)
    const3 = lambda i: (0, 0, 0)
    causal = jnp.tril(jnp.ones((SGU_CHUNK, SGU_CHUNK), w_s.dtype))
    ws = (w_s * causal[None]).astype(_BF16)
    tile = pl.BlockSpec((ts, d), lambda i: (i, 0))
    ptile = pl.BlockSpec((PIECES, ts, LANES), lambda i: (0, i, 0))
    out_shape = (jax.ShapeDtypeStruct((n_tok, d), _F32),
                 jax.ShapeDtypeStruct((PIECES, n_tok, LANES), _U32))
    return pl.pallas_call(
        _sgu_kernel,
        grid=(n_tok // ts,),
        in_specs=[
            tile,
            pl.BlockSpec((d, 2 * SGU_WIDTH), const2),
            pl.BlockSpec((1, 2 * SGU_WIDTH), const2),
            pl.BlockSpec((1, SGU_WIDTH), const2),
            pl.BlockSpec((1, SGU_WIDTH), const2),
            pl.BlockSpec((SGU_HEADS, SGU_CHUNK, SGU_CHUNK), const3),
            pl.BlockSpec((SGU_HEADS, SGU_CHUNK, 1), const3),
            pl.BlockSpec((SGU_WIDTH, d), const2),
            pl.BlockSpec((1, d), const2),
            pl.BlockSpec((1, d), const2),
        ],
        out_specs=(tile, ptile),
        out_shape=out_shape,
        scratch_shapes=[pltpu.VMEM((ts, SGU_WIDTH), _BF16)],
        compiler_params=pltpu.CompilerParams(
            dimension_semantics=("arbitrary",),
            vmem_limit_bytes=VMEM_LIMIT),
        name="sgu_mixer",
    )(x, w_in.astype(_BF16), b_in.reshape(1, -1), ln_g.reshape(1, -1), ln_b.reshape(1, -1),
      ws, b_s.reshape(SGU_HEADS, SGU_CHUNK, 1), w_out.astype(_BF16),
      mix_g.reshape(1, d), mix_b.reshape(1, d))


def _route_kernel(x_ref, wrt_ref, bias_ref, ek_ref, rk_ref, gk_ref, cnt_ref, carry_ref):
    i = pl.program_id(0)
    ts = x_ref.shape[0]

    @pl.when(i == 0)
    def _():
        carry_ref[...] = jnp.zeros_like(carry_ref)

    logits = lax.dot_general(wrt_ref[...], x_ref[...], (((1,), (1,)), ((), ())),
                             precision=lax.Precision.HIGHEST,
                             preferred_element_type=_F32)
    scores = jax.nn.sigmoid(logits).reshape(N_GROUPS, GROUP_SIZE, ts)
    biased = scores + bias_ref[...]
    neg_inf = jnp.float32(-jnp.inf)
    shape3 = (N_GROUPS, GROUP_SIZE, ts)
    in_grp = lax.broadcasted_iota(jnp.int32, shape3, 1)
    grp = lax.broadcasted_iota(jnp.int32, shape3, 0)
    eid = grp * GROUP_SIZE + in_grp

    m1 = jnp.max(biased, axis=1, keepdims=True)
    first1 = jnp.min(jnp.where(biased == m1, in_grp, GROUP_SIZE), axis=1, keepdims=True)
    m2 = jnp.max(jnp.where(in_grp == first1, neg_inf, biased), axis=1, keepdims=True)
    gscore = m1 + m2

    gid = lax.broadcasted_iota(jnp.int32, (N_GROUPS, 1, ts), 0)
    gsel = jnp.zeros((N_GROUPS, 1, ts), jnp.bool_)
    for _ in range(TOPK_GROUPS):
        m = jnp.max(gscore, axis=0, keepdims=True)
        first = jnp.min(jnp.where(gscore == m, gid, N_GROUPS), axis=0, keepdims=True)
        pick = gid == first
        gsel = jnp.logical_or(gsel, pick)
        gscore = jnp.where(pick, neg_inf, gscore)

    masked = jnp.where(gsel, biased, neg_inf)
    picked_any = jnp.zeros(shape3, jnp.bool_)
    e_k, s_k = [], []
    for _ in range(TOP_K):
        m = jnp.max(jnp.max(masked, axis=1, keepdims=True), axis=0, keepdims=True)
        first = jnp.min(jnp.min(jnp.where(masked == m, eid, N_EXPERTS), axis=1, keepdims=True),
                        axis=0, keepdims=True)
        pick = eid == first
        picked_any = jnp.logical_or(picked_any, pick)
        masked = jnp.where(pick, neg_inf, masked)
        sc = jnp.sum(jnp.sum(jnp.where(pick, scores, 0.0), axis=1, keepdims=True),
                     axis=0, keepdims=True)
        e_k.append(first)
        s_k.append(sc)
    denom = s_k[0]
    for k in range(1, TOP_K):
        denom = denom + s_k[k]

    sel = picked_any.astype(_F32).reshape(N_EXPERTS, ts)
    row = lax.broadcasted_iota(jnp.int32, (ts, ts), 0)
    col = lax.broadcasted_iota(jnp.int32, (ts, ts), 1)
    earlier = (row < col).astype(_BF16)
    rank = _dot(sel.astype(_BF16), earlier) + carry_ref[...]
    rank3 = rank.reshape(shape3)
    carry_ref[...] += jnp.sum(sel, axis=1, keepdims=True)
    cnt_ref[...] = carry_ref[...]

    for k in range(TOP_K):
        r = jnp.sum(jnp.sum(jnp.where(eid == e_k[k], rank3, 0.0), axis=1, keepdims=True),
                    axis=0, keepdims=True)
        ek_ref[k:k + 1, :] = e_k[k].reshape(1, ts)
        rk_ref[k:k + 1, :] = r.reshape(1, ts).astype(jnp.int32)
        gk_ref[k:k + 1, :] = (s_k[k] / denom * ROUTED_SCALE).reshape(1, ts)


def _route(x, w_router, router_bias):
    n_tok, d = x.shape
    ts = ROUTE_TILE
    out_shape = (jax.ShapeDtypeStruct((TOP_K, n_tok), jnp.int32),
                 jax.ShapeDtypeStruct((TOP_K, n_tok), jnp.int32),
                 jax.ShapeDtypeStruct((TOP_K, n_tok), _F32),
                 jax.ShapeDtypeStruct((N_EXPERTS, 1), _F32))
    kspec = pl.BlockSpec((TOP_K, ts), lambda i: (0, i))
    return pl.pallas_call(
        _route_kernel,
        grid=(n_tok // ts,),
        in_specs=[
            pl.BlockSpec((ts, d), lambda i: (i, 0)),
            pl.BlockSpec((N_EXPERTS, d), lambda i: (0, 0)),
            pl.BlockSpec((N_GROUPS, GROUP_SIZE, 1), lambda i: (0, 0, 0)),
        ],
        out_specs=(kspec, kspec, kspec, pl.BlockSpec((N_EXPERTS, 1), lambda i: (0, 0))),
        out_shape=out_shape,
        scratch_shapes=[pltpu.VMEM((N_EXPERTS, 1), _F32)],
        compiler_params=pltpu.CompilerParams(
            dimension_semantics=("arbitrary",),
            vmem_limit_bytes=VMEM_LIMIT),
        name="moe_route",
    )(x, w_router.T, router_bias.reshape(N_GROUPS, GROUP_SIZE, 1))


def _sc_mesh():
    return plsc.VectorSubcoreMesh(core_axis_name="core", subcore_axis_name="subcore")


def _sc_scatter_rows(src, idx, n_out):
    d = src.shape[1]
    n_tok = idx.shape[1]
    wins = n_tok // SC_WINDOW

    def body(src_hbm, idx_hbm, out_hbm):
        def step(src_vmem, idx_vmem):
            for k in range(TOP_K):
                pltpu.sync_copy(src_vmem, out_hbm.at[idx_vmem.at[k]])

        pltpu.emit_pipeline(
            step,
            grid=(src.shape[0] // SC_WINDOW,),
            in_specs=[pl.BlockSpec((SC_WINDOW, d), index_map=lambda i: (i, 0)),
                      pl.BlockSpec((TOP_K, SC_WINDOW), index_map=lambda i: (i // wins, i % wins))],
            out_specs=[],
            core_axis_name=("core", "subcore"),
            dimension_semantics=(pltpu.PARALLEL,),
        )(src_hbm, idx_hbm)

    return pl.kernel(body, out_type=jax.ShapeDtypeStruct((n_out, d), src.dtype),
                     mesh=_sc_mesh(), scratch_types=[], name="sc_scatter_rows")(src, idx)


def _sc_gather_rows(table, idx):
    d = table.shape[1]
    n_idx_rows, n_tok = idx.shape
    wins = n_tok // SC_WINDOW

    def body(table_hbm, idx_hbm, out_hbm):
        def step(idx_vmem, out_vmem):
            pltpu.sync_copy(table_hbm.at[idx_vmem.at[0]], out_vmem)

        pltpu.emit_pipeline(
            step,
            grid=(n_idx_rows * wins,),
            in_specs=[pl.BlockSpec((1, SC_WINDOW), index_map=lambda i: (i // wins, i % wins))],
            out_specs=[pl.BlockSpec((SC_WINDOW, d), index_map=lambda i: (i, 0))],
            core_axis_name=("core", "subcore"),
            dimension_semantics=(pltpu.PARALLEL,),
        )(idx_hbm, out_hbm)

    return pl.kernel(body, out_type=jax.ShapeDtypeStruct((n_idx_rows * n_tok, d), table.dtype),
                     mesh=_sc_mesh(), scratch_types=[], name="sc_gather_rows")(table, idx)


def _index_kernel(start_ref, ek_ref, rk_ref, idx_ref, *, n_rows):
    e = ek_ref[...]
    start = jnp.zeros(e.shape, jnp.int32)
    for ex in range(N_EXPERTS):
        start = jnp.where(e == ex, start_ref[ex], start)
    dest = start + rk_ref[...]
    for c in range(PIECES):
        idx_ref[c] = dest + c * n_rows


def _row_indices(padded_start, e_k, r_k, n_rows):
    n_tok = e_k.shape[1]
    ts = INDEX_TILE
    kspec = pl.BlockSpec((TOP_K, ts), lambda i, st: (0, i))
    grid_spec = pltpu.PrefetchScalarGridSpec(
        num_scalar_prefetch=1,
        grid=(n_tok // ts,),
        in_specs=[kspec, kspec],
        out_specs=pl.BlockSpec((PIECES, TOP_K, ts), lambda i, st: (0, 0, i)),
    )
    return pl.pallas_call(
        functools.partial(_index_kernel, n_rows=n_rows),
        grid_spec=grid_spec,
        out_shape=jax.ShapeDtypeStruct((PIECES, TOP_K, n_tok), jnp.int32),
        compiler_params=pltpu.CompilerParams(dimension_semantics=("arbitrary",)),
        name="moe_row_indices",
    )(padded_start, e_k, r_k)


def _expert_kernel(chunk0_ref, nchunk_ref, cnt_ref, total_ref,
                   xs_hbm, wg_ref, wu_ref, wd_ref, o_hbm,
                   xbuf, obuf, wgu_bf, wd_bf, sem_in, sem_out):
    e = pl.program_id(0)
    total = total_ref[0]
    ch = EXPERT_ROWS
    ahead = IN_SLOTS - 1

    def in_copy(g):
        slot = g % IN_SLOTS
        return pltpu.make_async_copy(xs_hbm.at[:, pl.ds(g * ch, ch), :], xbuf.at[slot],
                                     sem_in.at[slot])

    def out_copy(g):
        slot = g % OUT_SLOTS
        return pltpu.make_async_copy(obuf.at[slot], o_hbm.at[:, pl.ds(g * ch, ch), :],
                                     sem_out.at[slot])

    @pl.when(e == 0)
    def _():
        obuf[...] = jnp.zeros_like(obuf)
        for g in range(ahead):
            @pl.when(g < total)
            def _():
                in_copy(g).start()

    wgu_bf[:, :EXPERT_DIM] = wg_ref[0, 0].astype(_BF16)
    wgu_bf[:, EXPERT_DIM:] = wu_ref[0, 0].astype(_BF16)
    wd_bf[...] = wd_ref[0, 0].astype(_BF16)

    def chunk(j, carry):
        g = chunk0_ref[e] + j
        in_copy(g).wait()

        @pl.when(g + ahead < total)
        def _():
            in_copy(g + ahead).start()

        @pl.when(g >= OUT_SLOTS)
        def _():
            out_copy(g - OUT_SLOTS).wait()

        islot = g % IN_SLOTS
        oslot = g % OUT_SLOTS
        valid = cnt_ref[e] - j * ch

        def sub_block(s):
            r0 = s * EXPERT_SUB
            w = jnp.concatenate([xbuf[islot, c, r0:r0 + EXPERT_SUB, :] for c in range(PIECES)],
                                axis=1)
            lo, hi = _unpack_halves(w)
            gu = (_dot(lo.astype(_BF16), wgu_bf[:PACKED]) +
                  _dot(hi.astype(_BF16), wgu_bf[PACKED:]))
            h = _silu(gu[:, :EXPERT_DIM]) * gu[:, EXPERT_DIM:]
            y = _pack_halves(_dot(h.astype(_BF16), wd_bf[...]))
            for c in range(PIECES):
                obuf[oslot, c, r0:r0 + EXPERT_SUB, :] = y[:, c * LANES:(c + 1) * LANES]

        n_sub = ch // EXPERT_SUB

        @pl.when(valid >= ch)
        def _():
            for s in range(n_sub):
                sub_block(s)

        @pl.when(valid < ch)
        def _():
            for s in range(n_sub):
                @pl.when(s * EXPERT_SUB < valid)
                def _():
                    sub_block(s)

        out_copy(g).start()
        return carry

    lax.fori_loop(0, nchunk_ref[e], chunk, 0)

    @pl.when(e == pl.num_programs(0) - 1)
    def _():
        for back in range(OUT_SLOTS, 0, -1):
            @pl.when(total >= back)
            def _():
                out_copy(total - back).wait()


def _experts(layer, chunk0, nchunk, counts, total, xs, w_gate, w_up, w_down):
    d = D_MODEL
    ch = EXPERT_ROWS
    wspec_in = pl.BlockSpec((1, 1, d, EXPERT_DIM), lambda e, *_: (layer, e, 0, 0))
    any_spec = pl.BlockSpec(memory_space=pl.ANY)
    grid_spec = pltpu.PrefetchScalarGridSpec(
        num_scalar_prefetch=4,
        grid=(N_EXPERTS,),
        in_specs=[
            any_spec,
            wspec_in,
            wspec_in,
            pl.BlockSpec((1, 1, EXPERT_DIM, d), lambda e, *_: (layer, e, 0, 0)),
        ],
        out_specs=any_spec,
        scratch_shapes=[pltpu.VMEM((IN_SLOTS, PIECES, ch, LANES), _U32),
                        pltpu.VMEM((OUT_SLOTS, PIECES, ch, LANES), _U32),
                        pltpu.VMEM((d, 2 * EXPERT_DIM), _BF16),
                        pltpu.VMEM((EXPERT_DIM, d), _BF16),
                        pltpu.SemaphoreType.DMA((IN_SLOTS,)),
                        pltpu.SemaphoreType.DMA((OUT_SLOTS,))],
    )
    return pl.pallas_call(
        _expert_kernel,
        grid_spec=grid_spec,
        out_shape=jax.ShapeDtypeStruct(xs.shape, _U32),
        compiler_params=pltpu.CompilerParams(
            dimension_semantics=("arbitrary",),
            vmem_limit_bytes=VMEM_LIMIT),
        name="moe_experts",
    )(chunk0, nchunk, counts, total, xs, w_gate, w_up, w_down)


def _combine_kernel(x_ref, p_ref, gate_ref, sg_ref, su_ref, sd_ref, g_ref, b_ref, o_ref):
    x = x_ref[...]
    xb = x.astype(_BF16)
    gates = gate_ref[...]
    r_lo = r_hi = None
    for k in range(TOP_K):
        w = jnp.concatenate([p_ref[c, k] for c in range(PIECES)], axis=1)
        lo, hi = _unpack_halves(w)
        gk = gates[:, k:k + 1]
        r_lo = gk * lo if r_lo is None else r_lo + gk * lo
        r_hi = gk * hi if r_hi is None else r_hi + gk * hi
    routed = jnp.concatenate([r_lo, r_hi], axis=1)
    h = _silu(_dot(xb, sg_ref[...])) * _dot(xb, su_ref[...])
    shared = _dot(h.astype(_BF16), sd_ref[...])
    o_ref[...] = _layer_norm(DEEPNORM_ALPHA * x + (routed + shared), g_ref[...], b_ref[...])


def _combine(x, picked, gates, sh_gate, sh_up, sh_down, ln_g, ln_b):
    n_tok, d = x.shape
    ts = COMBINE_TILE
    const2 = lambda i: (0, 0)
    return pl.pallas_call(
        _combine_kernel,
        grid=(n_tok // ts,),
        in_specs=[
            pl.BlockSpec((ts, d), lambda i: (i, 0)),
            pl.BlockSpec((PIECES, TOP_K, ts, LANES), lambda i: (0, 0, i, 0)),
            pl.BlockSpec((ts, TOP_K), lambda i: (i, 0)),
            pl.BlockSpec((d, EXPERT_DIM), const2),
            pl.BlockSpec((d, EXPERT_DIM), const2),
            pl.BlockSpec((EXPERT_DIM, d), const2),
            pl.BlockSpec((1, d), const2),
            pl.BlockSpec((1, d), const2),
        ],
        out_specs=pl.BlockSpec((ts, d), lambda i: (i, 0)),
        out_shape=jax.ShapeDtypeStruct((n_tok, d), _F32),
        compiler_params=pltpu.CompilerParams(
            dimension_semantics=("arbitrary",),
            vmem_limit_bytes=VMEM_LIMIT),
        name="moe_combine",
    )(x, picked, gates, sh_gate.astype(_BF16), sh_up.astype(_BF16), sh_down.astype(_BF16),
      ln_g.reshape(1, d), ln_b.reshape(1, d))


def _moe_layer(layer, x, x_packed, w_router, router_bias, w_gate, w_up, w_down,
               sh_gate, sh_up, sh_down, ln_g, ln_b):
    n_tok, d = x.shape
    bm = EXPERT_ROWS
    e_k, r_k, g_k, counts = _route(x, w_router, router_bias)
    counts = counts.reshape(N_EXPERTS).astype(jnp.int32)
    padded = (counts + bm - 1) // bm * bm
    padded_end = jnp.cumsum(padded)
    padded_start = padded_end - padded
    n_rows = n_tok * TOP_K + N_EXPERTS * bm

    idx = _row_indices(padded_start, e_k, r_k, n_rows).reshape(PIECES * TOP_K, n_tok)
    xs = _sc_scatter_rows(x_packed.reshape(PIECES * n_tok, LANES), idx, PIECES * n_rows)
    rows = _experts(layer, padded_start // bm, padded // bm, counts,
                    (padded_end[-1:] // bm), xs.reshape(PIECES, n_rows, LANES),
                    w_gate, w_up, w_down)
    picked = _sc_gather_rows(rows.reshape(PIECES * n_rows, LANES), idx)
    picked = picked.reshape(PIECES, TOP_K, n_tok, LANES)
    return _combine(x, picked, g_k.T, sh_gate, sh_up, sh_down, ln_g, ln_b)


def kernel(x, pool_w_in, pool_w_grp, pool_scale, pool_w_out, sgu_w_in, sgu_b_in, sgu_ln_g, sgu_ln_b, sgu_w_s, sgu_b_s, sgu_w_out, ln_mix_g, ln_mix_b, moe_w_router, moe_router_bias, moe_w_gate, moe_w_up, moe_w_down, moe_sh_gate, moe_sh_up, moe_sh_down, ln_ffn_g, ln_ffn_b):
    bsz, seq, d = x.shape
    n_tok = bsz * seq

    def moe(i, h, h_packed):
        return _moe_layer(i, h, h_packed, moe_w_router[i], moe_router_bias[i], moe_w_gate,
                          moe_w_up, moe_w_down, moe_sh_gate[i], moe_sh_up[i],
                          moe_sh_down[i], ln_ffn_g[i], ln_ffn_b[i])

    h, h_packed = _pool_layer(x, pool_w_in[0], pool_w_grp[0], pool_scale[0], pool_w_out[0],
                              ln_mix_g[0], ln_mix_b[0])
    h = moe(0, h.reshape(n_tok, d), h_packed)
    h, h_packed = _sgu_layer(h, sgu_w_in[0], sgu_b_in[0], sgu_ln_g[0], sgu_ln_b[0], sgu_w_s[0],
                             sgu_b_s[0], sgu_w_out[0], ln_mix_g[1], ln_mix_b[1])
    h = moe(1, h, h_packed)
    return h.reshape(bsz, seq, d)
```

```python
import functools

import jax
import jax.numpy as jnp
from jax import lax
from jax.experimental import pallas as pl
from jax.experimental.pallas import tpu as pltpu
from jax.experimental.pallas import tpu_sc as plsc

D_MODEL = 1024
DEPTH = 2
POOL_WINDOWS = (2, 4, 8, 16)
POOL_GROUP_DIM = D_MODEL // len(POOL_WINDOWS)
POOL_HALO = 16
SGU_CHUNK = 128
SGU_HEADS = 4
SGU_WIDTH = 2 * D_MODEL
SGU_HEAD_DIM = SGU_WIDTH // SGU_HEADS
N_EXPERTS = 64
TOP_K = 8
N_GROUPS = 8
GROUP_SIZE = N_EXPERTS // N_GROUPS
TOPK_GROUPS = 4
EXPERT_DIM = D_MODEL // 4
ROUTED_SCALE = 2.5
DEEPNORM_ALPHA = (2 * DEPTH) ** 0.25
LN_EPS = 1e-5

LANES = 128
PACKED = D_MODEL // 2
PIECES = PACKED // LANES

POOL_TILE = 512
SGU_TILE = 256
ROUTE_TILE = 512
INDEX_TILE = 2048
EXPERT_ROWS = 512
EXPERT_SUB = 256
IN_SLOTS = 4
OUT_SLOTS = 3
COMBINE_TILE = 256
SC_WINDOW = 128
TOKEN_PARTS = 2
VMEM_LIMIT = 56 * 1024 * 1024

_F32 = jnp.float32
_BF16 = jnp.bfloat16
_U32 = jnp.uint32


def _dot(a, b):
    return jnp.dot(a, b, preferred_element_type=_F32)


def _layer_norm(h, g, b):
    mu = jnp.mean(h, axis=-1, keepdims=True)
    hc = h - mu
    var = jnp.mean(hc * hc, axis=-1, keepdims=True)
    return hc * lax.rsqrt(var + LN_EPS) * g + b


def _silu(x):
    return x * jax.nn.sigmoid(x)


def _gelu_tanh(x):
    c = 0.7978845608028654
    return 0.5 * x * (1.0 + jnp.tanh(c * (x + 0.044715 * (x * x * x))))


def _pack_halves(v):
    half = v.shape[1] // 2
    lo = lax.bitcast_convert_type(v[:, :half].astype(_BF16).astype(_F32), _U32)
    hi = lax.bitcast_convert_type(v[:, half:].astype(_BF16).astype(_F32), _U32)
    return (hi & _U32(0xFFFF0000)) | (lo >> 16)


def _unpack_halves(w):
    lo = lax.bitcast_convert_type(w << 16, _F32)
    hi = lax.bitcast_convert_type(w & _U32(0xFFFF0000), _F32)
    return lo, hi


def _store_pieces(ref, w):
    for c in range(PIECES):
        ref[c] = w[:, c * LANES:(c + 1) * LANES]


def _load_pieces(ref):
    return jnp.concatenate([ref[c] for c in range(PIECES)], axis=1)


def _pool_kernel(x_ref, win_ref, wgrp_ref, scale_ref, wout_ref, g_ref, b_ref,
                 o_ref, op_ref, zs_ref, y_ref):
    s = pl.program_id(1)
    ts = x_ref.shape[1]
    x = x_ref[0]
    z = _dot(x.astype(_BF16), win_ref[...])

    @pl.when(s == 0)
    def _():
        zs_ref[0:POOL_HALO, :] = jnp.zeros((POOL_HALO, D_MODEL), _F32)

    zs_ref[POOL_HALO:POOL_HALO + ts, :] = z
    pos = s * ts + lax.broadcasted_iota(jnp.int32, (ts, 1), 0)
    for g, w in enumerate(POOL_WINDOWS):
        c0 = g * POOL_GROUP_DIM
        c1 = c0 + POOL_GROUP_DIM
        zg = zs_ref[POOL_HALO:POOL_HALO + ts, c0:c1]
        acc = zg
        for k in range(1, w):
            acc = acc + zs_ref[POOL_HALO - k:POOL_HALO - k + ts, c0:c1]
        cnt = jnp.minimum(pos + 1, w).astype(_F32)
        pooled = acc / cnt - zg
        yg = _dot(pooled.astype(_BF16), wgrp_ref[g]) * scale_ref[:, c0:c1]
        y_ref[:, c0:c1] = yg.astype(_BF16)
    zs_ref[0:POOL_HALO, :] = zs_ref[ts:ts + POOL_HALO, :]
    mix = _dot(y_ref[...], wout_ref[...])
    out = _layer_norm(DEEPNORM_ALPHA * x + mix, g_ref[...], b_ref[...])
    o_ref[0] = out
    _store_pieces(op_ref, _pack_halves(out))


def _pool_layer(x, w_in, w_grp, scale, w_out, ln_g, ln_b):
    bsz, seq, d = x.shape
    ts = POOL_TILE
    steps = seq // ts
    const2 = lambda b, s: (0, 0)
    out_shape = (jax.ShapeDtypeStruct((bsz, seq, d), _F32),
                 jax.ShapeDtypeStruct((PIECES, bsz * seq, LANES), _U32))
    tile = pl.BlockSpec((1, ts, d), lambda b, s: (b, s, 0))
    ptile = pl.BlockSpec((PIECES, ts, LANES), lambda b, s: (0, b * steps + s, 0))
    return pl.pallas_call(
        _pool_kernel,
        grid=(bsz, steps),
        in_specs=[
            tile,
            pl.BlockSpec((d, d), const2),
            pl.BlockSpec((len(POOL_WINDOWS), POOL_GROUP_DIM, POOL_GROUP_DIM), lambda b, s: (0, 0, 0)),
            pl.BlockSpec((1, d), const2),
            pl.BlockSpec((d, d), const2),
            pl.BlockSpec((1, d), const2),
            pl.BlockSpec((1, d), const2),
        ],
        out_specs=(tile, ptile),
        out_shape=out_shape,
        scratch_shapes=[pltpu.VMEM((POOL_HALO + ts, d), _F32),
                        pltpu.VMEM((ts, d), _BF16)],
        compiler_params=pltpu.CompilerParams(
            dimension_semantics=("arbitrary", "arbitrary"),
            vmem_limit_bytes=VMEM_LIMIT),
        name="pool_mixer",
    )(x, w_in.astype(_BF16), w_grp.astype(_BF16), scale.reshape(1, d),
      w_out.astype(_BF16), ln_g.reshape(1, d), ln_b.reshape(1, d))


def _sgu_kernel(x_ref, win_ref, bin_ref, lng_ref, lnb_ref, ws_ref, bs_ref, wout_ref,
                g_ref, b_ref, o_ref, op_ref, gated_ref):
    ts = x_ref.shape[0]
    x = x_ref[...]
    xb = x.astype(_BF16)
    v = _gelu_tanh(_dot(xb, win_ref[:, SGU_WIDTH:]) + bin_ref[:, SGU_WIDTH:])
    v = _layer_norm(v, lng_ref[...], lnb_ref[...]).astype(_BF16)
    u = _gelu_tanh(_dot(xb, win_ref[:, :SGU_WIDTH]) + bin_ref[:, :SGU_WIDTH])
    for c in range(ts // SGU_CHUNK):
        r0 = c * SGU_CHUNK
        for h in range(SGU_HEADS):
            c0 = h * SGU_HEAD_DIM
            sv = _dot(ws_ref[h], v[r0:r0 + SGU_CHUNK, c0:c0 + SGU_HEAD_DIM]) + bs_ref[h]
            gated_ref[r0:r0 + SGU_CHUNK, c0:c0 + SGU_HEAD_DIM] = (
                u[r0:r0 + SGU_CHUNK, c0:c0 + SGU_HEAD_DIM] * sv).astype(_BF16)
    mix = _dot(gated_ref[...], wout_ref[...])
    out = _layer_norm(DEEPNORM_ALPHA * x + mix, g_ref[...], b_ref[...])
    o_ref[...] = out
    _store_pieces(op_ref, _pack_halves(out))


def _sgu_layer(x, w_in, b_in, ln_g, ln_b, w_s, b_s, w_out, mix_g, mix_b):
    n_tok, d = x.shape
    ts = SGU_TILE
    const2 = lambda i: (0, 0)
    const3 = lambda i: (0, 0, 0)
    causal = jnp.tril(jnp.ones((SGU_CHUNK, SGU_CHUNK), w_s.dtype))
    ws = (w_s * causal[None]).astype(_BF16)
    tile = pl.BlockSpec((ts, d), lambda i: (i, 0))
    ptile = pl.BlockSpec((PIECES, ts, LANES), lambda i: (0, i, 0))
    out_shape = (jax.ShapeDtypeStruct((n_tok, d), _F32),
                 jax.ShapeDtypeStruct((PIECES, n_tok, LANES), _U32))
    return pl.pallas_call(
        _sgu_kernel,
        grid=(n_tok // ts,),
        in_specs=[
            tile,
            pl.BlockSpec((d, 2 * SGU_WIDTH), const2),
            pl.BlockSpec((1, 2 * SGU_WIDTH), const2),
            pl.BlockSpec((1, SGU_WIDTH), const2),
            pl.BlockSpec((1, SGU_WIDTH), const2),
            pl.BlockSpec((SGU_HEADS, SGU_CHUNK, SGU_CHUNK), const3),
            pl.BlockSpec((SGU_HEADS, SGU_CHUNK, 1), const3),
            pl.BlockSpec((SGU_WIDTH, d), const2),
            pl.BlockSpec((1, d), const2),
            pl.BlockSpec((1, d), const2),
        ],
        out_specs=(tile, ptile),
        out_shape=out_shape,
        scratch_shapes=[pltpu.VMEM((ts, SGU_WIDTH), _BF16)],
        compiler_params=pltpu.CompilerParams(
            dimension_semantics=("arbitrary",),
            vmem_limit_bytes=VMEM_LIMIT),
        name="sgu_mixer",
    )(x, w_in.astype(_BF16), b_in.reshape(1, -1), ln_g.reshape(1, -1), ln_b.reshape(1, -1),
      ws, b_s.reshape(SGU_HEADS, SGU_CHUNK, 1), w_out.astype(_BF16),
      mix_g.reshape(1, d), mix_b.reshape(1, d))


def _route_kernel(x_ref, wrt_ref, bias_ref, ek_ref, rk_ref, gk_ref, cnt_ref, carry_ref):
    i = pl.program_id(0)
    ts = x_ref.shape[0]

    @pl.when(i == 0)
    def _():
        carry_ref[...] = jnp.zeros_like(carry_ref)

    logits = lax.dot_general(wrt_ref[...], x_ref[...], (((1,), (1,)), ((), ())),
                             precision=lax.Precision.HIGHEST,
                             preferred_element_type=_F32)
    scores = jax.nn.sigmoid(logits).reshape(N_GROUPS, GROUP_SIZE, ts)
    biased = scores + bias_ref[...]
    neg_inf = jnp.float32(-jnp.inf)
    shape3 = (N_GROUPS, GROUP_SIZE, ts)
    in_grp = lax.broadcasted_iota(jnp.int32, shape3, 1)
    grp = lax.broadcasted_iota(jnp.int32, shape3, 0)
    eid = grp * GROUP_SIZE + in_grp

    m1 = jnp.max(biased, axis=1, keepdims=True)
    first1 = jnp.min(jnp.where(biased == m1, in_grp, GROUP_SIZE), axis=1, keepdims=True)
    m2 = jnp.max(jnp.where(in_grp == first1, neg_inf, biased), axis=1, keepdims=True)
    gscore = m1 + m2

    gid = lax.broadcasted_iota(jnp.int32, (N_GROUPS, 1, ts), 0)
    gsel = jnp.zeros((N_GROUPS, 1, ts), jnp.bool_)
    for _ in range(TOPK_GROUPS):
        m = jnp.max(gscore, axis=0, keepdims=True)
        first = jnp.min(jnp.where(gscore == m, gid, N_GROUPS), axis=0, keepdims=True)
        pick = gid == first
        gsel = jnp.logical_or(gsel, pick)
        gscore = jnp.where(pick, neg_inf, gscore)

    masked = jnp.where(gsel, biased, neg_inf)
    picked_any = jnp.zeros(shape3, jnp.bool_)
    e_k, s_k = [], []
    for _ in range(TOP_K):
        m = jnp.max(jnp.max(masked, axis=1, keepdims=True), axis=0, keepdims=True)
        first = jnp.min(jnp.min(jnp.where(masked == m, eid, N_EXPERTS), axis=1, keepdims=True),
                        axis=0, keepdims=True)
        pick = eid == first
        picked_any = jnp.logical_or(picked_any, pick)
        masked = jnp.where(pick, neg_inf, masked)
        sc = jnp.sum(jnp.sum(jnp.where(pick, scores, 0.0), axis=1, keepdims=True),
                     axis=0, keepdims=True)
        e_k.append(first)
        s_k.append(sc)
    denom = s_k[0]
    for k in range(1, TOP_K):
        denom = denom + s_k[k]

    sel = picked_any.astype(_F32).reshape(N_EXPERTS, ts)
    row = lax.broadcasted_iota(jnp.int32, (ts, ts), 0)
    col = lax.broadcasted_iota(jnp.int32, (ts, ts), 1)
    earlier = (row < col).astype(_BF16)
    rank = _dot(sel.astype(_BF16), earlier) + carry_ref[...]
    rank3 = rank.reshape(shape3)
    carry_ref[...] += jnp.sum(sel, axis=1, keepdims=True)
    cnt_ref[...] = carry_ref[...]

    for k in range(TOP_K):
        r = jnp.sum(jnp.sum(jnp.where(eid == e_k[k], rank3, 0.0), axis=1, keepdims=True),
                    axis=0, keepdims=True)
        ek_ref[k:k + 1, :] = e_k[k].reshape(1, ts)
        rk_ref[k:k + 1, :] = r.reshape(1, ts).astype(jnp.int32)
        gk_ref[k:k + 1, :] = (s_k[k] / denom * ROUTED_SCALE).reshape(1, ts)


def _route(x, w_router, router_bias):
    n_tok, d = x.shape
    ts = ROUTE_TILE
    out_shape = (jax.ShapeDtypeStruct((TOP_K, n_tok), jnp.int32),
                 jax.ShapeDtypeStruct((TOP_K, n_tok), jnp.int32),
                 jax.ShapeDtypeStruct((TOP_K, n_tok), _F32),
                 jax.ShapeDtypeStruct((N_EXPERTS, 1), _F32))
    kspec = pl.BlockSpec((TOP_K, ts), lambda i: (0, i))
    return pl.pallas_call(
        _route_kernel,
        grid=(n_tok // ts,),
        in_specs=[
            pl.BlockSpec((ts, d), lambda i: (i, 0)),
            pl.BlockSpec((N_EXPERTS, d), lambda i: (0, 0)),
            pl.BlockSpec((N_GROUPS, GROUP_SIZE, 1), lambda i: (0, 0, 0)),
        ],
        out_specs=(kspec, kspec, kspec, pl.BlockSpec((N_EXPERTS, 1), lambda i: (0, 0))),
        out_shape=out_shape,
        scratch_shapes=[pltpu.VMEM((N_EXPERTS, 1), _F32)],
        compiler_params=pltpu.CompilerParams(
            dimension_semantics=("arbitrary",),
            vmem_limit_bytes=VMEM_LIMIT),
        name="moe_route",
    )(x, w_router.T, router_bias.reshape(N_GROUPS, GROUP_SIZE, 1))


def _sc_mesh():
    return plsc.VectorSubcoreMesh(core_axis_name="core", subcore_axis_name="subcore")


def _sc_scatter_rows(src, idx, n_out):
    d = src.shape[1]
    n_tok = idx.shape[1]
    wins = n_tok // SC_WINDOW

    def body(src_hbm, idx_hbm, out_hbm):
        def step(src_vmem, idx_vmem):
            for k in range(TOP_K):
                pltpu.sync_copy(src_vmem, out_hbm.at[idx_vmem.at[k]])

        pltpu.emit_pipeline(
            step,
            grid=(src.shape[0] // SC_WINDOW,),
            in_specs=[pl.BlockSpec((SC_WINDOW, d), index_map=lambda i: (i, 0)),
                      pl.BlockSpec((TOP_K, SC_WINDOW), index_map=lambda i: (i // wins, i % wins))],
            out_specs=[],
            core_axis_name=("core", "subcore"),
            dimension_semantics=(pltpu.PARALLEL,),
        )(src_hbm, idx_hbm)

    return pl.kernel(body, out_type=jax.ShapeDtypeStruct((n_out, d), src.dtype),
                     mesh=_sc_mesh(), scratch_types=[], name="sc_scatter_rows")(src, idx)


def _sc_gather_rows(table, idx, part, n_parts):
    d = table.shape[1]
    n_idx_rows, n_tok = idx.shape
    wins = n_tok // n_parts // SC_WINDOW

    def body(table_hbm, idx_hbm, out_hbm):
        def step(idx_vmem, out_vmem):
            pltpu.sync_copy(table_hbm.at[idx_vmem.at[0]], out_vmem)

        pltpu.emit_pipeline(
            step,
            grid=(n_idx_rows * wins,),
            in_specs=[pl.BlockSpec((1, SC_WINDOW),
                                   index_map=lambda i: (i // wins, part * wins + i % wins))],
            out_specs=[pl.BlockSpec((SC_WINDOW, d), index_map=lambda i: (i, 0))],
            core_axis_name=("core", "subcore"),
            dimension_semantics=(pltpu.PARALLEL,),
        )(idx_hbm, out_hbm)

    out_type = jax.ShapeDtypeStruct((n_idx_rows * n_tok // n_parts, d), table.dtype)
    return pl.kernel(body, out_type=out_type, mesh=_sc_mesh(), scratch_types=[],
                     name="sc_gather_rows")(table, idx)


def _index_kernel(start_ref, ek_ref, rk_ref, idx_ref, *, n_rows):
    e = ek_ref[...]
    start = jnp.zeros(e.shape, jnp.int32)
    for ex in range(N_EXPERTS):
        start = jnp.where(e == ex, start_ref[ex], start)
    dest = start + rk_ref[...]
    for c in range(PIECES):
        idx_ref[c] = dest + c * n_rows


def _row_indices(padded_start, e_k, r_k, n_rows):
    n_tok = e_k.shape[1]
    ts = INDEX_TILE
    kspec = pl.BlockSpec((TOP_K, ts), lambda i, st: (0, i))
    grid_spec = pltpu.PrefetchScalarGridSpec(
        num_scalar_prefetch=1,
        grid=(n_tok // ts,),
        in_specs=[kspec, kspec],
        out_specs=pl.BlockSpec((PIECES, TOP_K, ts), lambda i, st: (0, 0, i)),
    )
    return pl.pallas_call(
        functools.partial(_index_kernel, n_rows=n_rows),
        grid_spec=grid_spec,
        out_shape=jax.ShapeDtypeStruct((PIECES, TOP_K, n_tok), jnp.int32),
        compiler_params=pltpu.CompilerParams(dimension_semantics=("arbitrary",)),
        name="moe_row_indices",
    )(padded_start, e_k, r_k)


def _expert_kernel(chunk0_ref, nchunk_ref, cnt_ref, total_ref,
                   xs_hbm, wg_ref, wu_ref, wd_ref, o_hbm,
                   xbuf, obuf, wgu_bf, wd_bf, sem_in, sem_out):
    e = pl.program_id(0)
    total = total_ref[0]
    ch = EXPERT_ROWS
    ahead = IN_SLOTS - 1

    def in_copy(g):
        slot = g % IN_SLOTS
        return pltpu.make_async_copy(xs_hbm.at[:, pl.ds(g * ch, ch), :], xbuf.at[slot],
                                     sem_in.at[slot])

    def out_copy(g):
        slot = g % OUT_SLOTS
        return pltpu.make_async_copy(obuf.at[slot], o_hbm.at[:, pl.ds(g * ch, ch), :],
                                     sem_out.at[slot])

    @pl.when(e == 0)
    def _():
        obuf[...] = jnp.zeros_like(obuf)
        for g in range(ahead):
            @pl.when(g < total)
            def _():
                in_copy(g).start()

    wgu_bf[:, :EXPERT_DIM] = wg_ref[0, 0].astype(_BF16)
    wgu_bf[:, EXPERT_DIM:] = wu_ref[0, 0].astype(_BF16)
    wd_bf[...] = wd_ref[0, 0].astype(_BF16)

    def chunk(j, carry):
        g = chunk0_ref[e] + j
        in_copy(g).wait()

        @pl.when(g + ahead < total)
        def _():
            in_copy(g + ahead).start()

        @pl.when(g >= OUT_SLOTS)
        def _():
            out_copy(g - OUT_SLOTS).wait()

        islot = g % IN_SLOTS
        oslot = g % OUT_SLOTS
        valid = cnt_ref[e] - j * ch

        def sub_block(s):
            r0 = s * EXPERT_SUB
            w = jnp.concatenate([xbuf[islot, c, r0:r0 + EXPERT_SUB, :] for c in range(PIECES)],
                                axis=1)
            lo, hi = _unpack_halves(w)
            gu = (_dot(lo.astype(_BF16), wgu_bf[:PACKED]) +
                  _dot(hi.astype(_BF16), wgu_bf[PACKED:]))
            h = _silu(gu[:, :EXPERT_DIM]) * gu[:, EXPERT_DIM:]
            y = _pack_halves(_dot(h.astype(_BF16), wd_bf[...]))
            for c in range(PIECES):
                obuf[oslot, c, r0:r0 + EXPERT_SUB, :] = y[:, c * LANES:(c + 1) * LANES]

        n_sub = ch // EXPERT_SUB

        @pl.when(valid >= ch)
        def _():
            for s in range(n_sub):
                sub_block(s)

        @pl.when(valid < ch)
        def _():
            for s in range(n_sub):
                @pl.when(s * EXPERT_SUB < valid)
                def _():
                    sub_block(s)

        out_copy(g).start()
        return carry

    lax.fori_loop(0, nchunk_ref[e], chunk, 0)

    @pl.when(e == pl.num_programs(0) - 1)
    def _():
        for back in range(OUT_SLOTS, 0, -1):
            @pl.when(total >= back)
            def _():
                out_copy(total - back).wait()


def _experts(layer, chunk0, nchunk, counts, total, xs, w_gate, w_up, w_down):
    d = D_MODEL
    ch = EXPERT_ROWS
    wspec_in = pl.BlockSpec((1, 1, d, EXPERT_DIM), lambda e, *_: (layer, e, 0, 0))
    any_spec = pl.BlockSpec(memory_space=pl.ANY)
    grid_spec = pltpu.PrefetchScalarGridSpec(
        num_scalar_prefetch=4,
        grid=(N_EXPERTS,),
        in_specs=[
            any_spec,
            wspec_in,
            wspec_in,
            pl.BlockSpec((1, 1, EXPERT_DIM, d), lambda e, *_: (layer, e, 0, 0)),
        ],
        out_specs=any_spec,
        scratch_shapes=[pltpu.VMEM((IN_SLOTS, PIECES, ch, LANES), _U32),
                        pltpu.VMEM((OUT_SLOTS, PIECES, ch, LANES), _U32),
                        pltpu.VMEM((d, 2 * EXPERT_DIM), _BF16),
                        pltpu.VMEM((EXPERT_DIM, d), _BF16),
                        pltpu.SemaphoreType.DMA((IN_SLOTS,)),
                        pltpu.SemaphoreType.DMA((OUT_SLOTS,))],
    )
    return pl.pallas_call(
        _expert_kernel,
        grid_spec=grid_spec,
        out_shape=jax.ShapeDtypeStruct(xs.shape, _U32),
        compiler_params=pltpu.CompilerParams(
            dimension_semantics=("arbitrary",),
            vmem_limit_bytes=VMEM_LIMIT),
        name="moe_experts",
    )(chunk0, nchunk, counts, total, xs, w_gate, w_up, w_down)


def _combine_kernel(x_ref, p_ref, gate_ref, sg_ref, su_ref, sd_ref, g_ref, b_ref, *rest):
    o_ref = rest[-1]
    x = x_ref[...]
    xb = x.astype(_BF16)
    gates = gate_ref[...]
    r_lo = r_hi = None
    for k in range(TOP_K):
        w = jnp.concatenate([p_ref[c, k] for c in range(PIECES)], axis=1)
        lo, hi = _unpack_halves(w)
        gk = gates[:, k:k + 1]
        r_lo = gk * lo if r_lo is None else r_lo + gk * lo
        r_hi = gk * hi if r_hi is None else r_hi + gk * hi
    routed = jnp.concatenate([r_lo, r_hi], axis=1)
    h = _silu(_dot(xb, sg_ref[...])) * _dot(xb, su_ref[...])
    shared = _dot(h.astype(_BF16), sd_ref[...])
    o_ref[...] = _layer_norm(DEEPNORM_ALPHA * x + (routed + shared), g_ref[...], b_ref[...])


def _combine(x, picked, gates, sh_gate, sh_up, sh_down, ln_g, ln_b, part, n_parts, prev):
    n_tok, d = x.shape
    ts = COMBINE_TILE
    steps = n_tok // n_parts // ts
    const2 = lambda i: (0, 0)
    rows_map = lambda i: (part * steps + i, 0)
    in_specs = [
        pl.BlockSpec((ts, d), rows_map),
        pl.BlockSpec((PIECES, TOP_K, ts, LANES), lambda i: (0, 0, i, 0)),
        pl.BlockSpec((ts, TOP_K), rows_map),
        pl.BlockSpec((d, EXPERT_DIM), const2),
        pl.BlockSpec((d, EXPERT_DIM), const2),
        pl.BlockSpec((EXPERT_DIM, d), const2),
        pl.BlockSpec((1, d), const2),
        pl.BlockSpec((1, d), const2),
    ]
    args = [x, picked, gates, sh_gate.astype(_BF16), sh_up.astype(_BF16), sh_down.astype(_BF16),
            ln_g.reshape(1, d), ln_b.reshape(1, d)]
    aliases = {}
    if prev is not None:
        in_specs.append(pl.BlockSpec(memory_space=pl.ANY))
        aliases = {len(args): 0}
        args.append(prev)
    return pl.pallas_call(
        _combine_kernel,
        grid=(steps,),
        in_specs=in_specs,
        out_specs=pl.BlockSpec((ts, d), rows_map),
        out_shape=jax.ShapeDtypeStruct((n_tok, d), _F32),
        input_output_aliases=aliases,
        compiler_params=pltpu.CompilerParams(
            dimension_semantics=("arbitrary",),
            vmem_limit_bytes=VMEM_LIMIT),
        name="moe_combine",
    )(*args)


def _moe_layer(layer, x, x_packed, w_router, router_bias, w_gate, w_up, w_down,
               sh_gate, sh_up, sh_down, ln_g, ln_b):
    n_tok, d = x.shape
    bm = EXPERT_ROWS
    e_k, r_k, g_k, counts = _route(x, w_router, router_bias)
    counts = counts.reshape(N_EXPERTS).astype(jnp.int32)
    padded = (counts + bm - 1) // bm * bm
    padded_end = jnp.cumsum(padded)
    padded_start = padded_end - padded
    n_rows = n_tok * TOP_K + N_EXPERTS * bm

    idx = _row_indices(padded_start, e_k, r_k, n_rows).reshape(PIECES * TOP_K, n_tok)
    xs = _sc_scatter_rows(x_packed.reshape(PIECES * n_tok, LANES), idx, PIECES * n_rows)
    rows = _experts(layer, padded_start // bm, padded // bm, counts,
                    (padded_end[-1:] // bm), xs.reshape(PIECES, n_rows, LANES),
                    w_gate, w_up, w_down)
    rows = rows.reshape(PIECES * n_rows, LANES)
    gates = g_k.T
    out = None
    for part in range(TOKEN_PARTS):
        picked = _sc_gather_rows(rows, idx, part, TOKEN_PARTS)
        picked = picked.reshape(PIECES, TOP_K, n_tok // TOKEN_PARTS, LANES)
        out = _combine(x, picked, gates, sh_gate, sh_up, sh_down, ln_g, ln_b,
                       part, TOKEN_PARTS, out)
    return out


def kernel(x, pool_w_in, pool_w_grp, pool_scale, pool_w_out, sgu_w_in, sgu_b_in, sgu_ln_g, sgu_ln_b, sgu_w_s, sgu_b_s, sgu_w_out, ln_mix_g, ln_mix_b, moe_w_router, moe_router_bias, moe_w_gate, moe_w_up, moe_w_down, moe_sh_gate, moe_sh_up, moe_sh_down, ln_ffn_g, ln_ffn_b):
    bsz, seq, d = x.shape
    n_tok = bsz * seq

    def moe(i, h, h_packed):
        return _moe_layer(i, h, h_packed, moe_w_router[i], moe_router_bias[i], moe_w_gate,
                          moe_w_up, moe_w_down, moe_sh_gate[i], moe_sh_up[i],
                          moe_sh_down[i], ln_ffn_g[i], ln_ffn_b[i])

    h, h_packed = _pool_layer(x, pool_w_in[0], pool_w_grp[0], pool_scale[0], pool_w_out[0],
                              ln_mix_g[0], ln_mix_b[0])
    h = moe(0, h.reshape(n_tok, d), h_packed)
    h, h_packed = _sgu_layer(h, sgu_w_in[0], sgu_b_in[0], sgu_ln_g[0], sgu_ln_b[0], sgu_w_s[0],
                             sgu_b_s[0], sgu_w_out[0], ln_mix_g[1], ln_mix_b[1])
    h = moe(1, h, h_packed)
    return h.reshape(bsz, seq, d)
```

```python
import functools

import jax
import jax.numpy as jnp
from jax import lax
from jax.experimental import pallas as pl
from jax.experimental.pallas import tpu as pltpu
from jax.experimental.pallas import tpu_sc as plsc

D_MODEL = 1024
DEPTH = 2
POOL_WINDOWS = (2, 4, 8, 16)
POOL_GROUP_DIM = D_MODEL // len(POOL_WINDOWS)
POOL_HALO = 16
SGU_CHUNK = 128
SGU_HEADS = 4
SGU_WIDTH = 2 * D_MODEL
SGU_HEAD_DIM = SGU_WIDTH // SGU_HEADS
N_EXPERTS = 64
TOP_K = 8
N_GROUPS = 8
GROUP_SIZE = N_EXPERTS // N_GROUPS
TOPK_GROUPS = 4
EXPERT_DIM = D_MODEL // 4
ROUTED_SCALE = 2.5
DEEPNORM_ALPHA = (2 * DEPTH) ** 0.25
LN_EPS = 1e-5

LANES = 128
PACKED = D_MODEL // 2
PIECES = PACKED // LANES

POOL_TILE = 512
SGU_TILE = 256
ROUTE_TILE = 512
INDEX_TILE = 2048
EXPERT_ROWS = 512
EXPERT_SUB = 256
IN_SLOTS = 4
OUT_SLOTS = 3
COMBINE_TILE = 256
SC_WINDOW = 128
TOKEN_PARTS = 2
VMEM_LIMIT = 56 * 1024 * 1024

_F32 = jnp.float32
_BF16 = jnp.bfloat16
_U32 = jnp.uint32


def _dot(a, b):
    return jnp.dot(a, b, preferred_element_type=_F32)


def _layer_norm(h, g, b):
    mu = jnp.mean(h, axis=-1, keepdims=True)
    hc = h - mu
    var = jnp.mean(hc * hc, axis=-1, keepdims=True)
    return hc * lax.rsqrt(var + LN_EPS) * g + b


def _silu(x):
    return x * jax.nn.sigmoid(x)


def _gelu_tanh(x):
    c = 0.7978845608028654
    return 0.5 * x * (1.0 + jnp.tanh(c * (x + 0.044715 * (x * x * x))))


def _pack_halves(v):
    half = v.shape[1] // 2
    lo = lax.bitcast_convert_type(v[:, :half].astype(_BF16).astype(_F32), _U32)
    hi = lax.bitcast_convert_type(v[:, half:].astype(_BF16).astype(_F32), _U32)
    return (hi & _U32(0xFFFF0000)) | (lo >> 16)


def _unpack_halves(w):
    lo = lax.bitcast_convert_type(w << 16, _F32)
    hi = lax.bitcast_convert_type(w & _U32(0xFFFF0000), _F32)
    return lo, hi


def _store_pieces(ref, w):
    for c in range(PIECES):
        ref[c] = w[:, c * LANES:(c + 1) * LANES]


def _load_pieces(ref):
    return jnp.concatenate([ref[c] for c in range(PIECES)], axis=1)


def _pool_kernel(x_ref, win_ref, wgrp_ref, scale_ref, wout_ref, g_ref, b_ref,
                 o_ref, op_ref, zs_ref, y_ref):
    s = pl.program_id(1)
    ts = x_ref.shape[1]
    x = x_ref[0]
    z = _dot(x.astype(_BF16), win_ref[...])

    @pl.when(s == 0)
    def _():
        zs_ref[0:POOL_HALO, :] = jnp.zeros((POOL_HALO, D_MODEL), _F32)

    zs_ref[POOL_HALO:POOL_HALO + ts, :] = z
    pos = s * ts + lax.broadcasted_iota(jnp.int32, (ts, 1), 0)
    for g, w in enumerate(POOL_WINDOWS):
        c0 = g * POOL_GROUP_DIM
        c1 = c0 + POOL_GROUP_DIM
        zg = zs_ref[POOL_HALO:POOL_HALO + ts, c0:c1]
        acc = zg
        for k in range(1, w):
            acc = acc + zs_ref[POOL_HALO - k:POOL_HALO - k + ts, c0:c1]
        cnt = jnp.minimum(pos + 1, w).astype(_F32)
        pooled = acc / cnt - zg
        yg = _dot(pooled.astype(_BF16), wgrp_ref[g]) * scale_ref[:, c0:c1]
        y_ref[:, c0:c1] = yg.astype(_BF16)
    zs_ref[0:POOL_HALO, :] = zs_ref[ts:ts + POOL_HALO, :]
    mix = _dot(y_ref[...], wout_ref[...])
    out = _layer_norm(DEEPNORM_ALPHA * x + mix, g_ref[...], b_ref[...])
    o_ref[0] = out
    _store_pieces(op_ref, _pack_halves(out))


def _pool_layer(x, w_in, w_grp, scale, w_out, ln_g, ln_b):
    bsz, seq, d = x.shape
    ts = POOL_TILE
    steps = seq // ts
    const2 = lambda b, s: (0, 0)
    out_shape = (jax.ShapeDtypeStruct((bsz, seq, d), _F32),
                 jax.ShapeDtypeStruct((PIECES, bsz * seq, LANES), _U32))
    tile = pl.BlockSpec((1, ts, d), lambda b, s: (b, s, 0))
    ptile = pl.BlockSpec((PIECES, ts, LANES), lambda b, s: (0, b * steps + s, 0))
    return pl.pallas_call(
        _pool_kernel,
        grid=(bsz, steps),
        in_specs=[
            tile,
            pl.BlockSpec((d, d), const2),
            pl.BlockSpec((len(POOL_WINDOWS), POOL_GROUP_DIM, POOL_GROUP_DIM), lambda b, s: (0, 0, 0)),
            pl.BlockSpec((1, d), const2),
            pl.BlockSpec((d, d), const2),
            pl.BlockSpec((1, d), const2),
            pl.BlockSpec((1, d), const2),
        ],
        out_specs=(tile, ptile),
        out_shape=out_shape,
        scratch_shapes=[pltpu.VMEM((POOL_HALO + ts, d), _F32),
                        pltpu.VMEM((ts, d), _BF16)],
        compiler_params=pltpu.CompilerParams(
            dimension_semantics=("arbitrary", "arbitrary"),
            vmem_limit_bytes=VMEM_LIMIT),
        name="pool_mixer",
    )(x, w_in.astype(_BF16), w_grp.astype(_BF16), scale.reshape(1, d),
      w_out.astype(_BF16), ln_g.reshape(1, d), ln_b.reshape(1, d))


def _sgu_kernel(x_ref, win_ref, bin_ref, lng_ref, lnb_ref, ws_ref, bs_ref, wout_ref,
                g_ref, b_ref, o_ref, op_ref, gated_ref):
    ts = x_ref.shape[0]
    x = x_ref[...]
    xb = x.astype(_BF16)
    v = _gelu_tanh(_dot(xb, win_ref[:, SGU_WIDTH:]) + bin_ref[:, SGU_WIDTH:])
    v = _layer_norm(v, lng_ref[...], lnb_ref[...]).astype(_BF16)
    u = _gelu_tanh(_dot(xb, win_ref[:, :SGU_WIDTH]) + bin_ref[:, :SGU_WIDTH])
    for c in range(ts // SGU_CHUNK):
        r0 = c * SGU_CHUNK
        for h in range(SGU_HEADS):
            c0 = h * SGU_HEAD_DIM
            sv = _dot(ws_ref[h], v[r0:r0 + SGU_CHUNK, c0:c0 + SGU_HEAD_DIM]) + bs_ref[h]
            gated_ref[r0:r0 + SGU_CHUNK, c0:c0 + SGU_HEAD_DIM] = (
                u[r0:r0 + SGU_CHUNK, c0:c0 + SGU_HEAD_DIM] * sv).astype(_BF16)
    mix = _dot(gated_ref[...], wout_ref[...])
    out = _layer_norm(DEEPNORM_ALPHA * x + mix, g_ref[...], b_ref[...])
    o_ref[...] = out
    _store_pieces(op_ref, _pack_halves(out))


def _sgu_layer(x, w_in, b_in, ln_g, ln_b, w_s, b_s, w_out, mix_g, mix_b):
    n_tok, d = x.shape
    ts = SGU_TILE
    const2 = lambda i: (0, 0)
    const3 = lambda i: (0, 0, 0)
    causal = jnp.tril(jnp.ones((SGU_CHUNK, SGU_CHUNK), w_s.dtype))
    ws = (w_s * causal[None]).astype(_BF16)
    tile = pl.BlockSpec((ts, d), lambda i: (i, 0))
    ptile = pl.BlockSpec((PIECES, ts, LANES), lambda i: (0, i, 0))
    out_shape = (jax.ShapeDtypeStruct((n_tok, d), _F32),
                 jax.ShapeDtypeStruct((PIECES, n_tok, LANES), _U32))
    return pl.pallas_call(
        _sgu_kernel,
        grid=(n_tok // ts,),
        in_specs=[
            tile,
            pl.BlockSpec((d, 2 * SGU_WIDTH), const2),
            pl.BlockSpec((1, 2 * SGU_WIDTH), const2),
            pl.BlockSpec((1, SGU_WIDTH), const2),
            pl.BlockSpec((1, SGU_WIDTH), const2),
            pl.BlockSpec((SGU_HEADS, SGU_CHUNK, SGU_CHUNK), const3),
            pl.BlockSpec((SGU_HEADS, SGU_CHUNK, 1), const3),
            pl.BlockSpec((SGU_WIDTH, d), const2),
            pl.BlockSpec((1, d), const2),
            pl.BlockSpec((1, d), const2),
        ],
        out_specs=(tile, ptile),
        out_shape=out_shape,
        scratch_shapes=[pltpu.VMEM((ts, SGU_WIDTH), _BF16)],
        compiler_params=pltpu.CompilerParams(
            dimension_semantics=("arbitrary",),
            vmem_limit_bytes=VMEM_LIMIT),
        name="sgu_mixer",
    )(x, w_in.astype(_BF16), b_in.reshape(1, -1), ln_g.reshape(1, -1), ln_b.reshape(1, -1),
      ws, b_s.reshape(SGU_HEADS, SGU_CHUNK, 1), w_out.astype(_BF16),
      mix_g.reshape(1, d), mix_b.reshape(1, d))


def _route_kernel(x_ref, wrt_ref, bias_ref, ek_ref, rk_ref, gk_ref, cnt_ref, carry_ref):
    i = pl.program_id(0)
    ts = x_ref.shape[0]

    @pl.when(i == 0)
    def _():
        carry_ref[...] = jnp.zeros_like(carry_ref)

    logits = lax.dot_general(wrt_ref[...], x_ref[...], (((1,), (1,)), ((), ())),
                             precision=lax.Precision.HIGHEST,
                             preferred_element_type=_F32)
    scores = jax.nn.sigmoid(logits).reshape(N_GROUPS, GROUP_SIZE, ts)
    biased = scores + bias_ref[...]
    neg_inf = jnp.float32(-jnp.inf)
    shape3 = (N_GROUPS, GROUP_SIZE, ts)
    in_grp = lax.broadcasted_iota(jnp.int32, shape3, 1)
    grp = lax.broadcasted_iota(jnp.int32, shape3, 0)
    eid = grp * GROUP_SIZE + in_grp

    m1 = jnp.max(biased, axis=1, keepdims=True)
    first1 = jnp.min(jnp.where(biased == m1, in_grp, GROUP_SIZE), axis=1, keepdims=True)
    m2 = jnp.max(jnp.where(in_grp == first1, neg_inf, biased), axis=1, keepdims=True)
    gscore = m1 + m2

    gid = lax.broadcasted_iota(jnp.int32, (N_GROUPS, 1, ts), 0)
    gsel = jnp.zeros((N_GROUPS, 1, ts), jnp.bool_)
    for _ in range(TOPK_GROUPS):
        m = jnp.max(gscore, axis=0, keepdims=True)
        first = jnp.min(jnp.where(gscore == m, gid, N_GROUPS), axis=0, keepdims=True)
        pick = gid == first
        gsel = jnp.logical_or(gsel, pick)
        gscore = jnp.where(pick, neg_inf, gscore)

    masked = jnp.where(gsel, biased, neg_inf)
    picked_any = jnp.zeros(shape3, jnp.bool_)
    e_k, s_k = [], []
    for _ in range(TOP_K):
        m = jnp.max(jnp.max(masked, axis=1, keepdims=True), axis=0, keepdims=True)
        first = jnp.min(jnp.min(jnp.where(masked == m, eid, N_EXPERTS), axis=1, keepdims=True),
                        axis=0, keepdims=True)
        pick = eid == first
        picked_any = jnp.logical_or(picked_any, pick)
        masked = jnp.where(pick, neg_inf, masked)
        sc = jnp.sum(jnp.sum(jnp.where(pick, scores, 0.0), axis=1, keepdims=True),
                     axis=0, keepdims=True)
        e_k.append(first)
        s_k.append(sc)
    denom = s_k[0]
    for k in range(1, TOP_K):
        denom = denom + s_k[k]

    sel = picked_any.astype(_F32).reshape(N_EXPERTS, ts)
    row = lax.broadcasted_iota(jnp.int32, (ts, ts), 0)
    col = lax.broadcasted_iota(jnp.int32, (ts, ts), 1)
    earlier = (row < col).astype(_BF16)
    rank = _dot(sel.astype(_BF16), earlier) + carry_ref[...]
    rank3 = rank.reshape(shape3)
    carry_ref[...] += jnp.sum(sel, axis=1, keepdims=True)
    cnt_ref[...] = carry_ref[...]

    for k in range(TOP_K):
        r = jnp.sum(jnp.sum(jnp.where(eid == e_k[k], rank3, 0.0), axis=1, keepdims=True),
                    axis=0, keepdims=True)
        ek_ref[k:k + 1, :] = e_k[k].reshape(1, ts)
        rk_ref[k:k + 1, :] = r.reshape(1, ts).astype(jnp.int32)
        gk_ref[k:k + 1, :] = (s_k[k] / denom * ROUTED_SCALE).reshape(1, ts)


def _route(x, w_router, router_bias, part, n_parts):
    n_tok, d = x.shape
    ts = ROUTE_TILE
    n_part = n_tok // n_parts
    steps = n_part // ts
    out_shape = (jax.ShapeDtypeStruct((TOP_K, n_part), jnp.int32),
                 jax.ShapeDtypeStruct((TOP_K, n_part), jnp.int32),
                 jax.ShapeDtypeStruct((TOP_K, n_part), _F32),
                 jax.ShapeDtypeStruct((N_EXPERTS, 1), _F32))
    kspec = pl.BlockSpec((TOP_K, ts), lambda i: (0, i))
    return pl.pallas_call(
        _route_kernel,
        grid=(steps,),
        in_specs=[
            pl.BlockSpec((ts, d), lambda i: (part * steps + i, 0)),
            pl.BlockSpec((N_EXPERTS, d), lambda i: (0, 0)),
            pl.BlockSpec((N_GROUPS, GROUP_SIZE, 1), lambda i: (0, 0, 0)),
        ],
        out_specs=(kspec, kspec, kspec, pl.BlockSpec((N_EXPERTS, 1), lambda i: (0, 0))),
        out_shape=out_shape,
        scratch_shapes=[pltpu.VMEM((N_EXPERTS, 1), _F32)],
        compiler_params=pltpu.CompilerParams(
            dimension_semantics=("arbitrary",),
            vmem_limit_bytes=VMEM_LIMIT),
        name="moe_route",
    )(x, w_router.T, router_bias.reshape(N_GROUPS, GROUP_SIZE, 1))


def _sc_mesh():
    return plsc.VectorSubcoreMesh(core_axis_name="core", subcore_axis_name="subcore")


def _sc_scatter_rows(src, idx, n_out, part, n_parts):
    d = src.shape[1]
    n_part = idx.shape[1]
    wins = n_part // SC_WINDOW

    def body(src_hbm, idx_hbm, out_hbm):
        def step(src_vmem, idx_vmem):
            for k in range(TOP_K):
                pltpu.sync_copy(src_vmem, out_hbm.at[idx_vmem.at[k]])

        pltpu.emit_pipeline(
            step,
            grid=(PIECES * wins,),
            in_specs=[pl.BlockSpec((SC_WINDOW, d),
                                   index_map=lambda i: ((i // wins) * n_parts * wins
                                                        + part * wins + i % wins, 0)),
                      pl.BlockSpec((TOP_K, SC_WINDOW), index_map=lambda i: (i // wins, i % wins))],
            out_specs=[],
            core_axis_name=("core", "subcore"),
            dimension_semantics=(pltpu.PARALLEL,),
        )(src_hbm, idx_hbm)

    return pl.kernel(body, out_type=jax.ShapeDtypeStruct((n_out, d), src.dtype),
                     mesh=_sc_mesh(), scratch_types=[], name="sc_scatter_rows")(src, idx)


def _sc_gather_rows(table, idx):
    d = table.shape[1]
    n_idx_rows, n_tok = idx.shape
    wins = n_tok // SC_WINDOW

    def body(table_hbm, idx_hbm, out_hbm):
        def step(idx_vmem, out_vmem):
            pltpu.sync_copy(table_hbm.at[idx_vmem.at[0]], out_vmem)

        pltpu.emit_pipeline(
            step,
            grid=(n_idx_rows * wins,),
            in_specs=[pl.BlockSpec((1, SC_WINDOW), index_map=lambda i: (i // wins, i % wins))],
            out_specs=[pl.BlockSpec((SC_WINDOW, d), index_map=lambda i: (i, 0))],
            core_axis_name=("core", "subcore"),
            dimension_semantics=(pltpu.PARALLEL,),
        )(idx_hbm, out_hbm)

    return pl.kernel(body, out_type=jax.ShapeDtypeStruct((n_idx_rows * n_tok, d), table.dtype),
                     mesh=_sc_mesh(), scratch_types=[], name="sc_gather_rows")(table, idx)


def _index_kernel(start_ref, ek_ref, rk_ref, idx_ref, *, n_rows):
    e = ek_ref[...]
    start = jnp.zeros(e.shape, jnp.int32)
    for ex in range(N_EXPERTS):
        start = jnp.where(e == ex, start_ref[ex], start)
    dest = start + rk_ref[...]
    for c in range(PIECES):
        idx_ref[c] = dest + c * n_rows


def _row_indices(padded_start, e_k, r_k, n_rows):
    n_tok = e_k.shape[1]
    ts = INDEX_TILE
    kspec = pl.BlockSpec((TOP_K, ts), lambda i, st: (0, i))
    grid_spec = pltpu.PrefetchScalarGridSpec(
        num_scalar_prefetch=1,
        grid=(n_tok // ts,),
        in_specs=[kspec, kspec],
        out_specs=pl.BlockSpec((PIECES, TOP_K, ts), lambda i, st: (0, 0, i)),
    )
    return pl.pallas_call(
        functools.partial(_index_kernel, n_rows=n_rows),
        grid_spec=grid_spec,
        out_shape=jax.ShapeDtypeStruct((PIECES, TOP_K, n_tok), jnp.int32),
        compiler_params=pltpu.CompilerParams(dimension_semantics=("arbitrary",)),
        name="moe_row_indices",
    )(padded_start, e_k, r_k)


def _expert_kernel(chunk0_ref, nchunk_ref, cnt_ref, total_ref,
                   xs_hbm, wg_ref, wu_ref, wd_ref, o_hbm,
                   xbuf, obuf, wgu_bf, wd_bf, sem_in, sem_out):
    e = pl.program_id(0)
    total = total_ref[0]
    ch = EXPERT_ROWS
    ahead = IN_SLOTS - 1

    def in_copy(g):
        slot = g % IN_SLOTS
        return pltpu.make_async_copy(xs_hbm.at[:, pl.ds(g * ch, ch), :], xbuf.at[slot],
                                     sem_in.at[slot])

    def out_copy(g):
        slot = g % OUT_SLOTS
        return pltpu.make_async_copy(obuf.at[slot], o_hbm.at[:, pl.ds(g * ch, ch), :],
                                     sem_out.at[slot])

    @pl.when(e == 0)
    def _():
        obuf[...] = jnp.zeros_like(obuf)
        for g in range(ahead):
            @pl.when(g < total)
            def _():
                in_copy(g).start()

    wgu_bf[:, :EXPERT_DIM] = wg_ref[0, 0].astype(_BF16)
    wgu_bf[:, EXPERT_DIM:] = wu_ref[0, 0].astype(_BF16)
    wd_bf[...] = wd_ref[0, 0].astype(_BF16)

    def chunk(j, carry):
        g = chunk0_ref[e] + j
        in_copy(g).wait()

        @pl.when(g + ahead < total)
        def _():
            in_copy(g + ahead).start()

        @pl.when(g >= OUT_SLOTS)
        def _():
            out_copy(g - OUT_SLOTS).wait()

        islot = g % IN_SLOTS
        oslot = g % OUT_SLOTS
        valid = cnt_ref[e] - j * ch

        def sub_block(s):
            r0 = s * EXPERT_SUB
            w = jnp.concatenate([xbuf[islot, c, r0:r0 + EXPERT_SUB, :] for c in range(PIECES)],
                                axis=1)
            lo, hi = _unpack_halves(w)
            gu = (_dot(lo.astype(_BF16), wgu_bf[:PACKED]) +
                  _dot(hi.astype(_BF16), wgu_bf[PACKED:]))
            h = _silu(gu[:, :EXPERT_DIM]) * gu[:, EXPERT_DIM:]
            y = _pack_halves(_dot(h.astype(_BF16), wd_bf[...]))
            for c in range(PIECES):
                obuf[oslot, c, r0:r0 + EXPERT_SUB, :] = y[:, c * LANES:(c + 1) * LANES]

        n_sub = ch // EXPERT_SUB

        @pl.when(valid >= ch)
        def _():
            for s in range(n_sub):
                sub_block(s)

        @pl.when(valid < ch)
        def _():
            for s in range(n_sub):
                @pl.when(s * EXPERT_SUB < valid)
                def _():
                    sub_block(s)

        out_copy(g).start()
        return carry

    lax.fori_loop(0, nchunk_ref[e], chunk, 0)

    @pl.when(e == pl.num_programs(0) - 1)
    def _():
        for back in range(OUT_SLOTS, 0, -1):
            @pl.when(total >= back)
            def _():
                out_copy(total - back).wait()


def _experts(layer, chunk0, nchunk, counts, total, xs, w_gate, w_up, w_down):
    d = D_MODEL
    ch = EXPERT_ROWS
    wspec_in = pl.BlockSpec((1, 1, d, EXPERT_DIM), lambda e, *_: (layer, e, 0, 0))
    any_spec = pl.BlockSpec(memory_space=pl.ANY)
    grid_spec = pltpu.PrefetchScalarGridSpec(
        num_scalar_prefetch=4,
        grid=(N_EXPERTS,),
        in_specs=[
            any_spec,
            wspec_in,
            wspec_in,
            pl.BlockSpec((1, 1, EXPERT_DIM, d), lambda e, *_: (layer, e, 0, 0)),
        ],
        out_specs=any_spec,
        scratch_shapes=[pltpu.VMEM((IN_SLOTS, PIECES, ch, LANES), _U32),
                        pltpu.VMEM((OUT_SLOTS, PIECES, ch, LANES), _U32),
                        pltpu.VMEM((d, 2 * EXPERT_DIM), _BF16),
                        pltpu.VMEM((EXPERT_DIM, d), _BF16),
                        pltpu.SemaphoreType.DMA((IN_SLOTS,)),
                        pltpu.SemaphoreType.DMA((OUT_SLOTS,))],
    )
    return pl.pallas_call(
        _expert_kernel,
        grid_spec=grid_spec,
        out_shape=jax.ShapeDtypeStruct(xs.shape, _U32),
        compiler_params=pltpu.CompilerParams(
            dimension_semantics=("arbitrary",),
            vmem_limit_bytes=VMEM_LIMIT),
        name="moe_experts",
    )(chunk0, nchunk, counts, total, xs, w_gate, w_up, w_down)


def _combine_kernel(x_ref, p_ref, gate_ref, sg_ref, su_ref, sd_ref, g_ref, b_ref, *rest):
    o_ref = rest[-1]
    x = x_ref[...]
    xb = x.astype(_BF16)
    gates = gate_ref[...]
    r_lo = r_hi = None
    for k in range(TOP_K):
        w = jnp.concatenate([p_ref[c, k] for c in range(PIECES)], axis=1)
        lo, hi = _unpack_halves(w)
        gk = gates[:, k:k + 1]
        r_lo = gk * lo if r_lo is None else r_lo + gk * lo
        r_hi = gk * hi if r_hi is None else r_hi + gk * hi
    routed = jnp.concatenate([r_lo, r_hi], axis=1)
    h = _silu(_dot(xb, sg_ref[...])) * _dot(xb, su_ref[...])
    shared = _dot(h.astype(_BF16), sd_ref[...])
    o_ref[...] = _layer_norm(DEEPNORM_ALPHA * x + (routed + shared), g_ref[...], b_ref[...])


def _combine(x, picked, gates, sh_gate, sh_up, sh_down, ln_g, ln_b, part, n_parts, prev):
    n_tok, d = x.shape
    ts = COMBINE_TILE
    steps = n_tok // n_parts // ts
    const2 = lambda i: (0, 0)
    rows_map = lambda i: (part * steps + i, 0)
    in_specs = [
        pl.BlockSpec((ts, d), rows_map),
        pl.BlockSpec((PIECES, TOP_K, ts, LANES), lambda i: (0, 0, i, 0)),
        pl.BlockSpec((ts, TOP_K), lambda i: (i, 0)),
        pl.BlockSpec((d, EXPERT_DIM), const2),
        pl.BlockSpec((d, EXPERT_DIM), const2),
        pl.BlockSpec((EXPERT_DIM, d), const2),
        pl.BlockSpec((1, d), const2),
        pl.BlockSpec((1, d), const2),
    ]
    args = [x, picked, gates, sh_gate.astype(_BF16), sh_up.astype(_BF16), sh_down.astype(_BF16),
            ln_g.reshape(1, d), ln_b.reshape(1, d)]
    aliases = {}
    if prev is not None:
        in_specs.append(pl.BlockSpec(memory_space=pl.ANY))
        aliases = {len(args): 0}
        args.append(prev)
    return pl.pallas_call(
        _combine_kernel,
        grid=(steps,),
        in_specs=in_specs,
        out_specs=pl.BlockSpec((ts, d), rows_map),
        out_shape=jax.ShapeDtypeStruct((n_tok, d), _F32),
        input_output_aliases=aliases,
        compiler_params=pltpu.CompilerParams(
            dimension_semantics=("arbitrary",),
            vmem_limit_bytes=VMEM_LIMIT),
        name="moe_combine",
    )(*args)


def _moe_layer(layer, x, x_packed, w_router, router_bias, w_gate, w_up, w_down,
               sh_gate, sh_up, sh_down, ln_g, ln_b):
    n_tok, d = x.shape
    bm = EXPERT_ROWS
    n_part = n_tok // TOKEN_PARTS
    n_rows = n_part * TOP_K + N_EXPERTS * bm
    x_flat = x_packed.reshape(PIECES * n_tok, LANES)

    routed = []
    for part in range(TOKEN_PARTS):
        e_k, r_k, g_k, counts = _route(x, w_router, router_bias, part, TOKEN_PARTS)
        counts = counts.reshape(N_EXPERTS).astype(jnp.int32)
        padded = (counts + bm - 1) // bm * bm
        padded_end = jnp.cumsum(padded)
        padded_start = padded_end - padded
        idx = _row_indices(padded_start, e_k, r_k, n_rows).reshape(PIECES * TOP_K, n_part)
        routed.append((idx, g_k.T, padded_start // bm, padded // bm, counts,
                       padded_end[-1:] // bm))

    rows = []
    for part, (idx, _, chunk0, nchunk, counts, total) in enumerate(routed):
        xs = _sc_scatter_rows(x_flat, idx, PIECES * n_rows, part, TOKEN_PARTS)
        rows.append(_experts(layer, chunk0, nchunk, counts, total,
                             xs.reshape(PIECES, n_rows, LANES), w_gate, w_up, w_down))

    out = None
    for part, (idx, gates, *_) in enumerate(routed):
        picked = _sc_gather_rows(rows[part].reshape(PIECES * n_rows, LANES), idx)
        picked = picked.reshape(PIECES, TOP_K, n_part, LANES)
        out = _combine(x, picked, gates, sh_gate, sh_up, sh_down, ln_g, ln_b,
                       part, TOKEN_PARTS, out)
    return out


def kernel(x, pool_w_in, pool_w_grp, pool_scale, pool_w_out, sgu_w_in, sgu_b_in, sgu_ln_g, sgu_ln_b, sgu_w_s, sgu_b_s, sgu_w_out, ln_mix_g, ln_mix_b, moe_w_router, moe_router_bias, moe_w_gate, moe_w_up, moe_w_down, moe_sh_gate, moe_sh_up, moe_sh_down, ln_ffn_g, ln_ffn_b):
    bsz, seq, d = x.shape
    n_tok = bsz * seq

    def moe(i, h, h_packed):
        return _moe_layer(i, h, h_packed, moe_w_router[i], moe_router_bias[i], moe_w_gate,
                          moe_w_up, moe_w_down, moe_sh_gate[i], moe_sh_up[i],
                          moe_sh_down[i], ln_ffn_g[i], ln_ffn_b[i])

    h, h_packed = _pool_layer(x, pool_w_in[0], pool_w_grp[0], pool_scale[0], pool_w_out[0],
                              ln_mix_g[0], ln_mix_b[0])
    h = moe(0, h.reshape(n_tok, d), h_packed)
    h, h_packed = _sgu_layer(h, sgu_w_in[0], sgu_b_in[0], sgu_ln_g[0], sgu_ln_b[0], sgu_w_s[0],
                             sgu_b_s[0], sgu_w_out[0], ln_mix_g[1], ln_mix_b[1])
    h = moe(1, h, h_packed)
    return h.reshape(bsz, seq, d)
```

```python
import functools

import jax
import jax.numpy as jnp
from jax import lax
from jax.experimental import pallas as pl
from jax.experimental.pallas import tpu as pltpu
from jax.experimental.pallas import tpu_sc as plsc

D_MODEL = 1024
DEPTH = 2
POOL_WINDOWS = (2, 4, 8, 16)
POOL_GROUP_DIM = D_MODEL // len(POOL_WINDOWS)
POOL_HALO = 16
SGU_CHUNK = 128
SGU_HEADS = 4
SGU_WIDTH = 2 * D_MODEL
SGU_HEAD_DIM = SGU_WIDTH // SGU_HEADS
N_EXPERTS = 64
TOP_K = 8
N_GROUPS = 8
GROUP_SIZE = N_EXPERTS // N_GROUPS
TOPK_GROUPS = 4
EXPERT_DIM = D_MODEL // 4
ROUTED_SCALE = 2.5
DEEPNORM_ALPHA = (2 * DEPTH) ** 0.25
LN_EPS = 1e-5

LANES = 128
PACKED = D_MODEL // 2
PIECES = PACKED // LANES

POOL_TILE = 512
SGU_TILE = 256
ROUTE_TILE = 512
INDEX_TILE = 2048
EXPERT_ROWS = 512
EXPERT_SUB = 256
IN_SLOTS = 4
OUT_SLOTS = 3
COMBINE_TILE = 256
SC_WINDOW = 128
TOKEN_PARTS = 2
VMEM_LIMIT = 56 * 1024 * 1024

_F32 = jnp.float32
_BF16 = jnp.bfloat16
_U32 = jnp.uint32


def _dot(a, b):
    return jnp.dot(a, b, preferred_element_type=_F32)


def _layer_norm(h, g, b):
    mu = jnp.mean(h, axis=-1, keepdims=True)
    hc = h - mu
    var = jnp.mean(hc * hc, axis=-1, keepdims=True)
    return hc * lax.rsqrt(var + LN_EPS) * g + b


def _silu(x):
    return x * jax.nn.sigmoid(x)


def _gelu_tanh(x):
    c = 0.7978845608028654
    return 0.5 * x * (1.0 + jnp.tanh(c * (x + 0.044715 * (x * x * x))))


def _pack_halves(v):
    half = v.shape[1] // 2
    lo = lax.bitcast_convert_type(v[:, :half].astype(_BF16).astype(_F32), _U32)
    hi = lax.bitcast_convert_type(v[:, half:].astype(_BF16).astype(_F32), _U32)
    return (hi & _U32(0xFFFF0000)) | (lo >> 16)


def _unpack_halves(w):
    lo = lax.bitcast_convert_type(w << 16, _F32)
    hi = lax.bitcast_convert_type(w & _U32(0xFFFF0000), _F32)
    return lo, hi


def _store_pieces(ref, w):
    for c in range(PIECES):
        ref[c] = w[:, c * LANES:(c + 1) * LANES]


def _load_pieces(ref):
    return jnp.concatenate([ref[c] for c in range(PIECES)], axis=1)


def _pool_kernel(x_ref, win_ref, wgrp_ref, scale_ref, wout_ref, g_ref, b_ref,
                 o_ref, op_ref, zs_ref, y_ref):
    s = pl.program_id(1)
    ts = x_ref.shape[1]
    x = x_ref[0]
    z = _dot(x.astype(_BF16), win_ref[...])

    @pl.when(s == 0)
    def _():
        zs_ref[0:POOL_HALO, :] = jnp.zeros((POOL_HALO, D_MODEL), _F32)

    zs_ref[POOL_HALO:POOL_HALO + ts, :] = z
    pos = s * ts + lax.broadcasted_iota(jnp.int32, (ts, 1), 0)
    for g, w in enumerate(POOL_WINDOWS):
        c0 = g * POOL_GROUP_DIM
        c1 = c0 + POOL_GROUP_DIM
        zg = zs_ref[POOL_HALO:POOL_HALO + ts, c0:c1]
        acc = zg
        for k in range(1, w):
            acc = acc + zs_ref[POOL_HALO - k:POOL_HALO - k + ts, c0:c1]
        cnt = jnp.minimum(pos + 1, w).astype(_F32)
        pooled = acc / cnt - zg
        yg = _dot(pooled.astype(_BF16), wgrp_ref[g]) * scale_ref[:, c0:c1]
        y_ref[:, c0:c1] = yg.astype(_BF16)
    zs_ref[0:POOL_HALO, :] = zs_ref[ts:ts + POOL_HALO, :]
    mix = _dot(y_ref[...], wout_ref[...])
    out = _layer_norm(DEEPNORM_ALPHA * x + mix, g_ref[...], b_ref[...])
    o_ref[0] = out
    _store_pieces(op_ref, _pack_halves(out))


def _pool_layer(x, w_in, w_grp, scale, w_out, ln_g, ln_b):
    bsz, seq, d = x.shape
    ts = POOL_TILE
    steps = seq // ts
    const2 = lambda b, s: (0, 0)
    out_shape = (jax.ShapeDtypeStruct((bsz, seq, d), _F32),
                 jax.ShapeDtypeStruct((PIECES, bsz * seq, LANES), _U32))
    tile = pl.BlockSpec((1, ts, d), lambda b, s: (b, s, 0))
    ptile = pl.BlockSpec((PIECES, ts, LANES), lambda b, s: (0, b * steps + s, 0))
    return pl.pallas_call(
        _pool_kernel,
        grid=(bsz, steps),
        in_specs=[
            tile,
            pl.BlockSpec((d, d), const2),
            pl.BlockSpec((len(POOL_WINDOWS), POOL_GROUP_DIM, POOL_GROUP_DIM), lambda b, s: (0, 0, 0)),
            pl.BlockSpec((1, d), const2),
            pl.BlockSpec((d, d), const2),
            pl.BlockSpec((1, d), const2),
            pl.BlockSpec((1, d), const2),
        ],
        out_specs=(tile, ptile),
        out_shape=out_shape,
        scratch_shapes=[pltpu.VMEM((POOL_HALO + ts, d), _F32),
                        pltpu.VMEM((ts, d), _BF16)],
        compiler_params=pltpu.CompilerParams(
            dimension_semantics=("arbitrary", "arbitrary"),
            vmem_limit_bytes=VMEM_LIMIT),
        name="pool_mixer",
    )(x, w_in.astype(_BF16), w_grp.astype(_BF16), scale.reshape(1, d),
      w_out.astype(_BF16), ln_g.reshape(1, d), ln_b.reshape(1, d))


def _sgu_kernel(x_ref, win_ref, bin_ref, lng_ref, lnb_ref, ws_ref, bs_ref, wout_ref,
                g_ref, b_ref, o_ref, op_ref, gated_ref):
    ts = x_ref.shape[0]
    x = x_ref[...]
    xb = x.astype(_BF16)
    v = _gelu_tanh(_dot(xb, win_ref[:, SGU_WIDTH:]) + bin_ref[:, SGU_WIDTH:])
    v = _layer_norm(v, lng_ref[...], lnb_ref[...]).astype(_BF16)
    u = _gelu_tanh(_dot(xb, win_ref[:, :SGU_WIDTH]) + bin_ref[:, :SGU_WIDTH])
    for c in range(ts // SGU_CHUNK):
        r0 = c * SGU_CHUNK
        for h in range(SGU_HEADS):
            c0 = h * SGU_HEAD_DIM
            sv = _dot(ws_ref[h], v[r0:r0 + SGU_CHUNK, c0:c0 + SGU_HEAD_DIM]) + bs_ref[h]
            gated_ref[r0:r0 + SGU_CHUNK, c0:c0 + SGU_HEAD_DIM] = (
                u[r0:r0 + SGU_CHUNK, c0:c0 + SGU_HEAD_DIM] * sv).astype(_BF16)
    mix = _dot(gated_ref[...], wout_ref[...])
    out = _layer_norm(DEEPNORM_ALPHA * x + mix, g_ref[...], b_ref[...])
    o_ref[...] = out
    _store_pieces(op_ref, _pack_halves(out))


def _sgu_layer(x, w_in, b_in, ln_g, ln_b, w_s, b_s, w_out, mix_g, mix_b):
    n_tok, d = x.shape
    ts = SGU_TILE
    const2 = lambda i: (0, 0)
    const3 = lambda i: (0, 0, 0)
    causal = jnp.tril(jnp.ones((SGU_CHUNK, SGU_CHUNK), w_s.dtype))
    ws = (w_s * causal[None]).astype(_BF16)
    tile = pl.BlockSpec((ts, d), lambda i: (i, 0))
    ptile = pl.BlockSpec((PIECES, ts, LANES), lambda i: (0, i, 0))
    out_shape = (jax.ShapeDtypeStruct((n_tok, d), _F32),
                 jax.ShapeDtypeStruct((PIECES, n_tok, LANES), _U32))
    return pl.pallas_call(
        _sgu_kernel,
        grid=(n_tok // ts,),
        in_specs=[
            tile,
            pl.BlockSpec((d, 2 * SGU_WIDTH), const2),
            pl.BlockSpec((1, 2 * SGU_WIDTH), const2),
            pl.BlockSpec((1, SGU_WIDTH), const2),
            pl.BlockSpec((1, SGU_WIDTH), const2),
            pl.BlockSpec((SGU_HEADS, SGU_CHUNK, SGU_CHUNK), const3),
            pl.BlockSpec((SGU_HEADS, SGU_CHUNK, 1), const3),
            pl.BlockSpec((SGU_WIDTH, d), const2),
            pl.BlockSpec((1, d), const2),
            pl.BlockSpec((1, d), const2),
        ],
        out_specs=(tile, ptile),
        out_shape=out_shape,
        scratch_shapes=[pltpu.VMEM((ts, SGU_WIDTH), _BF16)],
        compiler_params=pltpu.CompilerParams(
            dimension_semantics=("arbitrary",),
            vmem_limit_bytes=VMEM_LIMIT),
        name="sgu_mixer",
    )(x, w_in.astype(_BF16), b_in.reshape(1, -1), ln_g.reshape(1, -1), ln_b.reshape(1, -1),
      ws, b_s.reshape(SGU_HEADS, SGU_CHUNK, 1), w_out.astype(_BF16),
      mix_g.reshape(1, d), mix_b.reshape(1, d))


def _all_max(a):
    return jnp.max(jnp.max(a, axis=0, keepdims=True), axis=1, keepdims=True)


def _all_min(a):
    return jnp.min(jnp.min(a, axis=0, keepdims=True), axis=1, keepdims=True)


def _all_sum(a):
    return jnp.sum(jnp.sum(a, axis=0, keepdims=True), axis=1, keepdims=True)


def _route_kernel(x_ref, whi_ref, wlo_ref, bias_ref, earlier_ref,
                  ek_ref, rk_ref, gk_ref, cnt_ref, carry_ref):
    i = pl.program_id(0)
    ts = x_ref.shape[0]

    @pl.when(i == 0)
    def _():
        carry_ref[...] = jnp.zeros_like(carry_ref)

    x = x_ref[...]
    x_hi = x.astype(_BF16)
    x_lo = (x - x_hi.astype(_F32)).astype(_BF16)
    nt = lambda a, b: lax.dot_general(a, b, (((1,), (1,)), ((), ())),
                                      preferred_element_type=_F32)
    logits = nt(whi_ref[...], x_hi) + (nt(whi_ref[...], x_lo) + nt(wlo_ref[...], x_hi))
    scores = jax.nn.sigmoid(logits).reshape(N_GROUPS, GROUP_SIZE, ts)
    biased = scores + bias_ref[...]
    neg_inf = jnp.float32(-jnp.inf)
    shape3 = (N_GROUPS, GROUP_SIZE, ts)
    in_grp = lax.broadcasted_iota(jnp.int32, shape3, 1)
    grp = lax.broadcasted_iota(jnp.int32, shape3, 0)
    eid = grp * GROUP_SIZE + in_grp

    m1 = jnp.max(biased, axis=1, keepdims=True)
    first1 = jnp.min(jnp.where(biased == m1, in_grp, GROUP_SIZE), axis=1, keepdims=True)
    m2 = jnp.max(jnp.where(in_grp == first1, neg_inf, biased), axis=1, keepdims=True)
    gscore = m1 + m2

    gid = lax.broadcasted_iota(jnp.int32, (N_GROUPS, 1, ts), 0)
    gsel = jnp.zeros((N_GROUPS, 1, ts), jnp.bool_)
    for _ in range(TOPK_GROUPS):
        m = jnp.max(gscore, axis=0, keepdims=True)
        first = jnp.min(jnp.where(gscore == m, gid, N_GROUPS), axis=0, keepdims=True)
        pick = gid == first
        gsel = jnp.logical_or(gsel, pick)
        gscore = jnp.where(pick, neg_inf, gscore)

    masked = jnp.where(gsel, biased, neg_inf)
    picked_any = jnp.zeros(shape3, jnp.bool_)
    e_k, s_k = [], []
    for _ in range(TOP_K):
        m = _all_max(masked)
        first = _all_min(jnp.where(masked == m, eid, N_EXPERTS))
        pick = eid == first
        picked_any = jnp.logical_or(picked_any, pick)
        masked = jnp.where(pick, neg_inf, masked)
        e_k.append(first)
        s_k.append(_all_sum(jnp.where(pick, scores, 0.0)))
    denom = s_k[0]
    for k in range(1, TOP_K):
        denom = denom + s_k[k]

    sel = picked_any.astype(_F32).reshape(N_EXPERTS, ts)
    rank = _dot(sel.astype(_BF16), earlier_ref[...]) + carry_ref[...]
    rank3 = rank.reshape(shape3)
    carry_ref[...] += jnp.sum(sel, axis=1, keepdims=True)
    cnt_ref[...] = carry_ref[...]

    for k in range(TOP_K):
        r = _all_sum(jnp.where(eid == e_k[k], rank3, 0.0))
        ek_ref[k:k + 1, :] = e_k[k].reshape(1, ts)
        rk_ref[k:k + 1, :] = r.reshape(1, ts).astype(jnp.int32)
        gk_ref[k:k + 1, :] = (s_k[k] / denom * ROUTED_SCALE).reshape(1, ts)


def _route(x, w_router, router_bias, part, n_parts):
    n_tok, d = x.shape
    ts = ROUTE_TILE
    n_part = n_tok // n_parts
    steps = n_part // ts
    out_shape = (jax.ShapeDtypeStruct((TOP_K, n_part), jnp.int32),
                 jax.ShapeDtypeStruct((TOP_K, n_part), jnp.int32),
                 jax.ShapeDtypeStruct((TOP_K, n_part), _F32),
                 jax.ShapeDtypeStruct((N_EXPERTS, 1), _F32))
    kspec = pl.BlockSpec((TOP_K, ts), lambda i: (0, i))
    const2 = lambda i: (0, 0)
    w_t = w_router.T
    w_hi = w_t.astype(_BF16)
    w_lo = (w_t - w_hi.astype(_F32)).astype(_BF16)
    pos = jnp.arange(ts, dtype=jnp.int32)
    earlier = (pos[:, None] < pos[None, :]).astype(_BF16)
    return pl.pallas_call(
        _route_kernel,
        grid=(steps,),
        in_specs=[
            pl.BlockSpec((ts, d), lambda i: (part * steps + i, 0)),
            pl.BlockSpec((N_EXPERTS, d), const2),
            pl.BlockSpec((N_EXPERTS, d), const2),
            pl.BlockSpec((N_GROUPS, GROUP_SIZE, 1), lambda i: (0, 0, 0)),
            pl.BlockSpec((ts, ts), const2),
        ],
        out_specs=(kspec, kspec, kspec, pl.BlockSpec((N_EXPERTS, 1), const2)),
        out_shape=out_shape,
        scratch_shapes=[pltpu.VMEM((N_EXPERTS, 1), _F32)],
        compiler_params=pltpu.CompilerParams(
            dimension_semantics=("arbitrary",),
            vmem_limit_bytes=VMEM_LIMIT),
        name="moe_route",
    )(x, w_hi, w_lo, router_bias.reshape(N_GROUPS, GROUP_SIZE, 1), earlier)


def _sc_mesh():
    return plsc.VectorSubcoreMesh(core_axis_name="core", subcore_axis_name="subcore")


def _sc_scatter_rows(src, idx, n_out, part, n_parts):
    d = src.shape[1]
    n_part = idx.shape[1]
    wins = n_part // SC_WINDOW

    def body(src_hbm, idx_hbm, out_hbm):
        def step(src_vmem, idx_vmem):
            for k in range(TOP_K):
                pltpu.sync_copy(src_vmem, out_hbm.at[idx_vmem.at[k]])

        pltpu.emit_pipeline(
            step,
            grid=(PIECES * wins,),
            in_specs=[pl.BlockSpec((SC_WINDOW, d),
                                   index_map=lambda i: ((i // wins) * n_parts * wins
                                                        + part * wins + i % wins, 0)),
                      pl.BlockSpec((TOP_K, SC_WINDOW), index_map=lambda i: (i // wins, i % wins))],
            out_specs=[],
            core_axis_name=("core", "subcore"),
            dimension_semantics=(pltpu.PARALLEL,),
        )(src_hbm, idx_hbm)

    return pl.kernel(body, out_type=jax.ShapeDtypeStruct((n_out, d), src.dtype),
                     mesh=_sc_mesh(), scratch_types=[], name="sc_scatter_rows")(src, idx)


def _sc_gather_rows(table, idx):
    d = table.shape[1]
    n_idx_rows, n_tok = idx.shape
    wins = n_tok // SC_WINDOW

    def body(table_hbm, idx_hbm, out_hbm):
        def step(idx_vmem, out_vmem):
            pltpu.sync_copy(table_hbm.at[idx_vmem.at[0]], out_vmem)

        pltpu.emit_pipeline(
            step,
            grid=(n_idx_rows * wins,),
            in_specs=[pl.BlockSpec((1, SC_WINDOW), index_map=lambda i: (i // wins, i % wins))],
            out_specs=[pl.BlockSpec((SC_WINDOW, d), index_map=lambda i: (i, 0))],
            core_axis_name=("core", "subcore"),
            dimension_semantics=(pltpu.PARALLEL,),
        )(idx_hbm, out_hbm)

    return pl.kernel(body, out_type=jax.ShapeDtypeStruct((n_idx_rows * n_tok, d), table.dtype),
                     mesh=_sc_mesh(), scratch_types=[], name="sc_gather_rows")(table, idx)


def _index_kernel(start_ref, ek_ref, rk_ref, idx_ref, *, n_rows):
    e = ek_ref[...]
    start = jnp.zeros(e.shape, jnp.int32)
    for ex in range(N_EXPERTS):
        start = jnp.where(e == ex, start_ref[ex], start)
    dest = start + rk_ref[...]
    for c in range(PIECES):
        idx_ref[c] = dest + c * n_rows


def _row_indices(padded_start, e_k, r_k, n_rows):
    n_tok = e_k.shape[1]
    ts = INDEX_TILE
    kspec = pl.BlockSpec((TOP_K, ts), lambda i, st: (0, i))
    grid_spec = pltpu.PrefetchScalarGridSpec(
        num_scalar_prefetch=1,
        grid=(n_tok // ts,),
        in_specs=[kspec, kspec],
        out_specs=pl.BlockSpec((PIECES, TOP_K, ts), lambda i, st: (0, 0, i)),
    )
    return pl.pallas_call(
        functools.partial(_index_kernel, n_rows=n_rows),
        grid_spec=grid_spec,
        out_shape=jax.ShapeDtypeStruct((PIECES, TOP_K, n_tok), jnp.int32),
        compiler_params=pltpu.CompilerParams(dimension_semantics=("arbitrary",)),
        name="moe_row_indices",
    )(padded_start, e_k, r_k)


def _expert_kernel(chunk0_ref, nchunk_ref, cnt_ref, total_ref,
                   xs_hbm, wg_ref, wu_ref, wd_ref, o_hbm,
                   xbuf, obuf, wgu_bf, wd_bf, sem_in, sem_out):
    e = pl.program_id(0)
    total = total_ref[0]
    ch = EXPERT_ROWS
    ahead = IN_SLOTS - 1

    def in_copy(g):
        slot = g % IN_SLOTS
        return pltpu.make_async_copy(xs_hbm.at[:, pl.ds(g * ch, ch), :], xbuf.at[slot],
                                     sem_in.at[slot])

    def out_copy(g):
        slot = g % OUT_SLOTS
        return pltpu.make_async_copy(obuf.at[slot], o_hbm.at[:, pl.ds(g * ch, ch), :],
                                     sem_out.at[slot])

    @pl.when(e == 0)
    def _():
        obuf[...] = jnp.zeros_like(obuf)
        for g in range(ahead):
            @pl.when(g < total)
            def _():
                in_copy(g).start()

    wgu_bf[:, :EXPERT_DIM] = wg_ref[0, 0].astype(_BF16)
    wgu_bf[:, EXPERT_DIM:] = wu_ref[0, 0].astype(_BF16)
    wd_bf[...] = wd_ref[0, 0].astype(_BF16)

    def chunk(j, carry):
        g = chunk0_ref[e] + j
        in_copy(g).wait()

        @pl.when(g + ahead < total)
        def _():
            in_copy(g + ahead).start()

        @pl.when(g >= OUT_SLOTS)
        def _():
            out_copy(g - OUT_SLOTS).wait()

        islot = g % IN_SLOTS
        oslot = g % OUT_SLOTS
        valid = cnt_ref[e] - j * ch

        def sub_block(s):
            r0 = s * EXPERT_SUB
            w = jnp.concatenate([xbuf[islot, c, r0:r0 + EXPERT_SUB, :] for c in range(PIECES)],
                                axis=1)
            lo, hi = _unpack_halves(w)
            gu = (_dot(lo.astype(_BF16), wgu_bf[:PACKED]) +
                  _dot(hi.astype(_BF16), wgu_bf[PACKED:]))
            h = _silu(gu[:, :EXPERT_DIM]) * gu[:, EXPERT_DIM:]
            y = _pack_halves(_dot(h.astype(_BF16), wd_bf[...]))
            for c in range(PIECES):
                obuf[oslot, c, r0:r0 + EXPERT_SUB, :] = y[:, c * LANES:(c + 1) * LANES]

        n_sub = ch // EXPERT_SUB

        @pl.when(valid >= ch)
        def _():
            for s in range(n_sub):
                sub_block(s)

        @pl.when(valid < ch)
        def _():
            for s in range(n_sub):
                @pl.when(s * EXPERT_SUB < valid)
                def _():
                    sub_block(s)

        out_copy(g).start()
        return carry

    lax.fori_loop(0, nchunk_ref[e], chunk, 0)

    @pl.when(e == pl.num_programs(0) - 1)
    def _():
        for back in range(OUT_SLOTS, 0, -1):
            @pl.when(total >= back)
            def _():
                out_copy(total - back).wait()


def _experts(layer, chunk0, nchunk, counts, total, xs, w_gate, w_up, w_down):
    d = D_MODEL
    ch = EXPERT_ROWS
    wspec_in = pl.BlockSpec((1, 1, d, EXPERT_DIM), lambda e, *_: (layer, e, 0, 0))
    any_spec = pl.BlockSpec(memory_space=pl.ANY)
    grid_spec = pltpu.PrefetchScalarGridSpec(
        num_scalar_prefetch=4,
        grid=(N_EXPERTS,),
        in_specs=[
            any_spec,
            wspec_in,
            wspec_in,
            pl.BlockSpec((1, 1, EXPERT_DIM, d), lambda e, *_: (layer, e, 0, 0)),
        ],
        out_specs=any_spec,
        scratch_shapes=[pltpu.VMEM((IN_SLOTS, PIECES, ch, LANES), _U32),
                        pltpu.VMEM((OUT_SLOTS, PIECES, ch, LANES), _U32),
                        pltpu.VMEM((d, 2 * EXPERT_DIM), _BF16),
                        pltpu.VMEM((EXPERT_DIM, d), _BF16),
                        pltpu.SemaphoreType.DMA((IN_SLOTS,)),
                        pltpu.SemaphoreType.DMA((OUT_SLOTS,))],
    )
    return pl.pallas_call(
        _expert_kernel,
        grid_spec=grid_spec,
        out_shape=jax.ShapeDtypeStruct(xs.shape, _U32),
        compiler_params=pltpu.CompilerParams(
            dimension_semantics=("arbitrary",),
            vmem_limit_bytes=VMEM_LIMIT),
        name="moe_experts",
    )(chunk0, nchunk, counts, total, xs, w_gate, w_up, w_down)


def _combine_kernel(x_ref, p_ref, gate_ref, sg_ref, su_ref, sd_ref, g_ref, b_ref, *rest):
    o_ref = rest[-1]
    x = x_ref[...]
    xb = x.astype(_BF16)
    gates = gate_ref[...]
    r_lo = r_hi = None
    for k in range(TOP_K):
        w = jnp.concatenate([p_ref[c, k] for c in range(PIECES)], axis=1)
        lo, hi = _unpack_halves(w)
        gk = gates[:, k:k + 1]
        r_lo = gk * lo if r_lo is None else r_lo + gk * lo
        r_hi = gk * hi if r_hi is None else r_hi + gk * hi
    routed = jnp.concatenate([r_lo, r_hi], axis=1)
    h = _silu(_dot(xb, sg_ref[...])) * _dot(xb, su_ref[...])
    shared = _dot(h.astype(_BF16), sd_ref[...])
    o_ref[...] = _layer_norm(DEEPNORM_ALPHA * x + (routed + shared), g_ref[...], b_ref[...])


def _combine(x, picked, gates, sh_gate, sh_up, sh_down, ln_g, ln_b, part, n_parts, prev):
    n_tok, d = x.shape
    ts = COMBINE_TILE
    steps = n_tok // n_parts // ts
    const2 = lambda i: (0, 0)
    rows_map = lambda i: (part * steps + i, 0)
    in_specs = [
        pl.BlockSpec((ts, d), rows_map),
        pl.BlockSpec((PIECES, TOP_K, ts, LANES), lambda i: (0, 0, i, 0)),
        pl.BlockSpec((ts, TOP_K), lambda i: (i, 0)),
        pl.BlockSpec((d, EXPERT_DIM), const2),
        pl.BlockSpec((d, EXPERT_DIM), const2),
        pl.BlockSpec((EXPERT_DIM, d), const2),
        pl.BlockSpec((1, d), const2),
        pl.BlockSpec((1, d), const2),
    ]
    args = [x, picked, gates, sh_gate.astype(_BF16), sh_up.astype(_BF16), sh_down.astype(_BF16),
            ln_g.reshape(1, d), ln_b.reshape(1, d)]
    aliases = {}
    if prev is not None:
        in_specs.append(pl.BlockSpec(memory_space=pl.ANY))
        aliases = {len(args): 0}
        args.append(prev)
    return pl.pallas_call(
        _combine_kernel,
        grid=(steps,),
        in_specs=in_specs,
        out_specs=pl.BlockSpec((ts, d), rows_map),
        out_shape=jax.ShapeDtypeStruct((n_tok, d), _F32),
        input_output_aliases=aliases,
        compiler_params=pltpu.CompilerParams(
            dimension_semantics=("arbitrary",),
            vmem_limit_bytes=VMEM_LIMIT),
        name="moe_combine",
    )(*args)


def _moe_layer(layer, x, x_packed, w_router, router_bias, w_gate, w_up, w_down,
               sh_gate, sh_up, sh_down, ln_g, ln_b):
    n_tok, d = x.shape
    bm = EXPERT_ROWS
    n_part = n_tok // TOKEN_PARTS
    n_rows = n_part * TOP_K + N_EXPERTS * bm
    x_flat = x_packed.reshape(PIECES * n_tok, LANES)

    routed = []
    for part in range(TOKEN_PARTS):
        e_k, r_k, g_k, counts = _route(x, w_router, router_bias, part, TOKEN_PARTS)
        counts = counts.reshape(N_EXPERTS).astype(jnp.int32)
        padded = (counts + bm - 1) // bm * bm
        padded_end = jnp.cumsum(padded)
        padded_start = padded_end - padded
        idx = _row_indices(padded_start, e_k, r_k, n_rows).reshape(PIECES * TOP_K, n_part)
        routed.append((idx, g_k.T, padded_start // bm, padded // bm, counts,
                       padded_end[-1:] // bm))

    rows = []
    for part, (idx, _, chunk0, nchunk, counts, total) in enumerate(routed):
        xs = _sc_scatter_rows(x_flat, idx, PIECES * n_rows, part, TOKEN_PARTS)
        rows.append(_experts(layer, chunk0, nchunk, counts, total,
                             xs.reshape(PIECES, n_rows, LANES), w_gate, w_up, w_down))

    out = None
    for part, (idx, gates, *_) in enumerate(routed):
        picked = _sc_gather_rows(rows[part].reshape(PIECES * n_rows, LANES), idx)
        picked = picked.reshape(PIECES, TOP_K, n_part, LANES)
        out = _combine(x, picked, gates, sh_gate, sh_up, sh_down, ln_g, ln_b,
                       part, TOKEN_PARTS, out)
    return out


def kernel(x, pool_w_in, pool_w_grp, pool_scale, pool_w_out, sgu_w_in, sgu_b_in, sgu_ln_g, sgu_ln_b, sgu_w_s, sgu_b_s, sgu_w_out, ln_mix_g, ln_mix_b, moe_w_router, moe_router_bias, moe_w_gate, moe_w_up, moe_w_down, moe_sh_gate, moe_sh_up, moe_sh_down, ln_ffn_g, ln_ffn_b):
    bsz, seq, d = x.shape
    n_tok = bsz * seq

    def moe(i, h, h_packed):
        return _moe_layer(i, h, h_packed, moe_w_router[i], moe_router_bias[i], moe_w_gate,
                          moe_w_up, moe_w_down, moe_sh_gate[i], moe_sh_up[i],
                          moe_sh_down[i], ln_ffn_g[i], ln_ffn_b[i])

    h, h_packed = _pool_layer(x, pool_w_in[0], pool_w_grp[0], pool_scale[0], pool_w_out[0],
                              ln_mix_g[0], ln_mix_b[0])
    h = moe(0, h.reshape(n_tok, d), h_packed)
    h, h_packed = _sgu_layer(h, sgu_w_in[0], sgu_b_in[0], sgu_ln_g[0], sgu_ln_b[0], sgu_w_s[0],
                             sgu_b_s[0], sgu_w_out[0], ln_mix_g[1], ln_mix_b[1])
    h = moe(1, h, h_packed)
    return h.reshape(bsz, seq, d)
```

```python
import functools

import jax
import jax.numpy as jnp
from jax import lax
from jax.experimental import pallas as pl
from jax.experimental.pallas import tpu as pltpu
from jax.experimental.pallas import tpu_sc as plsc

D_MODEL = 1024
DEPTH = 2
POOL_WINDOWS = (2, 4, 8, 16)
POOL_GROUP_DIM = D_MODEL // len(POOL_WINDOWS)
POOL_HALO = 16
SGU_CHUNK = 128
SGU_HEADS = 4
SGU_WIDTH = 2 * D_MODEL
SGU_HEAD_DIM = SGU_WIDTH // SGU_HEADS
N_EXPERTS = 64
TOP_K = 8
N_GROUPS = 8
GROUP_SIZE = N_EXPERTS // N_GROUPS
TOPK_GROUPS = 4
EXPERT_DIM = D_MODEL // 4
ROUTED_SCALE = 2.5
DEEPNORM_ALPHA = (2 * DEPTH) ** 0.25
LN_EPS = 1e-5

LANES = 128
PACKED = D_MODEL // 2
PIECES = PACKED // LANES

POOL_TILE = 512
SGU_TILE = 256
ROUTE_TILE = 512
INDEX_TILE = 2048
EXPERT_ROWS = 512
EXPERT_SUB = 256
IN_SLOTS = 4
OUT_SLOTS = 3
COMBINE_TILE = 256
SC_WINDOW = 128
TOKEN_PARTS = 2
VMEM_LIMIT = 56 * 1024 * 1024

_F32 = jnp.float32
_BF16 = jnp.bfloat16
_U32 = jnp.uint32


def _dot(a, b):
    return jnp.dot(a, b, preferred_element_type=_F32)


def _layer_norm(h, g, b):
    mu = jnp.mean(h, axis=-1, keepdims=True)
    hc = h - mu
    var = jnp.mean(hc * hc, axis=-1, keepdims=True)
    return hc * lax.rsqrt(var + LN_EPS) * g + b


def _silu(x):
    return x * jax.nn.sigmoid(x)


def _gelu_tanh(x):
    c = 0.7978845608028654
    return 0.5 * x * (1.0 + jnp.tanh(c * (x + 0.044715 * (x * x * x))))


def _pack_halves(v):
    half = v.shape[1] // 2
    lo = lax.bitcast_convert_type(v[:, :half].astype(_BF16).astype(_F32), _U32)
    hi = lax.bitcast_convert_type(v[:, half:].astype(_BF16).astype(_F32), _U32)
    return (hi & _U32(0xFFFF0000)) | (lo >> 16)


def _unpack_halves(w):
    lo = lax.bitcast_convert_type(w << 16, _F32)
    hi = lax.bitcast_convert_type(w & _U32(0xFFFF0000), _F32)
    return lo, hi


def _store_pieces(ref, w):
    for c in range(PIECES):
        ref[c] = w[:, c * LANES:(c + 1) * LANES]


def _load_pieces(ref):
    return jnp.concatenate([ref[c] for c in range(PIECES)], axis=1)


def _pool_kernel(x_ref, win_ref, wgrp_ref, scale_ref, wout_ref, g_ref, b_ref,
                 o_ref, op_ref, zs_ref, y_ref):
    s = pl.program_id(1)
    ts = x_ref.shape[1]
    x = x_ref[0]
    z = _dot(x.astype(_BF16), win_ref[...])

    @pl.when(s == 0)
    def _():
        zs_ref[0:POOL_HALO, :] = jnp.zeros((POOL_HALO, D_MODEL), _F32)

    zs_ref[POOL_HALO:POOL_HALO + ts, :] = z
    pos = s * ts + lax.broadcasted_iota(jnp.int32, (ts, 1), 0)
    for g, w in enumerate(POOL_WINDOWS):
        c0 = g * POOL_GROUP_DIM
        c1 = c0 + POOL_GROUP_DIM
        zg = zs_ref[POOL_HALO:POOL_HALO + ts, c0:c1]
        acc = zg
        for k in range(1, w):
            acc = acc + zs_ref[POOL_HALO - k:POOL_HALO - k + ts, c0:c1]
        cnt = jnp.minimum(pos + 1, w).astype(_F32)
        pooled = acc / cnt - zg
        yg = _dot(pooled.astype(_BF16), wgrp_ref[g]) * scale_ref[:, c0:c1]
        y_ref[:, c0:c1] = yg.astype(_BF16)
    zs_ref[0:POOL_HALO, :] = zs_ref[ts:ts + POOL_HALO, :]
    mix = _dot(y_ref[...], wout_ref[...])
    out = _layer_norm(DEEPNORM_ALPHA * x + mix, g_ref[...], b_ref[...])
    o_ref[0] = out
    _store_pieces(op_ref, _pack_halves(out))


def _pool_layer(x, w_in, w_grp, scale, w_out, ln_g, ln_b):
    bsz, seq, d = x.shape
    ts = POOL_TILE
    steps = seq // ts
    const2 = lambda b, s: (0, 0)
    out_shape = (jax.ShapeDtypeStruct((bsz, seq, d), _F32),
                 jax.ShapeDtypeStruct((PIECES, bsz * seq, LANES), _U32))
    tile = pl.BlockSpec((1, ts, d), lambda b, s: (b, s, 0))
    ptile = pl.BlockSpec((PIECES, ts, LANES), lambda b, s: (0, b * steps + s, 0))
    return pl.pallas_call(
        _pool_kernel,
        grid=(bsz, steps),
        in_specs=[
            tile,
            pl.BlockSpec((d, d), const2),
            pl.BlockSpec((len(POOL_WINDOWS), POOL_GROUP_DIM, POOL_GROUP_DIM), lambda b, s: (0, 0, 0)),
            pl.BlockSpec((1, d), const2),
            pl.BlockSpec((d, d), const2),
            pl.BlockSpec((1, d), const2),
            pl.BlockSpec((1, d), const2),
        ],
        out_specs=(tile, ptile),
        out_shape=out_shape,
        scratch_shapes=[pltpu.VMEM((POOL_HALO + ts, d), _F32),
                        pltpu.VMEM((ts, d), _BF16)],
        compiler_params=pltpu.CompilerParams(
            dimension_semantics=("arbitrary", "arbitrary"),
            vmem_limit_bytes=VMEM_LIMIT),
        name="pool_mixer",
    )(x, w_in.astype(_BF16), w_grp.astype(_BF16), scale.reshape(1, d),
      w_out.astype(_BF16), ln_g.reshape(1, d), ln_b.reshape(1, d))


def _sgu_kernel(x_ref, win_ref, bin_ref, lng_ref, lnb_ref, ws_ref, bs_ref, wout_ref,
                g_ref, b_ref, o_ref, op_ref, gated_ref):
    ts = x_ref.shape[0]
    x = x_ref[...]
    xb = x.astype(_BF16)
    v = _gelu_tanh(_dot(xb, win_ref[:, SGU_WIDTH:]) + bin_ref[:, SGU_WIDTH:])
    v = _layer_norm(v, lng_ref[...], lnb_ref[...]).astype(_BF16)
    u = _gelu_tanh(_dot(xb, win_ref[:, :SGU_WIDTH]) + bin_ref[:, :SGU_WIDTH])
    for c in range(ts // SGU_CHUNK):
        r0 = c * SGU_CHUNK
        for h in range(SGU_HEADS):
            c0 = h * SGU_HEAD_DIM
            sv = _dot(ws_ref[h], v[r0:r0 + SGU_CHUNK, c0:c0 + SGU_HEAD_DIM]) + bs_ref[h]
            gated_ref[r0:r0 + SGU_CHUNK, c0:c0 + SGU_HEAD_DIM] = (
                u[r0:r0 + SGU_CHUNK, c0:c0 + SGU_HEAD_DIM] * sv).astype(_BF16)
    mix = _dot(gated_ref[...], wout_ref[...])
    out = _layer_norm(DEEPNORM_ALPHA * x + mix, g_ref[...], b_ref[...])
    o_ref[...] = out
    _store_pieces(op_ref, _pack_halves(out))


def _sgu_layer(x, w_in, b_in, ln_g, ln_b, w_s, b_s, w_out, mix_g, mix_b):
    n_tok, d = x.shape
    ts = SGU_TILE
    const2 = lambda i: (0, 0)
    const3 = lambda i: (0, 0, 0)
    causal = jnp.tril(jnp.ones((SGU_CHUNK, SGU_CHUNK), w_s.dtype))
    ws = (w_s * causal[None]).astype(_BF16)
    tile = pl.BlockSpec((ts, d), lambda i: (i, 0))
    ptile = pl.BlockSpec((PIECES, ts, LANES), lambda i: (0, i, 0))
    out_shape = (jax.ShapeDtypeStruct((n_tok, d), _F32),
                 jax.ShapeDtypeStruct((PIECES, n_tok, LANES), _U32))
    return pl.pallas_call(
        _sgu_kernel,
        grid=(n_tok // ts,),
        in_specs=[
            tile,
            pl.BlockSpec((d, 2 * SGU_WIDTH), const2),
            pl.BlockSpec((1, 2 * SGU_WIDTH), const2),
            pl.BlockSpec((1, SGU_WIDTH), const2),
            pl.BlockSpec((1, SGU_WIDTH), const2),
            pl.BlockSpec((SGU_HEADS, SGU_CHUNK, SGU_CHUNK), const3),
            pl.BlockSpec((SGU_HEADS, SGU_CHUNK, 1), const3),
            pl.BlockSpec((SGU_WIDTH, d), const2),
            pl.BlockSpec((1, d), const2),
            pl.BlockSpec((1, d), const2),
        ],
        out_specs=(tile, ptile),
        out_shape=out_shape,
        scratch_shapes=[pltpu.VMEM((ts, SGU_WIDTH), _BF16)],
        compiler_params=pltpu.CompilerParams(
            dimension_semantics=("arbitrary",),
            vmem_limit_bytes=VMEM_LIMIT),
        name="sgu_mixer",
    )(x, w_in.astype(_BF16), b_in.reshape(1, -1), ln_g.reshape(1, -1), ln_b.reshape(1, -1),
      ws, b_s.reshape(SGU_HEADS, SGU_CHUNK, 1), w_out.astype(_BF16),
      mix_g.reshape(1, d), mix_b.reshape(1, d))


def _all_max(a):
    return jnp.max(jnp.max(a, axis=0, keepdims=True), axis=1, keepdims=True)


def _all_min(a):
    return jnp.min(jnp.min(a, axis=0, keepdims=True), axis=1, keepdims=True)


def _all_sum(a):
    return jnp.sum(jnp.sum(a, axis=0, keepdims=True), axis=1, keepdims=True)


def _route_kernel(x_ref, whi_ref, wlo_ref, bias_ref, earlier_ref,
                  ek_ref, rk_ref, gk_ref, cnt_ref, carry_ref):
    i = pl.program_id(0)
    ts = x_ref.shape[0]

    @pl.when(i == 0)
    def _():
        carry_ref[...] = jnp.zeros_like(carry_ref)

    x = x_ref[...]
    x_hi = x.astype(_BF16)
    x_lo = (x - x_hi.astype(_F32)).astype(_BF16)
    nt = lambda a, b: lax.dot_general(a, b, (((1,), (1,)), ((), ())),
                                      preferred_element_type=_F32)
    logits = nt(whi_ref[...], x_hi) + (nt(whi_ref[...], x_lo) + nt(wlo_ref[...], x_hi))
    scores = jax.nn.sigmoid(logits).reshape(N_GROUPS, GROUP_SIZE, ts)
    biased = scores + bias_ref[...]
    neg_inf = jnp.float32(-jnp.inf)
    shape3 = (N_GROUPS, GROUP_SIZE, ts)
    in_grp = lax.broadcasted_iota(jnp.int32, shape3, 1)
    grp = lax.broadcasted_iota(jnp.int32, shape3, 0)
    eid = grp * GROUP_SIZE + in_grp

    m1 = jnp.max(biased, axis=1, keepdims=True)
    first1 = jnp.min(jnp.where(biased == m1, in_grp, GROUP_SIZE), axis=1, keepdims=True)
    m2 = jnp.max(jnp.where(in_grp == first1, neg_inf, biased), axis=1, keepdims=True)
    gscore = m1 + m2

    gid = lax.broadcasted_iota(jnp.int32, (N_GROUPS, 1, ts), 0)
    gsel = jnp.zeros((N_GROUPS, 1, ts), jnp.bool_)
    for _ in range(TOPK_GROUPS):
        m = jnp.max(gscore, axis=0, keepdims=True)
        first = jnp.min(jnp.where(gscore == m, gid, N_GROUPS), axis=0, keepdims=True)
        pick = gid == first
        gsel = jnp.logical_or(gsel, pick)
        gscore = jnp.where(pick, neg_inf, gscore)

    masked = jnp.where(gsel, biased, neg_inf)
    picked_any = jnp.zeros(shape3, jnp.bool_)
    e_k, s_k = [], []
    for _ in range(TOP_K):
        m = _all_max(masked)
        first = _all_min(jnp.where(masked == m, eid, N_EXPERTS))
        pick = eid == first
        picked_any = jnp.logical_or(picked_any, pick)
        masked = jnp.where(pick, neg_inf, masked)
        e_k.append(first)
        s_k.append(_all_sum(jnp.where(pick, scores, 0.0)))
    denom = s_k[0]
    for k in range(1, TOP_K):
        denom = denom + s_k[k]

    sel = picked_any.astype(_F32).reshape(N_EXPERTS, ts)
    rank = _dot(sel.astype(_BF16), earlier_ref[...]) + carry_ref[...]
    rank3 = rank.reshape(shape3)
    carry_ref[...] += jnp.sum(sel, axis=1, keepdims=True)
    cnt_ref[...] = carry_ref[...]

    for k in range(TOP_K):
        r = _all_sum(jnp.where(eid == e_k[k], rank3, 0.0))
        ek_ref[k:k + 1, :] = e_k[k].reshape(1, ts)
        rk_ref[k:k + 1, :] = r.reshape(1, ts).astype(jnp.int32)
        gk_ref[k:k + 1, :] = (s_k[k] / denom * ROUTED_SCALE).reshape(1, ts)


def _route(x, w_router, router_bias, part, n_parts):
    n_tok, d = x.shape
    ts = ROUTE_TILE
    n_part = n_tok // n_parts
    steps = n_part // ts
    out_shape = (jax.ShapeDtypeStruct((TOP_K, n_part), jnp.int32),
                 jax.ShapeDtypeStruct((TOP_K, n_part), jnp.int32),
                 jax.ShapeDtypeStruct((TOP_K, n_part), _F32),
                 jax.ShapeDtypeStruct((N_EXPERTS, 1), _F32))
    kspec = pl.BlockSpec((TOP_K, ts), lambda i: (0, i))
    const2 = lambda i: (0, 0)
    w_t = w_router.T
    w_hi = w_t.astype(_BF16)
    w_lo = (w_t - w_hi.astype(_F32)).astype(_BF16)
    pos = jnp.arange(ts, dtype=jnp.int32)
    earlier = (pos[:, None] < pos[None, :]).astype(_BF16)
    return pl.pallas_call(
        _route_kernel,
        grid=(steps,),
        in_specs=[
            pl.BlockSpec((ts, d), lambda i: (part * steps + i, 0)),
            pl.BlockSpec((N_EXPERTS, d), const2),
            pl.BlockSpec((N_EXPERTS, d), const2),
            pl.BlockSpec((N_GROUPS, GROUP_SIZE, 1), lambda i: (0, 0, 0)),
            pl.BlockSpec((ts, ts), const2),
        ],
        out_specs=(kspec, kspec, kspec, pl.BlockSpec((N_EXPERTS, 1), const2)),
        out_shape=out_shape,
        scratch_shapes=[pltpu.VMEM((N_EXPERTS, 1), _F32)],
        compiler_params=pltpu.CompilerParams(
            dimension_semantics=("arbitrary",),
            vmem_limit_bytes=VMEM_LIMIT),
        name="moe_route",
    )(x, w_hi, w_lo, router_bias.reshape(N_GROUPS, GROUP_SIZE, 1), earlier)


def _sc_mesh():
    return plsc.VectorSubcoreMesh(core_axis_name="core", subcore_axis_name="subcore")


def _sc_scatter_rows(src, idx, n_out, part, n_parts):
    d = src.shape[1]
    n_part = idx.shape[1]
    wins = n_part // SC_WINDOW

    def body(src_hbm, idx_hbm, out_hbm):
        def step(src_vmem, idx_vmem):
            for k in range(TOP_K):
                pltpu.sync_copy(src_vmem, out_hbm.at[idx_vmem.at[k]])

        pltpu.emit_pipeline(
            step,
            grid=(PIECES * wins,),
            in_specs=[pl.BlockSpec((SC_WINDOW, d),
                                   index_map=lambda i: ((i // wins) * n_parts * wins
                                                        + part * wins + i % wins, 0)),
                      pl.BlockSpec((TOP_K, SC_WINDOW), index_map=lambda i: (i // wins, i % wins))],
            out_specs=[],
            core_axis_name=("core", "subcore"),
            dimension_semantics=(pltpu.PARALLEL,),
        )(src_hbm, idx_hbm)

    return pl.kernel(body, out_type=jax.ShapeDtypeStruct((n_out, d), src.dtype),
                     mesh=_sc_mesh(), scratch_types=[], name="sc_scatter_rows")(src, idx)


def _sc_gather_rows(table, idx):
    d = table.shape[1]
    n_idx_rows, n_tok = idx.shape
    wins = n_tok // SC_WINDOW

    def body(table_hbm, idx_hbm, out_hbm):
        def step(idx_vmem, out_vmem):
            pltpu.sync_copy(table_hbm.at[idx_vmem.at[0]], out_vmem)

        pltpu.emit_pipeline(
            step,
            grid=(n_idx_rows * wins,),
            in_specs=[pl.BlockSpec((1, SC_WINDOW), index_map=lambda i: (i // wins, i % wins))],
            out_specs=[pl.BlockSpec((SC_WINDOW, d), index_map=lambda i: (i, 0))],
            core_axis_name=("core", "subcore"),
            dimension_semantics=(pltpu.PARALLEL,),
        )(idx_hbm, out_hbm)

    return pl.kernel(body, out_type=jax.ShapeDtypeStruct((n_idx_rows * n_tok, d), table.dtype),
                     mesh=_sc_mesh(), scratch_types=[], name="sc_gather_rows")(table, idx)


def _index_kernel(start_ref, ek_ref, rk_ref, idx_ref, *, n_rows):
    e = ek_ref[...]
    start = jnp.zeros(e.shape, jnp.int32)
    for ex in range(N_EXPERTS):
        start = jnp.where(e == ex, start_ref[ex], start)
    dest = start + rk_ref[...]
    for c in range(PIECES):
        idx_ref[c] = dest + c * n_rows


def _row_indices(padded_start, e_k, r_k, n_rows):
    n_tok = e_k.shape[1]
    ts = INDEX_TILE
    kspec = pl.BlockSpec((TOP_K, ts), lambda i, st: (0, i))
    grid_spec = pltpu.PrefetchScalarGridSpec(
        num_scalar_prefetch=1,
        grid=(n_tok // ts,),
        in_specs=[kspec, kspec],
        out_specs=pl.BlockSpec((PIECES, TOP_K, ts), lambda i, st: (0, 0, i)),
    )
    return pl.pallas_call(
        functools.partial(_index_kernel, n_rows=n_rows),
        grid_spec=grid_spec,
        out_shape=jax.ShapeDtypeStruct((PIECES, TOP_K, n_tok), jnp.int32),
        compiler_params=pltpu.CompilerParams(dimension_semantics=("arbitrary",)),
        name="moe_row_indices",
    )(padded_start, e_k, r_k)


def _expert_kernel(chunk0_ref, nchunk_ref, cnt_ref, total_ref,
                   xs_hbm, wg_ref, wu_ref, wd_ref, o_hbm,
                   xbuf, obuf, wgu_bf, wd_bf, sem_in, sem_out):
    e = pl.program_id(0)
    total = total_ref[0]
    ch = EXPERT_ROWS
    ahead = IN_SLOTS - 1

    def in_copy(g):
        slot = g % IN_SLOTS
        return pltpu.make_async_copy(xs_hbm.at[:, pl.ds(g * ch, ch), :], xbuf.at[slot],
                                     sem_in.at[slot])

    def out_copy(g):
        slot = g % OUT_SLOTS
        return pltpu.make_async_copy(obuf.at[slot], o_hbm.at[:, pl.ds(g * ch, ch), :],
                                     sem_out.at[slot])

    @pl.when(e == 0)
    def _():
        obuf[...] = jnp.zeros_like(obuf)
        for g in range(ahead):
            @pl.when(g < total)
            def _():
                in_copy(g).start()

    wgu_bf[:, :EXPERT_DIM] = wg_ref[0, 0].astype(_BF16)
    wgu_bf[:, EXPERT_DIM:] = wu_ref[0, 0].astype(_BF16)
    wd_bf[...] = wd_ref[0, 0].astype(_BF16)

    def chunk(j, carry):
        g = chunk0_ref[e] + j
        in_copy(g).wait()

        @pl.when(g + ahead < total)
        def _():
            in_copy(g + ahead).start()

        @pl.when(g >= OUT_SLOTS)
        def _():
            out_copy(g - OUT_SLOTS).wait()

        islot = g % IN_SLOTS
        oslot = g % OUT_SLOTS
        valid = cnt_ref[e] - j * ch

        def sub_block(s):
            r0 = s * EXPERT_SUB
            w = jnp.concatenate([xbuf[islot, c, r0:r0 + EXPERT_SUB, :] for c in range(PIECES)],
                                axis=1)
            lo, hi = _unpack_halves(w)
            gu = (_dot(lo.astype(_BF16), wgu_bf[:PACKED]) +
                  _dot(hi.astype(_BF16), wgu_bf[PACKED:]))
            h = _silu(gu[:, :EXPERT_DIM]) * gu[:, EXPERT_DIM:]
            y = _pack_halves(_dot(h.astype(_BF16), wd_bf[...]))
            for c in range(PIECES):
                obuf[oslot, c, r0:r0 + EXPERT_SUB, :] = y[:, c * LANES:(c + 1) * LANES]

        n_sub = ch // EXPERT_SUB

        @pl.when(valid >= ch)
        def _():
            for s in range(n_sub):
                sub_block(s)

        @pl.when(valid < ch)
        def _():
            for s in range(n_sub):
                @pl.when(s * EXPERT_SUB < valid)
                def _():
                    sub_block(s)

        out_copy(g).start()
        return carry

    lax.fori_loop(0, nchunk_ref[e], chunk, 0)

    @pl.when(e == pl.num_programs(0) - 1)
    def _():
        for back in range(OUT_SLOTS, 0, -1):
            @pl.when(total >= back)
            def _():
                out_copy(total - back).wait()


def _experts(layer, chunk0, nchunk, counts, total, xs, w_gate, w_up, w_down):
    d = D_MODEL
    ch = EXPERT_ROWS
    wspec_in = pl.BlockSpec((1, 1, d, EXPERT_DIM), lambda e, *_: (layer, e, 0, 0))
    any_spec = pl.BlockSpec(memory_space=pl.ANY)
    grid_spec = pltpu.PrefetchScalarGridSpec(
        num_scalar_prefetch=4,
        grid=(N_EXPERTS,),
        in_specs=[
            any_spec,
            wspec_in,
            wspec_in,
            pl.BlockSpec((1, 1, EXPERT_DIM, d), lambda e, *_: (layer, e, 0, 0)),
        ],
        out_specs=any_spec,
        scratch_shapes=[pltpu.VMEM((IN_SLOTS, PIECES, ch, LANES), _U32),
                        pltpu.VMEM((OUT_SLOTS, PIECES, ch, LANES), _U32),
                        pltpu.VMEM((d, 2 * EXPERT_DIM), _BF16),
                        pltpu.VMEM((EXPERT_DIM, d), _BF16),
                        pltpu.SemaphoreType.DMA((IN_SLOTS,)),
                        pltpu.SemaphoreType.DMA((OUT_SLOTS,))],
    )
    return pl.pallas_call(
        _expert_kernel,
        grid_spec=grid_spec,
        out_shape=jax.ShapeDtypeStruct(xs.shape, _U32),
        compiler_params=pltpu.CompilerParams(
            dimension_semantics=("arbitrary",),
            vmem_limit_bytes=VMEM_LIMIT),
        name="moe_experts",
    )(chunk0, nchunk, counts, total, xs, w_gate, w_up, w_down)


def _combine_kernel(x_ref, p_ref, gate_ref, sg_ref, su_ref, sd_ref, g_ref, b_ref, *rest):
    o_ref = rest[-1]
    x = x_ref[...]
    xb = x.astype(_BF16)
    gates = gate_ref[...].T
    r_lo = r_hi = None
    for k in range(TOP_K):
        w = jnp.concatenate([p_ref[c, k] for c in range(PIECES)], axis=1)
        lo, hi = _unpack_halves(w)
        gk = gates[:, k:k + 1]
        r_lo = gk * lo if r_lo is None else r_lo + gk * lo
        r_hi = gk * hi if r_hi is None else r_hi + gk * hi
    routed = jnp.concatenate([r_lo, r_hi], axis=1)
    h = _silu(_dot(xb, sg_ref[...])) * _dot(xb, su_ref[...])
    shared = _dot(h.astype(_BF16), sd_ref[...])
    o_ref[...] = _layer_norm(DEEPNORM_ALPHA * x + (routed + shared), g_ref[...], b_ref[...])


def _combine(x, picked, gates, sh_gate, sh_up, sh_down, ln_g, ln_b, part, n_parts, prev):
    n_tok, d = x.shape
    ts = COMBINE_TILE
    steps = n_tok // n_parts // ts
    const2 = lambda i: (0, 0)
    rows_map = lambda i: (part * steps + i, 0)
    in_specs = [
        pl.BlockSpec((ts, d), rows_map),
        pl.BlockSpec((PIECES, TOP_K, ts, LANES), lambda i: (0, 0, i, 0)),
        pl.BlockSpec((TOP_K, ts), lambda i: (0, i)),
        pl.BlockSpec((d, EXPERT_DIM), const2),
        pl.BlockSpec((d, EXPERT_DIM), const2),
        pl.BlockSpec((EXPERT_DIM, d), const2),
        pl.BlockSpec((1, d), const2),
        pl.BlockSpec((1, d), const2),
    ]
    args = [x, picked, gates, sh_gate.astype(_BF16), sh_up.astype(_BF16), sh_down.astype(_BF16),
            ln_g.reshape(1, d), ln_b.reshape(1, d)]
    aliases = {}
    if prev is not None:
        in_specs.append(pl.BlockSpec(memory_space=pl.ANY))
        aliases = {len(args): 0}
        args.append(prev)
    return pl.pallas_call(
        _combine_kernel,
        grid=(steps,),
        in_specs=in_specs,
        out_specs=pl.BlockSpec((ts, d), rows_map),
        out_shape=jax.ShapeDtypeStruct((n_tok, d), _F32),
        input_output_aliases=aliases,
        compiler_params=pltpu.CompilerParams(
            dimension_semantics=("arbitrary",),
            vmem_limit_bytes=VMEM_LIMIT),
        name="moe_combine",
    )(*args)


def _moe_layer(layer, x, x_packed, w_router, router_bias, w_gate, w_up, w_down,
               sh_gate, sh_up, sh_down, ln_g, ln_b):
    n_tok, d = x.shape
    bm = EXPERT_ROWS
    n_part = n_tok // TOKEN_PARTS
    n_rows = n_part * TOP_K + N_EXPERTS * bm
    x_flat = x_packed.reshape(PIECES * n_tok, LANES)

    routed = []
    for part in range(TOKEN_PARTS):
        e_k, r_k, g_k, counts = _route(x, w_router, router_bias, part, TOKEN_PARTS)
        counts = counts.reshape(N_EXPERTS).astype(jnp.int32)
        padded = (counts + bm - 1) // bm * bm
        padded_end = jnp.cumsum(padded)
        padded_start = padded_end - padded
        idx = _row_indices(padded_start, e_k, r_k, n_rows).reshape(PIECES * TOP_K, n_part)
        routed.append((idx, g_k, padded_start // bm, padded // bm, counts,
                       padded_end[-1:] // bm))

    rows = []
    for part, (idx, _, chunk0, nchunk, counts, total) in enumerate(routed):
        xs = _sc_scatter_rows(x_flat, idx, PIECES * n_rows, part, TOKEN_PARTS)
        rows.append(_experts(layer, chunk0, nchunk, counts, total,
                             xs.reshape(PIECES, n_rows, LANES), w_gate, w_up, w_down))

    out = None
    for part, (idx, gates, *_) in enumerate(routed):
        picked = _sc_gather_rows(rows[part].reshape(PIECES * n_rows, LANES), idx)
        picked = picked.reshape(PIECES, TOP_K, n_part, LANES)
        out = _combine(x, picked, gates, sh_gate, sh_up, sh_down, ln_g, ln_b,
                       part, TOKEN_PARTS, out)
    return out


def kernel(x, pool_w_in, pool_w_grp, pool_scale, pool_w_out, sgu_w_in, sgu_b_in, sgu_ln_g, sgu_ln_b, sgu_w_s, sgu_b_s, sgu_w_out, ln_mix_g, ln_mix_b, moe_w_router, moe_router_bias, moe_w_gate, moe_w_up, moe_w_down, moe_sh_gate, moe_sh_up, moe_sh_down, ln_ffn_g, ln_ffn_b):
    bsz, seq, d = x.shape
    n_tok = bsz * seq

    def moe(i, h, h_packed):
        return _moe_layer(i, h, h_packed, moe_w_router[i], moe_router_bias[i], moe_w_gate,
                          moe_w_up, moe_w_down, moe_sh_gate[i], moe_sh_up[i],
                          moe_sh_down[i], ln_ffn_g[i], ln_ffn_b[i])

    h, h_packed = _pool_layer(x, pool_w_in[0], pool_w_grp[0], pool_scale[0], pool_w_out[0],
                              ln_mix_g[0], ln_mix_b[0])
    h = moe(0, h.reshape(n_tok, d), h_packed)
    h, h_packed = _sgu_layer(h, sgu_w_in[0], sgu_b_in[0], sgu_ln_g[0], sgu_ln_b[0], sgu_w_s[0],
                             sgu_b_s[0], sgu_w_out[0], ln_mix_g[1], ln_mix_b[1])
    h = moe(1, h, h_packed)
    return h.reshape(bsz, seq, d)
```

```python
import functools

import jax
import jax.numpy as jnp
from jax import lax
from jax.experimental import pallas as pl
from jax.experimental.pallas import tpu as pltpu
from jax.experimental.pallas import tpu_sc as plsc

D_MODEL = 1024
DEPTH = 2
POOL_WINDOWS = (2, 4, 8, 16)
POOL_GROUP_DIM = D_MODEL // len(POOL_WINDOWS)
POOL_HALO = 16
SGU_CHUNK = 128
SGU_HEADS = 4
SGU_WIDTH = 2 * D_MODEL
SGU_HEAD_DIM = SGU_WIDTH // SGU_HEADS
N_EXPERTS = 64
TOP_K = 8
N_GROUPS = 8
GROUP_SIZE = N_EXPERTS // N_GROUPS
TOPK_GROUPS = 4
EXPERT_DIM = D_MODEL // 4
ROUTED_SCALE = 2.5
DEEPNORM_ALPHA = (2 * DEPTH) ** 0.25
LN_EPS = 1e-5

LANES = 128
PACKED = D_MODEL // 2
PIECES = PACKED // LANES

POOL_TILE = 512
SGU_TILE = 256
ROUTE_TILE = 512
INDEX_TILE = 2048
EXPERT_ROWS = 512
EXPERT_SUB = 256
EXPERT_TAIL = 128
IN_SLOTS = 4
OUT_SLOTS = 3
COMBINE_TILE = 256
SC_WINDOW = 128
TOKEN_PARTS = 2
VMEM_LIMIT = 56 * 1024 * 1024

_F32 = jnp.float32
_BF16 = jnp.bfloat16
_U32 = jnp.uint32


def _dot(a, b):
    return jnp.dot(a, b, preferred_element_type=_F32)


def _layer_norm(h, g, b):
    mu = jnp.mean(h, axis=-1, keepdims=True)
    hc = h - mu
    var = jnp.mean(hc * hc, axis=-1, keepdims=True)
    return hc * lax.rsqrt(var + LN_EPS) * g + b


def _silu(x):
    return x * jax.nn.sigmoid(x)


def _gelu_tanh(x):
    c = 0.7978845608028654
    return 0.5 * x * (1.0 + jnp.tanh(c * (x + 0.044715 * (x * x * x))))


def _pack_halves(v):
    half = v.shape[1] // 2
    lo = lax.bitcast_convert_type(v[:, :half].astype(_BF16).astype(_F32), _U32)
    hi = lax.bitcast_convert_type(v[:, half:].astype(_BF16).astype(_F32), _U32)
    return (hi & _U32(0xFFFF0000)) | (lo >> 16)


def _unpack_halves(w):
    lo = lax.bitcast_convert_type(w << 16, _F32)
    hi = lax.bitcast_convert_type(w & _U32(0xFFFF0000), _F32)
    return lo, hi


def _store_pieces(ref, w):
    for c in range(PIECES):
        ref[c] = w[:, c * LANES:(c + 1) * LANES]


def _load_pieces(ref):
    return jnp.concatenate([ref[c] for c in range(PIECES)], axis=1)


def _pool_kernel(x_ref, win_ref, wgrp_ref, scale_ref, wout_ref, g_ref, b_ref,
                 o_ref, op_ref, zs_ref, y_ref):
    s = pl.program_id(1)
    ts = x_ref.shape[1]
    x = x_ref[0]
    z = _dot(x.astype(_BF16), win_ref[...])

    @pl.when(s == 0)
    def _():
        zs_ref[0:POOL_HALO, :] = jnp.zeros((POOL_HALO, D_MODEL), _F32)

    zs_ref[POOL_HALO:POOL_HALO + ts, :] = z
    pos = s * ts + lax.broadcasted_iota(jnp.int32, (ts, 1), 0)
    for g, w in enumerate(POOL_WINDOWS):
        c0 = g * POOL_GROUP_DIM
        c1 = c0 + POOL_GROUP_DIM
        zg = zs_ref[POOL_HALO:POOL_HALO + ts, c0:c1]
        acc = zg
        for k in range(1, w):
            acc = acc + zs_ref[POOL_HALO - k:POOL_HALO - k + ts, c0:c1]
        cnt = jnp.minimum(pos + 1, w).astype(_F32)
        pooled = acc / cnt - zg
        yg = _dot(pooled.astype(_BF16), wgrp_ref[g]) * scale_ref[:, c0:c1]
        y_ref[:, c0:c1] = yg.astype(_BF16)
    zs_ref[0:POOL_HALO, :] = zs_ref[ts:ts + POOL_HALO, :]
    mix = _dot(y_ref[...], wout_ref[...])
    out = _layer_norm(DEEPNORM_ALPHA * x + mix, g_ref[...], b_ref[...])
    o_ref[0] = out
    _store_pieces(op_ref, _pack_halves(out))


def _pool_layer(x, w_in, w_grp, scale, w_out, ln_g, ln_b):
    bsz, seq, d = x.shape
    ts = POOL_TILE
    steps = seq // ts
    const2 = lambda b, s: (0, 0)
    out_shape = (jax.ShapeDtypeStruct((bsz, seq, d), _F32),
                 jax.ShapeDtypeStruct((PIECES, bsz * seq, LANES), _U32))
    tile = pl.BlockSpec((1, ts, d), lambda b, s: (b, s, 0))
    ptile = pl.BlockSpec((PIECES, ts, LANES), lambda b, s: (0, b * steps + s, 0))
    return pl.pallas_call(
        _pool_kernel,
        grid=(bsz, steps),
        in_specs=[
            tile,
            pl.BlockSpec((d, d), const2),
            pl.BlockSpec((len(POOL_WINDOWS), POOL_GROUP_DIM, POOL_GROUP_DIM), lambda b, s: (0, 0, 0)),
            pl.BlockSpec((1, d), const2),
            pl.BlockSpec((d, d), const2),
            pl.BlockSpec((1, d), const2),
            pl.BlockSpec((1, d), const2),
        ],
        out_specs=(tile, ptile),
        out_shape=out_shape,
        scratch_shapes=[pltpu.VMEM((POOL_HALO + ts, d), _F32),
                        pltpu.VMEM((ts, d), _BF16)],
        compiler_params=pltpu.CompilerParams(
            dimension_semantics=("arbitrary", "arbitrary"),
            vmem_limit_bytes=VMEM_LIMIT),
        name="pool_mixer",
    )(x, w_in.astype(_BF16), w_grp.astype(_BF16), scale.reshape(1, d),
      w_out.astype(_BF16), ln_g.reshape(1, d), ln_b.reshape(1, d))


def _sgu_kernel(x_ref, win_ref, bin_ref, lng_ref, lnb_ref, ws_ref, bs_ref, wout_ref,
                g_ref, b_ref, o_ref, op_ref, gated_ref):
    ts = x_ref.shape[0]
    x = x_ref[...]
    xb = x.astype(_BF16)
    v = _gelu_tanh(_dot(xb, win_ref[:, SGU_WIDTH:]) + bin_ref[:, SGU_WIDTH:])
    v = _layer_norm(v, lng_ref[...], lnb_ref[...]).astype(_BF16)
    u = _gelu_tanh(_dot(xb, win_ref[:, :SGU_WIDTH]) + bin_ref[:, :SGU_WIDTH])
    for c in range(ts // SGU_CHUNK):
        r0 = c * SGU_CHUNK
        for h in range(SGU_HEADS):
            c0 = h * SGU_HEAD_DIM
            sv = _dot(ws_ref[h], v[r0:r0 + SGU_CHUNK, c0:c0 + SGU_HEAD_DIM]) + bs_ref[h]
            gated_ref[r0:r0 + SGU_CHUNK, c0:c0 + SGU_HEAD_DIM] = (
                u[r0:r0 + SGU_CHUNK, c0:c0 + SGU_HEAD_DIM] * sv).astype(_BF16)
    mix = _dot(gated_ref[...], wout_ref[...])
    out = _layer_norm(DEEPNORM_ALPHA * x + mix, g_ref[...], b_ref[...])
    o_ref[...] = out
    _store_pieces(op_ref, _pack_halves(out))


def _sgu_layer(x, w_in, b_in, ln_g, ln_b, w_s, b_s, w_out, mix_g, mix_b):
    n_tok, d = x.shape
    ts = SGU_TILE
    const2 = lambda i: (0, 0)
    const3 = lambda i: (0, 0, 0)
    causal = jnp.tril(jnp.ones((SGU_CHUNK, SGU_CHUNK), w_s.dtype))
    ws = (w_s * causal[None]).astype(_BF16)
    tile = pl.BlockSpec((ts, d), lambda i: (i, 0))
    ptile = pl.BlockSpec((PIECES, ts, LANES), lambda i: (0, i, 0))
    out_shape = (jax.ShapeDtypeStruct((n_tok, d), _F32),
                 jax.ShapeDtypeStruct((PIECES, n_tok, LANES), _U32))
    return pl.pallas_call(
        _sgu_kernel,
        grid=(n_tok // ts,),
        in_specs=[
            tile,
            pl.BlockSpec((d, 2 * SGU_WIDTH), const2),
            pl.BlockSpec((1, 2 * SGU_WIDTH), const2),
            pl.BlockSpec((1, SGU_WIDTH), const2),
            pl.BlockSpec((1, SGU_WIDTH), const2),
            pl.BlockSpec((SGU_HEADS, SGU_CHUNK, SGU_CHUNK), const3),
            pl.BlockSpec((SGU_HEADS, SGU_CHUNK, 1), const3),
            pl.BlockSpec((SGU_WIDTH, d), const2),
            pl.BlockSpec((1, d), const2),
            pl.BlockSpec((1, d), const2),
        ],
        out_specs=(tile, ptile),
        out_shape=out_shape,
        scratch_shapes=[pltpu.VMEM((ts, SGU_WIDTH), _BF16)],
        compiler_params=pltpu.CompilerParams(
            dimension_semantics=("arbitrary",),
            vmem_limit_bytes=VMEM_LIMIT),
        name="sgu_mixer",
    )(x, w_in.astype(_BF16), b_in.reshape(1, -1), ln_g.reshape(1, -1), ln_b.reshape(1, -1),
      ws, b_s.reshape(SGU_HEADS, SGU_CHUNK, 1), w_out.astype(_BF16),
      mix_g.reshape(1, d), mix_b.reshape(1, d))


def _all_max(a):
    return jnp.max(jnp.max(a, axis=0, keepdims=True), axis=1, keepdims=True)


def _all_min(a):
    return jnp.min(jnp.min(a, axis=0, keepdims=True), axis=1, keepdims=True)


def _all_sum(a):
    return jnp.sum(jnp.sum(a, axis=0, keepdims=True), axis=1, keepdims=True)


def _route_kernel(x_ref, whi_ref, wlo_ref, bias_ref, earlier_ref,
                  ek_ref, rk_ref, gk_ref, cnt_ref, carry_ref):
    i = pl.program_id(0)
    ts = x_ref.shape[0]

    @pl.when(i == 0)
    def _():
        carry_ref[...] = jnp.zeros_like(carry_ref)

    x = x_ref[...]
    x_hi = x.astype(_BF16)
    x_lo = (x - x_hi.astype(_F32)).astype(_BF16)
    nt = lambda a, b: lax.dot_general(a, b, (((1,), (1,)), ((), ())),
                                      preferred_element_type=_F32)
    logits = nt(whi_ref[...], x_hi) + (nt(whi_ref[...], x_lo) + nt(wlo_ref[...], x_hi))
    scores = jax.nn.sigmoid(logits).reshape(N_GROUPS, GROUP_SIZE, ts)
    biased = scores + bias_ref[...]
    neg_inf = jnp.float32(-jnp.inf)
    shape3 = (N_GROUPS, GROUP_SIZE, ts)
    in_grp = lax.broadcasted_iota(jnp.int32, shape3, 1)
    grp = lax.broadcasted_iota(jnp.int32, shape3, 0)
    eid = grp * GROUP_SIZE + in_grp

    m1 = jnp.max(biased, axis=1, keepdims=True)
    first1 = jnp.min(jnp.where(biased == m1, in_grp, GROUP_SIZE), axis=1, keepdims=True)
    m2 = jnp.max(jnp.where(in_grp == first1, neg_inf, biased), axis=1, keepdims=True)
    gscore = m1 + m2

    gid = lax.broadcasted_iota(jnp.int32, (N_GROUPS, 1, ts), 0)
    gsel = jnp.zeros((N_GROUPS, 1, ts), jnp.bool_)
    for _ in range(TOPK_GROUPS):
        m = jnp.max(gscore, axis=0, keepdims=True)
        first = jnp.min(jnp.where(gscore == m, gid, N_GROUPS), axis=0, keepdims=True)
        pick = gid == first
        gsel = jnp.logical_or(gsel, pick)
        gscore = jnp.where(pick, neg_inf, gscore)

    masked = jnp.where(gsel, biased, neg_inf)
    picked_any = jnp.zeros(shape3, jnp.bool_)
    e_k, s_k = [], []
    for _ in range(TOP_K):
        m = _all_max(masked)
        first = _all_min(jnp.where(masked == m, eid, N_EXPERTS))
        pick = eid == first
        picked_any = jnp.logical_or(picked_any, pick)
        masked = jnp.where(pick, neg_inf, masked)
        e_k.append(first)
        s_k.append(_all_sum(jnp.where(pick, scores, 0.0)))
    denom = s_k[0]
    for k in range(1, TOP_K):
        denom = denom + s_k[k]

    sel = picked_any.astype(_F32).reshape(N_EXPERTS, ts)
    rank = _dot(sel.astype(_BF16), earlier_ref[...]) + carry_ref[...]
    rank3 = rank.reshape(shape3)
    carry_ref[...] += jnp.sum(sel, axis=1, keepdims=True)
    cnt_ref[...] = carry_ref[...]

    for k in range(TOP_K):
        r = _all_sum(jnp.where(eid == e_k[k], rank3, 0.0))
        ek_ref[k:k + 1, :] = e_k[k].reshape(1, ts)
        rk_ref[k:k + 1, :] = r.reshape(1, ts).astype(jnp.int32)
        gk_ref[k:k + 1, :] = (s_k[k] / denom * ROUTED_SCALE).reshape(1, ts)


def _route(x, w_router, router_bias, part, n_parts):
    n_tok, d = x.shape
    ts = ROUTE_TILE
    n_part = n_tok // n_parts
    steps = n_part // ts
    out_shape = (jax.ShapeDtypeStruct((TOP_K, n_part), jnp.int32),
                 jax.ShapeDtypeStruct((TOP_K, n_part), jnp.int32),
                 jax.ShapeDtypeStruct((TOP_K, n_part), _F32),
                 jax.ShapeDtypeStruct((N_EXPERTS, 1), _F32))
    kspec = pl.BlockSpec((TOP_K, ts), lambda i: (0, i))
    const2 = lambda i: (0, 0)
    w_t = w_router.T
    w_hi = w_t.astype(_BF16)
    w_lo = (w_t - w_hi.astype(_F32)).astype(_BF16)
    pos = jnp.arange(ts, dtype=jnp.int32)
    earlier = (pos[:, None] < pos[None, :]).astype(_BF16)
    return pl.pallas_call(
        _route_kernel,
        grid=(steps,),
        in_specs=[
            pl.BlockSpec((ts, d), lambda i: (part * steps + i, 0)),
            pl.BlockSpec((N_EXPERTS, d), const2),
            pl.BlockSpec((N_EXPERTS, d), const2),
            pl.BlockSpec((N_GROUPS, GROUP_SIZE, 1), lambda i: (0, 0, 0)),
            pl.BlockSpec((ts, ts), const2),
        ],
        out_specs=(kspec, kspec, kspec, pl.BlockSpec((N_EXPERTS, 1), const2)),
        out_shape=out_shape,
        scratch_shapes=[pltpu.VMEM((N_EXPERTS, 1), _F32)],
        compiler_params=pltpu.CompilerParams(
            dimension_semantics=("arbitrary",),
            vmem_limit_bytes=VMEM_LIMIT),
        name="moe_route",
    )(x, w_hi, w_lo, router_bias.reshape(N_GROUPS, GROUP_SIZE, 1), earlier)


def _sc_mesh():
    return plsc.VectorSubcoreMesh(core_axis_name="core", subcore_axis_name="subcore")


def _sc_scatter_rows(src, idx, n_out, part, n_parts):
    d = src.shape[1]
    n_part = idx.shape[1]
    wins = n_part // SC_WINDOW

    def body(src_hbm, idx_hbm, out_hbm):
        def step(src_vmem, idx_vmem):
            for k in range(TOP_K):
                pltpu.sync_copy(src_vmem, out_hbm.at[idx_vmem.at[k]])

        pltpu.emit_pipeline(
            step,
            grid=(PIECES * wins,),
            in_specs=[pl.BlockSpec((SC_WINDOW, d),
                                   index_map=lambda i: ((i // wins) * n_parts * wins
                                                        + part * wins + i % wins, 0)),
                      pl.BlockSpec((TOP_K, SC_WINDOW), index_map=lambda i: (i // wins, i % wins))],
            out_specs=[],
            core_axis_name=("core", "subcore"),
            dimension_semantics=(pltpu.PARALLEL,),
        )(src_hbm, idx_hbm)

    return pl.kernel(body, out_type=jax.ShapeDtypeStruct((n_out, d), src.dtype),
                     mesh=_sc_mesh(), scratch_types=[], name="sc_scatter_rows")(src, idx)


def _sc_gather_rows(table, idx):
    d = table.shape[1]
    n_idx_rows, n_tok = idx.shape
    wins = n_tok // SC_WINDOW

    def body(table_hbm, idx_hbm, out_hbm):
        def step(idx_vmem, out_vmem):
            pltpu.sync_copy(table_hbm.at[idx_vmem.at[0]], out_vmem)

        pltpu.emit_pipeline(
            step,
            grid=(n_idx_rows * wins,),
            in_specs=[pl.BlockSpec((1, SC_WINDOW), index_map=lambda i: (i // wins, i % wins))],
            out_specs=[pl.BlockSpec((SC_WINDOW, d), index_map=lambda i: (i, 0))],
            core_axis_name=("core", "subcore"),
            dimension_semantics=(pltpu.PARALLEL,),
        )(idx_hbm, out_hbm)

    return pl.kernel(body, out_type=jax.ShapeDtypeStruct((n_idx_rows * n_tok, d), table.dtype),
                     mesh=_sc_mesh(), scratch_types=[], name="sc_gather_rows")(table, idx)


def _index_kernel(start_ref, ek_ref, rk_ref, idx_ref, *, n_rows):
    e = ek_ref[...]
    start = jnp.zeros(e.shape, jnp.int32)
    for ex in range(N_EXPERTS):
        start = jnp.where(e == ex, start_ref[ex], start)
    dest = start + rk_ref[...]
    for c in range(PIECES):
        idx_ref[c] = dest + c * n_rows


def _row_indices(padded_start, e_k, r_k, n_rows):
    n_tok = e_k.shape[1]
    ts = INDEX_TILE
    kspec = pl.BlockSpec((TOP_K, ts), lambda i, st: (0, i))
    grid_spec = pltpu.PrefetchScalarGridSpec(
        num_scalar_prefetch=1,
        grid=(n_tok // ts,),
        in_specs=[kspec, kspec],
        out_specs=pl.BlockSpec((PIECES, TOP_K, ts), lambda i, st: (0, 0, i)),
    )
    return pl.pallas_call(
        functools.partial(_index_kernel, n_rows=n_rows),
        grid_spec=grid_spec,
        out_shape=jax.ShapeDtypeStruct((PIECES, TOP_K, n_tok), jnp.int32),
        compiler_params=pltpu.CompilerParams(dimension_semantics=("arbitrary",)),
        name="moe_row_indices",
    )(padded_start, e_k, r_k)


class _RowBands:
    def __init__(self, valid, make_copy):
        self.bands = [(q * EXPERT_TAIL < valid, make_copy(q))
                      for q in range(EXPERT_ROWS // EXPERT_TAIL)]

    def start(self):
        for live, copy in self.bands:
            pl.when(live)(copy.start)

    def wait(self):
        for live, copy in self.bands:
            pl.when(live)(copy.wait)


def _expert_kernel(chunk0_ref, nchunk_ref, valid_ref, total_ref,
                   xs_hbm, wg_ref, wu_ref, wd_ref, o_hbm,
                   xbuf, obuf, wgu_bf, wd_bf, sem_in, sem_out):
    e = pl.program_id(0)
    total = total_ref[0]
    ch = EXPERT_ROWS
    ahead = IN_SLOTS - 1

    def in_copy(g):
        slot = g % IN_SLOTS

        def band(q):
            r0 = q * EXPERT_TAIL
            return pltpu.make_async_copy(xs_hbm.at[:, pl.ds(g * ch + r0, EXPERT_TAIL), :],
                                         xbuf.at[slot, :, pl.ds(r0, EXPERT_TAIL), :],
                                         sem_in.at[slot])
        return _RowBands(valid_ref[g], band)

    def out_copy(g):
        slot = g % OUT_SLOTS

        def band(q):
            r0 = q * EXPERT_TAIL
            return pltpu.make_async_copy(obuf.at[slot, :, pl.ds(r0, EXPERT_TAIL), :],
                                         o_hbm.at[:, pl.ds(g * ch + r0, EXPERT_TAIL), :],
                                         sem_out.at[slot])
        return _RowBands(valid_ref[g], band)

    @pl.when(e == 0)
    def _():
        for g in range(ahead):
            @pl.when(g < total)
            def _():
                in_copy(g).start()

    wgu_bf[:, :EXPERT_DIM] = wg_ref[0, 0].astype(_BF16)
    wgu_bf[:, EXPERT_DIM:] = wu_ref[0, 0].astype(_BF16)
    wd_bf[...] = wd_ref[0, 0].astype(_BF16)

    def chunk(j, carry):
        g = chunk0_ref[e] + j
        in_copy(g).wait()

        @pl.when(g + ahead < total)
        def _():
            in_copy(g + ahead).start()

        @pl.when(g >= OUT_SLOTS)
        def _():
            out_copy(g - OUT_SLOTS).wait()

        islot = g % IN_SLOTS
        oslot = g % OUT_SLOTS
        valid = valid_ref[g]

        def sub_block(r0, rows):
            w = jnp.concatenate([xbuf[islot, c, r0:r0 + rows, :] for c in range(PIECES)], axis=1)
            lo, hi = _unpack_halves(w)
            gu = (_dot(lo.astype(_BF16), wgu_bf[:PACKED]) +
                  _dot(hi.astype(_BF16), wgu_bf[PACKED:]))
            h = _silu(gu[:, :EXPERT_DIM]) * gu[:, EXPERT_DIM:]
            y = _pack_halves(_dot(h.astype(_BF16), wd_bf[...]))
            for c in range(PIECES):
                obuf[oslot, c, r0:r0 + rows, :] = y[:, c * LANES:(c + 1) * LANES]

        @pl.when(valid >= ch)
        def _():
            for s in range(ch // EXPERT_SUB):
                sub_block(s * EXPERT_SUB, EXPERT_SUB)

        @pl.when(valid < ch)
        def _():
            for q in range(ch // EXPERT_TAIL):
                @pl.when(q * EXPERT_TAIL < valid)
                def _():
                    sub_block(q * EXPERT_TAIL, EXPERT_TAIL)

        out_copy(g).start()
        return carry

    lax.fori_loop(0, nchunk_ref[e], chunk, 0)

    @pl.when(e == pl.num_programs(0) - 1)
    def _():
        for back in range(OUT_SLOTS, 0, -1):
            @pl.when(total >= back)
            def _():
                out_copy(total - back).wait()


def _experts(layer, chunk0, nchunk, chunk_valid, total, xs, w_gate, w_up, w_down):
    d = D_MODEL
    ch = EXPERT_ROWS
    wspec_in = pl.BlockSpec((1, 1, d, EXPERT_DIM), lambda e, *_: (layer, e, 0, 0))
    any_spec = pl.BlockSpec(memory_space=pl.ANY)
    grid_spec = pltpu.PrefetchScalarGridSpec(
        num_scalar_prefetch=4,
        grid=(N_EXPERTS,),
        in_specs=[
            any_spec,
            wspec_in,
            wspec_in,
            pl.BlockSpec((1, 1, EXPERT_DIM, d), lambda e, *_: (layer, e, 0, 0)),
        ],
        out_specs=any_spec,
        scratch_shapes=[pltpu.VMEM((IN_SLOTS, PIECES, ch, LANES), _U32),
                        pltpu.VMEM((OUT_SLOTS, PIECES, ch, LANES), _U32),
                        pltpu.VMEM((d, 2 * EXPERT_DIM), _BF16),
                        pltpu.VMEM((EXPERT_DIM, d), _BF16),
                        pltpu.SemaphoreType.DMA((IN_SLOTS,)),
                        pltpu.SemaphoreType.DMA((OUT_SLOTS,))],
    )
    return pl.pallas_call(
        _expert_kernel,
        grid_spec=grid_spec,
        out_shape=jax.ShapeDtypeStruct(xs.shape, _U32),
        compiler_params=pltpu.CompilerParams(
            dimension_semantics=("arbitrary",),
            vmem_limit_bytes=VMEM_LIMIT),
        name="moe_experts",
    )(chunk0, nchunk, chunk_valid, total, xs, w_gate, w_up, w_down)


def _combine_kernel(x_ref, p_ref, gate_ref, sg_ref, su_ref, sd_ref, g_ref, b_ref, *rest):
    o_ref = rest[-1]
    x = x_ref[...]
    xb = x.astype(_BF16)
    gates = gate_ref[...].T
    r_lo = r_hi = None
    for k in range(TOP_K):
        w = jnp.concatenate([p_ref[c, k] for c in range(PIECES)], axis=1)
        lo, hi = _unpack_halves(w)
        gk = gates[:, k:k + 1]
        r_lo = gk * lo if r_lo is None else r_lo + gk * lo
        r_hi = gk * hi if r_hi is None else r_hi + gk * hi
    routed = jnp.concatenate([r_lo, r_hi], axis=1)
    h = _silu(_dot(xb, sg_ref[...])) * _dot(xb, su_ref[...])
    shared = _dot(h.astype(_BF16), sd_ref[...])
    o_ref[...] = _layer_norm(DEEPNORM_ALPHA * x + (routed + shared), g_ref[...], b_ref[...])


def _combine(x, picked, gates, sh_gate, sh_up, sh_down, ln_g, ln_b, part, n_parts, prev):
    n_tok, d = x.shape
    ts = COMBINE_TILE
    steps = n_tok // n_parts // ts
    const2 = lambda i: (0, 0)
    rows_map = lambda i: (part * steps + i, 0)
    in_specs = [
        pl.BlockSpec((ts, d), rows_map),
        pl.BlockSpec((PIECES, TOP_K, ts, LANES), lambda i: (0, 0, i, 0)),
        pl.BlockSpec((TOP_K, ts), lambda i: (0, i)),
        pl.BlockSpec((d, EXPERT_DIM), const2),
        pl.BlockSpec((d, EXPERT_DIM), const2),
        pl.BlockSpec((EXPERT_DIM, d), const2),
        pl.BlockSpec((1, d), const2),
        pl.BlockSpec((1, d), const2),
    ]
    args = [x, picked, gates, sh_gate.astype(_BF16), sh_up.astype(_BF16), sh_down.astype(_BF16),
            ln_g.reshape(1, d), ln_b.reshape(1, d)]
    aliases = {}
    if prev is not None:
        in_specs.append(pl.BlockSpec(memory_space=pl.ANY))
        aliases = {len(args): 0}
        args.append(prev)
    return pl.pallas_call(
        _combine_kernel,
        grid=(steps,),
        in_specs=in_specs,
        out_specs=pl.BlockSpec((ts, d), rows_map),
        out_shape=jax.ShapeDtypeStruct((n_tok, d), _F32),
        input_output_aliases=aliases,
        compiler_params=pltpu.CompilerParams(
            dimension_semantics=("arbitrary",),
            vmem_limit_bytes=VMEM_LIMIT),
        name="moe_combine",
    )(*args)


def _moe_layer(layer, x, x_packed, w_router, router_bias, w_gate, w_up, w_down,
               sh_gate, sh_up, sh_down, ln_g, ln_b):
    n_tok, d = x.shape
    bm = EXPERT_ROWS
    n_part = n_tok // TOKEN_PARTS
    n_rows = n_part * TOP_K + N_EXPERTS * bm
    x_flat = x_packed.reshape(PIECES * n_tok, LANES)

    routed = []
    for part in range(TOKEN_PARTS):
        e_k, r_k, g_k, counts = _route(x, w_router, router_bias, part, TOKEN_PARTS)
        counts = counts.reshape(N_EXPERTS).astype(jnp.int32)
        padded = (counts + bm - 1) // bm * bm
        padded_end = jnp.cumsum(padded)
        padded_start = padded_end - padded
        idx = _row_indices(padded_start, e_k, r_k, n_rows).reshape(PIECES * TOP_K, n_part)
        chunk_row = (jnp.arange(n_rows // bm, dtype=jnp.int32) * bm)[:, None]
        in_expert = (chunk_row >= padded_start[None, :]) & (chunk_row < padded_end[None, :])
        left = jnp.clip(counts[None, :] - (chunk_row - padded_start[None, :]), 0, bm)
        chunk_valid = jnp.sum(jnp.where(in_expert, left, 0), axis=1).astype(jnp.int32)
        routed.append((idx, g_k, padded_start // bm, padded // bm, chunk_valid,
                       padded_end[-1:] // bm))

    rows = []
    for part, (idx, _, chunk0, nchunk, chunk_valid, total) in enumerate(routed):
        xs = _sc_scatter_rows(x_flat, idx, PIECES * n_rows, part, TOKEN_PARTS)
        rows.append(_experts(layer, chunk0, nchunk, chunk_valid, total,
                             xs.reshape(PIECES, n_rows, LANES), w_gate, w_up, w_down))

    out = None
    for part, (idx, gates, *_) in enumerate(routed):
        picked = _sc_gather_rows(rows[part].reshape(PIECES * n_rows, LANES), idx)
        picked = picked.reshape(PIECES, TOP_K, n_part, LANES)
        out = _combine(x, picked, gates, sh_gate, sh_up, sh_down, ln_g, ln_b,
                       part, TOKEN_PARTS, out)
    return out


def kernel(x, pool_w_in, pool_w_grp, pool_scale, pool_w_out, sgu_w_in, sgu_b_in, sgu_ln_g, sgu_ln_b, sgu_w_s, sgu_b_s, sgu_w_out, ln_mix_g, ln_mix_b, moe_w_router, moe_router_bias, moe_w_gate, moe_w_up, moe_w_down, moe_sh_gate, moe_sh_up, moe_sh_down, ln_ffn_g, ln_ffn_b):
    bsz, seq, d = x.shape
    n_tok = bsz * seq

    def moe(i, h, h_packed):
        return _moe_layer(i, h, h_packed, moe_w_router[i], moe_router_bias[i], moe_w_gate,
                          moe_w_up, moe_w_down, moe_sh_gate[i], moe_sh_up[i],
                          moe_sh_down[i], ln_ffn_g[i], ln_ffn_b[i])

    h, h_packed = _pool_layer(x, pool_w_in[0], pool_w_grp[0], pool_scale[0], pool_w_out[0],
                              ln_mix_g[0], ln_mix_b[0])
    h = moe(0, h.reshape(n_tok, d), h_packed)
    h, h_packed = _sgu_layer(h, sgu_w_in[0], sgu_b_in[0], sgu_ln_g[0], sgu_ln_b[0], sgu_w_s[0],
                             sgu_b_s[0], sgu_w_out[0], ln_mix_g[1], ln_mix_b[1])
    h = moe(1, h, h_packed)
    return h.reshape(bsz, seq, d)
```

```python
import functools

import jax
import jax.numpy as jnp
from jax import lax
from jax.experimental import pallas as pl
from jax.experimental.pallas import tpu as pltpu
from jax.experimental.pallas import tpu_sc as plsc

D_MODEL = 1024
DEPTH = 2
POOL_WINDOWS = (2, 4, 8, 16)
POOL_GROUP_DIM = D_MODEL // len(POOL_WINDOWS)
POOL_HALO = 16
SGU_CHUNK = 128
SGU_HEADS = 4
SGU_WIDTH = 2 * D_MODEL
SGU_HEAD_DIM = SGU_WIDTH // SGU_HEADS
N_EXPERTS = 64
TOP_K = 8
N_GROUPS = 8
GROUP_SIZE = N_EXPERTS // N_GROUPS
TOPK_GROUPS = 4
EXPERT_DIM = D_MODEL // 4
ROUTED_SCALE = 2.5
DEEPNORM_ALPHA = (2 * DEPTH) ** 0.25
LN_EPS = 1e-5

LANES = 128
PACKED = D_MODEL // 2
PIECES = PACKED // LANES

POOL_TILE = 512
SGU_TILE = 256
ROUTE_TILE = 512
INDEX_TILE = 2048
EXPERT_ROWS = 512
EXPERT_SUB = 256
IN_SLOTS = 6
OUT_SLOTS = 4
COMBINE_TILE = 512
SC_WINDOW = 128
TOKEN_PARTS = 2
VMEM_LIMIT = 56 * 1024 * 1024

_F32 = jnp.float32
_BF16 = jnp.bfloat16
_U32 = jnp.uint32


def _dot(a, b):
    return jnp.dot(a, b, preferred_element_type=_F32)


def _layer_norm(h, g, b):
    mu = jnp.mean(h, axis=-1, keepdims=True)
    hc = h - mu
    var = jnp.mean(hc * hc, axis=-1, keepdims=True)
    return hc * lax.rsqrt(var + LN_EPS) * g + b


def _silu(x):
    return x * jax.nn.sigmoid(x)


def _gelu_tanh(x):
    c = 0.7978845608028654
    return 0.5 * x * (1.0 + jnp.tanh(c * (x + 0.044715 * (x * x * x))))


def _pack_halves(v):
    half = v.shape[1] // 2
    lo = lax.bitcast_convert_type(v[:, :half].astype(_BF16).astype(_F32), _U32)
    hi = lax.bitcast_convert_type(v[:, half:].astype(_BF16).astype(_F32), _U32)
    return (hi & _U32(0xFFFF0000)) | (lo >> 16)


def _unpack_halves(w):
    lo = lax.bitcast_convert_type(w << 16, _F32)
    hi = lax.bitcast_convert_type(w & _U32(0xFFFF0000), _F32)
    return lo, hi


def _store_pieces(ref, w):
    for c in range(PIECES):
        ref[c] = w[:, c * LANES:(c + 1) * LANES]


def _load_pieces(ref):
    return jnp.concatenate([ref[c] for c in range(PIECES)], axis=1)


def _pool_kernel(x_ref, win_ref, wgrp_ref, scale_ref, wout_ref, g_ref, b_ref,
                 o_ref, op_ref, zs_ref, y_ref):
    s = pl.program_id(1)
    ts = x_ref.shape[1]
    x = x_ref[0]
    z = _dot(x.astype(_BF16), win_ref[...])

    @pl.when(s == 0)
    def _():
        zs_ref[0:POOL_HALO, :] = jnp.zeros((POOL_HALO, D_MODEL), _F32)

    zs_ref[POOL_HALO:POOL_HALO + ts, :] = z
    pos = s * ts + lax.broadcasted_iota(jnp.int32, (ts, 1), 0)
    for g, w in enumerate(POOL_WINDOWS):
        c0 = g * POOL_GROUP_DIM
        c1 = c0 + POOL_GROUP_DIM
        zg = zs_ref[POOL_HALO:POOL_HALO + ts, c0:c1]
        acc = zg
        for k in range(1, w):
            acc = acc + zs_ref[POOL_HALO - k:POOL_HALO - k + ts, c0:c1]
        cnt = jnp.minimum(pos + 1, w).astype(_F32)
        pooled = acc / cnt - zg
        yg = _dot(pooled.astype(_BF16), wgrp_ref[g]) * scale_ref[:, c0:c1]
        y_ref[:, c0:c1] = yg.astype(_BF16)
    zs_ref[0:POOL_HALO, :] = zs_ref[ts:ts + POOL_HALO, :]
    mix = _dot(y_ref[...], wout_ref[...])
    out = _layer_norm(DEEPNORM_ALPHA * x + mix, g_ref[...], b_ref[...])
    o_ref[0] = out
    _store_pieces(op_ref, _pack_halves(out))


def _pool_layer(x, w_in, w_grp, scale, w_out, ln_g, ln_b):
    bsz, seq, d = x.shape
    ts = POOL_TILE
    steps = seq // ts
    const2 = lambda b, s: (0, 0)
    out_shape = (jax.ShapeDtypeStruct((bsz, seq, d), _F32),
                 jax.ShapeDtypeStruct((PIECES, bsz * seq, LANES), _U32))
    tile = pl.BlockSpec((1, ts, d), lambda b, s: (b, s, 0))
    ptile = pl.BlockSpec((PIECES, ts, LANES), lambda b, s: (0, b * steps + s, 0))
    return pl.pallas_call(
        _pool_kernel,
        grid=(bsz, steps),
        in_specs=[
            tile,
            pl.BlockSpec((d, d), const2),
            pl.BlockSpec((len(POOL_WINDOWS), POOL_GROUP_DIM, POOL_GROUP_DIM), lambda b, s: (0, 0, 0)),
            pl.BlockSpec((1, d), const2),
            pl.BlockSpec((d, d), const2),
            pl.BlockSpec((1, d), const2),
            pl.BlockSpec((1, d), const2),
        ],
        out_specs=(tile, ptile),
        out_shape=out_shape,
        scratch_shapes=[pltpu.VMEM((POOL_HALO + ts, d), _F32),
                        pltpu.VMEM((ts, d), _BF16)],
        compiler_params=pltpu.CompilerParams(
            dimension_semantics=("arbitrary", "arbitrary"),
            vmem_limit_bytes=VMEM_LIMIT),
        name="pool_mixer",
    )(x, w_in.astype(_BF16), w_grp.astype(_BF16), scale.reshape(1, d),
      w_out.astype(_BF16), ln_g.reshape(1, d), ln_b.reshape(1, d))


def _sgu_kernel(x_ref, win_ref, bin_ref, lng_ref, lnb_ref, ws_ref, bs_ref, wout_ref,
                g_ref, b_ref, o_ref, op_ref, gated_ref):
    ts = x_ref.shape[0]
    x = x_ref[...]
    xb = x.astype(_BF16)
    v = _gelu_tanh(_dot(xb, win_ref[:, SGU_WIDTH:]) + bin_ref[:, SGU_WIDTH:])
    v = _layer_norm(v, lng_ref[...], lnb_ref[...]).astype(_BF16)
    u = _gelu_tanh(_dot(xb, win_ref[:, :SGU_WIDTH]) + bin_ref[:, :SGU_WIDTH])
    for c in range(ts // SGU_CHUNK):
        r0 = c * SGU_CHUNK
        for h in range(SGU_HEADS):
            c0 = h * SGU_HEAD_DIM
            sv = _dot(ws_ref[h], v[r0:r0 + SGU_CHUNK, c0:c0 + SGU_HEAD_DIM]) + bs_ref[h]
            gated_ref[r0:r0 + SGU_CHUNK, c0:c0 + SGU_HEAD_DIM] = (
                u[r0:r0 + SGU_CHUNK, c0:c0 + SGU_HEAD_DIM] * sv).astype(_BF16)
    mix = _dot(gated_ref[...], wout_ref[...])
    out = _layer_norm(DEEPNORM_ALPHA * x + mix, g_ref[...], b_ref[...])
    o_ref[...] = out
    _store_pieces(op_ref, _pack_halves(out))


def _sgu_layer(x, w_in, b_in, ln_g, ln_b, w_s, b_s, w_out, mix_g, mix_b):
    n_tok, d = x.shape
    ts = SGU_TILE
    const2 = lambda i: (0, 0)
    const3 = lambda i: (0, 0, 0)
    causal = jnp.tril(jnp.ones((SGU_CHUNK, SGU_CHUNK), w_s.dtype))
    ws = (w_s * causal[None]).astype(_BF16)
    tile = pl.BlockSpec((ts, d), lambda i: (i, 0))
    ptile = pl.BlockSpec((PIECES, ts, LANES), lambda i: (0, i, 0))
    out_shape = (jax.ShapeDtypeStruct((n_tok, d), _F32),
                 jax.ShapeDtypeStruct((PIECES, n_tok, LANES), _U32))
    return pl.pallas_call(
        _sgu_kernel,
        grid=(n_tok // ts,),
        in_specs=[
            tile,
            pl.BlockSpec((d, 2 * SGU_WIDTH), const2),
            pl.BlockSpec((1, 2 * SGU_WIDTH), const2),
            pl.BlockSpec((1, SGU_WIDTH), const2),
            pl.BlockSpec((1, SGU_WIDTH), const2),
            pl.BlockSpec((SGU_HEADS, SGU_CHUNK, SGU_CHUNK), const3),
            pl.BlockSpec((SGU_HEADS, SGU_CHUNK, 1), const3),
            pl.BlockSpec((SGU_WIDTH, d), const2),
            pl.BlockSpec((1, d), const2),
            pl.BlockSpec((1, d), const2),
        ],
        out_specs=(tile, ptile),
        out_shape=out_shape,
        scratch_shapes=[pltpu.VMEM((ts, SGU_WIDTH), _BF16)],
        compiler_params=pltpu.CompilerParams(
            dimension_semantics=("arbitrary",),
            vmem_limit_bytes=VMEM_LIMIT),
        name="sgu_mixer",
    )(x, w_in.astype(_BF16), b_in.reshape(1, -1), ln_g.reshape(1, -1), ln_b.reshape(1, -1),
      ws, b_s.reshape(SGU_HEADS, SGU_CHUNK, 1), w_out.astype(_BF16),
      mix_g.reshape(1, d), mix_b.reshape(1, d))


def _all_max(a):
    return jnp.max(jnp.max(a, axis=0, keepdims=True), axis=1, keepdims=True)


def _all_min(a):
    return jnp.min(jnp.min(a, axis=0, keepdims=True), axis=1, keepdims=True)


def _all_sum(a):
    return jnp.sum(jnp.sum(a, axis=0, keepdims=True), axis=1, keepdims=True)


def _route_kernel(x_ref, whi_ref, wlo_ref, bias_ref, earlier_ref,
                  ek_ref, rk_ref, gk_ref, cnt_ref, carry_ref):
    i = pl.program_id(0)
    ts = x_ref.shape[0]

    @pl.when(i == 0)
    def _():
        carry_ref[...] = jnp.zeros_like(carry_ref)

    x = x_ref[...]
    x_hi = x.astype(_BF16)
    x_lo = (x - x_hi.astype(_F32)).astype(_BF16)
    nt = lambda a, b: lax.dot_general(a, b, (((1,), (1,)), ((), ())),
                                      preferred_element_type=_F32)
    logits = nt(whi_ref[...], x_hi) + (nt(whi_ref[...], x_lo) + nt(wlo_ref[...], x_hi))
    scores = jax.nn.sigmoid(logits).reshape(N_GROUPS, GROUP_SIZE, ts)
    biased = scores + bias_ref[...]
    neg_inf = jnp.float32(-jnp.inf)
    shape3 = (N_GROUPS, GROUP_SIZE, ts)
    in_grp = lax.broadcasted_iota(jnp.int32, shape3, 1)
    grp = lax.broadcasted_iota(jnp.int32, shape3, 0)
    eid = grp * GROUP_SIZE + in_grp

    m1 = jnp.max(biased, axis=1, keepdims=True)
    first1 = jnp.min(jnp.where(biased == m1, in_grp, GROUP_SIZE), axis=1, keepdims=True)
    m2 = jnp.max(jnp.where(in_grp == first1, neg_inf, biased), axis=1, keepdims=True)
    gscore = m1 + m2

    gid = lax.broadcasted_iota(jnp.int32, (N_GROUPS, 1, ts), 0)
    gsel = jnp.zeros((N_GROUPS, 1, ts), jnp.bool_)
    for _ in range(TOPK_GROUPS):
        m = jnp.max(gscore, axis=0, keepdims=True)
        first = jnp.min(jnp.where(gscore == m, gid, N_GROUPS), axis=0, keepdims=True)
        pick = gid == first
        gsel = jnp.logical_or(gsel, pick)
        gscore = jnp.where(pick, neg_inf, gscore)

    masked = jnp.where(gsel, biased, neg_inf)
    picked_any = jnp.zeros(shape3, jnp.bool_)
    e_k, s_k = [], []
    for _ in range(TOP_K):
        m = _all_max(masked)
        first = _all_min(jnp.where(masked == m, eid, N_EXPERTS))
        pick = eid == first
        picked_any = jnp.logical_or(picked_any, pick)
        masked = jnp.where(pick, neg_inf, masked)
        e_k.append(first)
        s_k.append(_all_sum(jnp.where(pick, scores, 0.0)))
    denom = s_k[0]
    for k in range(1, TOP_K):
        denom = denom + s_k[k]

    sel = picked_any.astype(_F32).reshape(N_EXPERTS, ts)
    rank = _dot(sel.astype(_BF16), earlier_ref[...]) + carry_ref[...]
    rank3 = rank.reshape(shape3)
    carry_ref[...] += jnp.sum(sel, axis=1, keepdims=True)
    cnt_ref[...] = carry_ref[...]

    for k in range(TOP_K):
        r = _all_sum(jnp.where(eid == e_k[k], rank3, 0.0))
        ek_ref[k:k + 1, :] = e_k[k].reshape(1, ts)
        rk_ref[k:k + 1, :] = r.reshape(1, ts).astype(jnp.int32)
        gk_ref[k:k + 1, :] = (s_k[k] / denom * ROUTED_SCALE).reshape(1, ts)


def _route(x, w_router, router_bias, part, n_parts):
    n_tok, d = x.shape
    ts = ROUTE_TILE
    n_part = n_tok // n_parts
    steps = n_part // ts
    out_shape = (jax.ShapeDtypeStruct((TOP_K, n_part), jnp.int32),
                 jax.ShapeDtypeStruct((TOP_K, n_part), jnp.int32),
                 jax.ShapeDtypeStruct((TOP_K, n_part), _F32),
                 jax.ShapeDtypeStruct((N_EXPERTS, 1), _F32))
    kspec = pl.BlockSpec((TOP_K, ts), lambda i: (0, i))
    const2 = lambda i: (0, 0)
    w_t = w_router.T
    w_hi = w_t.astype(_BF16)
    w_lo = (w_t - w_hi.astype(_F32)).astype(_BF16)
    pos = jnp.arange(ts, dtype=jnp.int32)
    earlier = (pos[:, None] < pos[None, :]).astype(_BF16)
    return pl.pallas_call(
        _route_kernel,
        grid=(steps,),
        in_specs=[
            pl.BlockSpec((ts, d), lambda i: (part * steps + i, 0)),
            pl.BlockSpec((N_EXPERTS, d), const2),
            pl.BlockSpec((N_EXPERTS, d), const2),
            pl.BlockSpec((N_GROUPS, GROUP_SIZE, 1), lambda i: (0, 0, 0)),
            pl.BlockSpec((ts, ts), const2),
        ],
        out_specs=(kspec, kspec, kspec, pl.BlockSpec((N_EXPERTS, 1), const2)),
        out_shape=out_shape,
        scratch_shapes=[pltpu.VMEM((N_EXPERTS, 1), _F32)],
        compiler_params=pltpu.CompilerParams(
            dimension_semantics=("arbitrary",),
            vmem_limit_bytes=VMEM_LIMIT),
        name="moe_route",
    )(x, w_hi, w_lo, router_bias.reshape(N_GROUPS, GROUP_SIZE, 1), earlier)


def _sc_mesh():
    return plsc.VectorSubcoreMesh(core_axis_name="core", subcore_axis_name="subcore")


def _sc_scatter_rows(src, idx, n_out, part, n_parts):
    d = src.shape[1]
    n_part = idx.shape[1]
    wins = n_part // SC_WINDOW

    def body(src_hbm, idx_hbm, out_hbm):
        def step(src_vmem, idx_vmem):
            for k in range(TOP_K):
                pltpu.sync_copy(src_vmem, out_hbm.at[idx_vmem.at[k]])

        pltpu.emit_pipeline(
            step,
            grid=(PIECES * wins,),
            in_specs=[pl.BlockSpec((SC_WINDOW, d),
                                   index_map=lambda i: ((i // wins) * n_parts * wins
                                                        + part * wins + i % wins, 0)),
                      pl.BlockSpec((TOP_K, SC_WINDOW), index_map=lambda i: (i // wins, i % wins))],
            out_specs=[],
            core_axis_name=("core", "subcore"),
            dimension_semantics=(pltpu.PARALLEL,),
        )(src_hbm, idx_hbm)

    return pl.kernel(body, out_type=jax.ShapeDtypeStruct((n_out, d), src.dtype),
                     mesh=_sc_mesh(), scratch_types=[], name="sc_scatter_rows")(src, idx)


def _sc_gather_rows(table, idx):
    d = table.shape[1]
    n_idx_rows, n_tok = idx.shape
    wins = n_tok // SC_WINDOW

    def body(table_hbm, idx_hbm, out_hbm):
        def step(idx_vmem, out_vmem):
            pltpu.sync_copy(table_hbm.at[idx_vmem.at[0]], out_vmem)

        pltpu.emit_pipeline(
            step,
            grid=(n_idx_rows * wins,),
            in_specs=[pl.BlockSpec((1, SC_WINDOW), index_map=lambda i: (i // wins, i % wins))],
            out_specs=[pl.BlockSpec((SC_WINDOW, d), index_map=lambda i: (i, 0))],
            core_axis_name=("core", "subcore"),
            dimension_semantics=(pltpu.PARALLEL,),
        )(idx_hbm, out_hbm)

    return pl.kernel(body, out_type=jax.ShapeDtypeStruct((n_idx_rows * n_tok, d), table.dtype),
                     mesh=_sc_mesh(), scratch_types=[], name="sc_gather_rows")(table, idx)


def _index_kernel(start_ref, ek_ref, rk_ref, idx_ref, *, n_rows):
    e = ek_ref[...]
    start = jnp.zeros(e.shape, jnp.int32)
    for ex in range(N_EXPERTS):
        start = jnp.where(e == ex, start_ref[ex], start)
    dest = start + rk_ref[...]
    for c in range(PIECES):
        idx_ref[c] = dest + c * n_rows


def _row_indices(padded_start, e_k, r_k, n_rows):
    n_tok = e_k.shape[1]
    ts = INDEX_TILE
    kspec = pl.BlockSpec((TOP_K, ts), lambda i, st: (0, i))
    grid_spec = pltpu.PrefetchScalarGridSpec(
        num_scalar_prefetch=1,
        grid=(n_tok // ts,),
        in_specs=[kspec, kspec],
        out_specs=pl.BlockSpec((PIECES, TOP_K, ts), lambda i, st: (0, 0, i)),
    )
    return pl.pallas_call(
        functools.partial(_index_kernel, n_rows=n_rows),
        grid_spec=grid_spec,
        out_shape=jax.ShapeDtypeStruct((PIECES, TOP_K, n_tok), jnp.int32),
        compiler_params=pltpu.CompilerParams(dimension_semantics=("arbitrary",)),
        name="moe_row_indices",
    )(padded_start, e_k, r_k)


def _expert_kernel(chunk0_ref, nchunk_ref, cnt_ref, total_ref,
                   xs_hbm, wg_ref, wu_ref, wd_ref, o_hbm,
                   xbuf, obuf, wgu_bf, wd_bf, sem_in, sem_out):
    e = pl.program_id(0)
    total = total_ref[0]
    ch = EXPERT_ROWS
    ahead = IN_SLOTS - 1

    def in_copy(g):
        slot = g % IN_SLOTS
        return pltpu.make_async_copy(xs_hbm.at[:, pl.ds(g * ch, ch), :], xbuf.at[slot],
                                     sem_in.at[slot])

    def out_copy(g):
        slot = g % OUT_SLOTS
        return pltpu.make_async_copy(obuf.at[slot], o_hbm.at[:, pl.ds(g * ch, ch), :],
                                     sem_out.at[slot])

    @pl.when(e == 0)
    def _():
        obuf[...] = jnp.zeros_like(obuf)
        for g in range(ahead):
            @pl.when(g < total)
            def _():
                in_copy(g).start()

    wgu_bf[:, :EXPERT_DIM] = wg_ref[0, 0].astype(_BF16)
    wgu_bf[:, EXPERT_DIM:] = wu_ref[0, 0].astype(_BF16)
    wd_bf[...] = wd_ref[0, 0].astype(_BF16)

    def chunk(j, carry):
        g = chunk0_ref[e] + j
        in_copy(g).wait()

        @pl.when(g + ahead < total)
        def _():
            in_copy(g + ahead).start()

        @pl.when(g >= OUT_SLOTS)
        def _():
            out_copy(g - OUT_SLOTS).wait()

        islot = g % IN_SLOTS
        oslot = g % OUT_SLOTS
        valid = cnt_ref[e] - j * ch

        def sub_block(s):
            r0 = s * EXPERT_SUB
            w = jnp.concatenate([xbuf[islot, c, r0:r0 + EXPERT_SUB, :] for c in range(PIECES)],
                                axis=1)
            lo, hi = _unpack_halves(w)
            gu = (_dot(lo.astype(_BF16), wgu_bf[:PACKED]) +
                  _dot(hi.astype(_BF16), wgu_bf[PACKED:]))
            h = _silu(gu[:, :EXPERT_DIM]) * gu[:, EXPERT_DIM:]
            y = _pack_halves(_dot(h.astype(_BF16), wd_bf[...]))
            for c in range(PIECES):
                obuf[oslot, c, r0:r0 + EXPERT_SUB, :] = y[:, c * LANES:(c + 1) * LANES]

        n_sub = ch // EXPERT_SUB

        @pl.when(valid >= ch)
        def _():
            for s in range(n_sub):
                sub_block(s)

        @pl.when(valid < ch)
        def _():
            for s in range(n_sub):
                @pl.when(s * EXPERT_SUB < valid)
                def _():
                    sub_block(s)

        out_copy(g).start()
        return carry

    lax.fori_loop(0, nchunk_ref[e], chunk, 0)

    @pl.when(e == pl.num_programs(0) - 1)
    def _():
        for back in range(OUT_SLOTS, 0, -1):
            @pl.when(total >= back)
            def _():
                out_copy(total - back).wait()


def _experts(layer, chunk0, nchunk, counts, total, xs, w_gate, w_up, w_down):
    d = D_MODEL
    ch = EXPERT_ROWS
    wspec_in = pl.BlockSpec((1, 1, d, EXPERT_DIM), lambda e, *_: (layer, e, 0, 0))
    any_spec = pl.BlockSpec(memory_space=pl.ANY)
    grid_spec = pltpu.PrefetchScalarGridSpec(
        num_scalar_prefetch=4,
        grid=(N_EXPERTS,),
        in_specs=[
            any_spec,
            wspec_in,
            wspec_in,
            pl.BlockSpec((1, 1, EXPERT_DIM, d), lambda e, *_: (layer, e, 0, 0)),
        ],
        out_specs=any_spec,
        scratch_shapes=[pltpu.VMEM((IN_SLOTS, PIECES, ch, LANES), _U32),
                        pltpu.VMEM((OUT_SLOTS, PIECES, ch, LANES), _U32),
                        pltpu.VMEM((d, 2 * EXPERT_DIM), _BF16),
                        pltpu.VMEM((EXPERT_DIM, d), _BF16),
                        pltpu.SemaphoreType.DMA((IN_SLOTS,)),
                        pltpu.SemaphoreType.DMA((OUT_SLOTS,))],
    )
    return pl.pallas_call(
        _expert_kernel,
        grid_spec=grid_spec,
        out_shape=jax.ShapeDtypeStruct(xs.shape, _U32),
        compiler_params=pltpu.CompilerParams(
            dimension_semantics=("arbitrary",),
            vmem_limit_bytes=VMEM_LIMIT),
        name="moe_experts",
    )(chunk0, nchunk, counts, total, xs, w_gate, w_up, w_down)


def _combine_kernel(x_ref, p_ref, gate_ref, sg_ref, su_ref, sd_ref, g_ref, b_ref, *rest):
    o_ref = rest[-1]
    x = x_ref[...]
    xb = x.astype(_BF16)
    gates = gate_ref[...].T
    r_lo = r_hi = None
    for k in range(TOP_K):
        w = jnp.concatenate([p_ref[c, k] for c in range(PIECES)], axis=1)
        lo, hi = _unpack_halves(w)
        gk = gates[:, k:k + 1]
        r_lo = gk * lo if r_lo is None else r_lo + gk * lo
        r_hi = gk * hi if r_hi is None else r_hi + gk * hi
    routed = jnp.concatenate([r_lo, r_hi], axis=1)
    h = _silu(_dot(xb, sg_ref[...])) * _dot(xb, su_ref[...])
    shared = _dot(h.astype(_BF16), sd_ref[...])
    o_ref[...] = _layer_norm(DEEPNORM_ALPHA * x + (routed + shared), g_ref[...], b_ref[...])


def _combine(x, picked, gates, sh_gate, sh_up, sh_down, ln_g, ln_b, part, n_parts, prev):
    n_tok, d = x.shape
    ts = COMBINE_TILE
    steps = n_tok // n_parts // ts
    const2 = lambda i: (0, 0)
    rows_map = lambda i: (part * steps + i, 0)
    in_specs = [
        pl.BlockSpec((ts, d), rows_map),
        pl.BlockSpec((PIECES, TOP_K, ts, LANES), lambda i: (0, 0, i, 0)),
        pl.BlockSpec((TOP_K, ts), lambda i: (0, i)),
        pl.BlockSpec((d, EXPERT_DIM), const2),
        pl.BlockSpec((d, EXPERT_DIM), const2),
        pl.BlockSpec((EXPERT_DIM, d), const2),
        pl.BlockSpec((1, d), const2),
        pl.BlockSpec((1, d), const2),
    ]
    args = [x, picked, gates, sh_gate.astype(_BF16), sh_up.astype(_BF16), sh_down.astype(_BF16),
            ln_g.reshape(1, d), ln_b.reshape(1, d)]
    aliases = {}
    if prev is not None:
        in_specs.append(pl.BlockSpec(memory_space=pl.ANY))
        aliases = {len(args): 0}
        args.append(prev)
    return pl.pallas_call(
        _combine_kernel,
        grid=(steps,),
        in_specs=in_specs,
        out_specs=pl.BlockSpec((ts, d), rows_map),
        out_shape=jax.ShapeDtypeStruct((n_tok, d), _F32),
        input_output_aliases=aliases,
        compiler_params=pltpu.CompilerParams(
            dimension_semantics=("arbitrary",),
            vmem_limit_bytes=VMEM_LIMIT),
        name="moe_combine",
    )(*args)


def _moe_layer(layer, x, x_packed, w_router, router_bias, w_gate, w_up, w_down,
               sh_gate, sh_up, sh_down, ln_g, ln_b):
    n_tok, d = x.shape
    bm = EXPERT_ROWS
    n_part = n_tok // TOKEN_PARTS
    n_rows = n_part * TOP_K + N_EXPERTS * bm
    x_flat = x_packed.reshape(PIECES * n_tok, LANES)

    routed = []
    for part in range(TOKEN_PARTS):
        e_k, r_k, g_k, counts = _route(x, w_router, router_bias, part, TOKEN_PARTS)
        counts = counts.reshape(N_EXPERTS).astype(jnp.int32)
        padded = (counts + bm - 1) // bm * bm
        padded_end = jnp.cumsum(padded)
        padded_start = padded_end - padded
        idx = _row_indices(padded_start, e_k, r_k, n_rows).reshape(PIECES * TOP_K, n_part)
        routed.append((idx, g_k, padded_start // bm, padded // bm, counts,
                       padded_end[-1:] // bm))

    rows = []
    for part, (idx, _, chunk0, nchunk, counts, total) in enumerate(routed):
        xs = _sc_scatter_rows(x_flat, idx, PIECES * n_rows, part, TOKEN_PARTS)
        rows.append(_experts(layer, chunk0, nchunk, counts, total,
                             xs.reshape(PIECES, n_rows, LANES), w_gate, w_up, w_down))

    out = None
    for part, (idx, gates, *_) in enumerate(routed):
        picked = _sc_gather_rows(rows[part].reshape(PIECES * n_rows, LANES), idx)
        picked = picked.reshape(PIECES, TOP_K, n_part, LANES)
        out = _combine(x, picked, gates, sh_gate, sh_up, sh_down, ln_g, ln_b,
                       part, TOKEN_PARTS, out)
    return out


def kernel(x, pool_w_in, pool_w_grp, pool_scale, pool_w_out, sgu_w_in, sgu_b_in, sgu_ln_g, sgu_ln_b, sgu_w_s, sgu_b_s, sgu_w_out, ln_mix_g, ln_mix_b, moe_w_router, moe_router_bias, moe_w_gate, moe_w_up, moe_w_down, moe_sh_gate, moe_sh_up, moe_sh_down, ln_ffn_g, ln_ffn_b):
    bsz, seq, d = x.shape
    n_tok = bsz * seq

    def moe(i, h, h_packed):
        return _moe_layer(i, h, h_packed, moe_w_router[i], moe_router_bias[i], moe_w_gate,
                          moe_w_up, moe_w_down, moe_sh_gate[i], moe_sh_up[i],
                          moe_sh_down[i], ln_ffn_g[i], ln_ffn_b[i])

    h, h_packed = _pool_layer(x, pool_w_in[0], pool_w_grp[0], pool_scale[0], pool_w_out[0],
                              ln_mix_g[0], ln_mix_b[0])
    h = moe(0, h.reshape(n_tok, d), h_packed)
    h, h_packed = _sgu_layer(h, sgu_w_in[0], sgu_b_in[0], sgu_ln_g[0], sgu_ln_b[0], sgu_w_s[0],
                             sgu_b_s[0], sgu_w_out[0], ln_mix_g[1], ln_mix_b[1])
    h = moe(1, h, h_packed)
    return h.reshape(bsz, seq, d)
```

```python
import functools

import jax
import jax.numpy as jnp
from jax import lax
from jax.experimental import pallas as pl
from jax.experimental.pallas import tpu as pltpu
from jax.experimental.pallas import tpu_sc as plsc

D_MODEL = 1024
DEPTH = 2
POOL_WINDOWS = (2, 4, 8, 16)
POOL_GROUP_DIM = D_MODEL // len(POOL_WINDOWS)
POOL_HALO = 16
SGU_CHUNK = 128
SGU_HEADS = 4
SGU_WIDTH = 2 * D_MODEL
SGU_HEAD_DIM = SGU_WIDTH // SGU_HEADS
N_EXPERTS = 64
TOP_K = 8
N_GROUPS = 8
GROUP_SIZE = N_EXPERTS // N_GROUPS
TOPK_GROUPS = 4
EXPERT_DIM = D_MODEL // 4
ROUTED_SCALE = 2.5
DEEPNORM_ALPHA = (2 * DEPTH) ** 0.25
LN_EPS = 1e-5

LANES = 128
PACKED = D_MODEL // 2
PIECES = PACKED // LANES

POOL_TILE = 1024
POOL_SUB = 256
SGU_TILE = 512
SGU_SUB = 256
ROUTE_TILE = 512
INDEX_TILE = 2048
EXPERT_ROWS = 512
EXPERT_SUB = 256
IN_SLOTS = 6
OUT_SLOTS = 4
COMBINE_TILE = 512
SC_WINDOW = 128
TOKEN_PARTS = 2
VMEM_LIMIT = 56 * 1024 * 1024

_F32 = jnp.float32
_BF16 = jnp.bfloat16
_U32 = jnp.uint32


def _dot(a, b):
    return jnp.dot(a, b, preferred_element_type=_F32)


def _layer_norm(h, g, b):
    mu = jnp.mean(h, axis=-1, keepdims=True)
    hc = h - mu
    var = jnp.mean(hc * hc, axis=-1, keepdims=True)
    return hc * lax.rsqrt(var + LN_EPS) * g + b


def _silu(x):
    return x * jax.nn.sigmoid(x)


def _gelu_tanh(x):
    c = 0.7978845608028654
    return 0.5 * x * (1.0 + jnp.tanh(c * (x + 0.044715 * (x * x * x))))


def _pack_halves(v):
    half = v.shape[1] // 2
    lo = lax.bitcast_convert_type(v[:, :half].astype(_BF16).astype(_F32), _U32)
    hi = lax.bitcast_convert_type(v[:, half:].astype(_BF16).astype(_F32), _U32)
    return (hi & _U32(0xFFFF0000)) | (lo >> 16)


def _unpack_halves(w):
    lo = lax.bitcast_convert_type(w << 16, _F32)
    hi = lax.bitcast_convert_type(w & _U32(0xFFFF0000), _F32)
    return lo, hi


def _store_pieces(ref, r0, w):
    for c in range(PIECES):
        ref[c, r0:r0 + w.shape[0], :] = w[:, c * LANES:(c + 1) * LANES]


def _pool_kernel(x_ref, win_ref, wgrp_ref, scale_ref, wout_ref, g_ref, b_ref,
                 o_ref, op_ref, zs_ref, y_ref):
    s = pl.program_id(1)
    ts = x_ref.shape[1]

    @pl.when(s == 0)
    def _():
        zs_ref[0:POOL_HALO, :] = jnp.zeros((POOL_HALO, D_MODEL), _F32)

    for p0 in range(0, ts, POOL_SUB):
        x = x_ref[0, p0:p0 + POOL_SUB, :]
        z = _dot(x.astype(_BF16), win_ref[...])
        base = POOL_HALO + p0
        zs_ref[base:base + POOL_SUB, :] = z
        pos = s * ts + p0 + lax.broadcasted_iota(jnp.int32, (POOL_SUB, 1), 0)
        for g, w in enumerate(POOL_WINDOWS):
            c0 = g * POOL_GROUP_DIM
            c1 = c0 + POOL_GROUP_DIM
            zg = zs_ref[base:base + POOL_SUB, c0:c1]
            acc = zg
            for k in range(1, w):
                acc = acc + zs_ref[base - k:base - k + POOL_SUB, c0:c1]
            cnt = jnp.minimum(pos + 1, w).astype(_F32)
            pooled = acc / cnt - zg
            yg = _dot(pooled.astype(_BF16), wgrp_ref[g]) * scale_ref[:, c0:c1]
            y_ref[p0:p0 + POOL_SUB, c0:c1] = yg.astype(_BF16)
        mix = _dot(y_ref[p0:p0 + POOL_SUB, :], wout_ref[...])
        out = _layer_norm(DEEPNORM_ALPHA * x + mix, g_ref[...], b_ref[...])
        o_ref[0, p0:p0 + POOL_SUB, :] = out
        _store_pieces(op_ref, p0, _pack_halves(out))
    zs_ref[0:POOL_HALO, :] = zs_ref[ts:ts + POOL_HALO, :]


def _pool_layer(x, w_in, w_grp, scale, w_out, ln_g, ln_b):
    bsz, seq, d = x.shape
    ts = POOL_TILE
    steps = seq // ts
    const2 = lambda b, s: (0, 0)
    out_shape = (jax.ShapeDtypeStruct((bsz, seq, d), _F32),
                 jax.ShapeDtypeStruct((PIECES, bsz * seq, LANES), _U32))
    tile = pl.BlockSpec((1, ts, d), lambda b, s: (b, s, 0))
    ptile = pl.BlockSpec((PIECES, ts, LANES), lambda b, s: (0, b * steps + s, 0))
    return pl.pallas_call(
        _pool_kernel,
        grid=(bsz, steps),
        in_specs=[
            tile,
            pl.BlockSpec((d, d), const2),
            pl.BlockSpec((len(POOL_WINDOWS), POOL_GROUP_DIM, POOL_GROUP_DIM), lambda b, s: (0, 0, 0)),
            pl.BlockSpec((1, d), const2),
            pl.BlockSpec((d, d), const2),
            pl.BlockSpec((1, d), const2),
            pl.BlockSpec((1, d), const2),
        ],
        out_specs=(tile, ptile),
        out_shape=out_shape,
        scratch_shapes=[pltpu.VMEM((POOL_HALO + ts, d), _F32),
                        pltpu.VMEM((ts, d), _BF16)],
        compiler_params=pltpu.CompilerParams(
            dimension_semantics=("arbitrary", "arbitrary"),
            vmem_limit_bytes=VMEM_LIMIT),
        name="pool_mixer",
    )(x, w_in.astype(_BF16), w_grp.astype(_BF16), scale.reshape(1, d),
      w_out.astype(_BF16), ln_g.reshape(1, d), ln_b.reshape(1, d))


def _sgu_kernel(x_ref, win_ref, bin_ref, lng_ref, lnb_ref, ws_ref, bs_ref, wout_ref,
                g_ref, b_ref, o_ref, op_ref, gated_ref):
    ts = x_ref.shape[0]
    for p0 in range(0, ts, SGU_SUB):
        x = x_ref[p0:p0 + SGU_SUB, :]
        xb = x.astype(_BF16)
        v = _gelu_tanh(_dot(xb, win_ref[:, SGU_WIDTH:]) + bin_ref[:, SGU_WIDTH:])
        v = _layer_norm(v, lng_ref[...], lnb_ref[...]).astype(_BF16)
        u = _gelu_tanh(_dot(xb, win_ref[:, :SGU_WIDTH]) + bin_ref[:, :SGU_WIDTH])
        for r0 in range(0, SGU_SUB, SGU_CHUNK):
            for h in range(SGU_HEADS):
                c0 = h * SGU_HEAD_DIM
                sv = _dot(ws_ref[h], v[r0:r0 + SGU_CHUNK, c0:c0 + SGU_HEAD_DIM]) + bs_ref[h]
                gated_ref[p0 + r0:p0 + r0 + SGU_CHUNK, c0:c0 + SGU_HEAD_DIM] = (
                    u[r0:r0 + SGU_CHUNK, c0:c0 + SGU_HEAD_DIM] * sv).astype(_BF16)
        mix = _dot(gated_ref[p0:p0 + SGU_SUB, :], wout_ref[...])
        out = _layer_norm(DEEPNORM_ALPHA * x + mix, g_ref[...], b_ref[...])
        o_ref[p0:p0 + SGU_SUB, :] = out
        _store_pieces(op_ref, p0, _pack_halves(out))


def _sgu_layer(x, w_in, b_in, ln_g, ln_b, w_s, b_s, w_out, mix_g, mix_b):
    n_tok, d = x.shape
    ts = SGU_TILE
    const2 = lambda i: (0, 0)
    const3 = lambda i: (0, 0, 0)
    causal = jnp.tril(jnp.ones((SGU_CHUNK, SGU_CHUNK), w_s.dtype))
    ws = (w_s * causal[None]).astype(_BF16)
    tile = pl.BlockSpec((ts, d), lambda i: (i, 0))
    ptile = pl.BlockSpec((PIECES, ts, LANES), lambda i: (0, i, 0))
    out_shape = (jax.ShapeDtypeStruct((n_tok, d), _F32),
                 jax.ShapeDtypeStruct((PIECES, n_tok, LANES), _U32))
    return pl.pallas_call(
        _sgu_kernel,
        grid=(n_tok // ts,),
        in_specs=[
            tile,
            pl.BlockSpec((d, 2 * SGU_WIDTH), const2),
            pl.BlockSpec((1, 2 * SGU_WIDTH), const2),
            pl.BlockSpec((1, SGU_WIDTH), const2),
            pl.BlockSpec((1, SGU_WIDTH), const2),
            pl.BlockSpec((SGU_HEADS, SGU_CHUNK, SGU_CHUNK), const3),
            pl.BlockSpec((SGU_HEADS, SGU_CHUNK, 1), const3),
            pl.BlockSpec((SGU_WIDTH, d), const2),
            pl.BlockSpec((1, d), const2),
            pl.BlockSpec((1, d), const2),
        ],
        out_specs=(tile, ptile),
        out_shape=out_shape,
        scratch_shapes=[pltpu.VMEM((ts, SGU_WIDTH), _BF16)],
        compiler_params=pltpu.CompilerParams(
            dimension_semantics=("arbitrary",),
            vmem_limit_bytes=VMEM_LIMIT),
        name="sgu_mixer",
    )(x, w_in.astype(_BF16), b_in.reshape(1, -1), ln_g.reshape(1, -1), ln_b.reshape(1, -1),
      ws, b_s.reshape(SGU_HEADS, SGU_CHUNK, 1), w_out.astype(_BF16),
      mix_g.reshape(1, d), mix_b.reshape(1, d))


def _all_max(a):
    return jnp.max(jnp.max(a, axis=0, keepdims=True), axis=1, keepdims=True)


def _all_min(a):
    return jnp.min(jnp.min(a, axis=0, keepdims=True), axis=1, keepdims=True)


def _all_sum(a):
    return jnp.sum(jnp.sum(a, axis=0, keepdims=True), axis=1, keepdims=True)


def _route_kernel(x_ref, whi_ref, wlo_ref, bias_ref, earlier_ref,
                  ek_ref, rk_ref, gk_ref, cnt_ref, carry_ref):
    i = pl.program_id(0)
    ts = x_ref.shape[0]

    @pl.when(i == 0)
    def _():
        carry_ref[...] = jnp.zeros_like(carry_ref)

    x = x_ref[...]
    x_hi = x.astype(_BF16)
    x_lo = (x - x_hi.astype(_F32)).astype(_BF16)
    nt = lambda a, b: lax.dot_general(a, b, (((1,), (1,)), ((), ())),
                                      preferred_element_type=_F32)
    logits = nt(whi_ref[...], x_hi) + (nt(whi_ref[...], x_lo) + nt(wlo_ref[...], x_hi))
    scores = jax.nn.sigmoid(logits).reshape(N_GROUPS, GROUP_SIZE, ts)
    biased = scores + bias_ref[...]
    neg_inf = jnp.float32(-jnp.inf)
    shape3 = (N_GROUPS, GROUP_SIZE, ts)
    in_grp = lax.broadcasted_iota(jnp.int32, shape3, 1)
    grp = lax.broadcasted_iota(jnp.int32, shape3, 0)
    eid = grp * GROUP_SIZE + in_grp

    m1 = jnp.max(biased, axis=1, keepdims=True)
    first1 = jnp.min(jnp.where(biased == m1, in_grp, GROUP_SIZE), axis=1, keepdims=True)
    m2 = jnp.max(jnp.where(in_grp == first1, neg_inf, biased), axis=1, keepdims=True)
    gscore = m1 + m2

    gid = lax.broadcasted_iota(jnp.int32, (N_GROUPS, 1, ts), 0)
    gsel = jnp.zeros((N_GROUPS, 1, ts), jnp.bool_)
    for _ in range(TOPK_GROUPS):
        m = jnp.max(gscore, axis=0, keepdims=True)
        first = jnp.min(jnp.where(gscore == m, gid, N_GROUPS), axis=0, keepdims=True)
        pick = gid == first
        gsel = jnp.logical_or(gsel, pick)
        gscore = jnp.where(pick, neg_inf, gscore)

    masked = jnp.where(gsel, biased, neg_inf)
    picked_any = jnp.zeros(shape3, jnp.bool_)
    e_k, s_k = [], []
    for _ in range(TOP_K):
        m = _all_max(masked)
        first = _all_min(jnp.where(masked == m, eid, N_EXPERTS))
        pick = eid == first
        picked_any = jnp.logical_or(picked_any, pick)
        masked = jnp.where(pick, neg_inf, masked)
        e_k.append(first)
        s_k.append(_all_sum(jnp.where(pick, scores, 0.0)))
    denom = s_k[0]
    for k in range(1, TOP_K):
        denom = denom + s_k[k]

    sel = picked_any.astype(_F32).reshape(N_EXPERTS, ts)
    rank = _dot(sel.astype(_BF16), earlier_ref[...]) + carry_ref[...]
    rank3 = rank.reshape(shape3)
    carry_ref[...] += jnp.sum(sel, axis=1, keepdims=True)
    cnt_ref[...] = carry_ref[...]

    for k in range(TOP_K):
        r = _all_sum(jnp.where(eid == e_k[k], rank3, 0.0))
        ek_ref[k:k + 1, :] = e_k[k].reshape(1, ts)
        rk_ref[k:k + 1, :] = r.reshape(1, ts).astype(jnp.int32)
        gk_ref[k:k + 1, :] = (s_k[k] / denom * ROUTED_SCALE).reshape(1, ts)


def _route(x, w_router, router_bias, part, n_parts):
    n_tok, d = x.shape
    ts = ROUTE_TILE
    n_part = n_tok // n_parts
    steps = n_part // ts
    out_shape = (jax.ShapeDtypeStruct((TOP_K, n_part), jnp.int32),
                 jax.ShapeDtypeStruct((TOP_K, n_part), jnp.int32),
                 jax.ShapeDtypeStruct((TOP_K, n_part), _F32),
                 jax.ShapeDtypeStruct((N_EXPERTS, 1), _F32))
    kspec = pl.BlockSpec((TOP_K, ts), lambda i: (0, i))
    const2 = lambda i: (0, 0)
    w_t = w_router.T
    w_hi = w_t.astype(_BF16)
    w_lo = (w_t - w_hi.astype(_F32)).astype(_BF16)
    pos = jnp.arange(ts, dtype=jnp.int32)
    earlier = (pos[:, None] < pos[None, :]).astype(_BF16)
    return pl.pallas_call(
        _route_kernel,
        grid=(steps,),
        in_specs=[
            pl.BlockSpec((ts, d), lambda i: (part * steps + i, 0)),
            pl.BlockSpec((N_EXPERTS, d), const2),
            pl.BlockSpec((N_EXPERTS, d), const2),
            pl.BlockSpec((N_GROUPS, GROUP_SIZE, 1), lambda i: (0, 0, 0)),
            pl.BlockSpec((ts, ts), const2),
        ],
        out_specs=(kspec, kspec, kspec, pl.BlockSpec((N_EXPERTS, 1), const2)),
        out_shape=out_shape,
        scratch_shapes=[pltpu.VMEM((N_EXPERTS, 1), _F32)],
        compiler_params=pltpu.CompilerParams(
            dimension_semantics=("arbitrary",),
            vmem_limit_bytes=VMEM_LIMIT),
        name="moe_route",
    )(x, w_hi, w_lo, router_bias.reshape(N_GROUPS, GROUP_SIZE, 1), earlier)


def _sc_mesh():
    return plsc.VectorSubcoreMesh(core_axis_name="core", subcore_axis_name="subcore")


def _sc_scatter_rows(src, idx, n_out, part, n_parts):
    d = src.shape[1]
    n_part = idx.shape[1]
    wins = n_part // SC_WINDOW

    def body(src_hbm, idx_hbm, out_hbm):
        def step(src_vmem, idx_vmem):
            for k in range(TOP_K):
                pltpu.sync_copy(src_vmem, out_hbm.at[idx_vmem.at[k]])

        pltpu.emit_pipeline(
            step,
            grid=(PIECES * wins,),
            in_specs=[pl.BlockSpec((SC_WINDOW, d),
                                   index_map=lambda i: ((i // wins) * n_parts * wins
                                                        + part * wins + i % wins, 0)),
                      pl.BlockSpec((TOP_K, SC_WINDOW), index_map=lambda i: (i // wins, i % wins))],
            out_specs=[],
            core_axis_name=("core", "subcore"),
            dimension_semantics=(pltpu.PARALLEL,),
        )(src_hbm, idx_hbm)

    return pl.kernel(body, out_type=jax.ShapeDtypeStruct((n_out, d), src.dtype),
                     mesh=_sc_mesh(), scratch_types=[], name="sc_scatter_rows")(src, idx)


def _sc_gather_rows(table, idx):
    d = table.shape[1]
    n_idx_rows, n_tok = idx.shape
    wins = n_tok // SC_WINDOW

    def body(table_hbm, idx_hbm, out_hbm):
        def step(idx_vmem, out_vmem):
            pltpu.sync_copy(table_hbm.at[idx_vmem.at[0]], out_vmem)

        pltpu.emit_pipeline(
            step,
            grid=(n_idx_rows * wins,),
            in_specs=[pl.BlockSpec((1, SC_WINDOW), index_map=lambda i: (i // wins, i % wins))],
            out_specs=[pl.BlockSpec((SC_WINDOW, d), index_map=lambda i: (i, 0))],
            core_axis_name=("core", "subcore"),
            dimension_semantics=(pltpu.PARALLEL,),
        )(idx_hbm, out_hbm)

    return pl.kernel(body, out_type=jax.ShapeDtypeStruct((n_idx_rows * n_tok, d), table.dtype),
                     mesh=_sc_mesh(), scratch_types=[], name="sc_gather_rows")(table, idx)


def _index_kernel(start_ref, ek_ref, rk_ref, idx_ref, *, n_rows):
    e = ek_ref[...]
    start = jnp.zeros(e.shape, jnp.int32)
    for ex in range(N_EXPERTS):
        start = jnp.where(e == ex, start_ref[ex], start)
    dest = start + rk_ref[...]
    for c in range(PIECES):
        idx_ref[c] = dest + c * n_rows


def _row_indices(padded_start, e_k, r_k, n_rows):
    n_tok = e_k.shape[1]
    ts = INDEX_TILE
    kspec = pl.BlockSpec((TOP_K, ts), lambda i, st: (0, i))
    grid_spec = pltpu.PrefetchScalarGridSpec(
        num_scalar_prefetch=1,
        grid=(n_tok // ts,),
        in_specs=[kspec, kspec],
        out_specs=pl.BlockSpec((PIECES, TOP_K, ts), lambda i, st: (0, 0, i)),
    )
    return pl.pallas_call(
        functools.partial(_index_kernel, n_rows=n_rows),
        grid_spec=grid_spec,
        out_shape=jax.ShapeDtypeStruct((PIECES, TOP_K, n_tok), jnp.int32),
        compiler_params=pltpu.CompilerParams(dimension_semantics=("arbitrary",)),
        name="moe_row_indices",
    )(padded_start, e_k, r_k)


def _expert_kernel(chunk0_ref, nchunk_ref, cnt_ref, total_ref,
                   xs_hbm, wg_ref, wu_ref, wd_ref, o_hbm,
                   xbuf, obuf, wgu_bf, wd_bf, sem_in, sem_out):
    e = pl.program_id(0)
    total = total_ref[0]
    ch = EXPERT_ROWS
    ahead = IN_SLOTS - 1

    def in_copy(g):
        slot = g % IN_SLOTS
        return pltpu.make_async_copy(xs_hbm.at[:, pl.ds(g * ch, ch), :], xbuf.at[slot],
                                     sem_in.at[slot])

    def out_copy(g):
        slot = g % OUT_SLOTS
        return pltpu.make_async_copy(obuf.at[slot], o_hbm.at[:, pl.ds(g * ch, ch), :],
                                     sem_out.at[slot])

    @pl.when(e == 0)
    def _():
        obuf[...] = jnp.zeros_like(obuf)
        for g in range(ahead):
            @pl.when(g < total)
            def _():
                in_copy(g).start()

    wgu_bf[:, :EXPERT_DIM] = wg_ref[0, 0].astype(_BF16)
    wgu_bf[:, EXPERT_DIM:] = wu_ref[0, 0].astype(_BF16)
    wd_bf[...] = wd_ref[0, 0].astype(_BF16)

    def chunk(j, carry):
        g = chunk0_ref[e] + j
        in_copy(g).wait()

        @pl.when(g + ahead < total)
        def _():
            in_copy(g + ahead).start()

        @pl.when(g >= OUT_SLOTS)
        def _():
            out_copy(g - OUT_SLOTS).wait()

        islot = g % IN_SLOTS
        oslot = g % OUT_SLOTS
        valid = cnt_ref[e] - j * ch

        def sub_block(s):
            r0 = s * EXPERT_SUB
            w = jnp.concatenate([xbuf[islot, c, r0:r0 + EXPERT_SUB, :] for c in range(PIECES)],
                                axis=1)
            lo, hi = _unpack_halves(w)
            gu = (_dot(lo.astype(_BF16), wgu_bf[:PACKED]) +
                  _dot(hi.astype(_BF16), wgu_bf[PACKED:]))
            h = _silu(gu[:, :EXPERT_DIM]) * gu[:, EXPERT_DIM:]
            y = _pack_halves(_dot(h.astype(_BF16), wd_bf[...]))
            for c in range(PIECES):
                obuf[oslot, c, r0:r0 + EXPERT_SUB, :] = y[:, c * LANES:(c + 1) * LANES]

        n_sub = ch // EXPERT_SUB

        @pl.when(valid >= ch)
        def _():
            for s in range(n_sub):
                sub_block(s)

        @pl.when(valid < ch)
        def _():
            for s in range(n_sub):
                @pl.when(s * EXPERT_SUB < valid)
                def _():
                    sub_block(s)

        out_copy(g).start()
        return carry

    lax.fori_loop(0, nchunk_ref[e], chunk, 0)

    @pl.when(e == pl.num_programs(0) - 1)
    def _():
        for back in range(OUT_SLOTS, 0, -1):
            @pl.when(total >= back)
            def _():
                out_copy(total - back).wait()


def _experts(layer, chunk0, nchunk, counts, total, xs, w_gate, w_up, w_down):
    d = D_MODEL
    ch = EXPERT_ROWS
    wspec_in = pl.BlockSpec((1, 1, d, EXPERT_DIM), lambda e, *_: (layer, e, 0, 0))
    any_spec = pl.BlockSpec(memory_space=pl.ANY)
    grid_spec = pltpu.PrefetchScalarGridSpec(
        num_scalar_prefetch=4,
        grid=(N_EXPERTS,),
        in_specs=[
            any_spec,
            wspec_in,
            wspec_in,
            pl.BlockSpec((1, 1, EXPERT_DIM, d), lambda e, *_: (layer, e, 0, 0)),
        ],
        out_specs=any_spec,
        scratch_shapes=[pltpu.VMEM((IN_SLOTS, PIECES, ch, LANES), _U32),
                        pltpu.VMEM((OUT_SLOTS, PIECES, ch, LANES), _U32),
                        pltpu.VMEM((d, 2 * EXPERT_DIM), _BF16),
                        pltpu.VMEM((EXPERT_DIM, d), _BF16),
                        pltpu.SemaphoreType.DMA((IN_SLOTS,)),
                        pltpu.SemaphoreType.DMA((OUT_SLOTS,))],
    )
    return pl.pallas_call(
        _expert_kernel,
        grid_spec=grid_spec,
        out_shape=jax.ShapeDtypeStruct(xs.shape, _U32),
        compiler_params=pltpu.CompilerParams(
            dimension_semantics=("arbitrary",),
            vmem_limit_bytes=VMEM_LIMIT),
        name="moe_experts",
    )(chunk0, nchunk, counts, total, xs, w_gate, w_up, w_down)


def _combine_kernel(x_ref, p_ref, gate_ref, sg_ref, su_ref, sd_ref, g_ref, b_ref, *rest):
    o_ref = rest[-1]
    x = x_ref[...]
    xb = x.astype(_BF16)
    gates = gate_ref[...].T
    r_lo = r_hi = None
    for k in range(TOP_K):
        w = jnp.concatenate([p_ref[c, k] for c in range(PIECES)], axis=1)
        lo, hi = _unpack_halves(w)
        gk = gates[:, k:k + 1]
        r_lo = gk * lo if r_lo is None else r_lo + gk * lo
        r_hi = gk * hi if r_hi is None else r_hi + gk * hi
    routed = jnp.concatenate([r_lo, r_hi], axis=1)
    h = _silu(_dot(xb, sg_ref[...])) * _dot(xb, su_ref[...])
    shared = _dot(h.astype(_BF16), sd_ref[...])
    o_ref[...] = _layer_norm(DEEPNORM_ALPHA * x + (routed + shared), g_ref[...], b_ref[...])


def _combine(x, picked, gates, sh_gate, sh_up, sh_down, ln_g, ln_b, part, n_parts, prev):
    n_tok, d = x.shape
    ts = COMBINE_TILE
    steps = n_tok // n_parts // ts
    const2 = lambda i: (0, 0)
    rows_map = lambda i: (part * steps + i, 0)
    in_specs = [
        pl.BlockSpec((ts, d), rows_map),
        pl.BlockSpec((PIECES, TOP_K, ts, LANES), lambda i: (0, 0, i, 0)),
        pl.BlockSpec((TOP_K, ts), lambda i: (0, i)),
        pl.BlockSpec((d, EXPERT_DIM), const2),
        pl.BlockSpec((d, EXPERT_DIM), const2),
        pl.BlockSpec((EXPERT_DIM, d), const2),
        pl.BlockSpec((1, d), const2),
        pl.BlockSpec((1, d), const2),
    ]
    args = [x, picked, gates, sh_gate.astype(_BF16), sh_up.astype(_BF16), sh_down.astype(_BF16),
            ln_g.reshape(1, d), ln_b.reshape(1, d)]
    aliases = {}
    if prev is not None:
        in_specs.append(pl.BlockSpec(memory_space=pl.ANY))
        aliases = {len(args): 0}
        args.append(prev)
    return pl.pallas_call(
        _combine_kernel,
        grid=(steps,),
        in_specs=in_specs,
        out_specs=pl.BlockSpec((ts, d), rows_map),
        out_shape=jax.ShapeDtypeStruct((n_tok, d), _F32),
        input_output_aliases=aliases,
        compiler_params=pltpu.CompilerParams(
            dimension_semantics=("arbitrary",),
            vmem_limit_bytes=VMEM_LIMIT),
        name="moe_combine",
    )(*args)


def _moe_layer(layer, x, x_packed, w_router, router_bias, w_gate, w_up, w_down,
               sh_gate, sh_up, sh_down, ln_g, ln_b):
    n_tok, d = x.shape
    bm = EXPERT_ROWS
    n_part = n_tok // TOKEN_PARTS
    n_rows = n_part * TOP_K + N_EXPERTS * bm
    x_flat = x_packed.reshape(PIECES * n_tok, LANES)

    routed = []
    for part in range(TOKEN_PARTS):
        e_k, r_k, g_k, counts = _route(x, w_router, router_bias, part, TOKEN_PARTS)
        counts = counts.reshape(N_EXPERTS).astype(jnp.int32)
        padded = (counts + bm - 1) // bm * bm
        padded_end = jnp.cumsum(padded)
        padded_start = padded_end - padded
        idx = _row_indices(padded_start, e_k, r_k, n_rows).reshape(PIECES * TOP_K, n_part)
        routed.append((idx, g_k, padded_start // bm, padded // bm, counts,
                       padded_end[-1:] // bm))

    rows = []
    for part, (idx, _, chunk0, nchunk, counts, total) in enumerate(routed):
        xs = _sc_scatter_rows(x_flat, idx, PIECES * n_rows, part, TOKEN_PARTS)
        rows.append(_experts(layer, chunk0, nchunk, counts, total,
                             xs.reshape(PIECES, n_rows, LANES), w_gate, w_up, w_down))

    out = None
    for part, (idx, gates, *_) in enumerate(routed):
        picked = _sc_gather_rows(rows[part].reshape(PIECES * n_rows, LANES), idx)
        picked = picked.reshape(PIECES, TOP_K, n_part, LANES)
        out = _combine(x, picked, gates, sh_gate, sh_up, sh_down, ln_g, ln_b,
                       part, TOKEN_PARTS, out)
    return out


def kernel(x, pool_w_in, pool_w_grp, pool_scale, pool_w_out, sgu_w_in, sgu_b_in, sgu_ln_g, sgu_ln_b, sgu_w_s, sgu_b_s, sgu_w_out, ln_mix_g, ln_mix_b, moe_w_router, moe_router_bias, moe_w_gate, moe_w_up, moe_w_down, moe_sh_gate, moe_sh_up, moe_sh_down, ln_ffn_g, ln_ffn_b):
    bsz, seq, d = x.shape
    n_tok = bsz * seq

    def moe(i, h, h_packed):
        return _moe_layer(i, h, h_packed, moe_w_router[i], moe_router_bias[i], moe_w_gate,
                          moe_w_up, moe_w_down, moe_sh_gate[i], moe_sh_up[i],
                          moe_sh_down[i], ln_ffn_g[i], ln_ffn_b[i])

    h, h_packed = _pool_layer(x, pool_w_in[0], pool_w_grp[0], pool_scale[0], pool_w_out[0],
                              ln_mix_g[0], ln_mix_b[0])
    h = moe(0, h.reshape(n_tok, d), h_packed)
    h, h_packed = _sgu_layer(h, sgu_w_in[0], sgu_b_in[0], sgu_ln_g[0], sgu_ln_b[0], sgu_w_s[0],
                             sgu_b_s[0], sgu_w_out[0], ln_mix_g[1], ln_mix_b[1])
    h = moe(1, h, h_packed)
    return h.reshape(bsz, seq, d)
```

```python
import functools

import jax
import jax.numpy as jnp
from jax import lax
from jax.experimental import pallas as pl
from jax.experimental.pallas import tpu as pltpu
from jax.experimental.pallas import tpu_sc as plsc

D_MODEL = 1024
DEPTH = 2
POOL_WINDOWS = (2, 4, 8, 16)
POOL_GROUP_DIM = D_MODEL // len(POOL_WINDOWS)
POOL_HALO = 16
SGU_CHUNK = 128
SGU_HEADS = 4
SGU_WIDTH = 2 * D_MODEL
SGU_HEAD_DIM = SGU_WIDTH // SGU_HEADS
N_EXPERTS = 64
TOP_K = 8
N_GROUPS = 8
GROUP_SIZE = N_EXPERTS // N_GROUPS
TOPK_GROUPS = 4
EXPERT_DIM = D_MODEL // 4
ROUTED_SCALE = 2.5
DEEPNORM_ALPHA = (2 * DEPTH) ** 0.25
LN_EPS = 1e-5

LANES = 128
PACKED = D_MODEL // 2
PIECES = PACKED // LANES

POOL_TILE = 1024
POOL_SUB = 256
SGU_TILE = 512
SGU_SUB = 256
ROUTE_TILE = 1024
ROUTE_SUB = 512
INDEX_TILE = 2048
EXPERT_ROWS = 512
EXPERT_SUB = 256
IN_SLOTS = 6
OUT_SLOTS = 4
COMBINE_TILE = 512
SC_WINDOW = 128
TOKEN_PARTS = 2
VMEM_LIMIT = 56 * 1024 * 1024

_F32 = jnp.float32
_BF16 = jnp.bfloat16
_U32 = jnp.uint32


def _dot(a, b):
    return jnp.dot(a, b, preferred_element_type=_F32)


def _layer_norm(h, g, b):
    mu = jnp.mean(h, axis=-1, keepdims=True)
    hc = h - mu
    var = jnp.mean(hc * hc, axis=-1, keepdims=True)
    return hc * lax.rsqrt(var + LN_EPS) * g + b


def _silu(x):
    return x * jax.nn.sigmoid(x)


def _gelu_tanh(x):
    c = 0.7978845608028654
    return 0.5 * x * (1.0 + jnp.tanh(c * (x + 0.044715 * (x * x * x))))


def _pack_halves(v):
    half = v.shape[1] // 2
    lo = lax.bitcast_convert_type(v[:, :half].astype(_BF16).astype(_F32), _U32)
    hi = lax.bitcast_convert_type(v[:, half:].astype(_BF16).astype(_F32), _U32)
    return (hi & _U32(0xFFFF0000)) | (lo >> 16)


def _unpack_halves(w):
    lo = lax.bitcast_convert_type(w << 16, _F32)
    hi = lax.bitcast_convert_type(w & _U32(0xFFFF0000), _F32)
    return lo, hi


def _store_pieces(ref, r0, w):
    for c in range(PIECES):
        ref[c, r0:r0 + w.shape[0], :] = w[:, c * LANES:(c + 1) * LANES]


def _pool_kernel(x_ref, win_ref, wgrp_ref, scale_ref, wout_ref, g_ref, b_ref,
                 o_ref, op_ref, zs_ref, y_ref):
    s = pl.program_id(1)
    ts = x_ref.shape[1]

    @pl.when(s == 0)
    def _():
        zs_ref[0:POOL_HALO, :] = jnp.zeros((POOL_HALO, D_MODEL), _F32)

    for p0 in range(0, ts, POOL_SUB):
        z = _dot(x_ref[0, p0:p0 + POOL_SUB, :].astype(_BF16), win_ref[...])
        zs_ref[POOL_HALO + p0:POOL_HALO + p0 + POOL_SUB, :] = z
    for p0 in range(0, ts, POOL_SUB):
        x = x_ref[0, p0:p0 + POOL_SUB, :]
        base = POOL_HALO + p0
        pos = s * ts + p0 + lax.broadcasted_iota(jnp.int32, (POOL_SUB, 1), 0)
        for g, w in enumerate(POOL_WINDOWS):
            c0 = g * POOL_GROUP_DIM
            c1 = c0 + POOL_GROUP_DIM
            zg = zs_ref[base:base + POOL_SUB, c0:c1]
            acc = zg
            for k in range(1, w):
                acc = acc + zs_ref[base - k:base - k + POOL_SUB, c0:c1]
            cnt = jnp.minimum(pos + 1, w).astype(_F32)
            pooled = acc / cnt - zg
            yg = _dot(pooled.astype(_BF16), wgrp_ref[g]) * scale_ref[:, c0:c1]
            y_ref[p0:p0 + POOL_SUB, c0:c1] = yg.astype(_BF16)
        mix = _dot(y_ref[p0:p0 + POOL_SUB, :], wout_ref[...])
        out = _layer_norm(DEEPNORM_ALPHA * x + mix, g_ref[...], b_ref[...])
        o_ref[0, p0:p0 + POOL_SUB, :] = out
        _store_pieces(op_ref, p0, _pack_halves(out))
    zs_ref[0:POOL_HALO, :] = zs_ref[ts:ts + POOL_HALO, :]


def _pool_layer(x, w_in, w_grp, scale, w_out, ln_g, ln_b):
    bsz, seq, d = x.shape
    ts = POOL_TILE
    steps = seq // ts
    const2 = lambda b, s: (0, 0)
    out_shape = (jax.ShapeDtypeStruct((bsz, seq, d), _F32),
                 jax.ShapeDtypeStruct((PIECES, bsz * seq, LANES), _U32))
    tile = pl.BlockSpec((1, ts, d), lambda b, s: (b, s, 0))
    ptile = pl.BlockSpec((PIECES, ts, LANES), lambda b, s: (0, b * steps + s, 0))
    return pl.pallas_call(
        _pool_kernel,
        grid=(bsz, steps),
        in_specs=[
            tile,
            pl.BlockSpec((d, d), const2),
            pl.BlockSpec((len(POOL_WINDOWS), POOL_GROUP_DIM, POOL_GROUP_DIM), lambda b, s: (0, 0, 0)),
            pl.BlockSpec((1, d), const2),
            pl.BlockSpec((d, d), const2),
            pl.BlockSpec((1, d), const2),
            pl.BlockSpec((1, d), const2),
        ],
        out_specs=(tile, ptile),
        out_shape=out_shape,
        scratch_shapes=[pltpu.VMEM((POOL_HALO + ts, d), _F32),
                        pltpu.VMEM((ts, d), _BF16)],
        compiler_params=pltpu.CompilerParams(
            dimension_semantics=("arbitrary", "arbitrary"),
            vmem_limit_bytes=VMEM_LIMIT),
        name="pool_mixer",
    )(x, w_in.astype(_BF16), w_grp.astype(_BF16), scale.reshape(1, d),
      w_out.astype(_BF16), ln_g.reshape(1, d), ln_b.reshape(1, d))


def _sgu_kernel(x_ref, win_ref, bin_ref, lng_ref, lnb_ref, ws_ref, bs_ref, wout_ref,
                g_ref, b_ref, o_ref, op_ref, gated_ref):
    ts = x_ref.shape[0]
    projected = []
    for p0 in range(0, ts, SGU_SUB):
        x = x_ref[p0:p0 + SGU_SUB, :]
        xb = x.astype(_BF16)
        v = _gelu_tanh(_dot(xb, win_ref[:, SGU_WIDTH:]) + bin_ref[:, SGU_WIDTH:])
        v = _layer_norm(v, lng_ref[...], lnb_ref[...]).astype(_BF16)
        u = _gelu_tanh(_dot(xb, win_ref[:, :SGU_WIDTH]) + bin_ref[:, :SGU_WIDTH])
        projected.append((p0, x, u, v))
    for p0, x, u, v in projected:
        for r0 in range(0, SGU_SUB, SGU_CHUNK):
            for h in range(SGU_HEADS):
                c0 = h * SGU_HEAD_DIM
                sv = _dot(ws_ref[h], v[r0:r0 + SGU_CHUNK, c0:c0 + SGU_HEAD_DIM]) + bs_ref[h]
                gated_ref[p0 + r0:p0 + r0 + SGU_CHUNK, c0:c0 + SGU_HEAD_DIM] = (
                    u[r0:r0 + SGU_CHUNK, c0:c0 + SGU_HEAD_DIM] * sv).astype(_BF16)
        mix = _dot(gated_ref[p0:p0 + SGU_SUB, :], wout_ref[...])
        out = _layer_norm(DEEPNORM_ALPHA * x + mix, g_ref[...], b_ref[...])
        o_ref[p0:p0 + SGU_SUB, :] = out
        _store_pieces(op_ref, p0, _pack_halves(out))


def _sgu_layer(x, w_in, b_in, ln_g, ln_b, w_s, b_s, w_out, mix_g, mix_b):
    n_tok, d = x.shape
    ts = SGU_TILE
    const2 = lambda i: (0, 0)
    const3 = lambda i: (0, 0, 0)
    causal = jnp.tril(jnp.ones((SGU_CHUNK, SGU_CHUNK), w_s.dtype))
    ws = (w_s * causal[None]).astype(_BF16)
    tile = pl.BlockSpec((ts, d), lambda i: (i, 0))
    ptile = pl.BlockSpec((PIECES, ts, LANES), lambda i: (0, i, 0))
    out_shape = (jax.ShapeDtypeStruct((n_tok, d), _F32),
                 jax.ShapeDtypeStruct((PIECES, n_tok, LANES), _U32))
    return pl.pallas_call(
        _sgu_kernel,
        grid=(n_tok // ts,),
        in_specs=[
            tile,
            pl.BlockSpec((d, 2 * SGU_WIDTH), const2),
            pl.BlockSpec((1, 2 * SGU_WIDTH), const2),
            pl.BlockSpec((1, SGU_WIDTH), const2),
            pl.BlockSpec((1, SGU_WIDTH), const2),
            pl.BlockSpec((SGU_HEADS, SGU_CHUNK, SGU_CHUNK), const3),
            pl.BlockSpec((SGU_HEADS, SGU_CHUNK, 1), const3),
            pl.BlockSpec((SGU_WIDTH, d), const2),
            pl.BlockSpec((1, d), const2),
            pl.BlockSpec((1, d), const2),
        ],
        out_specs=(tile, ptile),
        out_shape=out_shape,
        scratch_shapes=[pltpu.VMEM((ts, SGU_WIDTH), _BF16)],
        compiler_params=pltpu.CompilerParams(
            dimension_semantics=("arbitrary",),
            vmem_limit_bytes=VMEM_LIMIT),
        name="sgu_mixer",
    )(x, w_in.astype(_BF16), b_in.reshape(1, -1), ln_g.reshape(1, -1), ln_b.reshape(1, -1),
      ws, b_s.reshape(SGU_HEADS, SGU_CHUNK, 1), w_out.astype(_BF16),
      mix_g.reshape(1, d), mix_b.reshape(1, d))


def _all_max(a):
    return jnp.max(jnp.max(a, axis=0, keepdims=True), axis=1, keepdims=True)


def _all_min(a):
    return jnp.min(jnp.min(a, axis=0, keepdims=True), axis=1, keepdims=True)


def _all_sum(a):
    return jnp.sum(jnp.sum(a, axis=0, keepdims=True), axis=1, keepdims=True)


def _route_kernel(x_ref, whi_ref, wlo_ref, bias_ref, earlier_ref,
                  ek_ref, rk_ref, gk_ref, cnt_ref, carry_ref):
    @pl.when(pl.program_id(0) == 0)
    def _():
        carry_ref[...] = jnp.zeros_like(carry_ref)

    nt = lambda a, b: lax.dot_general(a, b, (((1,), (1,)), ((), ())),
                                      preferred_element_type=_F32)
    logits = []
    for p0 in range(0, x_ref.shape[0], ROUTE_SUB):
        x = x_ref[p0:p0 + ROUTE_SUB, :]
        x_hi = x.astype(_BF16)
        x_lo = (x - x_hi.astype(_F32)).astype(_BF16)
        logits.append(nt(whi_ref[...], x_hi) + (nt(whi_ref[...], x_lo) + nt(wlo_ref[...], x_hi)))
    for n, sub_logits in enumerate(logits):
        _route_sub_tile(sub_logits, n * ROUTE_SUB, bias_ref, earlier_ref,
                        ek_ref, rk_ref, gk_ref, carry_ref)
    cnt_ref[...] = carry_ref[...]


def _route_sub_tile(logits, p0, bias_ref, earlier_ref, ek_ref, rk_ref, gk_ref, carry_ref):
    ts = ROUTE_SUB
    scores = jax.nn.sigmoid(logits).reshape(N_GROUPS, GROUP_SIZE, ts)
    biased = scores + bias_ref[...]
    neg_inf = jnp.float32(-jnp.inf)
    shape3 = (N_GROUPS, GROUP_SIZE, ts)
    in_grp = lax.broadcasted_iota(jnp.int32, shape3, 1)
    grp = lax.broadcasted_iota(jnp.int32, shape3, 0)
    eid = grp * GROUP_SIZE + in_grp

    m1 = jnp.max(biased, axis=1, keepdims=True)
    first1 = jnp.min(jnp.where(biased == m1, in_grp, GROUP_SIZE), axis=1, keepdims=True)
    m2 = jnp.max(jnp.where(in_grp == first1, neg_inf, biased), axis=1, keepdims=True)
    gscore = m1 + m2

    gid = lax.broadcasted_iota(jnp.int32, (N_GROUPS, 1, ts), 0)
    gsel = jnp.zeros((N_GROUPS, 1, ts), jnp.bool_)
    for _ in range(TOPK_GROUPS):
        m = jnp.max(gscore, axis=0, keepdims=True)
        first = jnp.min(jnp.where(gscore == m, gid, N_GROUPS), axis=0, keepdims=True)
        pick = gid == first
        gsel = jnp.logical_or(gsel, pick)
        gscore = jnp.where(pick, neg_inf, gscore)

    masked = jnp.where(gsel, biased, neg_inf)
    picked_any = jnp.zeros(shape3, jnp.bool_)
    e_k, s_k = [], []
    for _ in range(TOP_K):
        m = _all_max(masked)
        first = _all_min(jnp.where(masked == m, eid, N_EXPERTS))
        pick = eid == first
        picked_any = jnp.logical_or(picked_any, pick)
        masked = jnp.where(pick, neg_inf, masked)
        e_k.append(first)
        s_k.append(_all_sum(jnp.where(pick, scores, 0.0)))
    denom = s_k[0]
    for k in range(1, TOP_K):
        denom = denom + s_k[k]

    sel = picked_any.astype(_F32).reshape(N_EXPERTS, ts)
    rank = _dot(sel.astype(_BF16), earlier_ref[...]) + carry_ref[...]
    rank3 = rank.reshape(shape3)
    carry_ref[...] += jnp.sum(sel, axis=1, keepdims=True)

    for k in range(TOP_K):
        r = _all_sum(jnp.where(eid == e_k[k], rank3, 0.0))
        ek_ref[k:k + 1, p0:p0 + ts] = e_k[k].reshape(1, ts)
        rk_ref[k:k + 1, p0:p0 + ts] = r.reshape(1, ts).astype(jnp.int32)
        gk_ref[k:k + 1, p0:p0 + ts] = (s_k[k] / denom * ROUTED_SCALE).reshape(1, ts)


def _route(x, w_router, router_bias, part, n_parts):
    n_tok, d = x.shape
    ts = ROUTE_TILE
    n_part = n_tok // n_parts
    steps = n_part // ts
    out_shape = (jax.ShapeDtypeStruct((TOP_K, n_part), jnp.int32),
                 jax.ShapeDtypeStruct((TOP_K, n_part), jnp.int32),
                 jax.ShapeDtypeStruct((TOP_K, n_part), _F32),
                 jax.ShapeDtypeStruct((N_EXPERTS, 1), _F32))
    kspec = pl.BlockSpec((TOP_K, ts), lambda i: (0, i))
    const2 = lambda i: (0, 0)
    w_t = w_router.T
    w_hi = w_t.astype(_BF16)
    w_lo = (w_t - w_hi.astype(_F32)).astype(_BF16)
    pos = jnp.arange(ROUTE_SUB, dtype=jnp.int32)
    earlier = (pos[:, None] < pos[None, :]).astype(_BF16)
    return pl.pallas_call(
        _route_kernel,
        grid=(steps,),
        in_specs=[
            pl.BlockSpec((ts, d), lambda i: (part * steps + i, 0)),
            pl.BlockSpec((N_EXPERTS, d), const2),
            pl.BlockSpec((N_EXPERTS, d), const2),
            pl.BlockSpec((N_GROUPS, GROUP_SIZE, 1), lambda i: (0, 0, 0)),
            pl.BlockSpec((ROUTE_SUB, ROUTE_SUB), const2),
        ],
        out_specs=(kspec, kspec, kspec, pl.BlockSpec((N_EXPERTS, 1), const2)),
        out_shape=out_shape,
        scratch_shapes=[pltpu.VMEM((N_EXPERTS, 1), _F32)],
        compiler_params=pltpu.CompilerParams(
            dimension_semantics=("arbitrary",),
            vmem_limit_bytes=VMEM_LIMIT),
        name="moe_route",
    )(x, w_hi, w_lo, router_bias.reshape(N_GROUPS, GROUP_SIZE, 1), earlier)


def _sc_mesh():
    return plsc.VectorSubcoreMesh(core_axis_name="core", subcore_axis_name="subcore")


def _sc_scatter_rows(src, idx, n_out, part, n_parts):
    d = src.shape[1]
    n_part = idx.shape[1]
    wins = n_part // SC_WINDOW

    def body(src_hbm, idx_hbm, out_hbm):
        def step(src_vmem, idx_vmem):
            for k in range(TOP_K):
                pltpu.sync_copy(src_vmem, out_hbm.at[idx_vmem.at[k]])

        pltpu.emit_pipeline(
            step,
            grid=(PIECES * wins,),
            in_specs=[pl.BlockSpec((SC_WINDOW, d),
                                   index_map=lambda i: ((i // wins) * n_parts * wins
                                                        + part * wins + i % wins, 0)),
                      pl.BlockSpec((TOP_K, SC_WINDOW), index_map=lambda i: (i // wins, i % wins))],
            out_specs=[],
            core_axis_name=("core", "subcore"),
            dimension_semantics=(pltpu.PARALLEL,),
        )(src_hbm, idx_hbm)

    return pl.kernel(body, out_type=jax.ShapeDtypeStruct((n_out, d), src.dtype),
                     mesh=_sc_mesh(), scratch_types=[], name="sc_scatter_rows")(src, idx)


def _sc_gather_rows(table, idx):
    d = table.shape[1]
    n_idx_rows, n_tok = idx.shape
    wins = n_tok // SC_WINDOW

    def body(table_hbm, idx_hbm, out_hbm):
        def step(idx_vmem, out_vmem):
            pltpu.sync_copy(table_hbm.at[idx_vmem.at[0]], out_vmem)

        pltpu.emit_pipeline(
            step,
            grid=(n_idx_rows * wins,),
            in_specs=[pl.BlockSpec((1, SC_WINDOW), index_map=lambda i: (i // wins, i % wins))],
            out_specs=[pl.BlockSpec((SC_WINDOW, d), index_map=lambda i: (i, 0))],
            core_axis_name=("core", "subcore"),
            dimension_semantics=(pltpu.PARALLEL,),
        )(idx_hbm, out_hbm)

    return pl.kernel(body, out_type=jax.ShapeDtypeStruct((n_idx_rows * n_tok, d), table.dtype),
                     mesh=_sc_mesh(), scratch_types=[], name="sc_gather_rows")(table, idx)


def _index_kernel(start_ref, ek_ref, rk_ref, idx_ref, *, n_rows):
    e = ek_ref[...]
    start = jnp.zeros(e.shape, jnp.int32)
    for ex in range(N_EXPERTS):
        start = jnp.where(e == ex, start_ref[ex], start)
    dest = start + rk_ref[...]
    for c in range(PIECES):
        idx_ref[c] = dest + c * n_rows


def _row_indices(padded_start, e_k, r_k, n_rows):
    n_tok = e_k.shape[1]
    ts = INDEX_TILE
    kspec = pl.BlockSpec((TOP_K, ts), lambda i, st: (0, i))
    grid_spec = pltpu.PrefetchScalarGridSpec(
        num_scalar_prefetch=1,
        grid=(n_tok // ts,),
        in_specs=[kspec, kspec],
        out_specs=pl.BlockSpec((PIECES, TOP_K, ts), lambda i, st: (0, 0, i)),
    )
    return pl.pallas_call(
        functools.partial(_index_kernel, n_rows=n_rows),
        grid_spec=grid_spec,
        out_shape=jax.ShapeDtypeStruct((PIECES, TOP_K, n_tok), jnp.int32),
        compiler_params=pltpu.CompilerParams(dimension_semantics=("arbitrary",)),
        name="moe_row_indices",
    )(padded_start, e_k, r_k)


def _expert_kernel(chunk0_ref, nchunk_ref, cnt_ref, total_ref,
                   xs_hbm, wg_ref, wu_ref, wd_ref, o_hbm,
                   xbuf, obuf, wgu_bf, wd_bf, sem_in, sem_out):
    e = pl.program_id(0)
    total = total_ref[0]
    ch = EXPERT_ROWS
    ahead = IN_SLOTS - 1

    def in_copy(g):
        slot = g % IN_SLOTS
        return pltpu.make_async_copy(xs_hbm.at[:, pl.ds(g * ch, ch), :], xbuf.at[slot],
                                     sem_in.at[slot])

    def out_copy(g):
        slot = g % OUT_SLOTS
        return pltpu.make_async_copy(obuf.at[slot], o_hbm.at[:, pl.ds(g * ch, ch), :],
                                     sem_out.at[slot])

    @pl.when(e == 0)
    def _():
        obuf[...] = jnp.zeros_like(obuf)
        for g in range(ahead):
            @pl.when(g < total)
            def _():
                in_copy(g).start()

    wgu_bf[:, :EXPERT_DIM] = wg_ref[0, 0].astype(_BF16)
    wgu_bf[:, EXPERT_DIM:] = wu_ref[0, 0].astype(_BF16)
    wd_bf[...] = wd_ref[0, 0].astype(_BF16)

    def chunk(j, carry):
        g = chunk0_ref[e] + j
        in_copy(g).wait()

        @pl.when(g + ahead < total)
        def _():
            in_copy(g + ahead).start()

        @pl.when(g >= OUT_SLOTS)
        def _():
            out_copy(g - OUT_SLOTS).wait()

        islot = g % IN_SLOTS
        oslot = g % OUT_SLOTS
        valid = cnt_ref[e] - j * ch

        def sub_block(s):
            r0 = s * EXPERT_SUB
            w = jnp.concatenate([xbuf[islot, c, r0:r0 + EXPERT_SUB, :] for c in range(PIECES)],
                                axis=1)
            lo, hi = _unpack_halves(w)
            gu = (_dot(lo.astype(_BF16), wgu_bf[:PACKED]) +
                  _dot(hi.astype(_BF16), wgu_bf[PACKED:]))
            h = _silu(gu[:, :EXPERT_DIM]) * gu[:, EXPERT_DIM:]
            y = _pack_halves(_dot(h.astype(_BF16), wd_bf[...]))
            for c in range(PIECES):
                obuf[oslot, c, r0:r0 + EXPERT_SUB, :] = y[:, c * LANES:(c + 1) * LANES]

        n_sub = ch // EXPERT_SUB

        @pl.when(valid >= ch)
        def _():
            for s in range(n_sub):
                sub_block(s)

        @pl.when(valid < ch)
        def _():
            for s in range(n_sub):
                @pl.when(s * EXPERT_SUB < valid)
                def _():
                    sub_block(s)

        out_copy(g).start()
        return carry

    lax.fori_loop(0, nchunk_ref[e], chunk, 0)

    @pl.when(e == pl.num_programs(0) - 1)
    def _():
        for back in range(OUT_SLOTS, 0, -1):
            @pl.when(total >= back)
            def _():
                out_copy(total - back).wait()


def _experts(layer, chunk0, nchunk, counts, total, xs, w_gate, w_up, w_down):
    d = D_MODEL
    ch = EXPERT_ROWS
    wspec_in = pl.BlockSpec((1, 1, d, EXPERT_DIM), lambda e, *_: (layer, e, 0, 0))
    any_spec = pl.BlockSpec(memory_space=pl.ANY)
    grid_spec = pltpu.PrefetchScalarGridSpec(
        num_scalar_prefetch=4,
        grid=(N_EXPERTS,),
        in_specs=[
            any_spec,
            wspec_in,
            wspec_in,
            pl.BlockSpec((1, 1, EXPERT_DIM, d), lambda e, *_: (layer, e, 0, 0)),
        ],
        out_specs=any_spec,
        scratch_shapes=[pltpu.VMEM((IN_SLOTS, PIECES, ch, LANES), _U32),
                        pltpu.VMEM((OUT_SLOTS, PIECES, ch, LANES), _U32),
                        pltpu.VMEM((d, 2 * EXPERT_DIM), _BF16),
                        pltpu.VMEM((EXPERT_DIM, d), _BF16),
                        pltpu.SemaphoreType.DMA((IN_SLOTS,)),
                        pltpu.SemaphoreType.DMA((OUT_SLOTS,))],
    )
    return pl.pallas_call(
        _expert_kernel,
        grid_spec=grid_spec,
        out_shape=jax.ShapeDtypeStruct(xs.shape, _U32),
        compiler_params=pltpu.CompilerParams(
            dimension_semantics=("arbitrary",),
            vmem_limit_bytes=VMEM_LIMIT),
        name="moe_experts",
    )(chunk0, nchunk, counts, total, xs, w_gate, w_up, w_down)


def _combine_kernel(x_ref, p_ref, gate_ref, sg_ref, su_ref, sd_ref, g_ref, b_ref, *rest):
    o_ref = rest[-1]
    x = x_ref[...]
    xb = x.astype(_BF16)
    gates = gate_ref[...].T
    r_lo = r_hi = None
    for k in range(TOP_K):
        w = jnp.concatenate([p_ref[c, k] for c in range(PIECES)], axis=1)
        lo, hi = _unpack_halves(w)
        gk = gates[:, k:k + 1]
        r_lo = gk * lo if r_lo is None else r_lo + gk * lo
        r_hi = gk * hi if r_hi is None else r_hi + gk * hi
    routed = jnp.concatenate([r_lo, r_hi], axis=1)
    h = _silu(_dot(xb, sg_ref[...])) * _dot(xb, su_ref[...])
    shared = _dot(h.astype(_BF16), sd_ref[...])
    o_ref[...] = _layer_norm(DEEPNORM_ALPHA * x + (routed + shared), g_ref[...], b_ref[...])


def _combine(x, picked, gates, sh_gate, sh_up, sh_down, ln_g, ln_b, part, n_parts, prev):
    n_tok, d = x.shape
    ts = COMBINE_TILE
    steps = n_tok // n_parts // ts
    const2 = lambda i: (0, 0)
    rows_map = lambda i: (part * steps + i, 0)
    in_specs = [
        pl.BlockSpec((ts, d), rows_map),
        pl.BlockSpec((PIECES, TOP_K, ts, LANES), lambda i: (0, 0, i, 0)),
        pl.BlockSpec((TOP_K, ts), lambda i: (0, i)),
        pl.BlockSpec((d, EXPERT_DIM), const2),
        pl.BlockSpec((d, EXPERT_DIM), const2),
        pl.BlockSpec((EXPERT_DIM, d), const2),
        pl.BlockSpec((1, d), const2),
        pl.BlockSpec((1, d), const2),
    ]
    args = [x, picked, gates, sh_gate.astype(_BF16), sh_up.astype(_BF16), sh_down.astype(_BF16),
            ln_g.reshape(1, d), ln_b.reshape(1, d)]
    aliases = {}
    if prev is not None:
        in_specs.append(pl.BlockSpec(memory_space=pl.ANY))
        aliases = {len(args): 0}
        args.append(prev)
    return pl.pallas_call(
        _combine_kernel,
        grid=(steps,),
        in_specs=in_specs,
        out_specs=pl.BlockSpec((ts, d), rows_map),
        out_shape=jax.ShapeDtypeStruct((n_tok, d), _F32),
        input_output_aliases=aliases,
        compiler_params=pltpu.CompilerParams(
            dimension_semantics=("arbitrary",),
            vmem_limit_bytes=VMEM_LIMIT),
        name="moe_combine",
    )(*args)


def _moe_layer(layer, x, x_packed, w_router, router_bias, w_gate, w_up, w_down,
               sh_gate, sh_up, sh_down, ln_g, ln_b):
    n_tok, d = x.shape
    bm = EXPERT_ROWS
    n_part = n_tok // TOKEN_PARTS
    n_rows = n_part * TOP_K + N_EXPERTS * bm
    x_flat = x_packed.reshape(PIECES * n_tok, LANES)

    routed = []
    for part in range(TOKEN_PARTS):
        e_k, r_k, g_k, counts = _route(x, w_router, router_bias, part, TOKEN_PARTS)
        counts = counts.reshape(N_EXPERTS).astype(jnp.int32)
        padded = (counts + bm - 1) // bm * bm
        padded_end = jnp.cumsum(padded)
        padded_start = padded_end - padded
        idx = _row_indices(padded_start, e_k, r_k, n_rows).reshape(PIECES * TOP_K, n_part)
        routed.append((idx, g_k, padded_start // bm, padded // bm, counts,
                       padded_end[-1:] // bm))

    rows = []
    for part, (idx, _, chunk0, nchunk, counts, total) in enumerate(routed):
        xs = _sc_scatter_rows(x_flat, idx, PIECES * n_rows, part, TOKEN_PARTS)
        rows.append(_experts(layer, chunk0, nchunk, counts, total,
                             xs.reshape(PIECES, n_rows, LANES), w_gate, w_up, w_down))

    out = None
    for part, (idx, gates, *_) in enumerate(routed):
        picked = _sc_gather_rows(rows[part].reshape(PIECES * n_rows, LANES), idx)
        picked = picked.reshape(PIECES, TOP_K, n_part, LANES)
        out = _combine(x, picked, gates, sh_gate, sh_up, sh_down, ln_g, ln_b,
                       part, TOKEN_PARTS, out)
    return out


def kernel(x, pool_w_in, pool_w_grp, pool_scale, pool_w_out, sgu_w_in, sgu_b_in, sgu_ln_g, sgu_ln_b, sgu_w_s, sgu_b_s, sgu_w_out, ln_mix_g, ln_mix_b, moe_w_router, moe_router_bias, moe_w_gate, moe_w_up, moe_w_down, moe_sh_gate, moe_sh_up, moe_sh_down, ln_ffn_g, ln_ffn_b):
    bsz, seq, d = x.shape
    n_tok = bsz * seq

    def moe(i, h, h_packed):
        return _moe_layer(i, h, h_packed, moe_w_router[i], moe_router_bias[i], moe_w_gate,
                          moe_w_up, moe_w_down, moe_sh_gate[i], moe_sh_up[i],
                          moe_sh_down[i], ln_ffn_g[i], ln_ffn_b[i])

    h, h_packed = _pool_layer(x, pool_w_in[0], pool_w_grp[0], pool_scale[0], pool_w_out[0],
                              ln_mix_g[0], ln_mix_b[0])
    h = moe(0, h.reshape(n_tok, d), h_packed)
    h, h_packed = _sgu_layer(h, sgu_w_in[0], sgu_b_in[0], sgu_ln_g[0], sgu_ln_b[0], sgu_w_s[0],
                             sgu_b_s[0], sgu_w_out[0], ln_mix_g[1], ln_mix_b[1])
    h = moe(1, h, h_packed)
    return h.reshape(bsz, seq, d)
```

```python
import functools

import jax
import jax.numpy as jnp
from jax import lax
from jax.experimental import pallas as pl
from jax.experimental.pallas import tpu as pltpu
from jax.experimental.pallas import tpu_sc as plsc

D_MODEL = 1024
DEPTH = 2
POOL_WINDOWS = (2, 4, 8, 16)
POOL_GROUP_DIM = D_MODEL // len(POOL_WINDOWS)
POOL_HALO = 16
SGU_CHUNK = 128
SGU_HEADS = 4
SGU_WIDTH = 2 * D_MODEL
SGU_HEAD_DIM = SGU_WIDTH // SGU_HEADS
N_EXPERTS = 64
TOP_K = 8
N_GROUPS = 8
GROUP_SIZE = N_EXPERTS // N_GROUPS
TOPK_GROUPS = 4
EXPERT_DIM = D_MODEL // 4
ROUTED_SCALE = 2.5
DEEPNORM_ALPHA = (2 * DEPTH) ** 0.25
LN_EPS = 1e-5

LANES = 128
PACKED = D_MODEL // 2
PIECES = PACKED // LANES

POOL_TILE = 1024
POOL_SUB = 256
SGU_TILE = 512
SGU_SUB = 256
ROUTE_TILE = 1024
ROUTE_SUB = 512
INDEX_TILE = 2048
EXPERT_ROWS = 512
EXPERT_SUB = 256
IN_SLOTS = 6
OUT_SLOTS = 4
COMBINE_TILE = 512
SC_WINDOW = 128
TOKEN_PARTS = 2
VMEM_LIMIT = 56 * 1024 * 1024

_F32 = jnp.float32
_BF16 = jnp.bfloat16
_U32 = jnp.uint32


def _dot(a, b):
    return jnp.dot(a, b, preferred_element_type=_F32)


def _layer_norm(h, g, b):
    mu = jnp.mean(h, axis=-1, keepdims=True)
    hc = h - mu
    var = jnp.mean(hc * hc, axis=-1, keepdims=True)
    return hc * lax.rsqrt(var + LN_EPS) * g + b


def _silu(x):
    return x * jax.nn.sigmoid(x)


def _gelu_tanh(x):
    c = 0.7978845608028654
    return 0.5 * x * (1.0 + jnp.tanh(c * (x + 0.044715 * (x * x * x))))


def _pack_halves(v):
    half = v.shape[1] // 2
    lo = lax.bitcast_convert_type(v[:, :half].astype(_BF16).astype(_F32), _U32)
    hi = lax.bitcast_convert_type(v[:, half:].astype(_BF16).astype(_F32), _U32)
    return (hi & _U32(0xFFFF0000)) | (lo >> 16)


def _unpack_halves(w):
    lo = lax.bitcast_convert_type(w << 16, _F32)
    hi = lax.bitcast_convert_type(w & _U32(0xFFFF0000), _F32)
    return lo, hi


def _store_pieces(ref, r0, w):
    for c in range(PIECES):
        ref[c, r0:r0 + w.shape[0], :] = w[:, c * LANES:(c + 1) * LANES]


def _pool_kernel(x_ref, win_ref, wgrp_ref, scale_ref, wout_ref, g_ref, b_ref,
                 o_ref, op_ref, zs_ref, y_ref):
    s = pl.program_id(1)
    ts = x_ref.shape[1]

    @pl.when(s == 0)
    def _():
        zs_ref[0:POOL_HALO, :] = jnp.zeros((POOL_HALO, D_MODEL), _F32)

    for p0 in range(0, ts, POOL_SUB):
        z = _dot(x_ref[0, p0:p0 + POOL_SUB, :].astype(_BF16), win_ref[...])
        zs_ref[POOL_HALO + p0:POOL_HALO + p0 + POOL_SUB, :] = z
    for p0 in range(0, ts, POOL_SUB):
        x = x_ref[0, p0:p0 + POOL_SUB, :]
        base = POOL_HALO + p0
        pos = s * ts + p0 + lax.broadcasted_iota(jnp.int32, (POOL_SUB, 1), 0)
        for g, w in enumerate(POOL_WINDOWS):
            c0 = g * POOL_GROUP_DIM
            c1 = c0 + POOL_GROUP_DIM
            zg = zs_ref[base:base + POOL_SUB, c0:c1]
            acc = zg
            for k in range(1, w):
                acc = acc + zs_ref[base - k:base - k + POOL_SUB, c0:c1]
            cnt = jnp.minimum(pos + 1, w).astype(_F32)
            pooled = acc / cnt - zg
            yg = _dot(pooled.astype(_BF16), wgrp_ref[g]) * scale_ref[:, c0:c1]
            y_ref[p0:p0 + POOL_SUB, c0:c1] = yg.astype(_BF16)
        mix = _dot(y_ref[p0:p0 + POOL_SUB, :], wout_ref[...])
        out = _layer_norm(DEEPNORM_ALPHA * x + mix, g_ref[...], b_ref[...])
        o_ref[0, p0:p0 + POOL_SUB, :] = out
        _store_pieces(op_ref, p0, _pack_halves(out))
    zs_ref[0:POOL_HALO, :] = zs_ref[ts:ts + POOL_HALO, :]


def _pool_layer(x, w_in, w_grp, scale, w_out, ln_g, ln_b):
    bsz, seq, d = x.shape
    ts = POOL_TILE
    steps = seq // ts
    const2 = lambda b, s: (0, 0)
    out_shape = (jax.ShapeDtypeStruct((bsz, seq, d), _F32),
                 jax.ShapeDtypeStruct((PIECES, bsz * seq, LANES), _U32))
    tile = pl.BlockSpec((1, ts, d), lambda b, s: (b, s, 0))
    ptile = pl.BlockSpec((PIECES, ts, LANES), lambda b, s: (0, b * steps + s, 0))
    return pl.pallas_call(
        _pool_kernel,
        grid=(bsz, steps),
        in_specs=[
            tile,
            pl.BlockSpec((d, d), const2),
            pl.BlockSpec((len(POOL_WINDOWS), POOL_GROUP_DIM, POOL_GROUP_DIM), lambda b, s: (0, 0, 0)),
            pl.BlockSpec((1, d), const2),
            pl.BlockSpec((d, d), const2),
            pl.BlockSpec((1, d), const2),
            pl.BlockSpec((1, d), const2),
        ],
        out_specs=(tile, ptile),
        out_shape=out_shape,
        scratch_shapes=[pltpu.VMEM((POOL_HALO + ts, d), _F32),
                        pltpu.VMEM((ts, d), _BF16)],
        compiler_params=pltpu.CompilerParams(
            dimension_semantics=("arbitrary", "arbitrary"),
            vmem_limit_bytes=VMEM_LIMIT),
        name="pool_mixer",
    )(x, w_in.astype(_BF16), w_grp.astype(_BF16), scale.reshape(1, d),
      w_out.astype(_BF16), ln_g.reshape(1, d), ln_b.reshape(1, d))


def _sgu_kernel(x_ref, win_ref, bin_ref, lng_ref, lnb_ref, ws_ref, bs_ref, wout_ref,
                g_ref, b_ref, o_ref, op_ref, gated_ref):
    ts = x_ref.shape[0]
    projected = []
    for p0 in range(0, ts, SGU_SUB):
        x = x_ref[p0:p0 + SGU_SUB, :]
        xb = x.astype(_BF16)
        v = _gelu_tanh(_dot(xb, win_ref[:, SGU_WIDTH:]) + bin_ref[:, SGU_WIDTH:])
        v = _layer_norm(v, lng_ref[...], lnb_ref[...]).astype(_BF16)
        u = _gelu_tanh(_dot(xb, win_ref[:, :SGU_WIDTH]) + bin_ref[:, :SGU_WIDTH])
        projected.append((p0, x, u, v))
    for p0, x, u, v in projected:
        for r0 in range(0, SGU_SUB, SGU_CHUNK):
            for h in range(SGU_HEADS):
                c0 = h * SGU_HEAD_DIM
                sv = _dot(ws_ref[h], v[r0:r0 + SGU_CHUNK, c0:c0 + SGU_HEAD_DIM]) + bs_ref[h]
                gated_ref[p0 + r0:p0 + r0 + SGU_CHUNK, c0:c0 + SGU_HEAD_DIM] = (
                    u[r0:r0 + SGU_CHUNK, c0:c0 + SGU_HEAD_DIM] * sv).astype(_BF16)
        mix = _dot(gated_ref[p0:p0 + SGU_SUB, :], wout_ref[...])
        out = _layer_norm(DEEPNORM_ALPHA * x + mix, g_ref[...], b_ref[...])
        o_ref[p0:p0 + SGU_SUB, :] = out
        _store_pieces(op_ref, p0, _pack_halves(out))


def _sgu_layer(x, w_in, b_in, ln_g, ln_b, w_s, b_s, w_out, mix_g, mix_b):
    n_tok, d = x.shape
    ts = SGU_TILE
    const2 = lambda i: (0, 0)
    const3 = lambda i: (0, 0, 0)
    causal = jnp.tril(jnp.ones((SGU_CHUNK, SGU_CHUNK), w_s.dtype))
    ws = (w_s * causal[None]).astype(_BF16)
    tile = pl.BlockSpec((ts, d), lambda i: (i, 0))
    ptile = pl.BlockSpec((PIECES, ts, LANES), lambda i: (0, i, 0))
    out_shape = (jax.ShapeDtypeStruct((n_tok, d), _F32),
                 jax.ShapeDtypeStruct((PIECES, n_tok, LANES), _U32))
    return pl.pallas_call(
        _sgu_kernel,
        grid=(n_tok // ts,),
        in_specs=[
            tile,
            pl.BlockSpec((d, 2 * SGU_WIDTH), const2),
            pl.BlockSpec((1, 2 * SGU_WIDTH), const2),
            pl.BlockSpec((1, SGU_WIDTH), const2),
            pl.BlockSpec((1, SGU_WIDTH), const2),
            pl.BlockSpec((SGU_HEADS, SGU_CHUNK, SGU_CHUNK), const3),
            pl.BlockSpec((SGU_HEADS, SGU_CHUNK, 1), const3),
            pl.BlockSpec((SGU_WIDTH, d), const2),
            pl.BlockSpec((1, d), const2),
            pl.BlockSpec((1, d), const2),
        ],
        out_specs=(tile, ptile),
        out_shape=out_shape,
        scratch_shapes=[pltpu.VMEM((ts, SGU_WIDTH), _BF16)],
        compiler_params=pltpu.CompilerParams(
            dimension_semantics=("arbitrary",),
            vmem_limit_bytes=VMEM_LIMIT),
        name="sgu_mixer",
    )(x, w_in.astype(_BF16), b_in.reshape(1, -1), ln_g.reshape(1, -1), ln_b.reshape(1, -1),
      ws, b_s.reshape(SGU_HEADS, SGU_CHUNK, 1), w_out.astype(_BF16),
      mix_g.reshape(1, d), mix_b.reshape(1, d))


def _all_max(a):
    return jnp.max(jnp.max(a, axis=0, keepdims=True), axis=1, keepdims=True)


def _all_min(a):
    return jnp.min(jnp.min(a, axis=0, keepdims=True), axis=1, keepdims=True)


def _all_sum(a):
    return jnp.sum(jnp.sum(a, axis=0, keepdims=True), axis=1, keepdims=True)


def _route_kernel(x_ref, whi_ref, wlo_ref, bias_ref, earlier_ref,
                  ek_ref, rk_ref, gk_ref, cnt_ref, carry_ref):
    @pl.when(pl.program_id(0) == 0)
    def _():
        carry_ref[...] = jnp.zeros_like(carry_ref)

    nt = lambda a, b: lax.dot_general(a, b, (((1,), (1,)), ((), ())),
                                      preferred_element_type=_F32)
    logits = []
    for p0 in range(0, x_ref.shape[0], ROUTE_SUB):
        x = x_ref[p0:p0 + ROUTE_SUB, :]
        x_hi = x.astype(_BF16)
        x_lo = (x - x_hi.astype(_F32)).astype(_BF16)
        logits.append(nt(whi_ref[...], x_hi) + (nt(whi_ref[...], x_lo) + nt(wlo_ref[...], x_hi)))
    for n, sub_logits in enumerate(logits):
        _route_sub_tile(sub_logits, n * ROUTE_SUB, bias_ref, earlier_ref,
                        ek_ref, rk_ref, gk_ref, carry_ref)
    cnt_ref[...] = carry_ref[...]


def _route_sub_tile(logits, p0, bias_ref, earlier_ref, ek_ref, rk_ref, gk_ref, carry_ref):
    ts = ROUTE_SUB
    scores = jax.nn.sigmoid(logits).reshape(N_GROUPS, GROUP_SIZE, ts)
    biased = scores + bias_ref[...]
    neg_inf = jnp.float32(-jnp.inf)
    shape3 = (N_GROUPS, GROUP_SIZE, ts)
    in_grp = lax.broadcasted_iota(jnp.int32, shape3, 1)
    grp = lax.broadcasted_iota(jnp.int32, shape3, 0)
    eid = grp * GROUP_SIZE + in_grp

    m1 = jnp.max(biased, axis=1, keepdims=True)
    first1 = jnp.min(jnp.where(biased == m1, in_grp, GROUP_SIZE), axis=1, keepdims=True)
    m2 = jnp.max(jnp.where(in_grp == first1, neg_inf, biased), axis=1, keepdims=True)
    gscore = m1 + m2

    gid = lax.broadcasted_iota(jnp.int32, (N_GROUPS, 1, ts), 0)
    gsel = jnp.zeros((N_GROUPS, 1, ts), jnp.bool_)
    for _ in range(TOPK_GROUPS):
        m = jnp.max(gscore, axis=0, keepdims=True)
        first = jnp.min(jnp.where(gscore == m, gid, N_GROUPS), axis=0, keepdims=True)
        pick = gid == first
        gsel = jnp.logical_or(gsel, pick)
        gscore = jnp.where(pick, neg_inf, gscore)

    masked = jnp.where(gsel, biased, neg_inf)
    picked_any = jnp.zeros(shape3, jnp.bool_)
    e_k, s_k = [], []
    for _ in range(TOP_K):
        m = _all_max(masked)
        first = _all_min(jnp.where(masked == m, eid, N_EXPERTS))
        pick = eid == first
        picked_any = jnp.logical_or(picked_any, pick)
        masked = jnp.where(pick, neg_inf, masked)
        e_k.append(first)
        s_k.append(_all_sum(jnp.where(pick, scores, 0.0)))
    denom = s_k[0]
    for k in range(1, TOP_K):
        denom = denom + s_k[k]

    sel = picked_any.astype(_F32).reshape(N_EXPERTS, ts)
    rank = _dot(sel.astype(_BF16), earlier_ref[...]) + carry_ref[...]
    rank3 = rank.reshape(shape3)
    carry_ref[...] += jnp.sum(sel, axis=1, keepdims=True)

    for k in range(TOP_K):
        r = _all_sum(jnp.where(eid == e_k[k], rank3, 0.0))
        ek_ref[k:k + 1, p0:p0 + ts] = e_k[k].reshape(1, ts)
        rk_ref[k:k + 1, p0:p0 + ts] = r.reshape(1, ts).astype(jnp.int32)
        gk_ref[k:k + 1, p0:p0 + ts] = (s_k[k] / denom * ROUTED_SCALE).reshape(1, ts)


def _route(x, w_router, router_bias, part, n_parts):
    n_tok, d = x.shape
    ts = ROUTE_TILE
    n_part = n_tok // n_parts
    steps = n_part // ts
    out_shape = (jax.ShapeDtypeStruct((TOP_K, n_part), jnp.int32),
                 jax.ShapeDtypeStruct((TOP_K, n_part), jnp.int32),
                 jax.ShapeDtypeStruct((TOP_K, n_part), _F32),
                 jax.ShapeDtypeStruct((N_EXPERTS, 1), _F32))
    kspec = pl.BlockSpec((TOP_K, ts), lambda i: (0, i))
    const2 = lambda i: (0, 0)
    w_t = w_router.T
    w_hi = w_t.astype(_BF16)
    w_lo = (w_t - w_hi.astype(_F32)).astype(_BF16)
    pos = jnp.arange(ROUTE_SUB, dtype=jnp.int32)
    earlier = (pos[:, None] < pos[None, :]).astype(_BF16)
    return pl.pallas_call(
        _route_kernel,
        grid=(steps,),
        in_specs=[
            pl.BlockSpec((ts, d), lambda i: (part * steps + i, 0)),
            pl.BlockSpec((N_EXPERTS, d), const2),
            pl.BlockSpec((N_EXPERTS, d), const2),
            pl.BlockSpec((N_GROUPS, GROUP_SIZE, 1), lambda i: (0, 0, 0)),
            pl.BlockSpec((ROUTE_SUB, ROUTE_SUB), const2),
        ],
        out_specs=(kspec, kspec, kspec, pl.BlockSpec((N_EXPERTS, 1), const2)),
        out_shape=out_shape,
        scratch_shapes=[pltpu.VMEM((N_EXPERTS, 1), _F32)],
        compiler_params=pltpu.CompilerParams(
            dimension_semantics=("arbitrary",),
            vmem_limit_bytes=VMEM_LIMIT),
        name="moe_route",
    )(x, w_hi, w_lo, router_bias.reshape(N_GROUPS, GROUP_SIZE, 1), earlier)


def _sc_mesh():
    return plsc.VectorSubcoreMesh(core_axis_name="core", subcore_axis_name="subcore")


def _sc_scatter_rows(src, idx, n_out, part, n_parts):
    d = src.shape[1]
    n_part = idx.shape[1]
    wins = n_part // SC_WINDOW

    def body(src_hbm, idx_hbm, out_hbm):
        def step(src_vmem, idx_vmem):
            for k in range(TOP_K):
                pltpu.sync_copy(src_vmem, out_hbm.at[idx_vmem.at[k]])

        pltpu.emit_pipeline(
            step,
            grid=(PIECES * wins,),
            in_specs=[pl.BlockSpec((SC_WINDOW, d),
                                   index_map=lambda i: ((i // wins) * n_parts * wins
                                                        + part * wins + i % wins, 0)),
                      pl.BlockSpec((TOP_K, SC_WINDOW), index_map=lambda i: (i // wins, i % wins))],
            out_specs=[],
            core_axis_name=("core", "subcore"),
            dimension_semantics=(pltpu.PARALLEL,),
        )(src_hbm, idx_hbm)

    return pl.kernel(body, out_type=jax.ShapeDtypeStruct((n_out, d), src.dtype),
                     mesh=_sc_mesh(), scratch_types=[], name="sc_scatter_rows")(src, idx)


def _sc_gather_rows(table, idx):
    d = table.shape[1]
    n_idx_rows, n_tok = idx.shape
    wins = n_tok // SC_WINDOW

    def body(table_hbm, idx_hbm, out_hbm):
        def step(idx_vmem, out_vmem):
            pltpu.sync_copy(table_hbm.at[idx_vmem.at[0]], out_vmem)

        pltpu.emit_pipeline(
            step,
            grid=(n_idx_rows * wins,),
            in_specs=[pl.BlockSpec((1, SC_WINDOW), index_map=lambda i: (i // wins, i % wins))],
            out_specs=[pl.BlockSpec((SC_WINDOW, d), index_map=lambda i: (i, 0))],
            core_axis_name=("core", "subcore"),
            dimension_semantics=(pltpu.PARALLEL,),
        )(idx_hbm, out_hbm)

    return pl.kernel(body, out_type=jax.ShapeDtypeStruct((n_idx_rows * n_tok, d), table.dtype),
                     mesh=_sc_mesh(), scratch_types=[], name="sc_gather_rows")(table, idx)


def _index_kernel(start_ref, ek_ref, rk_ref, idx_ref, *, n_rows):
    e = ek_ref[...]
    start = jnp.zeros(e.shape, jnp.int32)
    for ex in range(N_EXPERTS):
        start = jnp.where(e == ex, start_ref[ex], start)
    dest = start + rk_ref[...]
    for c in range(PIECES):
        idx_ref[c] = dest + c * n_rows


def _row_indices(padded_start, e_k, r_k, n_rows):
    n_tok = e_k.shape[1]
    ts = INDEX_TILE
    kspec = pl.BlockSpec((TOP_K, ts), lambda i, st: (0, i))
    grid_spec = pltpu.PrefetchScalarGridSpec(
        num_scalar_prefetch=1,
        grid=(n_tok // ts,),
        in_specs=[kspec, kspec],
        out_specs=pl.BlockSpec((PIECES, TOP_K, ts), lambda i, st: (0, 0, i)),
    )
    return pl.pallas_call(
        functools.partial(_index_kernel, n_rows=n_rows),
        grid_spec=grid_spec,
        out_shape=jax.ShapeDtypeStruct((PIECES, TOP_K, n_tok), jnp.int32),
        compiler_params=pltpu.CompilerParams(dimension_semantics=("arbitrary",)),
        name="moe_row_indices",
    )(padded_start, e_k, r_k)


def _expert_kernel(chunk0_ref, nchunk_ref, cnt_ref, total_ref,
                   xs_hbm, wg_ref, wu_ref, wd_ref, o_hbm,
                   xbuf, obuf, wgu_bf, wd_bf, sem_in, sem_out):
    e = pl.program_id(0)
    total = total_ref[0]
    ch = EXPERT_ROWS
    ahead = IN_SLOTS - 1

    def in_copy(g):
        slot = g % IN_SLOTS
        return pltpu.make_async_copy(xs_hbm.at[:, pl.ds(g * ch, ch), :], xbuf.at[slot],
                                     sem_in.at[slot])

    def out_copy(g):
        slot = g % OUT_SLOTS
        return pltpu.make_async_copy(obuf.at[slot], o_hbm.at[:, pl.ds(g * ch, ch), :],
                                     sem_out.at[slot])

    @pl.when(e == 0)
    def _():
        obuf[...] = jnp.zeros_like(obuf)
        for g in range(ahead):
            @pl.when(g < total)
            def _():
                in_copy(g).start()

    wgu_bf[:, :EXPERT_DIM] = wg_ref[0, 0].astype(_BF16)
    wgu_bf[:, EXPERT_DIM:] = wu_ref[0, 0].astype(_BF16)
    wd_bf[...] = wd_ref[0, 0].astype(_BF16)

    def chunk(j, carry):
        g = chunk0_ref[e] + j
        in_copy(g).wait()

        @pl.when(g + ahead < total)
        def _():
            in_copy(g + ahead).start()

        @pl.when(g >= OUT_SLOTS)
        def _():
            out_copy(g - OUT_SLOTS).wait()

        islot = g % IN_SLOTS
        oslot = g % OUT_SLOTS
        valid = cnt_ref[e] - j * ch

        def gate_up(s):
            r0 = s * EXPERT_SUB
            w = jnp.concatenate([xbuf[islot, c, r0:r0 + EXPERT_SUB, :] for c in range(PIECES)],
                                axis=1)
            lo, hi = _unpack_halves(w)
            return (_dot(lo.astype(_BF16), wgu_bf[:PACKED]) +
                    _dot(hi.astype(_BF16), wgu_bf[PACKED:]))

        def down(s, gu):
            r0 = s * EXPERT_SUB
            h = _silu(gu[:, :EXPERT_DIM]) * gu[:, EXPERT_DIM:]
            y = _pack_halves(_dot(h.astype(_BF16), wd_bf[...]))
            for c in range(PIECES):
                obuf[oslot, c, r0:r0 + EXPERT_SUB, :] = y[:, c * LANES:(c + 1) * LANES]

        n_sub = ch // EXPERT_SUB

        @pl.when(valid >= ch)
        def _():
            projected = [gate_up(s) for s in range(n_sub)]
            for s, gu in enumerate(projected):
                down(s, gu)

        @pl.when(valid < ch)
        def _():
            for s in range(n_sub):
                @pl.when(s * EXPERT_SUB < valid)
                def _():
                    down(s, gate_up(s))

        out_copy(g).start()
        return carry

    lax.fori_loop(0, nchunk_ref[e], chunk, 0)

    @pl.when(e == pl.num_programs(0) - 1)
    def _():
        for back in range(OUT_SLOTS, 0, -1):
            @pl.when(total >= back)
            def _():
                out_copy(total - back).wait()


def _experts(layer, chunk0, nchunk, counts, total, xs, w_gate, w_up, w_down):
    d = D_MODEL
    ch = EXPERT_ROWS
    wspec_in = pl.BlockSpec((1, 1, d, EXPERT_DIM), lambda e, *_: (layer, e, 0, 0))
    any_spec = pl.BlockSpec(memory_space=pl.ANY)
    grid_spec = pltpu.PrefetchScalarGridSpec(
        num_scalar_prefetch=4,
        grid=(N_EXPERTS,),
        in_specs=[
            any_spec,
            wspec_in,
            wspec_in,
            pl.BlockSpec((1, 1, EXPERT_DIM, d), lambda e, *_: (layer, e, 0, 0)),
        ],
        out_specs=any_spec,
        scratch_shapes=[pltpu.VMEM((IN_SLOTS, PIECES, ch, LANES), _U32),
                        pltpu.VMEM((OUT_SLOTS, PIECES, ch, LANES), _U32),
                        pltpu.VMEM((d, 2 * EXPERT_DIM), _BF16),
                        pltpu.VMEM((EXPERT_DIM, d), _BF16),
                        pltpu.SemaphoreType.DMA((IN_SLOTS,)),
                        pltpu.SemaphoreType.DMA((OUT_SLOTS,))],
    )
    return pl.pallas_call(
        _expert_kernel,
        grid_spec=grid_spec,
        out_shape=jax.ShapeDtypeStruct(xs.shape, _U32),
        compiler_params=pltpu.CompilerParams(
            dimension_semantics=("arbitrary",),
            vmem_limit_bytes=VMEM_LIMIT),
        name="moe_experts",
    )(chunk0, nchunk, counts, total, xs, w_gate, w_up, w_down)


def _combine_kernel(x_ref, p_ref, gate_ref, sg_ref, su_ref, sd_ref, g_ref, b_ref, *rest):
    o_ref = rest[-1]
    x = x_ref[...]
    xb = x.astype(_BF16)
    gates = gate_ref[...].T
    r_lo = r_hi = None
    for k in range(TOP_K):
        w = jnp.concatenate([p_ref[c, k] for c in range(PIECES)], axis=1)
        lo, hi = _unpack_halves(w)
        gk = gates[:, k:k + 1]
        r_lo = gk * lo if r_lo is None else r_lo + gk * lo
        r_hi = gk * hi if r_hi is None else r_hi + gk * hi
    routed = jnp.concatenate([r_lo, r_hi], axis=1)
    h = _silu(_dot(xb, sg_ref[...])) * _dot(xb, su_ref[...])
    shared = _dot(h.astype(_BF16), sd_ref[...])
    o_ref[...] = _layer_norm(DEEPNORM_ALPHA * x + (routed + shared), g_ref[...], b_ref[...])


def _combine(x, picked, gates, sh_gate, sh_up, sh_down, ln_g, ln_b, part, n_parts, prev):
    n_tok, d = x.shape
    ts = COMBINE_TILE
    steps = n_tok // n_parts // ts
    const2 = lambda i: (0, 0)
    rows_map = lambda i: (part * steps + i, 0)
    in_specs = [
        pl.BlockSpec((ts, d), rows_map),
        pl.BlockSpec((PIECES, TOP_K, ts, LANES), lambda i: (0, 0, i, 0)),
        pl.BlockSpec((TOP_K, ts), lambda i: (0, i)),
        pl.BlockSpec((d, EXPERT_DIM), const2),
        pl.BlockSpec((d, EXPERT_DIM), const2),
        pl.BlockSpec((EXPERT_DIM, d), const2),
        pl.BlockSpec((1, d), const2),
        pl.BlockSpec((1, d), const2),
    ]
    args = [x, picked, gates, sh_gate.astype(_BF16), sh_up.astype(_BF16), sh_down.astype(_BF16),
            ln_g.reshape(1, d), ln_b.reshape(1, d)]
    aliases = {}
    if prev is not None:
        in_specs.append(pl.BlockSpec(memory_space=pl.ANY))
        aliases = {len(args): 0}
        args.append(prev)
    return pl.pallas_call(
        _combine_kernel,
        grid=(steps,),
        in_specs=in_specs,
        out_specs=pl.BlockSpec((ts, d), rows_map),
        out_shape=jax.ShapeDtypeStruct((n_tok, d), _F32),
        input_output_aliases=aliases,
        compiler_params=pltpu.CompilerParams(
            dimension_semantics=("arbitrary",),
            vmem_limit_bytes=VMEM_LIMIT),
        name="moe_combine",
    )(*args)


def _moe_layer(layer, x, x_packed, w_router, router_bias, w_gate, w_up, w_down,
               sh_gate, sh_up, sh_down, ln_g, ln_b):
    n_tok, d = x.shape
    bm = EXPERT_ROWS
    n_part = n_tok // TOKEN_PARTS
    n_rows = n_part * TOP_K + N_EXPERTS * bm
    x_flat = x_packed.reshape(PIECES * n_tok, LANES)

    routed = []
    for part in range(TOKEN_PARTS):
        e_k, r_k, g_k, counts = _route(x, w_router, router_bias, part, TOKEN_PARTS)
        counts = counts.reshape(N_EXPERTS).astype(jnp.int32)
        padded = (counts + bm - 1) // bm * bm
        padded_end = jnp.cumsum(padded)
        padded_start = padded_end - padded
        idx = _row_indices(padded_start, e_k, r_k, n_rows).reshape(PIECES * TOP_K, n_part)
        routed.append((idx, g_k, padded_start // bm, padded // bm, counts,
                       padded_end[-1:] // bm))

    rows = []
    for part, (idx, _, chunk0, nchunk, counts, total) in enumerate(routed):
        xs = _sc_scatter_rows(x_flat, idx, PIECES * n_rows, part, TOKEN_PARTS)
        rows.append(_experts(layer, chunk0, nchunk, counts, total,
                             xs.reshape(PIECES, n_rows, LANES), w_gate, w_up, w_down))

    out = None
    for part, (idx, gates, *_) in enumerate(routed):
        picked = _sc_gather_rows(rows[part].reshape(PIECES * n_rows, LANES), idx)
        picked = picked.reshape(PIECES, TOP_K, n_part, LANES)
        out = _combine(x, picked, gates, sh_gate, sh_up, sh_down, ln_g, ln_b,
                       part, TOKEN_PARTS, out)
    return out


def kernel(x, pool_w_in, pool_w_grp, pool_scale, pool_w_out, sgu_w_in, sgu_b_in, sgu_ln_g, sgu_ln_b, sgu_w_s, sgu_b_s, sgu_w_out, ln_mix_g, ln_mix_b, moe_w_router, moe_router_bias, moe_w_gate, moe_w_up, moe_w_down, moe_sh_gate, moe_sh_up, moe_sh_down, ln_ffn_g, ln_ffn_b):
    bsz, seq, d = x.shape
    n_tok = bsz * seq

    def moe(i, h, h_packed):
        return _moe_layer(i, h, h_packed, moe_w_router[i], moe_router_bias[i], moe_w_gate,
                          moe_w_up, moe_w_down, moe_sh_gate[i], moe_sh_up[i],
                          moe_sh_down[i], ln_ffn_g[i], ln_ffn_b[i])

    h, h_packed = _pool_layer(x, pool_w_in[0], pool_w_grp[0], pool_scale[0], pool_w_out[0],
                              ln_mix_g[0], ln_mix_b[0])
    h = moe(0, h.reshape(n_tok, d), h_packed)
    h, h_packed = _sgu_layer(h, sgu_w_in[0], sgu_b_in[0], sgu_ln_g[0], sgu_ln_b[0], sgu_w_s[0],
                             sgu_b_s[0], sgu_w_out[0], ln_mix_g[1], ln_mix_b[1])
    h = moe(1, h, h_packed)
    return h.reshape(bsz, seq, d)
```

```python
import functools

import jax
import jax.numpy as jnp
from jax import lax
from jax.experimental import pallas as pl
from jax.experimental.pallas import tpu as pltpu
from jax.experimental.pallas import tpu_sc as plsc

D_MODEL = 1024
DEPTH = 2
POOL_WINDOWS = (2, 4, 8, 16)
POOL_GROUP_DIM = D_MODEL // len(POOL_WINDOWS)
POOL_HALO = 16
SGU_CHUNK = 128
SGU_HEADS = 4
SGU_WIDTH = 2 * D_MODEL
SGU_HEAD_DIM = SGU_WIDTH // SGU_HEADS
N_EXPERTS = 64
TOP_K = 8
N_GROUPS = 8
GROUP_SIZE = N_EXPERTS // N_GROUPS
TOPK_GROUPS = 4
EXPERT_DIM = D_MODEL // 4
ROUTED_SCALE = 2.5
DEEPNORM_ALPHA = (2 * DEPTH) ** 0.25
LN_EPS = 1e-5

LANES = 128
PACKED = D_MODEL // 2
PIECES = PACKED // LANES

POOL_TILE = 1024
POOL_SUB = 256
SGU_TILE = 512
SGU_SUB = 256
ROUTE_TILE = 1024
ROUTE_SUB = 512
INDEX_TILE = 2048
EXPERT_ROWS = 512
EXPERT_SUB = 256
IN_SLOTS = 6
OUT_SLOTS = 4
COMBINE_TILE = 512
SC_WINDOW = 128
TOKEN_PARTS = 2
VMEM_LIMIT = 56 * 1024 * 1024

_F32 = jnp.float32
_BF16 = jnp.bfloat16
_U32 = jnp.uint32


def _dot(a, b):
    return jnp.dot(a, b, preferred_element_type=_F32)


def _layer_norm(h, g, b):
    mu = jnp.mean(h, axis=-1, keepdims=True)
    hc = h - mu
    var = jnp.mean(hc * hc, axis=-1, keepdims=True)
    return hc * lax.rsqrt(var + LN_EPS) * g + b


def _silu(x):
    return x * jax.nn.sigmoid(x)


def _gelu_tanh(x):
    c = 0.7978845608028654
    return 0.5 * x * (1.0 + jnp.tanh(c * (x + 0.044715 * (x * x * x))))


def _pack_halves(v):
    half = v.shape[1] // 2
    lo = lax.bitcast_convert_type(v[:, :half].astype(_BF16).astype(_F32), _U32)
    hi = lax.bitcast_convert_type(v[:, half:].astype(_BF16).astype(_F32), _U32)
    return (hi & _U32(0xFFFF0000)) | (lo >> 16)


def _unpack_halves(w):
    lo = lax.bitcast_convert_type(w << 16, _F32)
    hi = lax.bitcast_convert_type(w & _U32(0xFFFF0000), _F32)
    return lo, hi


def _store_pieces(ref, r0, w):
    for c in range(PIECES):
        ref[c, r0:r0 + w.shape[0], :] = w[:, c * LANES:(c + 1) * LANES]


def _pool_kernel(x_ref, win_ref, wgrp_ref, scale_ref, wout_ref, g_ref, b_ref,
                 o_ref, op_ref, zs_ref, y_ref):
    s = pl.program_id(1)
    ts = x_ref.shape[1]

    @pl.when(s == 0)
    def _():
        zs_ref[0:POOL_HALO, :] = jnp.zeros((POOL_HALO, D_MODEL), _F32)

    for p0 in range(0, ts, POOL_SUB):
        z = _dot(x_ref[0, p0:p0 + POOL_SUB, :].astype(_BF16), win_ref[...])
        zs_ref[POOL_HALO + p0:POOL_HALO + p0 + POOL_SUB, :] = z
    for p0 in range(0, ts, POOL_SUB):
        x = x_ref[0, p0:p0 + POOL_SUB, :]
        base = POOL_HALO + p0
        pos = s * ts + p0 + lax.broadcasted_iota(jnp.int32, (POOL_SUB, 1), 0)
        for g, w in enumerate(POOL_WINDOWS):
            c0 = g * POOL_GROUP_DIM
            c1 = c0 + POOL_GROUP_DIM
            zg = zs_ref[base:base + POOL_SUB, c0:c1]
            acc = zg
            for k in range(1, w):
                acc = acc + zs_ref[base - k:base - k + POOL_SUB, c0:c1]
            cnt = jnp.minimum(pos + 1, w).astype(_F32)
            pooled = acc / cnt - zg
            yg = _dot(pooled.astype(_BF16), wgrp_ref[g]) * scale_ref[:, c0:c1]
            y_ref[p0:p0 + POOL_SUB, c0:c1] = yg.astype(_BF16)
        mix = _dot(y_ref[p0:p0 + POOL_SUB, :], wout_ref[...])
        out = _layer_norm(DEEPNORM_ALPHA * x + mix, g_ref[...], b_ref[...])
        o_ref[0, p0:p0 + POOL_SUB, :] = out
        _store_pieces(op_ref, p0, _pack_halves(out))
    zs_ref[0:POOL_HALO, :] = zs_ref[ts:ts + POOL_HALO, :]


def _pool_layer(x, part, w_in, w_grp, scale, w_out, ln_g, ln_b):
    bsz, seq, d = x.shape
    rows = bsz // TOKEN_PARTS
    ts = POOL_TILE
    steps = seq // ts
    const2 = lambda b, s: (0, 0)
    out_shape = (jax.ShapeDtypeStruct((rows, seq, d), _F32),
                 jax.ShapeDtypeStruct((PIECES, rows * seq, LANES), _U32))
    tile = pl.BlockSpec((1, ts, d), lambda b, s: (b, s, 0))
    ptile = pl.BlockSpec((PIECES, ts, LANES), lambda b, s: (0, b * steps + s, 0))
    return pl.pallas_call(
        _pool_kernel,
        grid=(rows, steps),
        in_specs=[
            pl.BlockSpec((1, ts, d), lambda b, s: (part * rows + b, s, 0)),
            pl.BlockSpec((d, d), const2),
            pl.BlockSpec((len(POOL_WINDOWS), POOL_GROUP_DIM, POOL_GROUP_DIM), lambda b, s: (0, 0, 0)),
            pl.BlockSpec((1, d), const2),
            pl.BlockSpec((d, d), const2),
            pl.BlockSpec((1, d), const2),
            pl.BlockSpec((1, d), const2),
        ],
        out_specs=(tile, ptile),
        out_shape=out_shape,
        scratch_shapes=[pltpu.VMEM((POOL_HALO + ts, d), _F32),
                        pltpu.VMEM((ts, d), _BF16)],
        compiler_params=pltpu.CompilerParams(
            dimension_semantics=("arbitrary", "arbitrary"),
            vmem_limit_bytes=VMEM_LIMIT),
        name="pool_mixer",
    )(x, w_in.astype(_BF16), w_grp.astype(_BF16), scale.reshape(1, d),
      w_out.astype(_BF16), ln_g.reshape(1, d), ln_b.reshape(1, d))


def _sgu_kernel(x_ref, win_ref, bin_ref, lng_ref, lnb_ref, ws_ref, bs_ref, wout_ref,
                g_ref, b_ref, o_ref, op_ref, gated_ref):
    ts = x_ref.shape[0]
    projected = []
    for p0 in range(0, ts, SGU_SUB):
        x = x_ref[p0:p0 + SGU_SUB, :]
        xb = x.astype(_BF16)
        v = _gelu_tanh(_dot(xb, win_ref[:, SGU_WIDTH:]) + bin_ref[:, SGU_WIDTH:])
        v = _layer_norm(v, lng_ref[...], lnb_ref[...]).astype(_BF16)
        u = _gelu_tanh(_dot(xb, win_ref[:, :SGU_WIDTH]) + bin_ref[:, :SGU_WIDTH])
        projected.append((p0, x, u, v))
    for p0, x, u, v in projected:
        for r0 in range(0, SGU_SUB, SGU_CHUNK):
            for h in range(SGU_HEADS):
                c0 = h * SGU_HEAD_DIM
                sv = _dot(ws_ref[h], v[r0:r0 + SGU_CHUNK, c0:c0 + SGU_HEAD_DIM]) + bs_ref[h]
                gated_ref[p0 + r0:p0 + r0 + SGU_CHUNK, c0:c0 + SGU_HEAD_DIM] = (
                    u[r0:r0 + SGU_CHUNK, c0:c0 + SGU_HEAD_DIM] * sv).astype(_BF16)
        mix = _dot(gated_ref[p0:p0 + SGU_SUB, :], wout_ref[...])
        out = _layer_norm(DEEPNORM_ALPHA * x + mix, g_ref[...], b_ref[...])
        o_ref[p0:p0 + SGU_SUB, :] = out
        _store_pieces(op_ref, p0, _pack_halves(out))


def _sgu_layer(x, w_in, b_in, ln_g, ln_b, w_s, b_s, w_out, mix_g, mix_b):
    n_tok, d = x.shape
    ts = SGU_TILE
    const2 = lambda i: (0, 0)
    const3 = lambda i: (0, 0, 0)
    causal = jnp.tril(jnp.ones((SGU_CHUNK, SGU_CHUNK), w_s.dtype))
    ws = (w_s * causal[None]).astype(_BF16)
    tile = pl.BlockSpec((ts, d), lambda i: (i, 0))
    ptile = pl.BlockSpec((PIECES, ts, LANES), lambda i: (0, i, 0))
    out_shape = (jax.ShapeDtypeStruct((n_tok, d), _F32),
                 jax.ShapeDtypeStruct((PIECES, n_tok, LANES), _U32))
    return pl.pallas_call(
        _sgu_kernel,
        grid=(n_tok // ts,),
        in_specs=[
            tile,
            pl.BlockSpec((d, 2 * SGU_WIDTH), const2),
            pl.BlockSpec((1, 2 * SGU_WIDTH), const2),
            pl.BlockSpec((1, SGU_WIDTH), const2),
            pl.BlockSpec((1, SGU_WIDTH), const2),
            pl.BlockSpec((SGU_HEADS, SGU_CHUNK, SGU_CHUNK), const3),
            pl.BlockSpec((SGU_HEADS, SGU_CHUNK, 1), const3),
            pl.BlockSpec((SGU_WIDTH, d), const2),
            pl.BlockSpec((1, d), const2),
            pl.BlockSpec((1, d), const2),
        ],
        out_specs=(tile, ptile),
        out_shape=out_shape,
        scratch_shapes=[pltpu.VMEM((ts, SGU_WIDTH), _BF16)],
        compiler_params=pltpu.CompilerParams(
            dimension_semantics=("arbitrary",),
            vmem_limit_bytes=VMEM_LIMIT),
        name="sgu_mixer",
    )(x, w_in.astype(_BF16), b_in.reshape(1, -1), ln_g.reshape(1, -1), ln_b.reshape(1, -1),
      ws, b_s.reshape(SGU_HEADS, SGU_CHUNK, 1), w_out.astype(_BF16),
      mix_g.reshape(1, d), mix_b.reshape(1, d))


def _all_max(a):
    return jnp.max(jnp.max(a, axis=0, keepdims=True), axis=1, keepdims=True)


def _all_min(a):
    return jnp.min(jnp.min(a, axis=0, keepdims=True), axis=1, keepdims=True)


def _all_sum(a):
    return jnp.sum(jnp.sum(a, axis=0, keepdims=True), axis=1, keepdims=True)


def _route_kernel(x_ref, whi_ref, wlo_ref, bias_ref, earlier_ref,
                  ek_ref, rk_ref, gk_ref, cnt_ref, carry_ref):
    @pl.when(pl.program_id(0) == 0)
    def _():
        carry_ref[...] = jnp.zeros_like(carry_ref)

    nt = lambda a, b: lax.dot_general(a, b, (((1,), (1,)), ((), ())),
                                      preferred_element_type=_F32)
    logits = []
    for p0 in range(0, x_ref.shape[0], ROUTE_SUB):
        x = x_ref[p0:p0 + ROUTE_SUB, :]
        x_hi = x.astype(_BF16)
        x_lo = (x - x_hi.astype(_F32)).astype(_BF16)
        logits.append(nt(whi_ref[...], x_hi) + (nt(whi_ref[...], x_lo) + nt(wlo_ref[...], x_hi)))
    for n, sub_logits in enumerate(logits):
        _route_sub_tile(sub_logits, n * ROUTE_SUB, bias_ref, earlier_ref,
                        ek_ref, rk_ref, gk_ref, carry_ref)
    cnt_ref[...] = carry_ref[...]


def _route_sub_tile(logits, p0, bias_ref, earlier_ref, ek_ref, rk_ref, gk_ref, carry_ref):
    ts = ROUTE_SUB
    scores = jax.nn.sigmoid(logits).reshape(N_GROUPS, GROUP_SIZE, ts)
    biased = scores + bias_ref[...]
    neg_inf = jnp.float32(-jnp.inf)
    shape3 = (N_GROUPS, GROUP_SIZE, ts)
    in_grp = lax.broadcasted_iota(jnp.int32, shape3, 1)
    grp = lax.broadcasted_iota(jnp.int32, shape3, 0)
    eid = grp * GROUP_SIZE + in_grp

    m1 = jnp.max(biased, axis=1, keepdims=True)
    first1 = jnp.min(jnp.where(biased == m1, in_grp, GROUP_SIZE), axis=1, keepdims=True)
    m2 = jnp.max(jnp.where(in_grp == first1, neg_inf, biased), axis=1, keepdims=True)
    gscore = m1 + m2

    gid = lax.broadcasted_iota(jnp.int32, (N_GROUPS, 1, ts), 0)
    gsel = jnp.zeros((N_GROUPS, 1, ts), jnp.bool_)
    for _ in range(TOPK_GROUPS):
        m = jnp.max(gscore, axis=0, keepdims=True)
        first = jnp.min(jnp.where(gscore == m, gid, N_GROUPS), axis=0, keepdims=True)
        pick = gid == first
        gsel = jnp.logical_or(gsel, pick)
        gscore = jnp.where(pick, neg_inf, gscore)

    masked = jnp.where(gsel, biased, neg_inf)
    picked_any = jnp.zeros(shape3, jnp.bool_)
    e_k, s_k = [], []
    for _ in range(TOP_K):
        m = _all_max(masked)
        first = _all_min(jnp.where(masked == m, eid, N_EXPERTS))
        pick = eid == first
        picked_any = jnp.logical_or(picked_any, pick)
        masked = jnp.where(pick, neg_inf, masked)
        e_k.append(first)
        s_k.append(_all_sum(jnp.where(pick, scores, 0.0)))
    denom = s_k[0]
    for k in range(1, TOP_K):
        denom = denom + s_k[k]

    sel = picked_any.astype(_F32).reshape(N_EXPERTS, ts)
    rank = _dot(sel.astype(_BF16), earlier_ref[...]) + carry_ref[...]
    rank3 = rank.reshape(shape3)
    carry_ref[...] += jnp.sum(sel, axis=1, keepdims=True)

    for k in range(TOP_K):
        r = _all_sum(jnp.where(eid == e_k[k], rank3, 0.0))
        ek_ref[k:k + 1, p0:p0 + ts] = e_k[k].reshape(1, ts)
        rk_ref[k:k + 1, p0:p0 + ts] = r.reshape(1, ts).astype(jnp.int32)
        gk_ref[k:k + 1, p0:p0 + ts] = (s_k[k] / denom * ROUTED_SCALE).reshape(1, ts)


def _route(x, w_router, router_bias):
    n_part, d = x.shape
    ts = ROUTE_TILE
    steps = n_part // ts
    out_shape = (jax.ShapeDtypeStruct((TOP_K, n_part), jnp.int32),
                 jax.ShapeDtypeStruct((TOP_K, n_part), jnp.int32),
                 jax.ShapeDtypeStruct((TOP_K, n_part), _F32),
                 jax.ShapeDtypeStruct((N_EXPERTS, 1), _F32))
    kspec = pl.BlockSpec((TOP_K, ts), lambda i: (0, i))
    const2 = lambda i: (0, 0)
    w_t = w_router.T
    w_hi = w_t.astype(_BF16)
    w_lo = (w_t - w_hi.astype(_F32)).astype(_BF16)
    pos = jnp.arange(ROUTE_SUB, dtype=jnp.int32)
    earlier = (pos[:, None] < pos[None, :]).astype(_BF16)
    return pl.pallas_call(
        _route_kernel,
        grid=(steps,),
        in_specs=[
            pl.BlockSpec((ts, d), lambda i: (i, 0)),
            pl.BlockSpec((N_EXPERTS, d), const2),
            pl.BlockSpec((N_EXPERTS, d), const2),
            pl.BlockSpec((N_GROUPS, GROUP_SIZE, 1), lambda i: (0, 0, 0)),
            pl.BlockSpec((ROUTE_SUB, ROUTE_SUB), const2),
        ],
        out_specs=(kspec, kspec, kspec, pl.BlockSpec((N_EXPERTS, 1), const2)),
        out_shape=out_shape,
        scratch_shapes=[pltpu.VMEM((N_EXPERTS, 1), _F32)],
        compiler_params=pltpu.CompilerParams(
            dimension_semantics=("arbitrary",),
            vmem_limit_bytes=VMEM_LIMIT),
        name="moe_route",
    )(x, w_hi, w_lo, router_bias.reshape(N_GROUPS, GROUP_SIZE, 1), earlier)


def _sc_mesh():
    return plsc.VectorSubcoreMesh(core_axis_name="core", subcore_axis_name="subcore")


def _sc_scatter_rows(src, idx, n_out):
    d = src.shape[1]
    wins = idx.shape[1] // SC_WINDOW

    def body(src_hbm, idx_hbm, out_hbm):
        def step(src_vmem, idx_vmem):
            for k in range(TOP_K):
                pltpu.sync_copy(src_vmem, out_hbm.at[idx_vmem.at[k]])

        pltpu.emit_pipeline(
            step,
            grid=(src.shape[0] // SC_WINDOW,),
            in_specs=[pl.BlockSpec((SC_WINDOW, d), index_map=lambda i: (i, 0)),
                      pl.BlockSpec((TOP_K, SC_WINDOW), index_map=lambda i: (i // wins, i % wins))],
            out_specs=[],
            core_axis_name=("core", "subcore"),
            dimension_semantics=(pltpu.PARALLEL,),
        )(src_hbm, idx_hbm)

    return pl.kernel(body, out_type=jax.ShapeDtypeStruct((n_out, d), src.dtype),
                     mesh=_sc_mesh(), scratch_types=[], name="sc_scatter_rows")(src, idx)


def _sc_gather_rows(table, idx):
    d = table.shape[1]
    n_idx_rows, n_tok = idx.shape
    wins = n_tok // SC_WINDOW

    def body(table_hbm, idx_hbm, out_hbm):
        def step(idx_vmem, out_vmem):
            pltpu.sync_copy(table_hbm.at[idx_vmem.at[0]], out_vmem)

        pltpu.emit_pipeline(
            step,
            grid=(n_idx_rows * wins,),
            in_specs=[pl.BlockSpec((1, SC_WINDOW), index_map=lambda i: (i // wins, i % wins))],
            out_specs=[pl.BlockSpec((SC_WINDOW, d), index_map=lambda i: (i, 0))],
            core_axis_name=("core", "subcore"),
            dimension_semantics=(pltpu.PARALLEL,),
        )(idx_hbm, out_hbm)

    return pl.kernel(body, out_type=jax.ShapeDtypeStruct((n_idx_rows * n_tok, d), table.dtype),
                     mesh=_sc_mesh(), scratch_types=[], name="sc_gather_rows")(table, idx)


def _index_kernel(start_ref, ek_ref, rk_ref, idx_ref, *, n_rows):
    e = ek_ref[...]
    start = jnp.zeros(e.shape, jnp.int32)
    for ex in range(N_EXPERTS):
        start = jnp.where(e == ex, start_ref[ex], start)
    dest = start + rk_ref[...]
    for c in range(PIECES):
        idx_ref[c] = dest + c * n_rows


def _row_indices(padded_start, e_k, r_k, n_rows):
    n_tok = e_k.shape[1]
    ts = INDEX_TILE
    kspec = pl.BlockSpec((TOP_K, ts), lambda i, st: (0, i))
    grid_spec = pltpu.PrefetchScalarGridSpec(
        num_scalar_prefetch=1,
        grid=(n_tok // ts,),
        in_specs=[kspec, kspec],
        out_specs=pl.BlockSpec((PIECES, TOP_K, ts), lambda i, st: (0, 0, i)),
    )
    return pl.pallas_call(
        functools.partial(_index_kernel, n_rows=n_rows),
        grid_spec=grid_spec,
        out_shape=jax.ShapeDtypeStruct((PIECES, TOP_K, n_tok), jnp.int32),
        compiler_params=pltpu.CompilerParams(dimension_semantics=("arbitrary",)),
        name="moe_row_indices",
    )(padded_start, e_k, r_k)


def _expert_kernel(chunk0_ref, nchunk_ref, cnt_ref, total_ref,
                   xs_hbm, wg_ref, wu_ref, wd_ref, o_hbm,
                   xbuf, obuf, wgu_bf, wd_bf, sem_in, sem_out):
    e = pl.program_id(0)
    total = total_ref[0]
    ch = EXPERT_ROWS
    ahead = IN_SLOTS - 1

    def in_copy(g):
        slot = g % IN_SLOTS
        return pltpu.make_async_copy(xs_hbm.at[:, pl.ds(g * ch, ch), :], xbuf.at[slot],
                                     sem_in.at[slot])

    def out_copy(g):
        slot = g % OUT_SLOTS
        return pltpu.make_async_copy(obuf.at[slot], o_hbm.at[:, pl.ds(g * ch, ch), :],
                                     sem_out.at[slot])

    @pl.when(e == 0)
    def _():
        obuf[...] = jnp.zeros_like(obuf)
        for g in range(ahead):
            @pl.when(g < total)
            def _():
                in_copy(g).start()

    wgu_bf[:, :EXPERT_DIM] = wg_ref[0, 0].astype(_BF16)
    wgu_bf[:, EXPERT_DIM:] = wu_ref[0, 0].astype(_BF16)
    wd_bf[...] = wd_ref[0, 0].astype(_BF16)

    def chunk(j, carry):
        g = chunk0_ref[e] + j
        in_copy(g).wait()

        @pl.when(g + ahead < total)
        def _():
            in_copy(g + ahead).start()

        @pl.when(g >= OUT_SLOTS)
        def _():
            out_copy(g - OUT_SLOTS).wait()

        islot = g % IN_SLOTS
        oslot = g % OUT_SLOTS
        valid = cnt_ref[e] - j * ch

        def gate_up(s):
            r0 = s * EXPERT_SUB
            w = jnp.concatenate([xbuf[islot, c, r0:r0 + EXPERT_SUB, :] for c in range(PIECES)],
                                axis=1)
            lo, hi = _unpack_halves(w)
            return (_dot(lo.astype(_BF16), wgu_bf[:PACKED]) +
                    _dot(hi.astype(_BF16), wgu_bf[PACKED:]))

        def down(s, gu):
            r0 = s * EXPERT_SUB
            h = _silu(gu[:, :EXPERT_DIM]) * gu[:, EXPERT_DIM:]
            y = _pack_halves(_dot(h.astype(_BF16), wd_bf[...]))
            for c in range(PIECES):
                obuf[oslot, c, r0:r0 + EXPERT_SUB, :] = y[:, c * LANES:(c + 1) * LANES]

        n_sub = ch // EXPERT_SUB

        @pl.when(valid >= ch)
        def _():
            projected = [gate_up(s) for s in range(n_sub)]
            for s, gu in enumerate(projected):
                down(s, gu)

        @pl.when(valid < ch)
        def _():
            for s in range(n_sub):
                @pl.when(s * EXPERT_SUB < valid)
                def _():
                    down(s, gate_up(s))

        out_copy(g).start()
        return carry

    lax.fori_loop(0, nchunk_ref[e], chunk, 0)

    @pl.when(e == pl.num_programs(0) - 1)
    def _():
        for back in range(OUT_SLOTS, 0, -1):
            @pl.when(total >= back)
            def _():
                out_copy(total - back).wait()


def _experts(layer, chunk0, nchunk, counts, total, xs, w_gate, w_up, w_down):
    d = D_MODEL
    ch = EXPERT_ROWS
    wspec_in = pl.BlockSpec((1, 1, d, EXPERT_DIM), lambda e, *_: (layer, e, 0, 0))
    any_spec = pl.BlockSpec(memory_space=pl.ANY)
    grid_spec = pltpu.PrefetchScalarGridSpec(
        num_scalar_prefetch=4,
        grid=(N_EXPERTS,),
        in_specs=[
            any_spec,
            wspec_in,
            wspec_in,
            pl.BlockSpec((1, 1, EXPERT_DIM, d), lambda e, *_: (layer, e, 0, 0)),
        ],
        out_specs=any_spec,
        scratch_shapes=[pltpu.VMEM((IN_SLOTS, PIECES, ch, LANES), _U32),
                        pltpu.VMEM((OUT_SLOTS, PIECES, ch, LANES), _U32),
                        pltpu.VMEM((d, 2 * EXPERT_DIM), _BF16),
                        pltpu.VMEM((EXPERT_DIM, d), _BF16),
                        pltpu.SemaphoreType.DMA((IN_SLOTS,)),
                        pltpu.SemaphoreType.DMA((OUT_SLOTS,))],
    )
    return pl.pallas_call(
        _expert_kernel,
        grid_spec=grid_spec,
        out_shape=jax.ShapeDtypeStruct(xs.shape, _U32),
        compiler_params=pltpu.CompilerParams(
            dimension_semantics=("arbitrary",),
            vmem_limit_bytes=VMEM_LIMIT),
        name="moe_experts",
    )(chunk0, nchunk, counts, total, xs, w_gate, w_up, w_down)


def _combine_kernel(x_ref, p_ref, gate_ref, sg_ref, su_ref, sd_ref, g_ref, b_ref, *rest):
    o_ref = rest[-1]
    x = x_ref[...]
    xb = x.astype(_BF16)
    gates = gate_ref[...].T
    r_lo = r_hi = None
    for k in range(TOP_K):
        w = jnp.concatenate([p_ref[c, k] for c in range(PIECES)], axis=1)
        lo, hi = _unpack_halves(w)
        gk = gates[:, k:k + 1]
        r_lo = gk * lo if r_lo is None else r_lo + gk * lo
        r_hi = gk * hi if r_hi is None else r_hi + gk * hi
    routed = jnp.concatenate([r_lo, r_hi], axis=1)
    h = _silu(_dot(xb, sg_ref[...])) * _dot(xb, su_ref[...])
    shared = _dot(h.astype(_BF16), sd_ref[...])
    o_ref[...] = _layer_norm(DEEPNORM_ALPHA * x + (routed + shared), g_ref[...], b_ref[...])


def _combine(x, picked, gates, sh_gate, sh_up, sh_down, ln_g, ln_b, out_parts, part, prev):
    n_part, d = x.shape
    ts = COMBINE_TILE
    steps = n_part // ts
    const2 = lambda i: (0, 0)
    in_specs = [
        pl.BlockSpec((ts, d), lambda i: (i, 0)),
        pl.BlockSpec((PIECES, TOP_K, ts, LANES), lambda i: (0, 0, i, 0)),
        pl.BlockSpec((TOP_K, ts), lambda i: (0, i)),
        pl.BlockSpec((d, EXPERT_DIM), const2),
        pl.BlockSpec((d, EXPERT_DIM), const2),
        pl.BlockSpec((EXPERT_DIM, d), const2),
        pl.BlockSpec((1, d), const2),
        pl.BlockSpec((1, d), const2),
    ]
    args = [x, picked, gates, sh_gate.astype(_BF16), sh_up.astype(_BF16), sh_down.astype(_BF16),
            ln_g.reshape(1, d), ln_b.reshape(1, d)]
    aliases = {}
    if prev is not None:
        in_specs.append(pl.BlockSpec(memory_space=pl.ANY))
        aliases = {len(args): 0}
        args.append(prev)
    return pl.pallas_call(
        _combine_kernel,
        grid=(steps,),
        in_specs=in_specs,
        out_specs=pl.BlockSpec((ts, d), lambda i: (part * steps + i, 0)),
        out_shape=jax.ShapeDtypeStruct((out_parts * n_part, d), _F32),
        input_output_aliases=aliases,
        compiler_params=pltpu.CompilerParams(
            dimension_semantics=("arbitrary",),
            vmem_limit_bytes=VMEM_LIMIT),
        name="moe_combine",
    )(*args)


def _moe_dispatch(layer, x, x_packed, w_router, router_bias, w_gate, w_up, w_down):
    n_part, d = x.shape
    bm = EXPERT_ROWS
    n_rows = n_part * TOP_K + N_EXPERTS * bm
    e_k, r_k, g_k, counts = _route(x, w_router, router_bias)
    counts = counts.reshape(N_EXPERTS).astype(jnp.int32)
    padded = (counts + bm - 1) // bm * bm
    padded_end = jnp.cumsum(padded)
    padded_start = padded_end - padded
    idx = _row_indices(padded_start, e_k, r_k, n_rows).reshape(PIECES * TOP_K, n_part)
    xs = _sc_scatter_rows(x_packed.reshape(PIECES * n_part, LANES), idx, PIECES * n_rows)
    rows = _experts(layer, padded_start // bm, padded // bm, counts, padded_end[-1:] // bm,
                    xs.reshape(PIECES, n_rows, LANES), w_gate, w_up, w_down)
    return rows.reshape(PIECES * n_rows, LANES), idx, g_k


def _moe_combine(x, rows, idx, gates, sh_gate, sh_up, sh_down, ln_g, ln_b,
                 out_parts=1, part=0, prev=None):
    picked = _sc_gather_rows(rows, idx).reshape(PIECES, TOP_K, x.shape[0], LANES)
    return _combine(x, picked, gates, sh_gate, sh_up, sh_down, ln_g, ln_b, out_parts, part, prev)


def kernel(x, pool_w_in, pool_w_grp, pool_scale, pool_w_out, sgu_w_in, sgu_b_in, sgu_ln_g, sgu_ln_b, sgu_w_s, sgu_b_s, sgu_w_out, ln_mix_g, ln_mix_b, moe_w_router, moe_router_bias, moe_w_gate, moe_w_up, moe_w_down, moe_sh_gate, moe_sh_up, moe_sh_down, ln_ffn_g, ln_ffn_b):
    bsz, seq, d = x.shape
    parts = range(TOKEN_PARTS)

    def dispatch(i, h, h_packed):
        return _moe_dispatch(i, h, h_packed, moe_w_router[i], moe_router_bias[i],
                             moe_w_gate, moe_w_up, moe_w_down)

    def combine(i, h, routed, **where):
        return _moe_combine(h, *routed, moe_sh_gate[i], moe_sh_up[i], moe_sh_down[i],
                            ln_ffn_g[i], ln_ffn_b[i], **where)

    mixed = [_pool_layer(x, p, pool_w_in[0], pool_w_grp[0], pool_scale[0], pool_w_out[0],
                         ln_mix_g[0], ln_mix_b[0]) for p in parts]
    mixed = [(h.reshape(-1, d), h_packed) for h, h_packed in mixed]
    routed = [dispatch(0, h, h_packed) for h, h_packed in mixed]
    hs = [combine(0, mixed[p][0], routed[p]) for p in parts]
    mixed = [_sgu_layer(h, sgu_w_in[0], sgu_b_in[0], sgu_ln_g[0], sgu_ln_b[0], sgu_w_s[0],
                        sgu_b_s[0], sgu_w_out[0], ln_mix_g[1], ln_mix_b[1]) for h in hs]
    routed = [dispatch(1, h, h_packed) for h, h_packed in mixed]
    out = None
    for p in parts:
        out = combine(1, mixed[p][0], routed[p], out_parts=TOKEN_PARTS, part=p, prev=out)
    return out.reshape(bsz, seq, d)
```

```python
import functools

import jax
import jax.numpy as jnp
from jax import lax
from jax.experimental import pallas as pl
from jax.experimental.pallas import tpu as pltpu
from jax.experimental.pallas import tpu_sc as plsc

D_MODEL = 1024
DEPTH = 2
POOL_WINDOWS = (2, 4, 8, 16)
POOL_GROUP_DIM = D_MODEL // len(POOL_WINDOWS)
POOL_HALO = 16
SGU_CHUNK = 128
SGU_HEADS = 4
SGU_WIDTH = 2 * D_MODEL
SGU_HEAD_DIM = SGU_WIDTH // SGU_HEADS
N_EXPERTS = 64
TOP_K = 8
N_GROUPS = 8
GROUP_SIZE = N_EXPERTS // N_GROUPS
TOPK_GROUPS = 4
EXPERT_DIM = D_MODEL // 4
ROUTED_SCALE = 2.5
DEEPNORM_ALPHA = (2 * DEPTH) ** 0.25
LN_EPS = 1e-5

LANES = 128
PACKED = D_MODEL // 2
PIECES = PACKED // LANES

POOL_TILE = 1024
POOL_SUB = 256
SGU_TILE = 512
SGU_SUB = 256
ROUTE_TILE = 1024
ROUTE_SUB = 512
INDEX_TILE = 2048
EXPERT_ROWS = 512
EXPERT_SUB = 256
IN_SLOTS = 6
OUT_SLOTS = 4
COMBINE_TILE = 512
SC_WINDOW = 128
TOKEN_PARTS = 4
VMEM_LIMIT = 56 * 1024 * 1024

_F32 = jnp.float32
_BF16 = jnp.bfloat16
_U32 = jnp.uint32


def _dot(a, b):
    return jnp.dot(a, b, preferred_element_type=_F32)


def _layer_norm(h, g, b):
    mu = jnp.mean(h, axis=-1, keepdims=True)
    hc = h - mu
    var = jnp.mean(hc * hc, axis=-1, keepdims=True)
    return hc * lax.rsqrt(var + LN_EPS) * g + b


def _silu(x):
    return x * jax.nn.sigmoid(x)


def _gelu_tanh(x):
    c = 0.7978845608028654
    return 0.5 * x * (1.0 + jnp.tanh(c * (x + 0.044715 * (x * x * x))))


def _pack_halves(v):
    half = v.shape[1] // 2
    lo = lax.bitcast_convert_type(v[:, :half].astype(_BF16).astype(_F32), _U32)
    hi = lax.bitcast_convert_type(v[:, half:].astype(_BF16).astype(_F32), _U32)
    return (hi & _U32(0xFFFF0000)) | (lo >> 16)


def _unpack_halves(w):
    lo = lax.bitcast_convert_type(w << 16, _F32)
    hi = lax.bitcast_convert_type(w & _U32(0xFFFF0000), _F32)
    return lo, hi


def _store_pieces(ref, r0, w):
    for c in range(PIECES):
        ref[c, r0:r0 + w.shape[0], :] = w[:, c * LANES:(c + 1) * LANES]


def _pool_kernel(x_ref, win_ref, wgrp_ref, scale_ref, wout_ref, g_ref, b_ref,
                 o_ref, op_ref, zs_ref, y_ref):
    s = pl.program_id(1)
    ts = x_ref.shape[1]

    @pl.when(s == 0)
    def _():
        zs_ref[0:POOL_HALO, :] = jnp.zeros((POOL_HALO, D_MODEL), _F32)

    for p0 in range(0, ts, POOL_SUB):
        z = _dot(x_ref[0, p0:p0 + POOL_SUB, :].astype(_BF16), win_ref[...])
        zs_ref[POOL_HALO + p0:POOL_HALO + p0 + POOL_SUB, :] = z
    for p0 in range(0, ts, POOL_SUB):
        x = x_ref[0, p0:p0 + POOL_SUB, :]
        base = POOL_HALO + p0
        pos = s * ts + p0 + lax.broadcasted_iota(jnp.int32, (POOL_SUB, 1), 0)
        for g, w in enumerate(POOL_WINDOWS):
            c0 = g * POOL_GROUP_DIM
            c1 = c0 + POOL_GROUP_DIM
            zg = zs_ref[base:base + POOL_SUB, c0:c1]
            acc = zg
            for k in range(1, w):
                acc = acc + zs_ref[base - k:base - k + POOL_SUB, c0:c1]
            cnt = jnp.minimum(pos + 1, w).astype(_F32)
            pooled = acc / cnt - zg
            yg = _dot(pooled.astype(_BF16), wgrp_ref[g]) * scale_ref[:, c0:c1]
            y_ref[p0:p0 + POOL_SUB, c0:c1] = yg.astype(_BF16)
        mix = _dot(y_ref[p0:p0 + POOL_SUB, :], wout_ref[...])
        out = _layer_norm(DEEPNORM_ALPHA * x + mix, g_ref[...], b_ref[...])
        o_ref[0, p0:p0 + POOL_SUB, :] = out
        _store_pieces(op_ref, p0, _pack_halves(out))
    zs_ref[0:POOL_HALO, :] = zs_ref[ts:ts + POOL_HALO, :]


def _pool_layer(x, part, w_in, w_grp, scale, w_out, ln_g, ln_b):
    bsz, seq, d = x.shape
    rows = bsz // TOKEN_PARTS
    ts = POOL_TILE
    steps = seq // ts
    const2 = lambda b, s: (0, 0)
    out_shape = (jax.ShapeDtypeStruct((rows, seq, d), _F32),
                 jax.ShapeDtypeStruct((PIECES, rows * seq, LANES), _U32))
    tile = pl.BlockSpec((1, ts, d), lambda b, s: (b, s, 0))
    ptile = pl.BlockSpec((PIECES, ts, LANES), lambda b, s: (0, b * steps + s, 0))
    return pl.pallas_call(
        _pool_kernel,
        grid=(rows, steps),
        in_specs=[
            pl.BlockSpec((1, ts, d), lambda b, s: (part * rows + b, s, 0)),
            pl.BlockSpec((d, d), const2),
            pl.BlockSpec((len(POOL_WINDOWS), POOL_GROUP_DIM, POOL_GROUP_DIM), lambda b, s: (0, 0, 0)),
            pl.BlockSpec((1, d), const2),
            pl.BlockSpec((d, d), const2),
            pl.BlockSpec((1, d), const2),
            pl.BlockSpec((1, d), const2),
        ],
        out_specs=(tile, ptile),
        out_shape=out_shape,
        scratch_shapes=[pltpu.VMEM((POOL_HALO + ts, d), _F32),
                        pltpu.VMEM((ts, d), _BF16)],
        compiler_params=pltpu.CompilerParams(
            dimension_semantics=("arbitrary", "arbitrary"),
            vmem_limit_bytes=VMEM_LIMIT),
        name="pool_mixer",
    )(x, w_in.astype(_BF16), w_grp.astype(_BF16), scale.reshape(1, d),
      w_out.astype(_BF16), ln_g.reshape(1, d), ln_b.reshape(1, d))


def _sgu_kernel(x_ref, win_ref, bin_ref, lng_ref, lnb_ref, ws_ref, bs_ref, wout_ref,
                g_ref, b_ref, o_ref, op_ref, gated_ref):
    ts = x_ref.shape[0]
    projected = []
    for p0 in range(0, ts, SGU_SUB):
        x = x_ref[p0:p0 + SGU_SUB, :]
        xb = x.astype(_BF16)
        v = _gelu_tanh(_dot(xb, win_ref[:, SGU_WIDTH:]) + bin_ref[:, SGU_WIDTH:])
        v = _layer_norm(v, lng_ref[...], lnb_ref[...]).astype(_BF16)
        u = _gelu_tanh(_dot(xb, win_ref[:, :SGU_WIDTH]) + bin_ref[:, :SGU_WIDTH])
        projected.append((p0, x, u, v))
    for p0, x, u, v in projected:
        for r0 in range(0, SGU_SUB, SGU_CHUNK):
            for h in range(SGU_HEADS):
                c0 = h * SGU_HEAD_DIM
                sv = _dot(ws_ref[h], v[r0:r0 + SGU_CHUNK, c0:c0 + SGU_HEAD_DIM]) + bs_ref[h]
                gated_ref[p0 + r0:p0 + r0 + SGU_CHUNK, c0:c0 + SGU_HEAD_DIM] = (
                    u[r0:r0 + SGU_CHUNK, c0:c0 + SGU_HEAD_DIM] * sv).astype(_BF16)
        mix = _dot(gated_ref[p0:p0 + SGU_SUB, :], wout_ref[...])
        out = _layer_norm(DEEPNORM_ALPHA * x + mix, g_ref[...], b_ref[...])
        o_ref[p0:p0 + SGU_SUB, :] = out
        _store_pieces(op_ref, p0, _pack_halves(out))


def _sgu_layer(x, w_in, b_in, ln_g, ln_b, w_s, b_s, w_out, mix_g, mix_b):
    n_tok, d = x.shape
    ts = SGU_TILE
    const2 = lambda i: (0, 0)
    const3 = lambda i: (0, 0, 0)
    causal = jnp.tril(jnp.ones((SGU_CHUNK, SGU_CHUNK), w_s.dtype))
    ws = (w_s * causal[None]).astype(_BF16)
    tile = pl.BlockSpec((ts, d), lambda i: (i, 0))
    ptile = pl.BlockSpec((PIECES, ts, LANES), lambda i: (0, i, 0))
    out_shape = (jax.ShapeDtypeStruct((n_tok, d), _F32),
                 jax.ShapeDtypeStruct((PIECES, n_tok, LANES), _U32))
    return pl.pallas_call(
        _sgu_kernel,
        grid=(n_tok // ts,),
        in_specs=[
            tile,
            pl.BlockSpec((d, 2 * SGU_WIDTH), const2),
            pl.BlockSpec((1, 2 * SGU_WIDTH), const2),
            pl.BlockSpec((1, SGU_WIDTH), const2),
            pl.BlockSpec((1, SGU_WIDTH), const2),
            pl.BlockSpec((SGU_HEADS, SGU_CHUNK, SGU_CHUNK), const3),
            pl.BlockSpec((SGU_HEADS, SGU_CHUNK, 1), const3),
            pl.BlockSpec((SGU_WIDTH, d), const2),
            pl.BlockSpec((1, d), const2),
            pl.BlockSpec((1, d), const2),
        ],
        out_specs=(tile, ptile),
        out_shape=out_shape,
        scratch_shapes=[pltpu.VMEM((ts, SGU_WIDTH), _BF16)],
        compiler_params=pltpu.CompilerParams(
            dimension_semantics=("arbitrary",),
            vmem_limit_bytes=VMEM_LIMIT),
        name="sgu_mixer",
    )(x, w_in.astype(_BF16), b_in.reshape(1, -1), ln_g.reshape(1, -1), ln_b.reshape(1, -1),
      ws, b_s.reshape(SGU_HEADS, SGU_CHUNK, 1), w_out.astype(_BF16),
      mix_g.reshape(1, d), mix_b.reshape(1, d))


def _all_max(a):
    return jnp.max(jnp.max(a, axis=0, keepdims=True), axis=1, keepdims=True)


def _all_min(a):
    return jnp.min(jnp.min(a, axis=0, keepdims=True), axis=1, keepdims=True)


def _all_sum(a):
    return jnp.sum(jnp.sum(a, axis=0, keepdims=True), axis=1, keepdims=True)


def _route_kernel(x_ref, whi_ref, wlo_ref, bias_ref, earlier_ref,
                  ek_ref, rk_ref, gk_ref, cnt_ref, carry_ref):
    @pl.when(pl.program_id(0) == 0)
    def _():
        carry_ref[...] = jnp.zeros_like(carry_ref)

    nt = lambda a, b: lax.dot_general(a, b, (((1,), (1,)), ((), ())),
                                      preferred_element_type=_F32)
    logits = []
    for p0 in range(0, x_ref.shape[0], ROUTE_SUB):
        x = x_ref[p0:p0 + ROUTE_SUB, :]
        x_hi = x.astype(_BF16)
        x_lo = (x - x_hi.astype(_F32)).astype(_BF16)
        logits.append(nt(whi_ref[...], x_hi) + (nt(whi_ref[...], x_lo) + nt(wlo_ref[...], x_hi)))
    for n, sub_logits in enumerate(logits):
        _route_sub_tile(sub_logits, n * ROUTE_SUB, bias_ref, earlier_ref,
                        ek_ref, rk_ref, gk_ref, carry_ref)
    cnt_ref[...] = carry_ref[...]


def _route_sub_tile(logits, p0, bias_ref, earlier_ref, ek_ref, rk_ref, gk_ref, carry_ref):
    ts = ROUTE_SUB
    scores = jax.nn.sigmoid(logits).reshape(N_GROUPS, GROUP_SIZE, ts)
    biased = scores + bias_ref[...]
    neg_inf = jnp.float32(-jnp.inf)
    shape3 = (N_GROUPS, GROUP_SIZE, ts)
    in_grp = lax.broadcasted_iota(jnp.int32, shape3, 1)
    grp = lax.broadcasted_iota(jnp.int32, shape3, 0)
    eid = grp * GROUP_SIZE + in_grp

    m1 = jnp.max(biased, axis=1, keepdims=True)
    first1 = jnp.min(jnp.where(biased == m1, in_grp, GROUP_SIZE), axis=1, keepdims=True)
    m2 = jnp.max(jnp.where(in_grp == first1, neg_inf, biased), axis=1, keepdims=True)
    gscore = m1 + m2

    gid = lax.broadcasted_iota(jnp.int32, (N_GROUPS, 1, ts), 0)
    gsel = jnp.zeros((N_GROUPS, 1, ts), jnp.bool_)
    for _ in range(TOPK_GROUPS):
        m = jnp.max(gscore, axis=0, keepdims=True)
        first = jnp.min(jnp.where(gscore == m, gid, N_GROUPS), axis=0, keepdims=True)
        pick = gid == first
        gsel = jnp.logical_or(gsel, pick)
        gscore = jnp.where(pick, neg_inf, gscore)

    masked = jnp.where(gsel, biased, neg_inf)
    picked_any = jnp.zeros(shape3, jnp.bool_)
    e_k, s_k = [], []
    for _ in range(TOP_K):
        m = _all_max(masked)
        first = _all_min(jnp.where(masked == m, eid, N_EXPERTS))
        pick = eid == first
        picked_any = jnp.logical_or(picked_any, pick)
        masked = jnp.where(pick, neg_inf, masked)
        e_k.append(first)
        s_k.append(_all_sum(jnp.where(pick, scores, 0.0)))
    denom = s_k[0]
    for k in range(1, TOP_K):
        denom = denom + s_k[k]

    sel = picked_any.astype(_F32).reshape(N_EXPERTS, ts)
    rank = _dot(sel.astype(_BF16), earlier_ref[...]) + carry_ref[...]
    rank3 = rank.reshape(shape3)
    carry_ref[...] += jnp.sum(sel, axis=1, keepdims=True)

    for k in range(TOP_K):
        r = _all_sum(jnp.where(eid == e_k[k], rank3, 0.0))
        ek_ref[k:k + 1, p0:p0 + ts] = e_k[k].reshape(1, ts)
        rk_ref[k:k + 1, p0:p0 + ts] = r.reshape(1, ts).astype(jnp.int32)
        gk_ref[k:k + 1, p0:p0 + ts] = (s_k[k] / denom * ROUTED_SCALE).reshape(1, ts)


def _route(x, w_router, router_bias):
    n_part, d = x.shape
    ts = ROUTE_TILE
    steps = n_part // ts
    out_shape = (jax.ShapeDtypeStruct((TOP_K, n_part), jnp.int32),
                 jax.ShapeDtypeStruct((TOP_K, n_part), jnp.int32),
                 jax.ShapeDtypeStruct((TOP_K, n_part), _F32),
                 jax.ShapeDtypeStruct((N_EXPERTS, 1), _F32))
    kspec = pl.BlockSpec((TOP_K, ts), lambda i: (0, i))
    const2 = lambda i: (0, 0)
    w_t = w_router.T
    w_hi = w_t.astype(_BF16)
    w_lo = (w_t - w_hi.astype(_F32)).astype(_BF16)
    pos = jnp.arange(ROUTE_SUB, dtype=jnp.int32)
    earlier = (pos[:, None] < pos[None, :]).astype(_BF16)
    return pl.pallas_call(
        _route_kernel,
        grid=(steps,),
        in_specs=[
            pl.BlockSpec((ts, d), lambda i: (i, 0)),
            pl.BlockSpec((N_EXPERTS, d), const2),
            pl.BlockSpec((N_EXPERTS, d), const2),
            pl.BlockSpec((N_GROUPS, GROUP_SIZE, 1), lambda i: (0, 0, 0)),
            pl.BlockSpec((ROUTE_SUB, ROUTE_SUB), const2),
        ],
        out_specs=(kspec, kspec, kspec, pl.BlockSpec((N_EXPERTS, 1), const2)),
        out_shape=out_shape,
        scratch_shapes=[pltpu.VMEM((N_EXPERTS, 1), _F32)],
        compiler_params=pltpu.CompilerParams(
            dimension_semantics=("arbitrary",),
            vmem_limit_bytes=VMEM_LIMIT),
        name="moe_route",
    )(x, w_hi, w_lo, router_bias.reshape(N_GROUPS, GROUP_SIZE, 1), earlier)


def _sc_mesh():
    return plsc.VectorSubcoreMesh(core_axis_name="core", subcore_axis_name="subcore")


def _sc_scatter_rows(src, idx, n_out):
    d = src.shape[1]
    wins = idx.shape[1] // SC_WINDOW

    def body(src_hbm, idx_hbm, out_hbm):
        def step(src_vmem, idx_vmem):
            for k in range(TOP_K):
                pltpu.sync_copy(src_vmem, out_hbm.at[idx_vmem.at[k]])

        pltpu.emit_pipeline(
            step,
            grid=(src.shape[0] // SC_WINDOW,),
            in_specs=[pl.BlockSpec((SC_WINDOW, d), index_map=lambda i: (i, 0)),
                      pl.BlockSpec((TOP_K, SC_WINDOW), index_map=lambda i: (i // wins, i % wins))],
            out_specs=[],
            core_axis_name=("core", "subcore"),
            dimension_semantics=(pltpu.PARALLEL,),
        )(src_hbm, idx_hbm)

    return pl.kernel(body, out_type=jax.ShapeDtypeStruct((n_out, d), src.dtype),
                     mesh=_sc_mesh(), scratch_types=[], name="sc_scatter_rows")(src, idx)


def _sc_gather_rows(table, idx):
    d = table.shape[1]
    n_idx_rows, n_tok = idx.shape
    wins = n_tok // SC_WINDOW

    def body(table_hbm, idx_hbm, out_hbm):
        def step(idx_vmem, out_vmem):
            pltpu.sync_copy(table_hbm.at[idx_vmem.at[0]], out_vmem)

        pltpu.emit_pipeline(
            step,
            grid=(n_idx_rows * wins,),
            in_specs=[pl.BlockSpec((1, SC_WINDOW), index_map=lambda i: (i // wins, i % wins))],
            out_specs=[pl.BlockSpec((SC_WINDOW, d), index_map=lambda i: (i, 0))],
            core_axis_name=("core", "subcore"),
            dimension_semantics=(pltpu.PARALLEL,),
        )(idx_hbm, out_hbm)

    return pl.kernel(body, out_type=jax.ShapeDtypeStruct((n_idx_rows * n_tok, d), table.dtype),
                     mesh=_sc_mesh(), scratch_types=[], name="sc_gather_rows")(table, idx)


def _index_kernel(start_ref, ek_ref, rk_ref, idx_ref, *, n_rows):
    e = ek_ref[...]
    start = jnp.zeros(e.shape, jnp.int32)
    for ex in range(N_EXPERTS):
        start = jnp.where(e == ex, start_ref[ex], start)
    dest = start + rk_ref[...]
    for c in range(PIECES):
        idx_ref[c] = dest + c * n_rows


def _row_indices(padded_start, e_k, r_k, n_rows):
    n_tok = e_k.shape[1]
    ts = INDEX_TILE
    kspec = pl.BlockSpec((TOP_K, ts), lambda i, st: (0, i))
    grid_spec = pltpu.PrefetchScalarGridSpec(
        num_scalar_prefetch=1,
        grid=(n_tok // ts,),
        in_specs=[kspec, kspec],
        out_specs=pl.BlockSpec((PIECES, TOP_K, ts), lambda i, st: (0, 0, i)),
    )
    return pl.pallas_call(
        functools.partial(_index_kernel, n_rows=n_rows),
        grid_spec=grid_spec,
        out_shape=jax.ShapeDtypeStruct((PIECES, TOP_K, n_tok), jnp.int32),
        compiler_params=pltpu.CompilerParams(dimension_semantics=("arbitrary",)),
        name="moe_row_indices",
    )(padded_start, e_k, r_k)


def _expert_kernel(chunk0_ref, nchunk_ref, cnt_ref, total_ref,
                   xs_hbm, wg_ref, wu_ref, wd_ref, o_hbm,
                   xbuf, obuf, wgu_bf, wd_bf, sem_in, sem_out):
    e = pl.program_id(0)
    total = total_ref[0]
    ch = EXPERT_ROWS
    ahead = IN_SLOTS - 1

    def in_copy(g):
        slot = g % IN_SLOTS
        return pltpu.make_async_copy(xs_hbm.at[:, pl.ds(g * ch, ch), :], xbuf.at[slot],
                                     sem_in.at[slot])

    def out_copy(g):
        slot = g % OUT_SLOTS
        return pltpu.make_async_copy(obuf.at[slot], o_hbm.at[:, pl.ds(g * ch, ch), :],
                                     sem_out.at[slot])

    @pl.when(e == 0)
    def _():
        obuf[...] = jnp.zeros_like(obuf)
        for g in range(ahead):
            @pl.when(g < total)
            def _():
                in_copy(g).start()

    wgu_bf[:, :EXPERT_DIM] = wg_ref[0, 0].astype(_BF16)
    wgu_bf[:, EXPERT_DIM:] = wu_ref[0, 0].astype(_BF16)
    wd_bf[...] = wd_ref[0, 0].astype(_BF16)

    def chunk(j, carry):
        g = chunk0_ref[e] + j
        in_copy(g).wait()

        @pl.when(g + ahead < total)
        def _():
            in_copy(g + ahead).start()

        @pl.when(g >= OUT_SLOTS)
        def _():
            out_copy(g - OUT_SLOTS).wait()

        islot = g % IN_SLOTS
        oslot = g % OUT_SLOTS
        valid = cnt_ref[e] - j * ch

        def gate_up(s):
            r0 = s * EXPERT_SUB
            w = jnp.concatenate([xbuf[islot, c, r0:r0 + EXPERT_SUB, :] for c in range(PIECES)],
                                axis=1)
            lo, hi = _unpack_halves(w)
            return (_dot(lo.astype(_BF16), wgu_bf[:PACKED]) +
                    _dot(hi.astype(_BF16), wgu_bf[PACKED:]))

        def down(s, gu):
            r0 = s * EXPERT_SUB
            h = _silu(gu[:, :EXPERT_DIM]) * gu[:, EXPERT_DIM:]
            y = _pack_halves(_dot(h.astype(_BF16), wd_bf[...]))
            for c in range(PIECES):
                obuf[oslot, c, r0:r0 + EXPERT_SUB, :] = y[:, c * LANES:(c + 1) * LANES]

        n_sub = ch // EXPERT_SUB

        @pl.when(valid >= ch)
        def _():
            projected = [gate_up(s) for s in range(n_sub)]
            for s, gu in enumerate(projected):
                down(s, gu)

        @pl.when(valid < ch)
        def _():
            for s in range(n_sub):
                @pl.when(s * EXPERT_SUB < valid)
                def _():
                    down(s, gate_up(s))

        out_copy(g).start()
        return carry

    lax.fori_loop(0, nchunk_ref[e], chunk, 0)

    @pl.when(e == pl.num_programs(0) - 1)
    def _():
        for back in range(OUT_SLOTS, 0, -1):
            @pl.when(total >= back)
            def _():
                out_copy(total - back).wait()


def _experts(layer, chunk0, nchunk, counts, total, xs, w_gate, w_up, w_down):
    d = D_MODEL
    ch = EXPERT_ROWS
    wspec_in = pl.BlockSpec((1, 1, d, EXPERT_DIM), lambda e, *_: (layer, e, 0, 0))
    any_spec = pl.BlockSpec(memory_space=pl.ANY)
    grid_spec = pltpu.PrefetchScalarGridSpec(
        num_scalar_prefetch=4,
        grid=(N_EXPERTS,),
        in_specs=[
            any_spec,
            wspec_in,
            wspec_in,
            pl.BlockSpec((1, 1, EXPERT_DIM, d), lambda e, *_: (layer, e, 0, 0)),
        ],
        out_specs=any_spec,
        scratch_shapes=[pltpu.VMEM((IN_SLOTS, PIECES, ch, LANES), _U32),
                        pltpu.VMEM((OUT_SLOTS, PIECES, ch, LANES), _U32),
                        pltpu.VMEM((d, 2 * EXPERT_DIM), _BF16),
                        pltpu.VMEM((EXPERT_DIM, d), _BF16),
                        pltpu.SemaphoreType.DMA((IN_SLOTS,)),
                        pltpu.SemaphoreType.DMA((OUT_SLOTS,))],
    )
    return pl.pallas_call(
        _expert_kernel,
        grid_spec=grid_spec,
        out_shape=jax.ShapeDtypeStruct(xs.shape, _U32),
        compiler_params=pltpu.CompilerParams(
            dimension_semantics=("arbitrary",),
            vmem_limit_bytes=VMEM_LIMIT),
        name="moe_experts",
    )(chunk0, nchunk, counts, total, xs, w_gate, w_up, w_down)


def _combine_kernel(x_ref, p_ref, gate_ref, sg_ref, su_ref, sd_ref, g_ref, b_ref, *rest):
    o_ref = rest[-1]
    x = x_ref[...]
    xb = x.astype(_BF16)
    gates = gate_ref[...].T
    r_lo = r_hi = None
    for k in range(TOP_K):
        w = jnp.concatenate([p_ref[c, k] for c in range(PIECES)], axis=1)
        lo, hi = _unpack_halves(w)
        gk = gates[:, k:k + 1]
        r_lo = gk * lo if r_lo is None else r_lo + gk * lo
        r_hi = gk * hi if r_hi is None else r_hi + gk * hi
    routed = jnp.concatenate([r_lo, r_hi], axis=1)
    h = _silu(_dot(xb, sg_ref[...])) * _dot(xb, su_ref[...])
    shared = _dot(h.astype(_BF16), sd_ref[...])
    o_ref[...] = _layer_norm(DEEPNORM_ALPHA * x + (routed + shared), g_ref[...], b_ref[...])


def _combine(x, picked, gates, sh_gate, sh_up, sh_down, ln_g, ln_b, out_parts, part, prev):
    n_part, d = x.shape
    ts = COMBINE_TILE
    steps = n_part // ts
    const2 = lambda i: (0, 0)
    in_specs = [
        pl.BlockSpec((ts, d), lambda i: (i, 0)),
        pl.BlockSpec((PIECES, TOP_K, ts, LANES), lambda i: (0, 0, i, 0)),
        pl.BlockSpec((TOP_K, ts), lambda i: (0, i)),
        pl.BlockSpec((d, EXPERT_DIM), const2),
        pl.BlockSpec((d, EXPERT_DIM), const2),
        pl.BlockSpec((EXPERT_DIM, d), const2),
        pl.BlockSpec((1, d), const2),
        pl.BlockSpec((1, d), const2),
    ]
    args = [x, picked, gates, sh_gate.astype(_BF16), sh_up.astype(_BF16), sh_down.astype(_BF16),
            ln_g.reshape(1, d), ln_b.reshape(1, d)]
    aliases = {}
    if prev is not None:
        in_specs.append(pl.BlockSpec(memory_space=pl.ANY))
        aliases = {len(args): 0}
        args.append(prev)
    return pl.pallas_call(
        _combine_kernel,
        grid=(steps,),
        in_specs=in_specs,
        out_specs=pl.BlockSpec((ts, d), lambda i: (part * steps + i, 0)),
        out_shape=jax.ShapeDtypeStruct((out_parts * n_part, d), _F32),
        input_output_aliases=aliases,
        compiler_params=pltpu.CompilerParams(
            dimension_semantics=("arbitrary",),
            vmem_limit_bytes=VMEM_LIMIT),
        name="moe_combine",
    )(*args)


def _moe_dispatch(layer, x, x_packed, w_router, router_bias, w_gate, w_up, w_down):
    n_part, d = x.shape
    bm = EXPERT_ROWS
    n_rows = n_part * TOP_K + N_EXPERTS * bm
    e_k, r_k, g_k, counts = _route(x, w_router, router_bias)
    counts = counts.reshape(N_EXPERTS).astype(jnp.int32)
    padded = (counts + bm - 1) // bm * bm
    padded_end = jnp.cumsum(padded)
    padded_start = padded_end - padded
    idx = _row_indices(padded_start, e_k, r_k, n_rows).reshape(PIECES * TOP_K, n_part)
    xs = _sc_scatter_rows(x_packed.reshape(PIECES * n_part, LANES), idx, PIECES * n_rows)
    rows = _experts(layer, padded_start // bm, padded // bm, counts, padded_end[-1:] // bm,
                    xs.reshape(PIECES, n_rows, LANES), w_gate, w_up, w_down)
    return rows.reshape(PIECES * n_rows, LANES), idx, g_k


def _moe_combine(x, rows, idx, gates, sh_gate, sh_up, sh_down, ln_g, ln_b,
                 out_parts=1, part=0, prev=None):
    picked = _sc_gather_rows(rows, idx).reshape(PIECES, TOP_K, x.shape[0], LANES)
    return _combine(x, picked, gates, sh_gate, sh_up, sh_down, ln_g, ln_b, out_parts, part, prev)


def kernel(x, pool_w_in, pool_w_grp, pool_scale, pool_w_out, sgu_w_in, sgu_b_in, sgu_ln_g, sgu_ln_b, sgu_w_s, sgu_b_s, sgu_w_out, ln_mix_g, ln_mix_b, moe_w_router, moe_router_bias, moe_w_gate, moe_w_up, moe_w_down, moe_sh_gate, moe_sh_up, moe_sh_down, ln_ffn_g, ln_ffn_b):
    bsz, seq, d = x.shape
    parts = range(TOKEN_PARTS)

    def dispatch(i, h, h_packed):
        return _moe_dispatch(i, h, h_packed, moe_w_router[i], moe_router_bias[i],
                             moe_w_gate, moe_w_up, moe_w_down)

    def combine(i, h, routed, **where):
        return _moe_combine(h, *routed, moe_sh_gate[i], moe_sh_up[i], moe_sh_down[i],
                            ln_ffn_g[i], ln_ffn_b[i], **where)

    mixed = [_pool_layer(x, p, pool_w_in[0], pool_w_grp[0], pool_scale[0], pool_w_out[0],
                         ln_mix_g[0], ln_mix_b[0]) for p in parts]
    mixed = [(h.reshape(-1, d), h_packed) for h, h_packed in mixed]
    routed = [dispatch(0, h, h_packed) for h, h_packed in mixed]
    hs = [combine(0, mixed[p][0], routed[p]) for p in parts]
    mixed = [_sgu_layer(h, sgu_w_in[0], sgu_b_in[0], sgu_ln_g[0], sgu_ln_b[0], sgu_w_s[0],
                        sgu_b_s[0], sgu_w_out[0], ln_mix_g[1], ln_mix_b[1]) for h in hs]
    routed = [dispatch(1, h, h_packed) for h, h_packed in mixed]
    out = None
    for p in parts:
        out = combine(1, mixed[p][0], routed[p], out_parts=TOKEN_PARTS, part=p, prev=out)
    return out.reshape(bsz, seq, d)
```

```python
import functools

import jax
import jax.numpy as jnp
from jax import lax
from jax.experimental import pallas as pl
from jax.experimental.pallas import tpu as pltpu
from jax.experimental.pallas import tpu_sc as plsc

D_MODEL = 1024
DEPTH = 2
POOL_WINDOWS = (2, 4, 8, 16)
POOL_GROUP_DIM = D_MODEL // len(POOL_WINDOWS)
POOL_HALO = 16
SGU_CHUNK = 128
SGU_HEADS = 4
SGU_WIDTH = 2 * D_MODEL
SGU_HEAD_DIM = SGU_WIDTH // SGU_HEADS
N_EXPERTS = 64
TOP_K = 8
N_GROUPS = 8
GROUP_SIZE = N_EXPERTS // N_GROUPS
TOPK_GROUPS = 4
EXPERT_DIM = D_MODEL // 4
ROUTED_SCALE = 2.5
DEEPNORM_ALPHA = (2 * DEPTH) ** 0.25
LN_EPS = 1e-5

LANES = 128
PACKED = D_MODEL // 2
PIECES = PACKED // LANES

POOL_TILE = 1024
POOL_SUB = 256
SGU_TILE = 512
SGU_SUB = 256
ROUTE_TILE = 1024
ROUTE_SUB = 512
INDEX_TILE = 2048
EXPERT_ROWS = 512
EXPERT_SUB = 256
IN_SLOTS = 6
OUT_SLOTS = 4
COMBINE_TILE = 512
SC_WINDOW = 128
SC_LANES = 16
SC_BURST = 16
TOKEN_PARTS = 2
VMEM_LIMIT = 56 * 1024 * 1024

_F32 = jnp.float32
_BF16 = jnp.bfloat16
_U32 = jnp.uint32


def _dot(a, b):
    return jnp.dot(a, b, preferred_element_type=_F32)


def _layer_norm(h, g, b):
    mu = jnp.mean(h, axis=-1, keepdims=True)
    hc = h - mu
    var = jnp.mean(hc * hc, axis=-1, keepdims=True)
    return hc * lax.rsqrt(var + LN_EPS) * g + b


def _silu(x):
    return x * jax.nn.sigmoid(x)


def _gelu_tanh(x):
    c = 0.7978845608028654
    return 0.5 * x * (1.0 + jnp.tanh(c * (x + 0.044715 * (x * x * x))))


def _pack_halves(v):
    half = v.shape[1] // 2
    lo = lax.bitcast_convert_type(v[:, :half].astype(_BF16).astype(_F32), _U32)
    hi = lax.bitcast_convert_type(v[:, half:].astype(_BF16).astype(_F32), _U32)
    return (hi & _U32(0xFFFF0000)) | (lo >> 16)


def _unpack_halves(w):
    lo = lax.bitcast_convert_type(w << 16, _F32)
    hi = lax.bitcast_convert_type(w & _U32(0xFFFF0000), _F32)
    return lo, hi


def _store_pieces(ref, r0, w):
    for c in range(PIECES):
        ref[c, r0:r0 + w.shape[0], :] = w[:, c * LANES:(c + 1) * LANES]


def _pool_kernel(x_ref, win_ref, wgrp_ref, scale_ref, wout_ref, g_ref, b_ref,
                 o_ref, op_ref, zs_ref, y_ref):
    s = pl.program_id(1)
    ts = x_ref.shape[1]

    @pl.when(s == 0)
    def _():
        zs_ref[0:POOL_HALO, :] = jnp.zeros((POOL_HALO, D_MODEL), _F32)

    for p0 in range(0, ts, POOL_SUB):
        z = _dot(x_ref[0, p0:p0 + POOL_SUB, :].astype(_BF16), win_ref[...])
        zs_ref[POOL_HALO + p0:POOL_HALO + p0 + POOL_SUB, :] = z
    for p0 in range(0, ts, POOL_SUB):
        x = x_ref[0, p0:p0 + POOL_SUB, :]
        base = POOL_HALO + p0
        pos = s * ts + p0 + lax.broadcasted_iota(jnp.int32, (POOL_SUB, 1), 0)
        for g, w in enumerate(POOL_WINDOWS):
            c0 = g * POOL_GROUP_DIM
            c1 = c0 + POOL_GROUP_DIM
            zg = zs_ref[base:base + POOL_SUB, c0:c1]
            acc = zg
            for k in range(1, w):
                acc = acc + zs_ref[base - k:base - k + POOL_SUB, c0:c1]
            cnt = jnp.minimum(pos + 1, w).astype(_F32)
            pooled = acc / cnt - zg
            yg = _dot(pooled.astype(_BF16), wgrp_ref[g]) * scale_ref[:, c0:c1]
            y_ref[p0:p0 + POOL_SUB, c0:c1] = yg.astype(_BF16)
        mix = _dot(y_ref[p0:p0 + POOL_SUB, :], wout_ref[...])
        out = _layer_norm(DEEPNORM_ALPHA * x + mix, g_ref[...], b_ref[...])
        o_ref[0, p0:p0 + POOL_SUB, :] = out
        _store_pieces(op_ref, p0, _pack_halves(out))
    zs_ref[0:POOL_HALO, :] = zs_ref[ts:ts + POOL_HALO, :]


def _pool_layer(x, part, w_in, w_grp, scale, w_out, ln_g, ln_b):
    bsz, seq, d = x.shape
    rows = bsz // TOKEN_PARTS
    ts = POOL_TILE
    steps = seq // ts
    const2 = lambda b, s: (0, 0)
    out_shape = (jax.ShapeDtypeStruct((rows, seq, d), _F32),
                 jax.ShapeDtypeStruct((PIECES, rows * seq, LANES), _U32))
    tile = pl.BlockSpec((1, ts, d), lambda b, s: (b, s, 0))
    ptile = pl.BlockSpec((PIECES, ts, LANES), lambda b, s: (0, b * steps + s, 0))
    return pl.pallas_call(
        _pool_kernel,
        grid=(rows, steps),
        in_specs=[
            pl.BlockSpec((1, ts, d), lambda b, s: (part * rows + b, s, 0)),
            pl.BlockSpec((d, d), const2),
            pl.BlockSpec((len(POOL_WINDOWS), POOL_GROUP_DIM, POOL_GROUP_DIM), lambda b, s: (0, 0, 0)),
            pl.BlockSpec((1, d), const2),
            pl.BlockSpec((d, d), const2),
            pl.BlockSpec((1, d), const2),
            pl.BlockSpec((1, d), const2),
        ],
        out_specs=(tile, ptile),
        out_shape=out_shape,
        scratch_shapes=[pltpu.VMEM((POOL_HALO + ts, d), _F32),
                        pltpu.VMEM((ts, d), _BF16)],
        compiler_params=pltpu.CompilerParams(
            dimension_semantics=("arbitrary", "arbitrary"),
            vmem_limit_bytes=VMEM_LIMIT),
        name="pool_mixer",
    )(x, w_in.astype(_BF16), w_grp.astype(_BF16), scale.reshape(1, d),
      w_out.astype(_BF16), ln_g.reshape(1, d), ln_b.reshape(1, d))


def _sgu_kernel(x_ref, win_ref, bin_ref, lng_ref, lnb_ref, ws_ref, bs_ref, wout_ref,
                g_ref, b_ref, o_ref, op_ref, gated_ref):
    ts = x_ref.shape[0]
    projected = []
    for p0 in range(0, ts, SGU_SUB):
        x = x_ref[p0:p0 + SGU_SUB, :]
        xb = x.astype(_BF16)
        v = _gelu_tanh(_dot(xb, win_ref[:, SGU_WIDTH:]) + bin_ref[:, SGU_WIDTH:])
        v = _layer_norm(v, lng_ref[...], lnb_ref[...]).astype(_BF16)
        u = _gelu_tanh(_dot(xb, win_ref[:, :SGU_WIDTH]) + bin_ref[:, :SGU_WIDTH])
        projected.append((p0, x, u, v))
    for p0, x, u, v in projected:
        for r0 in range(0, SGU_SUB, SGU_CHUNK):
            for h in range(SGU_HEADS):
                c0 = h * SGU_HEAD_DIM
                sv = _dot(ws_ref[h], v[r0:r0 + SGU_CHUNK, c0:c0 + SGU_HEAD_DIM]) + bs_ref[h]
                gated_ref[p0 + r0:p0 + r0 + SGU_CHUNK, c0:c0 + SGU_HEAD_DIM] = (
                    u[r0:r0 + SGU_CHUNK, c0:c0 + SGU_HEAD_DIM] * sv).astype(_BF16)
        mix = _dot(gated_ref[p0:p0 + SGU_SUB, :], wout_ref[...])
        out = _layer_norm(DEEPNORM_ALPHA * x + mix, g_ref[...], b_ref[...])
        o_ref[p0:p0 + SGU_SUB, :] = out
        _store_pieces(op_ref, p0, _pack_halves(out))


def _sgu_layer(x, w_in, b_in, ln_g, ln_b, w_s, b_s, w_out, mix_g, mix_b):
    n_tok, d = x.shape
    ts = SGU_TILE
    const2 = lambda i: (0, 0)
    const3 = lambda i: (0, 0, 0)
    causal = jnp.tril(jnp.ones((SGU_CHUNK, SGU_CHUNK), w_s.dtype))
    ws = (w_s * causal[None]).astype(_BF16)
    tile = pl.BlockSpec((ts, d), lambda i: (i, 0))
    ptile = pl.BlockSpec((PIECES, ts, LANES), lambda i: (0, i, 0))
    out_shape = (jax.ShapeDtypeStruct((n_tok, d), _F32),
                 jax.ShapeDtypeStruct((PIECES, n_tok, LANES), _U32))
    return pl.pallas_call(
        _sgu_kernel,
        grid=(n_tok // ts,),
        in_specs=[
            tile,
            pl.BlockSpec((d, 2 * SGU_WIDTH), const2),
            pl.BlockSpec((1, 2 * SGU_WIDTH), const2),
            pl.BlockSpec((1, SGU_WIDTH), const2),
            pl.BlockSpec((1, SGU_WIDTH), const2),
            pl.BlockSpec((SGU_HEADS, SGU_CHUNK, SGU_CHUNK), const3),
            pl.BlockSpec((SGU_HEADS, SGU_CHUNK, 1), const3),
            pl.BlockSpec((SGU_WIDTH, d), const2),
            pl.BlockSpec((1, d), const2),
            pl.BlockSpec((1, d), const2),
        ],
        out_specs=(tile, ptile),
        out_shape=out_shape,
        scratch_shapes=[pltpu.VMEM((ts, SGU_WIDTH), _BF16)],
        compiler_params=pltpu.CompilerParams(
            dimension_semantics=("arbitrary",),
            vmem_limit_bytes=VMEM_LIMIT),
        name="sgu_mixer",
    )(x, w_in.astype(_BF16), b_in.reshape(1, -1), ln_g.reshape(1, -1), ln_b.reshape(1, -1),
      ws, b_s.reshape(SGU_HEADS, SGU_CHUNK, 1), w_out.astype(_BF16),
      mix_g.reshape(1, d), mix_b.reshape(1, d))


def _all_max(a):
    return jnp.max(jnp.max(a, axis=0, keepdims=True), axis=1, keepdims=True)


def _all_min(a):
    return jnp.min(jnp.min(a, axis=0, keepdims=True), axis=1, keepdims=True)


def _all_sum(a):
    return jnp.sum(jnp.sum(a, axis=0, keepdims=True), axis=1, keepdims=True)


def _route_kernel(x_ref, whi_ref, wlo_ref, bias_ref, earlier_ref,
                  ek_ref, rk_ref, gk_ref, cnt_ref, carry_ref):
    @pl.when(pl.program_id(0) == 0)
    def _():
        carry_ref[...] = jnp.zeros_like(carry_ref)

    nt = lambda a, b: lax.dot_general(a, b, (((1,), (1,)), ((), ())),
                                      preferred_element_type=_F32)
    logits = []
    for p0 in range(0, x_ref.shape[0], ROUTE_SUB):
        x = x_ref[p0:p0 + ROUTE_SUB, :]
        x_hi = x.astype(_BF16)
        x_lo = (x - x_hi.astype(_F32)).astype(_BF16)
        logits.append(nt(whi_ref[...], x_hi) + (nt(whi_ref[...], x_lo) + nt(wlo_ref[...], x_hi)))
    for n, sub_logits in enumerate(logits):
        _route_sub_tile(sub_logits, n * ROUTE_SUB, bias_ref, earlier_ref,
                        ek_ref, rk_ref, gk_ref, carry_ref)
    cnt_ref[...] = carry_ref[...]


def _route_sub_tile(logits, p0, bias_ref, earlier_ref, ek_ref, rk_ref, gk_ref, carry_ref):
    ts = ROUTE_SUB
    scores = jax.nn.sigmoid(logits).reshape(N_GROUPS, GROUP_SIZE, ts)
    biased = scores + bias_ref[...]
    neg_inf = jnp.float32(-jnp.inf)
    shape3 = (N_GROUPS, GROUP_SIZE, ts)
    in_grp = lax.broadcasted_iota(jnp.int32, shape3, 1)
    grp = lax.broadcasted_iota(jnp.int32, shape3, 0)
    eid = grp * GROUP_SIZE + in_grp

    m1 = jnp.max(biased, axis=1, keepdims=True)
    first1 = jnp.min(jnp.where(biased == m1, in_grp, GROUP_SIZE), axis=1, keepdims=True)
    m2 = jnp.max(jnp.where(in_grp == first1, neg_inf, biased), axis=1, keepdims=True)
    gscore = m1 + m2

    gid = lax.broadcasted_iota(jnp.int32, (N_GROUPS, 1, ts), 0)
    gsel = jnp.zeros((N_GROUPS, 1, ts), jnp.bool_)
    for _ in range(TOPK_GROUPS):
        m = jnp.max(gscore, axis=0, keepdims=True)
        first = jnp.min(jnp.where(gscore == m, gid, N_GROUPS), axis=0, keepdims=True)
        pick = gid == first
        gsel = jnp.logical_or(gsel, pick)
        gscore = jnp.where(pick, neg_inf, gscore)

    masked = jnp.where(gsel, biased, neg_inf)
    picked_any = jnp.zeros(shape3, jnp.bool_)
    e_k, s_k = [], []
    for _ in range(TOP_K):
        m = _all_max(masked)
        first = _all_min(jnp.where(masked == m, eid, N_EXPERTS))
        pick = eid == first
        picked_any = jnp.logical_or(picked_any, pick)
        masked = jnp.where(pick, neg_inf, masked)
        e_k.append(first)
        s_k.append(_all_sum(jnp.where(pick, scores, 0.0)))
    denom = s_k[0]
    for k in range(1, TOP_K):
        denom = denom + s_k[k]

    sel = picked_any.astype(_F32).reshape(N_EXPERTS, ts)
    rank = _dot(sel.astype(_BF16), earlier_ref[...]) + carry_ref[...]
    rank3 = rank.reshape(shape3)
    carry_ref[...] += jnp.sum(sel, axis=1, keepdims=True)

    for k in range(TOP_K):
        r = _all_sum(jnp.where(eid == e_k[k], rank3, 0.0))
        ek_ref[k:k + 1, p0:p0 + ts] = e_k[k].reshape(1, ts)
        rk_ref[k:k + 1, p0:p0 + ts] = r.reshape(1, ts).astype(jnp.int32)
        gk_ref[k:k + 1, p0:p0 + ts] = (s_k[k] / denom * ROUTED_SCALE).reshape(1, ts)


def _route(x, w_router, router_bias):
    n_part, d = x.shape
    ts = ROUTE_TILE
    steps = n_part // ts
    out_shape = (jax.ShapeDtypeStruct((TOP_K, n_part), jnp.int32),
                 jax.ShapeDtypeStruct((TOP_K, n_part), jnp.int32),
                 jax.ShapeDtypeStruct((TOP_K, n_part), _F32),
                 jax.ShapeDtypeStruct((N_EXPERTS, 1), _F32))
    kspec = pl.BlockSpec((TOP_K, ts), lambda i: (0, i))
    const2 = lambda i: (0, 0)
    w_t = w_router.T
    w_hi = w_t.astype(_BF16)
    w_lo = (w_t - w_hi.astype(_F32)).astype(_BF16)
    pos = jnp.arange(ROUTE_SUB, dtype=jnp.int32)
    earlier = (pos[:, None] < pos[None, :]).astype(_BF16)
    return pl.pallas_call(
        _route_kernel,
        grid=(steps,),
        in_specs=[
            pl.BlockSpec((ts, d), lambda i: (i, 0)),
            pl.BlockSpec((N_EXPERTS, d), const2),
            pl.BlockSpec((N_EXPERTS, d), const2),
            pl.BlockSpec((N_GROUPS, GROUP_SIZE, 1), lambda i: (0, 0, 0)),
            pl.BlockSpec((ROUTE_SUB, ROUTE_SUB), const2),
        ],
        out_specs=(kspec, kspec, kspec, pl.BlockSpec((N_EXPERTS, 1), const2)),
        out_shape=out_shape,
        scratch_shapes=[pltpu.VMEM((N_EXPERTS, 1), _F32)],
        compiler_params=pltpu.CompilerParams(
            dimension_semantics=("arbitrary",),
            vmem_limit_bytes=VMEM_LIMIT),
        name="moe_route",
    )(x, w_hi, w_lo, router_bias.reshape(N_GROUPS, GROUP_SIZE, 1), earlier)


def _sc_mesh():
    return plsc.VectorSubcoreMesh(core_axis_name="core", subcore_axis_name="subcore")


def _sc_scatter_rows(src, idx, n_out):
    d = src.shape[1]
    wins = idx.shape[1] // SC_WINDOW

    def body(src_hbm, idx_hbm, out_hbm):
        def step(src_vmem, idx_vmem):
            for k in range(TOP_K):
                pltpu.sync_copy(src_vmem, out_hbm.at[idx_vmem.at[k]])

        pltpu.emit_pipeline(
            step,
            grid=(src.shape[0] // SC_WINDOW,),
            in_specs=[pl.BlockSpec((SC_WINDOW, d), index_map=lambda i: (i, 0)),
                      pl.BlockSpec((TOP_K, SC_WINDOW), index_map=lambda i: (i // wins, i % wins))],
            out_specs=[],
            core_axis_name=("core", "subcore"),
            dimension_semantics=(pltpu.PARALLEL,),
        )(src_hbm, idx_hbm)

    return pl.kernel(body, out_type=jax.ShapeDtypeStruct((n_out, d), src.dtype),
                     mesh=_sc_mesh(), scratch_types=[], name="sc_scatter_rows")(src, idx)


def _sc_gather_sum(table, idx, gate_lanes):
    d = table.shape[1]
    n_tok = idx.shape[1]
    wins = n_tok // SC_WINDOW
    lanes = SC_LANES
    out_type = jax.ShapeDtypeStruct((PIECES * n_tok, d), _F32)

    def body(table_hbm, idx_hbm, gate_hbm, lo_hbm, hi_hbm, rows_vmem, sem):
        def step(idx_vmem, gate_vmem, lo_vmem, hi_vmem):
            for burst in range(SC_WINDOW // SC_BURST):
                t0 = burst * SC_BURST
                copies = [pltpu.make_async_copy(
                    table_hbm.at[idx_vmem.at[k, pl.ds(t0, SC_BURST)]], rows_vmem.at[k], sem)
                    for k in range(TOP_K)]
                for copy in copies:
                    copy.start()
                for copy in copies:
                    copy.wait()

                @pl.loop(0, SC_BURST)
                def _(r):
                    t = t0 + r
                    gates = [gate_vmem[k, t // 8, pl.ds((t % 8) * lanes, lanes)]
                             for k in range(TOP_K)]
                    for j in range(d // lanes):
                        acc_lo = jnp.zeros((lanes,), _F32)
                        acc_hi = jnp.zeros((lanes,), _F32)
                        for k in range(TOP_K):
                            w = rows_vmem[k, r, pl.ds(lanes * j, lanes)]
                            lo = lax.bitcast_convert_type(w << 16, _F32)
                            hi = lax.bitcast_convert_type(w & _U32(0xFFFF0000), _F32)
                            acc_lo = acc_lo + gates[k] * lo
                            acc_hi = acc_hi + gates[k] * hi
                        lo_vmem[t, pl.ds(lanes * j, lanes)] = acc_lo
                        hi_vmem[t, pl.ds(lanes * j, lanes)] = acc_hi

        pltpu.emit_pipeline(
            step,
            grid=(PIECES * wins,),
            in_specs=[pl.BlockSpec((TOP_K, SC_WINDOW), index_map=lambda i: (i // wins, i % wins)),
                      pl.BlockSpec((TOP_K, SC_WINDOW // 8, d), index_map=lambda i: (0, i % wins, 0))],
            out_specs=[pl.BlockSpec((SC_WINDOW, d), index_map=lambda i: (i, 0)),
                       pl.BlockSpec((SC_WINDOW, d), index_map=lambda i: (i, 0))],
            core_axis_name=("core", "subcore"),
            dimension_semantics=(pltpu.PARALLEL,),
        )(idx_hbm, gate_hbm, lo_hbm, hi_hbm)

    return pl.kernel(body, out_type=(out_type, out_type), mesh=_sc_mesh(),
                     scratch_types=[pltpu.VMEM((TOP_K, SC_BURST, d), _U32),
                                    pltpu.SemaphoreType.DMA],
                     compiler_params=pltpu.CompilerParams(needs_layout_passes=False),
                     name="sc_gather_sum")(table, idx, gate_lanes)


def _index_kernel(start_ref, ek_ref, rk_ref, idx_ref, *, n_rows):
    e = ek_ref[...]
    start = jnp.zeros(e.shape, jnp.int32)
    for ex in range(N_EXPERTS):
        start = jnp.where(e == ex, start_ref[ex], start)
    dest = start + rk_ref[...]
    for c in range(PIECES):
        idx_ref[c] = dest + c * n_rows


def _row_indices(padded_start, e_k, r_k, n_rows):
    n_tok = e_k.shape[1]
    ts = INDEX_TILE
    kspec = pl.BlockSpec((TOP_K, ts), lambda i, st: (0, i))
    grid_spec = pltpu.PrefetchScalarGridSpec(
        num_scalar_prefetch=1,
        grid=(n_tok // ts,),
        in_specs=[kspec, kspec],
        out_specs=pl.BlockSpec((PIECES, TOP_K, ts), lambda i, st: (0, 0, i)),
    )
    return pl.pallas_call(
        functools.partial(_index_kernel, n_rows=n_rows),
        grid_spec=grid_spec,
        out_shape=jax.ShapeDtypeStruct((PIECES, TOP_K, n_tok), jnp.int32),
        compiler_params=pltpu.CompilerParams(dimension_semantics=("arbitrary",)),
        name="moe_row_indices",
    )(padded_start, e_k, r_k)


def _expert_kernel(chunk0_ref, nchunk_ref, cnt_ref, total_ref,
                   xs_hbm, wg_ref, wu_ref, wd_ref, o_hbm,
                   xbuf, obuf, wgu_bf, wd_bf, sem_in, sem_out):
    e = pl.program_id(0)
    total = total_ref[0]
    ch = EXPERT_ROWS
    ahead = IN_SLOTS - 1

    def in_copy(g):
        slot = g % IN_SLOTS
        return pltpu.make_async_copy(xs_hbm.at[:, pl.ds(g * ch, ch), :], xbuf.at[slot],
                                     sem_in.at[slot])

    def out_copy(g):
        slot = g % OUT_SLOTS
        return pltpu.make_async_copy(obuf.at[slot], o_hbm.at[:, pl.ds(g * ch, ch), :],
                                     sem_out.at[slot])

    @pl.when(e == 0)
    def _():
        obuf[...] = jnp.zeros_like(obuf)
        for g in range(ahead):
            @pl.when(g < total)
            def _():
                in_copy(g).start()

    wgu_bf[:, :EXPERT_DIM] = wg_ref[0, 0].astype(_BF16)
    wgu_bf[:, EXPERT_DIM:] = wu_ref[0, 0].astype(_BF16)
    wd_bf[...] = wd_ref[0, 0].astype(_BF16)

    def chunk(j, carry):
        g = chunk0_ref[e] + j
        in_copy(g).wait()

        @pl.when(g + ahead < total)
        def _():
            in_copy(g + ahead).start()

        @pl.when(g >= OUT_SLOTS)
        def _():
            out_copy(g - OUT_SLOTS).wait()

        islot = g % IN_SLOTS
        oslot = g % OUT_SLOTS
        valid = cnt_ref[e] - j * ch

        def gate_up(s):
            r0 = s * EXPERT_SUB
            w = jnp.concatenate([xbuf[islot, c, r0:r0 + EXPERT_SUB, :] for c in range(PIECES)],
                                axis=1)
            lo, hi = _unpack_halves(w)
            return (_dot(lo.astype(_BF16), wgu_bf[:PACKED]) +
                    _dot(hi.astype(_BF16), wgu_bf[PACKED:]))

        def down(s, gu):
            r0 = s * EXPERT_SUB
            h = _silu(gu[:, :EXPERT_DIM]) * gu[:, EXPERT_DIM:]
            y = _pack_halves(_dot(h.astype(_BF16), wd_bf[...]))
            for c in range(PIECES):
                obuf[oslot, c, r0:r0 + EXPERT_SUB, :] = y[:, c * LANES:(c + 1) * LANES]

        n_sub = ch // EXPERT_SUB

        @pl.when(valid >= ch)
        def _():
            projected = [gate_up(s) for s in range(n_sub)]
            for s, gu in enumerate(projected):
                down(s, gu)

        @pl.when(valid < ch)
        def _():
            for s in range(n_sub):
                @pl.when(s * EXPERT_SUB < valid)
                def _():
                    down(s, gate_up(s))

        out_copy(g).start()
        return carry

    lax.fori_loop(0, nchunk_ref[e], chunk, 0)

    @pl.when(e == pl.num_programs(0) - 1)
    def _():
        for back in range(OUT_SLOTS, 0, -1):
            @pl.when(total >= back)
            def _():
                out_copy(total - back).wait()


def _experts(layer, chunk0, nchunk, counts, total, xs, w_gate, w_up, w_down):
    d = D_MODEL
    ch = EXPERT_ROWS
    wspec_in = pl.BlockSpec((1, 1, d, EXPERT_DIM), lambda e, *_: (layer, e, 0, 0))
    any_spec = pl.BlockSpec(memory_space=pl.ANY)
    grid_spec = pltpu.PrefetchScalarGridSpec(
        num_scalar_prefetch=4,
        grid=(N_EXPERTS,),
        in_specs=[
            any_spec,
            wspec_in,
            wspec_in,
            pl.BlockSpec((1, 1, EXPERT_DIM, d), lambda e, *_: (layer, e, 0, 0)),
        ],
        out_specs=any_spec,
        scratch_shapes=[pltpu.VMEM((IN_SLOTS, PIECES, ch, LANES), _U32),
                        pltpu.VMEM((OUT_SLOTS, PIECES, ch, LANES), _U32),
                        pltpu.VMEM((d, 2 * EXPERT_DIM), _BF16),
                        pltpu.VMEM((EXPERT_DIM, d), _BF16),
                        pltpu.SemaphoreType.DMA((IN_SLOTS,)),
                        pltpu.SemaphoreType.DMA((OUT_SLOTS,))],
    )
    return pl.pallas_call(
        _expert_kernel,
        grid_spec=grid_spec,
        out_shape=jax.ShapeDtypeStruct(xs.shape, _U32),
        compiler_params=pltpu.CompilerParams(
            dimension_semantics=("arbitrary",),
            vmem_limit_bytes=VMEM_LIMIT),
        name="moe_experts",
    )(chunk0, nchunk, counts, total, xs, w_gate, w_up, w_down)


def _combine_kernel(x_ref, lo_ref, hi_ref, sg_ref, su_ref, sd_ref, g_ref, b_ref, *rest):
    o_ref = rest[-1]
    x = x_ref[...]
    xb = x.astype(_BF16)
    routed = jnp.concatenate([lo_ref[c] for c in range(PIECES)] +
                             [hi_ref[c] for c in range(PIECES)], axis=1)
    h = _silu(_dot(xb, sg_ref[...])) * _dot(xb, su_ref[...])
    shared = _dot(h.astype(_BF16), sd_ref[...])
    o_ref[...] = _layer_norm(DEEPNORM_ALPHA * x + (routed + shared), g_ref[...], b_ref[...])


def _combine(x, routed_lo, routed_hi, sh_gate, sh_up, sh_down, ln_g, ln_b, out_parts, part, prev):
    n_part, d = x.shape
    ts = COMBINE_TILE
    steps = n_part // ts
    const2 = lambda i: (0, 0)
    in_specs = [
        pl.BlockSpec((ts, d), lambda i: (i, 0)),
        pl.BlockSpec((PIECES, ts, LANES), lambda i: (0, i, 0)),
        pl.BlockSpec((PIECES, ts, LANES), lambda i: (0, i, 0)),
        pl.BlockSpec((d, EXPERT_DIM), const2),
        pl.BlockSpec((d, EXPERT_DIM), const2),
        pl.BlockSpec((EXPERT_DIM, d), const2),
        pl.BlockSpec((1, d), const2),
        pl.BlockSpec((1, d), const2),
    ]
    args = [x, routed_lo, routed_hi, sh_gate.astype(_BF16), sh_up.astype(_BF16), sh_down.astype(_BF16),
            ln_g.reshape(1, d), ln_b.reshape(1, d)]
    aliases = {}
    if prev is not None:
        in_specs.append(pl.BlockSpec(memory_space=pl.ANY))
        aliases = {len(args): 0}
        args.append(prev)
    return pl.pallas_call(
        _combine_kernel,
        grid=(steps,),
        in_specs=in_specs,
        out_specs=pl.BlockSpec((ts, d), lambda i: (part * steps + i, 0)),
        out_shape=jax.ShapeDtypeStruct((out_parts * n_part, d), _F32),
        input_output_aliases=aliases,
        compiler_params=pltpu.CompilerParams(
            dimension_semantics=("arbitrary",),
            vmem_limit_bytes=VMEM_LIMIT),
        name="moe_combine",
    )(*args)


def _moe_dispatch(layer, x, x_packed, w_router, router_bias, w_gate, w_up, w_down):
    n_part, d = x.shape
    bm = EXPERT_ROWS
    n_rows = n_part * TOP_K + N_EXPERTS * bm
    e_k, r_k, g_k, counts = _route(x, w_router, router_bias)
    counts = counts.reshape(N_EXPERTS).astype(jnp.int32)
    padded = (counts + bm - 1) // bm * bm
    padded_end = jnp.cumsum(padded)
    padded_start = padded_end - padded
    idx = _row_indices(padded_start, e_k, r_k, n_rows).reshape(PIECES * TOP_K, n_part)
    xs = _sc_scatter_rows(x_packed.reshape(PIECES * n_part, LANES), idx, PIECES * n_rows)
    rows = _experts(layer, padded_start // bm, padded // bm, counts, padded_end[-1:] // bm,
                    xs.reshape(PIECES, n_rows, LANES), w_gate, w_up, w_down)
    return rows.reshape(PIECES * n_rows, LANES), idx, g_k


def _moe_combine(x, rows, idx, gates, sh_gate, sh_up, sh_down, ln_g, ln_b,
                 out_parts=1, part=0, prev=None):
    n_part = x.shape[0]
    gate_lanes = jnp.repeat(gates, SC_LANES, axis=1).reshape(TOP_K, n_part // 8, LANES)
    lo, hi = _sc_gather_sum(rows, idx, gate_lanes)
    return _combine(x, lo.reshape(PIECES, n_part, LANES), hi.reshape(PIECES, n_part, LANES),
                    sh_gate, sh_up, sh_down, ln_g, ln_b, out_parts, part, prev)


def kernel(x, pool_w_in, pool_w_grp, pool_scale, pool_w_out, sgu_w_in, sgu_b_in, sgu_ln_g, sgu_ln_b, sgu_w_s, sgu_b_s, sgu_w_out, ln_mix_g, ln_mix_b, moe_w_router, moe_router_bias, moe_w_gate, moe_w_up, moe_w_down, moe_sh_gate, moe_sh_up, moe_sh_down, ln_ffn_g, ln_ffn_b):
    bsz, seq, d = x.shape
    parts = range(TOKEN_PARTS)

    def dispatch(i, h, h_packed):
        return _moe_dispatch(i, h, h_packed, moe_w_router[i], moe_router_bias[i],
                             moe_w_gate, moe_w_up, moe_w_down)

    def combine(i, h, routed, **where):
        return _moe_combine(h, *routed, moe_sh_gate[i], moe_sh_up[i], moe_sh_down[i],
                            ln_ffn_g[i], ln_ffn_b[i], **where)

    mixed = [_pool_layer(x, p, pool_w_in[0], pool_w_grp[0], pool_scale[0], pool_w_out[0],
                         ln_mix_g[0], ln_mix_b[0]) for p in parts]
    mixed = [(h.reshape(-1, d), h_packed) for h, h_packed in mixed]
    routed = [dispatch(0, h, h_packed) for h, h_packed in mixed]
    hs = [combine(0, mixed[p][0], routed[p]) for p in parts]
    mixed = [_sgu_layer(h, sgu_w_in[0], sgu_b_in[0], sgu_ln_g[0], sgu_ln_b[0], sgu_w_s[0],
                        sgu_b_s[0], sgu_w_out[0], ln_mix_g[1], ln_mix_b[1]) for h in hs]
    routed = [dispatch(1, h, h_packed) for h, h_packed in mixed]
    out = None
    for p in parts:
        out = combine(1, mixed[p][0], routed[p], out_parts=TOKEN_PARTS, part=p, prev=out)
    return out.reshape(bsz, seq, d)
```

```python
import functools

import jax
import jax.numpy as jnp
from jax import lax
from jax.experimental import pallas as pl
from jax.experimental.pallas import tpu as pltpu
from jax.experimental.pallas import tpu_sc as plsc

D_MODEL = 1024
DEPTH = 2
POOL_WINDOWS = (2, 4, 8, 16)
POOL_GROUP_DIM = D_MODEL // len(POOL_WINDOWS)
POOL_HALO = 16
SGU_CHUNK = 128
SGU_HEADS = 4
SGU_WIDTH = 2 * D_MODEL
SGU_HEAD_DIM = SGU_WIDTH // SGU_HEADS
N_EXPERTS = 64
TOP_K = 8
N_GROUPS = 8
GROUP_SIZE = N_EXPERTS // N_GROUPS
TOPK_GROUPS = 4
EXPERT_DIM = D_MODEL // 4
ROUTED_SCALE = 2.5
DEEPNORM_ALPHA = (2 * DEPTH) ** 0.25
LN_EPS = 1e-5

LANES = 128
PACKED = D_MODEL // 2
PIECES = PACKED // LANES

POOL_TILE = 1024
POOL_SUB = 256
SGU_TILE = 512
SGU_SUB = 256
ROUTE_TILE = 1024
ROUTE_SUB = 512
INDEX_TILE = 2048
EXPERT_ROWS = 512
EXPERT_SUB = 256
IN_SLOTS = 6
OUT_SLOTS = 4
COMBINE_TILE = 512
SC_WINDOW = 128
SC_LANES = 16
SC_BURST = 32
TOKEN_PARTS = 2
VMEM_LIMIT = 56 * 1024 * 1024

_F32 = jnp.float32
_BF16 = jnp.bfloat16
_U32 = jnp.uint32


def _dot(a, b):
    return jnp.dot(a, b, preferred_element_type=_F32)


def _layer_norm(h, g, b):
    mu = jnp.mean(h, axis=-1, keepdims=True)
    hc = h - mu
    var = jnp.mean(hc * hc, axis=-1, keepdims=True)
    return hc * lax.rsqrt(var + LN_EPS) * g + b


def _silu(x):
    return x * jax.nn.sigmoid(x)


def _gelu_tanh(x):
    c = 0.7978845608028654
    return 0.5 * x * (1.0 + jnp.tanh(c * (x + 0.044715 * (x * x * x))))


def _pack_halves(v):
    half = v.shape[1] // 2
    lo = lax.bitcast_convert_type(v[:, :half].astype(_BF16).astype(_F32), _U32)
    hi = lax.bitcast_convert_type(v[:, half:].astype(_BF16).astype(_F32), _U32)
    return (hi & _U32(0xFFFF0000)) | (lo >> 16)


def _unpack_halves(w):
    lo = lax.bitcast_convert_type(w << 16, _F32)
    hi = lax.bitcast_convert_type(w & _U32(0xFFFF0000), _F32)
    return lo, hi


def _store_pieces(ref, r0, w):
    for c in range(PIECES):
        ref[c, r0:r0 + w.shape[0], :] = w[:, c * LANES:(c + 1) * LANES]


def _pool_kernel(x_ref, win_ref, wgrp_ref, scale_ref, wout_ref, g_ref, b_ref,
                 o_ref, op_ref, zs_ref, y_ref):
    s = pl.program_id(1)
    ts = x_ref.shape[1]

    @pl.when(s == 0)
    def _():
        zs_ref[0:POOL_HALO, :] = jnp.zeros((POOL_HALO, D_MODEL), _F32)

    for p0 in range(0, ts, POOL_SUB):
        z = _dot(x_ref[0, p0:p0 + POOL_SUB, :].astype(_BF16), win_ref[...])
        zs_ref[POOL_HALO + p0:POOL_HALO + p0 + POOL_SUB, :] = z
    for p0 in range(0, ts, POOL_SUB):
        x = x_ref[0, p0:p0 + POOL_SUB, :]
        base = POOL_HALO + p0
        pos = s * ts + p0 + lax.broadcasted_iota(jnp.int32, (POOL_SUB, 1), 0)
        for g, w in enumerate(POOL_WINDOWS):
            c0 = g * POOL_GROUP_DIM
            c1 = c0 + POOL_GROUP_DIM
            zg = zs_ref[base:base + POOL_SUB, c0:c1]
            acc = zg
            for k in range(1, w):
                acc = acc + zs_ref[base - k:base - k + POOL_SUB, c0:c1]
            cnt = jnp.minimum(pos + 1, w).astype(_F32)
            pooled = acc / cnt - zg
            yg = _dot(pooled.astype(_BF16), wgrp_ref[g]) * scale_ref[:, c0:c1]
            y_ref[p0:p0 + POOL_SUB, c0:c1] = yg.astype(_BF16)
        mix = _dot(y_ref[p0:p0 + POOL_SUB, :], wout_ref[...])
        out = _layer_norm(DEEPNORM_ALPHA * x + mix, g_ref[...], b_ref[...])
        o_ref[0, p0:p0 + POOL_SUB, :] = out
        _store_pieces(op_ref, p0, _pack_halves(out))
    zs_ref[0:POOL_HALO, :] = zs_ref[ts:ts + POOL_HALO, :]


def _pool_layer(x, part, w_in, w_grp, scale, w_out, ln_g, ln_b):
    bsz, seq, d = x.shape
    rows = bsz // TOKEN_PARTS
    ts = POOL_TILE
    steps = seq // ts
    const2 = lambda b, s: (0, 0)
    out_shape = (jax.ShapeDtypeStruct((rows, seq, d), _F32),
                 jax.ShapeDtypeStruct((PIECES, rows * seq, LANES), _U32))
    tile = pl.BlockSpec((1, ts, d), lambda b, s: (b, s, 0))
    ptile = pl.BlockSpec((PIECES, ts, LANES), lambda b, s: (0, b * steps + s, 0))
    return pl.pallas_call(
        _pool_kernel,
        grid=(rows, steps),
        in_specs=[
            pl.BlockSpec((1, ts, d), lambda b, s: (part * rows + b, s, 0)),
            pl.BlockSpec((d, d), const2),
            pl.BlockSpec((len(POOL_WINDOWS), POOL_GROUP_DIM, POOL_GROUP_DIM), lambda b, s: (0, 0, 0)),
            pl.BlockSpec((1, d), const2),
            pl.BlockSpec((d, d), const2),
            pl.BlockSpec((1, d), const2),
            pl.BlockSpec((1, d), const2),
        ],
        out_specs=(tile, ptile),
        out_shape=out_shape,
        scratch_shapes=[pltpu.VMEM((POOL_HALO + ts, d), _F32),
                        pltpu.VMEM((ts, d), _BF16)],
        compiler_params=pltpu.CompilerParams(
            dimension_semantics=("arbitrary", "arbitrary"),
            vmem_limit_bytes=VMEM_LIMIT),
        name="pool_mixer",
    )(x, w_in.astype(_BF16), w_grp.astype(_BF16), scale.reshape(1, d),
      w_out.astype(_BF16), ln_g.reshape(1, d), ln_b.reshape(1, d))


def _sgu_kernel(x_ref, win_ref, bin_ref, lng_ref, lnb_ref, ws_ref, bs_ref, wout_ref,
                g_ref, b_ref, o_ref, op_ref, gated_ref):
    ts = x_ref.shape[0]
    projected = []
    for p0 in range(0, ts, SGU_SUB):
        x = x_ref[p0:p0 + SGU_SUB, :]
        xb = x.astype(_BF16)
        v = _gelu_tanh(_dot(xb, win_ref[:, SGU_WIDTH:]) + bin_ref[:, SGU_WIDTH:])
        v = _layer_norm(v, lng_ref[...], lnb_ref[...]).astype(_BF16)
        u = _gelu_tanh(_dot(xb, win_ref[:, :SGU_WIDTH]) + bin_ref[:, :SGU_WIDTH])
        projected.append((p0, x, u, v))
    for p0, x, u, v in projected:
        for r0 in range(0, SGU_SUB, SGU_CHUNK):
            for h in range(SGU_HEADS):
                c0 = h * SGU_HEAD_DIM
                sv = _dot(ws_ref[h], v[r0:r0 + SGU_CHUNK, c0:c0 + SGU_HEAD_DIM]) + bs_ref[h]
                gated_ref[p0 + r0:p0 + r0 + SGU_CHUNK, c0:c0 + SGU_HEAD_DIM] = (
                    u[r0:r0 + SGU_CHUNK, c0:c0 + SGU_HEAD_DIM] * sv).astype(_BF16)
        mix = _dot(gated_ref[p0:p0 + SGU_SUB, :], wout_ref[...])
        out = _layer_norm(DEEPNORM_ALPHA * x + mix, g_ref[...], b_ref[...])
        o_ref[p0:p0 + SGU_SUB, :] = out
        _store_pieces(op_ref, p0, _pack_halves(out))


def _sgu_layer(x, w_in, b_in, ln_g, ln_b, w_s, b_s, w_out, mix_g, mix_b):
    n_tok, d = x.shape
    ts = SGU_TILE
    const2 = lambda i: (0, 0)
    const3 = lambda i: (0, 0, 0)
    causal = jnp.tril(jnp.ones((SGU_CHUNK, SGU_CHUNK), w_s.dtype))
    ws = (w_s * causal[None]).astype(_BF16)
    tile = pl.BlockSpec((ts, d), lambda i: (i, 0))
    ptile = pl.BlockSpec((PIECES, ts, LANES), lambda i: (0, i, 0))
    out_shape = (jax.ShapeDtypeStruct((n_tok, d), _F32),
                 jax.ShapeDtypeStruct((PIECES, n_tok, LANES), _U32))
    return pl.pallas_call(
        _sgu_kernel,
        grid=(n_tok // ts,),
        in_specs=[
            tile,
            pl.BlockSpec((d, 2 * SGU_WIDTH), const2),
            pl.BlockSpec((1, 2 * SGU_WIDTH), const2),
            pl.BlockSpec((1, SGU_WIDTH), const2),
            pl.BlockSpec((1, SGU_WIDTH), const2),
            pl.BlockSpec((SGU_HEADS, SGU_CHUNK, SGU_CHUNK), const3),
            pl.BlockSpec((SGU_HEADS, SGU_CHUNK, 1), const3),
            pl.BlockSpec((SGU_WIDTH, d), const2),
            pl.BlockSpec((1, d), const2),
            pl.BlockSpec((1, d), const2),
        ],
        out_specs=(tile, ptile),
        out_shape=out_shape,
        scratch_shapes=[pltpu.VMEM((ts, SGU_WIDTH), _BF16)],
        compiler_params=pltpu.CompilerParams(
            dimension_semantics=("arbitrary",),
            vmem_limit_bytes=VMEM_LIMIT),
        name="sgu_mixer",
    )(x, w_in.astype(_BF16), b_in.reshape(1, -1), ln_g.reshape(1, -1), ln_b.reshape(1, -1),
      ws, b_s.reshape(SGU_HEADS, SGU_CHUNK, 1), w_out.astype(_BF16),
      mix_g.reshape(1, d), mix_b.reshape(1, d))


def _all_max(a):
    return jnp.max(jnp.max(a, axis=0, keepdims=True), axis=1, keepdims=True)


def _all_min(a):
    return jnp.min(jnp.min(a, axis=0, keepdims=True), axis=1, keepdims=True)


def _all_sum(a):
    return jnp.sum(jnp.sum(a, axis=0, keepdims=True), axis=1, keepdims=True)


def _route_kernel(x_ref, whi_ref, wlo_ref, bias_ref, earlier_ref,
                  ek_ref, rk_ref, gk_ref, cnt_ref, carry_ref):
    @pl.when(pl.program_id(0) == 0)
    def _():
        carry_ref[...] = jnp.zeros_like(carry_ref)

    nt = lambda a, b: lax.dot_general(a, b, (((1,), (1,)), ((), ())),
                                      preferred_element_type=_F32)
    logits = []
    for p0 in range(0, x_ref.shape[0], ROUTE_SUB):
        x = x_ref[p0:p0 + ROUTE_SUB, :]
        x_hi = x.astype(_BF16)
        x_lo = (x - x_hi.astype(_F32)).astype(_BF16)
        logits.append(nt(whi_ref[...], x_hi) + (nt(whi_ref[...], x_lo) + nt(wlo_ref[...], x_hi)))
    for n, sub_logits in enumerate(logits):
        _route_sub_tile(sub_logits, n * ROUTE_SUB, bias_ref, earlier_ref,
                        ek_ref, rk_ref, gk_ref, carry_ref)
    cnt_ref[...] = carry_ref[...]


def _route_sub_tile(logits, p0, bias_ref, earlier_ref, ek_ref, rk_ref, gk_ref, carry_ref):
    ts = ROUTE_SUB
    scores = jax.nn.sigmoid(logits).reshape(N_GROUPS, GROUP_SIZE, ts)
    biased = scores + bias_ref[...]
    neg_inf = jnp.float32(-jnp.inf)
    shape3 = (N_GROUPS, GROUP_SIZE, ts)
    in_grp = lax.broadcasted_iota(jnp.int32, shape3, 1)
    grp = lax.broadcasted_iota(jnp.int32, shape3, 0)
    eid = grp * GROUP_SIZE + in_grp

    m1 = jnp.max(biased, axis=1, keepdims=True)
    first1 = jnp.min(jnp.where(biased == m1, in_grp, GROUP_SIZE), axis=1, keepdims=True)
    m2 = jnp.max(jnp.where(in_grp == first1, neg_inf, biased), axis=1, keepdims=True)
    gscore = m1 + m2

    gid = lax.broadcasted_iota(jnp.int32, (N_GROUPS, 1, ts), 0)
    gsel = jnp.zeros((N_GROUPS, 1, ts), jnp.bool_)
    for _ in range(TOPK_GROUPS):
        m = jnp.max(gscore, axis=0, keepdims=True)
        first = jnp.min(jnp.where(gscore == m, gid, N_GROUPS), axis=0, keepdims=True)
        pick = gid == first
        gsel = jnp.logical_or(gsel, pick)
        gscore = jnp.where(pick, neg_inf, gscore)

    masked = jnp.where(gsel, biased, neg_inf)
    picked_any = jnp.zeros(shape3, jnp.bool_)
    e_k, s_k = [], []
    for _ in range(TOP_K):
        m = _all_max(masked)
        first = _all_min(jnp.where(masked == m, eid, N_EXPERTS))
        pick = eid == first
        picked_any = jnp.logical_or(picked_any, pick)
        masked = jnp.where(pick, neg_inf, masked)
        e_k.append(first)
        s_k.append(_all_sum(jnp.where(pick, scores, 0.0)))
    denom = s_k[0]
    for k in range(1, TOP_K):
        denom = denom + s_k[k]

    sel = picked_any.astype(_F32).reshape(N_EXPERTS, ts)
    rank = _dot(sel.astype(_BF16), earlier_ref[...]) + carry_ref[...]
    rank3 = rank.reshape(shape3)
    carry_ref[...] += jnp.sum(sel, axis=1, keepdims=True)

    for k in range(TOP_K):
        r = _all_sum(jnp.where(eid == e_k[k], rank3, 0.0))
        ek_ref[k:k + 1, p0:p0 + ts] = e_k[k].reshape(1, ts)
        rk_ref[k:k + 1, p0:p0 + ts] = r.reshape(1, ts).astype(jnp.int32)
        gk_ref[k:k + 1, p0:p0 + ts] = (s_k[k] / denom * ROUTED_SCALE).reshape(1, ts)


def _route(x, w_router, router_bias):
    n_part, d = x.shape
    ts = ROUTE_TILE
    steps = n_part // ts
    out_shape = (jax.ShapeDtypeStruct((TOP_K, n_part), jnp.int32),
                 jax.ShapeDtypeStruct((TOP_K, n_part), jnp.int32),
                 jax.ShapeDtypeStruct((TOP_K, n_part), _F32),
                 jax.ShapeDtypeStruct((N_EXPERTS, 1), _F32))
    kspec = pl.BlockSpec((TOP_K, ts), lambda i: (0, i))
    const2 = lambda i: (0, 0)
    w_t = w_router.T
    w_hi = w_t.astype(_BF16)
    w_lo = (w_t - w_hi.astype(_F32)).astype(_BF16)
    pos = jnp.arange(ROUTE_SUB, dtype=jnp.int32)
    earlier = (pos[:, None] < pos[None, :]).astype(_BF16)
    return pl.pallas_call(
        _route_kernel,
        grid=(steps,),
        in_specs=[
            pl.BlockSpec((ts, d), lambda i: (i, 0)),
            pl.BlockSpec((N_EXPERTS, d), const2),
            pl.BlockSpec((N_EXPERTS, d), const2),
            pl.BlockSpec((N_GROUPS, GROUP_SIZE, 1), lambda i: (0, 0, 0)),
            pl.BlockSpec((ROUTE_SUB, ROUTE_SUB), const2),
        ],
        out_specs=(kspec, kspec, kspec, pl.BlockSpec((N_EXPERTS, 1), const2)),
        out_shape=out_shape,
        scratch_shapes=[pltpu.VMEM((N_EXPERTS, 1), _F32)],
        compiler_params=pltpu.CompilerParams(
            dimension_semantics=("arbitrary",),
            vmem_limit_bytes=VMEM_LIMIT),
        name="moe_route",
    )(x, w_hi, w_lo, router_bias.reshape(N_GROUPS, GROUP_SIZE, 1), earlier)


def _sc_mesh():
    return plsc.VectorSubcoreMesh(core_axis_name="core", subcore_axis_name="subcore")


def _sc_scatter_rows(src, idx, n_out):
    d = src.shape[1]
    wins = idx.shape[1] // SC_WINDOW

    def body(src_hbm, idx_hbm, out_hbm):
        def step(src_vmem, idx_vmem):
            for k in range(TOP_K):
                pltpu.sync_copy(src_vmem, out_hbm.at[idx_vmem.at[k]])

        pltpu.emit_pipeline(
            step,
            grid=(src.shape[0] // SC_WINDOW,),
            in_specs=[pl.BlockSpec((SC_WINDOW, d), index_map=lambda i: (i, 0)),
                      pl.BlockSpec((TOP_K, SC_WINDOW), index_map=lambda i: (i // wins, i % wins))],
            out_specs=[],
            core_axis_name=("core", "subcore"),
            dimension_semantics=(pltpu.PARALLEL,),
        )(src_hbm, idx_hbm)

    return pl.kernel(body, out_type=jax.ShapeDtypeStruct((n_out, d), src.dtype),
                     mesh=_sc_mesh(), scratch_types=[], name="sc_scatter_rows")(src, idx)


def _sc_gather_sum(table, idx, gates):
    d = table.shape[1]
    n_tok = idx.shape[1]
    wins = n_tok // SC_WINDOW
    lanes = SC_LANES
    out_type = jax.ShapeDtypeStruct((PIECES * n_tok, d), _F32)

    def tree_sum(terms):
        while len(terms) > 1:
            terms = [terms[i] + terms[i + 1] for i in range(0, len(terms), 2)]
        return terms[0]

    def body(table_hbm, idx_hbm, gate_hbm, lo_hbm, hi_hbm, rows_vmem, sem):
        def step(idx_vmem, gate_vmem, lo_vmem, hi_vmem):
            for t0 in range(0, SC_WINDOW, SC_BURST):
                copies = [pltpu.make_async_copy(
                    table_hbm.at[idx_vmem.at[k, pl.ds(t0, SC_BURST)]], rows_vmem.at[k], sem)
                    for k in range(TOP_K)]
                for copy in copies:
                    copy.start()
                for copy in copies:
                    copy.wait()

                @plsc.parallel_loop(0, SC_BURST)
                def _(r):
                    t = t0 + r
                    token = jnp.full((lanes,), t, jnp.int32)
                    gate = [plsc.load_gather(gate_vmem, [jnp.full((lanes,), k, jnp.int32), token])
                            for k in range(TOP_K)]
                    for j in range(0, d, lanes):
                        words = [rows_vmem[k, r, pl.ds(j, lanes)] for k in range(TOP_K)]
                        lo_vmem[t, pl.ds(j, lanes)] = tree_sum(
                            [gate[k] * lax.bitcast_convert_type(words[k] << 16, _F32)
                             for k in range(TOP_K)])
                        hi_vmem[t, pl.ds(j, lanes)] = tree_sum(
                            [gate[k] * lax.bitcast_convert_type(words[k] & _U32(0xFFFF0000), _F32)
                             for k in range(TOP_K)])

        window = lambda i: (i // wins, i % wins)
        pltpu.emit_pipeline(
            step,
            grid=(PIECES * wins,),
            in_specs=[pl.BlockSpec((TOP_K, SC_WINDOW), index_map=window),
                      pl.BlockSpec((TOP_K, SC_WINDOW), index_map=lambda i: (0, i % wins))],
            out_specs=[pl.BlockSpec((SC_WINDOW, d), index_map=lambda i: (i, 0)),
                       pl.BlockSpec((SC_WINDOW, d), index_map=lambda i: (i, 0))],
            core_axis_name=("core", "subcore"),
            dimension_semantics=(pltpu.PARALLEL,),
        )(idx_hbm, gate_hbm, lo_hbm, hi_hbm)

    return pl.kernel(body, out_type=(out_type, out_type), mesh=_sc_mesh(),
                     scratch_types=[pltpu.VMEM((TOP_K, SC_BURST, d), _U32),
                                    pltpu.SemaphoreType.DMA],
                     compiler_params=pltpu.CompilerParams(needs_layout_passes=False),
                     name="sc_gather_sum")(table, idx, gates)


def _index_kernel(start_ref, ek_ref, rk_ref, idx_ref, *, n_rows):
    e = ek_ref[...]
    start = jnp.zeros(e.shape, jnp.int32)
    for ex in range(N_EXPERTS):
        start = jnp.where(e == ex, start_ref[ex], start)
    dest = start + rk_ref[...]
    for c in range(PIECES):
        idx_ref[c] = dest + c * n_rows


def _row_indices(padded_start, e_k, r_k, n_rows):
    n_tok = e_k.shape[1]
    ts = INDEX_TILE
    kspec = pl.BlockSpec((TOP_K, ts), lambda i, st: (0, i))
    grid_spec = pltpu.PrefetchScalarGridSpec(
        num_scalar_prefetch=1,
        grid=(n_tok // ts,),
        in_specs=[kspec, kspec],
        out_specs=pl.BlockSpec((PIECES, TOP_K, ts), lambda i, st: (0, 0, i)),
    )
    return pl.pallas_call(
        functools.partial(_index_kernel, n_rows=n_rows),
        grid_spec=grid_spec,
        out_shape=jax.ShapeDtypeStruct((PIECES, TOP_K, n_tok), jnp.int32),
        compiler_params=pltpu.CompilerParams(dimension_semantics=("arbitrary",)),
        name="moe_row_indices",
    )(padded_start, e_k, r_k)


def _expert_kernel(chunk0_ref, nchunk_ref, cnt_ref, total_ref,
                   xs_hbm, wg_ref, wu_ref, wd_ref, o_hbm,
                   xbuf, obuf, wgu_bf, wd_bf, sem_in, sem_out):
    e = pl.program_id(0)
    total = total_ref[0]
    ch = EXPERT_ROWS
    ahead = IN_SLOTS - 1

    def in_copy(g):
        slot = g % IN_SLOTS
        return pltpu.make_async_copy(xs_hbm.at[:, pl.ds(g * ch, ch), :], xbuf.at[slot],
                                     sem_in.at[slot])

    def out_copy(g):
        slot = g % OUT_SLOTS
        return pltpu.make_async_copy(obuf.at[slot], o_hbm.at[:, pl.ds(g * ch, ch), :],
                                     sem_out.at[slot])

    @pl.when(e == 0)
    def _():
        obuf[...] = jnp.zeros_like(obuf)
        for g in range(ahead):
            @pl.when(g < total)
            def _():
                in_copy(g).start()

    wgu_bf[:, :EXPERT_DIM] = wg_ref[0, 0].astype(_BF16)
    wgu_bf[:, EXPERT_DIM:] = wu_ref[0, 0].astype(_BF16)
    wd_bf[...] = wd_ref[0, 0].astype(_BF16)

    def chunk(j, carry):
        g = chunk0_ref[e] + j
        in_copy(g).wait()

        @pl.when(g + ahead < total)
        def _():
            in_copy(g + ahead).start()

        @pl.when(g >= OUT_SLOTS)
        def _():
            out_copy(g - OUT_SLOTS).wait()

        islot = g % IN_SLOTS
        oslot = g % OUT_SLOTS
        valid = cnt_ref[e] - j * ch

        def gate_up(s):
            r0 = s * EXPERT_SUB
            w = jnp.concatenate([xbuf[islot, c, r0:r0 + EXPERT_SUB, :] for c in range(PIECES)],
                                axis=1)
            lo, hi = _unpack_halves(w)
            return (_dot(lo.astype(_BF16), wgu_bf[:PACKED]) +
                    _dot(hi.astype(_BF16), wgu_bf[PACKED:]))

        def down(s, gu):
            r0 = s * EXPERT_SUB
            h = _silu(gu[:, :EXPERT_DIM]) * gu[:, EXPERT_DIM:]
            y = _pack_halves(_dot(h.astype(_BF16), wd_bf[...]))
            for c in range(PIECES):
                obuf[oslot, c, r0:r0 + EXPERT_SUB, :] = y[:, c * LANES:(c + 1) * LANES]

        n_sub = ch // EXPERT_SUB

        @pl.when(valid >= ch)
        def _():
            projected = [gate_up(s) for s in range(n_sub)]
            for s, gu in enumerate(projected):
                down(s, gu)

        @pl.when(valid < ch)
        def _():
            for s in range(n_sub):
                @pl.when(s * EXPERT_SUB < valid)
                def _():
                    down(s, gate_up(s))

        out_copy(g).start()
        return carry

    lax.fori_loop(0, nchunk_ref[e], chunk, 0)

    @pl.when(e == pl.num_programs(0) - 1)
    def _():
        for back in range(OUT_SLOTS, 0, -1):
            @pl.when(total >= back)
            def _():
                out_copy(total - back).wait()


def _experts(layer, chunk0, nchunk, counts, total, xs, w_gate, w_up, w_down):
    d = D_MODEL
    ch = EXPERT_ROWS
    wspec_in = pl.BlockSpec((1, 1, d, EXPERT_DIM), lambda e, *_: (layer, e, 0, 0))
    any_spec = pl.BlockSpec(memory_space=pl.ANY)
    grid_spec = pltpu.PrefetchScalarGridSpec(
        num_scalar_prefetch=4,
        grid=(N_EXPERTS,),
        in_specs=[
            any_spec,
            wspec_in,
            wspec_in,
            pl.BlockSpec((1, 1, EXPERT_DIM, d), lambda e, *_: (layer, e, 0, 0)),
        ],
        out_specs=any_spec,
        scratch_shapes=[pltpu.VMEM((IN_SLOTS, PIECES, ch, LANES), _U32),
                        pltpu.VMEM((OUT_SLOTS, PIECES, ch, LANES), _U32),
                        pltpu.VMEM((d, 2 * EXPERT_DIM), _BF16),
                        pltpu.VMEM((EXPERT_DIM, d), _BF16),
                        pltpu.SemaphoreType.DMA((IN_SLOTS,)),
                        pltpu.SemaphoreType.DMA((OUT_SLOTS,))],
    )
    return pl.pallas_call(
        _expert_kernel,
        grid_spec=grid_spec,
        out_shape=jax.ShapeDtypeStruct(xs.shape, _U32),
        compiler_params=pltpu.CompilerParams(
            dimension_semantics=("arbitrary",),
            vmem_limit_bytes=VMEM_LIMIT),
        name="moe_experts",
    )(chunk0, nchunk, counts, total, xs, w_gate, w_up, w_down)


def _combine_kernel(x_ref, lo_ref, hi_ref, sg_ref, su_ref, sd_ref, g_ref, b_ref, *rest):
    o_ref = rest[-1]
    x = x_ref[...]
    xb = x.astype(_BF16)
    routed = jnp.concatenate([lo_ref[c] for c in range(PIECES)] +
                             [hi_ref[c] for c in range(PIECES)], axis=1)
    h = _silu(_dot(xb, sg_ref[...])) * _dot(xb, su_ref[...])
    shared = _dot(h.astype(_BF16), sd_ref[...])
    o_ref[...] = _layer_norm(DEEPNORM_ALPHA * x + (routed + shared), g_ref[...], b_ref[...])


def _combine(x, routed_lo, routed_hi, sh_gate, sh_up, sh_down, ln_g, ln_b, out_parts, part, prev):
    n_part, d = x.shape
    ts = COMBINE_TILE
    steps = n_part // ts
    const2 = lambda i: (0, 0)
    in_specs = [
        pl.BlockSpec((ts, d), lambda i: (i, 0)),
        pl.BlockSpec((PIECES, ts, LANES), lambda i: (0, i, 0)),
        pl.BlockSpec((PIECES, ts, LANES), lambda i: (0, i, 0)),
        pl.BlockSpec((d, EXPERT_DIM), const2),
        pl.BlockSpec((d, EXPERT_DIM), const2),
        pl.BlockSpec((EXPERT_DIM, d), const2),
        pl.BlockSpec((1, d), const2),
        pl.BlockSpec((1, d), const2),
    ]
    args = [x, routed_lo, routed_hi, sh_gate.astype(_BF16), sh_up.astype(_BF16), sh_down.astype(_BF16),
            ln_g.reshape(1, d), ln_b.reshape(1, d)]
    aliases = {}
    if prev is not None:
        in_specs.append(pl.BlockSpec(memory_space=pl.ANY))
        aliases = {len(args): 0}
        args.append(prev)
    return pl.pallas_call(
        _combine_kernel,
        grid=(steps,),
        in_specs=in_specs,
        out_specs=pl.BlockSpec((ts, d), lambda i: (part * steps + i, 0)),
        out_shape=jax.ShapeDtypeStruct((out_parts * n_part, d), _F32),
        input_output_aliases=aliases,
        compiler_params=pltpu.CompilerParams(
            dimension_semantics=("arbitrary",),
            vmem_limit_bytes=VMEM_LIMIT),
        name="moe_combine",
    )(*args)


def _moe_dispatch(layer, x, x_packed, w_router, router_bias, w_gate, w_up, w_down):
    n_part, d = x.shape
    bm = EXPERT_ROWS
    n_rows = n_part * TOP_K + N_EXPERTS * bm
    e_k, r_k, g_k, counts = _route(x, w_router, router_bias)
    counts = counts.reshape(N_EXPERTS).astype(jnp.int32)
    padded = (counts + bm - 1) // bm * bm
    padded_end = jnp.cumsum(padded)
    padded_start = padded_end - padded
    idx = _row_indices(padded_start, e_k, r_k, n_rows).reshape(PIECES * TOP_K, n_part)
    xs = _sc_scatter_rows(x_packed.reshape(PIECES * n_part, LANES), idx, PIECES * n_rows)
    rows = _experts(layer, padded_start // bm, padded // bm, counts, padded_end[-1:] // bm,
                    xs.reshape(PIECES, n_rows, LANES), w_gate, w_up, w_down)
    return rows.reshape(PIECES * n_rows, LANES), idx, g_k


def _moe_combine(x, rows, idx, gates, sh_gate, sh_up, sh_down, ln_g, ln_b,
                 out_parts=1, part=0, prev=None):
    n_part = x.shape[0]
    lo, hi = _sc_gather_sum(rows, idx, gates)
    return _combine(x, lo.reshape(PIECES, n_part, LANES), hi.reshape(PIECES, n_part, LANES),
                    sh_gate, sh_up, sh_down, ln_g, ln_b, out_parts, part, prev)


def kernel(x, pool_w_in, pool_w_grp, pool_scale, pool_w_out, sgu_w_in, sgu_b_in, sgu_ln_g, sgu_ln_b, sgu_w_s, sgu_b_s, sgu_w_out, ln_mix_g, ln_mix_b, moe_w_router, moe_router_bias, moe_w_gate, moe_w_up, moe_w_down, moe_sh_gate, moe_sh_up, moe_sh_down, ln_ffn_g, ln_ffn_b):
    bsz, seq, d = x.shape
    parts = range(TOKEN_PARTS)

    def dispatch(i, h, h_packed):
        return _moe_dispatch(i, h, h_packed, moe_w_router[i], moe_router_bias[i],
                             moe_w_gate, moe_w_up, moe_w_down)

    def combine(i, h, routed, **where):
        return _moe_combine(h, *routed, moe_sh_gate[i], moe_sh_up[i], moe_sh_down[i],
                            ln_ffn_g[i], ln_ffn_b[i], **where)

    mixed = [_pool_layer(x, p, pool_w_in[0], pool_w_grp[0], pool_scale[0], pool_w_out[0],
                         ln_mix_g[0], ln_mix_b[0]) for p in parts]
    mixed = [(h.reshape(-1, d), h_packed) for h, h_packed in mixed]
    routed = [dispatch(0, h, h_packed) for h, h_packed in mixed]
    hs = [combine(0, mixed[p][0], routed[p]) for p in parts]
    mixed = [_sgu_layer(h, sgu_w_in[0], sgu_b_in[0], sgu_ln_g[0], sgu_ln_b[0], sgu_w_s[0],
                        sgu_b_s[0], sgu_w_out[0], ln_mix_g[1], ln_mix_b[1]) for h in hs]
    routed = [dispatch(1, h, h_packed) for h, h_packed in mixed]
    out = None
    for p in parts:
        out = combine(1, mixed[p][0], routed[p], out_parts=TOKEN_PARTS, part=p, prev=out)
    return out.reshape(bsz, seq, d)
```

```python
import functools

import jax
import jax.numpy as jnp
from jax import lax
from jax.experimental import pallas as pl
from jax.experimental.pallas import tpu as pltpu
from jax.experimental.pallas import tpu_sc as plsc

D_MODEL = 1024
DEPTH = 2
POOL_WINDOWS = (2, 4, 8, 16)
POOL_GROUP_DIM = D_MODEL // len(POOL_WINDOWS)
POOL_HALO = 16
SGU_CHUNK = 128
SGU_HEADS = 4
SGU_WIDTH = 2 * D_MODEL
SGU_HEAD_DIM = SGU_WIDTH // SGU_HEADS
N_EXPERTS = 64
TOP_K = 8
N_GROUPS = 8
GROUP_SIZE = N_EXPERTS // N_GROUPS
TOPK_GROUPS = 4
EXPERT_DIM = D_MODEL // 4
ROUTED_SCALE = 2.5
DEEPNORM_ALPHA = (2 * DEPTH) ** 0.25
LN_EPS = 1e-5

LANES = 128
PACKED = D_MODEL // 2
PIECES = PACKED // LANES

POOL_TILE = 1024
POOL_SUB = 256
SGU_TILE = 512
SGU_SUB = 256
ROUTE_TILE = 1024
ROUTE_SUB = 512
INDEX_TILE = 2048
EXPERT_ROWS = 512
EXPERT_SUB = 256
IN_SLOTS = 6
OUT_SLOTS = 4
COMBINE_TILE = 512
SC_WINDOW = 128
SC_LANES = 16
SC_BURST = 32
TOKEN_PARTS = 2
VMEM_LIMIT = 56 * 1024 * 1024

_F32 = jnp.float32
_BF16 = jnp.bfloat16
_U32 = jnp.uint32


def _dot(a, b):
    return jnp.dot(a, b, preferred_element_type=_F32)


def _layer_norm(h, g, b):
    mu = jnp.mean(h, axis=-1, keepdims=True)
    hc = h - mu
    var = jnp.mean(hc * hc, axis=-1, keepdims=True)
    return hc * lax.rsqrt(var + LN_EPS) * g + b


def _silu(x):
    return x * jax.nn.sigmoid(x)


def _gelu_tanh(x):
    c = 0.7978845608028654
    return 0.5 * x * (1.0 + jnp.tanh(c * (x + 0.044715 * (x * x * x))))


def _pack_halves(v):
    half = v.shape[1] // 2
    lo = lax.bitcast_convert_type(v[:, :half].astype(_BF16).astype(_F32), _U32)
    hi = lax.bitcast_convert_type(v[:, half:].astype(_BF16).astype(_F32), _U32)
    return (hi & _U32(0xFFFF0000)) | (lo >> 16)


def _unpack_halves(w):
    lo = lax.bitcast_convert_type(w << 16, _F32)
    hi = lax.bitcast_convert_type(w & _U32(0xFFFF0000), _F32)
    return lo, hi


def _store_pieces(ref, r0, w):
    for c in range(PIECES):
        ref[c, r0:r0 + w.shape[0], :] = w[:, c * LANES:(c + 1) * LANES]


def _pool_kernel(x_ref, win_ref, wgrp_ref, scale_ref, wout_ref, g_ref, b_ref,
                 o_ref, op_ref, zs_ref, y_ref):
    s = pl.program_id(1)
    ts = x_ref.shape[1]

    @pl.when(s == 0)
    def _():
        zs_ref[0:POOL_HALO, :] = jnp.zeros((POOL_HALO, D_MODEL), _F32)

    for p0 in range(0, ts, POOL_SUB):
        z = _dot(x_ref[0, p0:p0 + POOL_SUB, :].astype(_BF16), win_ref[...])
        zs_ref[POOL_HALO + p0:POOL_HALO + p0 + POOL_SUB, :] = z
    for p0 in range(0, ts, POOL_SUB):
        x = x_ref[0, p0:p0 + POOL_SUB, :]
        base = POOL_HALO + p0
        pos = s * ts + p0 + lax.broadcasted_iota(jnp.int32, (POOL_SUB, 1), 0)
        for g, w in enumerate(POOL_WINDOWS):
            c0 = g * POOL_GROUP_DIM
            c1 = c0 + POOL_GROUP_DIM
            zg = zs_ref[base:base + POOL_SUB, c0:c1]
            acc = zg
            for k in range(1, w):
                acc = acc + zs_ref[base - k:base - k + POOL_SUB, c0:c1]
            cnt = jnp.minimum(pos + 1, w).astype(_F32)
            pooled = acc / cnt - zg
            yg = _dot(pooled.astype(_BF16), wgrp_ref[g]) * scale_ref[:, c0:c1]
            y_ref[p0:p0 + POOL_SUB, c0:c1] = yg.astype(_BF16)
        mix = _dot(y_ref[p0:p0 + POOL_SUB, :], wout_ref[...])
        out = _layer_norm(DEEPNORM_ALPHA * x + mix, g_ref[...], b_ref[...])
        o_ref[0, p0:p0 + POOL_SUB, :] = out
        _store_pieces(op_ref, p0, _pack_halves(out))
    zs_ref[0:POOL_HALO, :] = zs_ref[ts:ts + POOL_HALO, :]


def _pool_layer(x, part, w_in, w_grp, scale, w_out, ln_g, ln_b):
    bsz, seq, d = x.shape
    rows = bsz // TOKEN_PARTS
    ts = POOL_TILE
    steps = seq // ts
    const2 = lambda b, s: (0, 0)
    out_shape = (jax.ShapeDtypeStruct((rows, seq, d), _F32),
                 jax.ShapeDtypeStruct((PIECES, rows * seq, LANES), _U32))
    tile = pl.BlockSpec((1, ts, d), lambda b, s: (b, s, 0))
    ptile = pl.BlockSpec((PIECES, ts, LANES), lambda b, s: (0, b * steps + s, 0))
    return pl.pallas_call(
        _pool_kernel,
        grid=(rows, steps),
        in_specs=[
            pl.BlockSpec((1, ts, d), lambda b, s: (part * rows + b, s, 0)),
            pl.BlockSpec((d, d), const2),
            pl.BlockSpec((len(POOL_WINDOWS), POOL_GROUP_DIM, POOL_GROUP_DIM), lambda b, s: (0, 0, 0)),
            pl.BlockSpec((1, d), const2),
            pl.BlockSpec((d, d), const2),
            pl.BlockSpec((1, d), const2),
            pl.BlockSpec((1, d), const2),
        ],
        out_specs=(tile, ptile),
        out_shape=out_shape,
        scratch_shapes=[pltpu.VMEM((POOL_HALO + ts, d), _F32),
                        pltpu.VMEM((ts, d), _BF16)],
        compiler_params=pltpu.CompilerParams(
            dimension_semantics=("arbitrary", "arbitrary"),
            vmem_limit_bytes=VMEM_LIMIT),
        name="pool_mixer",
    )(x, w_in.astype(_BF16), w_grp.astype(_BF16), scale.reshape(1, d),
      w_out.astype(_BF16), ln_g.reshape(1, d), ln_b.reshape(1, d))


def _sgu_kernel(x_ref, win_ref, bin_ref, lng_ref, lnb_ref, ws_ref, bs_ref, wout_ref,
                g_ref, b_ref, o_ref, op_ref, gated_ref):
    ts = x_ref.shape[0]
    projected = []
    for p0 in range(0, ts, SGU_SUB):
        x = x_ref[p0:p0 + SGU_SUB, :]
        xb = x.astype(_BF16)
        v = _gelu_tanh(_dot(xb, win_ref[:, SGU_WIDTH:]) + bin_ref[:, SGU_WIDTH:])
        v = _layer_norm(v, lng_ref[...], lnb_ref[...]).astype(_BF16)
        u = _gelu_tanh(_dot(xb, win_ref[:, :SGU_WIDTH]) + bin_ref[:, :SGU_WIDTH])
        projected.append((p0, x, u, v))
    for p0, x, u, v in projected:
        for r0 in range(0, SGU_SUB, SGU_CHUNK):
            for h in range(SGU_HEADS):
                c0 = h * SGU_HEAD_DIM
                sv = _dot(ws_ref[h], v[r0:r0 + SGU_CHUNK, c0:c0 + SGU_HEAD_DIM]) + bs_ref[h]
                gated_ref[p0 + r0:p0 + r0 + SGU_CHUNK, c0:c0 + SGU_HEAD_DIM] = (
                    u[r0:r0 + SGU_CHUNK, c0:c0 + SGU_HEAD_DIM] * sv).astype(_BF16)
        mix = _dot(gated_ref[p0:p0 + SGU_SUB, :], wout_ref[...])
        out = _layer_norm(DEEPNORM_ALPHA * x + mix, g_ref[...], b_ref[...])
        o_ref[p0:p0 + SGU_SUB, :] = out
        _store_pieces(op_ref, p0, _pack_halves(out))


def _sgu_layer(x, w_in, b_in, ln_g, ln_b, w_s, b_s, w_out, mix_g, mix_b):
    n_tok, d = x.shape
    ts = SGU_TILE
    const2 = lambda i: (0, 0)
    const3 = lambda i: (0, 0, 0)
    causal = jnp.tril(jnp.ones((SGU_CHUNK, SGU_CHUNK), w_s.dtype))
    ws = (w_s * causal[None]).astype(_BF16)
    tile = pl.BlockSpec((ts, d), lambda i: (i, 0))
    ptile = pl.BlockSpec((PIECES, ts, LANES), lambda i: (0, i, 0))
    out_shape = (jax.ShapeDtypeStruct((n_tok, d), _F32),
                 jax.ShapeDtypeStruct((PIECES, n_tok, LANES), _U32))
    return pl.pallas_call(
        _sgu_kernel,
        grid=(n_tok // ts,),
        in_specs=[
            tile,
            pl.BlockSpec((d, 2 * SGU_WIDTH), const2),
            pl.BlockSpec((1, 2 * SGU_WIDTH), const2),
            pl.BlockSpec((1, SGU_WIDTH), const2),
            pl.BlockSpec((1, SGU_WIDTH), const2),
            pl.BlockSpec((SGU_HEADS, SGU_CHUNK, SGU_CHUNK), const3),
            pl.BlockSpec((SGU_HEADS, SGU_CHUNK, 1), const3),
            pl.BlockSpec((SGU_WIDTH, d), const2),
            pl.BlockSpec((1, d), const2),
            pl.BlockSpec((1, d), const2),
        ],
        out_specs=(tile, ptile),
        out_shape=out_shape,
        scratch_shapes=[pltpu.VMEM((ts, SGU_WIDTH), _BF16)],
        compiler_params=pltpu.CompilerParams(
            dimension_semantics=("arbitrary",),
            vmem_limit_bytes=VMEM_LIMIT),
        name="sgu_mixer",
    )(x, w_in.astype(_BF16), b_in.reshape(1, -1), ln_g.reshape(1, -1), ln_b.reshape(1, -1),
      ws, b_s.reshape(SGU_HEADS, SGU_CHUNK, 1), w_out.astype(_BF16),
      mix_g.reshape(1, d), mix_b.reshape(1, d))


def _all_max(a):
    return jnp.max(jnp.max(a, axis=0, keepdims=True), axis=1, keepdims=True)


def _all_min(a):
    return jnp.min(jnp.min(a, axis=0, keepdims=True), axis=1, keepdims=True)


def _all_sum(a):
    return jnp.sum(jnp.sum(a, axis=0, keepdims=True), axis=1, keepdims=True)


def _route_kernel(x_ref, whi_ref, wlo_ref, bias_ref, earlier_ref,
                  ek_ref, rk_ref, gk_ref, cnt_ref, carry_ref):
    @pl.when(pl.program_id(0) == 0)
    def _():
        carry_ref[...] = jnp.zeros_like(carry_ref)

    nt = lambda a, b: lax.dot_general(a, b, (((1,), (1,)), ((), ())),
                                      preferred_element_type=_F32)
    logits = []
    for p0 in range(0, x_ref.shape[0], ROUTE_SUB):
        x = x_ref[p0:p0 + ROUTE_SUB, :]
        x_hi = x.astype(_BF16)
        x_lo = (x - x_hi.astype(_F32)).astype(_BF16)
        logits.append(nt(whi_ref[...], x_hi) + (nt(whi_ref[...], x_lo) + nt(wlo_ref[...], x_hi)))
    for n, sub_logits in enumerate(logits):
        _route_sub_tile(sub_logits, n * ROUTE_SUB, bias_ref, earlier_ref,
                        ek_ref, rk_ref, gk_ref, carry_ref)
    cnt_ref[...] = carry_ref[...]


def _route_sub_tile(logits, p0, bias_ref, earlier_ref, ek_ref, rk_ref, gk_ref, carry_ref):
    ts = ROUTE_SUB
    scores = jax.nn.sigmoid(logits).reshape(N_GROUPS, GROUP_SIZE, ts)
    biased = scores + bias_ref[...]
    neg_inf = jnp.float32(-jnp.inf)
    shape3 = (N_GROUPS, GROUP_SIZE, ts)
    in_grp = lax.broadcasted_iota(jnp.int32, shape3, 1)
    grp = lax.broadcasted_iota(jnp.int32, shape3, 0)
    eid = grp * GROUP_SIZE + in_grp

    m1 = jnp.max(biased, axis=1, keepdims=True)
    first1 = jnp.min(jnp.where(biased == m1, in_grp, GROUP_SIZE), axis=1, keepdims=True)
    m2 = jnp.max(jnp.where(in_grp == first1, neg_inf, biased), axis=1, keepdims=True)
    gscore = m1 + m2

    gid = lax.broadcasted_iota(jnp.int32, (N_GROUPS, 1, ts), 0)
    gsel = jnp.zeros((N_GROUPS, 1, ts), jnp.bool_)
    for _ in range(TOPK_GROUPS):
        m = jnp.max(gscore, axis=0, keepdims=True)
        first = jnp.min(jnp.where(gscore == m, gid, N_GROUPS), axis=0, keepdims=True)
        pick = gid == first
        gsel = jnp.logical_or(gsel, pick)
        gscore = jnp.where(pick, neg_inf, gscore)

    masked = jnp.where(gsel, biased, neg_inf)
    picked_any = jnp.zeros(shape3, jnp.bool_)
    e_k, s_k = [], []
    for _ in range(TOP_K):
        m = _all_max(masked)
        first = _all_min(jnp.where(masked == m, eid, N_EXPERTS))
        pick = eid == first
        picked_any = jnp.logical_or(picked_any, pick)
        masked = jnp.where(pick, neg_inf, masked)
        e_k.append(first)
        s_k.append(_all_sum(jnp.where(pick, scores, 0.0)))
    denom = s_k[0]
    for k in range(1, TOP_K):
        denom = denom + s_k[k]

    sel = picked_any.astype(_F32).reshape(N_EXPERTS, ts)
    rank = _dot(sel.astype(_BF16), earlier_ref[...]) + carry_ref[...]
    rank3 = rank.reshape(shape3)
    carry_ref[...] += jnp.sum(sel, axis=1, keepdims=True)

    for k in range(TOP_K):
        r = _all_sum(jnp.where(eid == e_k[k], rank3, 0.0))
        ek_ref[k:k + 1, p0:p0 + ts] = e_k[k].reshape(1, ts)
        rk_ref[k:k + 1, p0:p0 + ts] = r.reshape(1, ts).astype(jnp.int32)
        gk_ref[k:k + 1, p0:p0 + ts] = (s_k[k] / denom * ROUTED_SCALE).reshape(1, ts)


def _route(x, w_router, router_bias):
    n_part, d = x.shape
    ts = ROUTE_TILE
    steps = n_part // ts
    out_shape = (jax.ShapeDtypeStruct((TOP_K, n_part), jnp.int32),
                 jax.ShapeDtypeStruct((TOP_K, n_part), jnp.int32),
                 jax.ShapeDtypeStruct((TOP_K, n_part), _F32),
                 jax.ShapeDtypeStruct((N_EXPERTS, 1), _F32))
    kspec = pl.BlockSpec((TOP_K, ts), lambda i: (0, i))
    const2 = lambda i: (0, 0)
    w_t = w_router.T
    w_hi = w_t.astype(_BF16)
    w_lo = (w_t - w_hi.astype(_F32)).astype(_BF16)
    pos = jnp.arange(ROUTE_SUB, dtype=jnp.int32)
    earlier = (pos[:, None] < pos[None, :]).astype(_BF16)
    return pl.pallas_call(
        _route_kernel,
        grid=(steps,),
        in_specs=[
            pl.BlockSpec((ts, d), lambda i: (i, 0)),
            pl.BlockSpec((N_EXPERTS, d), const2),
            pl.BlockSpec((N_EXPERTS, d), const2),
            pl.BlockSpec((N_GROUPS, GROUP_SIZE, 1), lambda i: (0, 0, 0)),
            pl.BlockSpec((ROUTE_SUB, ROUTE_SUB), const2),
        ],
        out_specs=(kspec, kspec, kspec, pl.BlockSpec((N_EXPERTS, 1), const2)),
        out_shape=out_shape,
        scratch_shapes=[pltpu.VMEM((N_EXPERTS, 1), _F32)],
        compiler_params=pltpu.CompilerParams(
            dimension_semantics=("arbitrary",),
            vmem_limit_bytes=VMEM_LIMIT),
        name="moe_route",
    )(x, w_hi, w_lo, router_bias.reshape(N_GROUPS, GROUP_SIZE, 1), earlier)


def _sc_mesh():
    return plsc.VectorSubcoreMesh(core_axis_name="core", subcore_axis_name="subcore")


def _sc_scatter_rows(src, idx, n_out):
    d = src.shape[1]
    wins = idx.shape[1] // SC_WINDOW

    def body(src_hbm, idx_hbm, out_hbm):
        def step(src_vmem, idx_vmem):
            for k in range(TOP_K):
                pltpu.sync_copy(src_vmem, out_hbm.at[idx_vmem.at[k]])

        pltpu.emit_pipeline(
            step,
            grid=(src.shape[0] // SC_WINDOW,),
            in_specs=[pl.BlockSpec((SC_WINDOW, d), index_map=lambda i: (i, 0)),
                      pl.BlockSpec((TOP_K, SC_WINDOW), index_map=lambda i: (i // wins, i % wins))],
            out_specs=[],
            core_axis_name=("core", "subcore"),
            dimension_semantics=(pltpu.PARALLEL,),
        )(src_hbm, idx_hbm)

    return pl.kernel(body, out_type=jax.ShapeDtypeStruct((n_out, d), src.dtype),
                     mesh=_sc_mesh(), scratch_types=[], name="sc_scatter_rows")(src, idx)


def _sc_gather_sum(table, idx, gates):
    d = table.shape[1]
    n_tok = idx.shape[1]
    wins = n_tok // SC_WINDOW
    lanes = SC_LANES
    out_type = jax.ShapeDtypeStruct((PIECES * n_tok, d), _F32)

    def tree_sum(terms):
        while len(terms) > 1:
            terms = [terms[i] + terms[i + 1] for i in range(0, len(terms), 2)]
        return terms[0]

    def body(table_hbm, idx_hbm, gate_hbm, lo_hbm, hi_hbm, rows_vmem, sem):
        def step(idx_vmem, gate_vmem, lo_vmem, hi_vmem):
            for t0 in range(0, SC_WINDOW, SC_BURST):
                copies = [pltpu.make_async_copy(
                    table_hbm.at[idx_vmem.at[k, pl.ds(t0, SC_BURST)]], rows_vmem.at[k], sem)
                    for k in range(TOP_K)]
                for copy in copies:
                    copy.start()
                for copy in copies:
                    copy.wait()

                @plsc.parallel_loop(0, SC_BURST)
                def _(r):
                    t = t0 + r
                    token = jnp.full((lanes,), t, jnp.int32)
                    gate = [plsc.load_gather(gate_vmem, [jnp.full((lanes,), k, jnp.int32), token])
                            for k in range(TOP_K)]
                    for j in range(0, d, lanes):
                        words = [rows_vmem[k, r, pl.ds(j, lanes)] for k in range(TOP_K)]
                        lo_vmem[t, pl.ds(j, lanes)] = tree_sum(
                            [gate[k] * lax.bitcast_convert_type(words[k] << 16, _F32)
                             for k in range(TOP_K)])
                        hi_vmem[t, pl.ds(j, lanes)] = tree_sum(
                            [gate[k] * lax.bitcast_convert_type(words[k] & _U32(0xFFFF0000), _F32)
                             for k in range(TOP_K)])

        window = lambda i: (i // wins, i % wins)
        pltpu.emit_pipeline(
            step,
            grid=(PIECES * wins,),
            in_specs=[pl.BlockSpec((TOP_K, SC_WINDOW), index_map=window),
                      pl.BlockSpec((TOP_K, SC_WINDOW), index_map=lambda i: (0, i % wins))],
            out_specs=[pl.BlockSpec((SC_WINDOW, d), index_map=lambda i: (i, 0)),
                       pl.BlockSpec((SC_WINDOW, d), index_map=lambda i: (i, 0))],
            core_axis_name=("core", "subcore"),
            dimension_semantics=(pltpu.PARALLEL,),
        )(idx_hbm, gate_hbm, lo_hbm, hi_hbm)

    return pl.kernel(body, out_type=(out_type, out_type), mesh=_sc_mesh(),
                     scratch_types=[pltpu.VMEM((TOP_K, SC_BURST, d), _U32),
                                    pltpu.SemaphoreType.DMA],
                     compiler_params=pltpu.CompilerParams(needs_layout_passes=False),
                     name="sc_gather_sum")(table, idx, gates)


def _index_kernel(start_ref, ek_ref, rk_ref, idx_ref, *, n_rows):
    e = ek_ref[...]
    start = jnp.zeros(e.shape, jnp.int32)
    for ex in range(N_EXPERTS):
        start = jnp.where(e == ex, start_ref[ex], start)
    dest = start + rk_ref[...]
    for c in range(PIECES):
        idx_ref[c] = dest + c * n_rows


def _row_indices(padded_start, e_k, r_k, n_rows):
    n_tok = e_k.shape[1]
    ts = INDEX_TILE
    kspec = pl.BlockSpec((TOP_K, ts), lambda i, st: (0, i))
    grid_spec = pltpu.PrefetchScalarGridSpec(
        num_scalar_prefetch=1,
        grid=(n_tok // ts,),
        in_specs=[kspec, kspec],
        out_specs=pl.BlockSpec((PIECES, TOP_K, ts), lambda i, st: (0, 0, i)),
    )
    return pl.pallas_call(
        functools.partial(_index_kernel, n_rows=n_rows),
        grid_spec=grid_spec,
        out_shape=jax.ShapeDtypeStruct((PIECES, TOP_K, n_tok), jnp.int32),
        compiler_params=pltpu.CompilerParams(dimension_semantics=("arbitrary",)),
        name="moe_row_indices",
    )(padded_start, e_k, r_k)


def _expert_kernel(chunk0_ref, nchunk_ref, cnt_ref, total_ref,
                   xs_hbm, wg_ref, wu_ref, wd_ref, o_hbm,
                   xbuf, obuf, wgu_bf, wd_bf, sem_in, sem_out):
    e = pl.program_id(0)
    total = total_ref[0]
    ch = EXPERT_ROWS
    ahead = IN_SLOTS - 1

    def in_copy(g):
        slot = g % IN_SLOTS
        return pltpu.make_async_copy(xs_hbm.at[:, pl.ds(g * ch, ch), :], xbuf.at[slot],
                                     sem_in.at[slot])

    def out_copy(g):
        slot = g % OUT_SLOTS
        return pltpu.make_async_copy(obuf.at[slot], o_hbm.at[:, pl.ds(g * ch, ch), :],
                                     sem_out.at[slot])

    @pl.when(e == 0)
    def _():
        obuf[...] = jnp.zeros_like(obuf)
        for g in range(ahead):
            @pl.when(g < total)
            def _():
                in_copy(g).start()

    wgu_bf[:, :EXPERT_DIM] = wg_ref[0, 0].astype(_BF16)
    wgu_bf[:, EXPERT_DIM:] = wu_ref[0, 0].astype(_BF16)
    wd_bf[...] = wd_ref[0, 0].astype(_BF16)

    def chunk(j, carry):
        g = chunk0_ref[e] + j
        in_copy(g).wait()

        @pl.when(g + ahead < total)
        def _():
            in_copy(g + ahead).start()

        @pl.when(g >= OUT_SLOTS)
        def _():
            out_copy(g - OUT_SLOTS).wait()

        islot = g % IN_SLOTS
        oslot = g % OUT_SLOTS
        valid = cnt_ref[e] - j * ch

        def gate_up(s):
            r0 = s * EXPERT_SUB
            w = jnp.concatenate([xbuf[islot, c, r0:r0 + EXPERT_SUB, :] for c in range(PIECES)],
                                axis=1)
            lo, hi = _unpack_halves(w)
            return (_dot(lo.astype(_BF16), wgu_bf[:PACKED]) +
                    _dot(hi.astype(_BF16), wgu_bf[PACKED:]))

        def down(s, gu):
            r0 = s * EXPERT_SUB
            h = _silu(gu[:, :EXPERT_DIM]) * gu[:, EXPERT_DIM:]
            y = _pack_halves(_dot(h.astype(_BF16), wd_bf[...]))
            for c in range(PIECES):
                obuf[oslot, c, r0:r0 + EXPERT_SUB, :] = y[:, c * LANES:(c + 1) * LANES]

        n_sub = ch // EXPERT_SUB

        @pl.when(valid >= ch)
        def _():
            projected = [gate_up(s) for s in range(n_sub)]
            for s, gu in enumerate(projected):
                down(s, gu)

        @pl.when(valid < ch)
        def _():
            for s in range(n_sub):
                @pl.when(s * EXPERT_SUB < valid)
                def _():
                    down(s, gate_up(s))

        out_copy(g).start()
        return carry

    lax.fori_loop(0, nchunk_ref[e], chunk, 0)

    @pl.when(e == pl.num_programs(0) - 1)
    def _():
        for back in range(OUT_SLOTS, 0, -1):
            @pl.when(total >= back)
            def _():
                out_copy(total - back).wait()


def _experts(layer, chunk0, nchunk, counts, total, xs, w_gate, w_up, w_down):
    d = D_MODEL
    ch = EXPERT_ROWS
    wspec_in = pl.BlockSpec((1, 1, d, EXPERT_DIM), lambda e, *_: (layer, e, 0, 0))
    any_spec = pl.BlockSpec(memory_space=pl.ANY)
    grid_spec = pltpu.PrefetchScalarGridSpec(
        num_scalar_prefetch=4,
        grid=(N_EXPERTS,),
        in_specs=[
            any_spec,
            wspec_in,
            wspec_in,
            pl.BlockSpec((1, 1, EXPERT_DIM, d), lambda e, *_: (layer, e, 0, 0)),
        ],
        out_specs=any_spec,
        scratch_shapes=[pltpu.VMEM((IN_SLOTS, PIECES, ch, LANES), _U32),
                        pltpu.VMEM((OUT_SLOTS, PIECES, ch, LANES), _U32),
                        pltpu.VMEM((d, 2 * EXPERT_DIM), _BF16),
                        pltpu.VMEM((EXPERT_DIM, d), _BF16),
                        pltpu.SemaphoreType.DMA((IN_SLOTS,)),
                        pltpu.SemaphoreType.DMA((OUT_SLOTS,))],
    )
    return pl.pallas_call(
        _expert_kernel,
        grid_spec=grid_spec,
        out_shape=jax.ShapeDtypeStruct(xs.shape, _U32),
        compiler_params=pltpu.CompilerParams(
            dimension_semantics=("arbitrary",),
            vmem_limit_bytes=VMEM_LIMIT),
        name="moe_experts",
    )(chunk0, nchunk, counts, total, xs, w_gate, w_up, w_down)


def _combine_kernel(x_ref, lo_ref, hi_ref, sg_ref, su_ref, sd_ref, g_ref, b_ref, *rest):
    o_ref = rest[-1]
    x = x_ref[...]
    xb = x.astype(_BF16)
    routed = jnp.concatenate([lo_ref[c] for c in range(PIECES)] +
                             [hi_ref[c] for c in range(PIECES)], axis=1)
    h = _silu(_dot(xb, sg_ref[...])) * _dot(xb, su_ref[...])
    shared = _dot(h.astype(_BF16), sd_ref[...])
    o_ref[...] = _layer_norm(DEEPNORM_ALPHA * x + (routed + shared), g_ref[...], b_ref[...])


def _combine(x, routed_lo, routed_hi, sh_gate, sh_up, sh_down, ln_g, ln_b, out_parts, part, prev):
    n_part, d = x.shape
    ts = COMBINE_TILE
    steps = n_part // ts
    const2 = lambda i: (0, 0)
    in_specs = [
        pl.BlockSpec((ts, d), lambda i: (i, 0)),
        pl.BlockSpec((PIECES, ts, LANES), lambda i: (0, i, 0)),
        pl.BlockSpec((PIECES, ts, LANES), lambda i: (0, i, 0)),
        pl.BlockSpec((d, EXPERT_DIM), const2),
        pl.BlockSpec((d, EXPERT_DIM), const2),
        pl.BlockSpec((EXPERT_DIM, d), const2),
        pl.BlockSpec((1, d), const2),
        pl.BlockSpec((1, d), const2),
    ]
    args = [x, routed_lo, routed_hi, sh_gate.astype(_BF16), sh_up.astype(_BF16), sh_down.astype(_BF16),
            ln_g.reshape(1, d), ln_b.reshape(1, d)]
    aliases = {}
    if prev is not None:
        in_specs.append(pl.BlockSpec(memory_space=pl.ANY))
        aliases = {len(args): 0}
        args.append(prev)
    return pl.pallas_call(
        _combine_kernel,
        grid=(steps,),
        in_specs=in_specs,
        out_specs=pl.BlockSpec((ts, d), lambda i: (part * steps + i, 0)),
        out_shape=jax.ShapeDtypeStruct((out_parts * n_part, d), _F32),
        input_output_aliases=aliases,
        compiler_params=pltpu.CompilerParams(
            dimension_semantics=("arbitrary",),
            vmem_limit_bytes=VMEM_LIMIT),
        name="moe_combine",
    )(*args)


def _moe_dispatch(layer, x, x_packed, w_router, router_bias, w_gate, w_up, w_down):
    n_part, d = x.shape
    bm = EXPERT_ROWS
    n_rows = n_part * TOP_K + N_EXPERTS * bm
    e_k, r_k, g_k, counts = _route(x, w_router, router_bias)
    counts = counts.reshape(N_EXPERTS).astype(jnp.int32)
    padded = (counts + bm - 1) // bm * bm
    padded_end = jnp.cumsum(padded)
    padded_start = padded_end - padded
    idx = _row_indices(padded_start, e_k, r_k, n_rows).reshape(PIECES * TOP_K, n_part)
    xs = _sc_scatter_rows(x_packed.reshape(PIECES * n_part, LANES), idx, PIECES * n_rows)
    rows = _experts(layer, padded_start // bm, padded // bm, counts, padded_end[-1:] // bm,
                    xs.reshape(PIECES, n_rows, LANES), w_gate, w_up, w_down)
    return rows.reshape(PIECES * n_rows, LANES), idx, g_k


def _moe_combine(x, rows, idx, gates, sh_gate, sh_up, sh_down, ln_g, ln_b,
                 out_parts=1, part=0, prev=None):
    n_part = x.shape[0]
    lo, hi = _sc_gather_sum(rows, idx, gates)
    return _combine(x, lo.reshape(PIECES, n_part, LANES), hi.reshape(PIECES, n_part, LANES),
                    sh_gate, sh_up, sh_down, ln_g, ln_b, out_parts, part, prev)


def kernel(x, pool_w_in, pool_w_grp, pool_scale, pool_w_out, sgu_w_in, sgu_b_in, sgu_ln_g, sgu_ln_b, sgu_w_s, sgu_b_s, sgu_w_out, ln_mix_g, ln_mix_b, moe_w_router, moe_router_bias, moe_w_gate, moe_w_up, moe_w_down, moe_sh_gate, moe_sh_up, moe_sh_down, ln_ffn_g, ln_ffn_b):
    bsz, seq, d = x.shape
    parts = range(TOKEN_PARTS)

    def dispatch(i, h, h_packed):
        return _moe_dispatch(i, h, h_packed, moe_w_router[i], moe_router_bias[i],
                             moe_w_gate.astype(_BF16), moe_w_up.astype(_BF16),
                             moe_w_down.astype(_BF16))

    def combine(i, h, routed, **where):
        return _moe_combine(h, *routed, moe_sh_gate[i], moe_sh_up[i], moe_sh_down[i],
                            ln_ffn_g[i], ln_ffn_b[i], **where)

    mixed = [_pool_layer(x, p, pool_w_in[0], pool_w_grp[0], pool_scale[0], pool_w_out[0],
                         ln_mix_g[0], ln_mix_b[0]) for p in parts]
    mixed = [(h.reshape(-1, d), h_packed) for h, h_packed in mixed]
    routed = [dispatch(0, h, h_packed) for h, h_packed in mixed]
    hs = [combine(0, mixed[p][0], routed[p]) for p in parts]
    mixed = [_sgu_layer(h, sgu_w_in[0], sgu_b_in[0], sgu_ln_g[0], sgu_ln_b[0], sgu_w_s[0],
                        sgu_b_s[0], sgu_w_out[0], ln_mix_g[1], ln_mix_b[1]) for h in hs]
    routed = [dispatch(1, h, h_packed) for h, h_packed in mixed]
    out = None
    for p in parts:
        out = combine(1, mixed[p][0], routed[p], out_parts=TOKEN_PARTS, part=p, prev=out)
    return out.reshape(bsz, seq, d)
```

```python
import functools

import jax
import jax.numpy as jnp
from jax import lax
from jax.experimental import pallas as pl
from jax.experimental.pallas import tpu as pltpu
from jax.experimental.pallas import tpu_sc as plsc

D_MODEL = 1024
DEPTH = 2
POOL_WINDOWS = (2, 4, 8, 16)
POOL_GROUP_DIM = D_MODEL // len(POOL_WINDOWS)
POOL_HALO = 16
SGU_CHUNK = 128
SGU_HEADS = 4
SGU_WIDTH = 2 * D_MODEL
SGU_HEAD_DIM = SGU_WIDTH // SGU_HEADS
N_EXPERTS = 64
TOP_K = 8
N_GROUPS = 8
GROUP_SIZE = N_EXPERTS // N_GROUPS
TOPK_GROUPS = 4
EXPERT_DIM = D_MODEL // 4
ROUTED_SCALE = 2.5
DEEPNORM_ALPHA = (2 * DEPTH) ** 0.25
LN_EPS = 1e-5

LANES = 128
PACKED = D_MODEL // 2
PIECES = PACKED // LANES

POOL_TILE = 1024
POOL_SUB = 256
SGU_TILE = 512
SGU_SUB = 256
ROUTE_TILE = 1024
ROUTE_SUB = 512
INDEX_TILE = 2048
EXPERT_ROWS = 512
EXPERT_SUB = 256
IN_SLOTS = 6
OUT_SLOTS = 4
COMBINE_TILE = 512
SC_WINDOW = 128
SC_LANES = 16
SC_BURST = 32
TOKEN_PARTS = 2
VMEM_LIMIT = 56 * 1024 * 1024

_F32 = jnp.float32
_BF16 = jnp.bfloat16
_U32 = jnp.uint32


def _dot(a, b):
    return jnp.dot(a, b, preferred_element_type=_F32)


def _layer_norm(h, g, b):
    mu = jnp.mean(h, axis=-1, keepdims=True)
    hc = h - mu
    var = jnp.mean(hc * hc, axis=-1, keepdims=True)
    return hc * lax.rsqrt(var + LN_EPS) * g + b


def _silu(x):
    return x * jax.nn.sigmoid(x)


def _gelu_tanh(x):
    c = 0.7978845608028654
    return 0.5 * x * (1.0 + jnp.tanh(c * (x + 0.044715 * (x * x * x))))


def _pack_halves(v):
    half = v.shape[1] // 2
    lo = lax.bitcast_convert_type(v[:, :half].astype(_BF16).astype(_F32), _U32)
    hi = lax.bitcast_convert_type(v[:, half:].astype(_BF16).astype(_F32), _U32)
    return (hi & _U32(0xFFFF0000)) | (lo >> 16)


def _unpack_halves(w):
    lo = lax.bitcast_convert_type(w << 16, _F32)
    hi = lax.bitcast_convert_type(w & _U32(0xFFFF0000), _F32)
    return lo, hi


def _store_pieces(ref, r0, w):
    for c in range(PIECES):
        ref[c, r0:r0 + w.shape[0], :] = w[:, c * LANES:(c + 1) * LANES]


def _pool_kernel(x_ref, win_ref, wgrp_ref, scale_ref, wout_ref, g_ref, b_ref,
                 o_ref, op_ref, zs_ref, y_ref):
    s = pl.program_id(1)
    ts = x_ref.shape[1]

    @pl.when(s == 0)
    def _():
        zs_ref[0:POOL_HALO, :] = jnp.zeros((POOL_HALO, D_MODEL), _F32)

    for p0 in range(0, ts, POOL_SUB):
        z = _dot(x_ref[0, p0:p0 + POOL_SUB, :].astype(_BF16), win_ref[...])
        zs_ref[POOL_HALO + p0:POOL_HALO + p0 + POOL_SUB, :] = z
    for p0 in range(0, ts, POOL_SUB):
        x = x_ref[0, p0:p0 + POOL_SUB, :]
        base = POOL_HALO + p0
        pos = s * ts + p0 + lax.broadcasted_iota(jnp.int32, (POOL_SUB, 1), 0)
        for g, w in enumerate(POOL_WINDOWS):
            c0 = g * POOL_GROUP_DIM
            c1 = c0 + POOL_GROUP_DIM
            zg = zs_ref[base:base + POOL_SUB, c0:c1]
            acc = zg
            for k in range(1, w):
                acc = acc + zs_ref[base - k:base - k + POOL_SUB, c0:c1]
            cnt = jnp.minimum(pos + 1, w).astype(_F32)
            pooled = acc / cnt - zg
            yg = _dot(pooled.astype(_BF16), wgrp_ref[g]) * scale_ref[:, c0:c1]
            y_ref[p0:p0 + POOL_SUB, c0:c1] = yg.astype(_BF16)
        mix = _dot(y_ref[p0:p0 + POOL_SUB, :], wout_ref[...])
        out = _layer_norm(DEEPNORM_ALPHA * x + mix, g_ref[...], b_ref[...])
        o_ref[0, p0:p0 + POOL_SUB, :] = out
        _store_pieces(op_ref, p0, _pack_halves(out))
    zs_ref[0:POOL_HALO, :] = zs_ref[ts:ts + POOL_HALO, :]


def _pool_layer(x, part, w_in, w_grp, scale, w_out, ln_g, ln_b):
    bsz, seq, d = x.shape
    rows = bsz // TOKEN_PARTS
    ts = POOL_TILE
    steps = seq // ts
    const2 = lambda b, s: (0, 0)
    out_shape = (jax.ShapeDtypeStruct((rows, seq, d), _F32),
                 jax.ShapeDtypeStruct((PIECES, rows * seq, LANES), _U32))
    tile = pl.BlockSpec((1, ts, d), lambda b, s: (b, s, 0))
    ptile = pl.BlockSpec((PIECES, ts, LANES), lambda b, s: (0, b * steps + s, 0))
    return pl.pallas_call(
        _pool_kernel,
        grid=(rows, steps),
        in_specs=[
            pl.BlockSpec((1, ts, d), lambda b, s: (part * rows + b, s, 0)),
            pl.BlockSpec((d, d), const2),
            pl.BlockSpec((len(POOL_WINDOWS), POOL_GROUP_DIM, POOL_GROUP_DIM), lambda b, s: (0, 0, 0)),
            pl.BlockSpec((1, d), const2),
            pl.BlockSpec((d, d), const2),
            pl.BlockSpec((1, d), const2),
            pl.BlockSpec((1, d), const2),
        ],
        out_specs=(tile, ptile),
        out_shape=out_shape,
        scratch_shapes=[pltpu.VMEM((POOL_HALO + ts, d), _F32),
                        pltpu.VMEM((ts, d), _BF16)],
        compiler_params=pltpu.CompilerParams(
            dimension_semantics=("arbitrary", "arbitrary"),
            vmem_limit_bytes=VMEM_LIMIT),
        name="pool_mixer",
    )(x, w_in.astype(_BF16), w_grp.astype(_BF16), scale.reshape(1, d),
      w_out.astype(_BF16), ln_g.reshape(1, d), ln_b.reshape(1, d))


def _sgu_kernel(x_ref, win_ref, bin_ref, lng_ref, lnb_ref, ws_ref, bs_ref, wout_ref,
                g_ref, b_ref, o_ref, op_ref, gated_ref):
    ts = x_ref.shape[0]
    projected = []
    for p0 in range(0, ts, SGU_SUB):
        x = x_ref[p0:p0 + SGU_SUB, :]
        xb = x.astype(_BF16)
        v = _gelu_tanh(_dot(xb, win_ref[:, SGU_WIDTH:]) + bin_ref[:, SGU_WIDTH:])
        v = _layer_norm(v, lng_ref[...], lnb_ref[...]).astype(_BF16)
        u = _gelu_tanh(_dot(xb, win_ref[:, :SGU_WIDTH]) + bin_ref[:, :SGU_WIDTH])
        projected.append((p0, x, u, v))
    for p0, x, u, v in projected:
        for r0 in range(0, SGU_SUB, SGU_CHUNK):
            for h in range(SGU_HEADS):
                c0 = h * SGU_HEAD_DIM
                sv = _dot(ws_ref[h], v[r0:r0 + SGU_CHUNK, c0:c0 + SGU_HEAD_DIM]) + bs_ref[h]
                gated_ref[p0 + r0:p0 + r0 + SGU_CHUNK, c0:c0 + SGU_HEAD_DIM] = (
                    u[r0:r0 + SGU_CHUNK, c0:c0 + SGU_HEAD_DIM] * sv).astype(_BF16)
        mix = _dot(gated_ref[p0:p0 + SGU_SUB, :], wout_ref[...])
        out = _layer_norm(DEEPNORM_ALPHA * x + mix, g_ref[...], b_ref[...])
        o_ref[p0:p0 + SGU_SUB, :] = out
        _store_pieces(op_ref, p0, _pack_halves(out))


def _sgu_layer(x, w_in, b_in, ln_g, ln_b, w_s, b_s, w_out, mix_g, mix_b):
    n_tok, d = x.shape
    ts = SGU_TILE
    const2 = lambda i: (0, 0)
    const3 = lambda i: (0, 0, 0)
    causal = jnp.tril(jnp.ones((SGU_CHUNK, SGU_CHUNK), w_s.dtype))
    ws = (w_s * causal[None]).astype(_BF16)
    tile = pl.BlockSpec((ts, d), lambda i: (i, 0))
    ptile = pl.BlockSpec((PIECES, ts, LANES), lambda i: (0, i, 0))
    out_shape = (jax.ShapeDtypeStruct((n_tok, d), _F32),
                 jax.ShapeDtypeStruct((PIECES, n_tok, LANES), _U32))
    return pl.pallas_call(
        _sgu_kernel,
        grid=(n_tok // ts,),
        in_specs=[
            tile,
            pl.BlockSpec((d, 2 * SGU_WIDTH), const2),
            pl.BlockSpec((1, 2 * SGU_WIDTH), const2),
            pl.BlockSpec((1, SGU_WIDTH), const2),
            pl.BlockSpec((1, SGU_WIDTH), const2),
            pl.BlockSpec((SGU_HEADS, SGU_CHUNK, SGU_CHUNK), const3),
            pl.BlockSpec((SGU_HEADS, SGU_CHUNK, 1), const3),
            pl.BlockSpec((SGU_WIDTH, d), const2),
            pl.BlockSpec((1, d), const2),
            pl.BlockSpec((1, d), const2),
        ],
        out_specs=(tile, ptile),
        out_shape=out_shape,
        scratch_shapes=[pltpu.VMEM((ts, SGU_WIDTH), _BF16)],
        compiler_params=pltpu.CompilerParams(
            dimension_semantics=("arbitrary",),
            vmem_limit_bytes=VMEM_LIMIT),
        name="sgu_mixer",
    )(x, w_in.astype(_BF16), b_in.reshape(1, -1), ln_g.reshape(1, -1), ln_b.reshape(1, -1),
      ws, b_s.reshape(SGU_HEADS, SGU_CHUNK, 1), w_out.astype(_BF16),
      mix_g.reshape(1, d), mix_b.reshape(1, d))


def _all_max(a):
    return jnp.max(jnp.max(a, axis=0, keepdims=True), axis=1, keepdims=True)


def _all_min(a):
    return jnp.min(jnp.min(a, axis=0, keepdims=True), axis=1, keepdims=True)


def _all_sum(a):
    return jnp.sum(jnp.sum(a, axis=0, keepdims=True), axis=1, keepdims=True)


def _route_kernel(x_ref, whi_ref, wlo_ref, bias_ref, earlier_ref,
                  ek_ref, rk_ref, gk_ref, cnt_ref, carry_ref):
    @pl.when(pl.program_id(0) == 0)
    def _():
        carry_ref[...] = jnp.zeros_like(carry_ref)

    nt = lambda a, b: lax.dot_general(a, b, (((1,), (1,)), ((), ())),
                                      preferred_element_type=_F32)
    logits = []
    for p0 in range(0, x_ref.shape[0], ROUTE_SUB):
        x = x_ref[p0:p0 + ROUTE_SUB, :]
        x_hi = x.astype(_BF16)
        x_lo = (x - x_hi.astype(_F32)).astype(_BF16)
        logits.append(nt(whi_ref[...], x_hi) + (nt(whi_ref[...], x_lo) + nt(wlo_ref[...], x_hi)))
    for n, sub_logits in enumerate(logits):
        _route_sub_tile(sub_logits, n * ROUTE_SUB, bias_ref, earlier_ref,
                        ek_ref, rk_ref, gk_ref, carry_ref)
    cnt_ref[...] = carry_ref[...]


def _route_sub_tile(logits, p0, bias_ref, earlier_ref, ek_ref, rk_ref, gk_ref, carry_ref):
    ts = ROUTE_SUB
    scores = jax.nn.sigmoid(logits).reshape(N_GROUPS, GROUP_SIZE, ts)
    biased = scores + bias_ref[...]
    neg_inf = jnp.float32(-jnp.inf)
    shape3 = (N_GROUPS, GROUP_SIZE, ts)
    in_grp = lax.broadcasted_iota(jnp.int32, shape3, 1)
    grp = lax.broadcasted_iota(jnp.int32, shape3, 0)
    eid = grp * GROUP_SIZE + in_grp

    m1 = jnp.max(biased, axis=1, keepdims=True)
    first1 = jnp.min(jnp.where(biased == m1, in_grp, GROUP_SIZE), axis=1, keepdims=True)
    m2 = jnp.max(jnp.where(in_grp == first1, neg_inf, biased), axis=1, keepdims=True)
    gscore = m1 + m2

    gid = lax.broadcasted_iota(jnp.int32, (N_GROUPS, 1, ts), 0)
    gsel = jnp.zeros((N_GROUPS, 1, ts), jnp.bool_)
    for _ in range(TOPK_GROUPS):
        m = jnp.max(gscore, axis=0, keepdims=True)
        first = jnp.min(jnp.where(gscore == m, gid, N_GROUPS), axis=0, keepdims=True)
        pick = gid == first
        gsel = jnp.logical_or(gsel, pick)
        gscore = jnp.where(pick, neg_inf, gscore)

    masked = jnp.where(gsel, biased, neg_inf)
    picked_any = jnp.zeros(shape3, jnp.bool_)
    e_k, s_k = [], []
    for _ in range(TOP_K):
        m = _all_max(masked)
        first = _all_min(jnp.where(masked == m, eid, N_EXPERTS))
        pick = eid == first
        picked_any = jnp.logical_or(picked_any, pick)
        masked = jnp.where(pick, neg_inf, masked)
        e_k.append(first)
        s_k.append(_all_sum(jnp.where(pick, scores, 0.0)))
    denom = s_k[0]
    for k in range(1, TOP_K):
        denom = denom + s_k[k]

    sel = picked_any.astype(_F32).reshape(N_EXPERTS, ts)
    rank = _dot(sel.astype(_BF16), earlier_ref[...]) + carry_ref[...]
    rank3 = rank.reshape(shape3)
    carry_ref[...] += jnp.sum(sel, axis=1, keepdims=True)

    for k in range(TOP_K):
        r = _all_sum(jnp.where(eid == e_k[k], rank3, 0.0))
        ek_ref[k:k + 1, p0:p0 + ts] = e_k[k].reshape(1, ts)
        rk_ref[k:k + 1, p0:p0 + ts] = r.reshape(1, ts).astype(jnp.int32)
        gk_ref[k:k + 1, p0:p0 + ts] = (s_k[k] / denom * ROUTED_SCALE).reshape(1, ts)


def _route(x, w_router, router_bias):
    n_part, d = x.shape
    ts = ROUTE_TILE
    steps = n_part // ts
    out_shape = (jax.ShapeDtypeStruct((TOP_K, n_part), jnp.int32),
                 jax.ShapeDtypeStruct((TOP_K, n_part), jnp.int32),
                 jax.ShapeDtypeStruct((TOP_K, n_part), _F32),
                 jax.ShapeDtypeStruct((N_EXPERTS, 1), _F32))
    kspec = pl.BlockSpec((TOP_K, ts), lambda i: (0, i))
    const2 = lambda i: (0, 0)
    w_t = w_router.T
    w_hi = w_t.astype(_BF16)
    w_lo = (w_t - w_hi.astype(_F32)).astype(_BF16)
    pos = jnp.arange(ROUTE_SUB, dtype=jnp.int32)
    earlier = (pos[:, None] < pos[None, :]).astype(_BF16)
    return pl.pallas_call(
        _route_kernel,
        grid=(steps,),
        in_specs=[
            pl.BlockSpec((ts, d), lambda i: (i, 0)),
            pl.BlockSpec((N_EXPERTS, d), const2),
            pl.BlockSpec((N_EXPERTS, d), const2),
            pl.BlockSpec((N_GROUPS, GROUP_SIZE, 1), lambda i: (0, 0, 0)),
            pl.BlockSpec((ROUTE_SUB, ROUTE_SUB), const2),
        ],
        out_specs=(kspec, kspec, kspec, pl.BlockSpec((N_EXPERTS, 1), const2)),
        out_shape=out_shape,
        scratch_shapes=[pltpu.VMEM((N_EXPERTS, 1), _F32)],
        compiler_params=pltpu.CompilerParams(
            dimension_semantics=("arbitrary",),
            vmem_limit_bytes=VMEM_LIMIT),
        name="moe_route",
    )(x, w_hi, w_lo, router_bias.reshape(N_GROUPS, GROUP_SIZE, 1), earlier)


def _sc_mesh():
    return plsc.VectorSubcoreMesh(core_axis_name="core", subcore_axis_name="subcore")


def _sc_scatter_rows(src, idx, n_out):
    d = src.shape[1]
    wins = idx.shape[1] // SC_WINDOW

    def body(src_hbm, idx_hbm, out_hbm):
        def step(src_vmem, idx_vmem):
            for k in range(TOP_K):
                pltpu.sync_copy(src_vmem, out_hbm.at[idx_vmem.at[k]])

        pltpu.emit_pipeline(
            step,
            grid=(src.shape[0] // SC_WINDOW,),
            in_specs=[pl.BlockSpec((SC_WINDOW, d), index_map=lambda i: (i, 0)),
                      pl.BlockSpec((TOP_K, SC_WINDOW), index_map=lambda i: (i // wins, i % wins))],
            out_specs=[],
            core_axis_name=("core", "subcore"),
            dimension_semantics=(pltpu.PARALLEL,),
        )(src_hbm, idx_hbm)

    return pl.kernel(body, out_type=jax.ShapeDtypeStruct((n_out, d), src.dtype),
                     mesh=_sc_mesh(), scratch_types=[], name="sc_scatter_rows")(src, idx)


def _sc_gather_sum(table, idx, gates):
    d = table.shape[1]
    n_tok = idx.shape[1]
    wins = n_tok // SC_WINDOW
    lanes = SC_LANES
    out_type = jax.ShapeDtypeStruct((PIECES * n_tok, d), _F32)

    def tree_sum(terms):
        while len(terms) > 1:
            terms = [terms[i] + terms[i + 1] for i in range(0, len(terms), 2)]
        return terms[0]

    def body(table_hbm, idx_hbm, gate_hbm, lo_hbm, hi_hbm, rows_vmem, sem):
        def step(idx_vmem, gate_vmem, lo_vmem, hi_vmem):
            for t0 in range(0, SC_WINDOW, SC_BURST):
                copies = [pltpu.make_async_copy(
                    table_hbm.at[idx_vmem.at[k, pl.ds(t0, SC_BURST)]], rows_vmem.at[k], sem)
                    for k in range(TOP_K)]
                for copy in copies:
                    copy.start()
                for copy in copies:
                    copy.wait()

                @plsc.parallel_loop(0, SC_BURST)
                def _(r):
                    t = t0 + r
                    token = jnp.full((lanes,), t, jnp.int32)
                    gate = [plsc.load_gather(gate_vmem, [jnp.full((lanes,), k, jnp.int32), token])
                            for k in range(TOP_K)]
                    for j in range(0, d, lanes):
                        words = [rows_vmem[k, r, pl.ds(j, lanes)] for k in range(TOP_K)]
                        lo_vmem[t, pl.ds(j, lanes)] = tree_sum(
                            [gate[k] * lax.bitcast_convert_type(words[k] << 16, _F32)
                             for k in range(TOP_K)])
                        hi_vmem[t, pl.ds(j, lanes)] = tree_sum(
                            [gate[k] * lax.bitcast_convert_type(words[k] & _U32(0xFFFF0000), _F32)
                             for k in range(TOP_K)])

        window = lambda i: (i // wins, i % wins)
        pltpu.emit_pipeline(
            step,
            grid=(PIECES * wins,),
            in_specs=[pl.BlockSpec((TOP_K, SC_WINDOW), index_map=window),
                      pl.BlockSpec((TOP_K, SC_WINDOW), index_map=lambda i: (0, i % wins))],
            out_specs=[pl.BlockSpec((SC_WINDOW, d), index_map=lambda i: (i, 0)),
                       pl.BlockSpec((SC_WINDOW, d), index_map=lambda i: (i, 0))],
            core_axis_name=("core", "subcore"),
            dimension_semantics=(pltpu.PARALLEL,),
        )(idx_hbm, gate_hbm, lo_hbm, hi_hbm)

    return pl.kernel(body, out_type=(out_type, out_type), mesh=_sc_mesh(),
                     scratch_types=[pltpu.VMEM((TOP_K, SC_BURST, d), _U32),
                                    pltpu.SemaphoreType.DMA],
                     compiler_params=pltpu.CompilerParams(needs_layout_passes=False),
                     name="sc_gather_sum")(table, idx, gates)


def _index_kernel(start_ref, ek_ref, rk_ref, idx_ref, *, n_rows):
    e = ek_ref[...]
    start = jnp.zeros(e.shape, jnp.int32)
    for ex in range(N_EXPERTS):
        start = jnp.where(e == ex, start_ref[ex], start)
    dest = start + rk_ref[...]
    for c in range(PIECES):
        idx_ref[c] = dest + c * n_rows


def _row_indices(padded_start, e_k, r_k, n_rows):
    n_tok = e_k.shape[1]
    ts = INDEX_TILE
    kspec = pl.BlockSpec((TOP_K, ts), lambda i, st: (0, i))
    grid_spec = pltpu.PrefetchScalarGridSpec(
        num_scalar_prefetch=1,
        grid=(n_tok // ts,),
        in_specs=[kspec, kspec],
        out_specs=pl.BlockSpec((PIECES, TOP_K, ts), lambda i, st: (0, 0, i)),
    )
    return pl.pallas_call(
        functools.partial(_index_kernel, n_rows=n_rows),
        grid_spec=grid_spec,
        out_shape=jax.ShapeDtypeStruct((PIECES, TOP_K, n_tok), jnp.int32),
        compiler_params=pltpu.CompilerParams(dimension_semantics=("arbitrary",)),
        name="moe_row_indices",
    )(padded_start, e_k, r_k)


def _expert_kernel(chunk0_ref, nchunk_ref, cnt_ref, total_ref,
                   xs_hbm, wg_ref, wu_ref, wd_ref, o_hbm,
                   xbuf, obuf, sem_in, sem_out):
    e = pl.program_id(0)
    total = total_ref[0]
    ch = EXPERT_ROWS
    ahead = IN_SLOTS - 1

    def in_copy(g):
        slot = g % IN_SLOTS
        return pltpu.make_async_copy(xs_hbm.at[:, pl.ds(g * ch, ch), :], xbuf.at[slot],
                                     sem_in.at[slot])

    def out_copy(g):
        slot = g % OUT_SLOTS
        return pltpu.make_async_copy(obuf.at[slot], o_hbm.at[:, pl.ds(g * ch, ch), :],
                                     sem_out.at[slot])

    @pl.when(e == 0)
    def _():
        obuf[...] = jnp.zeros_like(obuf)
        for g in range(ahead):
            @pl.when(g < total)
            def _():
                in_copy(g).start()


    def chunk(j, carry):
        g = chunk0_ref[e] + j
        in_copy(g).wait()

        @pl.when(g + ahead < total)
        def _():
            in_copy(g + ahead).start()

        @pl.when(g >= OUT_SLOTS)
        def _():
            out_copy(g - OUT_SLOTS).wait()

        islot = g % IN_SLOTS
        oslot = g % OUT_SLOTS
        valid = cnt_ref[e] - j * ch

        def gate_up(s):
            r0 = s * EXPERT_SUB
            w = jnp.concatenate([xbuf[islot, c, r0:r0 + EXPERT_SUB, :] for c in range(PIECES)],
                                axis=1)
            lo, hi = _unpack_halves(w)
            gate = _dot(lo, wg_ref[0, 0, :PACKED, :]) + _dot(hi, wg_ref[0, 0, PACKED:, :])
            up = _dot(lo, wu_ref[0, 0, :PACKED, :]) + _dot(hi, wu_ref[0, 0, PACKED:, :])
            return gate, up

        def down(s, gate_up_pair):
            r0 = s * EXPERT_SUB
            gate, up = gate_up_pair
            y = _pack_halves(_dot(_silu(gate) * up, wd_ref[0, 0]))
            for c in range(PIECES):
                obuf[oslot, c, r0:r0 + EXPERT_SUB, :] = y[:, c * LANES:(c + 1) * LANES]

        n_sub = ch // EXPERT_SUB

        @pl.when(valid >= ch)
        def _():
            projected = [gate_up(s) for s in range(n_sub)]
            for s, gu in enumerate(projected):
                down(s, gu)

        @pl.when(valid < ch)
        def _():
            for s in range(n_sub):
                @pl.when(s * EXPERT_SUB < valid)
                def _():
                    down(s, gate_up(s))

        out_copy(g).start()
        return carry

    lax.fori_loop(0, nchunk_ref[e], chunk, 0)

    @pl.when(e == pl.num_programs(0) - 1)
    def _():
        for back in range(OUT_SLOTS, 0, -1):
            @pl.when(total >= back)
            def _():
                out_copy(total - back).wait()


def _experts(layer, chunk0, nchunk, counts, total, xs, w_gate, w_up, w_down):
    d = D_MODEL
    ch = EXPERT_ROWS
    wspec_in = pl.BlockSpec((1, 1, d, EXPERT_DIM), lambda e, *_: (layer, e, 0, 0))
    any_spec = pl.BlockSpec(memory_space=pl.ANY)
    grid_spec = pltpu.PrefetchScalarGridSpec(
        num_scalar_prefetch=4,
        grid=(N_EXPERTS,),
        in_specs=[
            any_spec,
            wspec_in,
            wspec_in,
            pl.BlockSpec((1, 1, EXPERT_DIM, d), lambda e, *_: (layer, e, 0, 0)),
        ],
        out_specs=any_spec,
        scratch_shapes=[pltpu.VMEM((IN_SLOTS, PIECES, ch, LANES), _U32),
                        pltpu.VMEM((OUT_SLOTS, PIECES, ch, LANES), _U32),
                        pltpu.SemaphoreType.DMA((IN_SLOTS,)),
                        pltpu.SemaphoreType.DMA((OUT_SLOTS,))],
    )
    return pl.pallas_call(
        _expert_kernel,
        grid_spec=grid_spec,
        out_shape=jax.ShapeDtypeStruct(xs.shape, _U32),
        compiler_params=pltpu.CompilerParams(
            dimension_semantics=("arbitrary",),
            vmem_limit_bytes=VMEM_LIMIT),
        name="moe_experts",
    )(chunk0, nchunk, counts, total, xs, w_gate, w_up, w_down)


def _combine_kernel(x_ref, lo_ref, hi_ref, sg_ref, su_ref, sd_ref, g_ref, b_ref, *rest):
    o_ref = rest[-1]
    x = x_ref[...]
    xb = x.astype(_BF16)
    routed = jnp.concatenate([lo_ref[c] for c in range(PIECES)] +
                             [hi_ref[c] for c in range(PIECES)], axis=1)
    h = _silu(_dot(xb, sg_ref[...])) * _dot(xb, su_ref[...])
    shared = _dot(h.astype(_BF16), sd_ref[...])
    o_ref[...] = _layer_norm(DEEPNORM_ALPHA * x + (routed + shared), g_ref[...], b_ref[...])


def _combine(x, routed_lo, routed_hi, sh_gate, sh_up, sh_down, ln_g, ln_b, out_parts, part, prev):
    n_part, d = x.shape
    ts = COMBINE_TILE
    steps = n_part // ts
    const2 = lambda i: (0, 0)
    in_specs = [
        pl.BlockSpec((ts, d), lambda i: (i, 0)),
        pl.BlockSpec((PIECES, ts, LANES), lambda i: (0, i, 0)),
        pl.BlockSpec((PIECES, ts, LANES), lambda i: (0, i, 0)),
        pl.BlockSpec((d, EXPERT_DIM), const2),
        pl.BlockSpec((d, EXPERT_DIM), const2),
        pl.BlockSpec((EXPERT_DIM, d), const2),
        pl.BlockSpec((1, d), const2),
        pl.BlockSpec((1, d), const2),
    ]
    args = [x, routed_lo, routed_hi, sh_gate.astype(_BF16), sh_up.astype(_BF16), sh_down.astype(_BF16),
            ln_g.reshape(1, d), ln_b.reshape(1, d)]
    aliases = {}
    if prev is not None:
        in_specs.append(pl.BlockSpec(memory_space=pl.ANY))
        aliases = {len(args): 0}
        args.append(prev)
    return pl.pallas_call(
        _combine_kernel,
        grid=(steps,),
        in_specs=in_specs,
        out_specs=pl.BlockSpec((ts, d), lambda i: (part * steps + i, 0)),
        out_shape=jax.ShapeDtypeStruct((out_parts * n_part, d), _F32),
        input_output_aliases=aliases,
        compiler_params=pltpu.CompilerParams(
            dimension_semantics=("arbitrary",),
            vmem_limit_bytes=VMEM_LIMIT),
        name="moe_combine",
    )(*args)


def _moe_dispatch(layer, x, x_packed, w_router, router_bias, w_gate, w_up, w_down):
    n_part, d = x.shape
    bm = EXPERT_ROWS
    n_rows = n_part * TOP_K + N_EXPERTS * bm
    e_k, r_k, g_k, counts = _route(x, w_router, router_bias)
    counts = counts.reshape(N_EXPERTS).astype(jnp.int32)
    padded = (counts + bm - 1) // bm * bm
    padded_end = jnp.cumsum(padded)
    padded_start = padded_end - padded
    idx = _row_indices(padded_start, e_k, r_k, n_rows).reshape(PIECES * TOP_K, n_part)
    xs = _sc_scatter_rows(x_packed.reshape(PIECES * n_part, LANES), idx, PIECES * n_rows)
    rows = _experts(layer, padded_start // bm, padded // bm, counts, padded_end[-1:] // bm,
                    xs.reshape(PIECES, n_rows, LANES), w_gate, w_up, w_down)
    return rows.reshape(PIECES * n_rows, LANES), idx, g_k


def _moe_combine(x, rows, idx, gates, sh_gate, sh_up, sh_down, ln_g, ln_b,
                 out_parts=1, part=0, prev=None):
    n_part = x.shape[0]
    lo, hi = _sc_gather_sum(rows, idx, gates)
    return _combine(x, lo.reshape(PIECES, n_part, LANES), hi.reshape(PIECES, n_part, LANES),
                    sh_gate, sh_up, sh_down, ln_g, ln_b, out_parts, part, prev)


def kernel(x, pool_w_in, pool_w_grp, pool_scale, pool_w_out, sgu_w_in, sgu_b_in, sgu_ln_g, sgu_ln_b, sgu_w_s, sgu_b_s, sgu_w_out, ln_mix_g, ln_mix_b, moe_w_router, moe_router_bias, moe_w_gate, moe_w_up, moe_w_down, moe_sh_gate, moe_sh_up, moe_sh_down, ln_ffn_g, ln_ffn_b):
    bsz, seq, d = x.shape
    parts = range(TOKEN_PARTS)

    def dispatch(i, h, h_packed):
        return _moe_dispatch(i, h, h_packed, moe_w_router[i], moe_router_bias[i],
                             moe_w_gate, moe_w_up, moe_w_down)

    def combine(i, h, routed, **where):
        return _moe_combine(h, *routed, moe_sh_gate[i], moe_sh_up[i], moe_sh_down[i],
                            ln_ffn_g[i], ln_ffn_b[i], **where)

    mixed = [_pool_layer(x, p, pool_w_in[0], pool_w_grp[0], pool_scale[0], pool_w_out[0],
                         ln_mix_g[0], ln_mix_b[0]) for p in parts]
    mixed = [(h.reshape(-1, d), h_packed) for h, h_packed in mixed]
    routed = [dispatch(0, h, h_packed) for h, h_packed in mixed]
    hs = [combine(0, mixed[p][0], routed[p]) for p in parts]
    mixed = [_sgu_layer(h, sgu_w_in[0], sgu_b_in[0], sgu_ln_g[0], sgu_ln_b[0], sgu_w_s[0],
                        sgu_b_s[0], sgu_w_out[0], ln_mix_g[1], ln_mix_b[1]) for h in hs]
    routed = [dispatch(1, h, h_packed) for h, h_packed in mixed]
    out = None
    for p in parts:
        out = combine(1, mixed[p][0], routed[p], out_parts=TOKEN_PARTS, part=p, prev=out)
    return out.reshape(bsz, seq, d)
```

```python
import functools

import jax
import jax.numpy as jnp
from jax import lax
from jax.experimental import pallas as pl
from jax.experimental.pallas import tpu as pltpu
from jax.experimental.pallas import tpu_sc as plsc

D_MODEL = 1024
DEPTH = 2
POOL_WINDOWS = (2, 4, 8, 16)
POOL_GROUP_DIM = D_MODEL // len(POOL_WINDOWS)
POOL_HALO = 16
SGU_CHUNK = 128
SGU_HEADS = 4
SGU_WIDTH = 2 * D_MODEL
SGU_HEAD_DIM = SGU_WIDTH // SGU_HEADS
N_EXPERTS = 64
TOP_K = 8
N_GROUPS = 8
GROUP_SIZE = N_EXPERTS // N_GROUPS
TOPK_GROUPS = 4
EXPERT_DIM = D_MODEL // 4
ROUTED_SCALE = 2.5
DEEPNORM_ALPHA = (2 * DEPTH) ** 0.25
LN_EPS = 1e-5

LANES = 128
PACKED = D_MODEL // 2
PIECES = PACKED // LANES

POOL_TILE = 1024
POOL_SUB = 256
SGU_TILE = 512
SGU_SUB = 256
ROUTE_TILE = 1024
ROUTE_SUB = 512
INDEX_TILE = 2048
EXPERT_ROWS = 512
EXPERT_SUB = 256
IN_SLOTS = 6
OUT_SLOTS = 4
COMBINE_TILE = 512
SC_WINDOW = 128
SC_LANES = 16
SC_BURST = 16
TOKEN_PARTS = 2
V7X_VMEM_BYTES = 64 * 1024 * 1024
VMEM_LIMIT = V7X_VMEM_BYTES * 7 // 8

_F32 = jnp.float32
_BF16 = jnp.bfloat16
_U32 = jnp.uint32


def _dot(a, b):
    return jnp.dot(a, b, preferred_element_type=_F32)


def _layer_norm(h, g, b):
    mu = jnp.mean(h, axis=-1, keepdims=True)
    hc = h - mu
    var = jnp.mean(hc * hc, axis=-1, keepdims=True)
    return hc * lax.rsqrt(var + LN_EPS) * g + b


def _silu(x):
    return x * jax.nn.sigmoid(x)


def _gelu_tanh(x):
    c = 0.7978845608028654
    return 0.5 * x * (1.0 + jnp.tanh(c * (x + 0.044715 * (x * x * x))))


def _pack_halves(v):
    half = v.shape[1] // 2
    lo = lax.bitcast_convert_type(v[:, :half].astype(_BF16).astype(_F32), _U32)
    hi = lax.bitcast_convert_type(v[:, half:].astype(_BF16).astype(_F32), _U32)
    return (hi & _U32(0xFFFF0000)) | (lo >> 16)


def _unpack_halves(w):
    lo = lax.bitcast_convert_type(w << 16, _F32)
    hi = lax.bitcast_convert_type(w & _U32(0xFFFF0000), _F32)
    return lo, hi


def _store_pieces(ref, r0, w):
    for c in range(PIECES):
        ref[c, r0:r0 + w.shape[0], :] = w[:, c * LANES:(c + 1) * LANES]


def _pool_kernel(x_ref, win_ref, wgrp_ref, scale_ref, wout_ref, g_ref, b_ref,
                 o_ref, op_ref, zs_ref, y_ref):
    s = pl.program_id(1)
    ts = x_ref.shape[1]

    @pl.when(s == 0)
    def _():
        zs_ref[0:POOL_HALO, :] = jnp.zeros((POOL_HALO, D_MODEL), _F32)

    for p0 in range(0, ts, POOL_SUB):
        z = _dot(x_ref[0, p0:p0 + POOL_SUB, :].astype(_BF16), win_ref[...])
        zs_ref[POOL_HALO + p0:POOL_HALO + p0 + POOL_SUB, :] = z
    for p0 in range(0, ts, POOL_SUB):
        x = x_ref[0, p0:p0 + POOL_SUB, :]
        base = POOL_HALO + p0
        pos = s * ts + p0 + lax.broadcasted_iota(jnp.int32, (POOL_SUB, 1), 0)
        for g, w in enumerate(POOL_WINDOWS):
            c0 = g * POOL_GROUP_DIM
            c1 = c0 + POOL_GROUP_DIM
            zg = zs_ref[base:base + POOL_SUB, c0:c1]
            acc = zg
            for k in range(1, w):
                acc = acc + zs_ref[base - k:base - k + POOL_SUB, c0:c1]
            cnt = jnp.minimum(pos + 1, w).astype(_F32)
            pooled = acc / cnt - zg
            yg = _dot(pooled.astype(_BF16), wgrp_ref[g]) * scale_ref[:, c0:c1]
            y_ref[p0:p0 + POOL_SUB, c0:c1] = yg.astype(_BF16)
        mix = _dot(y_ref[p0:p0 + POOL_SUB, :], wout_ref[...])
        out = _layer_norm(DEEPNORM_ALPHA * x + mix, g_ref[...], b_ref[...])
        o_ref[0, p0:p0 + POOL_SUB, :] = out
        _store_pieces(op_ref, p0, _pack_halves(out))
    zs_ref[0:POOL_HALO, :] = zs_ref[ts:ts + POOL_HALO, :]


def _pool_layer(x, part, w_in, w_grp, scale, w_out, ln_g, ln_b):
    bsz, seq, d = x.shape
    rows = bsz // TOKEN_PARTS
    ts = POOL_TILE
    steps = seq // ts
    const2 = lambda b, s: (0, 0)
    out_shape = (jax.ShapeDtypeStruct((rows, seq, d), _F32),
                 jax.ShapeDtypeStruct((PIECES, rows * seq, LANES), _U32))
    tile = pl.BlockSpec((1, ts, d), lambda b, s: (b, s, 0))
    ptile = pl.BlockSpec((PIECES, ts, LANES), lambda b, s: (0, b * steps + s, 0))
    return pl.pallas_call(
        _pool_kernel,
        grid=(rows, steps),
        in_specs=[
            pl.BlockSpec((1, ts, d), lambda b, s: (part * rows + b, s, 0)),
            pl.BlockSpec((d, d), const2),
            pl.BlockSpec((len(POOL_WINDOWS), POOL_GROUP_DIM, POOL_GROUP_DIM), lambda b, s: (0, 0, 0)),
            pl.BlockSpec((1, d), const2),
            pl.BlockSpec((d, d), const2),
            pl.BlockSpec((1, d), const2),
            pl.BlockSpec((1, d), const2),
        ],
        out_specs=(tile, ptile),
        out_shape=out_shape,
        scratch_shapes=[pltpu.VMEM((POOL_HALO + ts, d), _F32),
                        pltpu.VMEM((ts, d), _BF16)],
        compiler_params=pltpu.CompilerParams(
            dimension_semantics=("arbitrary", "arbitrary"),
            vmem_limit_bytes=VMEM_LIMIT),
        name="pool_mixer",
    )(x, w_in.astype(_BF16), w_grp.astype(_BF16), scale.reshape(1, d),
      w_out.astype(_BF16), ln_g.reshape(1, d), ln_b.reshape(1, d))


def _sgu_kernel(x_ref, win_ref, bin_ref, lng_ref, lnb_ref, ws_ref, bs_ref, wout_ref,
                g_ref, b_ref, o_ref, op_ref, gated_ref):
    ts = x_ref.shape[0]
    projected = []
    for p0 in range(0, ts, SGU_SUB):
        x = x_ref[p0:p0 + SGU_SUB, :]
        xb = x.astype(_BF16)
        v = _gelu_tanh(_dot(xb, win_ref[:, SGU_WIDTH:]) + bin_ref[:, SGU_WIDTH:])
        v = _layer_norm(v, lng_ref[...], lnb_ref[...]).astype(_BF16)
        u = _gelu_tanh(_dot(xb, win_ref[:, :SGU_WIDTH]) + bin_ref[:, :SGU_WIDTH])
        projected.append((p0, x, u, v))
    for p0, x, u, v in projected:
        for r0 in range(0, SGU_SUB, SGU_CHUNK):
            for h in range(SGU_HEADS):
                c0 = h * SGU_HEAD_DIM
                sv = _dot(ws_ref[h], v[r0:r0 + SGU_CHUNK, c0:c0 + SGU_HEAD_DIM]) + bs_ref[h]
                gated_ref[p0 + r0:p0 + r0 + SGU_CHUNK, c0:c0 + SGU_HEAD_DIM] = (
                    u[r0:r0 + SGU_CHUNK, c0:c0 + SGU_HEAD_DIM] * sv).astype(_BF16)
        mix = _dot(gated_ref[p0:p0 + SGU_SUB, :], wout_ref[...])
        out = _layer_norm(DEEPNORM_ALPHA * x + mix, g_ref[...], b_ref[...])
        o_ref[p0:p0 + SGU_SUB, :] = out
        _store_pieces(op_ref, p0, _pack_halves(out))


def _sgu_layer(x, w_in, b_in, ln_g, ln_b, w_s, b_s, w_out, mix_g, mix_b):
    n_tok, d = x.shape
    ts = SGU_TILE
    const2 = lambda i: (0, 0)
    const3 = lambda i: (0, 0, 0)
    causal = jnp.tril(jnp.ones((SGU_CHUNK, SGU_CHUNK), w_s.dtype))
    ws = (w_s * causal[None]).astype(_BF16)
    tile = pl.BlockSpec((ts, d), lambda i: (i, 0))
    ptile = pl.BlockSpec((PIECES, ts, LANES), lambda i: (0, i, 0))
    out_shape = (jax.ShapeDtypeStruct((n_tok, d), _F32),
                 jax.ShapeDtypeStruct((PIECES, n_tok, LANES), _U32))
    return pl.pallas_call(
        _sgu_kernel,
        grid=(n_tok // ts,),
        in_specs=[
            tile,
            pl.BlockSpec((d, 2 * SGU_WIDTH), const2),
            pl.BlockSpec((1, 2 * SGU_WIDTH), const2),
            pl.BlockSpec((1, SGU_WIDTH), const2),
            pl.BlockSpec((1, SGU_WIDTH), const2),
            pl.BlockSpec((SGU_HEADS, SGU_CHUNK, SGU_CHUNK), const3),
            pl.BlockSpec((SGU_HEADS, SGU_CHUNK, 1), const3),
            pl.BlockSpec((SGU_WIDTH, d), const2),
            pl.BlockSpec((1, d), const2),
            pl.BlockSpec((1, d), const2),
        ],
        out_specs=(tile, ptile),
        out_shape=out_shape,
        scratch_shapes=[pltpu.VMEM((ts, SGU_WIDTH), _BF16)],
        compiler_params=pltpu.CompilerParams(
            dimension_semantics=("arbitrary",),
            vmem_limit_bytes=VMEM_LIMIT),
        name="sgu_mixer",
    )(x, w_in.astype(_BF16), b_in.reshape(1, -1), ln_g.reshape(1, -1), ln_b.reshape(1, -1),
      ws, b_s.reshape(SGU_HEADS, SGU_CHUNK, 1), w_out.astype(_BF16),
      mix_g.reshape(1, d), mix_b.reshape(1, d))


def _all_max(a):
    return jnp.max(jnp.max(a, axis=0, keepdims=True), axis=1, keepdims=True)


def _all_min(a):
    return jnp.min(jnp.min(a, axis=0, keepdims=True), axis=1, keepdims=True)


def _all_sum(a):
    return jnp.sum(jnp.sum(a, axis=0, keepdims=True), axis=1, keepdims=True)


def _route_kernel(x_ref, whi_ref, wlo_ref, bias_ref, earlier_ref,
                  ek_ref, rk_ref, gk_ref, cnt_ref, carry_ref):
    @pl.when(pl.program_id(0) == 0)
    def _():
        carry_ref[...] = jnp.zeros_like(carry_ref)

    nt = lambda a, b: lax.dot_general(a, b, (((1,), (1,)), ((), ())),
                                      preferred_element_type=_F32)
    logits = []
    for p0 in range(0, x_ref.shape[0], ROUTE_SUB):
        x = x_ref[p0:p0 + ROUTE_SUB, :]
        x_hi = x.astype(_BF16)
        x_lo = (x - x_hi.astype(_F32)).astype(_BF16)
        logits.append(nt(whi_ref[...], x_hi) + (nt(whi_ref[...], x_lo) + nt(wlo_ref[...], x_hi)))
    for n, sub_logits in enumerate(logits):
        _route_sub_tile(sub_logits, n * ROUTE_SUB, bias_ref, earlier_ref,
                        ek_ref, rk_ref, gk_ref, carry_ref)
    cnt_ref[...] = carry_ref[...]


def _route_sub_tile(logits, p0, bias_ref, earlier_ref, ek_ref, rk_ref, gk_ref, carry_ref):
    ts = ROUTE_SUB
    scores = jax.nn.sigmoid(logits).reshape(N_GROUPS, GROUP_SIZE, ts)
    biased = scores + bias_ref[...]
    neg_inf = jnp.float32(-jnp.inf)
    shape3 = (N_GROUPS, GROUP_SIZE, ts)
    in_grp = lax.broadcasted_iota(jnp.int32, shape3, 1)
    grp = lax.broadcasted_iota(jnp.int32, shape3, 0)
    eid = grp * GROUP_SIZE + in_grp

    m1 = jnp.max(biased, axis=1, keepdims=True)
    first1 = jnp.min(jnp.where(biased == m1, in_grp, GROUP_SIZE), axis=1, keepdims=True)
    m2 = jnp.max(jnp.where(in_grp == first1, neg_inf, biased), axis=1, keepdims=True)
    gscore = m1 + m2

    gid = lax.broadcasted_iota(jnp.int32, (N_GROUPS, 1, ts), 0)
    gsel = jnp.zeros((N_GROUPS, 1, ts), jnp.bool_)
    for _ in range(TOPK_GROUPS):
        m = jnp.max(gscore, axis=0, keepdims=True)
        first = jnp.min(jnp.where(gscore == m, gid, N_GROUPS), axis=0, keepdims=True)
        pick = gid == first
        gsel = jnp.logical_or(gsel, pick)
        gscore = jnp.where(pick, neg_inf, gscore)

    masked = jnp.where(gsel, biased, neg_inf)
    picked_any = jnp.zeros(shape3, jnp.bool_)
    e_k, s_k = [], []
    for _ in range(TOP_K):
        m = _all_max(masked)
        first = _all_min(jnp.where(masked == m, eid, N_EXPERTS))
        pick = eid == first
        picked_any = jnp.logical_or(picked_any, pick)
        masked = jnp.where(pick, neg_inf, masked)
        e_k.append(first)
        s_k.append(_all_sum(jnp.where(pick, scores, 0.0)))
    denom = s_k[0]
    for k in range(1, TOP_K):
        denom = denom + s_k[k]

    sel = picked_any.astype(_F32).reshape(N_EXPERTS, ts)
    rank = _dot(sel.astype(_BF16), earlier_ref[...]) + carry_ref[...]
    rank3 = rank.reshape(shape3)
    carry_ref[...] += jnp.sum(sel, axis=1, keepdims=True)

    for k in range(TOP_K):
        r = _all_sum(jnp.where(eid == e_k[k], rank3, 0.0))
        ek_ref[k:k + 1, p0:p0 + ts] = e_k[k].reshape(1, ts)
        rk_ref[k:k + 1, p0:p0 + ts] = r.reshape(1, ts).astype(jnp.int32)
        gk_ref[k:k + 1, p0:p0 + ts] = (s_k[k] / denom * ROUTED_SCALE).reshape(1, ts)


def _route(x, w_router, router_bias):
    n_part, d = x.shape
    ts = ROUTE_TILE
    steps = n_part // ts
    out_shape = (jax.ShapeDtypeStruct((TOP_K, n_part), jnp.int32),
                 jax.ShapeDtypeStruct((TOP_K, n_part), jnp.int32),
                 jax.ShapeDtypeStruct((TOP_K, n_part), _F32),
                 jax.ShapeDtypeStruct((N_EXPERTS, 1), _F32))
    kspec = pl.BlockSpec((TOP_K, ts), lambda i: (0, i))
    const2 = lambda i: (0, 0)
    w_t = w_router.T
    w_hi = w_t.astype(_BF16)
    w_lo = (w_t - w_hi.astype(_F32)).astype(_BF16)
    pos = jnp.arange(ROUTE_SUB, dtype=jnp.int32)
    earlier = (pos[:, None] < pos[None, :]).astype(_BF16)
    return pl.pallas_call(
        _route_kernel,
        grid=(steps,),
        in_specs=[
            pl.BlockSpec((ts, d), lambda i: (i, 0)),
            pl.BlockSpec((N_EXPERTS, d), const2),
            pl.BlockSpec((N_EXPERTS, d), const2),
            pl.BlockSpec((N_GROUPS, GROUP_SIZE, 1), lambda i: (0, 0, 0)),
            pl.BlockSpec((ROUTE_SUB, ROUTE_SUB), const2),
        ],
        out_specs=(kspec, kspec, kspec, pl.BlockSpec((N_EXPERTS, 1), const2)),
        out_shape=out_shape,
        scratch_shapes=[pltpu.VMEM((N_EXPERTS, 1), _F32)],
        compiler_params=pltpu.CompilerParams(
            dimension_semantics=("arbitrary",),
            vmem_limit_bytes=VMEM_LIMIT),
        name="moe_route",
    )(x, w_hi, w_lo, router_bias.reshape(N_GROUPS, GROUP_SIZE, 1), earlier)


def _sc_mesh():
    return plsc.VectorSubcoreMesh(core_axis_name="core", subcore_axis_name="subcore")


def _sc_scatter_rows(src, idx, n_out):
    d = src.shape[1]
    wins = idx.shape[1] // SC_WINDOW

    def body(src_hbm, idx_hbm, out_hbm):
        def step(src_vmem, idx_vmem):
            for k in range(TOP_K):
                pltpu.sync_copy(src_vmem, out_hbm.at[idx_vmem.at[k]])

        pltpu.emit_pipeline(
            step,
            grid=(src.shape[0] // SC_WINDOW,),
            in_specs=[pl.BlockSpec((SC_WINDOW, d), index_map=lambda i: (i, 0)),
                      pl.BlockSpec((TOP_K, SC_WINDOW), index_map=lambda i: (i // wins, i % wins))],
            out_specs=[],
            core_axis_name=("core", "subcore"),
            dimension_semantics=(pltpu.PARALLEL,),
        )(src_hbm, idx_hbm)

    return pl.kernel(body, out_type=jax.ShapeDtypeStruct((n_out, d), src.dtype),
                     mesh=_sc_mesh(), scratch_types=[], name="sc_scatter_rows")(src, idx)


def _sc_gather_sum(table, idx, gates):
    d = table.shape[1]
    n_tok = idx.shape[1]
    wins = n_tok // SC_WINDOW
    lanes = SC_LANES
    out_type = jax.ShapeDtypeStruct((PIECES * n_tok, d), _F32)

    def tree_sum(terms):
        while len(terms) > 1:
            terms = [terms[i] + terms[i + 1] for i in range(0, len(terms), 2)]
        return terms[0]

    def body(table_hbm, idx_hbm, gate_hbm, lo_hbm, hi_hbm, rows_vmem, sem):
        def step(idx_vmem, gate_vmem, lo_vmem, hi_vmem):
            def burst_copies(b):
                return [pltpu.make_async_copy(
                    table_hbm.at[idx_vmem.at[k, pl.ds(b * SC_BURST, SC_BURST)]],
                    rows_vmem.at[b % 2, k], sem.at[b % 2]) for k in range(TOP_K)]

            n_bursts = SC_WINDOW // SC_BURST
            for copy in burst_copies(0):
                copy.start()
            for b in range(n_bursts):
                if b + 1 < n_bursts:
                    for copy in burst_copies(b + 1):
                        copy.start()
                for copy in burst_copies(b):
                    copy.wait()
                t0 = b * SC_BURST
                rows = rows_vmem.at[b % 2]

                @plsc.parallel_loop(0, SC_BURST)
                def _(r):
                    t = t0 + r
                    token = jnp.full((lanes,), t, jnp.int32)
                    gate = [plsc.load_gather(gate_vmem, [jnp.full((lanes,), k, jnp.int32), token])
                            for k in range(TOP_K)]
                    for j in range(0, d, lanes):
                        words = [rows[k, r, pl.ds(j, lanes)] for k in range(TOP_K)]
                        lo_vmem[t, pl.ds(j, lanes)] = tree_sum(
                            [gate[k] * lax.bitcast_convert_type(words[k] << 16, _F32)
                             for k in range(TOP_K)])
                        hi_vmem[t, pl.ds(j, lanes)] = tree_sum(
                            [gate[k] * lax.bitcast_convert_type(words[k] & _U32(0xFFFF0000), _F32)
                             for k in range(TOP_K)])

        window = lambda i: (i // wins, i % wins)
        pltpu.emit_pipeline(
            step,
            grid=(PIECES * wins,),
            in_specs=[pl.BlockSpec((TOP_K, SC_WINDOW), index_map=window),
                      pl.BlockSpec((TOP_K, SC_WINDOW), index_map=lambda i: (0, i % wins))],
            out_specs=[pl.BlockSpec((SC_WINDOW, d), index_map=lambda i: (i, 0)),
                       pl.BlockSpec((SC_WINDOW, d), index_map=lambda i: (i, 0))],
            core_axis_name=("core", "subcore"),
            dimension_semantics=(pltpu.PARALLEL,),
        )(idx_hbm, gate_hbm, lo_hbm, hi_hbm)

    return pl.kernel(body, out_type=(out_type, out_type), mesh=_sc_mesh(),
                     scratch_types=[pltpu.VMEM((2, TOP_K, SC_BURST, d), _U32),
                                    pltpu.SemaphoreType.DMA((2,))],
                     compiler_params=pltpu.CompilerParams(needs_layout_passes=False),
                     name="sc_gather_sum")(table, idx, gates)


def _index_kernel(start_ref, ek_ref, rk_ref, idx_ref, *, n_rows):
    e = ek_ref[...]
    start = jnp.zeros(e.shape, jnp.int32)
    for ex in range(N_EXPERTS):
        start = jnp.where(e == ex, start_ref[ex], start)
    dest = start + rk_ref[...]
    for c in range(PIECES):
        idx_ref[c] = dest + c * n_rows


def _row_indices(padded_start, e_k, r_k, n_rows):
    n_tok = e_k.shape[1]
    ts = INDEX_TILE
    kspec = pl.BlockSpec((TOP_K, ts), lambda i, st: (0, i))
    grid_spec = pltpu.PrefetchScalarGridSpec(
        num_scalar_prefetch=1,
        grid=(n_tok // ts,),
        in_specs=[kspec, kspec],
        out_specs=pl.BlockSpec((PIECES, TOP_K, ts), lambda i, st: (0, 0, i)),
    )
    return pl.pallas_call(
        functools.partial(_index_kernel, n_rows=n_rows),
        grid_spec=grid_spec,
        out_shape=jax.ShapeDtypeStruct((PIECES, TOP_K, n_tok), jnp.int32),
        compiler_params=pltpu.CompilerParams(dimension_semantics=("arbitrary",)),
        name="moe_row_indices",
    )(padded_start, e_k, r_k)


def _expert_kernel(chunk0_ref, nchunk_ref, cnt_ref, total_ref,
                   xs_hbm, wg_ref, wu_ref, wd_ref, o_hbm,
                   xbuf, obuf, sem_in, sem_out):
    e = pl.program_id(0)
    total = total_ref[0]
    ch = EXPERT_ROWS
    ahead = IN_SLOTS - 1

    def in_copy(g):
        slot = g % IN_SLOTS
        return pltpu.make_async_copy(xs_hbm.at[:, pl.ds(g * ch, ch), :], xbuf.at[slot],
                                     sem_in.at[slot])

    def out_copy(g):
        slot = g % OUT_SLOTS
        return pltpu.make_async_copy(obuf.at[slot], o_hbm.at[:, pl.ds(g * ch, ch), :],
                                     sem_out.at[slot])

    @pl.when(e == 0)
    def _():
        obuf[...] = jnp.zeros_like(obuf)
        for g in range(ahead):
            @pl.when(g < total)
            def _():
                in_copy(g).start()


    def chunk(j, carry):
        g = chunk0_ref[e] + j
        in_copy(g).wait()

        @pl.when(g + ahead < total)
        def _():
            in_copy(g + ahead).start()

        @pl.when(g >= OUT_SLOTS)
        def _():
            out_copy(g - OUT_SLOTS).wait()

        islot = g % IN_SLOTS
        oslot = g % OUT_SLOTS
        valid = cnt_ref[e] - j * ch

        def gate_up(s):
            r0 = s * EXPERT_SUB
            w = jnp.concatenate([xbuf[islot, c, r0:r0 + EXPERT_SUB, :] for c in range(PIECES)],
                                axis=1)
            lo, hi = _unpack_halves(w)
            gate = _dot(lo, wg_ref[0, 0, :PACKED, :]) + _dot(hi, wg_ref[0, 0, PACKED:, :])
            up = _dot(lo, wu_ref[0, 0, :PACKED, :]) + _dot(hi, wu_ref[0, 0, PACKED:, :])
            return gate, up

        def down(s, gate_up_pair):
            r0 = s * EXPERT_SUB
            gate, up = gate_up_pair
            y = _pack_halves(_dot(_silu(gate) * up, wd_ref[0, 0]))
            for c in range(PIECES):
                obuf[oslot, c, r0:r0 + EXPERT_SUB, :] = y[:, c * LANES:(c + 1) * LANES]

        n_sub = ch // EXPERT_SUB

        @pl.when(valid >= ch)
        def _():
            projected = [gate_up(s) for s in range(n_sub)]
            for s, gu in enumerate(projected):
                down(s, gu)

        @pl.when(valid < ch)
        def _():
            for s in range(n_sub):
                @pl.when(s * EXPERT_SUB < valid)
                def _():
                    down(s, gate_up(s))

        out_copy(g).start()
        return carry

    lax.fori_loop(0, nchunk_ref[e], chunk, 0)

    @pl.when(e == pl.num_programs(0) - 1)
    def _():
        for back in range(OUT_SLOTS, 0, -1):
            @pl.when(total >= back)
            def _():
                out_copy(total - back).wait()


def _experts(layer, chunk0, nchunk, counts, total, xs, w_gate, w_up, w_down):
    d = D_MODEL
    ch = EXPERT_ROWS
    wspec_in = pl.BlockSpec((1, 1, d, EXPERT_DIM), lambda e, *_: (layer, e, 0, 0))
    any_spec = pl.BlockSpec(memory_space=pl.ANY)
    grid_spec = pltpu.PrefetchScalarGridSpec(
        num_scalar_prefetch=4,
        grid=(N_EXPERTS,),
        in_specs=[
            any_spec,
            wspec_in,
            wspec_in,
            pl.BlockSpec((1, 1, EXPERT_DIM, d), lambda e, *_: (layer, e, 0, 0)),
        ],
        out_specs=any_spec,
        scratch_shapes=[pltpu.VMEM((IN_SLOTS, PIECES, ch, LANES), _U32),
                        pltpu.VMEM((OUT_SLOTS, PIECES, ch, LANES), _U32),
                        pltpu.SemaphoreType.DMA((IN_SLOTS,)),
                        pltpu.SemaphoreType.DMA((OUT_SLOTS,))],
    )
    return pl.pallas_call(
        _expert_kernel,
        grid_spec=grid_spec,
        out_shape=jax.ShapeDtypeStruct(xs.shape, _U32),
        compiler_params=pltpu.CompilerParams(
            dimension_semantics=("arbitrary",),
            vmem_limit_bytes=VMEM_LIMIT),
        name="moe_experts",
    )(chunk0, nchunk, counts, total, xs, w_gate, w_up, w_down)


def _combine_kernel(x_ref, lo_ref, hi_ref, sg_ref, su_ref, sd_ref, g_ref, b_ref, *rest):
    o_ref = rest[-1]
    x = x_ref[...]
    xb = x.astype(_BF16)
    routed = jnp.concatenate([lo_ref[c] for c in range(PIECES)] +
                             [hi_ref[c] for c in range(PIECES)], axis=1)
    h = _silu(_dot(xb, sg_ref[...])) * _dot(xb, su_ref[...])
    shared = _dot(h.astype(_BF16), sd_ref[...])
    o_ref[...] = _layer_norm(DEEPNORM_ALPHA * x + (routed + shared), g_ref[...], b_ref[...])


def _combine(x, routed_lo, routed_hi, sh_gate, sh_up, sh_down, ln_g, ln_b, out_parts, part, prev):
    n_part, d = x.shape
    ts = COMBINE_TILE
    steps = n_part // ts
    const2 = lambda i: (0, 0)
    in_specs = [
        pl.BlockSpec((ts, d), lambda i: (i, 0)),
        pl.BlockSpec((PIECES, ts, LANES), lambda i: (0, i, 0)),
        pl.BlockSpec((PIECES, ts, LANES), lambda i: (0, i, 0)),
        pl.BlockSpec((d, EXPERT_DIM), const2),
        pl.BlockSpec((d, EXPERT_DIM), const2),
        pl.BlockSpec((EXPERT_DIM, d), const2),
        pl.BlockSpec((1, d), const2),
        pl.BlockSpec((1, d), const2),
    ]
    args = [x, routed_lo, routed_hi, sh_gate.astype(_BF16), sh_up.astype(_BF16), sh_down.astype(_BF16),
            ln_g.reshape(1, d), ln_b.reshape(1, d)]
    aliases = {}
    if prev is not None:
        in_specs.append(pl.BlockSpec(memory_space=pl.ANY))
        aliases = {len(args): 0}
        args.append(prev)
    return pl.pallas_call(
        _combine_kernel,
        grid=(steps,),
        in_specs=in_specs,
        out_specs=pl.BlockSpec((ts, d), lambda i: (part * steps + i, 0)),
        out_shape=jax.ShapeDtypeStruct((out_parts * n_part, d), _F32),
        input_output_aliases=aliases,
        compiler_params=pltpu.CompilerParams(
            dimension_semantics=("arbitrary",),
            vmem_limit_bytes=VMEM_LIMIT),
        name="moe_combine",
    )(*args)


def _moe_dispatch(layer, x, x_packed, w_router, router_bias, w_gate, w_up, w_down):
    n_part, d = x.shape
    bm = EXPERT_ROWS
    n_rows = n_part * TOP_K + N_EXPERTS * bm
    e_k, r_k, g_k, counts = _route(x, w_router, router_bias)
    counts = counts.reshape(N_EXPERTS).astype(jnp.int32)
    padded = (counts + bm - 1) // bm * bm
    padded_end = jnp.cumsum(padded)
    padded_start = padded_end - padded
    idx = _row_indices(padded_start, e_k, r_k, n_rows).reshape(PIECES * TOP_K, n_part)
    xs = _sc_scatter_rows(x_packed.reshape(PIECES * n_part, LANES), idx, PIECES * n_rows)
    rows = _experts(layer, padded_start // bm, padded // bm, counts, padded_end[-1:] // bm,
                    xs.reshape(PIECES, n_rows, LANES), w_gate, w_up, w_down)
    return rows.reshape(PIECES * n_rows, LANES), idx, g_k


def _moe_combine(x, rows, idx, gates, sh_gate, sh_up, sh_down, ln_g, ln_b,
                 out_parts=1, part=0, prev=None):
    n_part = x.shape[0]
    lo, hi = _sc_gather_sum(rows, idx, gates)
    return _combine(x, lo.reshape(PIECES, n_part, LANES), hi.reshape(PIECES, n_part, LANES),
                    sh_gate, sh_up, sh_down, ln_g, ln_b, out_parts, part, prev)


def kernel(x, pool_w_in, pool_w_grp, pool_scale, pool_w_out, sgu_w_in, sgu_b_in, sgu_ln_g, sgu_ln_b, sgu_w_s, sgu_b_s, sgu_w_out, ln_mix_g, ln_mix_b, moe_w_router, moe_router_bias, moe_w_gate, moe_w_up, moe_w_down, moe_sh_gate, moe_sh_up, moe_sh_down, ln_ffn_g, ln_ffn_b):
    bsz, seq, d = x.shape
    parts = range(TOKEN_PARTS)

    def dispatch(i, h, h_packed):
        return _moe_dispatch(i, h, h_packed, moe_w_router[i], moe_router_bias[i],
                             moe_w_gate, moe_w_up, moe_w_down)

    def combine(i, h, routed, **where):
        return _moe_combine(h, *routed, moe_sh_gate[i], moe_sh_up[i], moe_sh_down[i],
                            ln_ffn_g[i], ln_ffn_b[i], **where)

    mixed = [_pool_layer(x, p, pool_w_in[0], pool_w_grp[0], pool_scale[0], pool_w_out[0],
                         ln_mix_g[0], ln_mix_b[0]) for p in parts]
    mixed = [(h.reshape(-1, d), h_packed) for h, h_packed in mixed]
    routed = [dispatch(0, h, h_packed) for h, h_packed in mixed]
    hs = [combine(0, mixed[p][0], routed[p]) for p in parts]
    mixed = [_sgu_layer(h, sgu_w_in[0], sgu_b_in[0], sgu_ln_g[0], sgu_ln_b[0], sgu_w_s[0],
                        sgu_b_s[0], sgu_w_out[0], ln_mix_g[1], ln_mix_b[1]) for h in hs]
    routed = [dispatch(1, h, h_packed) for h, h_packed in mixed]
    out = None
    for p in parts:
        out = combine(1, mixed[p][0], routed[p], out_parts=TOKEN_PARTS, part=p, prev=out)
    return out.reshape(bsz, seq, d)
```

```python
import functools

import jax
import jax.numpy as jnp
from jax import lax
from jax.experimental import pallas as pl
from jax.experimental.pallas import tpu as pltpu
from jax.experimental.pallas import tpu_sc as plsc

D_MODEL = 1024
DEPTH = 2
POOL_WINDOWS = (2, 4, 8, 16)
POOL_GROUP_DIM = D_MODEL // len(POOL_WINDOWS)
POOL_HALO = 16
SGU_CHUNK = 128
SGU_HEADS = 4
SGU_WIDTH = 2 * D_MODEL
SGU_HEAD_DIM = SGU_WIDTH // SGU_HEADS
N_EXPERTS = 64
TOP_K = 8
N_GROUPS = 8
GROUP_SIZE = N_EXPERTS // N_GROUPS
TOPK_GROUPS = 4
EXPERT_DIM = D_MODEL // 4
ROUTED_SCALE = 2.5
DEEPNORM_ALPHA = (2 * DEPTH) ** 0.25
LN_EPS = 1e-5

LANES = 128
PACKED = D_MODEL // 2
PIECES = PACKED // LANES

POOL_TILE = 1024
POOL_SUB = 256
SGU_TILE = 512
SGU_SUB = 256
ROUTE_TILE = 1024
ROUTE_SUB = 512
INDEX_TILE = 2048
EXPERT_ROWS = 512
EXPERT_SUB = 256
IN_SLOTS = 6
OUT_SLOTS = 4
COMBINE_TILE = 512
SC_WINDOW = 128
SC_LANES = 16
SC_BURST = 16
TOKEN_PARTS = 2
V7X_VMEM_BYTES = 64 * 1024 * 1024
VMEM_LIMIT = V7X_VMEM_BYTES * 7 // 8

_F32 = jnp.float32
_BF16 = jnp.bfloat16
_U32 = jnp.uint32


def _dot(a, b):
    return jnp.dot(a, b, preferred_element_type=_F32)


def _layer_norm(h, g, b):
    mu = jnp.mean(h, axis=-1, keepdims=True)
    hc = h - mu
    var = jnp.mean(hc * hc, axis=-1, keepdims=True)
    return hc * lax.rsqrt(var + LN_EPS) * g + b


def _silu(x):
    return x * jax.nn.sigmoid(x)


def _gelu_tanh(x):
    c = 0.7978845608028654
    return 0.5 * x * (1.0 + jnp.tanh(c * (x + 0.044715 * (x * x * x))))


def _pack_halves(v):
    half = v.shape[1] // 2
    lo = lax.bitcast_convert_type(v[:, :half].astype(_BF16).astype(_F32), _U32)
    hi = lax.bitcast_convert_type(v[:, half:].astype(_BF16).astype(_F32), _U32)
    return (hi & _U32(0xFFFF0000)) | (lo >> 16)


def _unpack_halves(w):
    lo = lax.bitcast_convert_type(w << 16, _F32)
    hi = lax.bitcast_convert_type(w & _U32(0xFFFF0000), _F32)
    return lo, hi


def _store_pieces(ref, r0, w):
    for c in range(PIECES):
        ref[c, r0:r0 + w.shape[0], :] = w[:, c * LANES:(c + 1) * LANES]


def _pool_kernel(x_ref, win_ref, wgrp_ref, scale_ref, wout_ref, g_ref, b_ref,
                 o_ref, op_ref, zs_ref, y_ref):
    s = pl.program_id(1)
    ts = x_ref.shape[1]

    @pl.when(s == 0)
    def _():
        zs_ref[0:POOL_HALO, :] = jnp.zeros((POOL_HALO, D_MODEL), _F32)

    for p0 in range(0, ts, POOL_SUB):
        z = _dot(x_ref[0, p0:p0 + POOL_SUB, :].astype(_BF16), win_ref[...])
        zs_ref[POOL_HALO + p0:POOL_HALO + p0 + POOL_SUB, :] = z
    for p0 in range(0, ts, POOL_SUB):
        x = x_ref[0, p0:p0 + POOL_SUB, :]
        base = POOL_HALO + p0
        pos = s * ts + p0 + lax.broadcasted_iota(jnp.int32, (POOL_SUB, 1), 0)
        for g, w in enumerate(POOL_WINDOWS):
            c0 = g * POOL_GROUP_DIM
            c1 = c0 + POOL_GROUP_DIM
            zg = zs_ref[base:base + POOL_SUB, c0:c1]
            acc = zg
            for k in range(1, w):
                acc = acc + zs_ref[base - k:base - k + POOL_SUB, c0:c1]
            cnt = jnp.minimum(pos + 1, w).astype(_F32)
            pooled = acc / cnt - zg
            yg = _dot(pooled.astype(_BF16), wgrp_ref[g]) * scale_ref[:, c0:c1]
            y_ref[p0:p0 + POOL_SUB, c0:c1] = yg.astype(_BF16)
        mix = _dot(y_ref[p0:p0 + POOL_SUB, :], wout_ref[...])
        out = _layer_norm(DEEPNORM_ALPHA * x + mix, g_ref[...], b_ref[...])
        o_ref[0, p0:p0 + POOL_SUB, :] = out
        _store_pieces(op_ref, p0, _pack_halves(out))
    zs_ref[0:POOL_HALO, :] = zs_ref[ts:ts + POOL_HALO, :]


def _pool_layer(x, part, w_in, w_grp, scale, w_out, ln_g, ln_b):
    bsz, seq, d = x.shape
    rows = bsz // TOKEN_PARTS
    ts = POOL_TILE
    steps = seq // ts
    const2 = lambda b, s: (0, 0)
    out_shape = (jax.ShapeDtypeStruct((rows, seq, d), _F32),
                 jax.ShapeDtypeStruct((PIECES, rows * seq, LANES), _U32))
    tile = pl.BlockSpec((1, ts, d), lambda b, s: (b, s, 0))
    ptile = pl.BlockSpec((PIECES, ts, LANES), lambda b, s: (0, b * steps + s, 0))
    return pl.pallas_call(
        _pool_kernel,
        grid=(rows, steps),
        in_specs=[
            pl.BlockSpec((1, ts, d), lambda b, s: (part * rows + b, s, 0)),
            pl.BlockSpec((d, d), const2),
            pl.BlockSpec((len(POOL_WINDOWS), POOL_GROUP_DIM, POOL_GROUP_DIM), lambda b, s: (0, 0, 0)),
            pl.BlockSpec((1, d), const2),
            pl.BlockSpec((d, d), const2),
            pl.BlockSpec((1, d), const2),
            pl.BlockSpec((1, d), const2),
        ],
        out_specs=(tile, ptile),
        out_shape=out_shape,
        scratch_shapes=[pltpu.VMEM((POOL_HALO + ts, d), _F32),
                        pltpu.VMEM((ts, d), _BF16)],
        compiler_params=pltpu.CompilerParams(
            dimension_semantics=("arbitrary", "arbitrary"),
            vmem_limit_bytes=VMEM_LIMIT),
        name="pool_mixer",
    )(x, w_in.astype(_BF16), w_grp.astype(_BF16), scale.reshape(1, d),
      w_out.astype(_BF16), ln_g.reshape(1, d), ln_b.reshape(1, d))


def _sgu_kernel(x_ref, win_ref, bin_ref, lng_ref, lnb_ref, ws_ref, bs_ref, wout_ref,
                g_ref, b_ref, o_ref, op_ref, gated_ref):
    ts = x_ref.shape[0]
    projected = []
    for p0 in range(0, ts, SGU_SUB):
        x = x_ref[p0:p0 + SGU_SUB, :]
        xb = x.astype(_BF16)
        v = _gelu_tanh(_dot(xb, win_ref[:, SGU_WIDTH:]) + bin_ref[:, SGU_WIDTH:])
        v = _layer_norm(v, lng_ref[...], lnb_ref[...]).astype(_BF16)
        u = _gelu_tanh(_dot(xb, win_ref[:, :SGU_WIDTH]) + bin_ref[:, :SGU_WIDTH])
        projected.append((p0, x, u, v))
    for p0, x, u, v in projected:
        for r0 in range(0, SGU_SUB, SGU_CHUNK):
            for h in range(SGU_HEADS):
                c0 = h * SGU_HEAD_DIM
                sv = _dot(ws_ref[h], v[r0:r0 + SGU_CHUNK, c0:c0 + SGU_HEAD_DIM]) + bs_ref[h]
                gated_ref[p0 + r0:p0 + r0 + SGU_CHUNK, c0:c0 + SGU_HEAD_DIM] = (
                    u[r0:r0 + SGU_CHUNK, c0:c0 + SGU_HEAD_DIM] * sv).astype(_BF16)
        mix = _dot(gated_ref[p0:p0 + SGU_SUB, :], wout_ref[...])
        out = _layer_norm(DEEPNORM_ALPHA * x + mix, g_ref[...], b_ref[...])
        o_ref[p0:p0 + SGU_SUB, :] = out
        _store_pieces(op_ref, p0, _pack_halves(out))


def _sgu_layer(x, w_in, b_in, ln_g, ln_b, w_s, b_s, w_out, mix_g, mix_b):
    n_tok, d = x.shape
    ts = SGU_TILE
    const2 = lambda i: (0, 0)
    const3 = lambda i: (0, 0, 0)
    causal = jnp.tril(jnp.ones((SGU_CHUNK, SGU_CHUNK), w_s.dtype))
    ws = (w_s * causal[None]).astype(_BF16)
    tile = pl.BlockSpec((ts, d), lambda i: (i, 0))
    ptile = pl.BlockSpec((PIECES, ts, LANES), lambda i: (0, i, 0))
    out_shape = (jax.ShapeDtypeStruct((n_tok, d), _F32),
                 jax.ShapeDtypeStruct((PIECES, n_tok, LANES), _U32))
    return pl.pallas_call(
        _sgu_kernel,
        grid=(n_tok // ts,),
        in_specs=[
            tile,
            pl.BlockSpec((d, 2 * SGU_WIDTH), const2),
            pl.BlockSpec((1, 2 * SGU_WIDTH), const2),
            pl.BlockSpec((1, SGU_WIDTH), const2),
            pl.BlockSpec((1, SGU_WIDTH), const2),
            pl.BlockSpec((SGU_HEADS, SGU_CHUNK, SGU_CHUNK), const3),
            pl.BlockSpec((SGU_HEADS, SGU_CHUNK, 1), const3),
            pl.BlockSpec((SGU_WIDTH, d), const2),
            pl.BlockSpec((1, d), const2),
            pl.BlockSpec((1, d), const2),
        ],
        out_specs=(tile, ptile),
        out_shape=out_shape,
        scratch_shapes=[pltpu.VMEM((ts, SGU_WIDTH), _BF16)],
        compiler_params=pltpu.CompilerParams(
            dimension_semantics=("arbitrary",),
            vmem_limit_bytes=VMEM_LIMIT),
        name="sgu_mixer",
    )(x, w_in.astype(_BF16), b_in.reshape(1, -1), ln_g.reshape(1, -1), ln_b.reshape(1, -1),
      ws, b_s.reshape(SGU_HEADS, SGU_CHUNK, 1), w_out.astype(_BF16),
      mix_g.reshape(1, d), mix_b.reshape(1, d))


def _all_max(a):
    return jnp.max(jnp.max(a, axis=0, keepdims=True), axis=1, keepdims=True)


def _all_min(a):
    return jnp.min(jnp.min(a, axis=0, keepdims=True), axis=1, keepdims=True)


def _all_sum(a):
    return jnp.sum(jnp.sum(a, axis=0, keepdims=True), axis=1, keepdims=True)


def _route_kernel(x_ref, whi_ref, wlo_ref, bias_ref, earlier_ref,
                  ek_ref, rk_ref, gk_ref, cnt_ref, carry_ref):
    @pl.when(pl.program_id(0) == 0)
    def _():
        carry_ref[...] = jnp.zeros_like(carry_ref)

    nt = lambda a, b: lax.dot_general(a, b, (((1,), (1,)), ((), ())),
                                      preferred_element_type=_F32)
    logits = []
    for p0 in range(0, x_ref.shape[0], ROUTE_SUB):
        x = x_ref[p0:p0 + ROUTE_SUB, :]
        x_hi = x.astype(_BF16)
        x_lo = (x - x_hi.astype(_F32)).astype(_BF16)
        logits.append(nt(whi_ref[...], x_hi) + (nt(whi_ref[...], x_lo) + nt(wlo_ref[...], x_hi)))
    for n, sub_logits in enumerate(logits):
        _route_sub_tile(sub_logits, n * ROUTE_SUB, bias_ref, earlier_ref,
                        ek_ref, rk_ref, gk_ref, carry_ref)
    cnt_ref[...] = carry_ref[...]


def _route_sub_tile(logits, p0, bias_ref, earlier_ref, ek_ref, rk_ref, gk_ref, carry_ref):
    ts = ROUTE_SUB
    scores = jax.nn.sigmoid(logits).reshape(N_GROUPS, GROUP_SIZE, ts)
    biased = scores + bias_ref[...]
    neg_inf = jnp.float32(-jnp.inf)
    shape3 = (N_GROUPS, GROUP_SIZE, ts)
    in_grp = lax.broadcasted_iota(jnp.int32, shape3, 1)
    grp = lax.broadcasted_iota(jnp.int32, shape3, 0)
    eid = grp * GROUP_SIZE + in_grp

    m1 = jnp.max(biased, axis=1, keepdims=True)
    first1 = jnp.min(jnp.where(biased == m1, in_grp, GROUP_SIZE), axis=1, keepdims=True)
    m2 = jnp.max(jnp.where(in_grp == first1, neg_inf, biased), axis=1, keepdims=True)
    gscore = m1 + m2

    gid = lax.broadcasted_iota(jnp.int32, (N_GROUPS, 1, ts), 0)
    gsel = jnp.zeros((N_GROUPS, 1, ts), jnp.bool_)
    for _ in range(TOPK_GROUPS):
        m = jnp.max(gscore, axis=0, keepdims=True)
        first = jnp.min(jnp.where(gscore == m, gid, N_GROUPS), axis=0, keepdims=True)
        pick = gid == first
        gsel = jnp.logical_or(gsel, pick)
        gscore = jnp.where(pick, neg_inf, gscore)

    masked = jnp.where(gsel, biased, neg_inf)
    picked_any = jnp.zeros(shape3, jnp.bool_)
    e_k, s_k = [], []
    for _ in range(TOP_K):
        m = _all_max(masked)
        first = _all_min(jnp.where(masked == m, eid, N_EXPERTS))
        pick = eid == first
        picked_any = jnp.logical_or(picked_any, pick)
        masked = jnp.where(pick, neg_inf, masked)
        e_k.append(first)
        s_k.append(_all_sum(jnp.where(pick, scores, 0.0)))
    denom = s_k[0]
    for k in range(1, TOP_K):
        denom = denom + s_k[k]

    sel = picked_any.astype(_F32).reshape(N_EXPERTS, ts)
    rank = _dot(sel.astype(_BF16), earlier_ref[...]) + carry_ref[...]
    rank3 = rank.reshape(shape3)
    carry_ref[...] += jnp.sum(sel, axis=1, keepdims=True)

    for k in range(TOP_K):
        r = _all_sum(jnp.where(eid == e_k[k], rank3, 0.0))
        ek_ref[k:k + 1, p0:p0 + ts] = e_k[k].reshape(1, ts)
        rk_ref[k:k + 1, p0:p0 + ts] = r.reshape(1, ts).astype(jnp.int32)
        gk_ref[k:k + 1, p0:p0 + ts] = (s_k[k] / denom * ROUTED_SCALE).reshape(1, ts)


def _route(x, w_router, router_bias):
    n_part, d = x.shape
    ts = ROUTE_TILE
    steps = n_part // ts
    out_shape = (jax.ShapeDtypeStruct((TOP_K, n_part), jnp.int32),
                 jax.ShapeDtypeStruct((TOP_K, n_part), jnp.int32),
                 jax.ShapeDtypeStruct((TOP_K, n_part), _F32),
                 jax.ShapeDtypeStruct((N_EXPERTS, 1), _F32))
    kspec = pl.BlockSpec((TOP_K, ts), lambda i: (0, i))
    const2 = lambda i: (0, 0)
    w_t = w_router.T
    w_hi = w_t.astype(_BF16)
    w_lo = (w_t - w_hi.astype(_F32)).astype(_BF16)
    pos = jnp.arange(ROUTE_SUB, dtype=jnp.int32)
    earlier = (pos[:, None] < pos[None, :]).astype(_BF16)
    return pl.pallas_call(
        _route_kernel,
        grid=(steps,),
        in_specs=[
            pl.BlockSpec((ts, d), lambda i: (i, 0)),
            pl.BlockSpec((N_EXPERTS, d), const2),
            pl.BlockSpec((N_EXPERTS, d), const2),
            pl.BlockSpec((N_GROUPS, GROUP_SIZE, 1), lambda i: (0, 0, 0)),
            pl.BlockSpec((ROUTE_SUB, ROUTE_SUB), const2),
        ],
        out_specs=(kspec, kspec, kspec, pl.BlockSpec((N_EXPERTS, 1), const2)),
        out_shape=out_shape,
        scratch_shapes=[pltpu.VMEM((N_EXPERTS, 1), _F32)],
        compiler_params=pltpu.CompilerParams(
            dimension_semantics=("arbitrary",),
            vmem_limit_bytes=VMEM_LIMIT),
        name="moe_route",
    )(x, w_hi, w_lo, router_bias.reshape(N_GROUPS, GROUP_SIZE, 1), earlier)


def _sc_mesh():
    return plsc.VectorSubcoreMesh(core_axis_name="core", subcore_axis_name="subcore")


def _sc_scatter_rows(src, idx, n_out):
    d = src.shape[1]
    wins = idx.shape[1] // SC_WINDOW

    def body(src_hbm, idx_hbm, out_hbm):
        def step(src_vmem, idx_vmem):
            for k in range(TOP_K):
                pltpu.sync_copy(src_vmem, out_hbm.at[idx_vmem.at[k]])

        pltpu.emit_pipeline(
            step,
            grid=(src.shape[0] // SC_WINDOW,),
            in_specs=[pl.BlockSpec((SC_WINDOW, d), index_map=lambda i: (i, 0)),
                      pl.BlockSpec((TOP_K, SC_WINDOW), index_map=lambda i: (i // wins, i % wins))],
            out_specs=[],
            core_axis_name=("core", "subcore"),
            dimension_semantics=(pltpu.PARALLEL,),
        )(src_hbm, idx_hbm)

    return pl.kernel(body, out_type=jax.ShapeDtypeStruct((n_out, d), src.dtype),
                     mesh=_sc_mesh(), scratch_types=[], name="sc_scatter_rows")(src, idx)


def _sc_gather_sum(table, idx, gates):
    d = table.shape[1]
    n_tok = idx.shape[1]
    wins = n_tok // SC_WINDOW
    lanes = SC_LANES
    out_type = jax.ShapeDtypeStruct((PIECES * n_tok, d), _F32)

    def tree_sum(terms):
        while len(terms) > 1:
            terms = [terms[i] + terms[i + 1] for i in range(0, len(terms), 2)]
        return terms[0]

    def body(table_hbm, idx_hbm, gate_hbm, lo_hbm, hi_hbm, rows_vmem, sem):
        def step(idx_vmem, gate_vmem, lo_vmem, hi_vmem):
            def burst_copies(b):
                return [pltpu.make_async_copy(
                    table_hbm.at[idx_vmem.at[k, pl.ds(b * SC_BURST, SC_BURST)]],
                    rows_vmem.at[b % 2, k], sem.at[b % 2]) for k in range(TOP_K)]

            n_bursts = SC_WINDOW // SC_BURST
            for copy in burst_copies(0):
                copy.start()
            for b in range(n_bursts):
                if b + 1 < n_bursts:
                    for copy in burst_copies(b + 1):
                        copy.start()
                for copy in burst_copies(b):
                    copy.wait()
                t0 = b * SC_BURST
                rows = rows_vmem.at[b % 2]

                @plsc.parallel_loop(0, SC_BURST)
                def _(r):
                    t = t0 + r
                    token = jnp.full((lanes,), t, jnp.int32)
                    gate = [plsc.load_gather(gate_vmem, [jnp.full((lanes,), k, jnp.int32), token])
                            for k in range(TOP_K)]
                    for j in range(0, d, lanes):
                        words = [rows[k, r, pl.ds(j, lanes)] for k in range(TOP_K)]
                        lo_vmem[t, pl.ds(j, lanes)] = tree_sum(
                            [gate[k] * lax.bitcast_convert_type(words[k] << 16, _F32)
                             for k in range(TOP_K)])
                        hi_vmem[t, pl.ds(j, lanes)] = tree_sum(
                            [gate[k] * lax.bitcast_convert_type(words[k] & _U32(0xFFFF0000), _F32)
                             for k in range(TOP_K)])

        window = lambda i: (i // wins, i % wins)
        pltpu.emit_pipeline(
            step,
            grid=(PIECES * wins,),
            in_specs=[pl.BlockSpec((TOP_K, SC_WINDOW), index_map=window),
                      pl.BlockSpec((TOP_K, SC_WINDOW), index_map=lambda i: (0, i % wins))],
            out_specs=[pl.BlockSpec((SC_WINDOW, d), index_map=lambda i: (i, 0)),
                       pl.BlockSpec((SC_WINDOW, d), index_map=lambda i: (i, 0))],
            core_axis_name=("core", "subcore"),
            dimension_semantics=(pltpu.PARALLEL,),
        )(idx_hbm, gate_hbm, lo_hbm, hi_hbm)

    return pl.kernel(body, out_type=(out_type, out_type), mesh=_sc_mesh(),
                     scratch_types=[pltpu.VMEM((2, TOP_K, SC_BURST, d), _U32),
                                    pltpu.SemaphoreType.DMA((2,))],
                     compiler_params=pltpu.CompilerParams(needs_layout_passes=False),
                     name="sc_gather_sum")(table, idx, gates)


def _index_kernel(start_ref, ek_ref, rk_ref, idx_ref, *, n_rows):
    e = ek_ref[...]
    start = jnp.zeros(e.shape, jnp.int32)
    for ex in range(N_EXPERTS):
        start = jnp.where(e == ex, start_ref[ex], start)
    dest = start + rk_ref[...]
    for c in range(PIECES):
        idx_ref[c] = dest + c * n_rows


def _row_indices(padded_start, e_k, r_k, n_rows):
    n_tok = e_k.shape[1]
    ts = INDEX_TILE
    kspec = pl.BlockSpec((TOP_K, ts), lambda i, st: (0, i))
    grid_spec = pltpu.PrefetchScalarGridSpec(
        num_scalar_prefetch=1,
        grid=(n_tok // ts,),
        in_specs=[kspec, kspec],
        out_specs=pl.BlockSpec((PIECES, TOP_K, ts), lambda i, st: (0, 0, i)),
    )
    return pl.pallas_call(
        functools.partial(_index_kernel, n_rows=n_rows),
        grid_spec=grid_spec,
        out_shape=jax.ShapeDtypeStruct((PIECES, TOP_K, n_tok), jnp.int32),
        compiler_params=pltpu.CompilerParams(dimension_semantics=("arbitrary",)),
        name="moe_row_indices",
    )(padded_start, e_k, r_k)


def _expert_kernel(chunk0_ref, nchunk_ref, cnt_ref, total_ref,
                   xs_hbm, wg_ref, wu_ref, wd_ref, o_hbm,
                   xbuf, obuf, sem_in, sem_out):
    e = pl.program_id(0)
    total = total_ref[0]
    ch = EXPERT_ROWS
    ahead = IN_SLOTS - 1

    def in_copy(g):
        slot = g % IN_SLOTS
        return pltpu.make_async_copy(xs_hbm.at[:, pl.ds(g * ch, ch), :], xbuf.at[slot],
                                     sem_in.at[slot])

    def out_copy(g):
        slot = g % OUT_SLOTS
        return pltpu.make_async_copy(obuf.at[slot], o_hbm.at[:, pl.ds(g * ch, ch), :],
                                     sem_out.at[slot])

    @pl.when(e == 0)
    def _():
        obuf[...] = jnp.zeros_like(obuf)
        for g in range(ahead):
            @pl.when(g < total)
            def _():
                in_copy(g).start()


    def chunk(j, carry):
        g = chunk0_ref[e] + j
        in_copy(g).wait()

        @pl.when(g + ahead < total)
        def _():
            in_copy(g + ahead).start()

        @pl.when(g >= OUT_SLOTS)
        def _():
            out_copy(g - OUT_SLOTS).wait()

        islot = g % IN_SLOTS
        oslot = g % OUT_SLOTS
        valid = cnt_ref[e] - j * ch

        def gate_up(s):
            r0 = s * EXPERT_SUB
            w = jnp.concatenate([xbuf[islot, c, r0:r0 + EXPERT_SUB, :] for c in range(PIECES)],
                                axis=1)
            lo, hi = _unpack_halves(w)
            gate = _dot(lo, wg_ref[0, 0, :PACKED, :]) + _dot(hi, wg_ref[0, 0, PACKED:, :])
            up = _dot(lo, wu_ref[0, 0, :PACKED, :]) + _dot(hi, wu_ref[0, 0, PACKED:, :])
            return gate, up

        def down(s, gate_up_pair):
            r0 = s * EXPERT_SUB
            gate, up = gate_up_pair
            y = _pack_halves(_dot(_silu(gate) * up, wd_ref[0, 0]))
            for c in range(PIECES):
                obuf[oslot, c, r0:r0 + EXPERT_SUB, :] = y[:, c * LANES:(c + 1) * LANES]

        n_sub = ch // EXPERT_SUB

        all_subs = valid > ch - EXPERT_SUB

        @pl.when(all_subs)
        def _():
            projected = [gate_up(s) for s in range(n_sub)]
            for s, gu in enumerate(projected):
                down(s, gu)

        @pl.when(jnp.logical_not(all_subs))
        def _():
            for s in range(n_sub - 1):
                @pl.when(s * EXPERT_SUB < valid)
                def _():
                    down(s, gate_up(s))

        out_copy(g).start()
        return carry

    lax.fori_loop(0, nchunk_ref[e], chunk, 0)

    @pl.when(e == pl.num_programs(0) - 1)
    def _():
        for back in range(OUT_SLOTS, 0, -1):
            @pl.when(total >= back)
            def _():
                out_copy(total - back).wait()


def _experts(layer, chunk0, nchunk, counts, total, xs, w_gate, w_up, w_down):
    d = D_MODEL
    ch = EXPERT_ROWS
    wspec_in = pl.BlockSpec((1, 1, d, EXPERT_DIM), lambda e, *_: (layer, e, 0, 0))
    any_spec = pl.BlockSpec(memory_space=pl.ANY)
    grid_spec = pltpu.PrefetchScalarGridSpec(
        num_scalar_prefetch=4,
        grid=(N_EXPERTS,),
        in_specs=[
            any_spec,
            wspec_in,
            wspec_in,
            pl.BlockSpec((1, 1, EXPERT_DIM, d), lambda e, *_: (layer, e, 0, 0)),
        ],
        out_specs=any_spec,
        scratch_shapes=[pltpu.VMEM((IN_SLOTS, PIECES, ch, LANES), _U32),
                        pltpu.VMEM((OUT_SLOTS, PIECES, ch, LANES), _U32),
                        pltpu.SemaphoreType.DMA((IN_SLOTS,)),
                        pltpu.SemaphoreType.DMA((OUT_SLOTS,))],
    )
    return pl.pallas_call(
        _expert_kernel,
        grid_spec=grid_spec,
        out_shape=jax.ShapeDtypeStruct(xs.shape, _U32),
        compiler_params=pltpu.CompilerParams(
            dimension_semantics=("arbitrary",),
            vmem_limit_bytes=VMEM_LIMIT),
        name="moe_experts",
    )(chunk0, nchunk, counts, total, xs, w_gate, w_up, w_down)


def _combine_kernel(x_ref, lo_ref, hi_ref, sg_ref, su_ref, sd_ref, g_ref, b_ref, *rest):
    o_ref = rest[-1]
    x = x_ref[...]
    xb = x.astype(_BF16)
    routed = jnp.concatenate([lo_ref[c] for c in range(PIECES)] +
                             [hi_ref[c] for c in range(PIECES)], axis=1)
    h = _silu(_dot(xb, sg_ref[...])) * _dot(xb, su_ref[...])
    shared = _dot(h.astype(_BF16), sd_ref[...])
    o_ref[...] = _layer_norm(DEEPNORM_ALPHA * x + (routed + shared), g_ref[...], b_ref[...])


def _combine(x, routed_lo, routed_hi, sh_gate, sh_up, sh_down, ln_g, ln_b, out_parts, part, prev):
    n_part, d = x.shape
    ts = COMBINE_TILE
    steps = n_part // ts
    const2 = lambda i: (0, 0)
    in_specs = [
        pl.BlockSpec((ts, d), lambda i: (i, 0)),
        pl.BlockSpec((PIECES, ts, LANES), lambda i: (0, i, 0)),
        pl.BlockSpec((PIECES, ts, LANES), lambda i: (0, i, 0)),
        pl.BlockSpec((d, EXPERT_DIM), const2),
        pl.BlockSpec((d, EXPERT_DIM), const2),
        pl.BlockSpec((EXPERT_DIM, d), const2),
        pl.BlockSpec((1, d), const2),
        pl.BlockSpec((1, d), const2),
    ]
    args = [x, routed_lo, routed_hi, sh_gate.astype(_BF16), sh_up.astype(_BF16), sh_down.astype(_BF16),
            ln_g.reshape(1, d), ln_b.reshape(1, d)]
    aliases = {}
    if prev is not None:
        in_specs.append(pl.BlockSpec(memory_space=pl.ANY))
        aliases = {len(args): 0}
        args.append(prev)
    return pl.pallas_call(
        _combine_kernel,
        grid=(steps,),
        in_specs=in_specs,
        out_specs=pl.BlockSpec((ts, d), lambda i: (part * steps + i, 0)),
        out_shape=jax.ShapeDtypeStruct((out_parts * n_part, d), _F32),
        input_output_aliases=aliases,
        compiler_params=pltpu.CompilerParams(
            dimension_semantics=("arbitrary",),
            vmem_limit_bytes=VMEM_LIMIT),
        name="moe_combine",
    )(*args)


def _moe_dispatch(layer, x, x_packed, w_router, router_bias, w_gate, w_up, w_down):
    n_part, d = x.shape
    bm = EXPERT_ROWS
    n_rows = n_part * TOP_K + N_EXPERTS * bm
    e_k, r_k, g_k, counts = _route(x, w_router, router_bias)
    counts = counts.reshape(N_EXPERTS).astype(jnp.int32)
    padded = (counts + bm - 1) // bm * bm
    padded_end = jnp.cumsum(padded)
    padded_start = padded_end - padded
    idx = _row_indices(padded_start, e_k, r_k, n_rows).reshape(PIECES * TOP_K, n_part)
    xs = _sc_scatter_rows(x_packed.reshape(PIECES * n_part, LANES), idx, PIECES * n_rows)
    rows = _experts(layer, padded_start // bm, padded // bm, counts, padded_end[-1:] // bm,
                    xs.reshape(PIECES, n_rows, LANES), w_gate, w_up, w_down)
    return rows.reshape(PIECES * n_rows, LANES), idx, g_k


def _moe_combine(x, rows, idx, gates, sh_gate, sh_up, sh_down, ln_g, ln_b,
                 out_parts=1, part=0, prev=None):
    n_part = x.shape[0]
    lo, hi = _sc_gather_sum(rows, idx, gates)
    return _combine(x, lo.reshape(PIECES, n_part, LANES), hi.reshape(PIECES, n_part, LANES),
                    sh_gate, sh_up, sh_down, ln_g, ln_b, out_parts, part, prev)


def kernel(x, pool_w_in, pool_w_grp, pool_scale, pool_w_out, sgu_w_in, sgu_b_in, sgu_ln_g, sgu_ln_b, sgu_w_s, sgu_b_s, sgu_w_out, ln_mix_g, ln_mix_b, moe_w_router, moe_router_bias, moe_w_gate, moe_w_up, moe_w_down, moe_sh_gate, moe_sh_up, moe_sh_down, ln_ffn_g, ln_ffn_b):
    bsz, seq, d = x.shape
    parts = range(TOKEN_PARTS)

    def dispatch(i, h, h_packed):
        return _moe_dispatch(i, h, h_packed, moe_w_router[i], moe_router_bias[i],
                             moe_w_gate, moe_w_up, moe_w_down)

    def combine(i, h, routed, **where):
        return _moe_combine(h, *routed, moe_sh_gate[i], moe_sh_up[i], moe_sh_down[i],
                            ln_ffn_g[i], ln_ffn_b[i], **where)

    mixed = [_pool_layer(x, p, pool_w_in[0], pool_w_grp[0], pool_scale[0], pool_w_out[0],
                         ln_mix_g[0], ln_mix_b[0]) for p in parts]
    mixed = [(h.reshape(-1, d), h_packed) for h, h_packed in mixed]
    routed = [dispatch(0, h, h_packed) for h, h_packed in mixed]
    hs = [combine(0, mixed[p][0], routed[p]) for p in parts]
    mixed = [_sgu_layer(h, sgu_w_in[0], sgu_b_in[0], sgu_ln_g[0], sgu_ln_b[0], sgu_w_s[0],
                        sgu_b_s[0], sgu_w_out[0], ln_mix_g[1], ln_mix_b[1]) for h in hs]
    routed = [dispatch(1, h, h_packed) for h, h_packed in mixed]
    out = None
    for p in parts:
        out = combine(1, mixed[p][0], routed[p], out_parts=TOKEN_PARTS, part=p, prev=out)
    return out.reshape(bsz, seq, d)
```

```python
import functools

import jax
import jax.numpy as jnp
from jax import lax
from jax.experimental import pallas as pl
from jax.experimental.pallas import tpu as pltpu
from jax.experimental.pallas import tpu_sc as plsc

D_MODEL = 1024
DEPTH = 2
POOL_WINDOWS = (2, 4, 8, 16)
POOL_GROUP_DIM = D_MODEL // len(POOL_WINDOWS)
POOL_HALO = 16
SGU_CHUNK = 128
SGU_HEADS = 4
SGU_WIDTH = 2 * D_MODEL
SGU_HEAD_DIM = SGU_WIDTH // SGU_HEADS
N_EXPERTS = 64
TOP_K = 8
N_GROUPS = 8
GROUP_SIZE = N_EXPERTS // N_GROUPS
TOPK_GROUPS = 4
EXPERT_DIM = D_MODEL // 4
ROUTED_SCALE = 2.5
DEEPNORM_ALPHA = (2 * DEPTH) ** 0.25
LN_EPS = 1e-5

LANES = 128
PACKED = D_MODEL // 2
PIECES = PACKED // LANES

POOL_TILE = 1024
POOL_SUB = 256
SGU_TILE = 512
SGU_SUB = 256
ROUTE_TILE = 1024
ROUTE_SUB = 512
INDEX_TILE = 2048
EXPERT_ROWS = 512
EXPERT_SUB = 256
IN_SLOTS = 6
OUT_SLOTS = 4
COMBINE_TILE = 1024
SC_WINDOW = 128
SC_LANES = 16
SC_BURST = 16
TOKEN_PARTS = 2
V7X_VMEM_BYTES = 64 * 1024 * 1024
VMEM_LIMIT = V7X_VMEM_BYTES * 7 // 8

_F32 = jnp.float32
_BF16 = jnp.bfloat16
_U32 = jnp.uint32


def _dot(a, b):
    return jnp.dot(a, b, preferred_element_type=_F32)


def _layer_norm(h, g, b):
    mu = jnp.mean(h, axis=-1, keepdims=True)
    hc = h - mu
    var = jnp.mean(hc * hc, axis=-1, keepdims=True)
    return hc * lax.rsqrt(var + LN_EPS) * g + b


def _silu(x):
    return x * jax.nn.sigmoid(x)


def _gelu_tanh(x):
    c = 0.7978845608028654
    return 0.5 * x * (1.0 + jnp.tanh(c * (x + 0.044715 * (x * x * x))))


def _pack_halves(v):
    half = v.shape[1] // 2
    lo = lax.bitcast_convert_type(v[:, :half].astype(_BF16).astype(_F32), _U32)
    hi = lax.bitcast_convert_type(v[:, half:].astype(_BF16).astype(_F32), _U32)
    return (hi & _U32(0xFFFF0000)) | (lo >> 16)


def _unpack_halves(w):
    lo = lax.bitcast_convert_type(w << 16, _F32)
    hi = lax.bitcast_convert_type(w & _U32(0xFFFF0000), _F32)
    return lo, hi


def _store_pieces(ref, r0, w):
    for c in range(PIECES):
        ref[c, r0:r0 + w.shape[0], :] = w[:, c * LANES:(c + 1) * LANES]


def _pool_kernel(x_ref, win_ref, wgrp_ref, scale_ref, wout_ref, g_ref, b_ref,
                 o_ref, op_ref, zs_ref, y_ref):
    s = pl.program_id(1)
    ts = x_ref.shape[1]

    @pl.when(s == 0)
    def _():
        zs_ref[0:POOL_HALO, :] = jnp.zeros((POOL_HALO, D_MODEL), _F32)

    for p0 in range(0, ts, POOL_SUB):
        z = _dot(x_ref[0, p0:p0 + POOL_SUB, :].astype(_BF16), win_ref[...])
        zs_ref[POOL_HALO + p0:POOL_HALO + p0 + POOL_SUB, :] = z
    for p0 in range(0, ts, POOL_SUB):
        x = x_ref[0, p0:p0 + POOL_SUB, :]
        base = POOL_HALO + p0
        pos = s * ts + p0 + lax.broadcasted_iota(jnp.int32, (POOL_SUB, 1), 0)
        for g, w in enumerate(POOL_WINDOWS):
            c0 = g * POOL_GROUP_DIM
            c1 = c0 + POOL_GROUP_DIM
            zg = zs_ref[base:base + POOL_SUB, c0:c1]
            acc = zg
            for k in range(1, w):
                acc = acc + zs_ref[base - k:base - k + POOL_SUB, c0:c1]
            cnt = jnp.minimum(pos + 1, w).astype(_F32)
            pooled = acc / cnt - zg
            yg = _dot(pooled.astype(_BF16), wgrp_ref[g]) * scale_ref[:, c0:c1]
            y_ref[p0:p0 + POOL_SUB, c0:c1] = yg.astype(_BF16)
        mix = _dot(y_ref[p0:p0 + POOL_SUB, :], wout_ref[...])
        out = _layer_norm(DEEPNORM_ALPHA * x + mix, g_ref[...], b_ref[...])
        o_ref[0, p0:p0 + POOL_SUB, :] = out
        _store_pieces(op_ref, p0, _pack_halves(out))
    zs_ref[0:POOL_HALO, :] = zs_ref[ts:ts + POOL_HALO, :]


def _pool_layer(x, part, w_in, w_grp, scale, w_out, ln_g, ln_b):
    bsz, seq, d = x.shape
    rows = bsz // TOKEN_PARTS
    ts = POOL_TILE
    steps = seq // ts
    const2 = lambda b, s: (0, 0)
    out_shape = (jax.ShapeDtypeStruct((rows, seq, d), _F32),
                 jax.ShapeDtypeStruct((PIECES, rows * seq, LANES), _U32))
    tile = pl.BlockSpec((1, ts, d), lambda b, s: (b, s, 0))
    ptile = pl.BlockSpec((PIECES, ts, LANES), lambda b, s: (0, b * steps + s, 0))
    return pl.pallas_call(
        _pool_kernel,
        grid=(rows, steps),
        in_specs=[
            pl.BlockSpec((1, ts, d), lambda b, s: (part * rows + b, s, 0)),
            pl.BlockSpec((d, d), const2),
            pl.BlockSpec((len(POOL_WINDOWS), POOL_GROUP_DIM, POOL_GROUP_DIM), lambda b, s: (0, 0, 0)),
            pl.BlockSpec((1, d), const2),
            pl.BlockSpec((d, d), const2),
            pl.BlockSpec((1, d), const2),
            pl.BlockSpec((1, d), const2),
        ],
        out_specs=(tile, ptile),
        out_shape=out_shape,
        scratch_shapes=[pltpu.VMEM((POOL_HALO + ts, d), _F32),
                        pltpu.VMEM((ts, d), _BF16)],
        compiler_params=pltpu.CompilerParams(
            dimension_semantics=("arbitrary", "arbitrary"),
            vmem_limit_bytes=VMEM_LIMIT),
        name="pool_mixer",
    )(x, w_in.astype(_BF16), w_grp.astype(_BF16), scale.reshape(1, d),
      w_out.astype(_BF16), ln_g.reshape(1, d), ln_b.reshape(1, d))


def _sgu_kernel(x_ref, win_ref, bin_ref, lng_ref, lnb_ref, ws_ref, bs_ref, wout_ref,
                g_ref, b_ref, o_ref, op_ref, gated_ref):
    ts = x_ref.shape[0]
    projected = []
    for p0 in range(0, ts, SGU_SUB):
        x = x_ref[p0:p0 + SGU_SUB, :]
        xb = x.astype(_BF16)
        v = _gelu_tanh(_dot(xb, win_ref[:, SGU_WIDTH:]) + bin_ref[:, SGU_WIDTH:])
        v = _layer_norm(v, lng_ref[...], lnb_ref[...]).astype(_BF16)
        u = _gelu_tanh(_dot(xb, win_ref[:, :SGU_WIDTH]) + bin_ref[:, :SGU_WIDTH])
        projected.append((p0, x, u, v))
    for p0, x, u, v in projected:
        for r0 in range(0, SGU_SUB, SGU_CHUNK):
            for h in range(SGU_HEADS):
                c0 = h * SGU_HEAD_DIM
                sv = _dot(ws_ref[h], v[r0:r0 + SGU_CHUNK, c0:c0 + SGU_HEAD_DIM]) + bs_ref[h]
                gated_ref[p0 + r0:p0 + r0 + SGU_CHUNK, c0:c0 + SGU_HEAD_DIM] = (
                    u[r0:r0 + SGU_CHUNK, c0:c0 + SGU_HEAD_DIM] * sv).astype(_BF16)
        mix = _dot(gated_ref[p0:p0 + SGU_SUB, :], wout_ref[...])
        out = _layer_norm(DEEPNORM_ALPHA * x + mix, g_ref[...], b_ref[...])
        o_ref[p0:p0 + SGU_SUB, :] = out
        _store_pieces(op_ref, p0, _pack_halves(out))


def _sgu_layer(x, w_in, b_in, ln_g, ln_b, w_s, b_s, w_out, mix_g, mix_b):
    n_tok, d = x.shape
    ts = SGU_TILE
    const2 = lambda i: (0, 0)
    const3 = lambda i: (0, 0, 0)
    causal = jnp.tril(jnp.ones((SGU_CHUNK, SGU_CHUNK), w_s.dtype))
    ws = (w_s * causal[None]).astype(_BF16)
    tile = pl.BlockSpec((ts, d), lambda i: (i, 0))
    ptile = pl.BlockSpec((PIECES, ts, LANES), lambda i: (0, i, 0))
    out_shape = (jax.ShapeDtypeStruct((n_tok, d), _F32),
                 jax.ShapeDtypeStruct((PIECES, n_tok, LANES), _U32))
    return pl.pallas_call(
        _sgu_kernel,
        grid=(n_tok // ts,),
        in_specs=[
            tile,
            pl.BlockSpec((d, 2 * SGU_WIDTH), const2),
            pl.BlockSpec((1, 2 * SGU_WIDTH), const2),
            pl.BlockSpec((1, SGU_WIDTH), const2),
            pl.BlockSpec((1, SGU_WIDTH), const2),
            pl.BlockSpec((SGU_HEADS, SGU_CHUNK, SGU_CHUNK), const3),
            pl.BlockSpec((SGU_HEADS, SGU_CHUNK, 1), const3),
            pl.BlockSpec((SGU_WIDTH, d), const2),
            pl.BlockSpec((1, d), const2),
            pl.BlockSpec((1, d), const2),
        ],
        out_specs=(tile, ptile),
        out_shape=out_shape,
        scratch_shapes=[pltpu.VMEM((ts, SGU_WIDTH), _BF16)],
        compiler_params=pltpu.CompilerParams(
            dimension_semantics=("arbitrary",),
            vmem_limit_bytes=VMEM_LIMIT),
        name="sgu_mixer",
    )(x, w_in.astype(_BF16), b_in.reshape(1, -1), ln_g.reshape(1, -1), ln_b.reshape(1, -1),
      ws, b_s.reshape(SGU_HEADS, SGU_CHUNK, 1), w_out.astype(_BF16),
      mix_g.reshape(1, d), mix_b.reshape(1, d))


def _all_max(a):
    return jnp.max(jnp.max(a, axis=0, keepdims=True), axis=1, keepdims=True)


def _all_min(a):
    return jnp.min(jnp.min(a, axis=0, keepdims=True), axis=1, keepdims=True)


def _all_sum(a):
    return jnp.sum(jnp.sum(a, axis=0, keepdims=True), axis=1, keepdims=True)


def _route_kernel(x_ref, whi_ref, wlo_ref, bias_ref, earlier_ref,
                  ek_ref, rk_ref, gk_ref, cnt_ref, carry_ref):
    @pl.when(pl.program_id(0) == 0)
    def _():
        carry_ref[...] = jnp.zeros_like(carry_ref)

    nt = lambda a, b: lax.dot_general(a, b, (((1,), (1,)), ((), ())),
                                      preferred_element_type=_F32)
    logits = []
    for p0 in range(0, x_ref.shape[0], ROUTE_SUB):
        x = x_ref[p0:p0 + ROUTE_SUB, :]
        x_hi = x.astype(_BF16)
        x_lo = (x - x_hi.astype(_F32)).astype(_BF16)
        logits.append(nt(whi_ref[...], x_hi) + (nt(whi_ref[...], x_lo) + nt(wlo_ref[...], x_hi)))
    for n, sub_logits in enumerate(logits):
        _route_sub_tile(sub_logits, n * ROUTE_SUB, bias_ref, earlier_ref,
                        ek_ref, rk_ref, gk_ref, carry_ref)
    cnt_ref[...] = carry_ref[...]


def _route_sub_tile(logits, p0, bias_ref, earlier_ref, ek_ref, rk_ref, gk_ref, carry_ref):
    ts = ROUTE_SUB
    scores = jax.nn.sigmoid(logits).reshape(N_GROUPS, GROUP_SIZE, ts)
    biased = scores + bias_ref[...]
    neg_inf = jnp.float32(-jnp.inf)
    shape3 = (N_GROUPS, GROUP_SIZE, ts)
    in_grp = lax.broadcasted_iota(jnp.int32, shape3, 1)
    grp = lax.broadcasted_iota(jnp.int32, shape3, 0)
    eid = grp * GROUP_SIZE + in_grp

    m1 = jnp.max(biased, axis=1, keepdims=True)
    first1 = jnp.min(jnp.where(biased == m1, in_grp, GROUP_SIZE), axis=1, keepdims=True)
    m2 = jnp.max(jnp.where(in_grp == first1, neg_inf, biased), axis=1, keepdims=True)
    gscore = m1 + m2

    gid = lax.broadcasted_iota(jnp.int32, (N_GROUPS, 1, ts), 0)
    gsel = jnp.zeros((N_GROUPS, 1, ts), jnp.bool_)
    for _ in range(TOPK_GROUPS):
        m = jnp.max(gscore, axis=0, keepdims=True)
        first = jnp.min(jnp.where(gscore == m, gid, N_GROUPS), axis=0, keepdims=True)
        pick = gid == first
        gsel = jnp.logical_or(gsel, pick)
        gscore = jnp.where(pick, neg_inf, gscore)

    masked = jnp.where(gsel, biased, neg_inf)
    picked_any = jnp.zeros(shape3, jnp.bool_)
    e_k, s_k = [], []
    for _ in range(TOP_K):
        m = _all_max(masked)
        first = _all_min(jnp.where(masked == m, eid, N_EXPERTS))
        pick = eid == first
        picked_any = jnp.logical_or(picked_any, pick)
        masked = jnp.where(pick, neg_inf, masked)
        e_k.append(first)
        s_k.append(_all_sum(jnp.where(pick, scores, 0.0)))
    denom = s_k[0]
    for k in range(1, TOP_K):
        denom = denom + s_k[k]

    sel = picked_any.astype(_F32).reshape(N_EXPERTS, ts)
    rank = _dot(sel.astype(_BF16), earlier_ref[...]) + carry_ref[...]
    rank3 = rank.reshape(shape3)
    carry_ref[...] += jnp.sum(sel, axis=1, keepdims=True)

    for k in range(TOP_K):
        r = _all_sum(jnp.where(eid == e_k[k], rank3, 0.0))
        ek_ref[k:k + 1, p0:p0 + ts] = e_k[k].reshape(1, ts)
        rk_ref[k:k + 1, p0:p0 + ts] = r.reshape(1, ts).astype(jnp.int32)
        gk_ref[k:k + 1, p0:p0 + ts] = (s_k[k] / denom * ROUTED_SCALE).reshape(1, ts)


def _route(x, w_router, router_bias):
    n_part, d = x.shape
    ts = ROUTE_TILE
    steps = n_part // ts
    out_shape = (jax.ShapeDtypeStruct((TOP_K, n_part), jnp.int32),
                 jax.ShapeDtypeStruct((TOP_K, n_part), jnp.int32),
                 jax.ShapeDtypeStruct((TOP_K, n_part), _F32),
                 jax.ShapeDtypeStruct((N_EXPERTS, 1), _F32))
    kspec = pl.BlockSpec((TOP_K, ts), lambda i: (0, i))
    const2 = lambda i: (0, 0)
    w_t = w_router.T
    w_hi = w_t.astype(_BF16)
    w_lo = (w_t - w_hi.astype(_F32)).astype(_BF16)
    pos = jnp.arange(ROUTE_SUB, dtype=jnp.int32)
    earlier = (pos[:, None] < pos[None, :]).astype(_BF16)
    return pl.pallas_call(
        _route_kernel,
        grid=(steps,),
        in_specs=[
            pl.BlockSpec((ts, d), lambda i: (i, 0)),
            pl.BlockSpec((N_EXPERTS, d), const2),
            pl.BlockSpec((N_EXPERTS, d), const2),
            pl.BlockSpec((N_GROUPS, GROUP_SIZE, 1), lambda i: (0, 0, 0)),
            pl.BlockSpec((ROUTE_SUB, ROUTE_SUB), const2),
        ],
        out_specs=(kspec, kspec, kspec, pl.BlockSpec((N_EXPERTS, 1), const2)),
        out_shape=out_shape,
        scratch_shapes=[pltpu.VMEM((N_EXPERTS, 1), _F32)],
        compiler_params=pltpu.CompilerParams(
            dimension_semantics=("arbitrary",),
            vmem_limit_bytes=VMEM_LIMIT),
        name="moe_route",
    )(x, w_hi, w_lo, router_bias.reshape(N_GROUPS, GROUP_SIZE, 1), earlier)


def _sc_mesh():
    return plsc.VectorSubcoreMesh(core_axis_name="core", subcore_axis_name="subcore")


def _sc_scatter_rows(src, idx, n_out):
    d = src.shape[1]
    wins = idx.shape[1] // SC_WINDOW

    def body(src_hbm, idx_hbm, out_hbm, sem):
        def step(src_vmem, idx_vmem):
            copies = [pltpu.make_async_copy(src_vmem, out_hbm.at[idx_vmem.at[k]], sem)
                      for k in range(TOP_K)]
            for copy in copies:
                copy.start()
            for copy in copies:
                copy.wait()

        pltpu.emit_pipeline(
            step,
            grid=(src.shape[0] // SC_WINDOW,),
            in_specs=[pl.BlockSpec((SC_WINDOW, d), index_map=lambda i: (i, 0)),
                      pl.BlockSpec((TOP_K, SC_WINDOW), index_map=lambda i: (i // wins, i % wins))],
            out_specs=[],
            core_axis_name=("core", "subcore"),
            dimension_semantics=(pltpu.PARALLEL,),
        )(src_hbm, idx_hbm)

    return pl.kernel(body, out_type=jax.ShapeDtypeStruct((n_out, d), src.dtype),
                     mesh=_sc_mesh(), scratch_types=[pltpu.SemaphoreType.DMA],
                     name="sc_scatter_rows")(src, idx)


def _sc_gather_sum(table, idx, gates):
    d = table.shape[1]
    n_tok = idx.shape[1]
    wins = n_tok // SC_WINDOW
    lanes = SC_LANES
    out_type = jax.ShapeDtypeStruct((PIECES * n_tok, d), _F32)

    def tree_sum(terms):
        while len(terms) > 1:
            terms = [terms[i] + terms[i + 1] for i in range(0, len(terms), 2)]
        return terms[0]

    def body(table_hbm, idx_hbm, gate_hbm, lo_hbm, hi_hbm, rows_vmem, sem):
        def step(idx_vmem, gate_vmem, lo_vmem, hi_vmem):
            def burst_copies(b):
                return [pltpu.make_async_copy(
                    table_hbm.at[idx_vmem.at[k, pl.ds(b * SC_BURST, SC_BURST)]],
                    rows_vmem.at[b % 2, k], sem.at[b % 2]) for k in range(TOP_K)]

            n_bursts = SC_WINDOW // SC_BURST
            for copy in burst_copies(0):
                copy.start()
            for b in range(n_bursts):
                if b + 1 < n_bursts:
                    for copy in burst_copies(b + 1):
                        copy.start()
                for copy in burst_copies(b):
                    copy.wait()
                t0 = b * SC_BURST
                rows = rows_vmem.at[b % 2]

                @plsc.parallel_loop(0, SC_BURST)
                def _(r):
                    t = t0 + r
                    token = jnp.full((lanes,), t, jnp.int32)
                    gate = [plsc.load_gather(gate_vmem, [jnp.full((lanes,), k, jnp.int32), token])
                            for k in range(TOP_K)]
                    for j in range(0, d, lanes):
                        words = [rows[k, r, pl.ds(j, lanes)] for k in range(TOP_K)]
                        lo_vmem[t, pl.ds(j, lanes)] = tree_sum(
                            [gate[k] * lax.bitcast_convert_type(words[k] << 16, _F32)
                             for k in range(TOP_K)])
                        hi_vmem[t, pl.ds(j, lanes)] = tree_sum(
                            [gate[k] * lax.bitcast_convert_type(words[k] & _U32(0xFFFF0000), _F32)
                             for k in range(TOP_K)])

        window = lambda i: (i // wins, i % wins)
        pltpu.emit_pipeline(
            step,
            grid=(PIECES * wins,),
            in_specs=[pl.BlockSpec((TOP_K, SC_WINDOW), index_map=window),
                      pl.BlockSpec((TOP_K, SC_WINDOW), index_map=lambda i: (0, i % wins))],
            out_specs=[pl.BlockSpec((SC_WINDOW, d), index_map=lambda i: (i, 0)),
                       pl.BlockSpec((SC_WINDOW, d), index_map=lambda i: (i, 0))],
            core_axis_name=("core", "subcore"),
            dimension_semantics=(pltpu.PARALLEL,),
        )(idx_hbm, gate_hbm, lo_hbm, hi_hbm)

    return pl.kernel(body, out_type=(out_type, out_type), mesh=_sc_mesh(),
                     scratch_types=[pltpu.VMEM((2, TOP_K, SC_BURST, d), _U32),
                                    pltpu.SemaphoreType.DMA((2,))],
                     compiler_params=pltpu.CompilerParams(needs_layout_passes=False),
                     name="sc_gather_sum")(table, idx, gates)


def _index_kernel(start_ref, ek_ref, rk_ref, idx_ref, *, n_rows):
    e = ek_ref[...]
    start = jnp.zeros(e.shape, jnp.int32)
    for ex in range(N_EXPERTS):
        start = jnp.where(e == ex, start_ref[ex], start)
    dest = start + rk_ref[...]
    for c in range(PIECES):
        idx_ref[c] = dest + c * n_rows


def _row_indices(padded_start, e_k, r_k, n_rows):
    n_tok = e_k.shape[1]
    ts = INDEX_TILE
    kspec = pl.BlockSpec((TOP_K, ts), lambda i, st: (0, i))
    grid_spec = pltpu.PrefetchScalarGridSpec(
        num_scalar_prefetch=1,
        grid=(n_tok // ts,),
        in_specs=[kspec, kspec],
        out_specs=pl.BlockSpec((PIECES, TOP_K, ts), lambda i, st: (0, 0, i)),
    )
    return pl.pallas_call(
        functools.partial(_index_kernel, n_rows=n_rows),
        grid_spec=grid_spec,
        out_shape=jax.ShapeDtypeStruct((PIECES, TOP_K, n_tok), jnp.int32),
        compiler_params=pltpu.CompilerParams(dimension_semantics=("arbitrary",)),
        name="moe_row_indices",
    )(padded_start, e_k, r_k)


def _expert_kernel(chunk0_ref, nchunk_ref, cnt_ref, total_ref,
                   xs_hbm, wg_ref, wu_ref, wd_ref, o_hbm,
                   xbuf, obuf, sem_in, sem_out):
    e = pl.program_id(0)
    total = total_ref[0]
    ch = EXPERT_ROWS
    ahead = IN_SLOTS - 1

    def in_copy(g):
        slot = g % IN_SLOTS
        return pltpu.make_async_copy(xs_hbm.at[:, pl.ds(g * ch, ch), :], xbuf.at[slot],
                                     sem_in.at[slot])

    def out_copy(g):
        slot = g % OUT_SLOTS
        return pltpu.make_async_copy(obuf.at[slot], o_hbm.at[:, pl.ds(g * ch, ch), :],
                                     sem_out.at[slot])

    @pl.when(e == 0)
    def _():
        obuf[...] = jnp.zeros_like(obuf)
        for g in range(ahead):
            @pl.when(g < total)
            def _():
                in_copy(g).start()


    def chunk(j, carry):
        g = chunk0_ref[e] + j
        in_copy(g).wait()

        @pl.when(g + ahead < total)
        def _():
            in_copy(g + ahead).start()

        @pl.when(g >= OUT_SLOTS)
        def _():
            out_copy(g - OUT_SLOTS).wait()

        islot = g % IN_SLOTS
        oslot = g % OUT_SLOTS
        valid = cnt_ref[e] - j * ch

        def gate_up(s):
            r0 = s * EXPERT_SUB
            w = jnp.concatenate([xbuf[islot, c, r0:r0 + EXPERT_SUB, :] for c in range(PIECES)],
                                axis=1)
            lo, hi = _unpack_halves(w)
            gate = _dot(lo, wg_ref[0, 0, :PACKED, :]) + _dot(hi, wg_ref[0, 0, PACKED:, :])
            up = _dot(lo, wu_ref[0, 0, :PACKED, :]) + _dot(hi, wu_ref[0, 0, PACKED:, :])
            return gate, up

        def down(s, gate_up_pair):
            r0 = s * EXPERT_SUB
            gate, up = gate_up_pair
            y = _pack_halves(_dot(_silu(gate) * up, wd_ref[0, 0]))
            for c in range(PIECES):
                obuf[oslot, c, r0:r0 + EXPERT_SUB, :] = y[:, c * LANES:(c + 1) * LANES]

        n_sub = ch // EXPERT_SUB

        all_subs = valid > ch - EXPERT_SUB

        @pl.when(all_subs)
        def _():
            projected = [gate_up(s) for s in range(n_sub)]
            for s, gu in enumerate(projected):
                down(s, gu)

        @pl.when(jnp.logical_not(all_subs))
        def _():
            for s in range(n_sub - 1):
                @pl.when(s * EXPERT_SUB < valid)
                def _():
                    down(s, gate_up(s))

        out_copy(g).start()
        return carry

    lax.fori_loop(0, nchunk_ref[e], chunk, 0)

    @pl.when(e == pl.num_programs(0) - 1)
    def _():
        for back in range(OUT_SLOTS, 0, -1):
            @pl.when(total >= back)
            def _():
                out_copy(total - back).wait()


def _experts(layer, chunk0, nchunk, counts, total, xs, w_gate, w_up, w_down):
    d = D_MODEL
    ch = EXPERT_ROWS
    wspec_in = pl.BlockSpec((1, 1, d, EXPERT_DIM), lambda e, *_: (layer, e, 0, 0))
    any_spec = pl.BlockSpec(memory_space=pl.ANY)
    grid_spec = pltpu.PrefetchScalarGridSpec(
        num_scalar_prefetch=4,
        grid=(N_EXPERTS,),
        in_specs=[
            any_spec,
            wspec_in,
            wspec_in,
            pl.BlockSpec((1, 1, EXPERT_DIM, d), lambda e, *_: (layer, e, 0, 0)),
        ],
        out_specs=any_spec,
        scratch_shapes=[pltpu.VMEM((IN_SLOTS, PIECES, ch, LANES), _U32),
                        pltpu.VMEM((OUT_SLOTS, PIECES, ch, LANES), _U32),
                        pltpu.SemaphoreType.DMA((IN_SLOTS,)),
                        pltpu.SemaphoreType.DMA((OUT_SLOTS,))],
    )
    return pl.pallas_call(
        _expert_kernel,
        grid_spec=grid_spec,
        out_shape=jax.ShapeDtypeStruct(xs.shape, _U32),
        compiler_params=pltpu.CompilerParams(
            dimension_semantics=("arbitrary",),
            vmem_limit_bytes=VMEM_LIMIT),
        name="moe_experts",
    )(chunk0, nchunk, counts, total, xs, w_gate, w_up, w_down)


def _combine_kernel(x_ref, lo_ref, hi_ref, sg_ref, su_ref, sd_ref, g_ref, b_ref, *rest):
    o_ref = rest[-1]
    x = x_ref[...]
    xb = x.astype(_BF16)
    routed = jnp.concatenate([lo_ref[c] for c in range(PIECES)] +
                             [hi_ref[c] for c in range(PIECES)], axis=1)
    h = _silu(_dot(xb, sg_ref[...])) * _dot(xb, su_ref[...])
    shared = _dot(h.astype(_BF16), sd_ref[...])
    o_ref[...] = _layer_norm(DEEPNORM_ALPHA * x + (routed + shared), g_ref[...], b_ref[...])


def _combine(x, routed_lo, routed_hi, sh_gate, sh_up, sh_down, ln_g, ln_b, out_parts, part, prev):
    n_part, d = x.shape
    ts = COMBINE_TILE
    steps = n_part // ts
    const2 = lambda i: (0, 0)
    in_specs = [
        pl.BlockSpec((ts, d), lambda i: (i, 0)),
        pl.BlockSpec((PIECES, ts, LANES), lambda i: (0, i, 0)),
        pl.BlockSpec((PIECES, ts, LANES), lambda i: (0, i, 0)),
        pl.BlockSpec((d, EXPERT_DIM), const2),
        pl.BlockSpec((d, EXPERT_DIM), const2),
        pl.BlockSpec((EXPERT_DIM, d), const2),
        pl.BlockSpec((1, d), const2),
        pl.BlockSpec((1, d), const2),
    ]
    args = [x, routed_lo, routed_hi, sh_gate.astype(_BF16), sh_up.astype(_BF16), sh_down.astype(_BF16),
            ln_g.reshape(1, d), ln_b.reshape(1, d)]
    aliases = {}
    if prev is not None:
        in_specs.append(pl.BlockSpec(memory_space=pl.ANY))
        aliases = {len(args): 0}
        args.append(prev)
    return pl.pallas_call(
        _combine_kernel,
        grid=(steps,),
        in_specs=in_specs,
        out_specs=pl.BlockSpec((ts, d), lambda i: (part * steps + i, 0)),
        out_shape=jax.ShapeDtypeStruct((out_parts * n_part, d), _F32),
        input_output_aliases=aliases,
        compiler_params=pltpu.CompilerParams(
            dimension_semantics=("arbitrary",),
            vmem_limit_bytes=VMEM_LIMIT),
        name="moe_combine",
    )(*args)


def _moe_dispatch(layer, x, x_packed, w_router, router_bias, w_gate, w_up, w_down):
    n_part, d = x.shape
    bm = EXPERT_ROWS
    n_rows = n_part * TOP_K + N_EXPERTS * bm
    e_k, r_k, g_k, counts = _route(x, w_router, router_bias)
    counts = counts.reshape(N_EXPERTS).astype(jnp.int32)
    padded = (counts + bm - 1) // bm * bm
    padded_end = jnp.cumsum(padded)
    padded_start = padded_end - padded
    idx = _row_indices(padded_start, e_k, r_k, n_rows).reshape(PIECES * TOP_K, n_part)
    xs = _sc_scatter_rows(x_packed.reshape(PIECES * n_part, LANES), idx, PIECES * n_rows)
    rows = _experts(layer, padded_start // bm, padded // bm, counts, padded_end[-1:] // bm,
                    xs.reshape(PIECES, n_rows, LANES), w_gate, w_up, w_down)
    return rows.reshape(PIECES * n_rows, LANES), idx, g_k


def _moe_combine(x, rows, idx, gates, sh_gate, sh_up, sh_down, ln_g, ln_b,
                 out_parts=1, part=0, prev=None):
    n_part = x.shape[0]
    lo, hi = _sc_gather_sum(rows, idx, gates)
    return _combine(x, lo.reshape(PIECES, n_part, LANES), hi.reshape(PIECES, n_part, LANES),
                    sh_gate, sh_up, sh_down, ln_g, ln_b, out_parts, part, prev)


def kernel(x, pool_w_in, pool_w_grp, pool_scale, pool_w_out, sgu_w_in, sgu_b_in, sgu_ln_g, sgu_ln_b, sgu_w_s, sgu_b_s, sgu_w_out, ln_mix_g, ln_mix_b, moe_w_router, moe_router_bias, moe_w_gate, moe_w_up, moe_w_down, moe_sh_gate, moe_sh_up, moe_sh_down, ln_ffn_g, ln_ffn_b):
    bsz, seq, d = x.shape
    parts = range(TOKEN_PARTS)

    def dispatch(i, h, h_packed):
        return _moe_dispatch(i, h, h_packed, moe_w_router[i], moe_router_bias[i],
                             moe_w_gate, moe_w_up, moe_w_down)

    def combine(i, h, routed, **where):
        return _moe_combine(h, *routed, moe_sh_gate[i], moe_sh_up[i], moe_sh_down[i],
                            ln_ffn_g[i], ln_ffn_b[i], **where)

    mixed = [_pool_layer(x, p, pool_w_in[0], pool_w_grp[0], pool_scale[0], pool_w_out[0],
                         ln_mix_g[0], ln_mix_b[0]) for p in parts]
    mixed = [(h.reshape(-1, d), h_packed) for h, h_packed in mixed]
    routed = [dispatch(0, h, h_packed) for h, h_packed in mixed]
    hs = [combine(0, mixed[p][0], routed[p]) for p in parts]
    mixed = [_sgu_layer(h, sgu_w_in[0], sgu_b_in[0], sgu_ln_g[0], sgu_ln_b[0], sgu_w_s[0],
                        sgu_b_s[0], sgu_w_out[0], ln_mix_g[1], ln_mix_b[1]) for h in hs]
    routed = [dispatch(1, h, h_packed) for h, h_packed in mixed]
    out = None
    for p in parts:
        out = combine(1, mixed[p][0], routed[p], out_parts=TOKEN_PARTS, part=p, prev=out)
    return out.reshape(bsz, seq, d)
```

```python
import functools

import jax
import jax.numpy as jnp
from jax import lax
from jax.experimental import pallas as pl
from jax.experimental.pallas import tpu as pltpu
from jax.experimental.pallas import tpu_sc as plsc

D_MODEL = 1024
DEPTH = 2
POOL_WINDOWS = (2, 4, 8, 16)
POOL_GROUP_DIM = D_MODEL // len(POOL_WINDOWS)
POOL_HALO = 16
SGU_CHUNK = 128
SGU_HEADS = 4
SGU_WIDTH = 2 * D_MODEL
SGU_HEAD_DIM = SGU_WIDTH // SGU_HEADS
N_EXPERTS = 64
TOP_K = 8
N_GROUPS = 8
GROUP_SIZE = N_EXPERTS // N_GROUPS
TOPK_GROUPS = 4
EXPERT_DIM = D_MODEL // 4
ROUTED_SCALE = 2.5
DEEPNORM_ALPHA = (2 * DEPTH) ** 0.25
LN_EPS = 1e-5

LANES = 128
PACKED = D_MODEL // 2
PIECES = PACKED // LANES

POOL_TILE = 1024
POOL_SUB = 256
SGU_TILE = 512
SGU_SUB = 256
ROUTE_TILE = 1024
ROUTE_SUB = 512
INDEX_TILE = 2048
EXPERT_ROWS = 512
EXPERT_SUB = 256
IN_SLOTS = 6
OUT_SLOTS = 4
COMBINE_TILE = 512
SC_WINDOW = 128
SC_LANES = 16
SC_BURST = 16
TOKEN_PARTS = 2
V7X_VMEM_BYTES = 64 * 1024 * 1024
VMEM_LIMIT = V7X_VMEM_BYTES * 7 // 8

_F32 = jnp.float32
_BF16 = jnp.bfloat16
_U32 = jnp.uint32


def _dot(a, b):
    return jnp.dot(a, b, preferred_element_type=_F32)


def _layer_norm(h, g, b):
    mu = jnp.mean(h, axis=-1, keepdims=True)
    hc = h - mu
    var = jnp.mean(hc * hc, axis=-1, keepdims=True)
    return hc * lax.rsqrt(var + LN_EPS) * g + b


def _silu(x):
    return x * jax.nn.sigmoid(x)


def _gelu_tanh(x):
    c = 0.7978845608028654
    return 0.5 * x * (1.0 + jnp.tanh(c * (x + 0.044715 * (x * x * x))))


def _pack_halves(v):
    half = v.shape[1] // 2
    lo = lax.bitcast_convert_type(v[:, :half].astype(_BF16).astype(_F32), _U32)
    hi = lax.bitcast_convert_type(v[:, half:].astype(_BF16).astype(_F32), _U32)
    return hi | (lo >> 16)


def _unpack_halves(w):
    lo = lax.bitcast_convert_type(w << 16, _F32)
    hi = lax.bitcast_convert_type(w & _U32(0xFFFF0000), _F32)
    return lo, hi


def _store_pieces(ref, r0, w):
    for c in range(PIECES):
        ref[c, r0:r0 + w.shape[0], :] = w[:, c * LANES:(c + 1) * LANES]


def _pool_kernel(x_ref, win_ref, wgrp_ref, scale_ref, wout_ref, g_ref, b_ref,
                 o_ref, op_ref, zs_ref, y_ref):
    s = pl.program_id(1)
    ts = x_ref.shape[1]

    @pl.when(s == 0)
    def _():
        zs_ref[0:POOL_HALO, :] = jnp.zeros((POOL_HALO, D_MODEL), _F32)

    for p0 in range(0, ts, POOL_SUB):
        z = _dot(x_ref[0, p0:p0 + POOL_SUB, :].astype(_BF16), win_ref[...])
        zs_ref[POOL_HALO + p0:POOL_HALO + p0 + POOL_SUB, :] = z
    for p0 in range(0, ts, POOL_SUB):
        x = x_ref[0, p0:p0 + POOL_SUB, :]
        base = POOL_HALO + p0
        pos = s * ts + p0 + lax.broadcasted_iota(jnp.int32, (POOL_SUB, 1), 0)
        for g, w in enumerate(POOL_WINDOWS):
            c0 = g * POOL_GROUP_DIM
            c1 = c0 + POOL_GROUP_DIM
            zg = zs_ref[base:base + POOL_SUB, c0:c1]
            acc = zg
            for k in range(1, w):
                acc = acc + zs_ref[base - k:base - k + POOL_SUB, c0:c1]
            cnt = jnp.minimum(pos + 1, w).astype(_F32)
            pooled = acc / cnt - zg
            yg = _dot(pooled.astype(_BF16), wgrp_ref[g]) * scale_ref[:, c0:c1]
            y_ref[p0:p0 + POOL_SUB, c0:c1] = yg.astype(_BF16)
        mix = _dot(y_ref[p0:p0 + POOL_SUB, :], wout_ref[...])
        out = _layer_norm(DEEPNORM_ALPHA * x + mix, g_ref[...], b_ref[...])
        o_ref[0, p0:p0 + POOL_SUB, :] = out
        _store_pieces(op_ref, p0, _pack_halves(out))
    zs_ref[0:POOL_HALO, :] = zs_ref[ts:ts + POOL_HALO, :]


def _pool_layer(x, part, w_in, w_grp, scale, w_out, ln_g, ln_b):
    bsz, seq, d = x.shape
    rows = bsz // TOKEN_PARTS
    ts = POOL_TILE
    steps = seq // ts
    const2 = lambda b, s: (0, 0)
    out_shape = (jax.ShapeDtypeStruct((rows, seq, d), _F32),
                 jax.ShapeDtypeStruct((PIECES, rows * seq, LANES), _U32))
    tile = pl.BlockSpec((1, ts, d), lambda b, s: (b, s, 0))
    ptile = pl.BlockSpec((PIECES, ts, LANES), lambda b, s: (0, b * steps + s, 0))
    return pl.pallas_call(
        _pool_kernel,
        grid=(rows, steps),
        in_specs=[
            pl.BlockSpec((1, ts, d), lambda b, s: (part * rows + b, s, 0)),
            pl.BlockSpec((d, d), const2),
            pl.BlockSpec((len(POOL_WINDOWS), POOL_GROUP_DIM, POOL_GROUP_DIM), lambda b, s: (0, 0, 0)),
            pl.BlockSpec((1, d), const2),
            pl.BlockSpec((d, d), const2),
            pl.BlockSpec((1, d), const2),
            pl.BlockSpec((1, d), const2),
        ],
        out_specs=(tile, ptile),
        out_shape=out_shape,
        scratch_shapes=[pltpu.VMEM((POOL_HALO + ts, d), _F32),
                        pltpu.VMEM((ts, d), _BF16)],
        compiler_params=pltpu.CompilerParams(
            dimension_semantics=("arbitrary", "arbitrary"),
            vmem_limit_bytes=VMEM_LIMIT),
        name="pool_mixer",
    )(x, w_in.astype(_BF16), w_grp.astype(_BF16), scale.reshape(1, d),
      w_out.astype(_BF16), ln_g.reshape(1, d), ln_b.reshape(1, d))


def _sgu_kernel(x_ref, win_ref, bin_ref, lng_ref, lnb_ref, ws_ref, bs_ref, wout_ref,
                g_ref, b_ref, o_ref, op_ref, gated_ref):
    ts = x_ref.shape[0]
    projected = []
    for p0 in range(0, ts, SGU_SUB):
        x = x_ref[p0:p0 + SGU_SUB, :]
        xb = x.astype(_BF16)
        v = _gelu_tanh(_dot(xb, win_ref[:, SGU_WIDTH:]) + bin_ref[:, SGU_WIDTH:])
        v = _layer_norm(v, lng_ref[...], lnb_ref[...]).astype(_BF16)
        u = _gelu_tanh(_dot(xb, win_ref[:, :SGU_WIDTH]) + bin_ref[:, :SGU_WIDTH])
        projected.append((p0, x, u, v))
    for p0, x, u, v in projected:
        for r0 in range(0, SGU_SUB, SGU_CHUNK):
            for h in range(SGU_HEADS):
                c0 = h * SGU_HEAD_DIM
                sv = _dot(ws_ref[h], v[r0:r0 + SGU_CHUNK, c0:c0 + SGU_HEAD_DIM]) + bs_ref[h]
                gated_ref[p0 + r0:p0 + r0 + SGU_CHUNK, c0:c0 + SGU_HEAD_DIM] = (
                    u[r0:r0 + SGU_CHUNK, c0:c0 + SGU_HEAD_DIM] * sv).astype(_BF16)
        mix = _dot(gated_ref[p0:p0 + SGU_SUB, :], wout_ref[...])
        out = _layer_norm(DEEPNORM_ALPHA * x + mix, g_ref[...], b_ref[...])
        o_ref[p0:p0 + SGU_SUB, :] = out
        _store_pieces(op_ref, p0, _pack_halves(out))


def _sgu_layer(x, w_in, b_in, ln_g, ln_b, w_s, b_s, w_out, mix_g, mix_b):
    n_tok, d = x.shape
    ts = SGU_TILE
    const2 = lambda i: (0, 0)
    const3 = lambda i: (0, 0, 0)
    causal = jnp.tril(jnp.ones((SGU_CHUNK, SGU_CHUNK), w_s.dtype))
    ws = (w_s * causal[None]).astype(_BF16)
    tile = pl.BlockSpec((ts, d), lambda i: (i, 0))
    ptile = pl.BlockSpec((PIECES, ts, LANES), lambda i: (0, i, 0))
    out_shape = (jax.ShapeDtypeStruct((n_tok, d), _F32),
                 jax.ShapeDtypeStruct((PIECES, n_tok, LANES), _U32))
    return pl.pallas_call(
        _sgu_kernel,
        grid=(n_tok // ts,),
        in_specs=[
            tile,
            pl.BlockSpec((d, 2 * SGU_WIDTH), const2),
            pl.BlockSpec((1, 2 * SGU_WIDTH), const2),
            pl.BlockSpec((1, SGU_WIDTH), const2),
            pl.BlockSpec((1, SGU_WIDTH), const2),
            pl.BlockSpec((SGU_HEADS, SGU_CHUNK, SGU_CHUNK), const3),
            pl.BlockSpec((SGU_HEADS, SGU_CHUNK, 1), const3),
            pl.BlockSpec((SGU_WIDTH, d), const2),
            pl.BlockSpec((1, d), const2),
            pl.BlockSpec((1, d), const2),
        ],
        out_specs=(tile, ptile),
        out_shape=out_shape,
        scratch_shapes=[pltpu.VMEM((ts, SGU_WIDTH), _BF16)],
        compiler_params=pltpu.CompilerParams(
            dimension_semantics=("arbitrary",),
            vmem_limit_bytes=VMEM_LIMIT),
        name="sgu_mixer",
    )(x, w_in.astype(_BF16), b_in.reshape(1, -1), ln_g.reshape(1, -1), ln_b.reshape(1, -1),
      ws, b_s.reshape(SGU_HEADS, SGU_CHUNK, 1), w_out.astype(_BF16),
      mix_g.reshape(1, d), mix_b.reshape(1, d))


def _all_max(a):
    return jnp.max(jnp.max(a, axis=0, keepdims=True), axis=1, keepdims=True)


def _all_min(a):
    return jnp.min(jnp.min(a, axis=0, keepdims=True), axis=1, keepdims=True)


def _all_sum(a):
    return jnp.sum(jnp.sum(a, axis=0, keepdims=True), axis=1, keepdims=True)


def _route_kernel(x_ref, whi_ref, wlo_ref, bias_ref, earlier_ref,
                  ek_ref, rk_ref, gk_ref, cnt_ref, carry_ref):
    @pl.when(pl.program_id(0) == 0)
    def _():
        carry_ref[...] = jnp.zeros_like(carry_ref)

    nt = lambda a, b: lax.dot_general(a, b, (((1,), (1,)), ((), ())),
                                      preferred_element_type=_F32)
    logits = []
    for p0 in range(0, x_ref.shape[0], ROUTE_SUB):
        x = x_ref[p0:p0 + ROUTE_SUB, :]
        x_hi = x.astype(_BF16)
        x_lo = (x - x_hi.astype(_F32)).astype(_BF16)
        logits.append(nt(whi_ref[...], x_hi) + (nt(whi_ref[...], x_lo) + nt(wlo_ref[...], x_hi)))
    for n, sub_logits in enumerate(logits):
        _route_sub_tile(sub_logits, n * ROUTE_SUB, bias_ref, earlier_ref,
                        ek_ref, rk_ref, gk_ref, carry_ref)
    cnt_ref[...] = carry_ref[...]


def _route_sub_tile(logits, p0, bias_ref, earlier_ref, ek_ref, rk_ref, gk_ref, carry_ref):
    ts = ROUTE_SUB
    scores = jax.nn.sigmoid(logits).reshape(N_GROUPS, GROUP_SIZE, ts)
    biased = scores + bias_ref[...]
    neg_inf = jnp.float32(-jnp.inf)
    shape3 = (N_GROUPS, GROUP_SIZE, ts)
    in_grp = lax.broadcasted_iota(jnp.int32, shape3, 1)
    grp = lax.broadcasted_iota(jnp.int32, shape3, 0)
    eid = grp * GROUP_SIZE + in_grp

    m1 = jnp.max(biased, axis=1, keepdims=True)
    first1 = jnp.min(jnp.where(biased == m1, in_grp, GROUP_SIZE), axis=1, keepdims=True)
    m2 = jnp.max(jnp.where(in_grp == first1, neg_inf, biased), axis=1, keepdims=True)
    gscore = m1 + m2

    gid = lax.broadcasted_iota(jnp.int32, (N_GROUPS, 1, ts), 0)
    gsel = jnp.zeros((N_GROUPS, 1, ts), jnp.bool_)
    for _ in range(TOPK_GROUPS):
        m = jnp.max(gscore, axis=0, keepdims=True)
        first = jnp.min(jnp.where(gscore == m, gid, N_GROUPS), axis=0, keepdims=True)
        pick = gid == first
        gsel = jnp.logical_or(gsel, pick)
        gscore = jnp.where(pick, neg_inf, gscore)

    masked = jnp.where(gsel, biased, neg_inf)
    picked_any = jnp.zeros(shape3, jnp.bool_)
    e_k, s_k = [], []
    for _ in range(TOP_K):
        m = _all_max(masked)
        first = _all_min(jnp.where(masked == m, eid, N_EXPERTS))
        pick = eid == first
        picked_any = jnp.logical_or(picked_any, pick)
        masked = jnp.where(pick, neg_inf, masked)
        e_k.append(first)
        s_k.append(_all_sum(jnp.where(pick, scores, 0.0)))
    denom = s_k[0]
    for k in range(1, TOP_K):
        denom = denom + s_k[k]

    sel = picked_any.astype(_F32).reshape(N_EXPERTS, ts)
    rank = _dot(sel.astype(_BF16), earlier_ref[...]) + carry_ref[...]
    rank3 = rank.reshape(shape3)
    carry_ref[...] += jnp.sum(sel, axis=1, keepdims=True)

    for k in range(TOP_K):
        r = _all_sum(jnp.where(eid == e_k[k], rank3, 0.0))
        ek_ref[k:k + 1, p0:p0 + ts] = e_k[k].reshape(1, ts)
        rk_ref[k:k + 1, p0:p0 + ts] = r.reshape(1, ts).astype(jnp.int32)
        gk_ref[k:k + 1, p0:p0 + ts] = (s_k[k] / denom * ROUTED_SCALE).reshape(1, ts)


def _route(x, w_router, router_bias):
    n_part, d = x.shape
    ts = ROUTE_TILE
    steps = n_part // ts
    out_shape = (jax.ShapeDtypeStruct((TOP_K, n_part), jnp.int32),
                 jax.ShapeDtypeStruct((TOP_K, n_part), jnp.int32),
                 jax.ShapeDtypeStruct((TOP_K, n_part), _F32),
                 jax.ShapeDtypeStruct((N_EXPERTS, 1), _F32))
    kspec = pl.BlockSpec((TOP_K, ts), lambda i: (0, i))
    const2 = lambda i: (0, 0)
    w_t = w_router.T
    w_hi = w_t.astype(_BF16)
    w_lo = (w_t - w_hi.astype(_F32)).astype(_BF16)
    pos = jnp.arange(ROUTE_SUB, dtype=jnp.int32)
    earlier = (pos[:, None] < pos[None, :]).astype(_BF16)
    return pl.pallas_call(
        _route_kernel,
        grid=(steps,),
        in_specs=[
            pl.BlockSpec((ts, d), lambda i: (i, 0)),
            pl.BlockSpec((N_EXPERTS, d), const2),
            pl.BlockSpec((N_EXPERTS, d), const2),
            pl.BlockSpec((N_GROUPS, GROUP_SIZE, 1), lambda i: (0, 0, 0)),
            pl.BlockSpec((ROUTE_SUB, ROUTE_SUB), const2),
        ],
        out_specs=(kspec, kspec, kspec, pl.BlockSpec((N_EXPERTS, 1), const2)),
        out_shape=out_shape,
        scratch_shapes=[pltpu.VMEM((N_EXPERTS, 1), _F32)],
        compiler_params=pltpu.CompilerParams(
            dimension_semantics=("arbitrary",),
            vmem_limit_bytes=VMEM_LIMIT),
        name="moe_route",
    )(x, w_hi, w_lo, router_bias.reshape(N_GROUPS, GROUP_SIZE, 1), earlier)


def _sc_mesh():
    return plsc.VectorSubcoreMesh(core_axis_name="core", subcore_axis_name="subcore")


def _sc_scatter_rows(src, idx, n_out):
    d = src.shape[1]
    wins = idx.shape[1] // SC_WINDOW

    def body(src_hbm, idx_hbm, out_hbm):
        def step(src_vmem, idx_vmem):
            for k in range(TOP_K):
                pltpu.sync_copy(src_vmem, out_hbm.at[idx_vmem.at[k]])

        pltpu.emit_pipeline(
            step,
            grid=(src.shape[0] // SC_WINDOW,),
            in_specs=[pl.BlockSpec((SC_WINDOW, d), index_map=lambda i: (i, 0)),
                      pl.BlockSpec((TOP_K, SC_WINDOW), index_map=lambda i: (i // wins, i % wins))],
            out_specs=[],
            core_axis_name=("core", "subcore"),
            dimension_semantics=(pltpu.PARALLEL,),
        )(src_hbm, idx_hbm)

    return pl.kernel(body, out_type=jax.ShapeDtypeStruct((n_out, d), src.dtype),
                     mesh=_sc_mesh(), scratch_types=[], name="sc_scatter_rows")(src, idx)


def _sc_gather_sum(table, idx, gates):
    d = table.shape[1]
    n_tok = idx.shape[1]
    wins = n_tok // SC_WINDOW
    lanes = SC_LANES
    out_type = jax.ShapeDtypeStruct((PIECES * n_tok, d), _F32)

    def tree_sum(terms):
        while len(terms) > 1:
            terms = [terms[i] + terms[i + 1] for i in range(0, len(terms), 2)]
        return terms[0]

    def body(table_hbm, idx_hbm, gate_hbm, lo_hbm, hi_hbm, rows_vmem, sem):
        def step(idx_vmem, gate_vmem, lo_vmem, hi_vmem):
            def burst_copies(b):
                return [pltpu.make_async_copy(
                    table_hbm.at[idx_vmem.at[k, pl.ds(b * SC_BURST, SC_BURST)]],
                    rows_vmem.at[b % 2, k], sem.at[b % 2]) for k in range(TOP_K)]

            n_bursts = SC_WINDOW // SC_BURST
            for copy in burst_copies(0):
                copy.start()
            for b in range(n_bursts):
                if b + 1 < n_bursts:
                    for copy in burst_copies(b + 1):
                        copy.start()
                for copy in burst_copies(b):
                    copy.wait()
                t0 = b * SC_BURST
                rows = rows_vmem.at[b % 2]

                @plsc.parallel_loop(0, SC_BURST)
                def _(r):
                    t = t0 + r
                    token = jnp.full((lanes,), t, jnp.int32)
                    gate = [plsc.load_gather(gate_vmem, [jnp.full((lanes,), k, jnp.int32), token])
                            for k in range(TOP_K)]
                    for j in range(0, d, lanes):
                        words = [rows[k, r, pl.ds(j, lanes)] for k in range(TOP_K)]
                        lo_vmem[t, pl.ds(j, lanes)] = tree_sum(
                            [gate[k] * lax.bitcast_convert_type(words[k] << 16, _F32)
                             for k in range(TOP_K)])
                        hi_vmem[t, pl.ds(j, lanes)] = tree_sum(
                            [gate[k] * lax.bitcast_convert_type(words[k] & _U32(0xFFFF0000), _F32)
                             for k in range(TOP_K)])

        window = lambda i: (i // wins, i % wins)
        pltpu.emit_pipeline(
            step,
            grid=(PIECES * wins,),
            in_specs=[pl.BlockSpec((TOP_K, SC_WINDOW), index_map=window),
                      pl.BlockSpec((TOP_K, SC_WINDOW), index_map=lambda i: (0, i % wins))],
            out_specs=[pl.BlockSpec((SC_WINDOW, d), index_map=lambda i: (i, 0)),
                       pl.BlockSpec((SC_WINDOW, d), index_map=lambda i: (i, 0))],
            core_axis_name=("core", "subcore"),
            dimension_semantics=(pltpu.PARALLEL,),
        )(idx_hbm, gate_hbm, lo_hbm, hi_hbm)

    return pl.kernel(body, out_type=(out_type, out_type), mesh=_sc_mesh(),
                     scratch_types=[pltpu.VMEM((2, TOP_K, SC_BURST, d), _U32),
                                    pltpu.SemaphoreType.DMA((2,))],
                     compiler_params=pltpu.CompilerParams(needs_layout_passes=False),
                     name="sc_gather_sum")(table, idx, gates)


def _index_kernel(start_ref, ek_ref, rk_ref, idx_ref, *, n_rows):
    e = ek_ref[...]
    start = jnp.zeros(e.shape, jnp.int32)
    for ex in range(N_EXPERTS):
        start = jnp.where(e == ex, start_ref[ex], start)
    dest = start + rk_ref[...]
    for c in range(PIECES):
        idx_ref[c] = dest + c * n_rows


def _row_indices(padded_start, e_k, r_k, n_rows):
    n_tok = e_k.shape[1]
    ts = INDEX_TILE
    kspec = pl.BlockSpec((TOP_K, ts), lambda i, st: (0, i))
    grid_spec = pltpu.PrefetchScalarGridSpec(
        num_scalar_prefetch=1,
        grid=(n_tok // ts,),
        in_specs=[kspec, kspec],
        out_specs=pl.BlockSpec((PIECES, TOP_K, ts), lambda i, st: (0, 0, i)),
    )
    return pl.pallas_call(
        functools.partial(_index_kernel, n_rows=n_rows),
        grid_spec=grid_spec,
        out_shape=jax.ShapeDtypeStruct((PIECES, TOP_K, n_tok), jnp.int32),
        compiler_params=pltpu.CompilerParams(dimension_semantics=("arbitrary",)),
        name="moe_row_indices",
    )(padded_start, e_k, r_k)


def _expert_kernel(chunk0_ref, nchunk_ref, cnt_ref, total_ref,
                   xs_hbm, wg_ref, wu_ref, wd_ref, o_hbm,
                   xbuf, obuf, sem_in, sem_out):
    e = pl.program_id(0)
    total = total_ref[0]
    ch = EXPERT_ROWS
    ahead = IN_SLOTS - 1

    def in_copy(g):
        slot = g % IN_SLOTS
        return pltpu.make_async_copy(xs_hbm.at[:, pl.ds(g * ch, ch), :], xbuf.at[slot],
                                     sem_in.at[slot])

    def out_copy(g):
        slot = g % OUT_SLOTS
        return pltpu.make_async_copy(obuf.at[slot], o_hbm.at[:, pl.ds(g * ch, ch), :],
                                     sem_out.at[slot])

    @pl.when(e == 0)
    def _():
        obuf[...] = jnp.zeros_like(obuf)
        for g in range(ahead):
            @pl.when(g < total)
            def _():
                in_copy(g).start()


    def chunk(j, carry):
        g = chunk0_ref[e] + j
        in_copy(g).wait()

        @pl.when(g + ahead < total)
        def _():
            in_copy(g + ahead).start()

        @pl.when(g >= OUT_SLOTS)
        def _():
            out_copy(g - OUT_SLOTS).wait()

        islot = g % IN_SLOTS
        oslot = g % OUT_SLOTS
        valid = cnt_ref[e] - j * ch

        def gate_up(s):
            r0 = s * EXPERT_SUB
            w = jnp.concatenate([xbuf[islot, c, r0:r0 + EXPERT_SUB, :] for c in range(PIECES)],
                                axis=1)
            lo, hi = _unpack_halves(w)
            gate = _dot(lo, wg_ref[0, 0, :PACKED, :]) + _dot(hi, wg_ref[0, 0, PACKED:, :])
            up = _dot(lo, wu_ref[0, 0, :PACKED, :]) + _dot(hi, wu_ref[0, 0, PACKED:, :])
            return gate, up

        def down(s, gate_up_pair):
            r0 = s * EXPERT_SUB
            gate, up = gate_up_pair
            y = _pack_halves(_dot(_silu(gate) * up, wd_ref[0, 0]))
            for c in range(PIECES):
                obuf[oslot, c, r0:r0 + EXPERT_SUB, :] = y[:, c * LANES:(c + 1) * LANES]

        n_sub = ch // EXPERT_SUB

        all_subs = valid > ch - EXPERT_SUB

        @pl.when(all_subs)
        def _():
            projected = [gate_up(s) for s in range(n_sub)]
            for s, gu in enumerate(projected):
                down(s, gu)

        @pl.when(jnp.logical_not(all_subs))
        def _():
            for s in range(n_sub - 1):
                @pl.when(s * EXPERT_SUB < valid)
                def _():
                    down(s, gate_up(s))

        out_copy(g).start()
        return carry

    lax.fori_loop(0, nchunk_ref[e], chunk, 0)

    @pl.when(e == pl.num_programs(0) - 1)
    def _():
        for back in range(OUT_SLOTS, 0, -1):
            @pl.when(total >= back)
            def _():
                out_copy(total - back).wait()


def _experts(layer, chunk0, nchunk, counts, total, xs, w_gate, w_up, w_down):
    d = D_MODEL
    ch = EXPERT_ROWS
    wspec_in = pl.BlockSpec((1, 1, d, EXPERT_DIM), lambda e, *_: (layer, e, 0, 0))
    any_spec = pl.BlockSpec(memory_space=pl.ANY)
    grid_spec = pltpu.PrefetchScalarGridSpec(
        num_scalar_prefetch=4,
        grid=(N_EXPERTS,),
        in_specs=[
            any_spec,
            wspec_in,
            wspec_in,
            pl.BlockSpec((1, 1, EXPERT_DIM, d), lambda e, *_: (layer, e, 0, 0)),
        ],
        out_specs=any_spec,
        scratch_shapes=[pltpu.VMEM((IN_SLOTS, PIECES, ch, LANES), _U32),
                        pltpu.VMEM((OUT_SLOTS, PIECES, ch, LANES), _U32),
                        pltpu.SemaphoreType.DMA((IN_SLOTS,)),
                        pltpu.SemaphoreType.DMA((OUT_SLOTS,))],
    )
    return pl.pallas_call(
        _expert_kernel,
        grid_spec=grid_spec,
        out_shape=jax.ShapeDtypeStruct(xs.shape, _U32),
        compiler_params=pltpu.CompilerParams(
            dimension_semantics=("arbitrary",),
            vmem_limit_bytes=VMEM_LIMIT),
        name="moe_experts",
    )(chunk0, nchunk, counts, total, xs, w_gate, w_up, w_down)


def _combine_kernel(x_ref, lo_ref, hi_ref, sg_ref, su_ref, sd_ref, g_ref, b_ref, *rest):
    o_ref = rest[-1]
    x = x_ref[...]
    xb = x.astype(_BF16)
    routed = jnp.concatenate([lo_ref[c] for c in range(PIECES)] +
                             [hi_ref[c] for c in range(PIECES)], axis=1)
    h = _silu(_dot(xb, sg_ref[...])) * _dot(xb, su_ref[...])
    shared = _dot(h.astype(_BF16), sd_ref[...])
    o_ref[...] = _layer_norm(DEEPNORM_ALPHA * x + (routed + shared), g_ref[...], b_ref[...])


def _combine(x, routed_lo, routed_hi, sh_gate, sh_up, sh_down, ln_g, ln_b, out_parts, part, prev):
    n_part, d = x.shape
    ts = COMBINE_TILE
    steps = n_part // ts
    const2 = lambda i: (0, 0)
    in_specs = [
        pl.BlockSpec((ts, d), lambda i: (i, 0)),
        pl.BlockSpec((PIECES, ts, LANES), lambda i: (0, i, 0)),
        pl.BlockSpec((PIECES, ts, LANES), lambda i: (0, i, 0)),
        pl.BlockSpec((d, EXPERT_DIM), const2),
        pl.BlockSpec((d, EXPERT_DIM), const2),
        pl.BlockSpec((EXPERT_DIM, d), const2),
        pl.BlockSpec((1, d), const2),
        pl.BlockSpec((1, d), const2),
    ]
    args = [x, routed_lo, routed_hi, sh_gate.astype(_BF16), sh_up.astype(_BF16), sh_down.astype(_BF16),
            ln_g.reshape(1, d), ln_b.reshape(1, d)]
    aliases = {}
    if prev is not None:
        in_specs.append(pl.BlockSpec(memory_space=pl.ANY))
        aliases = {len(args): 0}
        args.append(prev)
    return pl.pallas_call(
        _combine_kernel,
        grid=(steps,),
        in_specs=in_specs,
        out_specs=pl.BlockSpec((ts, d), lambda i: (part * steps + i, 0)),
        out_shape=jax.ShapeDtypeStruct((out_parts * n_part, d), _F32),
        input_output_aliases=aliases,
        compiler_params=pltpu.CompilerParams(
            dimension_semantics=("arbitrary",),
            vmem_limit_bytes=VMEM_LIMIT),
        name="moe_combine",
    )(*args)


def _moe_dispatch(layer, x, x_packed, w_router, router_bias, w_gate, w_up, w_down):
    n_part, d = x.shape
    bm = EXPERT_ROWS
    n_rows = n_part * TOP_K + N_EXPERTS * bm
    e_k, r_k, g_k, counts = _route(x, w_router, router_bias)
    counts = counts.reshape(N_EXPERTS).astype(jnp.int32)
    padded = (counts + bm - 1) // bm * bm
    padded_end = jnp.cumsum(padded)
    padded_start = padded_end - padded
    idx = _row_indices(padded_start, e_k, r_k, n_rows).reshape(PIECES * TOP_K, n_part)
    xs = _sc_scatter_rows(x_packed.reshape(PIECES * n_part, LANES), idx, PIECES * n_rows)
    rows = _experts(layer, padded_start // bm, padded // bm, counts, padded_end[-1:] // bm,
                    xs.reshape(PIECES, n_rows, LANES), w_gate, w_up, w_down)
    return rows.reshape(PIECES * n_rows, LANES), idx, g_k


def _moe_combine(x, rows, idx, gates, sh_gate, sh_up, sh_down, ln_g, ln_b,
                 out_parts=1, part=0, prev=None):
    n_part = x.shape[0]
    lo, hi = _sc_gather_sum(rows, idx, gates)
    return _combine(x, lo.reshape(PIECES, n_part, LANES), hi.reshape(PIECES, n_part, LANES),
                    sh_gate, sh_up, sh_down, ln_g, ln_b, out_parts, part, prev)


def kernel(x, pool_w_in, pool_w_grp, pool_scale, pool_w_out, sgu_w_in, sgu_b_in, sgu_ln_g, sgu_ln_b, sgu_w_s, sgu_b_s, sgu_w_out, ln_mix_g, ln_mix_b, moe_w_router, moe_router_bias, moe_w_gate, moe_w_up, moe_w_down, moe_sh_gate, moe_sh_up, moe_sh_down, ln_ffn_g, ln_ffn_b):
    bsz, seq, d = x.shape
    parts = range(TOKEN_PARTS)

    def dispatch(i, h, h_packed):
        return _moe_dispatch(i, h, h_packed, moe_w_router[i], moe_router_bias[i],
                             moe_w_gate, moe_w_up, moe_w_down)

    def combine(i, h, routed, **where):
        return _moe_combine(h, *routed, moe_sh_gate[i], moe_sh_up[i], moe_sh_down[i],
                            ln_ffn_g[i], ln_ffn_b[i], **where)

    mixed = [_pool_layer(x, p, pool_w_in[0], pool_w_grp[0], pool_scale[0], pool_w_out[0],
                         ln_mix_g[0], ln_mix_b[0]) for p in parts]
    mixed = [(h.reshape(-1, d), h_packed) for h, h_packed in mixed]
    routed = [dispatch(0, h, h_packed) for h, h_packed in mixed]
    hs = [combine(0, mixed[p][0], routed[p]) for p in parts]
    mixed = [_sgu_layer(h, sgu_w_in[0], sgu_b_in[0], sgu_ln_g[0], sgu_ln_b[0], sgu_w_s[0],
                        sgu_b_s[0], sgu_w_out[0], ln_mix_g[1], ln_mix_b[1]) for h in hs]
    routed = [dispatch(1, h, h_packed) for h, h_packed in mixed]
    out = None
    for p in parts:
        out = combine(1, mixed[p][0], routed[p], out_parts=TOKEN_PARTS, part=p, prev=out)
    return out.reshape(bsz, seq, d)
```

```python
import functools

import jax
import jax.numpy as jnp
from jax import lax
from jax.experimental import pallas as pl
from jax.experimental.pallas import tpu as pltpu
from jax.experimental.pallas import tpu_sc as plsc

D_MODEL = 1024
DEPTH = 2
POOL_WINDOWS = (2, 4, 8, 16)
POOL_GROUP_DIM = D_MODEL // len(POOL_WINDOWS)
POOL_HALO = 16
SGU_CHUNK = 128
SGU_HEADS = 4
SGU_WIDTH = 2 * D_MODEL
SGU_HEAD_DIM = SGU_WIDTH // SGU_HEADS
N_EXPERTS = 64
TOP_K = 8
N_GROUPS = 8
GROUP_SIZE = N_EXPERTS // N_GROUPS
TOPK_GROUPS = 4
EXPERT_DIM = D_MODEL // 4
ROUTED_SCALE = 2.5
DEEPNORM_ALPHA = (2 * DEPTH) ** 0.25
LN_EPS = 1e-5

LANES = 128
PACKED = D_MODEL // 2
PIECES = PACKED // LANES

POOL_TILE = 1024
POOL_SUB = 256
SGU_TILE = 512
SGU_SUB = 256
ROUTE_TILE = 1024
ROUTE_SUB = 512
INDEX_TILE = 2048
EXPERT_ROWS = 512
EXPERT_SUB = 256
IN_SLOTS = 6
OUT_SLOTS = 4
COMBINE_TILE = 1024
SC_WINDOW = 128
SC_LANES = 16
SC_BURST = 16
TOKEN_PARTS = 2
V7X_VMEM_BYTES = 64 * 1024 * 1024
VMEM_LIMIT = V7X_VMEM_BYTES * 7 // 8

_F32 = jnp.float32
_BF16 = jnp.bfloat16
_U32 = jnp.uint32


def _dot(a, b):
    return jnp.dot(a, b, preferred_element_type=_F32)


def _layer_norm(h, g, b):
    mu = jnp.mean(h, axis=-1, keepdims=True)
    hc = h - mu
    var = jnp.mean(hc * hc, axis=-1, keepdims=True)
    return hc * lax.rsqrt(var + LN_EPS) * g + b


def _silu(x):
    return x * jax.nn.sigmoid(x)


def _gelu_tanh(x):
    c = 0.7978845608028654
    return 0.5 * x * (1.0 + jnp.tanh(c * (x + 0.044715 * (x * x * x))))


def _pack_halves(v):
    half = v.shape[1] // 2
    lo = lax.bitcast_convert_type(v[:, :half].astype(_BF16).astype(_F32), _U32)
    hi = lax.bitcast_convert_type(v[:, half:].astype(_BF16).astype(_F32), _U32)
    return hi | (lo >> 16)


def _unpack_halves(w):
    lo = lax.bitcast_convert_type(w << 16, _F32)
    hi = lax.bitcast_convert_type(w & _U32(0xFFFF0000), _F32)
    return lo, hi


def _store_pieces(ref, r0, w):
    for c in range(PIECES):
        ref[c, r0:r0 + w.shape[0], :] = w[:, c * LANES:(c + 1) * LANES]


def _pool_kernel(x_ref, win_ref, wgrp_ref, scale_ref, wout_ref, g_ref, b_ref,
                 o_ref, op_ref, zs_ref, y_ref):
    s = pl.program_id(1)
    ts = x_ref.shape[1]

    @pl.when(s == 0)
    def _():
        zs_ref[0:POOL_HALO, :] = jnp.zeros((POOL_HALO, D_MODEL), _F32)

    for p0 in range(0, ts, POOL_SUB):
        z = _dot(x_ref[0, p0:p0 + POOL_SUB, :].astype(_BF16), win_ref[...])
        zs_ref[POOL_HALO + p0:POOL_HALO + p0 + POOL_SUB, :] = z
    for p0 in range(0, ts, POOL_SUB):
        x = x_ref[0, p0:p0 + POOL_SUB, :]
        base = POOL_HALO + p0
        pos = s * ts + p0 + lax.broadcasted_iota(jnp.int32, (POOL_SUB, 1), 0)
        for g, w in enumerate(POOL_WINDOWS):
            c0 = g * POOL_GROUP_DIM
            c1 = c0 + POOL_GROUP_DIM
            zg = zs_ref[base:base + POOL_SUB, c0:c1]
            acc = zg
            for k in range(1, w):
                acc = acc + zs_ref[base - k:base - k + POOL_SUB, c0:c1]
            cnt = jnp.minimum(pos + 1, w).astype(_F32)
            pooled = acc / cnt - zg
            yg = _dot(pooled.astype(_BF16), wgrp_ref[g]) * scale_ref[:, c0:c1]
            y_ref[p0:p0 + POOL_SUB, c0:c1] = yg.astype(_BF16)
        mix = _dot(y_ref[p0:p0 + POOL_SUB, :], wout_ref[...])
        out = _layer_norm(DEEPNORM_ALPHA * x + mix, g_ref[...], b_ref[...])
        o_ref[0, p0:p0 + POOL_SUB, :] = out
        _store_pieces(op_ref, p0, _pack_halves(out))
    zs_ref[0:POOL_HALO, :] = zs_ref[ts:ts + POOL_HALO, :]


def _pool_layer(x, part, w_in, w_grp, scale, w_out, ln_g, ln_b):
    bsz, seq, d = x.shape
    rows = bsz // TOKEN_PARTS
    ts = POOL_TILE
    steps = seq // ts
    const2 = lambda b, s: (0, 0)
    out_shape = (jax.ShapeDtypeStruct((rows, seq, d), _F32),
                 jax.ShapeDtypeStruct((PIECES, rows * seq, LANES), _U32))
    tile = pl.BlockSpec((1, ts, d), lambda b, s: (b, s, 0))
    ptile = pl.BlockSpec((PIECES, ts, LANES), lambda b, s: (0, b * steps + s, 0))
    return pl.pallas_call(
        _pool_kernel,
        grid=(rows, steps),
        in_specs=[
            pl.BlockSpec((1, ts, d), lambda b, s: (part * rows + b, s, 0)),
            pl.BlockSpec((d, d), const2),
            pl.BlockSpec((len(POOL_WINDOWS), POOL_GROUP_DIM, POOL_GROUP_DIM), lambda b, s: (0, 0, 0)),
            pl.BlockSpec((1, d), const2),
            pl.BlockSpec((d, d), const2),
            pl.BlockSpec((1, d), const2),
            pl.BlockSpec((1, d), const2),
        ],
        out_specs=(tile, ptile),
        out_shape=out_shape,
        scratch_shapes=[pltpu.VMEM((POOL_HALO + ts, d), _F32),
                        pltpu.VMEM((ts, d), _BF16)],
        compiler_params=pltpu.CompilerParams(
            dimension_semantics=("arbitrary", "arbitrary"),
            vmem_limit_bytes=VMEM_LIMIT),
        name="pool_mixer",
    )(x, w_in.astype(_BF16), w_grp.astype(_BF16), scale.reshape(1, d),
      w_out.astype(_BF16), ln_g.reshape(1, d), ln_b.reshape(1, d))


def _sgu_kernel(x_ref, win_ref, bin_ref, lng_ref, lnb_ref, ws_ref, bs_ref, wout_ref,
                g_ref, b_ref, o_ref, op_ref, gated_ref):
    ts = x_ref.shape[0]
    projected = []
    for p0 in range(0, ts, SGU_SUB):
        x = x_ref[p0:p0 + SGU_SUB, :]
        xb = x.astype(_BF16)
        v = _gelu_tanh(_dot(xb, win_ref[:, SGU_WIDTH:]) + bin_ref[:, SGU_WIDTH:])
        v = _layer_norm(v, lng_ref[...], lnb_ref[...]).astype(_BF16)
        u = _gelu_tanh(_dot(xb, win_ref[:, :SGU_WIDTH]) + bin_ref[:, :SGU_WIDTH])
        projected.append((p0, x, u, v))
    for p0, x, u, v in projected:
        for r0 in range(0, SGU_SUB, SGU_CHUNK):
            for h in range(SGU_HEADS):
                c0 = h * SGU_HEAD_DIM
                sv = _dot(ws_ref[h], v[r0:r0 + SGU_CHUNK, c0:c0 + SGU_HEAD_DIM]) + bs_ref[h]
                gated_ref[p0 + r0:p0 + r0 + SGU_CHUNK, c0:c0 + SGU_HEAD_DIM] = (
                    u[r0:r0 + SGU_CHUNK, c0:c0 + SGU_HEAD_DIM] * sv).astype(_BF16)
        mix = _dot(gated_ref[p0:p0 + SGU_SUB, :], wout_ref[...])
        out = _layer_norm(DEEPNORM_ALPHA * x + mix, g_ref[...], b_ref[...])
        o_ref[p0:p0 + SGU_SUB, :] = out
        _store_pieces(op_ref, p0, _pack_halves(out))


def _sgu_layer(x, w_in, b_in, ln_g, ln_b, w_s, b_s, w_out, mix_g, mix_b):
    n_tok, d = x.shape
    ts = SGU_TILE
    const2 = lambda i: (0, 0)
    const3 = lambda i: (0, 0, 0)
    causal = jnp.tril(jnp.ones((SGU_CHUNK, SGU_CHUNK), w_s.dtype))
    ws = (w_s * causal[None]).astype(_BF16)
    tile = pl.BlockSpec((ts, d), lambda i: (i, 0))
    ptile = pl.BlockSpec((PIECES, ts, LANES), lambda i: (0, i, 0))
    out_shape = (jax.ShapeDtypeStruct((n_tok, d), _F32),
                 jax.ShapeDtypeStruct((PIECES, n_tok, LANES), _U32))
    return pl.pallas_call(
        _sgu_kernel,
        grid=(n_tok // ts,),
        in_specs=[
            tile,
            pl.BlockSpec((d, 2 * SGU_WIDTH), const2),
            pl.BlockSpec((1, 2 * SGU_WIDTH), const2),
            pl.BlockSpec((1, SGU_WIDTH), const2),
            pl.BlockSpec((1, SGU_WIDTH), const2),
            pl.BlockSpec((SGU_HEADS, SGU_CHUNK, SGU_CHUNK), const3),
            pl.BlockSpec((SGU_HEADS, SGU_CHUNK, 1), const3),
            pl.BlockSpec((SGU_WIDTH, d), const2),
            pl.BlockSpec((1, d), const2),
            pl.BlockSpec((1, d), const2),
        ],
        out_specs=(tile, ptile),
        out_shape=out_shape,
        scratch_shapes=[pltpu.VMEM((ts, SGU_WIDTH), _BF16)],
        compiler_params=pltpu.CompilerParams(
            dimension_semantics=("arbitrary",),
            vmem_limit_bytes=VMEM_LIMIT),
        name="sgu_mixer",
    )(x, w_in.astype(_BF16), b_in.reshape(1, -1), ln_g.reshape(1, -1), ln_b.reshape(1, -1),
      ws, b_s.reshape(SGU_HEADS, SGU_CHUNK, 1), w_out.astype(_BF16),
      mix_g.reshape(1, d), mix_b.reshape(1, d))


def _all_max(a):
    return jnp.max(jnp.max(a, axis=0, keepdims=True), axis=1, keepdims=True)


def _all_min(a):
    return jnp.min(jnp.min(a, axis=0, keepdims=True), axis=1, keepdims=True)


def _all_sum(a):
    return jnp.sum(jnp.sum(a, axis=0, keepdims=True), axis=1, keepdims=True)


def _route_kernel(x_ref, whi_ref, wlo_ref, bias_ref, earlier_ref,
                  ek_ref, rk_ref, gk_ref, cnt_ref, carry_ref):
    @pl.when(pl.program_id(0) == 0)
    def _():
        carry_ref[...] = jnp.zeros_like(carry_ref)

    nt = lambda a, b: lax.dot_general(a, b, (((1,), (1,)), ((), ())),
                                      preferred_element_type=_F32)
    logits = []
    for p0 in range(0, x_ref.shape[0], ROUTE_SUB):
        x = x_ref[p0:p0 + ROUTE_SUB, :]
        x_hi = x.astype(_BF16)
        x_lo = (x - x_hi.astype(_F32)).astype(_BF16)
        logits.append(nt(whi_ref[...], x_hi) + (nt(whi_ref[...], x_lo) + nt(wlo_ref[...], x_hi)))
    for n, sub_logits in enumerate(logits):
        _route_sub_tile(sub_logits, n * ROUTE_SUB, bias_ref, earlier_ref,
                        ek_ref, rk_ref, gk_ref, carry_ref)
    cnt_ref[...] = carry_ref[...]


def _route_sub_tile(logits, p0, bias_ref, earlier_ref, ek_ref, rk_ref, gk_ref, carry_ref):
    ts = ROUTE_SUB
    scores = jax.nn.sigmoid(logits).reshape(N_GROUPS, GROUP_SIZE, ts)
    biased = scores + bias_ref[...]
    neg_inf = jnp.float32(-jnp.inf)
    shape3 = (N_GROUPS, GROUP_SIZE, ts)
    in_grp = lax.broadcasted_iota(jnp.int32, shape3, 1)
    grp = lax.broadcasted_iota(jnp.int32, shape3, 0)
    eid = grp * GROUP_SIZE + in_grp

    m1 = jnp.max(biased, axis=1, keepdims=True)
    first1 = jnp.min(jnp.where(biased == m1, in_grp, GROUP_SIZE), axis=1, keepdims=True)
    m2 = jnp.max(jnp.where(in_grp == first1, neg_inf, biased), axis=1, keepdims=True)
    gscore = m1 + m2

    gid = lax.broadcasted_iota(jnp.int32, (N_GROUPS, 1, ts), 0)
    gsel = jnp.zeros((N_GROUPS, 1, ts), jnp.bool_)
    for _ in range(TOPK_GROUPS):
        m = jnp.max(gscore, axis=0, keepdims=True)
        first = jnp.min(jnp.where(gscore == m, gid, N_GROUPS), axis=0, keepdims=True)
        pick = gid == first
        gsel = jnp.logical_or(gsel, pick)
        gscore = jnp.where(pick, neg_inf, gscore)

    masked = jnp.where(gsel, biased, neg_inf)
    picked_any = jnp.zeros(shape3, jnp.bool_)
    e_k, s_k = [], []
    for _ in range(TOP_K):
        m = _all_max(masked)
        first = _all_min(jnp.where(masked == m, eid, N_EXPERTS))
        pick = eid == first
        picked_any = jnp.logical_or(picked_any, pick)
        masked = jnp.where(pick, neg_inf, masked)
        e_k.append(first)
        s_k.append(_all_sum(jnp.where(pick, scores, 0.0)))
    denom = s_k[0]
    for k in range(1, TOP_K):
        denom = denom + s_k[k]

    sel = picked_any.astype(_F32).reshape(N_EXPERTS, ts)
    rank = _dot(sel.astype(_BF16), earlier_ref[...]) + carry_ref[...]
    rank3 = rank.reshape(shape3)
    carry_ref[...] += jnp.sum(sel, axis=1, keepdims=True)

    for k in range(TOP_K):
        r = _all_sum(jnp.where(eid == e_k[k], rank3, 0.0))
        ek_ref[k:k + 1, p0:p0 + ts] = e_k[k].reshape(1, ts)
        rk_ref[k:k + 1, p0:p0 + ts] = r.reshape(1, ts).astype(jnp.int32)
        gk_ref[k:k + 1, p0:p0 + ts] = (s_k[k] / denom * ROUTED_SCALE).reshape(1, ts)


def _route(x, w_router, router_bias):
    n_part, d = x.shape
    ts = ROUTE_TILE
    steps = n_part // ts
    out_shape = (jax.ShapeDtypeStruct((TOP_K, n_part), jnp.int32),
                 jax.ShapeDtypeStruct((TOP_K, n_part), jnp.int32),
                 jax.ShapeDtypeStruct((TOP_K, n_part), _F32),
                 jax.ShapeDtypeStruct((N_EXPERTS, 1), _F32))
    kspec = pl.BlockSpec((TOP_K, ts), lambda i: (0, i))
    const2 = lambda i: (0, 0)
    w_t = w_router.T
    w_hi = w_t.astype(_BF16)
    w_lo = (w_t - w_hi.astype(_F32)).astype(_BF16)
    pos = jnp.arange(ROUTE_SUB, dtype=jnp.int32)
    earlier = (pos[:, None] < pos[None, :]).astype(_BF16)
    return pl.pallas_call(
        _route_kernel,
        grid=(steps,),
        in_specs=[
            pl.BlockSpec((ts, d), lambda i: (i, 0)),
            pl.BlockSpec((N_EXPERTS, d), const2),
            pl.BlockSpec((N_EXPERTS, d), const2),
            pl.BlockSpec((N_GROUPS, GROUP_SIZE, 1), lambda i: (0, 0, 0)),
            pl.BlockSpec((ROUTE_SUB, ROUTE_SUB), const2),
        ],
        out_specs=(kspec, kspec, kspec, pl.BlockSpec((N_EXPERTS, 1), const2)),
        out_shape=out_shape,
        scratch_shapes=[pltpu.VMEM((N_EXPERTS, 1), _F32)],
        compiler_params=pltpu.CompilerParams(
            dimension_semantics=("arbitrary",),
            vmem_limit_bytes=VMEM_LIMIT),
        name="moe_route",
    )(x, w_hi, w_lo, router_bias.reshape(N_GROUPS, GROUP_SIZE, 1), earlier)


def _sc_mesh():
    return plsc.VectorSubcoreMesh(core_axis_name="core", subcore_axis_name="subcore")


def _sc_scatter_rows(src, idx, n_out):
    d = src.shape[1]
    wins = idx.shape[1] // SC_WINDOW

    def body(src_hbm, idx_hbm, out_hbm, sem):
        def step(src_vmem, idx_vmem):
            copies = [pltpu.make_async_copy(src_vmem, out_hbm.at[idx_vmem.at[k]], sem)
                      for k in range(TOP_K)]
            for copy in copies:
                copy.start()
            for copy in copies:
                copy.wait()

        pltpu.emit_pipeline(
            step,
            grid=(src.shape[0] // SC_WINDOW,),
            in_specs=[pl.BlockSpec((SC_WINDOW, d), index_map=lambda i: (i, 0)),
                      pl.BlockSpec((TOP_K, SC_WINDOW), index_map=lambda i: (i // wins, i % wins))],
            out_specs=[],
            core_axis_name=("core", "subcore"),
            dimension_semantics=(pltpu.PARALLEL,),
        )(src_hbm, idx_hbm)

    return pl.kernel(body, out_type=jax.ShapeDtypeStruct((n_out, d), src.dtype),
                     mesh=_sc_mesh(), scratch_types=[pltpu.SemaphoreType.DMA],
                     name="sc_scatter_rows")(src, idx)


def _sc_gather_sum(table, idx, gates):
    d = table.shape[1]
    n_tok = idx.shape[1]
    wins = n_tok // SC_WINDOW
    lanes = SC_LANES
    out_type = jax.ShapeDtypeStruct((PIECES * n_tok, d), _F32)

    def tree_sum(terms):
        while len(terms) > 1:
            terms = [terms[i] + terms[i + 1] for i in range(0, len(terms), 2)]
        return terms[0]

    def body(table_hbm, idx_hbm, gate_hbm, lo_hbm, hi_hbm, rows_vmem, sem):
        def step(idx_vmem, gate_vmem, lo_vmem, hi_vmem):
            def burst_copies(b):
                return [pltpu.make_async_copy(
                    table_hbm.at[idx_vmem.at[k, pl.ds(b * SC_BURST, SC_BURST)]],
                    rows_vmem.at[b % 2, k], sem.at[b % 2]) for k in range(TOP_K)]

            n_bursts = SC_WINDOW // SC_BURST
            for copy in burst_copies(0):
                copy.start()
            for b in range(n_bursts):
                if b + 1 < n_bursts:
                    for copy in burst_copies(b + 1):
                        copy.start()
                for copy in burst_copies(b):
                    copy.wait()
                t0 = b * SC_BURST
                rows = rows_vmem.at[b % 2]

                @plsc.parallel_loop(0, SC_BURST)
                def _(r):
                    t = t0 + r
                    token = jnp.full((lanes,), t, jnp.int32)
                    gate = [plsc.load_gather(gate_vmem, [jnp.full((lanes,), k, jnp.int32), token])
                            for k in range(TOP_K)]
                    for j in range(0, d, lanes):
                        words = [rows[k, r, pl.ds(j, lanes)] for k in range(TOP_K)]
                        lo_vmem[t, pl.ds(j, lanes)] = tree_sum(
                            [gate[k] * lax.bitcast_convert_type(words[k] << 16, _F32)
                             for k in range(TOP_K)])
                        hi_vmem[t, pl.ds(j, lanes)] = tree_sum(
                            [gate[k] * lax.bitcast_convert_type(words[k] & _U32(0xFFFF0000), _F32)
                             for k in range(TOP_K)])

        window = lambda i: (i // wins, i % wins)
        pltpu.emit_pipeline(
            step,
            grid=(PIECES * wins,),
            in_specs=[pl.BlockSpec((TOP_K, SC_WINDOW), index_map=window),
                      pl.BlockSpec((TOP_K, SC_WINDOW), index_map=lambda i: (0, i % wins))],
            out_specs=[pl.BlockSpec((SC_WINDOW, d), index_map=lambda i: (i, 0)),
                       pl.BlockSpec((SC_WINDOW, d), index_map=lambda i: (i, 0))],
            core_axis_name=("core", "subcore"),
            dimension_semantics=(pltpu.PARALLEL,),
        )(idx_hbm, gate_hbm, lo_hbm, hi_hbm)

    return pl.kernel(body, out_type=(out_type, out_type), mesh=_sc_mesh(),
                     scratch_types=[pltpu.VMEM((2, TOP_K, SC_BURST, d), _U32),
                                    pltpu.SemaphoreType.DMA((2,))],
                     compiler_params=pltpu.CompilerParams(needs_layout_passes=False),
                     name="sc_gather_sum")(table, idx, gates)


def _index_kernel(start_ref, ek_ref, rk_ref, idx_ref, *, n_rows):
    e = ek_ref[...]
    start = jnp.zeros(e.shape, jnp.int32)
    for ex in range(N_EXPERTS):
        start = jnp.where(e == ex, start_ref[ex], start)
    dest = start + rk_ref[...]
    for c in range(PIECES):
        idx_ref[c] = dest + c * n_rows


def _row_indices(padded_start, e_k, r_k, n_rows):
    n_tok = e_k.shape[1]
    ts = INDEX_TILE
    kspec = pl.BlockSpec((TOP_K, ts), lambda i, st: (0, i))
    grid_spec = pltpu.PrefetchScalarGridSpec(
        num_scalar_prefetch=1,
        grid=(n_tok // ts,),
        in_specs=[kspec, kspec],
        out_specs=pl.BlockSpec((PIECES, TOP_K, ts), lambda i, st: (0, 0, i)),
    )
    return pl.pallas_call(
        functools.partial(_index_kernel, n_rows=n_rows),
        grid_spec=grid_spec,
        out_shape=jax.ShapeDtypeStruct((PIECES, TOP_K, n_tok), jnp.int32),
        compiler_params=pltpu.CompilerParams(dimension_semantics=("arbitrary",)),
        name="moe_row_indices",
    )(padded_start, e_k, r_k)


def _expert_kernel(chunk0_ref, nchunk_ref, cnt_ref, total_ref,
                   xs_hbm, wg_ref, wu_ref, wd_ref, o_hbm,
                   xbuf, obuf, sem_in, sem_out):
    e = pl.program_id(0)
    total = total_ref[0]
    ch = EXPERT_ROWS
    ahead = IN_SLOTS - 1

    def in_copy(g):
        slot = g % IN_SLOTS
        return pltpu.make_async_copy(xs_hbm.at[:, pl.ds(g * ch, ch), :], xbuf.at[slot],
                                     sem_in.at[slot])

    def out_copy(g):
        slot = g % OUT_SLOTS
        return pltpu.make_async_copy(obuf.at[slot], o_hbm.at[:, pl.ds(g * ch, ch), :],
                                     sem_out.at[slot])

    @pl.when(e == 0)
    def _():
        obuf[...] = jnp.zeros_like(obuf)
        for g in range(ahead):
            @pl.when(g < total)
            def _():
                in_copy(g).start()


    def chunk(j, carry):
        g = chunk0_ref[e] + j
        in_copy(g).wait()

        @pl.when(g + ahead < total)
        def _():
            in_copy(g + ahead).start()

        @pl.when(g >= OUT_SLOTS)
        def _():
            out_copy(g - OUT_SLOTS).wait()

        islot = g % IN_SLOTS
        oslot = g % OUT_SLOTS
        valid = cnt_ref[e] - j * ch

        def gate_up(s):
            r0 = s * EXPERT_SUB
            w = jnp.concatenate([xbuf[islot, c, r0:r0 + EXPERT_SUB, :] for c in range(PIECES)],
                                axis=1)
            lo, hi = _unpack_halves(w)
            gate = _dot(lo, wg_ref[0, 0, :PACKED, :]) + _dot(hi, wg_ref[0, 0, PACKED:, :])
            up = _dot(lo, wu_ref[0, 0, :PACKED, :]) + _dot(hi, wu_ref[0, 0, PACKED:, :])
            return gate, up

        def down(s, gate_up_pair):
            r0 = s * EXPERT_SUB
            gate, up = gate_up_pair
            y = _pack_halves(_dot(_silu(gate) * up, wd_ref[0, 0]))
            for c in range(PIECES):
                obuf[oslot, c, r0:r0 + EXPERT_SUB, :] = y[:, c * LANES:(c + 1) * LANES]

        n_sub = ch // EXPERT_SUB

        all_subs = valid > ch - EXPERT_SUB

        @pl.when(all_subs)
        def _():
            projected = [gate_up(s) for s in range(n_sub)]
            for s, gu in enumerate(projected):
                down(s, gu)

        @pl.when(jnp.logical_not(all_subs))
        def _():
            for s in range(n_sub - 1):
                @pl.when(s * EXPERT_SUB < valid)
                def _():
                    down(s, gate_up(s))

        out_copy(g).start()
        return carry

    lax.fori_loop(0, nchunk_ref[e], chunk, 0)

    @pl.when(e == pl.num_programs(0) - 1)
    def _():
        for back in range(OUT_SLOTS, 0, -1):
            @pl.when(total >= back)
            def _():
                out_copy(total - back).wait()


def _experts(layer, chunk0, nchunk, counts, total, xs, w_gate, w_up, w_down):
    d = D_MODEL
    ch = EXPERT_ROWS
    wspec_in = pl.BlockSpec((1, 1, d, EXPERT_DIM), lambda e, *_: (layer, e, 0, 0))
    any_spec = pl.BlockSpec(memory_space=pl.ANY)
    grid_spec = pltpu.PrefetchScalarGridSpec(
        num_scalar_prefetch=4,
        grid=(N_EXPERTS,),
        in_specs=[
            any_spec,
            wspec_in,
            wspec_in,
            pl.BlockSpec((1, 1, EXPERT_DIM, d), lambda e, *_: (layer, e, 0, 0)),
        ],
        out_specs=any_spec,
        scratch_shapes=[pltpu.VMEM((IN_SLOTS, PIECES, ch, LANES), _U32),
                        pltpu.VMEM((OUT_SLOTS, PIECES, ch, LANES), _U32),
                        pltpu.SemaphoreType.DMA((IN_SLOTS,)),
                        pltpu.SemaphoreType.DMA((OUT_SLOTS,))],
    )
    return pl.pallas_call(
        _expert_kernel,
        grid_spec=grid_spec,
        out_shape=jax.ShapeDtypeStruct(xs.shape, _U32),
        compiler_params=pltpu.CompilerParams(
            dimension_semantics=("arbitrary",),
            vmem_limit_bytes=VMEM_LIMIT),
        name="moe_experts",
    )(chunk0, nchunk, counts, total, xs, w_gate, w_up, w_down)


def _combine_kernel(x_ref, lo_ref, hi_ref, sg_ref, su_ref, sd_ref, g_ref, b_ref, *rest):
    o_ref = rest[-1]
    x = x_ref[...]
    xb = x.astype(_BF16)
    routed = jnp.concatenate([lo_ref[c] for c in range(PIECES)] +
                             [hi_ref[c] for c in range(PIECES)], axis=1)
    h = _silu(_dot(xb, sg_ref[...])) * _dot(xb, su_ref[...])
    shared = _dot(h.astype(_BF16), sd_ref[...])
    o_ref[...] = _layer_norm(DEEPNORM_ALPHA * x + (routed + shared), g_ref[...], b_ref[...])


def _combine(x, routed_lo, routed_hi, sh_gate, sh_up, sh_down, ln_g, ln_b, out_parts, part, prev):
    n_part, d = x.shape
    ts = COMBINE_TILE
    steps = n_part // ts
    const2 = lambda i: (0, 0)
    in_specs = [
        pl.BlockSpec((ts, d), lambda i: (i, 0)),
        pl.BlockSpec((PIECES, ts, LANES), lambda i: (0, i, 0)),
        pl.BlockSpec((PIECES, ts, LANES), lambda i: (0, i, 0)),
        pl.BlockSpec((d, EXPERT_DIM), const2),
        pl.BlockSpec((d, EXPERT_DIM), const2),
        pl.BlockSpec((EXPERT_DIM, d), const2),
        pl.BlockSpec((1, d), const2),
        pl.BlockSpec((1, d), const2),
    ]
    args = [x, routed_lo, routed_hi, sh_gate.astype(_BF16), sh_up.astype(_BF16), sh_down.astype(_BF16),
            ln_g.reshape(1, d), ln_b.reshape(1, d)]
    aliases = {}
    if prev is not None:
        in_specs.append(pl.BlockSpec(memory_space=pl.ANY))
        aliases = {len(args): 0}
        args.append(prev)
    return pl.pallas_call(
        _combine_kernel,
        grid=(steps,),
        in_specs=in_specs,
        out_specs=pl.BlockSpec((ts, d), lambda i: (part * steps + i, 0)),
        out_shape=jax.ShapeDtypeStruct((out_parts * n_part, d), _F32),
        input_output_aliases=aliases,
        compiler_params=pltpu.CompilerParams(
            dimension_semantics=("arbitrary",),
            vmem_limit_bytes=VMEM_LIMIT),
        name="moe_combine",
    )(*args)


def _moe_dispatch(layer, x, x_packed, w_router, router_bias, w_gate, w_up, w_down):
    n_part, d = x.shape
    bm = EXPERT_ROWS
    n_rows = n_part * TOP_K + N_EXPERTS * bm
    e_k, r_k, g_k, counts = _route(x, w_router, router_bias)
    counts = counts.reshape(N_EXPERTS).astype(jnp.int32)
    padded = (counts + bm - 1) // bm * bm
    padded_end = jnp.cumsum(padded)
    padded_start = padded_end - padded
    idx = _row_indices(padded_start, e_k, r_k, n_rows).reshape(PIECES * TOP_K, n_part)
    xs = _sc_scatter_rows(x_packed.reshape(PIECES * n_part, LANES), idx, PIECES * n_rows)
    rows = _experts(layer, padded_start // bm, padded // bm, counts, padded_end[-1:] // bm,
                    xs.reshape(PIECES, n_rows, LANES), w_gate, w_up, w_down)
    return rows.reshape(PIECES * n_rows, LANES), idx, g_k


def _moe_combine(x, rows, idx, gates, sh_gate, sh_up, sh_down, ln_g, ln_b,
                 out_parts=1, part=0, prev=None):
    n_part = x.shape[0]
    lo, hi = _sc_gather_sum(rows, idx, gates)
    return _combine(x, lo.reshape(PIECES, n_part, LANES), hi.reshape(PIECES, n_part, LANES),
                    sh_gate, sh_up, sh_down, ln_g, ln_b, out_parts, part, prev)


def kernel(x, pool_w_in, pool_w_grp, pool_scale, pool_w_out, sgu_w_in, sgu_b_in, sgu_ln_g, sgu_ln_b, sgu_w_s, sgu_b_s, sgu_w_out, ln_mix_g, ln_mix_b, moe_w_router, moe_router_bias, moe_w_gate, moe_w_up, moe_w_down, moe_sh_gate, moe_sh_up, moe_sh_down, ln_ffn_g, ln_ffn_b):
    bsz, seq, d = x.shape
    parts = range(TOKEN_PARTS)

    def dispatch(i, h, h_packed):
        return _moe_dispatch(i, h, h_packed, moe_w_router[i], moe_router_bias[i],
                             moe_w_gate, moe_w_up, moe_w_down)

    def combine(i, h, routed, **where):
        return _moe_combine(h, *routed, moe_sh_gate[i], moe_sh_up[i], moe_sh_down[i],
                            ln_ffn_g[i], ln_ffn_b[i], **where)

    mixed = [_pool_layer(x, p, pool_w_in[0], pool_w_grp[0], pool_scale[0], pool_w_out[0],
                         ln_mix_g[0], ln_mix_b[0]) for p in parts]
    mixed = [(h.reshape(-1, d), h_packed) for h, h_packed in mixed]
    routed = [dispatch(0, h, h_packed) for h, h_packed in mixed]
    hs = [combine(0, mixed[p][0], routed[p]) for p in parts]
    mixed = [_sgu_layer(h, sgu_w_in[0], sgu_b_in[0], sgu_ln_g[0], sgu_ln_b[0], sgu_w_s[0],
                        sgu_b_s[0], sgu_w_out[0], ln_mix_g[1], ln_mix_b[1]) for h in hs]
    routed = [dispatch(1, h, h_packed) for h, h_packed in mixed]
    out = None
    for p in parts:
        out = combine(1, mixed[p][0], routed[p], out_parts=TOKEN_PARTS, part=p, prev=out)
    return out.reshape(bsz, seq, d)
```

```python
import functools

import jax
import jax.numpy as jnp
from jax import lax
from jax.experimental import pallas as pl
from jax.experimental.pallas import tpu as pltpu
from jax.experimental.pallas import tpu_sc as plsc

D_MODEL = 1024
DEPTH = 2
POOL_WINDOWS = (2, 4, 8, 16)
POOL_GROUP_DIM = D_MODEL // len(POOL_WINDOWS)
POOL_HALO = 16
SGU_CHUNK = 128
SGU_HEADS = 4
SGU_WIDTH = 2 * D_MODEL
SGU_HEAD_DIM = SGU_WIDTH // SGU_HEADS
N_EXPERTS = 64
TOP_K = 8
N_GROUPS = 8
GROUP_SIZE = N_EXPERTS // N_GROUPS
TOPK_GROUPS = 4
EXPERT_DIM = D_MODEL // 4
ROUTED_SCALE = 2.5
DEEPNORM_ALPHA = (2 * DEPTH) ** 0.25
LN_EPS = 1e-5

LANES = 128
PACKED = D_MODEL // 2
PIECES = PACKED // LANES

POOL_TILE = 1024
POOL_SUB = 256
SGU_TILE = 512
SGU_SUB = 256
ROUTE_TILE = 1024
ROUTE_SUB = 512
INDEX_TILE = 2048
EXPERT_ROWS = 512
EXPERT_SUB = 256
IN_SLOTS = 10
OUT_SLOTS = 6
COMBINE_TILE = 1024
SC_WINDOW = 128
SC_LANES = 16
SC_BURST = 16
TOKEN_PARTS = 2
V7X_VMEM_BYTES = 64 * 1024 * 1024
VMEM_LIMIT = V7X_VMEM_BYTES * 7 // 8

_F32 = jnp.float32
_BF16 = jnp.bfloat16
_U32 = jnp.uint32


def _dot(a, b):
    return jnp.dot(a, b, preferred_element_type=_F32)


def _layer_norm(h, g, b):
    mu = jnp.mean(h, axis=-1, keepdims=True)
    hc = h - mu
    var = jnp.mean(hc * hc, axis=-1, keepdims=True)
    return hc * lax.rsqrt(var + LN_EPS) * g + b


def _silu(x):
    return x * jax.nn.sigmoid(x)


def _gelu_tanh(x):
    c = 0.7978845608028654
    return 0.5 * x * (1.0 + jnp.tanh(c * (x + 0.044715 * (x * x * x))))


def _pack_halves(v):
    half = v.shape[1] // 2
    lo = lax.bitcast_convert_type(v[:, :half].astype(_BF16).astype(_F32), _U32)
    hi = lax.bitcast_convert_type(v[:, half:].astype(_BF16).astype(_F32), _U32)
    return hi | (lo >> 16)


def _unpack_halves(w):
    lo = lax.bitcast_convert_type(w << 16, _F32)
    hi = lax.bitcast_convert_type(w & _U32(0xFFFF0000), _F32)
    return lo, hi


def _store_pieces(ref, r0, w):
    for c in range(PIECES):
        ref[c, r0:r0 + w.shape[0], :] = w[:, c * LANES:(c + 1) * LANES]


def _pool_kernel(x_ref, win_ref, wgrp_ref, scale_ref, wout_ref, g_ref, b_ref,
                 o_ref, op_ref, zs_ref, y_ref):
    s = pl.program_id(1)
    ts = x_ref.shape[1]

    @pl.when(s == 0)
    def _():
        zs_ref[0:POOL_HALO, :] = jnp.zeros((POOL_HALO, D_MODEL), _F32)

    for p0 in range(0, ts, POOL_SUB):
        z = _dot(x_ref[0, p0:p0 + POOL_SUB, :].astype(_BF16), win_ref[...])
        zs_ref[POOL_HALO + p0:POOL_HALO + p0 + POOL_SUB, :] = z
    for p0 in range(0, ts, POOL_SUB):
        x = x_ref[0, p0:p0 + POOL_SUB, :]
        base = POOL_HALO + p0
        pos = s * ts + p0 + lax.broadcasted_iota(jnp.int32, (POOL_SUB, 1), 0)
        for g, w in enumerate(POOL_WINDOWS):
            c0 = g * POOL_GROUP_DIM
            c1 = c0 + POOL_GROUP_DIM
            zg = zs_ref[base:base + POOL_SUB, c0:c1]
            acc = zg
            for k in range(1, w):
                acc = acc + zs_ref[base - k:base - k + POOL_SUB, c0:c1]
            cnt = jnp.minimum(pos + 1, w).astype(_F32)
            pooled = acc / cnt - zg
            yg = _dot(pooled.astype(_BF16), wgrp_ref[g]) * scale_ref[:, c0:c1]
            y_ref[p0:p0 + POOL_SUB, c0:c1] = yg.astype(_BF16)
        mix = _dot(y_ref[p0:p0 + POOL_SUB, :], wout_ref[...])
        out = _layer_norm(DEEPNORM_ALPHA * x + mix, g_ref[...], b_ref[...])
        o_ref[0, p0:p0 + POOL_SUB, :] = out
        _store_pieces(op_ref, p0, _pack_halves(out))
    zs_ref[0:POOL_HALO, :] = zs_ref[ts:ts + POOL_HALO, :]


def _pool_layer(x, part, w_in, w_grp, scale, w_out, ln_g, ln_b):
    bsz, seq, d = x.shape
    rows = bsz // TOKEN_PARTS
    ts = POOL_TILE
    steps = seq // ts
    const2 = lambda b, s: (0, 0)
    out_shape = (jax.ShapeDtypeStruct((rows, seq, d), _F32),
                 jax.ShapeDtypeStruct((PIECES, rows * seq, LANES), _U32))
    tile = pl.BlockSpec((1, ts, d), lambda b, s: (b, s, 0))
    ptile = pl.BlockSpec((PIECES, ts, LANES), lambda b, s: (0, b * steps + s, 0))
    return pl.pallas_call(
        _pool_kernel,
        grid=(rows, steps),
        in_specs=[
            pl.BlockSpec((1, ts, d), lambda b, s: (part * rows + b, s, 0)),
            pl.BlockSpec((d, d), const2),
            pl.BlockSpec((len(POOL_WINDOWS), POOL_GROUP_DIM, POOL_GROUP_DIM), lambda b, s: (0, 0, 0)),
            pl.BlockSpec((1, d), const2),
            pl.BlockSpec((d, d), const2),
            pl.BlockSpec((1, d), const2),
            pl.BlockSpec((1, d), const2),
        ],
        out_specs=(tile, ptile),
        out_shape=out_shape,
        scratch_shapes=[pltpu.VMEM((POOL_HALO + ts, d), _F32),
                        pltpu.VMEM((ts, d), _BF16)],
        compiler_params=pltpu.CompilerParams(
            dimension_semantics=("arbitrary", "arbitrary"),
            vmem_limit_bytes=VMEM_LIMIT),
        name="pool_mixer",
    )(x, w_in.astype(_BF16), w_grp.astype(_BF16), scale.reshape(1, d),
      w_out.astype(_BF16), ln_g.reshape(1, d), ln_b.reshape(1, d))


def _sgu_kernel(x_ref, win_ref, bin_ref, lng_ref, lnb_ref, ws_ref, bs_ref, wout_ref,
                g_ref, b_ref, o_ref, op_ref, gated_ref):
    ts = x_ref.shape[0]
    projected = []
    for p0 in range(0, ts, SGU_SUB):
        x = x_ref[p0:p0 + SGU_SUB, :]
        xb = x.astype(_BF16)
        v = _gelu_tanh(_dot(xb, win_ref[:, SGU_WIDTH:]) + bin_ref[:, SGU_WIDTH:])
        v = _layer_norm(v, lng_ref[...], lnb_ref[...]).astype(_BF16)
        u = _gelu_tanh(_dot(xb, win_ref[:, :SGU_WIDTH]) + bin_ref[:, :SGU_WIDTH])
        projected.append((p0, x, u, v))
    for p0, x, u, v in projected:
        for r0 in range(0, SGU_SUB, SGU_CHUNK):
            for h in range(SGU_HEADS):
                c0 = h * SGU_HEAD_DIM
                sv = _dot(ws_ref[h], v[r0:r0 + SGU_CHUNK, c0:c0 + SGU_HEAD_DIM]) + bs_ref[h]
                gated_ref[p0 + r0:p0 + r0 + SGU_CHUNK, c0:c0 + SGU_HEAD_DIM] = (
                    u[r0:r0 + SGU_CHUNK, c0:c0 + SGU_HEAD_DIM] * sv).astype(_BF16)
        mix = _dot(gated_ref[p0:p0 + SGU_SUB, :], wout_ref[...])
        out = _layer_norm(DEEPNORM_ALPHA * x + mix, g_ref[...], b_ref[...])
        o_ref[p0:p0 + SGU_SUB, :] = out
        _store_pieces(op_ref, p0, _pack_halves(out))


def _sgu_layer(x, w_in, b_in, ln_g, ln_b, w_s, b_s, w_out, mix_g, mix_b):
    n_tok, d = x.shape
    ts = SGU_TILE
    const2 = lambda i: (0, 0)
    const3 = lambda i: (0, 0, 0)
    causal = jnp.tril(jnp.ones((SGU_CHUNK, SGU_CHUNK), w_s.dtype))
    ws = (w_s * causal[None]).astype(_BF16)
    tile = pl.BlockSpec((ts, d), lambda i: (i, 0))
    ptile = pl.BlockSpec((PIECES, ts, LANES), lambda i: (0, i, 0))
    out_shape = (jax.ShapeDtypeStruct((n_tok, d), _F32),
                 jax.ShapeDtypeStruct((PIECES, n_tok, LANES), _U32))
    return pl.pallas_call(
        _sgu_kernel,
        grid=(n_tok // ts,),
        in_specs=[
            tile,
            pl.BlockSpec((d, 2 * SGU_WIDTH), const2),
            pl.BlockSpec((1, 2 * SGU_WIDTH), const2),
            pl.BlockSpec((1, SGU_WIDTH), const2),
            pl.BlockSpec((1, SGU_WIDTH), const2),
            pl.BlockSpec((SGU_HEADS, SGU_CHUNK, SGU_CHUNK), const3),
            pl.BlockSpec((SGU_HEADS, SGU_CHUNK, 1), const3),
            pl.BlockSpec((SGU_WIDTH, d), const2),
            pl.BlockSpec((1, d), const2),
            pl.BlockSpec((1, d), const2),
        ],
        out_specs=(tile, ptile),
        out_shape=out_shape,
        scratch_shapes=[pltpu.VMEM((ts, SGU_WIDTH), _BF16)],
        compiler_params=pltpu.CompilerParams(
            dimension_semantics=("arbitrary",),
            vmem_limit_bytes=VMEM_LIMIT),
        name="sgu_mixer",
    )(x, w_in.astype(_BF16), b_in.reshape(1, -1), ln_g.reshape(1, -1), ln_b.reshape(1, -1),
      ws, b_s.reshape(SGU_HEADS, SGU_CHUNK, 1), w_out.astype(_BF16),
      mix_g.reshape(1, d), mix_b.reshape(1, d))


def _all_max(a):
    return jnp.max(jnp.max(a, axis=0, keepdims=True), axis=1, keepdims=True)


def _all_min(a):
    return jnp.min(jnp.min(a, axis=0, keepdims=True), axis=1, keepdims=True)


def _all_sum(a):
    return jnp.sum(jnp.sum(a, axis=0, keepdims=True), axis=1, keepdims=True)


def _route_kernel(x_ref, whi_ref, wlo_ref, bias_ref, earlier_ref,
                  ek_ref, rk_ref, gk_ref, cnt_ref, carry_ref):
    @pl.when(pl.program_id(0) == 0)
    def _():
        carry_ref[...] = jnp.zeros_like(carry_ref)

    nt = lambda a, b: lax.dot_general(a, b, (((1,), (1,)), ((), ())),
                                      preferred_element_type=_F32)
    logits = []
    for p0 in range(0, x_ref.shape[0], ROUTE_SUB):
        x = x_ref[p0:p0 + ROUTE_SUB, :]
        x_hi = x.astype(_BF16)
        x_lo = (x - x_hi.astype(_F32)).astype(_BF16)
        logits.append(nt(whi_ref[...], x_hi) + (nt(whi_ref[...], x_lo) + nt(wlo_ref[...], x_hi)))
    for n, sub_logits in enumerate(logits):
        _route_sub_tile(sub_logits, n * ROUTE_SUB, bias_ref, earlier_ref,
                        ek_ref, rk_ref, gk_ref, carry_ref)
    cnt_ref[...] = carry_ref[...]


def _route_sub_tile(logits, p0, bias_ref, earlier_ref, ek_ref, rk_ref, gk_ref, carry_ref):
    ts = ROUTE_SUB
    scores = jax.nn.sigmoid(logits).reshape(N_GROUPS, GROUP_SIZE, ts)
    biased = scores + bias_ref[...]
    neg_inf = jnp.float32(-jnp.inf)
    shape3 = (N_GROUPS, GROUP_SIZE, ts)
    in_grp = lax.broadcasted_iota(jnp.int32, shape3, 1)
    grp = lax.broadcasted_iota(jnp.int32, shape3, 0)
    eid = grp * GROUP_SIZE + in_grp

    m1 = jnp.max(biased, axis=1, keepdims=True)
    first1 = jnp.min(jnp.where(biased == m1, in_grp, GROUP_SIZE), axis=1, keepdims=True)
    m2 = jnp.max(jnp.where(in_grp == first1, neg_inf, biased), axis=1, keepdims=True)
    gscore = m1 + m2

    gid = lax.broadcasted_iota(jnp.int32, (N_GROUPS, 1, ts), 0)
    gsel = jnp.zeros((N_GROUPS, 1, ts), jnp.bool_)
    for _ in range(TOPK_GROUPS):
        m = jnp.max(gscore, axis=0, keepdims=True)
        first = jnp.min(jnp.where(gscore == m, gid, N_GROUPS), axis=0, keepdims=True)
        pick = gid == first
        gsel = jnp.logical_or(gsel, pick)
        gscore = jnp.where(pick, neg_inf, gscore)

    masked = jnp.where(gsel, biased, neg_inf)
    picked_any = jnp.zeros(shape3, jnp.bool_)
    e_k, s_k = [], []
    for _ in range(TOP_K):
        m = _all_max(masked)
        first = _all_min(jnp.where(masked == m, eid, N_EXPERTS))
        pick = eid == first
        picked_any = jnp.logical_or(picked_any, pick)
        masked = jnp.where(pick, neg_inf, masked)
        e_k.append(first)
        s_k.append(_all_sum(jnp.where(pick, scores, 0.0)))
    denom = s_k[0]
    for k in range(1, TOP_K):
        denom = denom + s_k[k]

    sel = picked_any.astype(_F32).reshape(N_EXPERTS, ts)
    rank = _dot(sel.astype(_BF16), earlier_ref[...]) + carry_ref[...]
    rank3 = rank.reshape(shape3)
    carry_ref[...] += jnp.sum(sel, axis=1, keepdims=True)

    for k in range(TOP_K):
        r = _all_sum(jnp.where(eid == e_k[k], rank3, 0.0))
        ek_ref[k:k + 1, p0:p0 + ts] = e_k[k].reshape(1, ts)
        rk_ref[k:k + 1, p0:p0 + ts] = r.reshape(1, ts).astype(jnp.int32)
        gk_ref[k:k + 1, p0:p0 + ts] = (s_k[k] / denom * ROUTED_SCALE).reshape(1, ts)


def _route(x, w_router, router_bias):
    n_part, d = x.shape
    ts = ROUTE_TILE
    steps = n_part // ts
    out_shape = (jax.ShapeDtypeStruct((TOP_K, n_part), jnp.int32),
                 jax.ShapeDtypeStruct((TOP_K, n_part), jnp.int32),
                 jax.ShapeDtypeStruct((TOP_K, n_part), _F32),
                 jax.ShapeDtypeStruct((N_EXPERTS, 1), _F32))
    kspec = pl.BlockSpec((TOP_K, ts), lambda i: (0, i))
    const2 = lambda i: (0, 0)
    w_t = w_router.T
    w_hi = w_t.astype(_BF16)
    w_lo = (w_t - w_hi.astype(_F32)).astype(_BF16)
    pos = jnp.arange(ROUTE_SUB, dtype=jnp.int32)
    earlier = (pos[:, None] < pos[None, :]).astype(_BF16)
    return pl.pallas_call(
        _route_kernel,
        grid=(steps,),
        in_specs=[
            pl.BlockSpec((ts, d), lambda i: (i, 0)),
            pl.BlockSpec((N_EXPERTS, d), const2),
            pl.BlockSpec((N_EXPERTS, d), const2),
            pl.BlockSpec((N_GROUPS, GROUP_SIZE, 1), lambda i: (0, 0, 0)),
            pl.BlockSpec((ROUTE_SUB, ROUTE_SUB), const2),
        ],
        out_specs=(kspec, kspec, kspec, pl.BlockSpec((N_EXPERTS, 1), const2)),
        out_shape=out_shape,
        scratch_shapes=[pltpu.VMEM((N_EXPERTS, 1), _F32)],
        compiler_params=pltpu.CompilerParams(
            dimension_semantics=("arbitrary",),
            vmem_limit_bytes=VMEM_LIMIT),
        name="moe_route",
    )(x, w_hi, w_lo, router_bias.reshape(N_GROUPS, GROUP_SIZE, 1), earlier)


def _sc_mesh():
    return plsc.VectorSubcoreMesh(core_axis_name="core", subcore_axis_name="subcore")


def _sc_scatter_rows(src, idx, n_out):
    d = src.shape[1]
    wins = idx.shape[1] // SC_WINDOW

    def body(src_hbm, idx_hbm, out_hbm, sem):
        def step(src_vmem, idx_vmem):
            copies = [pltpu.make_async_copy(src_vmem, out_hbm.at[idx_vmem.at[k]], sem)
                      for k in range(TOP_K)]
            for copy in copies:
                copy.start()
            for copy in copies:
                copy.wait()

        pltpu.emit_pipeline(
            step,
            grid=(src.shape[0] // SC_WINDOW,),
            in_specs=[pl.BlockSpec((SC_WINDOW, d), index_map=lambda i: (i, 0)),
                      pl.BlockSpec((TOP_K, SC_WINDOW), index_map=lambda i: (i // wins, i % wins))],
            out_specs=[],
            core_axis_name=("core", "subcore"),
            dimension_semantics=(pltpu.PARALLEL,),
        )(src_hbm, idx_hbm)

    return pl.kernel(body, out_type=jax.ShapeDtypeStruct((n_out, d), src.dtype),
                     mesh=_sc_mesh(), scratch_types=[pltpu.SemaphoreType.DMA],
                     name="sc_scatter_rows")(src, idx)


def _sc_gather_sum(table, idx, gates):
    d = table.shape[1]
    n_tok = idx.shape[1]
    wins = n_tok // SC_WINDOW
    lanes = SC_LANES
    out_type = jax.ShapeDtypeStruct((PIECES * n_tok, d), _F32)

    def tree_sum(terms):
        while len(terms) > 1:
            terms = [terms[i] + terms[i + 1] for i in range(0, len(terms), 2)]
        return terms[0]

    def body(table_hbm, idx_hbm, gate_hbm, lo_hbm, hi_hbm, rows_vmem, sem):
        def step(idx_vmem, gate_vmem, lo_vmem, hi_vmem):
            def burst_copies(b):
                return [pltpu.make_async_copy(
                    table_hbm.at[idx_vmem.at[k, pl.ds(b * SC_BURST, SC_BURST)]],
                    rows_vmem.at[b % 2, k], sem.at[b % 2]) for k in range(TOP_K)]

            n_bursts = SC_WINDOW // SC_BURST
            for copy in burst_copies(0):
                copy.start()
            for b in range(n_bursts):
                if b + 1 < n_bursts:
                    for copy in burst_copies(b + 1):
                        copy.start()
                for copy in burst_copies(b):
                    copy.wait()
                t0 = b * SC_BURST
                rows = rows_vmem.at[b % 2]

                @plsc.parallel_loop(0, SC_BURST)
                def _(r):
                    t = t0 + r
                    token = jnp.full((lanes,), t, jnp.int32)
                    gate = [plsc.load_gather(gate_vmem, [jnp.full((lanes,), k, jnp.int32), token])
                            for k in range(TOP_K)]
                    for j in range(0, d, lanes):
                        words = [rows[k, r, pl.ds(j, lanes)] for k in range(TOP_K)]
                        lo_vmem[t, pl.ds(j, lanes)] = tree_sum(
                            [gate[k] * lax.bitcast_convert_type(words[k] << 16, _F32)
                             for k in range(TOP_K)])
                        hi_vmem[t, pl.ds(j, lanes)] = tree_sum(
                            [gate[k] * lax.bitcast_convert_type(words[k] & _U32(0xFFFF0000), _F32)
                             for k in range(TOP_K)])

        window = lambda i: (i // wins, i % wins)
        pltpu.emit_pipeline(
            step,
            grid=(PIECES * wins,),
            in_specs=[pl.BlockSpec((TOP_K, SC_WINDOW), index_map=window),
                      pl.BlockSpec((TOP_K, SC_WINDOW), index_map=lambda i: (0, i % wins))],
            out_specs=[pl.BlockSpec((SC_WINDOW, d), index_map=lambda i: (i, 0)),
                       pl.BlockSpec((SC_WINDOW, d), index_map=lambda i: (i, 0))],
            core_axis_name=("core", "subcore"),
            dimension_semantics=(pltpu.PARALLEL,),
        )(idx_hbm, gate_hbm, lo_hbm, hi_hbm)

    return pl.kernel(body, out_type=(out_type, out_type), mesh=_sc_mesh(),
                     scratch_types=[pltpu.VMEM((2, TOP_K, SC_BURST, d), _U32),
                                    pltpu.SemaphoreType.DMA((2,))],
                     compiler_params=pltpu.CompilerParams(needs_layout_passes=False),
                     name="sc_gather_sum")(table, idx, gates)


def _index_kernel(start_ref, ek_ref, rk_ref, idx_ref, *, n_rows):
    e = ek_ref[...]
    start = jnp.zeros(e.shape, jnp.int32)
    for ex in range(N_EXPERTS):
        start = jnp.where(e == ex, start_ref[ex], start)
    dest = start + rk_ref[...]
    for c in range(PIECES):
        idx_ref[c] = dest + c * n_rows


def _row_indices(padded_start, e_k, r_k, n_rows):
    n_tok = e_k.shape[1]
    ts = INDEX_TILE
    kspec = pl.BlockSpec((TOP_K, ts), lambda i, st: (0, i))
    grid_spec = pltpu.PrefetchScalarGridSpec(
        num_scalar_prefetch=1,
        grid=(n_tok // ts,),
        in_specs=[kspec, kspec],
        out_specs=pl.BlockSpec((PIECES, TOP_K, ts), lambda i, st: (0, 0, i)),
    )
    return pl.pallas_call(
        functools.partial(_index_kernel, n_rows=n_rows),
        grid_spec=grid_spec,
        out_shape=jax.ShapeDtypeStruct((PIECES, TOP_K, n_tok), jnp.int32),
        compiler_params=pltpu.CompilerParams(dimension_semantics=("arbitrary",)),
        name="moe_row_indices",
    )(padded_start, e_k, r_k)


def _expert_kernel(chunk0_ref, nchunk_ref, cnt_ref, total_ref,
                   xs_hbm, wg_ref, wu_ref, wd_ref, o_hbm,
                   xbuf, obuf, sem_in, sem_out):
    e = pl.program_id(0)
    total = total_ref[0]
    ch = EXPERT_ROWS
    ahead = IN_SLOTS - 1

    def in_copy(g):
        slot = g % IN_SLOTS
        return pltpu.make_async_copy(xs_hbm.at[:, pl.ds(g * ch, ch), :], xbuf.at[slot],
                                     sem_in.at[slot])

    def out_copy(g):
        slot = g % OUT_SLOTS
        return pltpu.make_async_copy(obuf.at[slot], o_hbm.at[:, pl.ds(g * ch, ch), :],
                                     sem_out.at[slot])

    @pl.when(e == 0)
    def _():
        obuf[...] = jnp.zeros_like(obuf)
        for g in range(ahead):
            @pl.when(g < total)
            def _():
                in_copy(g).start()


    def chunk(j, carry):
        g = chunk0_ref[e] + j
        in_copy(g).wait()

        @pl.when(g + ahead < total)
        def _():
            in_copy(g + ahead).start()

        @pl.when(g >= OUT_SLOTS)
        def _():
            out_copy(g - OUT_SLOTS).wait()

        islot = g % IN_SLOTS
        oslot = g % OUT_SLOTS
        valid = cnt_ref[e] - j * ch

        def gate_up(s):
            r0 = s * EXPERT_SUB
            w = jnp.concatenate([xbuf[islot, c, r0:r0 + EXPERT_SUB, :] for c in range(PIECES)],
                                axis=1)
            lo, hi = _unpack_halves(w)
            gate = _dot(lo, wg_ref[0, 0, :PACKED, :]) + _dot(hi, wg_ref[0, 0, PACKED:, :])
            up = _dot(lo, wu_ref[0, 0, :PACKED, :]) + _dot(hi, wu_ref[0, 0, PACKED:, :])
            return gate, up

        def down(s, gate_up_pair):
            r0 = s * EXPERT_SUB
            gate, up = gate_up_pair
            y = _pack_halves(_dot(_silu(gate) * up, wd_ref[0, 0]))
            for c in range(PIECES):
                obuf[oslot, c, r0:r0 + EXPERT_SUB, :] = y[:, c * LANES:(c + 1) * LANES]

        n_sub = ch // EXPERT_SUB

        all_subs = valid > ch - EXPERT_SUB

        @pl.when(all_subs)
        def _():
            projected = [gate_up(s) for s in range(n_sub)]
            for s, gu in enumerate(projected):
                down(s, gu)

        @pl.when(jnp.logical_not(all_subs))
        def _():
            for s in range(n_sub - 1):
                @pl.when(s * EXPERT_SUB < valid)
                def _():
                    down(s, gate_up(s))

        out_copy(g).start()
        return carry

    lax.fori_loop(0, nchunk_ref[e], chunk, 0)

    @pl.when(e == pl.num_programs(0) - 1)
    def _():
        for back in range(OUT_SLOTS, 0, -1):
            @pl.when(total >= back)
            def _():
                out_copy(total - back).wait()


def _experts(layer, chunk0, nchunk, counts, total, xs, w_gate, w_up, w_down):
    d = D_MODEL
    ch = EXPERT_ROWS
    wspec_in = pl.BlockSpec((1, 1, d, EXPERT_DIM), lambda e, *_: (layer, e, 0, 0))
    any_spec = pl.BlockSpec(memory_space=pl.ANY)
    grid_spec = pltpu.PrefetchScalarGridSpec(
        num_scalar_prefetch=4,
        grid=(N_EXPERTS,),
        in_specs=[
            any_spec,
            wspec_in,
            wspec_in,
            pl.BlockSpec((1, 1, EXPERT_DIM, d), lambda e, *_: (layer, e, 0, 0)),
        ],
        out_specs=any_spec,
        scratch_shapes=[pltpu.VMEM((IN_SLOTS, PIECES, ch, LANES), _U32),
                        pltpu.VMEM((OUT_SLOTS, PIECES, ch, LANES), _U32),
                        pltpu.SemaphoreType.DMA((IN_SLOTS,)),
                        pltpu.SemaphoreType.DMA((OUT_SLOTS,))],
    )
    return pl.pallas_call(
        _expert_kernel,
        grid_spec=grid_spec,
        out_shape=jax.ShapeDtypeStruct(xs.shape, _U32),
        compiler_params=pltpu.CompilerParams(
            dimension_semantics=("arbitrary",),
            vmem_limit_bytes=VMEM_LIMIT),
        name="moe_experts",
    )(chunk0, nchunk, counts, total, xs, w_gate, w_up, w_down)


def _combine_kernel(x_ref, lo_ref, hi_ref, sg_ref, su_ref, sd_ref, g_ref, b_ref, *rest):
    o_ref = rest[-1]
    x = x_ref[...]
    xb = x.astype(_BF16)
    routed = jnp.concatenate([lo_ref[c] for c in range(PIECES)] +
                             [hi_ref[c] for c in range(PIECES)], axis=1)
    h = _silu(_dot(xb, sg_ref[...])) * _dot(xb, su_ref[...])
    shared = _dot(h.astype(_BF16), sd_ref[...])
    o_ref[...] = _layer_norm(DEEPNORM_ALPHA * x + (routed + shared), g_ref[...], b_ref[...])


def _combine(x, routed_lo, routed_hi, sh_gate, sh_up, sh_down, ln_g, ln_b, out_parts, part, prev):
    n_part, d = x.shape
    ts = COMBINE_TILE
    steps = n_part // ts
    const2 = lambda i: (0, 0)
    in_specs = [
        pl.BlockSpec((ts, d), lambda i: (i, 0)),
        pl.BlockSpec((PIECES, ts, LANES), lambda i: (0, i, 0)),
        pl.BlockSpec((PIECES, ts, LANES), lambda i: (0, i, 0)),
        pl.BlockSpec((d, EXPERT_DIM), const2),
        pl.BlockSpec((d, EXPERT_DIM), const2),
        pl.BlockSpec((EXPERT_DIM, d), const2),
        pl.BlockSpec((1, d), const2),
        pl.BlockSpec((1, d), const2),
    ]
    args = [x, routed_lo, routed_hi, sh_gate.astype(_BF16), sh_up.astype(_BF16), sh_down.astype(_BF16),
            ln_g.reshape(1, d), ln_b.reshape(1, d)]
    aliases = {}
    if prev is not None:
        in_specs.append(pl.BlockSpec(memory_space=pl.ANY))
        aliases = {len(args): 0}
        args.append(prev)
    return pl.pallas_call(
        _combine_kernel,
        grid=(steps,),
        in_specs=in_specs,
        out_specs=pl.BlockSpec((ts, d), lambda i: (part * steps + i, 0)),
        out_shape=jax.ShapeDtypeStruct((out_parts * n_part, d), _F32),
        input_output_aliases=aliases,
        compiler_params=pltpu.CompilerParams(
            dimension_semantics=("arbitrary",),
            vmem_limit_bytes=VMEM_LIMIT),
        name="moe_combine",
    )(*args)


def _moe_dispatch(layer, x, x_packed, w_router, router_bias, w_gate, w_up, w_down):
    n_part, d = x.shape
    bm = EXPERT_ROWS
    n_rows = n_part * TOP_K + N_EXPERTS * bm
    e_k, r_k, g_k, counts = _route(x, w_router, router_bias)
    counts = counts.reshape(N_EXPERTS).astype(jnp.int32)
    padded = (counts + bm - 1) // bm * bm
    padded_end = jnp.cumsum(padded)
    padded_start = padded_end - padded
    idx = _row_indices(padded_start, e_k, r_k, n_rows).reshape(PIECES * TOP_K, n_part)
    xs = _sc_scatter_rows(x_packed.reshape(PIECES * n_part, LANES), idx, PIECES * n_rows)
    rows = _experts(layer, padded_start // bm, padded // bm, counts, padded_end[-1:] // bm,
                    xs.reshape(PIECES, n_rows, LANES), w_gate, w_up, w_down)
    return rows.reshape(PIECES * n_rows, LANES), idx, g_k


def _moe_combine(x, rows, idx, gates, sh_gate, sh_up, sh_down, ln_g, ln_b,
                 out_parts=1, part=0, prev=None):
    n_part = x.shape[0]
    lo, hi = _sc_gather_sum(rows, idx, gates)
    return _combine(x, lo.reshape(PIECES, n_part, LANES), hi.reshape(PIECES, n_part, LANES),
                    sh_gate, sh_up, sh_down, ln_g, ln_b, out_parts, part, prev)


def kernel(x, pool_w_in, pool_w_grp, pool_scale, pool_w_out, sgu_w_in, sgu_b_in, sgu_ln_g, sgu_ln_b, sgu_w_s, sgu_b_s, sgu_w_out, ln_mix_g, ln_mix_b, moe_w_router, moe_router_bias, moe_w_gate, moe_w_up, moe_w_down, moe_sh_gate, moe_sh_up, moe_sh_down, ln_ffn_g, ln_ffn_b):
    bsz, seq, d = x.shape
    parts = range(TOKEN_PARTS)

    def dispatch(i, h, h_packed):
        return _moe_dispatch(i, h, h_packed, moe_w_router[i], moe_router_bias[i],
                             moe_w_gate, moe_w_up, moe_w_down)

    def combine(i, h, routed, **where):
        return _moe_combine(h, *routed, moe_sh_gate[i], moe_sh_up[i], moe_sh_down[i],
                            ln_ffn_g[i], ln_ffn_b[i], **where)

    mixed = [_pool_layer(x, p, pool_w_in[0], pool_w_grp[0], pool_scale[0], pool_w_out[0],
                         ln_mix_g[0], ln_mix_b[0]) for p in parts]
    mixed = [(h.reshape(-1, d), h_packed) for h, h_packed in mixed]
    routed = [dispatch(0, h, h_packed) for h, h_packed in mixed]
    hs = [combine(0, mixed[p][0], routed[p]) for p in parts]
    mixed = [_sgu_layer(h, sgu_w_in[0], sgu_b_in[0], sgu_ln_g[0], sgu_ln_b[0], sgu_w_s[0],
                        sgu_b_s[0], sgu_w_out[0], ln_mix_g[1], ln_mix_b[1]) for h in hs]
    routed = [dispatch(1, h, h_packed) for h, h_packed in mixed]
    out = None
    for p in parts:
        out = combine(1, mixed[p][0], routed[p], out_parts=TOKEN_PARTS, part=p, prev=out)
    return out.reshape(bsz, seq, d)
```

```python
import functools

import jax
import jax.numpy as jnp
from jax import lax
from jax.experimental import pallas as pl
from jax.experimental.pallas import tpu as pltpu
from jax.experimental.pallas import tpu_sc as plsc

D_MODEL = 1024
DEPTH = 2
POOL_WINDOWS = (2, 4, 8, 16)
POOL_GROUP_DIM = D_MODEL // len(POOL_WINDOWS)
POOL_HALO = 16
SGU_CHUNK = 128
SGU_HEADS = 4
SGU_WIDTH = 2 * D_MODEL
SGU_HEAD_DIM = SGU_WIDTH // SGU_HEADS
N_EXPERTS = 64
TOP_K = 8
N_GROUPS = 8
GROUP_SIZE = N_EXPERTS // N_GROUPS
TOPK_GROUPS = 4
EXPERT_DIM = D_MODEL // 4
ROUTED_SCALE = 2.5
DEEPNORM_ALPHA = (2 * DEPTH) ** 0.25
LN_EPS = 1e-5

LANES = 128
PACKED = D_MODEL // 2
PIECES = PACKED // LANES

POOL_TILE = 1024
POOL_SUB = 256
SGU_TILE = 512
SGU_SUB = 256
ROUTE_TILE = 1024
ROUTE_SUB = 512
INDEX_TILE = 2048
EXPERT_ROWS = 512
EXPERT_SUB = 256
IN_SLOTS = 6
OUT_SLOTS = 4
COMBINE_TILE = 1024
SC_WINDOW = 128
SC_LANES = 16
SC_BURST = 16
TOKEN_PARTS = 2
V7X_VMEM_BYTES = 64 * 1024 * 1024
VMEM_LIMIT = V7X_VMEM_BYTES * 7 // 8

_F32 = jnp.float32
_BF16 = jnp.bfloat16
_U32 = jnp.uint32


def _dot(a, b):
    return jnp.dot(a, b, preferred_element_type=_F32)


def _layer_norm(h, g, b):
    mu = jnp.mean(h, axis=-1, keepdims=True)
    hc = h - mu
    var = jnp.mean(hc * hc, axis=-1, keepdims=True)
    return hc * lax.rsqrt(var + LN_EPS) * g + b


def _silu(x):
    return x * jax.nn.sigmoid(x)


def _gelu_tanh(x):
    c = 0.7978845608028654
    return 0.5 * x * (1.0 + jnp.tanh(c * (x + 0.044715 * (x * x * x))))


def _pack_halves(v):
    half = v.shape[1] // 2
    lo = lax.bitcast_convert_type(v[:, :half].astype(_BF16).astype(_F32), _U32)
    hi = lax.bitcast_convert_type(v[:, half:].astype(_BF16).astype(_F32), _U32)
    return hi | (lo >> 16)


def _unpack_halves(w):
    lo = lax.bitcast_convert_type(w << 16, _F32)
    hi = lax.bitcast_convert_type(w & _U32(0xFFFF0000), _F32)
    return lo, hi


def _store_pieces(ref, r0, w):
    for c in range(PIECES):
        ref[c, r0:r0 + w.shape[0], :] = w[:, c * LANES:(c + 1) * LANES]


def _pool_kernel(x_ref, win_ref, wgrp_ref, scale_ref, wout_ref, g_ref, b_ref,
                 o_ref, op_ref, zs_ref, y_ref):
    s = pl.program_id(1)
    ts = x_ref.shape[1]

    @pl.when(s == 0)
    def _():
        zs_ref[0:POOL_HALO, :] = jnp.zeros((POOL_HALO, D_MODEL), _F32)

    for p0 in range(0, ts, POOL_SUB):
        z = _dot(x_ref[0, p0:p0 + POOL_SUB, :].astype(_BF16), win_ref[...])
        zs_ref[POOL_HALO + p0:POOL_HALO + p0 + POOL_SUB, :] = z
    for p0 in range(0, ts, POOL_SUB):
        x = x_ref[0, p0:p0 + POOL_SUB, :]
        base = POOL_HALO + p0
        pos = s * ts + p0 + lax.broadcasted_iota(jnp.int32, (POOL_SUB, 1), 0)
        for g, w in enumerate(POOL_WINDOWS):
            c0 = g * POOL_GROUP_DIM
            c1 = c0 + POOL_GROUP_DIM
            zg = zs_ref[base:base + POOL_SUB, c0:c1]
            acc = zg
            for k in range(1, w):
                acc = acc + zs_ref[base - k:base - k + POOL_SUB, c0:c1]
            cnt = jnp.minimum(pos + 1, w).astype(_F32)
            pooled = acc / cnt - zg
            yg = _dot(pooled.astype(_BF16), wgrp_ref[g]) * scale_ref[:, c0:c1]
            y_ref[p0:p0 + POOL_SUB, c0:c1] = yg.astype(_BF16)
        mix = _dot(y_ref[p0:p0 + POOL_SUB, :], wout_ref[...])
        out = _layer_norm(DEEPNORM_ALPHA * x + mix, g_ref[...], b_ref[...])
        o_ref[0, p0:p0 + POOL_SUB, :] = out
        _store_pieces(op_ref, p0, _pack_halves(out))
    zs_ref[0:POOL_HALO, :] = zs_ref[ts:ts + POOL_HALO, :]


def _pool_layer(x, part, w_in, w_grp, scale, w_out, ln_g, ln_b):
    bsz, seq, d = x.shape
    rows = bsz // TOKEN_PARTS
    ts = POOL_TILE
    steps = seq // ts
    const2 = lambda b, s: (0, 0)
    out_shape = (jax.ShapeDtypeStruct((rows, seq, d), _F32),
                 jax.ShapeDtypeStruct((PIECES, rows * seq, LANES), _U32))
    tile = pl.BlockSpec((1, ts, d), lambda b, s: (b, s, 0))
    ptile = pl.BlockSpec((PIECES, ts, LANES), lambda b, s: (0, b * steps + s, 0))
    return pl.pallas_call(
        _pool_kernel,
        grid=(rows, steps),
        in_specs=[
            pl.BlockSpec((1, ts, d), lambda b, s: (part * rows + b, s, 0)),
            pl.BlockSpec((d, d), const2),
            pl.BlockSpec((len(POOL_WINDOWS), POOL_GROUP_DIM, POOL_GROUP_DIM), lambda b, s: (0, 0, 0)),
            pl.BlockSpec((1, d), const2),
            pl.BlockSpec((d, d), const2),
            pl.BlockSpec((1, d), const2),
            pl.BlockSpec((1, d), const2),
        ],
        out_specs=(tile, ptile),
        out_shape=out_shape,
        scratch_shapes=[pltpu.VMEM((POOL_HALO + ts, d), _F32),
                        pltpu.VMEM((ts, d), _BF16)],
        compiler_params=pltpu.CompilerParams(
            dimension_semantics=("arbitrary", "arbitrary"),
            vmem_limit_bytes=VMEM_LIMIT),
        name="pool_mixer",
    )(x, w_in.astype(_BF16), w_grp.astype(_BF16), scale.reshape(1, d),
      w_out.astype(_BF16), ln_g.reshape(1, d), ln_b.reshape(1, d))


def _sgu_kernel(x_ref, win_ref, bin_ref, lng_ref, lnb_ref, ws_ref, bs_ref, wout_ref,
                g_ref, b_ref, o_ref, op_ref, gated_ref):
    ts = x_ref.shape[0]
    projected = []
    for p0 in range(0, ts, SGU_SUB):
        x = x_ref[p0:p0 + SGU_SUB, :]
        xb = x.astype(_BF16)
        v = _gelu_tanh(_dot(xb, win_ref[:, SGU_WIDTH:]) + bin_ref[:, SGU_WIDTH:])
        v = _layer_norm(v, lng_ref[...], lnb_ref[...]).astype(_BF16)
        u = _gelu_tanh(_dot(xb, win_ref[:, :SGU_WIDTH]) + bin_ref[:, :SGU_WIDTH])
        projected.append((p0, x, u, v))
    for p0, x, u, v in projected:
        for r0 in range(0, SGU_SUB, SGU_CHUNK):
            for h in range(SGU_HEADS):
                c0 = h * SGU_HEAD_DIM
                sv = _dot(ws_ref[h], v[r0:r0 + SGU_CHUNK, c0:c0 + SGU_HEAD_DIM]) + bs_ref[h]
                gated_ref[p0 + r0:p0 + r0 + SGU_CHUNK, c0:c0 + SGU_HEAD_DIM] = (
                    u[r0:r0 + SGU_CHUNK, c0:c0 + SGU_HEAD_DIM] * sv).astype(_BF16)
        mix = _dot(gated_ref[p0:p0 + SGU_SUB, :], wout_ref[...])
        out = _layer_norm(DEEPNORM_ALPHA * x + mix, g_ref[...], b_ref[...])
        o_ref[p0:p0 + SGU_SUB, :] = out
        _store_pieces(op_ref, p0, _pack_halves(out))


def _all_max(a):
    return jnp.max(jnp.max(a, axis=0, keepdims=True), axis=1, keepdims=True)


def _all_min(a):
    return jnp.min(jnp.min(a, axis=0, keepdims=True), axis=1, keepdims=True)


def _all_sum(a):
    return jnp.sum(jnp.sum(a, axis=0, keepdims=True), axis=1, keepdims=True)


def _route_kernel(x_ref, whi_ref, wlo_ref, bias_ref, earlier_ref,
                  ek_ref, rk_ref, gk_ref, cnt_ref, carry_ref):
    @pl.when(pl.program_id(0) == 0)
    def _():
        carry_ref[...] = jnp.zeros_like(carry_ref)

    nt = lambda a, b: lax.dot_general(a, b, (((1,), (1,)), ((), ())),
                                      preferred_element_type=_F32)
    logits = []
    for p0 in range(0, x_ref.shape[0], ROUTE_SUB):
        x = x_ref[p0:p0 + ROUTE_SUB, :]
        x_hi = x.astype(_BF16)
        x_lo = (x - x_hi.astype(_F32)).astype(_BF16)
        logits.append(nt(whi_ref[...], x_hi) + (nt(whi_ref[...], x_lo) + nt(wlo_ref[...], x_hi)))
    for n, sub_logits in enumerate(logits):
        _route_sub_tile(sub_logits, n * ROUTE_SUB, bias_ref, earlier_ref,
                        ek_ref, rk_ref, gk_ref, carry_ref)
    cnt_ref[...] = carry_ref[...]


def _route_sub_tile(logits, p0, bias_ref, earlier_ref, ek_ref, rk_ref, gk_ref, carry_ref):
    ts = ROUTE_SUB
    scores = jax.nn.sigmoid(logits).reshape(N_GROUPS, GROUP_SIZE, ts)
    biased = scores + bias_ref[...]
    neg_inf = jnp.float32(-jnp.inf)
    shape3 = (N_GROUPS, GROUP_SIZE, ts)
    in_grp = lax.broadcasted_iota(jnp.int32, shape3, 1)
    grp = lax.broadcasted_iota(jnp.int32, shape3, 0)
    eid = grp * GROUP_SIZE + in_grp

    m1 = jnp.max(biased, axis=1, keepdims=True)
    first1 = jnp.min(jnp.where(biased == m1, in_grp, GROUP_SIZE), axis=1, keepdims=True)
    m2 = jnp.max(jnp.where(in_grp == first1, neg_inf, biased), axis=1, keepdims=True)
    gscore = m1 + m2

    gid = lax.broadcasted_iota(jnp.int32, (N_GROUPS, 1, ts), 0)
    gsel = jnp.zeros((N_GROUPS, 1, ts), jnp.bool_)
    for _ in range(TOPK_GROUPS):
        m = jnp.max(gscore, axis=0, keepdims=True)
        first = jnp.min(jnp.where(gscore == m, gid, N_GROUPS), axis=0, keepdims=True)
        pick = gid == first
        gsel = jnp.logical_or(gsel, pick)
        gscore = jnp.where(pick, neg_inf, gscore)

    masked = jnp.where(gsel, biased, neg_inf)
    picked_any = jnp.zeros(shape3, jnp.bool_)
    e_k, s_k = [], []
    for _ in range(TOP_K):
        m = _all_max(masked)
        first = _all_min(jnp.where(masked == m, eid, N_EXPERTS))
        pick = eid == first
        picked_any = jnp.logical_or(picked_any, pick)
        masked = jnp.where(pick, neg_inf, masked)
        e_k.append(first)
        s_k.append(_all_sum(jnp.where(pick, scores, 0.0)))
    denom = s_k[0]
    for k in range(1, TOP_K):
        denom = denom + s_k[k]

    sel = picked_any.astype(_F32).reshape(N_EXPERTS, ts)
    rank = _dot(sel.astype(_BF16), earlier_ref[...]) + carry_ref[...]
    rank3 = rank.reshape(shape3)
    carry_ref[...] += jnp.sum(sel, axis=1, keepdims=True)

    for k in range(TOP_K):
        r = _all_sum(jnp.where(eid == e_k[k], rank3, 0.0))
        ek_ref[k:k + 1, p0:p0 + ts] = e_k[k].reshape(1, ts)
        rk_ref[k:k + 1, p0:p0 + ts] = r.reshape(1, ts).astype(jnp.int32)
        gk_ref[k:k + 1, p0:p0 + ts] = (s_k[k] / denom * ROUTED_SCALE).reshape(1, ts)


def _route(x, w_router, router_bias):
    n_part, d = x.shape
    ts = ROUTE_TILE
    steps = n_part // ts
    out_shape = (jax.ShapeDtypeStruct((TOP_K, n_part), jnp.int32),
                 jax.ShapeDtypeStruct((TOP_K, n_part), jnp.int32),
                 jax.ShapeDtypeStruct((TOP_K, n_part), _F32),
                 jax.ShapeDtypeStruct((N_EXPERTS, 1), _F32))
    kspec = pl.BlockSpec((TOP_K, ts), lambda i: (0, i))
    const2 = lambda i: (0, 0)
    w_t = w_router.T
    w_hi = w_t.astype(_BF16)
    w_lo = (w_t - w_hi.astype(_F32)).astype(_BF16)
    pos = jnp.arange(ROUTE_SUB, dtype=jnp.int32)
    earlier = (pos[:, None] < pos[None, :]).astype(_BF16)
    return pl.pallas_call(
        _route_kernel,
        grid=(steps,),
        in_specs=[
            pl.BlockSpec((ts, d), lambda i: (i, 0)),
            pl.BlockSpec((N_EXPERTS, d), const2),
            pl.BlockSpec((N_EXPERTS, d), const2),
            pl.BlockSpec((N_GROUPS, GROUP_SIZE, 1), lambda i: (0, 0, 0)),
            pl.BlockSpec((ROUTE_SUB, ROUTE_SUB), const2),
        ],
        out_specs=(kspec, kspec, kspec, pl.BlockSpec((N_EXPERTS, 1), const2)),
        out_shape=out_shape,
        scratch_shapes=[pltpu.VMEM((N_EXPERTS, 1), _F32)],
        compiler_params=pltpu.CompilerParams(
            dimension_semantics=("arbitrary",),
            vmem_limit_bytes=VMEM_LIMIT),
        name="moe_route",
    )(x, w_hi, w_lo, router_bias.reshape(N_GROUPS, GROUP_SIZE, 1), earlier)


def _sc_mesh():
    return plsc.VectorSubcoreMesh(core_axis_name="core", subcore_axis_name="subcore")


def _sc_scatter_rows(src, idx, n_out):
    d = src.shape[1]
    wins = idx.shape[1] // SC_WINDOW

    def body(src_hbm, idx_hbm, out_hbm, sem):
        def step(src_vmem, idx_vmem):
            copies = [pltpu.make_async_copy(src_vmem, out_hbm.at[idx_vmem.at[k]], sem)
                      for k in range(TOP_K)]
            for copy in copies:
                copy.start()
            for copy in copies:
                copy.wait()

        pltpu.emit_pipeline(
            step,
            grid=(src.shape[0] // SC_WINDOW,),
            in_specs=[pl.BlockSpec((SC_WINDOW, d), index_map=lambda i: (i, 0)),
                      pl.BlockSpec((TOP_K, SC_WINDOW), index_map=lambda i: (i // wins, i % wins))],
            out_specs=[],
            core_axis_name=("core", "subcore"),
            dimension_semantics=(pltpu.PARALLEL,),
        )(src_hbm, idx_hbm)

    return pl.kernel(body, out_type=jax.ShapeDtypeStruct((n_out, d), src.dtype),
                     mesh=_sc_mesh(), scratch_types=[pltpu.SemaphoreType.DMA],
                     name="sc_scatter_rows")(src, idx)


def _sc_gather_sum(table, idx, gates):
    d = table.shape[1]
    n_tok = idx.shape[1]
    wins = n_tok // SC_WINDOW
    lanes = SC_LANES
    out_type = jax.ShapeDtypeStruct((PIECES * n_tok, d), _F32)

    def tree_sum(terms):
        while len(terms) > 1:
            terms = [terms[i] + terms[i + 1] for i in range(0, len(terms), 2)]
        return terms[0]

    def body(table_hbm, idx_hbm, gate_hbm, lo_hbm, hi_hbm, rows_vmem, sem):
        def step(idx_vmem, gate_vmem, lo_vmem, hi_vmem):
            def burst_copies(b):
                return [pltpu.make_async_copy(
                    table_hbm.at[idx_vmem.at[k, pl.ds(b * SC_BURST, SC_BURST)]],
                    rows_vmem.at[b % 2, k], sem.at[b % 2]) for k in range(TOP_K)]

            n_bursts = SC_WINDOW // SC_BURST
            for copy in burst_copies(0):
                copy.start()
            for b in range(n_bursts):
                if b + 1 < n_bursts:
                    for copy in burst_copies(b + 1):
                        copy.start()
                for copy in burst_copies(b):
                    copy.wait()
                t0 = b * SC_BURST
                rows = rows_vmem.at[b % 2]

                @plsc.parallel_loop(0, SC_BURST)
                def _(r):
                    t = t0 + r
                    token = jnp.full((lanes,), t, jnp.int32)
                    gate = [plsc.load_gather(gate_vmem, [jnp.full((lanes,), k, jnp.int32), token])
                            for k in range(TOP_K)]
                    for j in range(0, d, lanes):
                        words = [rows[k, r, pl.ds(j, lanes)] for k in range(TOP_K)]
                        lo_vmem[t, pl.ds(j, lanes)] = tree_sum(
                            [gate[k] * lax.bitcast_convert_type(words[k] << 16, _F32)
                             for k in range(TOP_K)])
                        hi_vmem[t, pl.ds(j, lanes)] = tree_sum(
                            [gate[k] * lax.bitcast_convert_type(words[k] & _U32(0xFFFF0000), _F32)
                             for k in range(TOP_K)])

        window = lambda i: (i // wins, i % wins)
        pltpu.emit_pipeline(
            step,
            grid=(PIECES * wins,),
            in_specs=[pl.BlockSpec((TOP_K, SC_WINDOW), index_map=window),
                      pl.BlockSpec((TOP_K, SC_WINDOW), index_map=lambda i: (0, i % wins))],
            out_specs=[pl.BlockSpec((SC_WINDOW, d), index_map=lambda i: (i, 0)),
                       pl.BlockSpec((SC_WINDOW, d), index_map=lambda i: (i, 0))],
            core_axis_name=("core", "subcore"),
            dimension_semantics=(pltpu.PARALLEL,),
        )(idx_hbm, gate_hbm, lo_hbm, hi_hbm)

    return pl.kernel(body, out_type=(out_type, out_type), mesh=_sc_mesh(),
                     scratch_types=[pltpu.VMEM((2, TOP_K, SC_BURST, d), _U32),
                                    pltpu.SemaphoreType.DMA((2,))],
                     compiler_params=pltpu.CompilerParams(needs_layout_passes=False),
                     name="sc_gather_sum")(table, idx, gates)


def _index_kernel(start_ref, ek_ref, rk_ref, idx_ref, *, n_rows):
    e = ek_ref[...]
    start = jnp.zeros(e.shape, jnp.int32)
    for ex in range(N_EXPERTS):
        start = jnp.where(e == ex, start_ref[ex], start)
    dest = start + rk_ref[...]
    for c in range(PIECES):
        idx_ref[c] = dest + c * n_rows


def _row_indices(padded_start, e_k, r_k, n_rows):
    n_tok = e_k.shape[1]
    ts = INDEX_TILE
    kspec = pl.BlockSpec((TOP_K, ts), lambda i, st: (0, i))
    grid_spec = pltpu.PrefetchScalarGridSpec(
        num_scalar_prefetch=1,
        grid=(n_tok // ts,),
        in_specs=[kspec, kspec],
        out_specs=pl.BlockSpec((PIECES, TOP_K, ts), lambda i, st: (0, 0, i)),
    )
    return pl.pallas_call(
        functools.partial(_index_kernel, n_rows=n_rows),
        grid_spec=grid_spec,
        out_shape=jax.ShapeDtypeStruct((PIECES, TOP_K, n_tok), jnp.int32),
        compiler_params=pltpu.CompilerParams(dimension_semantics=("arbitrary",)),
        name="moe_row_indices",
    )(padded_start, e_k, r_k)


def _expert_kernel(chunk0_ref, nchunk_ref, cnt_ref, total_ref,
                   xs_hbm, wg_ref, wu_ref, wd_ref, o_hbm,
                   xbuf, obuf, sem_in, sem_out):
    e = pl.program_id(0)
    total = total_ref[0]
    ch = EXPERT_ROWS
    ahead = IN_SLOTS - 1

    def in_copy(g):
        slot = g % IN_SLOTS
        return pltpu.make_async_copy(xs_hbm.at[:, pl.ds(g * ch, ch), :], xbuf.at[slot],
                                     sem_in.at[slot])

    def out_copy(g):
        slot = g % OUT_SLOTS
        return pltpu.make_async_copy(obuf.at[slot], o_hbm.at[:, pl.ds(g * ch, ch), :],
                                     sem_out.at[slot])

    @pl.when(e == 0)
    def _():
        obuf[...] = jnp.zeros_like(obuf)
        for g in range(ahead):
            @pl.when(g < total)
            def _():
                in_copy(g).start()


    def chunk(j, carry):
        g = chunk0_ref[e] + j
        in_copy(g).wait()

        @pl.when(g + ahead < total)
        def _():
            in_copy(g + ahead).start()

        @pl.when(g >= OUT_SLOTS)
        def _():
            out_copy(g - OUT_SLOTS).wait()

        islot = g % IN_SLOTS
        oslot = g % OUT_SLOTS
        valid = cnt_ref[e] - j * ch

        def gate_up(s):
            r0 = s * EXPERT_SUB
            w = jnp.concatenate([xbuf[islot, c, r0:r0 + EXPERT_SUB, :] for c in range(PIECES)],
                                axis=1)
            lo, hi = _unpack_halves(w)
            gate = _dot(lo, wg_ref[0, 0, :PACKED, :]) + _dot(hi, wg_ref[0, 0, PACKED:, :])
            up = _dot(lo, wu_ref[0, 0, :PACKED, :]) + _dot(hi, wu_ref[0, 0, PACKED:, :])
            return gate, up

        def down(s, gate_up_pair):
            r0 = s * EXPERT_SUB
            gate, up = gate_up_pair
            y = _pack_halves(_dot(_silu(gate) * up, wd_ref[0, 0]))
            for c in range(PIECES):
                obuf[oslot, c, r0:r0 + EXPERT_SUB, :] = y[:, c * LANES:(c + 1) * LANES]

        n_sub = ch // EXPERT_SUB

        all_subs = valid > ch - EXPERT_SUB

        @pl.when(all_subs)
        def _():
            projected = [gate_up(s) for s in range(n_sub)]
            for s, gu in enumerate(projected):
                down(s, gu)

        @pl.when(jnp.logical_not(all_subs))
        def _():
            for s in range(n_sub - 1):
                @pl.when(s * EXPERT_SUB < valid)
                def _():
                    down(s, gate_up(s))

        out_copy(g).start()
        return carry

    lax.fori_loop(0, nchunk_ref[e], chunk, 0)

    @pl.when(e == pl.num_programs(0) - 1)
    def _():
        for back in range(OUT_SLOTS, 0, -1):
            @pl.when(total >= back)
            def _():
                out_copy(total - back).wait()


def _experts(layer, chunk0, nchunk, counts, total, xs, w_gate, w_up, w_down):
    d = D_MODEL
    ch = EXPERT_ROWS
    wspec_in = pl.BlockSpec((1, 1, d, EXPERT_DIM), lambda e, *_: (layer, e, 0, 0))
    any_spec = pl.BlockSpec(memory_space=pl.ANY)
    grid_spec = pltpu.PrefetchScalarGridSpec(
        num_scalar_prefetch=4,
        grid=(N_EXPERTS,),
        in_specs=[
            any_spec,
            wspec_in,
            wspec_in,
            pl.BlockSpec((1, 1, EXPERT_DIM, d), lambda e, *_: (layer, e, 0, 0)),
        ],
        out_specs=any_spec,
        scratch_shapes=[pltpu.VMEM((IN_SLOTS, PIECES, ch, LANES), _U32),
                        pltpu.VMEM((OUT_SLOTS, PIECES, ch, LANES), _U32),
                        pltpu.SemaphoreType.DMA((IN_SLOTS,)),
                        pltpu.SemaphoreType.DMA((OUT_SLOTS,))],
    )
    return pl.pallas_call(
        _expert_kernel,
        grid_spec=grid_spec,
        out_shape=jax.ShapeDtypeStruct(xs.shape, _U32),
        compiler_params=pltpu.CompilerParams(
            dimension_semantics=("arbitrary",),
            vmem_limit_bytes=VMEM_LIMIT),
        name="moe_experts",
    )(chunk0, nchunk, counts, total, xs, w_gate, w_up, w_down)


def _moe_residual(x, lo_ref, hi_ref, rows, sg_ref, su_ref, sd_ref, g_ref, b_ref):
    xb = x.astype(_BF16)
    routed = jnp.concatenate([lo_ref[c, rows, :] for c in range(PIECES)] +
                             [hi_ref[c, rows, :] for c in range(PIECES)], axis=1)
    h = _silu(_dot(xb, sg_ref[...])) * _dot(xb, su_ref[...])
    shared = _dot(h.astype(_BF16), sd_ref[...])
    return _layer_norm(DEEPNORM_ALPHA * x + (routed + shared), g_ref[...], b_ref[...])


def _combine_kernel(x_ref, lo_ref, hi_ref, sg_ref, su_ref, sd_ref, g_ref, b_ref, *rest):
    o_ref = rest[-1]
    o_ref[...] = _moe_residual(x_ref[...], lo_ref, hi_ref, slice(None),
                               sg_ref, su_ref, sd_ref, g_ref, b_ref)


def _combine_sgu_kernel(x_ref, lo_ref, hi_ref, sg_ref, su_ref, sd_ref, fg_ref, fb_ref,
                        win_ref, bin_ref, lng_ref, lnb_ref, ws_ref, bs_ref, wout_ref,
                        mg_ref, mb_ref, o_ref, op_ref, gated_ref, h_ref):
    for p0 in range(0, x_ref.shape[0], SGU_SUB):
        rows = slice(p0, p0 + SGU_SUB)
        h_ref[rows, :] = _moe_residual(x_ref[rows, :], lo_ref, hi_ref, rows,
                                       sg_ref, su_ref, sd_ref, fg_ref, fb_ref)
    _sgu_kernel(h_ref, win_ref, bin_ref, lng_ref, lnb_ref, ws_ref, bs_ref, wout_ref,
                mg_ref, mb_ref, o_ref, op_ref, gated_ref)


def _combine(x, routed_lo, routed_hi, sh_gate, sh_up, sh_down, ln_g, ln_b, out_parts, part, prev):
    n_part, d = x.shape
    ts = COMBINE_TILE
    steps = n_part // ts
    const2 = lambda i: (0, 0)
    in_specs = [
        pl.BlockSpec((ts, d), lambda i: (i, 0)),
        pl.BlockSpec((PIECES, ts, LANES), lambda i: (0, i, 0)),
        pl.BlockSpec((PIECES, ts, LANES), lambda i: (0, i, 0)),
        pl.BlockSpec((d, EXPERT_DIM), const2),
        pl.BlockSpec((d, EXPERT_DIM), const2),
        pl.BlockSpec((EXPERT_DIM, d), const2),
        pl.BlockSpec((1, d), const2),
        pl.BlockSpec((1, d), const2),
    ]
    args = [x, routed_lo, routed_hi, sh_gate.astype(_BF16), sh_up.astype(_BF16), sh_down.astype(_BF16),
            ln_g.reshape(1, d), ln_b.reshape(1, d)]
    aliases = {}
    if prev is not None:
        in_specs.append(pl.BlockSpec(memory_space=pl.ANY))
        aliases = {len(args): 0}
        args.append(prev)
    return pl.pallas_call(
        _combine_kernel,
        grid=(steps,),
        in_specs=in_specs,
        out_specs=pl.BlockSpec((ts, d), lambda i: (part * steps + i, 0)),
        out_shape=jax.ShapeDtypeStruct((out_parts * n_part, d), _F32),
        input_output_aliases=aliases,
        compiler_params=pltpu.CompilerParams(
            dimension_semantics=("arbitrary",),
            vmem_limit_bytes=VMEM_LIMIT),
        name="moe_combine",
    )(*args)


def _combine_sgu(x, routed_lo, routed_hi, sh_gate, sh_up, sh_down, ffn_g, ffn_b,
                 w_in, b_in, ln_g, ln_b, w_s, b_s, w_out, mix_g, mix_b):
    n_part, d = x.shape
    ts = SGU_TILE
    const2 = lambda i: (0, 0)
    const3 = lambda i: (0, 0, 0)
    causal = jnp.tril(jnp.ones((SGU_CHUNK, SGU_CHUNK), w_s.dtype))
    ws = (w_s * causal[None]).astype(_BF16)
    tile = pl.BlockSpec((ts, d), lambda i: (i, 0))
    ptile = pl.BlockSpec((PIECES, ts, LANES), lambda i: (0, i, 0))
    row = lambda n: pl.BlockSpec((1, n), const2)
    return pl.pallas_call(
        _combine_sgu_kernel,
        grid=(n_part // ts,),
        in_specs=[
            tile, ptile, ptile,
            pl.BlockSpec((d, EXPERT_DIM), const2),
            pl.BlockSpec((d, EXPERT_DIM), const2),
            pl.BlockSpec((EXPERT_DIM, d), const2),
            row(d), row(d),
            pl.BlockSpec((d, 2 * SGU_WIDTH), const2),
            row(2 * SGU_WIDTH), row(SGU_WIDTH), row(SGU_WIDTH),
            pl.BlockSpec((SGU_HEADS, SGU_CHUNK, SGU_CHUNK), const3),
            pl.BlockSpec((SGU_HEADS, SGU_CHUNK, 1), const3),
            pl.BlockSpec((SGU_WIDTH, d), const2),
            row(d), row(d),
        ],
        out_specs=(tile, ptile),
        out_shape=(jax.ShapeDtypeStruct((n_part, d), _F32),
                   jax.ShapeDtypeStruct((PIECES, n_part, LANES), _U32)),
        scratch_shapes=[pltpu.VMEM((ts, SGU_WIDTH), _BF16), pltpu.VMEM((ts, d), _F32)],
        compiler_params=pltpu.CompilerParams(
            dimension_semantics=("arbitrary",),
            vmem_limit_bytes=VMEM_LIMIT),
        name="moe_combine_sgu_mixer",
    )(x, routed_lo, routed_hi, sh_gate.astype(_BF16), sh_up.astype(_BF16), sh_down.astype(_BF16),
      ffn_g.reshape(1, d), ffn_b.reshape(1, d),
      w_in.astype(_BF16), b_in.reshape(1, -1), ln_g.reshape(1, -1), ln_b.reshape(1, -1),
      ws, b_s.reshape(SGU_HEADS, SGU_CHUNK, 1), w_out.astype(_BF16),
      mix_g.reshape(1, d), mix_b.reshape(1, d))


def _moe_dispatch(layer, x, x_packed, w_router, router_bias, w_gate, w_up, w_down):
    n_part, d = x.shape
    bm = EXPERT_ROWS
    n_rows = n_part * TOP_K + N_EXPERTS * bm
    e_k, r_k, g_k, counts = _route(x, w_router, router_bias)
    counts = counts.reshape(N_EXPERTS).astype(jnp.int32)
    padded = (counts + bm - 1) // bm * bm
    padded_end = jnp.cumsum(padded)
    padded_start = padded_end - padded
    idx = _row_indices(padded_start, e_k, r_k, n_rows).reshape(PIECES * TOP_K, n_part)
    xs = _sc_scatter_rows(x_packed.reshape(PIECES * n_part, LANES), idx, PIECES * n_rows)
    rows = _experts(layer, padded_start // bm, padded // bm, counts, padded_end[-1:] // bm,
                    xs.reshape(PIECES, n_rows, LANES), w_gate, w_up, w_down)
    return rows.reshape(PIECES * n_rows, LANES), idx, g_k


def _routed_sums(x, rows, idx, gates):
    n_part = x.shape[0]
    lo, hi = _sc_gather_sum(rows, idx, gates)
    return lo.reshape(PIECES, n_part, LANES), hi.reshape(PIECES, n_part, LANES)


def kernel(x, pool_w_in, pool_w_grp, pool_scale, pool_w_out, sgu_w_in, sgu_b_in, sgu_ln_g, sgu_ln_b, sgu_w_s, sgu_b_s, sgu_w_out, ln_mix_g, ln_mix_b, moe_w_router, moe_router_bias, moe_w_gate, moe_w_up, moe_w_down, moe_sh_gate, moe_sh_up, moe_sh_down, ln_ffn_g, ln_ffn_b):
    bsz, seq, d = x.shape
    parts = range(TOKEN_PARTS)

    def dispatch(i, h, h_packed):
        return _moe_dispatch(i, h, h_packed, moe_w_router[i], moe_router_bias[i],
                             moe_w_gate, moe_w_up, moe_w_down)

    mixed = [_pool_layer(x, p, pool_w_in[0], pool_w_grp[0], pool_scale[0], pool_w_out[0],
                         ln_mix_g[0], ln_mix_b[0]) for p in parts]
    mixed = [(h.reshape(-1, d), h_packed) for h, h_packed in mixed]
    routed = [dispatch(0, h, h_packed) for h, h_packed in mixed]
    mixed = [_combine_sgu(mixed[p][0], *_routed_sums(mixed[p][0], *routed[p]),
                          moe_sh_gate[0], moe_sh_up[0], moe_sh_down[0], ln_ffn_g[0], ln_ffn_b[0],
                          sgu_w_in[0], sgu_b_in[0], sgu_ln_g[0], sgu_ln_b[0], sgu_w_s[0],
                          sgu_b_s[0], sgu_w_out[0], ln_mix_g[1], ln_mix_b[1]) for p in parts]
    routed = [dispatch(1, h, h_packed) for h, h_packed in mixed]
    out = None
    for p in parts:
        h = mixed[p][0]
        out = _combine(h, *_routed_sums(h, *routed[p]), moe_sh_gate[1], moe_sh_up[1],
                       moe_sh_down[1], ln_ffn_g[1], ln_ffn_b[1], TOKEN_PARTS, p, out)
    return out.reshape(bsz, seq, d)
```

```python
import functools

import jax
import jax.numpy as jnp
from jax import lax
from jax.experimental import pallas as pl
from jax.experimental.pallas import tpu as pltpu
from jax.experimental.pallas import tpu_sc as plsc

D_MODEL = 1024
DEPTH = 2
POOL_WINDOWS = (2, 4, 8, 16)
POOL_GROUP_DIM = D_MODEL // len(POOL_WINDOWS)
POOL_HALO = 16
SGU_CHUNK = 128
SGU_HEADS = 4
SGU_WIDTH = 2 * D_MODEL
SGU_HEAD_DIM = SGU_WIDTH // SGU_HEADS
N_EXPERTS = 64
TOP_K = 8
N_GROUPS = 8
GROUP_SIZE = N_EXPERTS // N_GROUPS
TOPK_GROUPS = 4
EXPERT_DIM = D_MODEL // 4
ROUTED_SCALE = 2.5
DEEPNORM_ALPHA = (2 * DEPTH) ** 0.25
LN_EPS = 1e-5

LANES = 128
PACKED = D_MODEL // 2
PIECES = PACKED // LANES

POOL_TILE = 1024
POOL_SUB = 256
SGU_TILE = 512
SGU_SUB = 256
ROUTE_TILE = 1024
ROUTE_SUB = 512
INDEX_TILE = 2048
EXPERT_ROWS = 512
EXPERT_SUB = 256
IN_SLOTS = 6
OUT_SLOTS = 4
COMBINE_TILE = 1024
SC_WINDOW = 128
SC_LANES = 16
SC_BURST = 16
TOKEN_PARTS = 2
V7X_VMEM_BYTES = 64 * 1024 * 1024
VMEM_LIMIT = V7X_VMEM_BYTES * 7 // 8

_F32 = jnp.float32
_BF16 = jnp.bfloat16
_U32 = jnp.uint32


def _dot(a, b):
    return jnp.dot(a, b, preferred_element_type=_F32)


def _layer_norm(h, g, b):
    mu = jnp.mean(h, axis=-1, keepdims=True)
    hc = h - mu
    var = jnp.mean(hc * hc, axis=-1, keepdims=True)
    return hc * lax.rsqrt(var + LN_EPS) * g + b


def _silu(x):
    return x * jax.nn.sigmoid(x)


def _gelu_tanh(x):
    c = 0.7978845608028654
    return 0.5 * x * (1.0 + jnp.tanh(c * (x + 0.044715 * (x * x * x))))


def _pack_halves(v):
    half = v.shape[1] // 2
    lo = lax.bitcast_convert_type(v[:, :half].astype(_BF16).astype(_F32), _U32)
    hi = lax.bitcast_convert_type(v[:, half:].astype(_BF16).astype(_F32), _U32)
    return hi | (lo >> 16)


def _unpack_halves(w):
    lo = lax.bitcast_convert_type(w << 16, _F32)
    hi = lax.bitcast_convert_type(w & _U32(0xFFFF0000), _F32)
    return lo, hi


def _store_pieces(ref, r0, w):
    for c in range(PIECES):
        ref[c, r0:r0 + w.shape[0], :] = w[:, c * LANES:(c + 1) * LANES]


def _pool_kernel(x_ref, win_ref, wgrp_ref, scale_ref, wout_ref, g_ref, b_ref,
                 o_ref, op_ref, zs_ref, y_ref):
    s = pl.program_id(1)
    ts = x_ref.shape[1]

    @pl.when(s == 0)
    def _():
        zs_ref[0:POOL_HALO, :] = jnp.zeros((POOL_HALO, D_MODEL), _F32)

    for p0 in range(0, ts, POOL_SUB):
        z = _dot(x_ref[0, p0:p0 + POOL_SUB, :].astype(_BF16), win_ref[...])
        zs_ref[POOL_HALO + p0:POOL_HALO + p0 + POOL_SUB, :] = z
    for p0 in range(0, ts, POOL_SUB):
        x = x_ref[0, p0:p0 + POOL_SUB, :]
        base = POOL_HALO + p0
        pos = s * ts + p0 + lax.broadcasted_iota(jnp.int32, (POOL_SUB, 1), 0)
        for g, w in enumerate(POOL_WINDOWS):
            c0 = g * POOL_GROUP_DIM
            c1 = c0 + POOL_GROUP_DIM
            zg = zs_ref[base:base + POOL_SUB, c0:c1]
            ext = zs_ref[base - POOL_HALO:base + POOL_SUB, c0:c1]
            m = 1
            while m < w:
                ext = ext + pltpu.roll(ext, m, 0)
                m *= 2
            acc = ext[POOL_HALO:, :]
            cnt = jnp.minimum(pos + 1, w).astype(_F32)
            pooled = acc / cnt - zg
            yg = _dot(pooled.astype(_BF16), wgrp_ref[g]) * scale_ref[:, c0:c1]
            y_ref[p0:p0 + POOL_SUB, c0:c1] = yg.astype(_BF16)
        mix = _dot(y_ref[p0:p0 + POOL_SUB, :], wout_ref[...])
        out = _layer_norm(DEEPNORM_ALPHA * x + mix, g_ref[...], b_ref[...])
        o_ref[0, p0:p0 + POOL_SUB, :] = out
        _store_pieces(op_ref, p0, _pack_halves(out))
    zs_ref[0:POOL_HALO, :] = zs_ref[ts:ts + POOL_HALO, :]


def _pool_layer(x, part, w_in, w_grp, scale, w_out, ln_g, ln_b):
    bsz, seq, d = x.shape
    rows = bsz // TOKEN_PARTS
    ts = POOL_TILE
    steps = seq // ts
    const2 = lambda b, s: (0, 0)
    out_shape = (jax.ShapeDtypeStruct((rows, seq, d), _F32),
                 jax.ShapeDtypeStruct((PIECES, rows * seq, LANES), _U32))
    tile = pl.BlockSpec((1, ts, d), lambda b, s: (b, s, 0))
    ptile = pl.BlockSpec((PIECES, ts, LANES), lambda b, s: (0, b * steps + s, 0))
    return pl.pallas_call(
        _pool_kernel,
        grid=(rows, steps),
        in_specs=[
            pl.BlockSpec((1, ts, d), lambda b, s: (part * rows + b, s, 0)),
            pl.BlockSpec((d, d), const2),
            pl.BlockSpec((len(POOL_WINDOWS), POOL_GROUP_DIM, POOL_GROUP_DIM), lambda b, s: (0, 0, 0)),
            pl.BlockSpec((1, d), const2),
            pl.BlockSpec((d, d), const2),
            pl.BlockSpec((1, d), const2),
            pl.BlockSpec((1, d), const2),
        ],
        out_specs=(tile, ptile),
        out_shape=out_shape,
        scratch_shapes=[pltpu.VMEM((POOL_HALO + ts, d), _F32),
                        pltpu.VMEM((ts, d), _BF16)],
        compiler_params=pltpu.CompilerParams(
            dimension_semantics=("arbitrary", "arbitrary"),
            vmem_limit_bytes=VMEM_LIMIT),
        name="pool_mixer",
    )(x, w_in.astype(_BF16), w_grp.astype(_BF16), scale.reshape(1, d),
      w_out.astype(_BF16), ln_g.reshape(1, d), ln_b.reshape(1, d))


def _sgu_kernel(x_ref, win_ref, bin_ref, lng_ref, lnb_ref, ws_ref, bs_ref, wout_ref,
                g_ref, b_ref, o_ref, op_ref, gated_ref):
    ts = x_ref.shape[0]
    projected = []
    for p0 in range(0, ts, SGU_SUB):
        x = x_ref[p0:p0 + SGU_SUB, :]
        xb = x.astype(_BF16)
        v = _gelu_tanh(_dot(xb, win_ref[:, SGU_WIDTH:]) + bin_ref[:, SGU_WIDTH:])
        v = _layer_norm(v, lng_ref[...], lnb_ref[...]).astype(_BF16)
        u = _gelu_tanh(_dot(xb, win_ref[:, :SGU_WIDTH]) + bin_ref[:, :SGU_WIDTH])
        projected.append((p0, x, u, v))
    for p0, x, u, v in projected:
        for r0 in range(0, SGU_SUB, SGU_CHUNK):
            for h in range(SGU_HEADS):
                c0 = h * SGU_HEAD_DIM
                sv = _dot(ws_ref[h], v[r0:r0 + SGU_CHUNK, c0:c0 + SGU_HEAD_DIM]) + bs_ref[h]
                gated_ref[p0 + r0:p0 + r0 + SGU_CHUNK, c0:c0 + SGU_HEAD_DIM] = (
                    u[r0:r0 + SGU_CHUNK, c0:c0 + SGU_HEAD_DIM] * sv).astype(_BF16)
        mix = _dot(gated_ref[p0:p0 + SGU_SUB, :], wout_ref[...])
        out = _layer_norm(DEEPNORM_ALPHA * x + mix, g_ref[...], b_ref[...])
        o_ref[p0:p0 + SGU_SUB, :] = out
        _store_pieces(op_ref, p0, _pack_halves(out))


def _all_max(a):
    return jnp.max(jnp.max(a, axis=0, keepdims=True), axis=1, keepdims=True)


def _all_min(a):
    return jnp.min(jnp.min(a, axis=0, keepdims=True), axis=1, keepdims=True)


def _all_sum(a):
    return jnp.sum(jnp.sum(a, axis=0, keepdims=True), axis=1, keepdims=True)


def _route_kernel(x_ref, whi_ref, wlo_ref, bias_ref, earlier_ref,
                  ek_ref, rk_ref, gk_ref, cnt_ref, carry_ref):
    @pl.when(pl.program_id(0) == 0)
    def _():
        carry_ref[...] = jnp.zeros_like(carry_ref)

    nt = lambda a, b: lax.dot_general(a, b, (((1,), (1,)), ((), ())),
                                      preferred_element_type=_F32)
    logits = []
    for p0 in range(0, x_ref.shape[0], ROUTE_SUB):
        x = x_ref[p0:p0 + ROUTE_SUB, :]
        x_hi = x.astype(_BF16)
        x_lo = (x - x_hi.astype(_F32)).astype(_BF16)
        logits.append(nt(whi_ref[...], x_hi) + (nt(whi_ref[...], x_lo) + nt(wlo_ref[...], x_hi)))
    for n, sub_logits in enumerate(logits):
        _route_sub_tile(sub_logits, n * ROUTE_SUB, bias_ref, earlier_ref,
                        ek_ref, rk_ref, gk_ref, carry_ref)
    cnt_ref[...] = carry_ref[...]


def _route_sub_tile(logits, p0, bias_ref, earlier_ref, ek_ref, rk_ref, gk_ref, carry_ref):
    ts = ROUTE_SUB
    scores = jax.nn.sigmoid(logits).reshape(N_GROUPS, GROUP_SIZE, ts)
    biased = scores + bias_ref[...]
    neg_inf = jnp.float32(-jnp.inf)
    shape3 = (N_GROUPS, GROUP_SIZE, ts)
    in_grp = lax.broadcasted_iota(jnp.int32, shape3, 1)
    grp = lax.broadcasted_iota(jnp.int32, shape3, 0)
    eid = grp * GROUP_SIZE + in_grp

    m1 = jnp.max(biased, axis=1, keepdims=True)
    first1 = jnp.min(jnp.where(biased == m1, in_grp, GROUP_SIZE), axis=1, keepdims=True)
    m2 = jnp.max(jnp.where(in_grp == first1, neg_inf, biased), axis=1, keepdims=True)
    gscore = m1 + m2

    gid = lax.broadcasted_iota(jnp.int32, (N_GROUPS, 1, ts), 0)
    gsel = jnp.zeros((N_GROUPS, 1, ts), jnp.bool_)
    for _ in range(TOPK_GROUPS):
        m = jnp.max(gscore, axis=0, keepdims=True)
        first = jnp.min(jnp.where(gscore == m, gid, N_GROUPS), axis=0, keepdims=True)
        pick = gid == first
        gsel = jnp.logical_or(gsel, pick)
        gscore = jnp.where(pick, neg_inf, gscore)

    masked = jnp.where(gsel, biased, neg_inf)
    picked_any = jnp.zeros(shape3, jnp.bool_)
    e_k, s_k = [], []
    for _ in range(TOP_K):
        m = _all_max(masked)
        first = _all_min(jnp.where(masked == m, eid, N_EXPERTS))
        pick = eid == first
        picked_any = jnp.logical_or(picked_any, pick)
        masked = jnp.where(pick, neg_inf, masked)
        e_k.append(first)
        s_k.append(_all_sum(jnp.where(pick, scores, 0.0)))
    denom = s_k[0]
    for k in range(1, TOP_K):
        denom = denom + s_k[k]

    sel = picked_any.astype(_F32).reshape(N_EXPERTS, ts)
    rank = _dot(sel.astype(_BF16), earlier_ref[...]) + carry_ref[...]
    rank3 = rank.reshape(shape3)
    carry_ref[...] += jnp.sum(sel, axis=1, keepdims=True)

    for k in range(TOP_K):
        r = _all_sum(jnp.where(eid == e_k[k], rank3, 0.0))
        ek_ref[k:k + 1, p0:p0 + ts] = e_k[k].reshape(1, ts)
        rk_ref[k:k + 1, p0:p0 + ts] = r.reshape(1, ts).astype(jnp.int32)
        gk_ref[k:k + 1, p0:p0 + ts] = (s_k[k] / denom * ROUTED_SCALE).reshape(1, ts)


def _route(x, w_router, router_bias):
    n_part, d = x.shape
    ts = ROUTE_TILE
    steps = n_part // ts
    out_shape = (jax.ShapeDtypeStruct((TOP_K, n_part), jnp.int32),
                 jax.ShapeDtypeStruct((TOP_K, n_part), jnp.int32),
                 jax.ShapeDtypeStruct((TOP_K, n_part), _F32),
                 jax.ShapeDtypeStruct((N_EXPERTS, 1), _F32))
    kspec = pl.BlockSpec((TOP_K, ts), lambda i: (0, i))
    const2 = lambda i: (0, 0)
    w_t = w_router.T
    w_hi = w_t.astype(_BF16)
    w_lo = (w_t - w_hi.astype(_F32)).astype(_BF16)
    pos = jnp.arange(ROUTE_SUB, dtype=jnp.int32)
    earlier = (pos[:, None] < pos[None, :]).astype(_BF16)
    return pl.pallas_call(
        _route_kernel,
        grid=(steps,),
        in_specs=[
            pl.BlockSpec((ts, d), lambda i: (i, 0)),
            pl.BlockSpec((N_EXPERTS, d), const2),
            pl.BlockSpec((N_EXPERTS, d), const2),
            pl.BlockSpec((N_GROUPS, GROUP_SIZE, 1), lambda i: (0, 0, 0)),
            pl.BlockSpec((ROUTE_SUB, ROUTE_SUB), const2),
        ],
        out_specs=(kspec, kspec, kspec, pl.BlockSpec((N_EXPERTS, 1), const2)),
        out_shape=out_shape,
        scratch_shapes=[pltpu.VMEM((N_EXPERTS, 1), _F32)],
        compiler_params=pltpu.CompilerParams(
            dimension_semantics=("arbitrary",),
            vmem_limit_bytes=VMEM_LIMIT),
        name="moe_route",
    )(x, w_hi, w_lo, router_bias.reshape(N_GROUPS, GROUP_SIZE, 1), earlier)


def _sc_mesh():
    return plsc.VectorSubcoreMesh(core_axis_name="core", subcore_axis_name="subcore")


def _sc_scatter_rows(src, idx, n_out):
    d = src.shape[1]
    wins = idx.shape[1] // SC_WINDOW

    def body(src_hbm, idx_hbm, out_hbm, sem):
        def step(src_vmem, idx_vmem):
            copies = [pltpu.make_async_copy(src_vmem, out_hbm.at[idx_vmem.at[k]], sem)
                      for k in range(TOP_K)]
            for copy in copies:
                copy.start()
            for copy in copies:
                copy.wait()

        pltpu.emit_pipeline(
            step,
            grid=(src.shape[0] // SC_WINDOW,),
            in_specs=[pl.BlockSpec((SC_WINDOW, d), index_map=lambda i: (i, 0)),
                      pl.BlockSpec((TOP_K, SC_WINDOW), index_map=lambda i: (i // wins, i % wins))],
            out_specs=[],
            core_axis_name=("core", "subcore"),
            dimension_semantics=(pltpu.PARALLEL,),
        )(src_hbm, idx_hbm)

    return pl.kernel(body, out_type=jax.ShapeDtypeStruct((n_out, d), src.dtype),
                     mesh=_sc_mesh(), scratch_types=[pltpu.SemaphoreType.DMA],
                     name="sc_scatter_rows")(src, idx)


def _sc_gather_sum(table, idx, gates):
    d = table.shape[1]
    n_tok = idx.shape[1]
    wins = n_tok // SC_WINDOW
    lanes = SC_LANES
    out_type = jax.ShapeDtypeStruct((PIECES * n_tok, d), _F32)

    def tree_sum(terms):
        while len(terms) > 1:
            terms = [terms[i] + terms[i + 1] for i in range(0, len(terms), 2)]
        return terms[0]

    def body(table_hbm, idx_hbm, gate_hbm, lo_hbm, hi_hbm, rows_vmem, sem):
        def step(idx_vmem, gate_vmem, lo_vmem, hi_vmem):
            def burst_copies(b):
                return [pltpu.make_async_copy(
                    table_hbm.at[idx_vmem.at[k, pl.ds(b * SC_BURST, SC_BURST)]],
                    rows_vmem.at[b % 2, k], sem.at[b % 2]) for k in range(TOP_K)]

            n_bursts = SC_WINDOW // SC_BURST
            for copy in burst_copies(0):
                copy.start()
            for b in range(n_bursts):
                if b + 1 < n_bursts:
                    for copy in burst_copies(b + 1):
                        copy.start()
                for copy in burst_copies(b):
                    copy.wait()
                t0 = b * SC_BURST
                rows = rows_vmem.at[b % 2]

                @plsc.parallel_loop(0, SC_BURST)
                def _(r):
                    t = t0 + r
                    token = jnp.full((lanes,), t, jnp.int32)
                    gate = [plsc.load_gather(gate_vmem, [jnp.full((lanes,), k, jnp.int32), token])
                            for k in range(TOP_K)]
                    for j in range(0, d, lanes):
                        words = [rows[k, r, pl.ds(j, lanes)] for k in range(TOP_K)]
                        lo_vmem[t, pl.ds(j, lanes)] = tree_sum(
                            [gate[k] * lax.bitcast_convert_type(words[k] << 16, _F32)
                             for k in range(TOP_K)])
                        hi_vmem[t, pl.ds(j, lanes)] = tree_sum(
                            [gate[k] * lax.bitcast_convert_type(words[k] & _U32(0xFFFF0000), _F32)
                             for k in range(TOP_K)])

        window = lambda i: (i // wins, i % wins)
        pltpu.emit_pipeline(
            step,
            grid=(PIECES * wins,),
            in_specs=[pl.BlockSpec((TOP_K, SC_WINDOW), index_map=window),
                      pl.BlockSpec((TOP_K, SC_WINDOW), index_map=lambda i: (0, i % wins))],
            out_specs=[pl.BlockSpec((SC_WINDOW, d), index_map=lambda i: (i, 0)),
                       pl.BlockSpec((SC_WINDOW, d), index_map=lambda i: (i, 0))],
            core_axis_name=("core", "subcore"),
            dimension_semantics=(pltpu.PARALLEL,),
        )(idx_hbm, gate_hbm, lo_hbm, hi_hbm)

    return pl.kernel(body, out_type=(out_type, out_type), mesh=_sc_mesh(),
                     scratch_types=[pltpu.VMEM((2, TOP_K, SC_BURST, d), _U32),
                                    pltpu.SemaphoreType.DMA((2,))],
                     compiler_params=pltpu.CompilerParams(needs_layout_passes=False),
                     name="sc_gather_sum")(table, idx, gates)


def _index_kernel(start_ref, ek_ref, rk_ref, idx_ref, *, n_rows):
    e = ek_ref[...]
    start = jnp.zeros(e.shape, jnp.int32)
    for ex in range(N_EXPERTS):
        start = jnp.where(e == ex, start_ref[ex], start)
    dest = start + rk_ref[...]
    for c in range(PIECES):
        idx_ref[c] = dest + c * n_rows


def _row_indices(padded_start, e_k, r_k, n_rows):
    n_tok = e_k.shape[1]
    ts = INDEX_TILE
    kspec = pl.BlockSpec((TOP_K, ts), lambda i, st: (0, i))
    grid_spec = pltpu.PrefetchScalarGridSpec(
        num_scalar_prefetch=1,
        grid=(n_tok // ts,),
        in_specs=[kspec, kspec],
        out_specs=pl.BlockSpec((PIECES, TOP_K, ts), lambda i, st: (0, 0, i)),
    )
    return pl.pallas_call(
        functools.partial(_index_kernel, n_rows=n_rows),
        grid_spec=grid_spec,
        out_shape=jax.ShapeDtypeStruct((PIECES, TOP_K, n_tok), jnp.int32),
        compiler_params=pltpu.CompilerParams(dimension_semantics=("arbitrary",)),
        name="moe_row_indices",
    )(padded_start, e_k, r_k)


def _expert_kernel(chunk0_ref, nchunk_ref, cnt_ref, total_ref,
                   xs_hbm, wg_ref, wu_ref, wd_ref, o_hbm,
                   xbuf, obuf, sem_in, sem_out):
    e = pl.program_id(0)
    total = total_ref[0]
    ch = EXPERT_ROWS
    ahead = IN_SLOTS - 1

    def in_copy(g):
        slot = g % IN_SLOTS
        return pltpu.make_async_copy(xs_hbm.at[:, pl.ds(g * ch, ch), :], xbuf.at[slot],
                                     sem_in.at[slot])

    def out_copy(g):
        slot = g % OUT_SLOTS
        return pltpu.make_async_copy(obuf.at[slot], o_hbm.at[:, pl.ds(g * ch, ch), :],
                                     sem_out.at[slot])

    @pl.when(e == 0)
    def _():
        obuf[...] = jnp.zeros_like(obuf)
        for g in range(ahead):
            @pl.when(g < total)
            def _():
                in_copy(g).start()


    def chunk(j, carry):
        g = chunk0_ref[e] + j
        in_copy(g).wait()

        @pl.when(g + ahead < total)
        def _():
            in_copy(g + ahead).start()

        @pl.when(g >= OUT_SLOTS)
        def _():
            out_copy(g - OUT_SLOTS).wait()

        islot = g % IN_SLOTS
        oslot = g % OUT_SLOTS
        valid = cnt_ref[e] - j * ch

        def gate_up(s):
            r0 = s * EXPERT_SUB
            w = jnp.concatenate([xbuf[islot, c, r0:r0 + EXPERT_SUB, :] for c in range(PIECES)],
                                axis=1)
            lo, hi = _unpack_halves(w)
            gate = _dot(lo, wg_ref[0, 0, :PACKED, :]) + _dot(hi, wg_ref[0, 0, PACKED:, :])
            up = _dot(lo, wu_ref[0, 0, :PACKED, :]) + _dot(hi, wu_ref[0, 0, PACKED:, :])
            return gate, up

        def down(s, gate_up_pair):
            r0 = s * EXPERT_SUB
            gate, up = gate_up_pair
            y = _pack_halves(_dot(_silu(gate) * up, wd_ref[0, 0]))
            for c in range(PIECES):
                obuf[oslot, c, r0:r0 + EXPERT_SUB, :] = y[:, c * LANES:(c + 1) * LANES]

        n_sub = ch // EXPERT_SUB

        all_subs = valid > ch - EXPERT_SUB

        @pl.when(all_subs)
        def _():
            projected = [gate_up(s) for s in range(n_sub)]
            for s, gu in enumerate(projected):
                down(s, gu)

        @pl.when(jnp.logical_not(all_subs))
        def _():
            for s in range(n_sub - 1):
                @pl.when(s * EXPERT_SUB < valid)
                def _():
                    down(s, gate_up(s))

        out_copy(g).start()
        return carry

    lax.fori_loop(0, nchunk_ref[e], chunk, 0)

    @pl.when(e == pl.num_programs(0) - 1)
    def _():
        for back in range(OUT_SLOTS, 0, -1):
            @pl.when(total >= back)
            def _():
                out_copy(total - back).wait()


def _experts(layer, chunk0, nchunk, counts, total, xs, w_gate, w_up, w_down):
    d = D_MODEL
    ch = EXPERT_ROWS
    wspec_in = pl.BlockSpec((1, 1, d, EXPERT_DIM), lambda e, *_: (layer, e, 0, 0))
    any_spec = pl.BlockSpec(memory_space=pl.ANY)
    grid_spec = pltpu.PrefetchScalarGridSpec(
        num_scalar_prefetch=4,
        grid=(N_EXPERTS,),
        in_specs=[
            any_spec,
            wspec_in,
            wspec_in,
            pl.BlockSpec((1, 1, EXPERT_DIM, d), lambda e, *_: (layer, e, 0, 0)),
        ],
        out_specs=any_spec,
        scratch_shapes=[pltpu.VMEM((IN_SLOTS, PIECES, ch, LANES), _U32),
                        pltpu.VMEM((OUT_SLOTS, PIECES, ch, LANES), _U32),
                        pltpu.SemaphoreType.DMA((IN_SLOTS,)),
                        pltpu.SemaphoreType.DMA((OUT_SLOTS,))],
    )
    return pl.pallas_call(
        _expert_kernel,
        grid_spec=grid_spec,
        out_shape=jax.ShapeDtypeStruct(xs.shape, _U32),
        compiler_params=pltpu.CompilerParams(
            dimension_semantics=("arbitrary",),
            vmem_limit_bytes=VMEM_LIMIT),
        name="moe_experts",
    )(chunk0, nchunk, counts, total, xs, w_gate, w_up, w_down)


def _moe_residual(x, lo_ref, hi_ref, rows, sg_ref, su_ref, sd_ref, g_ref, b_ref):
    xb = x.astype(_BF16)
    routed = jnp.concatenate([lo_ref[c, rows, :] for c in range(PIECES)] +
                             [hi_ref[c, rows, :] for c in range(PIECES)], axis=1)
    h = _silu(_dot(xb, sg_ref[...])) * _dot(xb, su_ref[...])
    shared = _dot(h.astype(_BF16), sd_ref[...])
    return _layer_norm(DEEPNORM_ALPHA * x + (routed + shared), g_ref[...], b_ref[...])


def _combine_kernel(x_ref, lo_ref, hi_ref, sg_ref, su_ref, sd_ref, g_ref, b_ref, *rest):
    o_ref = rest[-1]
    o_ref[...] = _moe_residual(x_ref[...], lo_ref, hi_ref, slice(None),
                               sg_ref, su_ref, sd_ref, g_ref, b_ref)


def _combine_sgu_kernel(x_ref, lo_ref, hi_ref, sg_ref, su_ref, sd_ref, fg_ref, fb_ref,
                        win_ref, bin_ref, lng_ref, lnb_ref, ws_ref, bs_ref, wout_ref,
                        mg_ref, mb_ref, o_ref, op_ref, gated_ref, h_ref):
    for p0 in range(0, x_ref.shape[0], SGU_SUB):
        rows = slice(p0, p0 + SGU_SUB)
        h_ref[rows, :] = _moe_residual(x_ref[rows, :], lo_ref, hi_ref, rows,
                                       sg_ref, su_ref, sd_ref, fg_ref, fb_ref)
    _sgu_kernel(h_ref, win_ref, bin_ref, lng_ref, lnb_ref, ws_ref, bs_ref, wout_ref,
                mg_ref, mb_ref, o_ref, op_ref, gated_ref)


def _combine(x, routed_lo, routed_hi, sh_gate, sh_up, sh_down, ln_g, ln_b, out_parts, part, prev):
    n_part, d = x.shape
    ts = COMBINE_TILE
    steps = n_part // ts
    const2 = lambda i: (0, 0)
    in_specs = [
        pl.BlockSpec((ts, d), lambda i: (i, 0)),
        pl.BlockSpec((PIECES, ts, LANES), lambda i: (0, i, 0)),
        pl.BlockSpec((PIECES, ts, LANES), lambda i: (0, i, 0)),
        pl.BlockSpec((d, EXPERT_DIM), const2),
        pl.BlockSpec((d, EXPERT_DIM), const2),
        pl.BlockSpec((EXPERT_DIM, d), const2),
        pl.BlockSpec((1, d), const2),
        pl.BlockSpec((1, d), const2),
    ]
    args = [x, routed_lo, routed_hi, sh_gate.astype(_BF16), sh_up.astype(_BF16), sh_down.astype(_BF16),
            ln_g.reshape(1, d), ln_b.reshape(1, d)]
    aliases = {}
    if prev is not None:
        in_specs.append(pl.BlockSpec(memory_space=pl.ANY))
        aliases = {len(args): 0}
        args.append(prev)
    return pl.pallas_call(
        _combine_kernel,
        grid=(steps,),
        in_specs=in_specs,
        out_specs=pl.BlockSpec((ts, d), lambda i: (part * steps + i, 0)),
        out_shape=jax.ShapeDtypeStruct((out_parts * n_part, d), _F32),
        input_output_aliases=aliases,
        compiler_params=pltpu.CompilerParams(
            dimension_semantics=("arbitrary",),
            vmem_limit_bytes=VMEM_LIMIT),
        name="moe_combine",
    )(*args)


def _combine_sgu(x, routed_lo, routed_hi, sh_gate, sh_up, sh_down, ffn_g, ffn_b,
                 w_in, b_in, ln_g, ln_b, w_s, b_s, w_out, mix_g, mix_b):
    n_part, d = x.shape
    ts = SGU_TILE
    const2 = lambda i: (0, 0)
    const3 = lambda i: (0, 0, 0)
    causal = jnp.tril(jnp.ones((SGU_CHUNK, SGU_CHUNK), w_s.dtype))
    ws = (w_s * causal[None]).astype(_BF16)
    tile = pl.BlockSpec((ts, d), lambda i: (i, 0))
    ptile = pl.BlockSpec((PIECES, ts, LANES), lambda i: (0, i, 0))
    row = lambda n: pl.BlockSpec((1, n), const2)
    return pl.pallas_call(
        _combine_sgu_kernel,
        grid=(n_part // ts,),
        in_specs=[
            tile, ptile, ptile,
            pl.BlockSpec((d, EXPERT_DIM), const2),
            pl.BlockSpec((d, EXPERT_DIM), const2),
            pl.BlockSpec((EXPERT_DIM, d), const2),
            row(d), row(d),
            pl.BlockSpec((d, 2 * SGU_WIDTH), const2),
            row(2 * SGU_WIDTH), row(SGU_WIDTH), row(SGU_WIDTH),
            pl.BlockSpec((SGU_HEADS, SGU_CHUNK, SGU_CHUNK), const3),
            pl.BlockSpec((SGU_HEADS, SGU_CHUNK, 1), const3),
            pl.BlockSpec((SGU_WIDTH, d), const2),
            row(d), row(d),
        ],
        out_specs=(tile, ptile),
        out_shape=(jax.ShapeDtypeStruct((n_part, d), _F32),
                   jax.ShapeDtypeStruct((PIECES, n_part, LANES), _U32)),
        scratch_shapes=[pltpu.VMEM((ts, SGU_WIDTH), _BF16), pltpu.VMEM((ts, d), _F32)],
        compiler_params=pltpu.CompilerParams(
            dimension_semantics=("arbitrary",),
            vmem_limit_bytes=VMEM_LIMIT),
        name="moe_combine_sgu_mixer",
    )(x, routed_lo, routed_hi, sh_gate.astype(_BF16), sh_up.astype(_BF16), sh_down.astype(_BF16),
      ffn_g.reshape(1, d), ffn_b.reshape(1, d),
      w_in.astype(_BF16), b_in.reshape(1, -1), ln_g.reshape(1, -1), ln_b.reshape(1, -1),
      ws, b_s.reshape(SGU_HEADS, SGU_CHUNK, 1), w_out.astype(_BF16),
      mix_g.reshape(1, d), mix_b.reshape(1, d))


def _moe_dispatch(layer, x, x_packed, w_router, router_bias, w_gate, w_up, w_down):
    n_part, d = x.shape
    bm = EXPERT_ROWS
    n_rows = n_part * TOP_K + N_EXPERTS * bm
    e_k, r_k, g_k, counts = _route(x, w_router, router_bias)
    counts = counts.reshape(N_EXPERTS).astype(jnp.int32)
    padded = (counts + bm - 1) // bm * bm
    padded_end = jnp.cumsum(padded)
    padded_start = padded_end - padded
    idx = _row_indices(padded_start, e_k, r_k, n_rows).reshape(PIECES * TOP_K, n_part)
    xs = _sc_scatter_rows(x_packed.reshape(PIECES * n_part, LANES), idx, PIECES * n_rows)
    rows = _experts(layer, padded_start // bm, padded // bm, counts, padded_end[-1:] // bm,
                    xs.reshape(PIECES, n_rows, LANES), w_gate, w_up, w_down)
    return rows.reshape(PIECES * n_rows, LANES), idx, g_k


def _routed_sums(x, rows, idx, gates):
    n_part = x.shape[0]
    lo, hi = _sc_gather_sum(rows, idx, gates)
    return lo.reshape(PIECES, n_part, LANES), hi.reshape(PIECES, n_part, LANES)


def kernel(x, pool_w_in, pool_w_grp, pool_scale, pool_w_out, sgu_w_in, sgu_b_in, sgu_ln_g, sgu_ln_b, sgu_w_s, sgu_b_s, sgu_w_out, ln_mix_g, ln_mix_b, moe_w_router, moe_router_bias, moe_w_gate, moe_w_up, moe_w_down, moe_sh_gate, moe_sh_up, moe_sh_down, ln_ffn_g, ln_ffn_b):
    bsz, seq, d = x.shape
    parts = range(TOKEN_PARTS)

    def dispatch(i, h, h_packed):
        return _moe_dispatch(i, h, h_packed, moe_w_router[i], moe_router_bias[i],
                             moe_w_gate, moe_w_up, moe_w_down)

    mixed = [_pool_layer(x, p, pool_w_in[0], pool_w_grp[0], pool_scale[0], pool_w_out[0],
                         ln_mix_g[0], ln_mix_b[0]) for p in parts]
    mixed = [(h.reshape(-1, d), h_packed) for h, h_packed in mixed]
    routed = [dispatch(0, h, h_packed) for h, h_packed in mixed]
    mixed = [_combine_sgu(mixed[p][0], *_routed_sums(mixed[p][0], *routed[p]),
                          moe_sh_gate[0], moe_sh_up[0], moe_sh_down[0], ln_ffn_g[0], ln_ffn_b[0],
                          sgu_w_in[0], sgu_b_in[0], sgu_ln_g[0], sgu_ln_b[0], sgu_w_s[0],
                          sgu_b_s[0], sgu_w_out[0], ln_mix_g[1], ln_mix_b[1]) for p in parts]
    routed = [dispatch(1, h, h_packed) for h, h_packed in mixed]
    out = None
    for p in parts:
        h = mixed[p][0]
        out = _combine(h, *_routed_sums(h, *routed[p]), moe_sh_gate[1], moe_sh_up[1],
                       moe_sh_down[1], ln_ffn_g[1], ln_ffn_b[1], TOKEN_PARTS, p, out)
    return out.reshape(bsz, seq, d)
```

```python
import functools

import jax
import jax.numpy as jnp
from jax import lax
from jax.experimental import pallas as pl
from jax.experimental.pallas import tpu as pltpu
from jax.experimental.pallas import tpu_sc as plsc

D_MODEL = 1024
DEPTH = 2
POOL_WINDOWS = (2, 4, 8, 16)
POOL_GROUP_DIM = D_MODEL // len(POOL_WINDOWS)
POOL_HALO = 16
SGU_CHUNK = 128
SGU_HEADS = 4
SGU_WIDTH = 2 * D_MODEL
SGU_HEAD_DIM = SGU_WIDTH // SGU_HEADS
N_EXPERTS = 64
TOP_K = 8
N_GROUPS = 8
GROUP_SIZE = N_EXPERTS // N_GROUPS
TOPK_GROUPS = 4
EXPERT_DIM = D_MODEL // 4
ROUTED_SCALE = 2.5
DEEPNORM_ALPHA = (2 * DEPTH) ** 0.25
LN_EPS = 1e-5

LANES = 128
PACKED = D_MODEL // 2
PIECES = PACKED // LANES

POOL_TILE = 1024
POOL_SUB = 256
SGU_TILE = 512
SGU_SUB = 256
ROUTE_TILE = 1024
ROUTE_SUB = 512
INDEX_TILE = 2048
EXPERT_ROWS = 512
EXPERT_SUB = 256
IN_SLOTS = 6
OUT_SLOTS = 4
COMBINE_TILE = 1024
SC_WINDOW = 128
SC_LANES = 16
SC_BURST = 16
TOKEN_PARTS = 2
V7X_VMEM_BYTES = 64 * 1024 * 1024
VMEM_LIMIT = V7X_VMEM_BYTES * 7 // 8

_F32 = jnp.float32
_BF16 = jnp.bfloat16
_U32 = jnp.uint32


def _dot(a, b):
    return jnp.dot(a, b, preferred_element_type=_F32)


def _layer_norm(h, g, b):
    mu = jnp.mean(h, axis=-1, keepdims=True)
    hc = h - mu
    var = jnp.mean(hc * hc, axis=-1, keepdims=True)
    return hc * lax.rsqrt(var + LN_EPS) * g + b


def _silu(x):
    return x * jax.nn.sigmoid(x)


def _gelu_tanh(x):
    c = 0.7978845608028654
    return x * (0.5 + 0.5 * jnp.tanh(x * (c + (c * 0.044715) * (x * x))))


def _pack_halves(v):
    half = v.shape[1] // 2
    lo = lax.bitcast_convert_type(v[:, :half].astype(_BF16).astype(_F32), _U32)
    hi = lax.bitcast_convert_type(v[:, half:].astype(_BF16).astype(_F32), _U32)
    return hi | (lo >> 16)


def _unpack_halves(w):
    lo = lax.bitcast_convert_type(w << 16, _F32)
    hi = lax.bitcast_convert_type(w & _U32(0xFFFF0000), _F32)
    return lo, hi


def _store_pieces(ref, r0, w):
    for c in range(PIECES):
        ref[c, r0:r0 + w.shape[0], :] = w[:, c * LANES:(c + 1) * LANES]


def _pool_kernel(x_ref, win_ref, wgrp_ref, scale_ref, wout_ref, g_ref, b_ref,
                 o_ref, op_ref, zs_ref, y_ref):
    s = pl.program_id(1)
    ts = x_ref.shape[1]

    @pl.when(s == 0)
    def _():
        zs_ref[0:POOL_HALO, :] = jnp.zeros((POOL_HALO, D_MODEL), _F32)

    for p0 in range(0, ts, POOL_SUB):
        z = _dot(x_ref[0, p0:p0 + POOL_SUB, :].astype(_BF16), win_ref[...])
        zs_ref[POOL_HALO + p0:POOL_HALO + p0 + POOL_SUB, :] = z
    for p0 in range(0, ts, POOL_SUB):
        x = x_ref[0, p0:p0 + POOL_SUB, :]
        base = POOL_HALO + p0
        pos = s * ts + p0 + lax.broadcasted_iota(jnp.int32, (POOL_SUB, 1), 0)
        for g, w in enumerate(POOL_WINDOWS):
            c0 = g * POOL_GROUP_DIM
            c1 = c0 + POOL_GROUP_DIM
            zg = zs_ref[base:base + POOL_SUB, c0:c1]
            ext = zs_ref[base - POOL_HALO:base + POOL_SUB, c0:c1]
            m = 1
            while m < w:
                ext = ext + pltpu.roll(ext, m, 0)
                m *= 2
            acc = ext[POOL_HALO:, :]
            cnt = jnp.minimum(pos + 1, w).astype(_F32)
            pooled = acc / cnt - zg
            yg = _dot(pooled.astype(_BF16), wgrp_ref[g]) * scale_ref[:, c0:c1]
            y_ref[p0:p0 + POOL_SUB, c0:c1] = yg.astype(_BF16)
        mix = _dot(y_ref[p0:p0 + POOL_SUB, :], wout_ref[...])
        out = _layer_norm(DEEPNORM_ALPHA * x + mix, g_ref[...], b_ref[...])
        o_ref[0, p0:p0 + POOL_SUB, :] = out
        _store_pieces(op_ref, p0, _pack_halves(out))
    zs_ref[0:POOL_HALO, :] = zs_ref[ts:ts + POOL_HALO, :]


def _pool_layer(x, part, w_in, w_grp, scale, w_out, ln_g, ln_b):
    bsz, seq, d = x.shape
    rows = bsz // TOKEN_PARTS
    ts = POOL_TILE
    steps = seq // ts
    const2 = lambda b, s: (0, 0)
    out_shape = (jax.ShapeDtypeStruct((rows, seq, d), _F32),
                 jax.ShapeDtypeStruct((PIECES, rows * seq, LANES), _U32))
    tile = pl.BlockSpec((1, ts, d), lambda b, s: (b, s, 0))
    ptile = pl.BlockSpec((PIECES, ts, LANES), lambda b, s: (0, b * steps + s, 0))
    return pl.pallas_call(
        _pool_kernel,
        grid=(rows, steps),
        in_specs=[
            pl.BlockSpec((1, ts, d), lambda b, s: (part * rows + b, s, 0)),
            pl.BlockSpec((d, d), const2),
            pl.BlockSpec((len(POOL_WINDOWS), POOL_GROUP_DIM, POOL_GROUP_DIM), lambda b, s: (0, 0, 0)),
            pl.BlockSpec((1, d), const2),
            pl.BlockSpec((d, d), const2),
            pl.BlockSpec((1, d), const2),
            pl.BlockSpec((1, d), const2),
        ],
        out_specs=(tile, ptile),
        out_shape=out_shape,
        scratch_shapes=[pltpu.VMEM((POOL_HALO + ts, d), _F32),
                        pltpu.VMEM((ts, d), _BF16)],
        compiler_params=pltpu.CompilerParams(
            dimension_semantics=("arbitrary", "arbitrary"),
            vmem_limit_bytes=VMEM_LIMIT),
        name="pool_mixer",
    )(x, w_in.astype(_BF16), w_grp.astype(_BF16), scale.reshape(1, d),
      w_out.astype(_BF16), ln_g.reshape(1, d), ln_b.reshape(1, d))


def _sgu_kernel(x_ref, win_ref, bin_ref, lng_ref, lnb_ref, ws_ref, bs_ref, wout_ref,
                g_ref, b_ref, o_ref, op_ref, gated_ref):
    ts = x_ref.shape[0]
    projected = []
    for p0 in range(0, ts, SGU_SUB):
        x = x_ref[p0:p0 + SGU_SUB, :]
        xb = x.astype(_BF16)
        v = _gelu_tanh(_dot(xb, win_ref[:, SGU_WIDTH:]) + bin_ref[:, SGU_WIDTH:])
        v = _layer_norm(v, lng_ref[...], lnb_ref[...]).astype(_BF16)
        u = _gelu_tanh(_dot(xb, win_ref[:, :SGU_WIDTH]) + bin_ref[:, :SGU_WIDTH])
        projected.append((p0, x, u, v))
    for p0, x, u, v in projected:
        for r0 in range(0, SGU_SUB, SGU_CHUNK):
            for h in range(SGU_HEADS):
                c0 = h * SGU_HEAD_DIM
                sv = _dot(ws_ref[h], v[r0:r0 + SGU_CHUNK, c0:c0 + SGU_HEAD_DIM]) + bs_ref[h]
                gated_ref[p0 + r0:p0 + r0 + SGU_CHUNK, c0:c0 + SGU_HEAD_DIM] = (
                    u[r0:r0 + SGU_CHUNK, c0:c0 + SGU_HEAD_DIM] * sv).astype(_BF16)
        mix = _dot(gated_ref[p0:p0 + SGU_SUB, :], wout_ref[...])
        out = _layer_norm(DEEPNORM_ALPHA * x + mix, g_ref[...], b_ref[...])
        o_ref[p0:p0 + SGU_SUB, :] = out
        _store_pieces(op_ref, p0, _pack_halves(out))


def _all_max(a):
    return jnp.max(jnp.max(a, axis=0, keepdims=True), axis=1, keepdims=True)


def _all_min(a):
    return jnp.min(jnp.min(a, axis=0, keepdims=True), axis=1, keepdims=True)


def _all_sum(a):
    return jnp.sum(jnp.sum(a, axis=0, keepdims=True), axis=1, keepdims=True)


def _route_kernel(x_ref, whi_ref, wlo_ref, bias_ref, earlier_ref,
                  ek_ref, rk_ref, gk_ref, cnt_ref, carry_ref):
    @pl.when(pl.program_id(0) == 0)
    def _():
        carry_ref[...] = jnp.zeros_like(carry_ref)

    nt = lambda a, b: lax.dot_general(a, b, (((1,), (1,)), ((), ())),
                                      preferred_element_type=_F32)
    logits = []
    for p0 in range(0, x_ref.shape[0], ROUTE_SUB):
        x = x_ref[p0:p0 + ROUTE_SUB, :]
        x_hi = x.astype(_BF16)
        x_lo = (x - x_hi.astype(_F32)).astype(_BF16)
        logits.append(nt(whi_ref[...], x_hi) + (nt(whi_ref[...], x_lo) + nt(wlo_ref[...], x_hi)))
    for n, sub_logits in enumerate(logits):
        _route_sub_tile(sub_logits, n * ROUTE_SUB, bias_ref, earlier_ref,
                        ek_ref, rk_ref, gk_ref, carry_ref)
    cnt_ref[...] = carry_ref[...]


def _route_sub_tile(logits, p0, bias_ref, earlier_ref, ek_ref, rk_ref, gk_ref, carry_ref):
    ts = ROUTE_SUB
    scores = jax.nn.sigmoid(logits).reshape(N_GROUPS, GROUP_SIZE, ts)
    biased = scores + bias_ref[...]
    neg_inf = jnp.float32(-jnp.inf)
    shape3 = (N_GROUPS, GROUP_SIZE, ts)
    in_grp = lax.broadcasted_iota(jnp.int32, shape3, 1)
    grp = lax.broadcasted_iota(jnp.int32, shape3, 0)
    eid = grp * GROUP_SIZE + in_grp

    m1 = jnp.max(biased, axis=1, keepdims=True)
    first1 = jnp.min(jnp.where(biased == m1, in_grp, GROUP_SIZE), axis=1, keepdims=True)
    m2 = jnp.max(jnp.where(in_grp == first1, neg_inf, biased), axis=1, keepdims=True)
    gscore = m1 + m2

    gid = lax.broadcasted_iota(jnp.int32, (N_GROUPS, 1, ts), 0)
    gsel = jnp.zeros((N_GROUPS, 1, ts), jnp.bool_)
    for _ in range(TOPK_GROUPS):
        m = jnp.max(gscore, axis=0, keepdims=True)
        first = jnp.min(jnp.where(gscore == m, gid, N_GROUPS), axis=0, keepdims=True)
        pick = gid == first
        gsel = jnp.logical_or(gsel, pick)
        gscore = jnp.where(pick, neg_inf, gscore)

    masked = jnp.where(gsel, biased, neg_inf)
    picked_any = jnp.zeros(shape3, jnp.bool_)
    e_k, s_k = [], []
    for _ in range(TOP_K):
        m = _all_max(masked)
        first = _all_min(jnp.where(masked == m, eid, N_EXPERTS))
        pick = eid == first
        picked_any = jnp.logical_or(picked_any, pick)
        masked = jnp.where(pick, neg_inf, masked)
        e_k.append(first)
        s_k.append(_all_sum(jnp.where(pick, scores, 0.0)))
    denom = s_k[0]
    for k in range(1, TOP_K):
        denom = denom + s_k[k]

    sel = picked_any.astype(_F32).reshape(N_EXPERTS, ts)
    rank = _dot(sel.astype(_BF16), earlier_ref[...]) + carry_ref[...]
    rank3 = rank.reshape(shape3)
    carry_ref[...] += jnp.sum(sel, axis=1, keepdims=True)

    for k in range(TOP_K):
        r = _all_sum(jnp.where(eid == e_k[k], rank3, 0.0))
        ek_ref[k:k + 1, p0:p0 + ts] = e_k[k].reshape(1, ts)
        rk_ref[k:k + 1, p0:p0 + ts] = r.reshape(1, ts).astype(jnp.int32)
        gk_ref[k:k + 1, p0:p0 + ts] = (s_k[k] / denom * ROUTED_SCALE).reshape(1, ts)


def _route(x, w_router, router_bias):
    n_part, d = x.shape
    ts = ROUTE_TILE
    steps = n_part // ts
    out_shape = (jax.ShapeDtypeStruct((TOP_K, n_part), jnp.int32),
                 jax.ShapeDtypeStruct((TOP_K, n_part), jnp.int32),
                 jax.ShapeDtypeStruct((TOP_K, n_part), _F32),
                 jax.ShapeDtypeStruct((N_EXPERTS, 1), _F32))
    kspec = pl.BlockSpec((TOP_K, ts), lambda i: (0, i))
    const2 = lambda i: (0, 0)
    w_t = w_router.T
    w_hi = w_t.astype(_BF16)
    w_lo = (w_t - w_hi.astype(_F32)).astype(_BF16)
    pos = jnp.arange(ROUTE_SUB, dtype=jnp.int32)
    earlier = (pos[:, None] < pos[None, :]).astype(_BF16)
    return pl.pallas_call(
        _route_kernel,
        grid=(steps,),
        in_specs=[
            pl.BlockSpec((ts, d), lambda i: (i, 0)),
            pl.BlockSpec((N_EXPERTS, d), const2),
            pl.BlockSpec((N_EXPERTS, d), const2),
            pl.BlockSpec((N_GROUPS, GROUP_SIZE, 1), lambda i: (0, 0, 0)),
            pl.BlockSpec((ROUTE_SUB, ROUTE_SUB), const2),
        ],
        out_specs=(kspec, kspec, kspec, pl.BlockSpec((N_EXPERTS, 1), const2)),
        out_shape=out_shape,
        scratch_shapes=[pltpu.VMEM((N_EXPERTS, 1), _F32)],
        compiler_params=pltpu.CompilerParams(
            dimension_semantics=("arbitrary",),
            vmem_limit_bytes=VMEM_LIMIT),
        name="moe_route",
    )(x, w_hi, w_lo, router_bias.reshape(N_GROUPS, GROUP_SIZE, 1), earlier)


def _sc_mesh():
    return plsc.VectorSubcoreMesh(core_axis_name="core", subcore_axis_name="subcore")


def _sc_scatter_rows(src, idx, n_out):
    d = src.shape[1]
    wins = idx.shape[1] // SC_WINDOW

    def body(src_hbm, idx_hbm, out_hbm, sem):
        def step(src_vmem, idx_vmem):
            copies = [pltpu.make_async_copy(src_vmem, out_hbm.at[idx_vmem.at[k]], sem)
                      for k in range(TOP_K)]
            for copy in copies:
                copy.start()
            for copy in copies:
                copy.wait()

        pltpu.emit_pipeline(
            step,
            grid=(src.shape[0] // SC_WINDOW,),
            in_specs=[pl.BlockSpec((SC_WINDOW, d), index_map=lambda i: (i, 0)),
                      pl.BlockSpec((TOP_K, SC_WINDOW), index_map=lambda i: (i // wins, i % wins))],
            out_specs=[],
            core_axis_name=("core", "subcore"),
            dimension_semantics=(pltpu.PARALLEL,),
        )(src_hbm, idx_hbm)

    return pl.kernel(body, out_type=jax.ShapeDtypeStruct((n_out, d), src.dtype),
                     mesh=_sc_mesh(), scratch_types=[pltpu.SemaphoreType.DMA],
                     name="sc_scatter_rows")(src, idx)


def _sc_gather_sum(table, idx, gates):
    d = table.shape[1]
    n_tok = idx.shape[1]
    wins = n_tok // SC_WINDOW
    lanes = SC_LANES
    out_type = jax.ShapeDtypeStruct((PIECES * n_tok, d), _F32)

    def tree_sum(terms):
        while len(terms) > 1:
            terms = [terms[i] + terms[i + 1] for i in range(0, len(terms), 2)]
        return terms[0]

    def body(table_hbm, idx_hbm, gate_hbm, lo_hbm, hi_hbm, rows_vmem, sem):
        def step(idx_vmem, gate_vmem, lo_vmem, hi_vmem):
            def burst_copies(b):
                return [pltpu.make_async_copy(
                    table_hbm.at[idx_vmem.at[k, pl.ds(b * SC_BURST, SC_BURST)]],
                    rows_vmem.at[b % 2, k], sem.at[b % 2]) for k in range(TOP_K)]

            n_bursts = SC_WINDOW // SC_BURST
            for copy in burst_copies(0):
                copy.start()
            for b in range(n_bursts):
                if b + 1 < n_bursts:
                    for copy in burst_copies(b + 1):
                        copy.start()
                for copy in burst_copies(b):
                    copy.wait()
                t0 = b * SC_BURST
                rows = rows_vmem.at[b % 2]

                @plsc.parallel_loop(0, SC_BURST)
                def _(r):
                    t = t0 + r
                    token = jnp.full((lanes,), t, jnp.int32)
                    gate = [plsc.load_gather(gate_vmem, [jnp.full((lanes,), k, jnp.int32), token])
                            for k in range(TOP_K)]
                    for j in range(0, d, lanes):
                        words = [rows[k, r, pl.ds(j, lanes)] for k in range(TOP_K)]
                        lo_vmem[t, pl.ds(j, lanes)] = tree_sum(
                            [gate[k] * lax.bitcast_convert_type(words[k] << 16, _F32)
                             for k in range(TOP_K)])
                        hi_vmem[t, pl.ds(j, lanes)] = tree_sum(
                            [gate[k] * lax.bitcast_convert_type(words[k] & _U32(0xFFFF0000), _F32)
                             for k in range(TOP_K)])

        window = lambda i: (i // wins, i % wins)
        pltpu.emit_pipeline(
            step,
            grid=(PIECES * wins,),
            in_specs=[pl.BlockSpec((TOP_K, SC_WINDOW), index_map=window),
                      pl.BlockSpec((TOP_K, SC_WINDOW), index_map=lambda i: (0, i % wins))],
            out_specs=[pl.BlockSpec((SC_WINDOW, d), index_map=lambda i: (i, 0)),
                       pl.BlockSpec((SC_WINDOW, d), index_map=lambda i: (i, 0))],
            core_axis_name=("core", "subcore"),
            dimension_semantics=(pltpu.PARALLEL,),
        )(idx_hbm, gate_hbm, lo_hbm, hi_hbm)

    return pl.kernel(body, out_type=(out_type, out_type), mesh=_sc_mesh(),
                     scratch_types=[pltpu.VMEM((2, TOP_K, SC_BURST, d), _U32),
                                    pltpu.SemaphoreType.DMA((2,))],
                     compiler_params=pltpu.CompilerParams(needs_layout_passes=False),
                     name="sc_gather_sum")(table, idx, gates)


def _index_kernel(start_ref, ek_ref, rk_ref, idx_ref, *, n_rows):
    e = ek_ref[...]
    start = jnp.zeros(e.shape, jnp.int32)
    for ex in range(N_EXPERTS):
        start = jnp.where(e == ex, start_ref[ex], start)
    dest = start + rk_ref[...]
    for c in range(PIECES):
        idx_ref[c] = dest + c * n_rows


def _row_indices(padded_start, e_k, r_k, n_rows):
    n_tok = e_k.shape[1]
    ts = INDEX_TILE
    kspec = pl.BlockSpec((TOP_K, ts), lambda i, st: (0, i))
    grid_spec = pltpu.PrefetchScalarGridSpec(
        num_scalar_prefetch=1,
        grid=(n_tok // ts,),
        in_specs=[kspec, kspec],
        out_specs=pl.BlockSpec((PIECES, TOP_K, ts), lambda i, st: (0, 0, i)),
    )
    return pl.pallas_call(
        functools.partial(_index_kernel, n_rows=n_rows),
        grid_spec=grid_spec,
        out_shape=jax.ShapeDtypeStruct((PIECES, TOP_K, n_tok), jnp.int32),
        compiler_params=pltpu.CompilerParams(dimension_semantics=("arbitrary",)),
        name="moe_row_indices",
    )(padded_start, e_k, r_k)


def _expert_kernel(chunk0_ref, nchunk_ref, cnt_ref, total_ref,
                   xs_hbm, wg_ref, wu_ref, wd_ref, o_hbm,
                   xbuf, obuf, sem_in, sem_out):
    e = pl.program_id(0)
    total = total_ref[0]
    ch = EXPERT_ROWS
    ahead = IN_SLOTS - 1

    def in_copy(g):
        slot = g % IN_SLOTS
        return pltpu.make_async_copy(xs_hbm.at[:, pl.ds(g * ch, ch), :], xbuf.at[slot],
                                     sem_in.at[slot])

    def out_copy(g):
        slot = g % OUT_SLOTS
        return pltpu.make_async_copy(obuf.at[slot], o_hbm.at[:, pl.ds(g * ch, ch), :],
                                     sem_out.at[slot])

    @pl.when(e == 0)
    def _():
        obuf[...] = jnp.zeros_like(obuf)
        for g in range(ahead):
            @pl.when(g < total)
            def _():
                in_copy(g).start()


    def chunk(j, carry):
        g = chunk0_ref[e] + j
        in_copy(g).wait()

        @pl.when(g + ahead < total)
        def _():
            in_copy(g + ahead).start()

        @pl.when(g >= OUT_SLOTS)
        def _():
            out_copy(g - OUT_SLOTS).wait()

        islot = g % IN_SLOTS
        oslot = g % OUT_SLOTS
        valid = cnt_ref[e] - j * ch

        def gate_up(s):
            r0 = s * EXPERT_SUB
            w = jnp.concatenate([xbuf[islot, c, r0:r0 + EXPERT_SUB, :] for c in range(PIECES)],
                                axis=1)
            lo, hi = _unpack_halves(w)
            gate = _dot(lo, wg_ref[0, 0, :PACKED, :]) + _dot(hi, wg_ref[0, 0, PACKED:, :])
            up = _dot(lo, wu_ref[0, 0, :PACKED, :]) + _dot(hi, wu_ref[0, 0, PACKED:, :])
            return gate, up

        def down(s, gate_up_pair):
            r0 = s * EXPERT_SUB
            gate, up = gate_up_pair
            y = _pack_halves(_dot(_silu(gate) * up, wd_ref[0, 0]))
            for c in range(PIECES):
                obuf[oslot, c, r0:r0 + EXPERT_SUB, :] = y[:, c * LANES:(c + 1) * LANES]

        n_sub = ch // EXPERT_SUB

        all_subs = valid > ch - EXPERT_SUB

        @pl.when(all_subs)
        def _():
            projected = [gate_up(s) for s in range(n_sub)]
            for s, gu in enumerate(projected):
                down(s, gu)

        @pl.when(jnp.logical_not(all_subs))
        def _():
            for s in range(n_sub - 1):
                @pl.when(s * EXPERT_SUB < valid)
                def _():
                    down(s, gate_up(s))

        out_copy(g).start()
        return carry

    lax.fori_loop(0, nchunk_ref[e], chunk, 0)

    @pl.when(e == pl.num_programs(0) - 1)
    def _():
        for back in range(OUT_SLOTS, 0, -1):
            @pl.when(total >= back)
            def _():
                out_copy(total - back).wait()


def _experts(layer, chunk0, nchunk, counts, total, xs, w_gate, w_up, w_down):
    d = D_MODEL
    ch = EXPERT_ROWS
    wspec_in = pl.BlockSpec((1, 1, d, EXPERT_DIM), lambda e, *_: (layer, e, 0, 0))
    any_spec = pl.BlockSpec(memory_space=pl.ANY)
    grid_spec = pltpu.PrefetchScalarGridSpec(
        num_scalar_prefetch=4,
        grid=(N_EXPERTS,),
        in_specs=[
            any_spec,
            wspec_in,
            wspec_in,
            pl.BlockSpec((1, 1, EXPERT_DIM, d), lambda e, *_: (layer, e, 0, 0)),
        ],
        out_specs=any_spec,
        scratch_shapes=[pltpu.VMEM((IN_SLOTS, PIECES, ch, LANES), _U32),
                        pltpu.VMEM((OUT_SLOTS, PIECES, ch, LANES), _U32),
                        pltpu.SemaphoreType.DMA((IN_SLOTS,)),
                        pltpu.SemaphoreType.DMA((OUT_SLOTS,))],
    )
    return pl.pallas_call(
        _expert_kernel,
        grid_spec=grid_spec,
        out_shape=jax.ShapeDtypeStruct(xs.shape, _U32),
        compiler_params=pltpu.CompilerParams(
            dimension_semantics=("arbitrary",),
            vmem_limit_bytes=VMEM_LIMIT),
        name="moe_experts",
    )(chunk0, nchunk, counts, total, xs, w_gate, w_up, w_down)


def _moe_residual(x, lo_ref, hi_ref, rows, sg_ref, su_ref, sd_ref, g_ref, b_ref):
    xb = x.astype(_BF16)
    routed = jnp.concatenate([lo_ref[c, rows, :] for c in range(PIECES)] +
                             [hi_ref[c, rows, :] for c in range(PIECES)], axis=1)
    h = _silu(_dot(xb, sg_ref[...])) * _dot(xb, su_ref[...])
    shared = _dot(h.astype(_BF16), sd_ref[...])
    return _layer_norm(DEEPNORM_ALPHA * x + (routed + shared), g_ref[...], b_ref[...])


def _combine_kernel(x_ref, lo_ref, hi_ref, sg_ref, su_ref, sd_ref, g_ref, b_ref, *rest):
    o_ref = rest[-1]
    o_ref[...] = _moe_residual(x_ref[...], lo_ref, hi_ref, slice(None),
                               sg_ref, su_ref, sd_ref, g_ref, b_ref)


def _combine_sgu_kernel(x_ref, lo_ref, hi_ref, sg_ref, su_ref, sd_ref, fg_ref, fb_ref,
                        win_ref, bin_ref, lng_ref, lnb_ref, ws_ref, bs_ref, wout_ref,
                        mg_ref, mb_ref, o_ref, op_ref, gated_ref, h_ref):
    for p0 in range(0, x_ref.shape[0], SGU_SUB):
        rows = slice(p0, p0 + SGU_SUB)
        h_ref[rows, :] = _moe_residual(x_ref[rows, :], lo_ref, hi_ref, rows,
                                       sg_ref, su_ref, sd_ref, fg_ref, fb_ref)
    _sgu_kernel(h_ref, win_ref, bin_ref, lng_ref, lnb_ref, ws_ref, bs_ref, wout_ref,
                mg_ref, mb_ref, o_ref, op_ref, gated_ref)


def _combine(x, routed_lo, routed_hi, sh_gate, sh_up, sh_down, ln_g, ln_b, out_parts, part, prev):
    n_part, d = x.shape
    ts = COMBINE_TILE
    steps = n_part // ts
    const2 = lambda i: (0, 0)
    in_specs = [
        pl.BlockSpec((ts, d), lambda i: (i, 0)),
        pl.BlockSpec((PIECES, ts, LANES), lambda i: (0, i, 0)),
        pl.BlockSpec((PIECES, ts, LANES), lambda i: (0, i, 0)),
        pl.BlockSpec((d, EXPERT_DIM), const2),
        pl.BlockSpec((d, EXPERT_DIM), const2),
        pl.BlockSpec((EXPERT_DIM, d), const2),
        pl.BlockSpec((1, d), const2),
        pl.BlockSpec((1, d), const2),
    ]
    args = [x, routed_lo, routed_hi, sh_gate.astype(_BF16), sh_up.astype(_BF16), sh_down.astype(_BF16),
            ln_g.reshape(1, d), ln_b.reshape(1, d)]
    aliases = {}
    if prev is not None:
        in_specs.append(pl.BlockSpec(memory_space=pl.ANY))
        aliases = {len(args): 0}
        args.append(prev)
    return pl.pallas_call(
        _combine_kernel,
        grid=(steps,),
        in_specs=in_specs,
        out_specs=pl.BlockSpec((ts, d), lambda i: (part * steps + i, 0)),
        out_shape=jax.ShapeDtypeStruct((out_parts * n_part, d), _F32),
        input_output_aliases=aliases,
        compiler_params=pltpu.CompilerParams(
            dimension_semantics=("arbitrary",),
            vmem_limit_bytes=VMEM_LIMIT),
        name="moe_combine",
    )(*args)


def _combine_sgu(x, routed_lo, routed_hi, sh_gate, sh_up, sh_down, ffn_g, ffn_b,
                 w_in, b_in, ln_g, ln_b, w_s, b_s, w_out, mix_g, mix_b):
    n_part, d = x.shape
    ts = SGU_TILE
    const2 = lambda i: (0, 0)
    const3 = lambda i: (0, 0, 0)
    causal = jnp.tril(jnp.ones((SGU_CHUNK, SGU_CHUNK), w_s.dtype))
    ws = (w_s * causal[None]).astype(_BF16)
    tile = pl.BlockSpec((ts, d), lambda i: (i, 0))
    ptile = pl.BlockSpec((PIECES, ts, LANES), lambda i: (0, i, 0))
    row = lambda n: pl.BlockSpec((1, n), const2)
    return pl.pallas_call(
        _combine_sgu_kernel,
        grid=(n_part // ts,),
        in_specs=[
            tile, ptile, ptile,
            pl.BlockSpec((d, EXPERT_DIM), const2),
            pl.BlockSpec((d, EXPERT_DIM), const2),
            pl.BlockSpec((EXPERT_DIM, d), const2),
            row(d), row(d),
            pl.BlockSpec((d, 2 * SGU_WIDTH), const2),
            row(2 * SGU_WIDTH), row(SGU_WIDTH), row(SGU_WIDTH),
            pl.BlockSpec((SGU_HEADS, SGU_CHUNK, SGU_CHUNK), const3),
            pl.BlockSpec((SGU_HEADS, SGU_CHUNK, 1), const3),
            pl.BlockSpec((SGU_WIDTH, d), const2),
            row(d), row(d),
        ],
        out_specs=(tile, ptile),
        out_shape=(jax.ShapeDtypeStruct((n_part, d), _F32),
                   jax.ShapeDtypeStruct((PIECES, n_part, LANES), _U32)),
        scratch_shapes=[pltpu.VMEM((ts, SGU_WIDTH), _BF16), pltpu.VMEM((ts, d), _F32)],
        compiler_params=pltpu.CompilerParams(
            dimension_semantics=("arbitrary",),
            vmem_limit_bytes=VMEM_LIMIT),
        name="moe_combine_sgu_mixer",
    )(x, routed_lo, routed_hi, sh_gate.astype(_BF16), sh_up.astype(_BF16), sh_down.astype(_BF16),
      ffn_g.reshape(1, d), ffn_b.reshape(1, d),
      w_in.astype(_BF16), b_in.reshape(1, -1), ln_g.reshape(1, -1), ln_b.reshape(1, -1),
      ws, b_s.reshape(SGU_HEADS, SGU_CHUNK, 1), w_out.astype(_BF16),
      mix_g.reshape(1, d), mix_b.reshape(1, d))


def _moe_dispatch(layer, x, x_packed, w_router, router_bias, w_gate, w_up, w_down):
    n_part, d = x.shape
    bm = EXPERT_ROWS
    n_rows = n_part * TOP_K + N_EXPERTS * bm
    e_k, r_k, g_k, counts = _route(x, w_router, router_bias)
    counts = counts.reshape(N_EXPERTS).astype(jnp.int32)
    padded = (counts + bm - 1) // bm * bm
    padded_end = jnp.cumsum(padded)
    padded_start = padded_end - padded
    idx = _row_indices(padded_start, e_k, r_k, n_rows).reshape(PIECES * TOP_K, n_part)
    xs = _sc_scatter_rows(x_packed.reshape(PIECES * n_part, LANES), idx, PIECES * n_rows)
    rows = _experts(layer, padded_start // bm, padded // bm, counts, padded_end[-1:] // bm,
                    xs.reshape(PIECES, n_rows, LANES), w_gate, w_up, w_down)
    return rows.reshape(PIECES * n_rows, LANES), idx, g_k


def _routed_sums(x, rows, idx, gates):
    n_part = x.shape[0]
    lo, hi = _sc_gather_sum(rows, idx, gates)
    return lo.reshape(PIECES, n_part, LANES), hi.reshape(PIECES, n_part, LANES)


def kernel(x, pool_w_in, pool_w_grp, pool_scale, pool_w_out, sgu_w_in, sgu_b_in, sgu_ln_g, sgu_ln_b, sgu_w_s, sgu_b_s, sgu_w_out, ln_mix_g, ln_mix_b, moe_w_router, moe_router_bias, moe_w_gate, moe_w_up, moe_w_down, moe_sh_gate, moe_sh_up, moe_sh_down, ln_ffn_g, ln_ffn_b):
    bsz, seq, d = x.shape
    parts = range(TOKEN_PARTS)

    def dispatch(i, h, h_packed):
        return _moe_dispatch(i, h, h_packed, moe_w_router[i], moe_router_bias[i],
                             moe_w_gate, moe_w_up, moe_w_down)

    mixed = [_pool_layer(x, p, pool_w_in[0], pool_w_grp[0], pool_scale[0], pool_w_out[0],
                         ln_mix_g[0], ln_mix_b[0]) for p in parts]
    mixed = [(h.reshape(-1, d), h_packed) for h, h_packed in mixed]
    routed = [dispatch(0, h, h_packed) for h, h_packed in mixed]
    mixed = [_combine_sgu(mixed[p][0], *_routed_sums(mixed[p][0], *routed[p]),
                          moe_sh_gate[0], moe_sh_up[0], moe_sh_down[0], ln_ffn_g[0], ln_ffn_b[0],
                          sgu_w_in[0], sgu_b_in[0], sgu_ln_g[0], sgu_ln_b[0], sgu_w_s[0],
                          sgu_b_s[0], sgu_w_out[0], ln_mix_g[1], ln_mix_b[1]) for p in parts]
    routed = [dispatch(1, h, h_packed) for h, h_packed in mixed]
    out = None
    for p in parts:
        h = mixed[p][0]
        out = _combine(h, *_routed_sums(h, *routed[p]), moe_sh_gate[1], moe_sh_up[1],
                       moe_sh_down[1], ln_ffn_g[1], ln_ffn_b[1], TOKEN_PARTS, p, out)
    return out.reshape(bsz, seq, d)
```

```python
import jax
import jax.numpy as jnp
from jax import lax
from jax.experimental import pallas as pl
from jax.experimental.pallas import tpu as pltpu
from jax.experimental.pallas import tpu_sc as plsc

D_MODEL = 1024
DEPTH = 2
POOL_WINDOWS = (2, 4, 8, 16)
POOL_GROUP_DIM = D_MODEL // len(POOL_WINDOWS)
POOL_HALO = 16
SGU_CHUNK = 128
SGU_HEADS = 4
SGU_WIDTH = 2 * D_MODEL
SGU_HEAD_DIM = SGU_WIDTH // SGU_HEADS
N_EXPERTS = 64
TOP_K = 8
N_GROUPS = 8
GROUP_SIZE = N_EXPERTS // N_GROUPS
TOPK_GROUPS = 4
EXPERT_DIM = D_MODEL // 4
ROUTED_SCALE = 2.5
DEEPNORM_ALPHA = (2 * DEPTH) ** 0.25
LN_EPS = 1e-5

LANES = 128
PACKED = D_MODEL // 2
PIECES = PACKED // LANES

POOL_TILE = 1024
POOL_SUB = 256
SGU_TILE = 512
SGU_SUB = 256
ROUTE_TILE = 1024
ROUTE_SUB = 512
EXPERT_ROWS = 512
EXPERT_SUB = 256
IN_SLOTS = 6
OUT_SLOTS = 4
COMBINE_TILE = 1024
SC_WINDOW = 128
SC_LANES = 16
SC_BURST = 16
TOKEN_PARTS = 2
V7X_VMEM_BYTES = 64 * 1024 * 1024
VMEM_LIMIT = V7X_VMEM_BYTES * 7 // 8

_F32 = jnp.float32
_BF16 = jnp.bfloat16
_U32 = jnp.uint32


def _dot(a, b):
    return jnp.dot(a, b, preferred_element_type=_F32)


def _layer_norm(h, g, b):
    mu = jnp.mean(h, axis=-1, keepdims=True)
    hc = h - mu
    var = jnp.mean(hc * hc, axis=-1, keepdims=True)
    return hc * lax.rsqrt(var + LN_EPS) * g + b


def _silu(x):
    return x * jax.nn.sigmoid(x)


def _gelu_tanh(x):
    c = 0.7978845608028654
    return x * (0.5 + 0.5 * jnp.tanh(x * (c + (c * 0.044715) * (x * x))))


def _pack_halves(v):
    half = v.shape[1] // 2
    lo = lax.bitcast_convert_type(v[:, :half].astype(_BF16).astype(_F32), _U32)
    hi = lax.bitcast_convert_type(v[:, half:].astype(_BF16).astype(_F32), _U32)
    return hi | (lo >> 16)


def _unpack_halves(w):
    lo = lax.bitcast_convert_type(w << 16, _F32)
    hi = lax.bitcast_convert_type(w & _U32(0xFFFF0000), _F32)
    return lo, hi


def _store_pieces(ref, r0, w):
    for c in range(PIECES):
        ref[c, r0:r0 + w.shape[0], :] = w[:, c * LANES:(c + 1) * LANES]


def _pool_kernel(x_ref, win_ref, wgrp_ref, scale_ref, wout_ref, g_ref, b_ref,
                 o_ref, op_ref, zs_ref, y_ref):
    s = pl.program_id(1)
    ts = x_ref.shape[1]

    @pl.when(s == 0)
    def _():
        zs_ref[0:POOL_HALO, :] = jnp.zeros((POOL_HALO, D_MODEL), _F32)

    for p0 in range(0, ts, POOL_SUB):
        z = _dot(x_ref[0, p0:p0 + POOL_SUB, :].astype(_BF16), win_ref[...])
        zs_ref[POOL_HALO + p0:POOL_HALO + p0 + POOL_SUB, :] = z
    for p0 in range(0, ts, POOL_SUB):
        x = x_ref[0, p0:p0 + POOL_SUB, :]
        base = POOL_HALO + p0
        pos = s * ts + p0 + lax.broadcasted_iota(jnp.int32, (POOL_SUB, 1), 0)
        for g, w in enumerate(POOL_WINDOWS):
            c0 = g * POOL_GROUP_DIM
            c1 = c0 + POOL_GROUP_DIM
            zg = zs_ref[base:base + POOL_SUB, c0:c1]
            ext = zs_ref[base - POOL_HALO:base + POOL_SUB, c0:c1]
            m = 1
            while m < w:
                ext = ext + pltpu.roll(ext, m, 0)
                m *= 2
            acc = ext[POOL_HALO:, :]
            cnt = jnp.minimum(pos + 1, w).astype(_F32)
            pooled = acc / cnt - zg
            yg = _dot(pooled.astype(_BF16), wgrp_ref[g]) * scale_ref[:, c0:c1]
            y_ref[p0:p0 + POOL_SUB, c0:c1] = yg.astype(_BF16)
        mix = _dot(y_ref[p0:p0 + POOL_SUB, :], wout_ref[...])
        out = _layer_norm(DEEPNORM_ALPHA * x + mix, g_ref[...], b_ref[...])
        o_ref[0, p0:p0 + POOL_SUB, :] = out
        _store_pieces(op_ref, p0, _pack_halves(out))
    zs_ref[0:POOL_HALO, :] = zs_ref[ts:ts + POOL_HALO, :]


def _pool_layer(x, part, w_in, w_grp, scale, w_out, ln_g, ln_b):
    bsz, seq, d = x.shape
    rows = bsz // TOKEN_PARTS
    ts = POOL_TILE
    steps = seq // ts
    const2 = lambda b, s: (0, 0)
    out_shape = (jax.ShapeDtypeStruct((rows, seq, d), _F32),
                 jax.ShapeDtypeStruct((PIECES, rows * seq, LANES), _U32))
    tile = pl.BlockSpec((1, ts, d), lambda b, s: (b, s, 0))
    ptile = pl.BlockSpec((PIECES, ts, LANES), lambda b, s: (0, b * steps + s, 0))
    return pl.pallas_call(
        _pool_kernel,
        grid=(rows, steps),
        in_specs=[
            pl.BlockSpec((1, ts, d), lambda b, s: (part * rows + b, s, 0)),
            pl.BlockSpec((d, d), const2),
            pl.BlockSpec((len(POOL_WINDOWS), POOL_GROUP_DIM, POOL_GROUP_DIM), lambda b, s: (0, 0, 0)),
            pl.BlockSpec((1, d), const2),
            pl.BlockSpec((d, d), const2),
            pl.BlockSpec((1, d), const2),
            pl.BlockSpec((1, d), const2),
        ],
        out_specs=(tile, ptile),
        out_shape=out_shape,
        scratch_shapes=[pltpu.VMEM((POOL_HALO + ts, d), _F32),
                        pltpu.VMEM((ts, d), _BF16)],
        compiler_params=pltpu.CompilerParams(
            dimension_semantics=("arbitrary", "arbitrary"),
            vmem_limit_bytes=VMEM_LIMIT),
        name="pool_mixer",
    )(x, w_in.astype(_BF16), w_grp.astype(_BF16), scale.reshape(1, d),
      w_out.astype(_BF16), ln_g.reshape(1, d), ln_b.reshape(1, d))


def _sgu_kernel(x_ref, win_ref, bin_ref, lng_ref, lnb_ref, ws_ref, bs_ref, wout_ref,
                g_ref, b_ref, o_ref, op_ref, gated_ref):
    ts = x_ref.shape[0]
    projected = []
    for p0 in range(0, ts, SGU_SUB):
        x = x_ref[p0:p0 + SGU_SUB, :]
        xb = x.astype(_BF16)
        v = _gelu_tanh(_dot(xb, win_ref[:, SGU_WIDTH:]) + bin_ref[:, SGU_WIDTH:])
        v = _layer_norm(v, lng_ref[...], lnb_ref[...]).astype(_BF16)
        u = _gelu_tanh(_dot(xb, win_ref[:, :SGU_WIDTH]) + bin_ref[:, :SGU_WIDTH])
        projected.append((p0, x, u, v))
    for p0, x, u, v in projected:
        for r0 in range(0, SGU_SUB, SGU_CHUNK):
            for h in range(SGU_HEADS):
                c0 = h * SGU_HEAD_DIM
                sv = _dot(ws_ref[h], v[r0:r0 + SGU_CHUNK, c0:c0 + SGU_HEAD_DIM]) + bs_ref[h]
                gated_ref[p0 + r0:p0 + r0 + SGU_CHUNK, c0:c0 + SGU_HEAD_DIM] = (
                    u[r0:r0 + SGU_CHUNK, c0:c0 + SGU_HEAD_DIM] * sv).astype(_BF16)
        mix = _dot(gated_ref[p0:p0 + SGU_SUB, :], wout_ref[...])
        out = _layer_norm(DEEPNORM_ALPHA * x + mix, g_ref[...], b_ref[...])
        o_ref[p0:p0 + SGU_SUB, :] = out
        _store_pieces(op_ref, p0, _pack_halves(out))


def _all_max(a):
    return jnp.max(jnp.max(a, axis=0, keepdims=True), axis=1, keepdims=True)


def _all_min(a):
    return jnp.min(jnp.min(a, axis=0, keepdims=True), axis=1, keepdims=True)


def _all_sum(a):
    return jnp.sum(jnp.sum(a, axis=0, keepdims=True), axis=1, keepdims=True)


def _route_kernel(x_ref, whi_ref, wlo_ref, bias_ref, earlier_ref,
                  ek_ref, rk_ref, gk_ref, cnt_ref, carry_ref):
    @pl.when(pl.program_id(0) == 0)
    def _():
        carry_ref[...] = jnp.zeros_like(carry_ref)

    nt = lambda a, b: lax.dot_general(a, b, (((1,), (1,)), ((), ())),
                                      preferred_element_type=_F32)
    logits = []
    for p0 in range(0, x_ref.shape[0], ROUTE_SUB):
        x = x_ref[p0:p0 + ROUTE_SUB, :]
        x_hi = x.astype(_BF16)
        x_lo = (x - x_hi.astype(_F32)).astype(_BF16)
        logits.append(nt(whi_ref[...], x_hi) + (nt(whi_ref[...], x_lo) + nt(wlo_ref[...], x_hi)))
    for n, sub_logits in enumerate(logits):
        _route_sub_tile(sub_logits, n * ROUTE_SUB, bias_ref, earlier_ref,
                        ek_ref, rk_ref, gk_ref, carry_ref)
    cnt_ref[...] = carry_ref[...]


def _route_sub_tile(logits, p0, bias_ref, earlier_ref, ek_ref, rk_ref, gk_ref, carry_ref):
    ts = ROUTE_SUB
    scores = jax.nn.sigmoid(logits).reshape(N_GROUPS, GROUP_SIZE, ts)
    biased = scores + bias_ref[...]
    neg_inf = jnp.float32(-jnp.inf)
    shape3 = (N_GROUPS, GROUP_SIZE, ts)
    in_grp = lax.broadcasted_iota(jnp.int32, shape3, 1)
    grp = lax.broadcasted_iota(jnp.int32, shape3, 0)
    eid = grp * GROUP_SIZE + in_grp

    m1 = jnp.max(biased, axis=1, keepdims=True)
    first1 = jnp.min(jnp.where(biased == m1, in_grp, GROUP_SIZE), axis=1, keepdims=True)
    m2 = jnp.max(jnp.where(in_grp == first1, neg_inf, biased), axis=1, keepdims=True)
    gscore = m1 + m2

    gid = lax.broadcasted_iota(jnp.int32, (N_GROUPS, 1, ts), 0)
    gsel = jnp.zeros((N_GROUPS, 1, ts), jnp.bool_)
    for _ in range(TOPK_GROUPS):
        m = jnp.max(gscore, axis=0, keepdims=True)
        first = jnp.min(jnp.where(gscore == m, gid, N_GROUPS), axis=0, keepdims=True)
        pick = gid == first
        gsel = jnp.logical_or(gsel, pick)
        gscore = jnp.where(pick, neg_inf, gscore)

    masked = jnp.where(gsel, biased, neg_inf)
    picked_any = jnp.zeros(shape3, jnp.bool_)
    e_k, s_k = [], []
    for _ in range(TOP_K):
        m = _all_max(masked)
        first = _all_min(jnp.where(masked == m, eid, N_EXPERTS))
        pick = eid == first
        picked_any = jnp.logical_or(picked_any, pick)
        masked = jnp.where(pick, neg_inf, masked)
        e_k.append(first)
        s_k.append(_all_sum(jnp.where(pick, scores, 0.0)))
    denom = s_k[0]
    for k in range(1, TOP_K):
        denom = denom + s_k[k]

    sel = picked_any.astype(_F32).reshape(N_EXPERTS, ts)
    rank = _dot(sel.astype(_BF16), earlier_ref[...]) + carry_ref[...]
    rank3 = rank.reshape(shape3)
    carry_ref[...] += jnp.sum(sel, axis=1, keepdims=True)

    for k in range(TOP_K):
        r = _all_sum(jnp.where(eid == e_k[k], rank3, 0.0))
        ek_ref[k:k + 1, p0:p0 + ts] = e_k[k].reshape(1, ts)
        rk_ref[k:k + 1, p0:p0 + ts] = r.reshape(1, ts).astype(jnp.int32)
        gk_ref[k:k + 1, p0:p0 + ts] = (s_k[k] / denom * ROUTED_SCALE).reshape(1, ts)


def _route(x, w_router, router_bias):
    n_part, d = x.shape
    ts = ROUTE_TILE
    steps = n_part // ts
    out_shape = (jax.ShapeDtypeStruct((TOP_K, n_part), jnp.int32),
                 jax.ShapeDtypeStruct((TOP_K, n_part), jnp.int32),
                 jax.ShapeDtypeStruct((TOP_K, n_part), _F32),
                 jax.ShapeDtypeStruct((N_EXPERTS, 1), _F32))
    kspec = pl.BlockSpec((TOP_K, ts), lambda i: (0, i))
    const2 = lambda i: (0, 0)
    w_t = w_router.T
    w_hi = w_t.astype(_BF16)
    w_lo = (w_t - w_hi.astype(_F32)).astype(_BF16)
    pos = jnp.arange(ROUTE_SUB, dtype=jnp.int32)
    earlier = (pos[:, None] < pos[None, :]).astype(_BF16)
    return pl.pallas_call(
        _route_kernel,
        grid=(steps,),
        in_specs=[
            pl.BlockSpec((ts, d), lambda i: (i, 0)),
            pl.BlockSpec((N_EXPERTS, d), const2),
            pl.BlockSpec((N_EXPERTS, d), const2),
            pl.BlockSpec((N_GROUPS, GROUP_SIZE, 1), lambda i: (0, 0, 0)),
            pl.BlockSpec((ROUTE_SUB, ROUTE_SUB), const2),
        ],
        out_specs=(kspec, kspec, kspec, pl.BlockSpec((N_EXPERTS, 1), const2)),
        out_shape=out_shape,
        scratch_shapes=[pltpu.VMEM((N_EXPERTS, 1), _F32)],
        compiler_params=pltpu.CompilerParams(
            dimension_semantics=("arbitrary",),
            vmem_limit_bytes=VMEM_LIMIT),
        name="moe_route",
    )(x, w_hi, w_lo, router_bias.reshape(N_GROUPS, GROUP_SIZE, 1), earlier)


def _sc_mesh():
    return plsc.VectorSubcoreMesh(core_axis_name="core", subcore_axis_name="subcore")


def _piece_offsets(n_rows):
    return jnp.broadcast_to(jnp.arange(PIECES, dtype=jnp.int32)[:, None] * n_rows,
                            (PIECES, SC_WINDOW))


def _sorted_rows(idx_vmem, ek_vmem, rk_vmem, start_vmem, poff_vmem):
    for k in range(TOP_K):
        for j in range(0, SC_WINDOW, SC_LANES):
            lanes = pl.ds(j, SC_LANES)
            start = plsc.load_gather(start_vmem, [ek_vmem[k, lanes]])
            idx_vmem[k, lanes] = start + rk_vmem[k, lanes] + poff_vmem[0, lanes]


def _sc_scatter_rows(src, e_k, r_k, start, n_rows):
    d = src.shape[1]
    wins = e_k.shape[1] // SC_WINDOW

    def body(src_hbm, ek_hbm, rk_hbm, start_hbm, poff_hbm, out_hbm, idx_vmem, start_vmem, sem):
        pltpu.sync_copy(start_hbm, start_vmem)

        def step(src_vmem, ek_vmem, rk_vmem, poff_vmem):
            _sorted_rows(idx_vmem, ek_vmem, rk_vmem, start_vmem, poff_vmem)
            copies = [pltpu.make_async_copy(src_vmem, out_hbm.at[idx_vmem.at[k]], sem)
                      for k in range(TOP_K)]
            for copy in copies:
                copy.start()
            for copy in copies:
                copy.wait()

        tokens = lambda i: (0, i % wins)
        pltpu.emit_pipeline(
            step,
            grid=(src.shape[0] // SC_WINDOW,),
            in_specs=[pl.BlockSpec((SC_WINDOW, d), index_map=lambda i: (i, 0)),
                      pl.BlockSpec((TOP_K, SC_WINDOW), index_map=tokens),
                      pl.BlockSpec((TOP_K, SC_WINDOW), index_map=tokens),
                      pl.BlockSpec((1, SC_WINDOW), index_map=lambda i: (i // wins, 0))],
            out_specs=[],
            core_axis_name=("core", "subcore"),
            dimension_semantics=(pltpu.PARALLEL,),
        )(src_hbm, ek_hbm, rk_hbm, poff_hbm)

    return pl.kernel(body, out_type=jax.ShapeDtypeStruct((PIECES * n_rows, d), src.dtype),
                     mesh=_sc_mesh(),
                     scratch_types=[pltpu.VMEM((TOP_K, SC_WINDOW), jnp.int32),
                                    pltpu.VMEM((N_EXPERTS,), jnp.int32),
                                    pltpu.SemaphoreType.DMA],
                     compiler_params=pltpu.CompilerParams(needs_layout_passes=False),
                     name="sc_scatter_rows")(src, e_k, r_k, start, _piece_offsets(n_rows))


def _sc_gather_sum(table, e_k, r_k, start, gates):
    d = table.shape[1]
    n_tok = e_k.shape[1]
    wins = n_tok // SC_WINDOW
    lanes = SC_LANES
    out_type = jax.ShapeDtypeStruct((PIECES * n_tok, d), _F32)

    def tree_sum(terms):
        while len(terms) > 1:
            terms = [terms[i] + terms[i + 1] for i in range(0, len(terms), 2)]
        return terms[0]

    def body(table_hbm, ek_hbm, rk_hbm, start_hbm, poff_hbm, gate_hbm, lo_hbm, hi_hbm,
             rows_vmem, idx_vmem, start_vmem, sem):
        pltpu.sync_copy(start_hbm, start_vmem)

        def step(ek_vmem, rk_vmem, poff_vmem, gate_vmem, lo_vmem, hi_vmem):
            _sorted_rows(idx_vmem, ek_vmem, rk_vmem, start_vmem, poff_vmem)

            def burst_copies(b):
                return [pltpu.make_async_copy(
                    table_hbm.at[idx_vmem.at[k, pl.ds(b * SC_BURST, SC_BURST)]],
                    rows_vmem.at[b % 2, k], sem.at[b % 2]) for k in range(TOP_K)]

            n_bursts = SC_WINDOW // SC_BURST
            for copy in burst_copies(0):
                copy.start()
            for b in range(n_bursts):
                if b + 1 < n_bursts:
                    for copy in burst_copies(b + 1):
                        copy.start()
                for copy in burst_copies(b):
                    copy.wait()
                t0 = b * SC_BURST
                rows = rows_vmem.at[b % 2]

                @plsc.parallel_loop(0, SC_BURST)
                def _(r):
                    t = t0 + r
                    token = jnp.full((lanes,), t, jnp.int32)
                    gate = [plsc.load_gather(gate_vmem, [jnp.full((lanes,), k, jnp.int32), token])
                            for k in range(TOP_K)]
                    for j in range(0, d, lanes):
                        words = [rows[k, r, pl.ds(j, lanes)] for k in range(TOP_K)]
                        lo_vmem[t, pl.ds(j, lanes)] = tree_sum(
                            [gate[k] * lax.bitcast_convert_type(words[k] << 16, _F32)
                             for k in range(TOP_K)])
                        hi_vmem[t, pl.ds(j, lanes)] = tree_sum(
                            [gate[k] * lax.bitcast_convert_type(words[k] & _U32(0xFFFF0000), _F32)
                             for k in range(TOP_K)])

        tokens = lambda i: (0, i % wins)
        pltpu.emit_pipeline(
            step,
            grid=(PIECES * wins,),
            in_specs=[pl.BlockSpec((TOP_K, SC_WINDOW), index_map=tokens),
                      pl.BlockSpec((TOP_K, SC_WINDOW), index_map=tokens),
                      pl.BlockSpec((1, SC_WINDOW), index_map=lambda i: (i // wins, 0)),
                      pl.BlockSpec((TOP_K, SC_WINDOW), index_map=tokens)],
            out_specs=[pl.BlockSpec((SC_WINDOW, d), index_map=lambda i: (i, 0)),
                       pl.BlockSpec((SC_WINDOW, d), index_map=lambda i: (i, 0))],
            core_axis_name=("core", "subcore"),
            dimension_semantics=(pltpu.PARALLEL,),
        )(ek_hbm, rk_hbm, poff_hbm, gate_hbm, lo_hbm, hi_hbm)

    return pl.kernel(body, out_type=(out_type, out_type), mesh=_sc_mesh(),
                     scratch_types=[pltpu.VMEM((2, TOP_K, SC_BURST, d), _U32),
                                    pltpu.VMEM((TOP_K, SC_WINDOW), jnp.int32),
                                    pltpu.VMEM((N_EXPERTS,), jnp.int32),
                                    pltpu.SemaphoreType.DMA((2,))],
                     compiler_params=pltpu.CompilerParams(needs_layout_passes=False),
                     name="sc_gather_sum")(table, e_k, r_k, start,
                                           _piece_offsets(table.shape[0] // PIECES), gates)


def _expert_kernel(chunk0_ref, nchunk_ref, cnt_ref, total_ref,
                   xs_hbm, wg_ref, wu_ref, wd_ref, o_hbm,
                   xbuf, obuf, sem_in, sem_out):
    e = pl.program_id(0)
    total = total_ref[0]
    ch = EXPERT_ROWS
    ahead = IN_SLOTS - 1

    def in_copy(g):
        slot = g % IN_SLOTS
        return pltpu.make_async_copy(xs_hbm.at[:, pl.ds(g * ch, ch), :], xbuf.at[slot],
                                     sem_in.at[slot])

    def out_copy(g):
        slot = g % OUT_SLOTS
        return pltpu.make_async_copy(obuf.at[slot], o_hbm.at[:, pl.ds(g * ch, ch), :],
                                     sem_out.at[slot])

    @pl.when(e == 0)
    def _():
        obuf[...] = jnp.zeros_like(obuf)
        for g in range(ahead):
            @pl.when(g < total)
            def _():
                in_copy(g).start()


    def chunk(j, carry):
        g = chunk0_ref[e] + j
        in_copy(g).wait()

        @pl.when(g + ahead < total)
        def _():
            in_copy(g + ahead).start()

        @pl.when(g >= OUT_SLOTS)
        def _():
            out_copy(g - OUT_SLOTS).wait()

        islot = g % IN_SLOTS
        oslot = g % OUT_SLOTS
        valid = cnt_ref[e] - j * ch

        def gate_up(s):
            r0 = s * EXPERT_SUB
            w = jnp.concatenate([xbuf[islot, c, r0:r0 + EXPERT_SUB, :] for c in range(PIECES)],
                                axis=1)
            lo, hi = _unpack_halves(w)
            gate = _dot(lo, wg_ref[0, 0, :PACKED, :]) + _dot(hi, wg_ref[0, 0, PACKED:, :])
            up = _dot(lo, wu_ref[0, 0, :PACKED, :]) + _dot(hi, wu_ref[0, 0, PACKED:, :])
            return gate, up

        def down(s, gate_up_pair):
            r0 = s * EXPERT_SUB
            gate, up = gate_up_pair
            y = _pack_halves(_dot(_silu(gate) * up, wd_ref[0, 0]))
            for c in range(PIECES):
                obuf[oslot, c, r0:r0 + EXPERT_SUB, :] = y[:, c * LANES:(c + 1) * LANES]

        n_sub = ch // EXPERT_SUB

        all_subs = valid > ch - EXPERT_SUB

        @pl.when(all_subs)
        def _():
            projected = [gate_up(s) for s in range(n_sub)]
            for s, gu in enumerate(projected):
                down(s, gu)

        @pl.when(jnp.logical_not(all_subs))
        def _():
            for s in range(n_sub - 1):
                @pl.when(s * EXPERT_SUB < valid)
                def _():
                    down(s, gate_up(s))

        out_copy(g).start()
        return carry

    lax.fori_loop(0, nchunk_ref[e], chunk, 0)

    @pl.when(e == pl.num_programs(0) - 1)
    def _():
        for back in range(OUT_SLOTS, 0, -1):
            @pl.when(total >= back)
            def _():
                out_copy(total - back).wait()


def _experts(layer, chunk0, nchunk, counts, total, xs, w_gate, w_up, w_down):
    d = D_MODEL
    ch = EXPERT_ROWS
    wspec_in = pl.BlockSpec((1, 1, d, EXPERT_DIM), lambda e, *_: (layer, e, 0, 0))
    any_spec = pl.BlockSpec(memory_space=pl.ANY)
    grid_spec = pltpu.PrefetchScalarGridSpec(
        num_scalar_prefetch=4,
        grid=(N_EXPERTS,),
        in_specs=[
            any_spec,
            wspec_in,
            wspec_in,
            pl.BlockSpec((1, 1, EXPERT_DIM, d), lambda e, *_: (layer, e, 0, 0)),
        ],
        out_specs=any_spec,
        scratch_shapes=[pltpu.VMEM((IN_SLOTS, PIECES, ch, LANES), _U32),
                        pltpu.VMEM((OUT_SLOTS, PIECES, ch, LANES), _U32),
                        pltpu.SemaphoreType.DMA((IN_SLOTS,)),
                        pltpu.SemaphoreType.DMA((OUT_SLOTS,))],
    )
    return pl.pallas_call(
        _expert_kernel,
        grid_spec=grid_spec,
        out_shape=jax.ShapeDtypeStruct(xs.shape, _U32),
        compiler_params=pltpu.CompilerParams(
            dimension_semantics=("arbitrary",),
            vmem_limit_bytes=VMEM_LIMIT),
        name="moe_experts",
    )(chunk0, nchunk, counts, total, xs, w_gate, w_up, w_down)


def _moe_residual(x, lo_ref, hi_ref, rows, sg_ref, su_ref, sd_ref, g_ref, b_ref):
    xb = x.astype(_BF16)
    routed = jnp.concatenate([lo_ref[c, rows, :] for c in range(PIECES)] +
                             [hi_ref[c, rows, :] for c in range(PIECES)], axis=1)
    h = _silu(_dot(xb, sg_ref[...])) * _dot(xb, su_ref[...])
    shared = _dot(h.astype(_BF16), sd_ref[...])
    return _layer_norm(DEEPNORM_ALPHA * x + (routed + shared), g_ref[...], b_ref[...])


def _combine_kernel(x_ref, lo_ref, hi_ref, sg_ref, su_ref, sd_ref, g_ref, b_ref, *rest):
    o_ref = rest[-1]
    o_ref[...] = _moe_residual(x_ref[...], lo_ref, hi_ref, slice(None),
                               sg_ref, su_ref, sd_ref, g_ref, b_ref)


def _combine_sgu_kernel(x_ref, lo_ref, hi_ref, sg_ref, su_ref, sd_ref, fg_ref, fb_ref,
                        win_ref, bin_ref, lng_ref, lnb_ref, ws_ref, bs_ref, wout_ref,
                        mg_ref, mb_ref, o_ref, op_ref, gated_ref, h_ref):
    for p0 in range(0, x_ref.shape[0], SGU_SUB):
        rows = slice(p0, p0 + SGU_SUB)
        h_ref[rows, :] = _moe_residual(x_ref[rows, :], lo_ref, hi_ref, rows,
                                       sg_ref, su_ref, sd_ref, fg_ref, fb_ref)
    _sgu_kernel(h_ref, win_ref, bin_ref, lng_ref, lnb_ref, ws_ref, bs_ref, wout_ref,
                mg_ref, mb_ref, o_ref, op_ref, gated_ref)


def _combine(x, routed_lo, routed_hi, sh_gate, sh_up, sh_down, ln_g, ln_b, out_parts, part, prev):
    n_part, d = x.shape
    ts = COMBINE_TILE
    steps = n_part // ts
    const2 = lambda i: (0, 0)
    in_specs = [
        pl.BlockSpec((ts, d), lambda i: (i, 0)),
        pl.BlockSpec((PIECES, ts, LANES), lambda i: (0, i, 0)),
        pl.BlockSpec((PIECES, ts, LANES), lambda i: (0, i, 0)),
        pl.BlockSpec((d, EXPERT_DIM), const2),
        pl.BlockSpec((d, EXPERT_DIM), const2),
        pl.BlockSpec((EXPERT_DIM, d), const2),
        pl.BlockSpec((1, d), const2),
        pl.BlockSpec((1, d), const2),
    ]
    args = [x, routed_lo, routed_hi, sh_gate.astype(_BF16), sh_up.astype(_BF16), sh_down.astype(_BF16),
            ln_g.reshape(1, d), ln_b.reshape(1, d)]
    aliases = {}
    if prev is not None:
        in_specs.append(pl.BlockSpec(memory_space=pl.ANY))
        aliases = {len(args): 0}
        args.append(prev)
    return pl.pallas_call(
        _combine_kernel,
        grid=(steps,),
        in_specs=in_specs,
        out_specs=pl.BlockSpec((ts, d), lambda i: (part * steps + i, 0)),
        out_shape=jax.ShapeDtypeStruct((out_parts * n_part, d), _F32),
        input_output_aliases=aliases,
        compiler_params=pltpu.CompilerParams(
            dimension_semantics=("arbitrary",),
            vmem_limit_bytes=VMEM_LIMIT),
        name="moe_combine",
    )(*args)


def _combine_sgu(x, routed_lo, routed_hi, sh_gate, sh_up, sh_down, ffn_g, ffn_b,
                 w_in, b_in, ln_g, ln_b, w_s, b_s, w_out, mix_g, mix_b):
    n_part, d = x.shape
    ts = SGU_TILE
    const2 = lambda i: (0, 0)
    const3 = lambda i: (0, 0, 0)
    causal = jnp.tril(jnp.ones((SGU_CHUNK, SGU_CHUNK), w_s.dtype))
    ws = (w_s * causal[None]).astype(_BF16)
    tile = pl.BlockSpec((ts, d), lambda i: (i, 0))
    ptile = pl.BlockSpec((PIECES, ts, LANES), lambda i: (0, i, 0))
    row = lambda n: pl.BlockSpec((1, n), const2)
    return pl.pallas_call(
        _combine_sgu_kernel,
        grid=(n_part // ts,),
        in_specs=[
            tile, ptile, ptile,
            pl.BlockSpec((d, EXPERT_DIM), const2),
            pl.BlockSpec((d, EXPERT_DIM), const2),
            pl.BlockSpec((EXPERT_DIM, d), const2),
            row(d), row(d),
            pl.BlockSpec((d, 2 * SGU_WIDTH), const2),
            row(2 * SGU_WIDTH), row(SGU_WIDTH), row(SGU_WIDTH),
            pl.BlockSpec((SGU_HEADS, SGU_CHUNK, SGU_CHUNK), const3),
            pl.BlockSpec((SGU_HEADS, SGU_CHUNK, 1), const3),
            pl.BlockSpec((SGU_WIDTH, d), const2),
            row(d), row(d),
        ],
        out_specs=(tile, ptile),
        out_shape=(jax.ShapeDtypeStruct((n_part, d), _F32),
                   jax.ShapeDtypeStruct((PIECES, n_part, LANES), _U32)),
        scratch_shapes=[pltpu.VMEM((ts, SGU_WIDTH), _BF16), pltpu.VMEM((ts, d), _F32)],
        compiler_params=pltpu.CompilerParams(
            dimension_semantics=("arbitrary",),
            vmem_limit_bytes=VMEM_LIMIT),
        name="moe_combine_sgu_mixer",
    )(x, routed_lo, routed_hi, sh_gate.astype(_BF16), sh_up.astype(_BF16), sh_down.astype(_BF16),
      ffn_g.reshape(1, d), ffn_b.reshape(1, d),
      w_in.astype(_BF16), b_in.reshape(1, -1), ln_g.reshape(1, -1), ln_b.reshape(1, -1),
      ws, b_s.reshape(SGU_HEADS, SGU_CHUNK, 1), w_out.astype(_BF16),
      mix_g.reshape(1, d), mix_b.reshape(1, d))


def _moe_dispatch(layer, x, x_packed, w_router, router_bias, w_gate, w_up, w_down):
    n_part, d = x.shape
    bm = EXPERT_ROWS
    n_rows = n_part * TOP_K + N_EXPERTS * bm
    e_k, r_k, g_k, counts = _route(x, w_router, router_bias)
    counts = counts.reshape(N_EXPERTS).astype(jnp.int32)
    padded = (counts + bm - 1) // bm * bm
    padded_end = jnp.cumsum(padded)
    padded_start = padded_end - padded
    xs = _sc_scatter_rows(x_packed.reshape(PIECES * n_part, LANES), e_k, r_k, padded_start, n_rows)
    rows = _experts(layer, padded_start // bm, padded // bm, counts, padded_end[-1:] // bm,
                    xs.reshape(PIECES, n_rows, LANES), w_gate, w_up, w_down)
    return rows.reshape(PIECES * n_rows, LANES), e_k, r_k, padded_start, g_k


def _routed_sums(x, rows, e_k, r_k, start, gates):
    n_part = x.shape[0]
    lo, hi = _sc_gather_sum(rows, e_k, r_k, start, gates)
    return lo.reshape(PIECES, n_part, LANES), hi.reshape(PIECES, n_part, LANES)


def kernel(x, pool_w_in, pool_w_grp, pool_scale, pool_w_out, sgu_w_in, sgu_b_in, sgu_ln_g, sgu_ln_b, sgu_w_s, sgu_b_s, sgu_w_out, ln_mix_g, ln_mix_b, moe_w_router, moe_router_bias, moe_w_gate, moe_w_up, moe_w_down, moe_sh_gate, moe_sh_up, moe_sh_down, ln_ffn_g, ln_ffn_b):
    bsz, seq, d = x.shape
    parts = range(TOKEN_PARTS)

    def dispatch(i, h, h_packed):
        return _moe_dispatch(i, h, h_packed, moe_w_router[i], moe_router_bias[i],
                             moe_w_gate, moe_w_up, moe_w_down)

    mixed = [_pool_layer(x, p, pool_w_in[0], pool_w_grp[0], pool_scale[0], pool_w_out[0],
                         ln_mix_g[0], ln_mix_b[0]) for p in parts]
    mixed = [(h.reshape(-1, d), h_packed) for h, h_packed in mixed]
    routed = [dispatch(0, h, h_packed) for h, h_packed in mixed]
    mixed = [_combine_sgu(mixed[p][0], *_routed_sums(mixed[p][0], *routed[p]),
                          moe_sh_gate[0], moe_sh_up[0], moe_sh_down[0], ln_ffn_g[0], ln_ffn_b[0],
                          sgu_w_in[0], sgu_b_in[0], sgu_ln_g[0], sgu_ln_b[0], sgu_w_s[0],
                          sgu_b_s[0], sgu_w_out[0], ln_mix_g[1], ln_mix_b[1]) for p in parts]
    routed = [dispatch(1, h, h_packed) for h, h_packed in mixed]
    out = None
    for p in parts:
        h = mixed[p][0]
        out = _combine(h, *_routed_sums(h, *routed[p]), moe_sh_gate[1], moe_sh_up[1],
                       moe_sh_down[1], ln_ffn_g[1], ln_ffn_b[1], TOKEN_PARTS, p, out)
    return out.reshape(bsz, seq, d)
```

```python
import functools

import jax
import jax.numpy as jnp
from jax import lax
from jax.experimental import pallas as pl
from jax.experimental.pallas import tpu as pltpu
from jax.experimental.pallas import tpu_sc as plsc

D_MODEL = 1024
DEPTH = 2
POOL_WINDOWS = (2, 4, 8, 16)
POOL_GROUP_DIM = D_MODEL // len(POOL_WINDOWS)
POOL_HALO = 16
SGU_CHUNK = 128
SGU_HEADS = 4
SGU_WIDTH = 2 * D_MODEL
SGU_HEAD_DIM = SGU_WIDTH // SGU_HEADS
N_EXPERTS = 64
TOP_K = 8
N_GROUPS = 8
GROUP_SIZE = N_EXPERTS // N_GROUPS
TOPK_GROUPS = 4
EXPERT_DIM = D_MODEL // 4
ROUTED_SCALE = 2.5
DEEPNORM_ALPHA = (2 * DEPTH) ** 0.25
LN_EPS = 1e-5

LANES = 128
PACKED = D_MODEL // 2
PIECES = PACKED // LANES

POOL_TILE = 1024
POOL_SUB = 256
SGU_TILE = 512
SGU_SUB = 256
ROUTE_TILE = 1024
ROUTE_SUB = 512
INDEX_TILE = 2048
EXPERT_ROWS = 512
EXPERT_SUB = 256
IN_SLOTS = 6
OUT_SLOTS = 4
COMBINE_TILE = 1024
SC_WINDOW = 128
SC_LANES = 16
SC_BURST = 16
TOKEN_PARTS = 2
V7X_VMEM_BYTES = 64 * 1024 * 1024
VMEM_LIMIT = V7X_VMEM_BYTES * 7 // 8

_F32 = jnp.float32
_BF16 = jnp.bfloat16
_U32 = jnp.uint32


def _dot(a, b):
    return jnp.dot(a, b, preferred_element_type=_F32)


def _layer_norm(h, g, b):
    mu = jnp.mean(h, axis=-1, keepdims=True)
    hc = h - mu
    var = jnp.mean(hc * hc, axis=-1, keepdims=True)
    return hc * lax.rsqrt(var + LN_EPS) * g + b


def _silu(x):
    return x * jax.nn.sigmoid(x)


def _gelu_tanh(x):
    c = 0.7978845608028654
    return x * (0.5 + 0.5 * jnp.tanh(x * (c + (c * 0.044715) * (x * x))))


def _pack_halves(v):
    half = v.shape[1] // 2
    lo = lax.bitcast_convert_type(v[:, :half].astype(_BF16).astype(_F32), _U32)
    hi = lax.bitcast_convert_type(v[:, half:].astype(_BF16).astype(_F32), _U32)
    return hi | (lo >> 16)


def _unpack_halves(w):
    lo = lax.bitcast_convert_type(w << 16, _F32)
    hi = lax.bitcast_convert_type(w & _U32(0xFFFF0000), _F32)
    return lo, hi


def _store_pieces(ref, r0, w):
    for c in range(PIECES):
        ref[c, r0:r0 + w.shape[0], :] = w[:, c * LANES:(c + 1) * LANES]


def _pool_kernel(x_ref, win_ref, wgrp_ref, scale_ref, wout_ref, g_ref, b_ref,
                 o_ref, op_ref, zs_ref, y_ref):
    s = pl.program_id(1)
    ts = x_ref.shape[1]

    @pl.when(s == 0)
    def _():
        zs_ref[0:POOL_HALO, :] = jnp.zeros((POOL_HALO, D_MODEL), _F32)

    for p0 in range(0, ts, POOL_SUB):
        z = _dot(x_ref[0, p0:p0 + POOL_SUB, :].astype(_BF16), win_ref[...])
        zs_ref[POOL_HALO + p0:POOL_HALO + p0 + POOL_SUB, :] = z
    for p0 in range(0, ts, POOL_SUB):
        x = x_ref[0, p0:p0 + POOL_SUB, :]
        base = POOL_HALO + p0
        pos = s * ts + p0 + lax.broadcasted_iota(jnp.int32, (POOL_SUB, 1), 0)
        for g, w in enumerate(POOL_WINDOWS):
            c0 = g * POOL_GROUP_DIM
            c1 = c0 + POOL_GROUP_DIM
            zg = zs_ref[base:base + POOL_SUB, c0:c1]
            ext = zs_ref[base - POOL_HALO:base + POOL_SUB, c0:c1]
            m = 1
            while m < w:
                ext = ext + pltpu.roll(ext, m, 0)
                m *= 2
            acc = ext[POOL_HALO:, :]
            cnt = jnp.minimum(pos + 1, w).astype(_F32)
            pooled = acc / cnt - zg
            yg = _dot(pooled.astype(_BF16), wgrp_ref[g]) * scale_ref[:, c0:c1]
            y_ref[p0:p0 + POOL_SUB, c0:c1] = yg.astype(_BF16)
        mix = _dot(y_ref[p0:p0 + POOL_SUB, :], wout_ref[...])
        out = _layer_norm(DEEPNORM_ALPHA * x + mix, g_ref[...], b_ref[...])
        o_ref[0, p0:p0 + POOL_SUB, :] = out
        _store_pieces(op_ref, p0, _pack_halves(out))
    zs_ref[0:POOL_HALO, :] = zs_ref[ts:ts + POOL_HALO, :]


def _pool_layer(x, part, w_in, w_grp, scale, w_out, ln_g, ln_b):
    bsz, seq, d = x.shape
    rows = bsz // TOKEN_PARTS
    ts = POOL_TILE
    steps = seq // ts
    const2 = lambda b, s: (0, 0)
    out_shape = (jax.ShapeDtypeStruct((rows, seq, d), _F32),
                 jax.ShapeDtypeStruct((PIECES, rows * seq, LANES), _U32))
    tile = pl.BlockSpec((1, ts, d), lambda b, s: (b, s, 0))
    ptile = pl.BlockSpec((PIECES, ts, LANES), lambda b, s: (0, b * steps + s, 0))
    return pl.pallas_call(
        _pool_kernel,
        grid=(rows, steps),
        in_specs=[
            pl.BlockSpec((1, ts, d), lambda b, s: (part * rows + b, s, 0)),
            pl.BlockSpec((d, d), const2),
            pl.BlockSpec((len(POOL_WINDOWS), POOL_GROUP_DIM, POOL_GROUP_DIM), lambda b, s: (0, 0, 0)),
            pl.BlockSpec((1, d), const2),
            pl.BlockSpec((d, d), const2),
            pl.BlockSpec((1, d), const2),
            pl.BlockSpec((1, d), const2),
        ],
        out_specs=(tile, ptile),
        out_shape=out_shape,
        scratch_shapes=[pltpu.VMEM((POOL_HALO + ts, d), _F32),
                        pltpu.VMEM((ts, d), _BF16)],
        compiler_params=pltpu.CompilerParams(
            dimension_semantics=("arbitrary", "arbitrary"),
            vmem_limit_bytes=VMEM_LIMIT),
        name="pool_mixer",
    )(x, w_in.astype(_BF16), w_grp.astype(_BF16), scale.reshape(1, d),
      w_out.astype(_BF16), ln_g.reshape(1, d), ln_b.reshape(1, d))


def _sgu_kernel(x_ref, win_ref, bin_ref, lng_ref, lnb_ref, ws_ref, bs_ref, wout_ref,
                g_ref, b_ref, o_ref, op_ref, gated_ref):
    ts = x_ref.shape[0]
    projected = []
    for p0 in range(0, ts, SGU_SUB):
        x = x_ref[p0:p0 + SGU_SUB, :]
        v = _gelu_tanh(_dot(x, win_ref[:, SGU_WIDTH:]) + bin_ref[:, SGU_WIDTH:])
        v = _layer_norm(v, lng_ref[...], lnb_ref[...]).astype(_BF16)
        u = _gelu_tanh(_dot(x, win_ref[:, :SGU_WIDTH]) + bin_ref[:, :SGU_WIDTH])
        projected.append((p0, x, u, v))
    for p0, x, u, v in projected:
        for r0 in range(0, SGU_SUB, SGU_CHUNK):
            for h in range(SGU_HEADS):
                c0 = h * SGU_HEAD_DIM
                sv = _dot(ws_ref[h], v[r0:r0 + SGU_CHUNK, c0:c0 + SGU_HEAD_DIM]) + bs_ref[h]
                gated_ref[p0 + r0:p0 + r0 + SGU_CHUNK, c0:c0 + SGU_HEAD_DIM] = (
                    u[r0:r0 + SGU_CHUNK, c0:c0 + SGU_HEAD_DIM] * sv)
        mix = _dot(gated_ref[p0:p0 + SGU_SUB, :], wout_ref[...])
        out = _layer_norm(DEEPNORM_ALPHA * x + mix, g_ref[...], b_ref[...])
        o_ref[p0:p0 + SGU_SUB, :] = out
        _store_pieces(op_ref, p0, _pack_halves(out))


def _all_max(a):
    return jnp.max(jnp.max(a, axis=0, keepdims=True), axis=1, keepdims=True)


def _all_min(a):
    return jnp.min(jnp.min(a, axis=0, keepdims=True), axis=1, keepdims=True)


def _all_sum(a):
    return jnp.sum(jnp.sum(a, axis=0, keepdims=True), axis=1, keepdims=True)


def _route_kernel(x_ref, whi_ref, wlo_ref, bias_ref, earlier_ref,
                  ek_ref, rk_ref, gk_ref, cnt_ref, carry_ref):
    @pl.when(pl.program_id(0) == 0)
    def _():
        carry_ref[...] = jnp.zeros_like(carry_ref)

    nt = lambda a, b: lax.dot_general(a, b, (((1,), (1,)), ((), ())),
                                      preferred_element_type=_F32)
    logits = []
    for p0 in range(0, x_ref.shape[0], ROUTE_SUB):
        x = x_ref[p0:p0 + ROUTE_SUB, :]
        x_hi = x.astype(_BF16)
        x_lo = (x - x_hi.astype(_F32)).astype(_BF16)
        logits.append(nt(whi_ref[...], x_hi) + (nt(whi_ref[...], x_lo) + nt(wlo_ref[...], x_hi)))
    for n, sub_logits in enumerate(logits):
        _route_sub_tile(sub_logits, n * ROUTE_SUB, bias_ref, earlier_ref,
                        ek_ref, rk_ref, gk_ref, carry_ref)
    cnt_ref[...] = carry_ref[...]


def _route_sub_tile(logits, p0, bias_ref, earlier_ref, ek_ref, rk_ref, gk_ref, carry_ref):
    ts = ROUTE_SUB
    scores = jax.nn.sigmoid(logits).reshape(N_GROUPS, GROUP_SIZE, ts)
    biased = scores + bias_ref[...]
    neg_inf = jnp.float32(-jnp.inf)
    shape3 = (N_GROUPS, GROUP_SIZE, ts)
    in_grp = lax.broadcasted_iota(jnp.int32, shape3, 1)
    grp = lax.broadcasted_iota(jnp.int32, shape3, 0)
    eid = grp * GROUP_SIZE + in_grp

    m1 = jnp.max(biased, axis=1, keepdims=True)
    first1 = jnp.min(jnp.where(biased == m1, in_grp, GROUP_SIZE), axis=1, keepdims=True)
    m2 = jnp.max(jnp.where(in_grp == first1, neg_inf, biased), axis=1, keepdims=True)
    gscore = m1 + m2

    gid = lax.broadcasted_iota(jnp.int32, (N_GROUPS, 1, ts), 0)
    gsel = jnp.zeros((N_GROUPS, 1, ts), jnp.bool_)
    for _ in range(TOPK_GROUPS):
        m = jnp.max(gscore, axis=0, keepdims=True)
        first = jnp.min(jnp.where(gscore == m, gid, N_GROUPS), axis=0, keepdims=True)
        pick = gid == first
        gsel = jnp.logical_or(gsel, pick)
        gscore = jnp.where(pick, neg_inf, gscore)

    masked = jnp.where(gsel, biased, neg_inf)
    picked_any = jnp.zeros(shape3, jnp.bool_)
    e_k, s_k = [], []
    for _ in range(TOP_K):
        m = _all_max(masked)
        first = _all_min(jnp.where(masked == m, eid, N_EXPERTS))
        pick = eid == first
        picked_any = jnp.logical_or(picked_any, pick)
        masked = jnp.where(pick, neg_inf, masked)
        e_k.append(first)
        s_k.append(_all_sum(jnp.where(pick, scores, 0.0)))
    denom = s_k[0]
    for k in range(1, TOP_K):
        denom = denom + s_k[k]

    sel = picked_any.astype(_F32).reshape(N_EXPERTS, ts)
    rank = _dot(sel.astype(_BF16), earlier_ref[...]) + carry_ref[...]
    rank3 = rank.reshape(shape3)
    carry_ref[...] += jnp.sum(sel, axis=1, keepdims=True)

    for k in range(TOP_K):
        r = _all_sum(jnp.where(eid == e_k[k], rank3, 0.0))
        ek_ref[k:k + 1, p0:p0 + ts] = e_k[k].reshape(1, ts)
        rk_ref[k:k + 1, p0:p0 + ts] = r.reshape(1, ts).astype(jnp.int32)
        gk_ref[k:k + 1, p0:p0 + ts] = (s_k[k] / denom * ROUTED_SCALE).reshape(1, ts)


def _route(x, w_router, router_bias):
    n_part, d = x.shape
    ts = ROUTE_TILE
    steps = n_part // ts
    out_shape = (jax.ShapeDtypeStruct((TOP_K, n_part), jnp.int32),
                 jax.ShapeDtypeStruct((TOP_K, n_part), jnp.int32),
                 jax.ShapeDtypeStruct((TOP_K, n_part), _F32),
                 jax.ShapeDtypeStruct((N_EXPERTS, 1), _F32))
    kspec = pl.BlockSpec((TOP_K, ts), lambda i: (0, i))
    const2 = lambda i: (0, 0)
    w_t = w_router.T
    w_hi = w_t.astype(_BF16)
    w_lo = (w_t - w_hi.astype(_F32)).astype(_BF16)
    pos = jnp.arange(ROUTE_SUB, dtype=jnp.int32)
    earlier = (pos[:, None] < pos[None, :]).astype(_BF16)
    return pl.pallas_call(
        _route_kernel,
        grid=(steps,),
        in_specs=[
            pl.BlockSpec((ts, d), lambda i: (i, 0)),
            pl.BlockSpec((N_EXPERTS, d), const2),
            pl.BlockSpec((N_EXPERTS, d), const2),
            pl.BlockSpec((N_GROUPS, GROUP_SIZE, 1), lambda i: (0, 0, 0)),
            pl.BlockSpec((ROUTE_SUB, ROUTE_SUB), const2),
        ],
        out_specs=(kspec, kspec, kspec, pl.BlockSpec((N_EXPERTS, 1), const2)),
        out_shape=out_shape,
        scratch_shapes=[pltpu.VMEM((N_EXPERTS, 1), _F32)],
        compiler_params=pltpu.CompilerParams(
            dimension_semantics=("arbitrary",),
            vmem_limit_bytes=VMEM_LIMIT),
        name="moe_route",
    )(x, w_hi, w_lo, router_bias.reshape(N_GROUPS, GROUP_SIZE, 1), earlier)


def _sc_mesh():
    return plsc.VectorSubcoreMesh(core_axis_name="core", subcore_axis_name="subcore")


def _sc_scatter_rows(src, idx, n_out):
    d = src.shape[1]
    wins = idx.shape[1] // SC_WINDOW

    def body(src_hbm, idx_hbm, out_hbm, sem):
        def step(src_vmem, idx_vmem):
            copies = [pltpu.make_async_copy(src_vmem, out_hbm.at[idx_vmem.at[k]], sem)
                      for k in range(TOP_K)]
            for copy in copies:
                copy.start()
            for copy in copies:
                copy.wait()

        pltpu.emit_pipeline(
            step,
            grid=(src.shape[0] // SC_WINDOW,),
            in_specs=[pl.BlockSpec((SC_WINDOW, d), index_map=lambda i: (i, 0)),
                      pl.BlockSpec((TOP_K, SC_WINDOW), index_map=lambda i: (i // wins, i % wins))],
            out_specs=[],
            core_axis_name=("core", "subcore"),
            dimension_semantics=(pltpu.PARALLEL,),
        )(src_hbm, idx_hbm)

    return pl.kernel(body, out_type=jax.ShapeDtypeStruct((n_out, d), src.dtype),
                     mesh=_sc_mesh(), scratch_types=[pltpu.SemaphoreType.DMA],
                     name="sc_scatter_rows")(src, idx)


def _sc_gather_sum(table, idx, gates):
    d = table.shape[1]
    n_tok = idx.shape[1]
    wins = n_tok // SC_WINDOW
    lanes = SC_LANES
    out_type = jax.ShapeDtypeStruct((PIECES * n_tok, d), _F32)

    def tree_sum(terms):
        while len(terms) > 1:
            terms = [terms[i] + terms[i + 1] for i in range(0, len(terms), 2)]
        return terms[0]

    def body(table_hbm, idx_hbm, gate_hbm, lo_hbm, hi_hbm, rows_vmem, sem):
        def step(idx_vmem, gate_vmem, lo_vmem, hi_vmem):
            def burst_copies(b):
                return [pltpu.make_async_copy(
                    table_hbm.at[idx_vmem.at[k, pl.ds(b * SC_BURST, SC_BURST)]],
                    rows_vmem.at[b % 2, k], sem.at[b % 2]) for k in range(TOP_K)]

            n_bursts = SC_WINDOW // SC_BURST
            for copy in burst_copies(0):
                copy.start()
            for b in range(n_bursts):
                if b + 1 < n_bursts:
                    for copy in burst_copies(b + 1):
                        copy.start()
                for copy in burst_copies(b):
                    copy.wait()
                t0 = b * SC_BURST
                rows = rows_vmem.at[b % 2]

                @plsc.parallel_loop(0, SC_BURST)
                def _(r):
                    t = t0 + r
                    token = jnp.full((lanes,), t, jnp.int32)
                    gate = [plsc.load_gather(gate_vmem, [jnp.full((lanes,), k, jnp.int32), token])
                            for k in range(TOP_K)]
                    for j in range(0, d, lanes):
                        words = [rows[k, r, pl.ds(j, lanes)] for k in range(TOP_K)]
                        lo_vmem[t, pl.ds(j, lanes)] = tree_sum(
                            [gate[k] * lax.bitcast_convert_type(words[k] << 16, _F32)
                             for k in range(TOP_K)])
                        hi_vmem[t, pl.ds(j, lanes)] = tree_sum(
                            [gate[k] * lax.bitcast_convert_type(words[k] & _U32(0xFFFF0000), _F32)
                             for k in range(TOP_K)])

        window = lambda i: (i // wins, i % wins)
        pltpu.emit_pipeline(
            step,
            grid=(PIECES * wins,),
            in_specs=[pl.BlockSpec((TOP_K, SC_WINDOW), index_map=window),
                      pl.BlockSpec((TOP_K, SC_WINDOW), index_map=lambda i: (0, i % wins))],
            out_specs=[pl.BlockSpec((SC_WINDOW, d), index_map=lambda i: (i, 0)),
                       pl.BlockSpec((SC_WINDOW, d), index_map=lambda i: (i, 0))],
            core_axis_name=("core", "subcore"),
            dimension_semantics=(pltpu.PARALLEL,),
        )(idx_hbm, gate_hbm, lo_hbm, hi_hbm)

    return pl.kernel(body, out_type=(out_type, out_type), mesh=_sc_mesh(),
                     scratch_types=[pltpu.VMEM((2, TOP_K, SC_BURST, d), _U32),
                                    pltpu.SemaphoreType.DMA((2,))],
                     compiler_params=pltpu.CompilerParams(needs_layout_passes=False),
                     name="sc_gather_sum")(table, idx, gates)


def _index_kernel(start_ref, ek_ref, rk_ref, idx_ref, *, n_rows):
    e = ek_ref[...]
    start = jnp.zeros(e.shape, jnp.int32)
    for ex in range(N_EXPERTS):
        start = jnp.where(e == ex, start_ref[ex], start)
    dest = start + rk_ref[...]
    for c in range(PIECES):
        idx_ref[c] = dest + c * n_rows


def _row_indices(padded_start, e_k, r_k, n_rows):
    n_tok = e_k.shape[1]
    ts = INDEX_TILE
    kspec = pl.BlockSpec((TOP_K, ts), lambda i, st: (0, i))
    grid_spec = pltpu.PrefetchScalarGridSpec(
        num_scalar_prefetch=1,
        grid=(n_tok // ts,),
        in_specs=[kspec, kspec],
        out_specs=pl.BlockSpec((PIECES, TOP_K, ts), lambda i, st: (0, 0, i)),
    )
    return pl.pallas_call(
        functools.partial(_index_kernel, n_rows=n_rows),
        grid_spec=grid_spec,
        out_shape=jax.ShapeDtypeStruct((PIECES, TOP_K, n_tok), jnp.int32),
        compiler_params=pltpu.CompilerParams(dimension_semantics=("arbitrary",)),
        name="moe_row_indices",
    )(padded_start, e_k, r_k)


def _expert_kernel(chunk0_ref, nchunk_ref, cnt_ref, total_ref,
                   xs_hbm, wg_ref, wu_ref, wd_ref, o_hbm,
                   xbuf, obuf, sem_in, sem_out):
    e = pl.program_id(0)
    total = total_ref[0]
    ch = EXPERT_ROWS
    ahead = IN_SLOTS - 1

    def in_copy(g):
        slot = g % IN_SLOTS
        return pltpu.make_async_copy(xs_hbm.at[:, pl.ds(g * ch, ch), :], xbuf.at[slot],
                                     sem_in.at[slot])

    def out_copy(g):
        slot = g % OUT_SLOTS
        return pltpu.make_async_copy(obuf.at[slot], o_hbm.at[:, pl.ds(g * ch, ch), :],
                                     sem_out.at[slot])

    @pl.when(e == 0)
    def _():
        obuf[...] = jnp.zeros_like(obuf)
        for g in range(ahead):
            @pl.when(g < total)
            def _():
                in_copy(g).start()


    def chunk(j, carry):
        g = chunk0_ref[e] + j
        in_copy(g).wait()

        @pl.when(g + ahead < total)
        def _():
            in_copy(g + ahead).start()

        @pl.when(g >= OUT_SLOTS)
        def _():
            out_copy(g - OUT_SLOTS).wait()

        islot = g % IN_SLOTS
        oslot = g % OUT_SLOTS
        valid = cnt_ref[e] - j * ch

        def gate_up(s):
            r0 = s * EXPERT_SUB
            w = jnp.concatenate([xbuf[islot, c, r0:r0 + EXPERT_SUB, :] for c in range(PIECES)],
                                axis=1)
            lo, hi = _unpack_halves(w)
            gate = _dot(lo, wg_ref[0, 0, :PACKED, :]) + _dot(hi, wg_ref[0, 0, PACKED:, :])
            up = _dot(lo, wu_ref[0, 0, :PACKED, :]) + _dot(hi, wu_ref[0, 0, PACKED:, :])
            return gate, up

        def down(s, gate_up_pair):
            r0 = s * EXPERT_SUB
            gate, up = gate_up_pair
            y = _pack_halves(_dot(_silu(gate) * up, wd_ref[0, 0]))
            for c in range(PIECES):
                obuf[oslot, c, r0:r0 + EXPERT_SUB, :] = y[:, c * LANES:(c + 1) * LANES]

        n_sub = ch // EXPERT_SUB

        all_subs = valid > ch - EXPERT_SUB

        @pl.when(all_subs)
        def _():
            projected = [gate_up(s) for s in range(n_sub)]
            for s, gu in enumerate(projected):
                down(s, gu)

        @pl.when(jnp.logical_not(all_subs))
        def _():
            for s in range(n_sub - 1):
                @pl.when(s * EXPERT_SUB < valid)
                def _():
                    down(s, gate_up(s))

        out_copy(g).start()
        return carry

    lax.fori_loop(0, nchunk_ref[e], chunk, 0)

    @pl.when(e == pl.num_programs(0) - 1)
    def _():
        for back in range(OUT_SLOTS, 0, -1):
            @pl.when(total >= back)
            def _():
                out_copy(total - back).wait()


def _experts(layer, chunk0, nchunk, counts, total, xs, w_gate, w_up, w_down):
    d = D_MODEL
    ch = EXPERT_ROWS
    wspec_in = pl.BlockSpec((1, 1, d, EXPERT_DIM), lambda e, *_: (layer, e, 0, 0))
    any_spec = pl.BlockSpec(memory_space=pl.ANY)
    grid_spec = pltpu.PrefetchScalarGridSpec(
        num_scalar_prefetch=4,
        grid=(N_EXPERTS,),
        in_specs=[
            any_spec,
            wspec_in,
            wspec_in,
            pl.BlockSpec((1, 1, EXPERT_DIM, d), lambda e, *_: (layer, e, 0, 0)),
        ],
        out_specs=any_spec,
        scratch_shapes=[pltpu.VMEM((IN_SLOTS, PIECES, ch, LANES), _U32),
                        pltpu.VMEM((OUT_SLOTS, PIECES, ch, LANES), _U32),
                        pltpu.SemaphoreType.DMA((IN_SLOTS,)),
                        pltpu.SemaphoreType.DMA((OUT_SLOTS,))],
    )
    return pl.pallas_call(
        _expert_kernel,
        grid_spec=grid_spec,
        out_shape=jax.ShapeDtypeStruct(xs.shape, _U32),
        compiler_params=pltpu.CompilerParams(
            dimension_semantics=("arbitrary",),
            vmem_limit_bytes=VMEM_LIMIT),
        name="moe_experts",
    )(chunk0, nchunk, counts, total, xs, w_gate, w_up, w_down)


def _moe_residual(x, lo_ref, hi_ref, rows, sg_ref, su_ref, sd_ref, g_ref, b_ref):
    xb = x.astype(_BF16)
    routed = jnp.concatenate([lo_ref[c, rows, :] for c in range(PIECES)] +
                             [hi_ref[c, rows, :] for c in range(PIECES)], axis=1)
    h = _silu(_dot(xb, sg_ref[...])) * _dot(xb, su_ref[...])
    shared = _dot(h.astype(_BF16), sd_ref[...])
    return _layer_norm(DEEPNORM_ALPHA * x + (routed + shared), g_ref[...], b_ref[...])


def _combine_kernel(x_ref, lo_ref, hi_ref, sg_ref, su_ref, sd_ref, g_ref, b_ref, *rest):
    o_ref = rest[-1]
    o_ref[...] = _moe_residual(x_ref[...], lo_ref, hi_ref, slice(None),
                               sg_ref, su_ref, sd_ref, g_ref, b_ref)


def _combine_sgu_kernel(x_ref, lo_ref, hi_ref, sg_ref, su_ref, sd_ref, fg_ref, fb_ref,
                        win_ref, bin_ref, lng_ref, lnb_ref, ws_ref, bs_ref, wout_ref,
                        mg_ref, mb_ref, o_ref, op_ref, gated_ref, h_ref):
    for p0 in range(0, x_ref.shape[0], SGU_SUB):
        rows = slice(p0, p0 + SGU_SUB)
        h_ref[rows, :] = _moe_residual(x_ref[rows, :], lo_ref, hi_ref, rows,
                                       sg_ref, su_ref, sd_ref, fg_ref, fb_ref)
    _sgu_kernel(h_ref, win_ref, bin_ref, lng_ref, lnb_ref, ws_ref, bs_ref, wout_ref,
                mg_ref, mb_ref, o_ref, op_ref, gated_ref)


def _combine(x, routed_lo, routed_hi, sh_gate, sh_up, sh_down, ln_g, ln_b, out_parts, part, prev):
    n_part, d = x.shape
    ts = COMBINE_TILE
    steps = n_part // ts
    const2 = lambda i: (0, 0)
    in_specs = [
        pl.BlockSpec((ts, d), lambda i: (i, 0)),
        pl.BlockSpec((PIECES, ts, LANES), lambda i: (0, i, 0)),
        pl.BlockSpec((PIECES, ts, LANES), lambda i: (0, i, 0)),
        pl.BlockSpec((d, EXPERT_DIM), const2),
        pl.BlockSpec((d, EXPERT_DIM), const2),
        pl.BlockSpec((EXPERT_DIM, d), const2),
        pl.BlockSpec((1, d), const2),
        pl.BlockSpec((1, d), const2),
    ]
    args = [x, routed_lo, routed_hi, sh_gate.astype(_BF16), sh_up.astype(_BF16), sh_down.astype(_BF16),
            ln_g.reshape(1, d), ln_b.reshape(1, d)]
    aliases = {}
    if prev is not None:
        in_specs.append(pl.BlockSpec(memory_space=pl.ANY))
        aliases = {len(args): 0}
        args.append(prev)
    return pl.pallas_call(
        _combine_kernel,
        grid=(steps,),
        in_specs=in_specs,
        out_specs=pl.BlockSpec((ts, d), lambda i: (part * steps + i, 0)),
        out_shape=jax.ShapeDtypeStruct((out_parts * n_part, d), _F32),
        input_output_aliases=aliases,
        compiler_params=pltpu.CompilerParams(
            dimension_semantics=("arbitrary",),
            vmem_limit_bytes=VMEM_LIMIT),
        name="moe_combine",
    )(*args)


def _combine_sgu(x, routed_lo, routed_hi, sh_gate, sh_up, sh_down, ffn_g, ffn_b,
                 w_in, b_in, ln_g, ln_b, w_s, b_s, w_out, mix_g, mix_b):
    n_part, d = x.shape
    ts = SGU_TILE
    const2 = lambda i: (0, 0)
    const3 = lambda i: (0, 0, 0)
    causal = jnp.tril(jnp.ones((SGU_CHUNK, SGU_CHUNK), w_s.dtype))
    ws = (w_s * causal[None]).astype(_BF16)
    tile = pl.BlockSpec((ts, d), lambda i: (i, 0))
    ptile = pl.BlockSpec((PIECES, ts, LANES), lambda i: (0, i, 0))
    row = lambda n: pl.BlockSpec((1, n), const2)
    return pl.pallas_call(
        _combine_sgu_kernel,
        grid=(n_part // ts,),
        in_specs=[
            tile, ptile, ptile,
            pl.BlockSpec((d, EXPERT_DIM), const2),
            pl.BlockSpec((d, EXPERT_DIM), const2),
            pl.BlockSpec((EXPERT_DIM, d), const2),
            row(d), row(d),
            pl.BlockSpec((d, 2 * SGU_WIDTH), const2, pipeline_mode=pl.Buffered(1)),
            row(2 * SGU_WIDTH), row(SGU_WIDTH), row(SGU_WIDTH),
            pl.BlockSpec((SGU_HEADS, SGU_CHUNK, SGU_CHUNK), const3),
            pl.BlockSpec((SGU_HEADS, SGU_CHUNK, 1), const3),
            pl.BlockSpec((SGU_WIDTH, d), const2, pipeline_mode=pl.Buffered(1)),
            row(d), row(d),
        ],
        out_specs=(tile, ptile),
        out_shape=(jax.ShapeDtypeStruct((n_part, d), _F32),
                   jax.ShapeDtypeStruct((PIECES, n_part, LANES), _U32)),
        scratch_shapes=[pltpu.VMEM((ts, SGU_WIDTH), _F32), pltpu.VMEM((ts, d), _F32)],
        compiler_params=pltpu.CompilerParams(
            dimension_semantics=("arbitrary",),
            vmem_limit_bytes=VMEM_LIMIT),
        name="moe_combine_sgu_mixer",
    )(x, routed_lo, routed_hi, sh_gate.astype(_BF16), sh_up.astype(_BF16), sh_down.astype(_BF16),
      ffn_g.reshape(1, d), ffn_b.reshape(1, d),
      w_in, b_in.reshape(1, -1), ln_g.reshape(1, -1), ln_b.reshape(1, -1),
      ws, b_s.reshape(SGU_HEADS, SGU_CHUNK, 1), w_out,
      mix_g.reshape(1, d), mix_b.reshape(1, d))


def _moe_dispatch(layer, x, x_packed, w_router, router_bias, w_gate, w_up, w_down):
    n_part, d = x.shape
    bm = EXPERT_ROWS
    n_rows = n_part * TOP_K + N_EXPERTS * bm
    e_k, r_k, g_k, counts = _route(x, w_router, router_bias)
    counts = counts.reshape(N_EXPERTS).astype(jnp.int32)
    padded = (counts + bm - 1) // bm * bm
    padded_end = jnp.cumsum(padded)
    padded_start = padded_end - padded
    idx = _row_indices(padded_start, e_k, r_k, n_rows).reshape(PIECES * TOP_K, n_part)
    xs = _sc_scatter_rows(x_packed.reshape(PIECES * n_part, LANES), idx, PIECES * n_rows)
    rows = _experts(layer, padded_start // bm, padded // bm, counts, padded_end[-1:] // bm,
                    xs.reshape(PIECES, n_rows, LANES), w_gate, w_up, w_down)
    return rows.reshape(PIECES * n_rows, LANES), idx, g_k


def _routed_sums(x, rows, idx, gates):
    n_part = x.shape[0]
    lo, hi = _sc_gather_sum(rows, idx, gates)
    return lo.reshape(PIECES, n_part, LANES), hi.reshape(PIECES, n_part, LANES)


def kernel(x, pool_w_in, pool_w_grp, pool_scale, pool_w_out, sgu_w_in, sgu_b_in, sgu_ln_g, sgu_ln_b, sgu_w_s, sgu_b_s, sgu_w_out, ln_mix_g, ln_mix_b, moe_w_router, moe_router_bias, moe_w_gate, moe_w_up, moe_w_down, moe_sh_gate, moe_sh_up, moe_sh_down, ln_ffn_g, ln_ffn_b):
    bsz, seq, d = x.shape
    parts = range(TOKEN_PARTS)

    def dispatch(i, h, h_packed):
        return _moe_dispatch(i, h, h_packed, moe_w_router[i], moe_router_bias[i],
                             moe_w_gate, moe_w_up, moe_w_down)

    mixed = [_pool_layer(x, p, pool_w_in[0], pool_w_grp[0], pool_scale[0], pool_w_out[0],
                         ln_mix_g[0], ln_mix_b[0]) for p in parts]
    mixed = [(h.reshape(-1, d), h_packed) for h, h_packed in mixed]
    routed = [dispatch(0, h, h_packed) for h, h_packed in mixed]
    mixed = [_combine_sgu(mixed[p][0], *_routed_sums(mixed[p][0], *routed[p]),
                          moe_sh_gate[0], moe_sh_up[0], moe_sh_down[0], ln_ffn_g[0], ln_ffn_b[0],
                          sgu_w_in[0], sgu_b_in[0], sgu_ln_g[0], sgu_ln_b[0], sgu_w_s[0],
                          sgu_b_s[0], sgu_w_out[0], ln_mix_g[1], ln_mix_b[1]) for p in parts]
    routed = [dispatch(1, h, h_packed) for h, h_packed in mixed]
    out = None
    for p in parts:
        h = mixed[p][0]
        out = _combine(h, *_routed_sums(h, *routed[p]), moe_sh_gate[1], moe_sh_up[1],
                       moe_sh_down[1], ln_ffn_g[1], ln_ffn_b[1], TOKEN_PARTS, p, out)
    return out.reshape(bsz, seq, d)
```

```python
import functools

import jax
import jax.numpy as jnp
from jax import lax
from jax.experimental import pallas as pl
from jax.experimental.pallas import tpu as pltpu
from jax.experimental.pallas import tpu_sc as plsc

D_MODEL = 1024
DEPTH = 2
POOL_WINDOWS = (2, 4, 8, 16)
POOL_GROUP_DIM = D_MODEL // len(POOL_WINDOWS)
POOL_HALO = 16
SGU_CHUNK = 128
SGU_HEADS = 4
SGU_WIDTH = 2 * D_MODEL
SGU_HEAD_DIM = SGU_WIDTH // SGU_HEADS
N_EXPERTS = 64
TOP_K = 8
N_GROUPS = 8
GROUP_SIZE = N_EXPERTS // N_GROUPS
TOPK_GROUPS = 4
EXPERT_DIM = D_MODEL // 4
ROUTED_SCALE = 2.5
DEEPNORM_ALPHA = (2 * DEPTH) ** 0.25
LN_EPS = 1e-5

LANES = 128
PACKED = D_MODEL // 2
PIECES = PACKED // LANES

POOL_TILE = 1024
POOL_SUB = 256
SGU_TILE = 512
SGU_SUB = 256
ROUTE_TILE = 1024
ROUTE_SUB = 512
INDEX_TILE = 2048
EXPERT_ROWS = 512
EXPERT_SUB = 256
IN_SLOTS = 6
OUT_SLOTS = 4
COMBINE_TILE = 1024
SC_WINDOW = 128
SC_LANES = 16
SC_BURST = 16
TOKEN_PARTS = 2
FINAL_SLICES = 2
V7X_VMEM_BYTES = 64 * 1024 * 1024
VMEM_LIMIT = V7X_VMEM_BYTES * 7 // 8

_F32 = jnp.float32
_BF16 = jnp.bfloat16
_U32 = jnp.uint32


def _dot(a, b):
    return jnp.dot(a, b, preferred_element_type=_F32)


def _layer_norm(h, g, b):
    mu = jnp.mean(h, axis=-1, keepdims=True)
    hc = h - mu
    var = jnp.mean(hc * hc, axis=-1, keepdims=True)
    return hc * lax.rsqrt(var + LN_EPS) * g + b


def _silu(x):
    return x * jax.nn.sigmoid(x)


def _gelu_tanh(x):
    c = 0.7978845608028654
    return x * (0.5 + 0.5 * jnp.tanh(x * (c + (c * 0.044715) * (x * x))))


def _pack_halves(v):
    half = v.shape[1] // 2
    lo = lax.bitcast_convert_type(v[:, :half].astype(_BF16).astype(_F32), _U32)
    hi = lax.bitcast_convert_type(v[:, half:].astype(_BF16).astype(_F32), _U32)
    return hi | (lo >> 16)


def _unpack_halves(w):
    lo = lax.bitcast_convert_type(w << 16, _F32)
    hi = lax.bitcast_convert_type(w & _U32(0xFFFF0000), _F32)
    return lo, hi


def _store_pieces(ref, r0, w):
    for c in range(PIECES):
        ref[c, r0:r0 + w.shape[0], :] = w[:, c * LANES:(c + 1) * LANES]


def _pool_kernel(x_ref, win_ref, wgrp_ref, scale_ref, wout_ref, g_ref, b_ref,
                 o_ref, op_ref, zs_ref, y_ref):
    s = pl.program_id(1)
    ts = x_ref.shape[1]

    @pl.when(s == 0)
    def _():
        zs_ref[0:POOL_HALO, :] = jnp.zeros((POOL_HALO, D_MODEL), _F32)

    for p0 in range(0, ts, POOL_SUB):
        z = _dot(x_ref[0, p0:p0 + POOL_SUB, :].astype(_BF16), win_ref[...])
        zs_ref[POOL_HALO + p0:POOL_HALO + p0 + POOL_SUB, :] = z
    for p0 in range(0, ts, POOL_SUB):
        x = x_ref[0, p0:p0 + POOL_SUB, :]
        base = POOL_HALO + p0
        pos = s * ts + p0 + lax.broadcasted_iota(jnp.int32, (POOL_SUB, 1), 0)
        for g, w in enumerate(POOL_WINDOWS):
            c0 = g * POOL_GROUP_DIM
            c1 = c0 + POOL_GROUP_DIM
            zg = zs_ref[base:base + POOL_SUB, c0:c1]
            ext = zs_ref[base - POOL_HALO:base + POOL_SUB, c0:c1]
            m = 1
            while m < w:
                ext = ext + pltpu.roll(ext, m, 0)
                m *= 2
            acc = ext[POOL_HALO:, :]
            cnt = jnp.minimum(pos + 1, w).astype(_F32)
            pooled = acc / cnt - zg
            yg = _dot(pooled.astype(_BF16), wgrp_ref[g]) * scale_ref[:, c0:c1]
            y_ref[p0:p0 + POOL_SUB, c0:c1] = yg.astype(_BF16)
        mix = _dot(y_ref[p0:p0 + POOL_SUB, :], wout_ref[...])
        out = _layer_norm(DEEPNORM_ALPHA * x + mix, g_ref[...], b_ref[...])
        o_ref[0, p0:p0 + POOL_SUB, :] = out
        _store_pieces(op_ref, p0, _pack_halves(out))
    zs_ref[0:POOL_HALO, :] = zs_ref[ts:ts + POOL_HALO, :]


def _pool_layer(x, part, w_in, w_grp, scale, w_out, ln_g, ln_b):
    bsz, seq, d = x.shape
    rows = bsz // TOKEN_PARTS
    ts = POOL_TILE
    steps = seq // ts
    const2 = lambda b, s: (0, 0)
    out_shape = (jax.ShapeDtypeStruct((rows, seq, d), _F32),
                 jax.ShapeDtypeStruct((PIECES, rows * seq, LANES), _U32))
    tile = pl.BlockSpec((1, ts, d), lambda b, s: (b, s, 0))
    ptile = pl.BlockSpec((PIECES, ts, LANES), lambda b, s: (0, b * steps + s, 0))
    return pl.pallas_call(
        _pool_kernel,
        grid=(rows, steps),
        in_specs=[
            pl.BlockSpec((1, ts, d), lambda b, s: (part * rows + b, s, 0)),
            pl.BlockSpec((d, d), const2),
            pl.BlockSpec((len(POOL_WINDOWS), POOL_GROUP_DIM, POOL_GROUP_DIM), lambda b, s: (0, 0, 0)),
            pl.BlockSpec((1, d), const2),
            pl.BlockSpec((d, d), const2),
            pl.BlockSpec((1, d), const2),
            pl.BlockSpec((1, d), const2),
        ],
        out_specs=(tile, ptile),
        out_shape=out_shape,
        scratch_shapes=[pltpu.VMEM((POOL_HALO + ts, d), _F32),
                        pltpu.VMEM((ts, d), _BF16)],
        compiler_params=pltpu.CompilerParams(
            dimension_semantics=("arbitrary", "arbitrary"),
            vmem_limit_bytes=VMEM_LIMIT),
        name="pool_mixer",
    )(x, w_in.astype(_BF16), w_grp.astype(_BF16), scale.reshape(1, d),
      w_out.astype(_BF16), ln_g.reshape(1, d), ln_b.reshape(1, d))


def _sgu_kernel(x_ref, win_ref, bin_ref, lng_ref, lnb_ref, ws_ref, bs_ref, wout_ref,
                g_ref, b_ref, o_ref, op_ref, gated_ref):
    ts = x_ref.shape[0]
    projected = []
    for p0 in range(0, ts, SGU_SUB):
        x = x_ref[p0:p0 + SGU_SUB, :]
        xb = x.astype(_BF16)
        v = _gelu_tanh(_dot(xb, win_ref[:, SGU_WIDTH:]) + bin_ref[:, SGU_WIDTH:])
        v = _layer_norm(v, lng_ref[...], lnb_ref[...]).astype(_BF16)
        u = _gelu_tanh(_dot(xb, win_ref[:, :SGU_WIDTH]) + bin_ref[:, :SGU_WIDTH])
        projected.append((p0, x, u, v))
    for p0, x, u, v in projected:
        for r0 in range(0, SGU_SUB, SGU_CHUNK):
            for h in range(SGU_HEADS):
                c0 = h * SGU_HEAD_DIM
                sv = _dot(ws_ref[h], v[r0:r0 + SGU_CHUNK, c0:c0 + SGU_HEAD_DIM]) + bs_ref[h]
                gated_ref[p0 + r0:p0 + r0 + SGU_CHUNK, c0:c0 + SGU_HEAD_DIM] = (
                    u[r0:r0 + SGU_CHUNK, c0:c0 + SGU_HEAD_DIM] * sv).astype(_BF16)
        mix = _dot(gated_ref[p0:p0 + SGU_SUB, :], wout_ref[...])
        out = _layer_norm(DEEPNORM_ALPHA * x + mix, g_ref[...], b_ref[...])
        o_ref[p0:p0 + SGU_SUB, :] = out
        _store_pieces(op_ref, p0, _pack_halves(out))


def _all_max(a):
    return jnp.max(jnp.max(a, axis=0, keepdims=True), axis=1, keepdims=True)


def _all_min(a):
    return jnp.min(jnp.min(a, axis=0, keepdims=True), axis=1, keepdims=True)


def _all_sum(a):
    return jnp.sum(jnp.sum(a, axis=0, keepdims=True), axis=1, keepdims=True)


def _route_kernel(x_ref, whi_ref, wlo_ref, bias_ref, earlier_ref,
                  ek_ref, rk_ref, gk_ref, cnt_ref, carry_ref):
    @pl.when(pl.program_id(0) == 0)
    def _():
        carry_ref[...] = jnp.zeros_like(carry_ref)

    nt = lambda a, b: lax.dot_general(a, b, (((1,), (1,)), ((), ())),
                                      preferred_element_type=_F32)
    logits = []
    for p0 in range(0, x_ref.shape[0], ROUTE_SUB):
        x = x_ref[p0:p0 + ROUTE_SUB, :]
        x_hi = x.astype(_BF16)
        x_lo = (x - x_hi.astype(_F32)).astype(_BF16)
        logits.append(nt(whi_ref[...], x_hi) + (nt(whi_ref[...], x_lo) + nt(wlo_ref[...], x_hi)))
    for n, sub_logits in enumerate(logits):
        _route_sub_tile(sub_logits, n * ROUTE_SUB, bias_ref, earlier_ref,
                        ek_ref, rk_ref, gk_ref, carry_ref)
    cnt_ref[...] = carry_ref[...]


def _route_sub_tile(logits, p0, bias_ref, earlier_ref, ek_ref, rk_ref, gk_ref, carry_ref):
    ts = ROUTE_SUB
    scores = jax.nn.sigmoid(logits).reshape(N_GROUPS, GROUP_SIZE, ts)
    biased = scores + bias_ref[...]
    neg_inf = jnp.float32(-jnp.inf)
    shape3 = (N_GROUPS, GROUP_SIZE, ts)
    in_grp = lax.broadcasted_iota(jnp.int32, shape3, 1)
    grp = lax.broadcasted_iota(jnp.int32, shape3, 0)
    eid = grp * GROUP_SIZE + in_grp

    m1 = jnp.max(biased, axis=1, keepdims=True)
    first1 = jnp.min(jnp.where(biased == m1, in_grp, GROUP_SIZE), axis=1, keepdims=True)
    m2 = jnp.max(jnp.where(in_grp == first1, neg_inf, biased), axis=1, keepdims=True)
    gscore = m1 + m2

    gid = lax.broadcasted_iota(jnp.int32, (N_GROUPS, 1, ts), 0)
    gsel = jnp.zeros((N_GROUPS, 1, ts), jnp.bool_)
    for _ in range(TOPK_GROUPS):
        m = jnp.max(gscore, axis=0, keepdims=True)
        first = jnp.min(jnp.where(gscore == m, gid, N_GROUPS), axis=0, keepdims=True)
        pick = gid == first
        gsel = jnp.logical_or(gsel, pick)
        gscore = jnp.where(pick, neg_inf, gscore)

    masked = jnp.where(gsel, biased, neg_inf)
    picked_any = jnp.zeros(shape3, jnp.bool_)
    e_k, s_k = [], []
    for _ in range(TOP_K):
        m = _all_max(masked)
        first = _all_min(jnp.where(masked == m, eid, N_EXPERTS))
        pick = eid == first
        picked_any = jnp.logical_or(picked_any, pick)
        masked = jnp.where(pick, neg_inf, masked)
        e_k.append(first)
        s_k.append(_all_sum(jnp.where(pick, scores, 0.0)))
    denom = s_k[0]
    for k in range(1, TOP_K):
        denom = denom + s_k[k]

    sel = picked_any.astype(_F32).reshape(N_EXPERTS, ts)
    rank = _dot(sel.astype(_BF16), earlier_ref[...]) + carry_ref[...]
    rank3 = rank.reshape(shape3)
    carry_ref[...] += jnp.sum(sel, axis=1, keepdims=True)

    for k in range(TOP_K):
        r = _all_sum(jnp.where(eid == e_k[k], rank3, 0.0))
        ek_ref[k:k + 1, p0:p0 + ts] = e_k[k].reshape(1, ts)
        rk_ref[k:k + 1, p0:p0 + ts] = r.reshape(1, ts).astype(jnp.int32)
        gk_ref[k:k + 1, p0:p0 + ts] = (s_k[k] / denom * ROUTED_SCALE).reshape(1, ts)


def _route(x, w_router, router_bias):
    n_part, d = x.shape
    ts = ROUTE_TILE
    steps = n_part // ts
    out_shape = (jax.ShapeDtypeStruct((TOP_K, n_part), jnp.int32),
                 jax.ShapeDtypeStruct((TOP_K, n_part), jnp.int32),
                 jax.ShapeDtypeStruct((TOP_K, n_part), _F32),
                 jax.ShapeDtypeStruct((N_EXPERTS, 1), _F32))
    kspec = pl.BlockSpec((TOP_K, ts), lambda i: (0, i))
    const2 = lambda i: (0, 0)
    w_t = w_router.T
    w_hi = w_t.astype(_BF16)
    w_lo = (w_t - w_hi.astype(_F32)).astype(_BF16)
    pos = jnp.arange(ROUTE_SUB, dtype=jnp.int32)
    earlier = (pos[:, None] < pos[None, :]).astype(_BF16)
    return pl.pallas_call(
        _route_kernel,
        grid=(steps,),
        in_specs=[
            pl.BlockSpec((ts, d), lambda i: (i, 0)),
            pl.BlockSpec((N_EXPERTS, d), const2),
            pl.BlockSpec((N_EXPERTS, d), const2),
            pl.BlockSpec((N_GROUPS, GROUP_SIZE, 1), lambda i: (0, 0, 0)),
            pl.BlockSpec((ROUTE_SUB, ROUTE_SUB), const2),
        ],
        out_specs=(kspec, kspec, kspec, pl.BlockSpec((N_EXPERTS, 1), const2)),
        out_shape=out_shape,
        scratch_shapes=[pltpu.VMEM((N_EXPERTS, 1), _F32)],
        compiler_params=pltpu.CompilerParams(
            dimension_semantics=("arbitrary",),
            vmem_limit_bytes=VMEM_LIMIT),
        name="moe_route",
    )(x, w_hi, w_lo, router_bias.reshape(N_GROUPS, GROUP_SIZE, 1), earlier)


def _sc_mesh():
    return plsc.VectorSubcoreMesh(core_axis_name="core", subcore_axis_name="subcore")


def _sc_scatter_rows(src, idx, n_out):
    d = src.shape[1]
    wins = idx.shape[1] // SC_WINDOW

    def body(src_hbm, idx_hbm, out_hbm, sem):
        def step(src_vmem, idx_vmem):
            copies = [pltpu.make_async_copy(src_vmem, out_hbm.at[idx_vmem.at[k]], sem)
                      for k in range(TOP_K)]
            for copy in copies:
                copy.start()
            for copy in copies:
                copy.wait()

        pltpu.emit_pipeline(
            step,
            grid=(src.shape[0] // SC_WINDOW,),
            in_specs=[pl.BlockSpec((SC_WINDOW, d), index_map=lambda i: (i, 0)),
                      pl.BlockSpec((TOP_K, SC_WINDOW), index_map=lambda i: (i // wins, i % wins))],
            out_specs=[],
            core_axis_name=("core", "subcore"),
            dimension_semantics=(pltpu.PARALLEL,),
        )(src_hbm, idx_hbm)

    return pl.kernel(body, out_type=jax.ShapeDtypeStruct((n_out, d), src.dtype),
                     mesh=_sc_mesh(), scratch_types=[pltpu.SemaphoreType.DMA],
                     name="sc_scatter_rows")(src, idx)


def _sc_gather_sum(table, idx, gates, t0, n_tok):
    d = table.shape[1]
    wins = n_tok // SC_WINDOW
    win0 = t0 // SC_WINDOW
    lanes = SC_LANES
    out_type = jax.ShapeDtypeStruct((PIECES * n_tok, d), _F32)

    def tree_sum(terms):
        while len(terms) > 1:
            terms = [terms[i] + terms[i + 1] for i in range(0, len(terms), 2)]
        return terms[0]

    def body(table_hbm, idx_hbm, gate_hbm, lo_hbm, hi_hbm, rows_vmem, sem):
        def step(idx_vmem, gate_vmem, lo_vmem, hi_vmem):
            def burst_copies(b):
                return [pltpu.make_async_copy(
                    table_hbm.at[idx_vmem.at[k, pl.ds(b * SC_BURST, SC_BURST)]],
                    rows_vmem.at[b % 2, k], sem.at[b % 2]) for k in range(TOP_K)]

            n_bursts = SC_WINDOW // SC_BURST
            for copy in burst_copies(0):
                copy.start()
            for b in range(n_bursts):
                if b + 1 < n_bursts:
                    for copy in burst_copies(b + 1):
                        copy.start()
                for copy in burst_copies(b):
                    copy.wait()
                t0 = b * SC_BURST
                rows = rows_vmem.at[b % 2]

                @plsc.parallel_loop(0, SC_BURST)
                def _(r):
                    t = t0 + r
                    token = jnp.full((lanes,), t, jnp.int32)
                    gate = [plsc.load_gather(gate_vmem, [jnp.full((lanes,), k, jnp.int32), token])
                            for k in range(TOP_K)]
                    for j in range(0, d, lanes):
                        words = [rows[k, r, pl.ds(j, lanes)] for k in range(TOP_K)]
                        lo_vmem[t, pl.ds(j, lanes)] = tree_sum(
                            [gate[k] * lax.bitcast_convert_type(words[k] << 16, _F32)
                             for k in range(TOP_K)])
                        hi_vmem[t, pl.ds(j, lanes)] = tree_sum(
                            [gate[k] * lax.bitcast_convert_type(words[k] & _U32(0xFFFF0000), _F32)
                             for k in range(TOP_K)])

        pltpu.emit_pipeline(
            step,
            grid=(PIECES * wins,),
            in_specs=[pl.BlockSpec((TOP_K, SC_WINDOW),
                                   index_map=lambda i: (i // wins, win0 + i % wins)),
                      pl.BlockSpec((TOP_K, SC_WINDOW), index_map=lambda i: (0, win0 + i % wins))],
            out_specs=[pl.BlockSpec((SC_WINDOW, d), index_map=lambda i: (i, 0)),
                       pl.BlockSpec((SC_WINDOW, d), index_map=lambda i: (i, 0))],
            core_axis_name=("core", "subcore"),
            dimension_semantics=(pltpu.PARALLEL,),
        )(idx_hbm, gate_hbm, lo_hbm, hi_hbm)

    return pl.kernel(body, out_type=(out_type, out_type), mesh=_sc_mesh(),
                     scratch_types=[pltpu.VMEM((2, TOP_K, SC_BURST, d), _U32),
                                    pltpu.SemaphoreType.DMA((2,))],
                     compiler_params=pltpu.CompilerParams(needs_layout_passes=False),
                     name="sc_gather_sum")(table, idx, gates)


def _index_kernel(start_ref, ek_ref, rk_ref, idx_ref, *, n_rows):
    e = ek_ref[...]
    start = jnp.zeros(e.shape, jnp.int32)
    for ex in range(N_EXPERTS):
        start = jnp.where(e == ex, start_ref[ex], start)
    dest = start + rk_ref[...]
    for c in range(PIECES):
        idx_ref[c] = dest + c * n_rows


def _row_indices(padded_start, e_k, r_k, n_rows):
    n_tok = e_k.shape[1]
    ts = INDEX_TILE
    kspec = pl.BlockSpec((TOP_K, ts), lambda i, st: (0, i))
    grid_spec = pltpu.PrefetchScalarGridSpec(
        num_scalar_prefetch=1,
        grid=(n_tok // ts,),
        in_specs=[kspec, kspec],
        out_specs=pl.BlockSpec((PIECES, TOP_K, ts), lambda i, st: (0, 0, i)),
    )
    return pl.pallas_call(
        functools.partial(_index_kernel, n_rows=n_rows),
        grid_spec=grid_spec,
        out_shape=jax.ShapeDtypeStruct((PIECES, TOP_K, n_tok), jnp.int32),
        compiler_params=pltpu.CompilerParams(dimension_semantics=("arbitrary",)),
        name="moe_row_indices",
    )(padded_start, e_k, r_k)


def _expert_kernel(chunk0_ref, nchunk_ref, cnt_ref, total_ref,
                   xs_hbm, wg_ref, wu_ref, wd_ref, o_hbm,
                   xbuf, obuf, sem_in, sem_out):
    e = pl.program_id(0)
    total = total_ref[0]
    ch = EXPERT_ROWS
    ahead = IN_SLOTS - 1

    def in_copy(g):
        slot = g % IN_SLOTS
        return pltpu.make_async_copy(xs_hbm.at[:, pl.ds(g * ch, ch), :], xbuf.at[slot],
                                     sem_in.at[slot])

    def out_copy(g):
        slot = g % OUT_SLOTS
        return pltpu.make_async_copy(obuf.at[slot], o_hbm.at[:, pl.ds(g * ch, ch), :],
                                     sem_out.at[slot])

    @pl.when(e == 0)
    def _():
        obuf[...] = jnp.zeros_like(obuf)
        for g in range(ahead):
            @pl.when(g < total)
            def _():
                in_copy(g).start()


    def chunk(j, carry):
        g = chunk0_ref[e] + j
        in_copy(g).wait()

        @pl.when(g + ahead < total)
        def _():
            in_copy(g + ahead).start()

        @pl.when(g >= OUT_SLOTS)
        def _():
            out_copy(g - OUT_SLOTS).wait()

        islot = g % IN_SLOTS
        oslot = g % OUT_SLOTS
        valid = cnt_ref[e] - j * ch

        def gate_up(s):
            r0 = s * EXPERT_SUB
            w = jnp.concatenate([xbuf[islot, c, r0:r0 + EXPERT_SUB, :] for c in range(PIECES)],
                                axis=1)
            lo, hi = _unpack_halves(w)
            gate = _dot(lo, wg_ref[0, 0, :PACKED, :]) + _dot(hi, wg_ref[0, 0, PACKED:, :])
            up = _dot(lo, wu_ref[0, 0, :PACKED, :]) + _dot(hi, wu_ref[0, 0, PACKED:, :])
            return gate, up

        def down(s, gate_up_pair):
            r0 = s * EXPERT_SUB
            gate, up = gate_up_pair
            y = _pack_halves(_dot(_silu(gate) * up, wd_ref[0, 0]))
            for c in range(PIECES):
                obuf[oslot, c, r0:r0 + EXPERT_SUB, :] = y[:, c * LANES:(c + 1) * LANES]

        n_sub = ch // EXPERT_SUB

        all_subs = valid > ch - EXPERT_SUB

        @pl.when(all_subs)
        def _():
            projected = [gate_up(s) for s in range(n_sub)]
            for s, gu in enumerate(projected):
                down(s, gu)

        @pl.when(jnp.logical_not(all_subs))
        def _():
            for s in range(n_sub - 1):
                @pl.when(s * EXPERT_SUB < valid)
                def _():
                    down(s, gate_up(s))

        out_copy(g).start()
        return carry

    lax.fori_loop(0, nchunk_ref[e], chunk, 0)

    @pl.when(e == pl.num_programs(0) - 1)
    def _():
        for back in range(OUT_SLOTS, 0, -1):
            @pl.when(total >= back)
            def _():
                out_copy(total - back).wait()


def _experts(layer, chunk0, nchunk, counts, total, xs, w_gate, w_up, w_down):
    d = D_MODEL
    ch = EXPERT_ROWS
    wspec_in = pl.BlockSpec((1, 1, d, EXPERT_DIM), lambda e, *_: (layer, e, 0, 0))
    any_spec = pl.BlockSpec(memory_space=pl.ANY)
    grid_spec = pltpu.PrefetchScalarGridSpec(
        num_scalar_prefetch=4,
        grid=(N_EXPERTS,),
        in_specs=[
            any_spec,
            wspec_in,
            wspec_in,
            pl.BlockSpec((1, 1, EXPERT_DIM, d), lambda e, *_: (layer, e, 0, 0)),
        ],
        out_specs=any_spec,
        scratch_shapes=[pltpu.VMEM((IN_SLOTS, PIECES, ch, LANES), _U32),
                        pltpu.VMEM((OUT_SLOTS, PIECES, ch, LANES), _U32),
                        pltpu.SemaphoreType.DMA((IN_SLOTS,)),
                        pltpu.SemaphoreType.DMA((OUT_SLOTS,))],
    )
    return pl.pallas_call(
        _expert_kernel,
        grid_spec=grid_spec,
        out_shape=jax.ShapeDtypeStruct(xs.shape, _U32),
        compiler_params=pltpu.CompilerParams(
            dimension_semantics=("arbitrary",),
            vmem_limit_bytes=VMEM_LIMIT),
        name="moe_experts",
    )(chunk0, nchunk, counts, total, xs, w_gate, w_up, w_down)


def _moe_residual(x, lo_ref, hi_ref, rows, sg_ref, su_ref, sd_ref, g_ref, b_ref):
    xb = x.astype(_BF16)
    routed = jnp.concatenate([lo_ref[c, rows, :] for c in range(PIECES)] +
                             [hi_ref[c, rows, :] for c in range(PIECES)], axis=1)
    h = _silu(_dot(xb, sg_ref[...])) * _dot(xb, su_ref[...])
    shared = _dot(h.astype(_BF16), sd_ref[...])
    return _layer_norm(DEEPNORM_ALPHA * x + (routed + shared), g_ref[...], b_ref[...])


def _combine_kernel(x_ref, lo_ref, hi_ref, sg_ref, su_ref, sd_ref, g_ref, b_ref, *rest):
    o_ref = rest[-1]
    o_ref[...] = _moe_residual(x_ref[...], lo_ref, hi_ref, slice(None),
                               sg_ref, su_ref, sd_ref, g_ref, b_ref)


def _combine_sgu_kernel(x_ref, lo_ref, hi_ref, sg_ref, su_ref, sd_ref, fg_ref, fb_ref,
                        win_ref, bin_ref, lng_ref, lnb_ref, ws_ref, bs_ref, wout_ref,
                        mg_ref, mb_ref, o_ref, op_ref, gated_ref, h_ref):
    for p0 in range(0, x_ref.shape[0], SGU_SUB):
        rows = slice(p0, p0 + SGU_SUB)
        h_ref[rows, :] = _moe_residual(x_ref[rows, :], lo_ref, hi_ref, rows,
                                       sg_ref, su_ref, sd_ref, fg_ref, fb_ref)
    _sgu_kernel(h_ref, win_ref, bin_ref, lng_ref, lnb_ref, ws_ref, bs_ref, wout_ref,
                mg_ref, mb_ref, o_ref, op_ref, gated_ref)


def _combine(x, routed_lo, routed_hi, sh_gate, sh_up, sh_down, ln_g, ln_b,
             out_parts, part, t0, prev):
    n_part, d = x.shape
    ts = COMBINE_TILE
    steps = routed_lo.shape[1] // ts
    tile0 = t0 // ts
    out_tile0 = part * (n_part // ts) + tile0
    const2 = lambda i: (0, 0)
    in_specs = [
        pl.BlockSpec((ts, d), lambda i: (tile0 + i, 0)),
        pl.BlockSpec((PIECES, ts, LANES), lambda i: (0, i, 0)),
        pl.BlockSpec((PIECES, ts, LANES), lambda i: (0, i, 0)),
        pl.BlockSpec((d, EXPERT_DIM), const2),
        pl.BlockSpec((d, EXPERT_DIM), const2),
        pl.BlockSpec((EXPERT_DIM, d), const2),
        pl.BlockSpec((1, d), const2),
        pl.BlockSpec((1, d), const2),
    ]
    args = [x, routed_lo, routed_hi, sh_gate.astype(_BF16), sh_up.astype(_BF16), sh_down.astype(_BF16),
            ln_g.reshape(1, d), ln_b.reshape(1, d)]
    aliases = {}
    if prev is not None:
        in_specs.append(pl.BlockSpec(memory_space=pl.ANY))
        aliases = {len(args): 0}
        args.append(prev)
    return pl.pallas_call(
        _combine_kernel,
        grid=(steps,),
        in_specs=in_specs,
        out_specs=pl.BlockSpec((ts, d), lambda i: (out_tile0 + i, 0)),
        out_shape=jax.ShapeDtypeStruct((out_parts * n_part, d), _F32),
        input_output_aliases=aliases,
        compiler_params=pltpu.CompilerParams(
            dimension_semantics=("arbitrary",),
            vmem_limit_bytes=VMEM_LIMIT),
        name="moe_combine",
    )(*args)


def _combine_sgu(x, routed_lo, routed_hi, sh_gate, sh_up, sh_down, ffn_g, ffn_b,
                 w_in, b_in, ln_g, ln_b, w_s, b_s, w_out, mix_g, mix_b):
    n_part, d = x.shape
    ts = SGU_TILE
    const2 = lambda i: (0, 0)
    const3 = lambda i: (0, 0, 0)
    causal = jnp.tril(jnp.ones((SGU_CHUNK, SGU_CHUNK), w_s.dtype))
    ws = (w_s * causal[None]).astype(_BF16)
    tile = pl.BlockSpec((ts, d), lambda i: (i, 0))
    ptile = pl.BlockSpec((PIECES, ts, LANES), lambda i: (0, i, 0))
    row = lambda n: pl.BlockSpec((1, n), const2)
    return pl.pallas_call(
        _combine_sgu_kernel,
        grid=(n_part // ts,),
        in_specs=[
            tile, ptile, ptile,
            pl.BlockSpec((d, EXPERT_DIM), const2),
            pl.BlockSpec((d, EXPERT_DIM), const2),
            pl.BlockSpec((EXPERT_DIM, d), const2),
            row(d), row(d),
            pl.BlockSpec((d, 2 * SGU_WIDTH), const2),
            row(2 * SGU_WIDTH), row(SGU_WIDTH), row(SGU_WIDTH),
            pl.BlockSpec((SGU_HEADS, SGU_CHUNK, SGU_CHUNK), const3),
            pl.BlockSpec((SGU_HEADS, SGU_CHUNK, 1), const3),
            pl.BlockSpec((SGU_WIDTH, d), const2),
            row(d), row(d),
        ],
        out_specs=(tile, ptile),
        out_shape=(jax.ShapeDtypeStruct((n_part, d), _F32),
                   jax.ShapeDtypeStruct((PIECES, n_part, LANES), _U32)),
        scratch_shapes=[pltpu.VMEM((ts, SGU_WIDTH), _BF16), pltpu.VMEM((ts, d), _F32)],
        compiler_params=pltpu.CompilerParams(
            dimension_semantics=("arbitrary",),
            vmem_limit_bytes=VMEM_LIMIT),
        name="moe_combine_sgu_mixer",
    )(x, routed_lo, routed_hi, sh_gate.astype(_BF16), sh_up.astype(_BF16), sh_down.astype(_BF16),
      ffn_g.reshape(1, d), ffn_b.reshape(1, d),
      w_in.astype(_BF16), b_in.reshape(1, -1), ln_g.reshape(1, -1), ln_b.reshape(1, -1),
      ws, b_s.reshape(SGU_HEADS, SGU_CHUNK, 1), w_out.astype(_BF16),
      mix_g.reshape(1, d), mix_b.reshape(1, d))


def _moe_dispatch(layer, x, x_packed, w_router, router_bias, w_gate, w_up, w_down):
    n_part, d = x.shape
    bm = EXPERT_ROWS
    n_rows = n_part * TOP_K + N_EXPERTS * bm
    e_k, r_k, g_k, counts = _route(x, w_router, router_bias)
    counts = counts.reshape(N_EXPERTS).astype(jnp.int32)
    padded = (counts + bm - 1) // bm * bm
    padded_end = jnp.cumsum(padded)
    padded_start = padded_end - padded
    idx = _row_indices(padded_start, e_k, r_k, n_rows).reshape(PIECES * TOP_K, n_part)
    xs = _sc_scatter_rows(x_packed.reshape(PIECES * n_part, LANES), idx, PIECES * n_rows)
    rows = _experts(layer, padded_start // bm, padded // bm, counts, padded_end[-1:] // bm,
                    xs.reshape(PIECES, n_rows, LANES), w_gate, w_up, w_down)
    return rows.reshape(PIECES * n_rows, LANES), idx, g_k


def _routed_sums(rows, idx, gates, t0, n_tok):
    lo, hi = _sc_gather_sum(rows, idx, gates, t0, n_tok)
    return lo.reshape(PIECES, n_tok, LANES), hi.reshape(PIECES, n_tok, LANES)


def kernel(x, pool_w_in, pool_w_grp, pool_scale, pool_w_out, sgu_w_in, sgu_b_in, sgu_ln_g, sgu_ln_b, sgu_w_s, sgu_b_s, sgu_w_out, ln_mix_g, ln_mix_b, moe_w_router, moe_router_bias, moe_w_gate, moe_w_up, moe_w_down, moe_sh_gate, moe_sh_up, moe_sh_down, ln_ffn_g, ln_ffn_b):
    bsz, seq, d = x.shape
    parts = range(TOKEN_PARTS)

    def dispatch(i, h, h_packed):
        return _moe_dispatch(i, h, h_packed, moe_w_router[i], moe_router_bias[i],
                             moe_w_gate, moe_w_up, moe_w_down)

    mixed = [_pool_layer(x, p, pool_w_in[0], pool_w_grp[0], pool_scale[0], pool_w_out[0],
                         ln_mix_g[0], ln_mix_b[0]) for p in parts]
    mixed = [(h.reshape(-1, d), h_packed) for h, h_packed in mixed]
    routed = [dispatch(0, h, h_packed) for h, h_packed in mixed]
    n_part = mixed[0][0].shape[0]
    mixed = [_combine_sgu(mixed[p][0], *_routed_sums(*routed[p], 0, n_part),
                          moe_sh_gate[0], moe_sh_up[0], moe_sh_down[0], ln_ffn_g[0], ln_ffn_b[0],
                          sgu_w_in[0], sgu_b_in[0], sgu_ln_g[0], sgu_ln_b[0], sgu_w_s[0],
                          sgu_b_s[0], sgu_w_out[0], ln_mix_g[1], ln_mix_b[1]) for p in parts]
    routed = [dispatch(1, h, h_packed) for h, h_packed in mixed]
    out = None
    n_slice = n_part // FINAL_SLICES
    for p in parts:
        for t0 in range(0, n_part, n_slice):
            out = _combine(mixed[p][0], *_routed_sums(*routed[p], t0, n_slice),
                           moe_sh_gate[1], moe_sh_up[1], moe_sh_down[1],
                           ln_ffn_g[1], ln_ffn_b[1], TOKEN_PARTS, p, t0, out)
    return out.reshape(bsz, seq, d)
```

```python
import functools

import jax
import jax.numpy as jnp
from jax import lax
from jax.experimental import pallas as pl
from jax.experimental.pallas import tpu as pltpu
from jax.experimental.pallas import tpu_sc as plsc

D_MODEL = 1024
DEPTH = 2
POOL_WINDOWS = (2, 4, 8, 16)
POOL_GROUP_DIM = D_MODEL // len(POOL_WINDOWS)
POOL_HALO = 16
SGU_CHUNK = 128
SGU_HEADS = 4
SGU_WIDTH = 2 * D_MODEL
SGU_HEAD_DIM = SGU_WIDTH // SGU_HEADS
N_EXPERTS = 64
TOP_K = 8
N_GROUPS = 8
GROUP_SIZE = N_EXPERTS // N_GROUPS
TOPK_GROUPS = 4
EXPERT_DIM = D_MODEL // 4
ROUTED_SCALE = 2.5
DEEPNORM_ALPHA = (2 * DEPTH) ** 0.25
LN_EPS = 1e-5

LANES = 128
PACKED = D_MODEL // 2
PIECES = PACKED // LANES

POOL_TILE = 1024
POOL_SUB = 256
SGU_TILE = 512
SGU_SUB = 256
ROUTE_TILE = 1024
ROUTE_SUB = 512
INDEX_TILE = 2048
EXPERT_ROWS = 512
EXPERT_SUB = 256
IN_SLOTS = 6
OUT_SLOTS = 4
COMBINE_TILE = 1024
SC_WINDOW = 128
SC_LANES = 16
SC_BURST = 16
TOKEN_PARTS = 2
V7X_VMEM_BYTES = 64 * 1024 * 1024
VMEM_LIMIT = V7X_VMEM_BYTES * 7 // 8

_F32 = jnp.float32
_BF16 = jnp.bfloat16
_U32 = jnp.uint32


def _dot(a, b):
    return jnp.dot(a, b, preferred_element_type=_F32)


def _layer_norm(h, g, b):
    mu = jnp.mean(h, axis=-1, keepdims=True)
    hc = h - mu
    var = jnp.mean(hc * hc, axis=-1, keepdims=True)
    return hc * lax.rsqrt(var + LN_EPS) * g + b


def _silu(x):
    return x * jax.nn.sigmoid(x)


def _gelu_tanh(x):
    c = 0.7978845608028654
    return x * (0.5 + 0.5 * jnp.tanh(x * (c + (c * 0.044715) * (x * x))))


def _pack_halves(v):
    half = v.shape[1] // 2
    lo = lax.bitcast_convert_type(v[:, :half].astype(_BF16).astype(_F32), _U32)
    hi = lax.bitcast_convert_type(v[:, half:].astype(_BF16).astype(_F32), _U32)
    return hi | (lo >> 16)


def _unpack_halves(w):
    lo = lax.bitcast_convert_type(w << 16, _F32)
    hi = lax.bitcast_convert_type(w & _U32(0xFFFF0000), _F32)
    return lo, hi


def _store_pieces(ref, r0, w):
    for c in range(PIECES):
        ref[c, r0:r0 + w.shape[0], :] = w[:, c * LANES:(c + 1) * LANES]


def _pool_kernel(x_ref, win_ref, wgrp_ref, scale_ref, wout_ref, g_ref, b_ref,
                 o_ref, op_ref, zs_ref, y_ref):
    s = pl.program_id(1)
    ts = x_ref.shape[1]

    @pl.when(s == 0)
    def _():
        zs_ref[0:POOL_HALO, :] = jnp.zeros((POOL_HALO, D_MODEL), _F32)

    for p0 in range(0, ts, POOL_SUB):
        z = _dot(x_ref[0, p0:p0 + POOL_SUB, :].astype(_BF16), win_ref[...])
        zs_ref[POOL_HALO + p0:POOL_HALO + p0 + POOL_SUB, :] = z
    for p0 in range(0, ts, POOL_SUB):
        x = x_ref[0, p0:p0 + POOL_SUB, :]
        base = POOL_HALO + p0
        pos = s * ts + p0 + lax.broadcasted_iota(jnp.int32, (POOL_SUB, 1), 0)
        for g, w in enumerate(POOL_WINDOWS):
            c0 = g * POOL_GROUP_DIM
            c1 = c0 + POOL_GROUP_DIM
            zg = zs_ref[base:base + POOL_SUB, c0:c1]
            ext = zs_ref[base - POOL_HALO:base + POOL_SUB, c0:c1]
            m = 1
            while m < w:
                ext = ext + pltpu.roll(ext, m, 0)
                m *= 2
            acc = ext[POOL_HALO:, :]
            cnt = jnp.minimum(pos + 1, w).astype(_F32)
            pooled = acc / cnt - zg
            yg = _dot(pooled.astype(_BF16), wgrp_ref[g]) * scale_ref[:, c0:c1]
            y_ref[p0:p0 + POOL_SUB, c0:c1] = yg.astype(_BF16)
        mix = _dot(y_ref[p0:p0 + POOL_SUB, :], wout_ref[...])
        out = _layer_norm(DEEPNORM_ALPHA * x + mix, g_ref[...], b_ref[...])
        o_ref[0, p0:p0 + POOL_SUB, :] = out
        _store_pieces(op_ref, p0, _pack_halves(out))
    zs_ref[0:POOL_HALO, :] = zs_ref[ts:ts + POOL_HALO, :]


def _pool_layer(x, part, w_in, w_grp, scale, w_out, ln_g, ln_b):
    bsz, seq, d = x.shape
    rows = bsz // TOKEN_PARTS
    ts = POOL_TILE
    steps = seq // ts
    const2 = lambda b, s: (0, 0)
    out_shape = (jax.ShapeDtypeStruct((rows, seq, d), _F32),
                 jax.ShapeDtypeStruct((PIECES, rows * seq, LANES), _U32))
    tile = pl.BlockSpec((1, ts, d), lambda b, s: (b, s, 0))
    ptile = pl.BlockSpec((PIECES, ts, LANES), lambda b, s: (0, b * steps + s, 0))
    return pl.pallas_call(
        _pool_kernel,
        grid=(rows, steps),
        in_specs=[
            pl.BlockSpec((1, ts, d), lambda b, s: (part * rows + b, s, 0)),
            pl.BlockSpec((d, d), const2),
            pl.BlockSpec((len(POOL_WINDOWS), POOL_GROUP_DIM, POOL_GROUP_DIM), lambda b, s: (0, 0, 0)),
            pl.BlockSpec((1, d), const2),
            pl.BlockSpec((d, d), const2),
            pl.BlockSpec((1, d), const2),
            pl.BlockSpec((1, d), const2),
        ],
        out_specs=(tile, ptile),
        out_shape=out_shape,
        scratch_shapes=[pltpu.VMEM((POOL_HALO + ts, d), _F32),
                        pltpu.VMEM((ts, d), _BF16)],
        compiler_params=pltpu.CompilerParams(
            dimension_semantics=("arbitrary", "arbitrary"),
            vmem_limit_bytes=VMEM_LIMIT),
        name="pool_mixer",
    )(x, w_in.astype(_BF16), w_grp.astype(_BF16), scale.reshape(1, d),
      w_out.astype(_BF16), ln_g.reshape(1, d), ln_b.reshape(1, d))


def _sgu_kernel(x_ref, win_ref, bin_ref, lng_ref, lnb_ref, ws_ref, bs_ref, wout_ref,
                g_ref, b_ref, o_ref, op_ref, gated_ref):
    ts = x_ref.shape[0]
    projected = []
    for p0 in range(0, ts, SGU_SUB):
        x = x_ref[p0:p0 + SGU_SUB, :]
        xb = x.astype(_BF16)
        v = _gelu_tanh(_dot(xb, win_ref[:, SGU_WIDTH:]) + bin_ref[:, SGU_WIDTH:])
        v = _layer_norm(v, lng_ref[...], lnb_ref[...]).astype(_BF16)
        u = _gelu_tanh(_dot(xb, win_ref[:, :SGU_WIDTH]) + bin_ref[:, :SGU_WIDTH])
        projected.append((p0, x, u, v))
    for p0, x, u, v in projected:
        for r0 in range(0, SGU_SUB, SGU_CHUNK):
            for h in range(SGU_HEADS):
                c0 = h * SGU_HEAD_DIM
                sv = _dot(ws_ref[h], v[r0:r0 + SGU_CHUNK, c0:c0 + SGU_HEAD_DIM]) + bs_ref[h]
                gated_ref[p0 + r0:p0 + r0 + SGU_CHUNK, c0:c0 + SGU_HEAD_DIM] = (
                    u[r0:r0 + SGU_CHUNK, c0:c0 + SGU_HEAD_DIM] * sv).astype(_BF16)
        mix = _dot(gated_ref[p0:p0 + SGU_SUB, :], wout_ref[...])
        out = _layer_norm(DEEPNORM_ALPHA * x + mix, g_ref[...], b_ref[...])
        o_ref[p0:p0 + SGU_SUB, :] = out
        _store_pieces(op_ref, p0, _pack_halves(out))


def _all_max(a):
    return jnp.max(jnp.max(a, axis=0, keepdims=True), axis=1, keepdims=True)


def _all_min(a):
    return jnp.min(jnp.min(a, axis=0, keepdims=True), axis=1, keepdims=True)


def _all_sum(a):
    return jnp.sum(jnp.sum(a, axis=0, keepdims=True), axis=1, keepdims=True)


def _route_kernel(x_ref, whi_ref, wlo_ref, bias_ref, earlier_ref,
                  ek_ref, rk_ref, gk_ref, cnt_ref, carry_ref):
    @pl.when(pl.program_id(0) == 0)
    def _():
        carry_ref[...] = jnp.zeros_like(carry_ref)

    nt = lambda a, b: lax.dot_general(a, b, (((1,), (1,)), ((), ())),
                                      preferred_element_type=_F32)
    logits = []
    for p0 in range(0, x_ref.shape[0], ROUTE_SUB):
        x = x_ref[p0:p0 + ROUTE_SUB, :]
        x_hi = x.astype(_BF16)
        x_lo = (x - x_hi.astype(_F32)).astype(_BF16)
        logits.append(nt(whi_ref[...], x_hi) + (nt(whi_ref[...], x_lo) + nt(wlo_ref[...], x_hi)))
    for n, sub_logits in enumerate(logits):
        _route_sub_tile(sub_logits, n * ROUTE_SUB, bias_ref, earlier_ref,
                        ek_ref, rk_ref, gk_ref, carry_ref)
    cnt_ref[...] = carry_ref[...]


def _route_sub_tile(logits, p0, bias_ref, earlier_ref, ek_ref, rk_ref, gk_ref, carry_ref):
    ts = ROUTE_SUB
    scores = jax.nn.sigmoid(logits).reshape(N_GROUPS, GROUP_SIZE, ts)
    biased = scores + bias_ref[...]
    neg_inf = jnp.float32(-jnp.inf)
    shape3 = (N_GROUPS, GROUP_SIZE, ts)
    in_grp = lax.broadcasted_iota(jnp.int32, shape3, 1)
    grp = lax.broadcasted_iota(jnp.int32, shape3, 0)
    eid = grp * GROUP_SIZE + in_grp

    m1 = jnp.max(biased, axis=1, keepdims=True)
    first1 = jnp.min(jnp.where(biased == m1, in_grp, GROUP_SIZE), axis=1, keepdims=True)
    m2 = jnp.max(jnp.where(in_grp == first1, neg_inf, biased), axis=1, keepdims=True)
    gscore = m1 + m2

    gid = lax.broadcasted_iota(jnp.int32, (N_GROUPS, 1, ts), 0)
    gsel = jnp.zeros((N_GROUPS, 1, ts), jnp.bool_)
    for _ in range(TOPK_GROUPS):
        m = jnp.max(gscore, axis=0, keepdims=True)
        first = jnp.min(jnp.where(gscore == m, gid, N_GROUPS), axis=0, keepdims=True)
        pick = gid == first
        gsel = jnp.logical_or(gsel, pick)
        gscore = jnp.where(pick, neg_inf, gscore)

    masked = jnp.where(gsel, biased, neg_inf)
    picked_any = jnp.zeros(shape3, jnp.bool_)
    e_k, s_k = [], []
    for _ in range(TOP_K):
        m = _all_max(masked)
        first = _all_min(jnp.where(masked == m, eid, N_EXPERTS))
        pick = eid == first
        picked_any = jnp.logical_or(picked_any, pick)
        masked = jnp.where(pick, neg_inf, masked)
        e_k.append(first)
        s_k.append(_all_sum(jnp.where(pick, scores, 0.0)))
    denom = s_k[0]
    for k in range(1, TOP_K):
        denom = denom + s_k[k]

    sel = picked_any.astype(_F32).reshape(N_EXPERTS, ts)
    rank = _dot(sel.astype(_BF16), earlier_ref[...]) + carry_ref[...]
    rank3 = rank.reshape(shape3)
    carry_ref[...] += jnp.sum(sel, axis=1, keepdims=True)

    for k in range(TOP_K):
        r = _all_sum(jnp.where(eid == e_k[k], rank3, 0.0))
        ek_ref[k:k + 1, p0:p0 + ts] = e_k[k].reshape(1, ts)
        rk_ref[k:k + 1, p0:p0 + ts] = r.reshape(1, ts).astype(jnp.int32)
        gk_ref[k:k + 1, p0:p0 + ts] = (s_k[k] / denom * ROUTED_SCALE).reshape(1, ts)


def _route(x, w_router, router_bias):
    n_part, d = x.shape
    ts = ROUTE_TILE
    steps = n_part // ts
    out_shape = (jax.ShapeDtypeStruct((TOP_K, n_part), jnp.int32),
                 jax.ShapeDtypeStruct((TOP_K, n_part), jnp.int32),
                 jax.ShapeDtypeStruct((TOP_K, n_part), _F32),
                 jax.ShapeDtypeStruct((N_EXPERTS, 1), _F32))
    kspec = pl.BlockSpec((TOP_K, ts), lambda i: (0, i))
    const2 = lambda i: (0, 0)
    w_t = w_router.T
    w_hi = w_t.astype(_BF16)
    w_lo = (w_t - w_hi.astype(_F32)).astype(_BF16)
    pos = jnp.arange(ROUTE_SUB, dtype=jnp.int32)
    earlier = (pos[:, None] < pos[None, :]).astype(_BF16)
    return pl.pallas_call(
        _route_kernel,
        grid=(steps,),
        in_specs=[
            pl.BlockSpec((ts, d), lambda i: (i, 0)),
            pl.BlockSpec((N_EXPERTS, d), const2),
            pl.BlockSpec((N_EXPERTS, d), const2),
            pl.BlockSpec((N_GROUPS, GROUP_SIZE, 1), lambda i: (0, 0, 0)),
            pl.BlockSpec((ROUTE_SUB, ROUTE_SUB), const2),
        ],
        out_specs=(kspec, kspec, kspec, pl.BlockSpec((N_EXPERTS, 1), const2)),
        out_shape=out_shape,
        scratch_shapes=[pltpu.VMEM((N_EXPERTS, 1), _F32)],
        compiler_params=pltpu.CompilerParams(
            dimension_semantics=("arbitrary",),
            vmem_limit_bytes=VMEM_LIMIT),
        name="moe_route",
    )(x, w_hi, w_lo, router_bias.reshape(N_GROUPS, GROUP_SIZE, 1), earlier)


def _sc_mesh():
    return plsc.VectorSubcoreMesh(core_axis_name="core", subcore_axis_name="subcore")


def _sc_scatter_rows(src, idx, n_out):
    d = src.shape[1]
    wins = idx.shape[1] // SC_WINDOW

    def body(src_hbm, idx_hbm, out_hbm, sem):
        def step(src_vmem, idx_vmem):
            copies = [pltpu.make_async_copy(src_vmem, out_hbm.at[idx_vmem.at[k]], sem)
                      for k in range(TOP_K)]
            for copy in copies:
                copy.start()
            for copy in copies:
                copy.wait()

        pltpu.emit_pipeline(
            step,
            grid=(src.shape[0] // SC_WINDOW,),
            in_specs=[pl.BlockSpec((SC_WINDOW, d), index_map=lambda i: (i, 0)),
                      pl.BlockSpec((TOP_K, SC_WINDOW), index_map=lambda i: (i // wins, i % wins))],
            out_specs=[],
            core_axis_name=("core", "subcore"),
            dimension_semantics=(pltpu.PARALLEL,),
        )(src_hbm, idx_hbm)

    return pl.kernel(body, out_type=jax.ShapeDtypeStruct((n_out, d), src.dtype),
                     mesh=_sc_mesh(), scratch_types=[pltpu.SemaphoreType.DMA],
                     name="sc_scatter_rows")(src, idx)


def _sc_gather_sum(table, idx, gates):
    d = table.shape[1]
    n_tok = idx.shape[1]
    wins = n_tok // SC_WINDOW
    lanes = SC_LANES
    out_type = jax.ShapeDtypeStruct((PIECES * n_tok, d), _F32)

    def tree_sum(terms):
        while len(terms) > 1:
            terms = [terms[i] + terms[i + 1] for i in range(0, len(terms), 2)]
        return terms[0]

    def body(table_hbm, idx_hbm, gate_hbm, lo_hbm, hi_hbm, rows_vmem, sem):
        def step(idx_vmem, gate_vmem, lo_vmem, hi_vmem):
            def burst_copies(b):
                return [pltpu.make_async_copy(
                    table_hbm.at[idx_vmem.at[k, pl.ds(b * SC_BURST, SC_BURST)]],
                    rows_vmem.at[b % 2, k], sem.at[b % 2]) for k in range(TOP_K)]

            n_bursts = SC_WINDOW // SC_BURST
            for copy in burst_copies(0):
                copy.start()
            for b in range(n_bursts):
                if b + 1 < n_bursts:
                    for copy in burst_copies(b + 1):
                        copy.start()
                for copy in burst_copies(b):
                    copy.wait()
                t0 = b * SC_BURST
                rows = rows_vmem.at[b % 2]

                @plsc.parallel_loop(0, SC_BURST)
                def _(r):
                    t = t0 + r
                    token = jnp.full((lanes,), t, jnp.int32)
                    gate = [plsc.load_gather(gate_vmem, [jnp.full((lanes,), k, jnp.int32), token])
                            for k in range(TOP_K)]
                    for j in range(0, d, lanes):
                        words = [rows[k, r, pl.ds(j, lanes)] for k in range(TOP_K)]
                        lo_vmem[t, pl.ds(j, lanes)] = tree_sum(
                            [gate[k] * lax.bitcast_convert_type(words[k] << 16, _F32)
                             for k in range(TOP_K)])
                        hi_vmem[t, pl.ds(j, lanes)] = tree_sum(
                            [gate[k] * lax.bitcast_convert_type(words[k] & _U32(0xFFFF0000), _F32)
                             for k in range(TOP_K)])

        window = lambda i: (i // wins, i % wins)
        pltpu.emit_pipeline(
            step,
            grid=(PIECES * wins,),
            in_specs=[pl.BlockSpec((TOP_K, SC_WINDOW), index_map=window),
                      pl.BlockSpec((TOP_K, SC_WINDOW), index_map=lambda i: (0, i % wins))],
            out_specs=[pl.BlockSpec((SC_WINDOW, d), index_map=lambda i: (i, 0)),
                       pl.BlockSpec((SC_WINDOW, d), index_map=lambda i: (i, 0))],
            core_axis_name=("core", "subcore"),
            dimension_semantics=(pltpu.PARALLEL,),
        )(idx_hbm, gate_hbm, lo_hbm, hi_hbm)

    return pl.kernel(body, out_type=(out_type, out_type), mesh=_sc_mesh(),
                     scratch_types=[pltpu.VMEM((2, TOP_K, SC_BURST, d), _U32),
                                    pltpu.SemaphoreType.DMA((2,))],
                     compiler_params=pltpu.CompilerParams(needs_layout_passes=False),
                     name="sc_gather_sum")(table, idx, gates)


def _index_kernel(start_ref, ek_ref, rk_ref, idx_ref, *, n_rows):
    e = ek_ref[...]
    start = jnp.zeros(e.shape, jnp.int32)
    for ex in range(N_EXPERTS):
        start = jnp.where(e == ex, start_ref[ex], start)
    dest = start + rk_ref[...]
    for c in range(PIECES):
        idx_ref[c] = dest + c * n_rows


def _row_indices(padded_start, e_k, r_k, n_rows):
    n_tok = e_k.shape[1]
    ts = INDEX_TILE
    kspec = pl.BlockSpec((TOP_K, ts), lambda i, st: (0, i))
    grid_spec = pltpu.PrefetchScalarGridSpec(
        num_scalar_prefetch=1,
        grid=(n_tok // ts,),
        in_specs=[kspec, kspec],
        out_specs=pl.BlockSpec((PIECES, TOP_K, ts), lambda i, st: (0, 0, i)),
    )
    return pl.pallas_call(
        functools.partial(_index_kernel, n_rows=n_rows),
        grid_spec=grid_spec,
        out_shape=jax.ShapeDtypeStruct((PIECES, TOP_K, n_tok), jnp.int32),
        compiler_params=pltpu.CompilerParams(dimension_semantics=("arbitrary",)),
        name="moe_row_indices",
    )(padded_start, e_k, r_k)


def _expert_kernel(chunk0_ref, nchunk_ref, cnt_ref, total_ref,
                   xs_hbm, wg_ref, wu_ref, wd_ref, o_hbm,
                   xbuf, obuf, sem_in, sem_out):
    e = pl.program_id(0)
    total = total_ref[0]
    ch = EXPERT_ROWS
    ahead = IN_SLOTS - 1

    class _Copy:
        def __init__(self, src, dst, sem):
            self.refs = (src, dst, sem)

        def start(self):
            pltpu.async_copy(*self.refs, priority=1)

        def wait(self):
            pltpu.make_async_copy(*self.refs).wait()

    def in_copy(g):
        slot = g % IN_SLOTS
        return _Copy(xs_hbm.at[:, pl.ds(g * ch, ch), :], xbuf.at[slot], sem_in.at[slot])

    def out_copy(g):
        slot = g % OUT_SLOTS
        return _Copy(obuf.at[slot], o_hbm.at[:, pl.ds(g * ch, ch), :], sem_out.at[slot])

    @pl.when(e == 0)
    def _():
        obuf[...] = jnp.zeros_like(obuf)
        for g in range(ahead):
            @pl.when(g < total)
            def _():
                in_copy(g).start()


    def chunk(j, carry):
        g = chunk0_ref[e] + j
        in_copy(g).wait()

        @pl.when(g + ahead < total)
        def _():
            in_copy(g + ahead).start()

        @pl.when(g >= OUT_SLOTS)
        def _():
            out_copy(g - OUT_SLOTS).wait()

        islot = g % IN_SLOTS
        oslot = g % OUT_SLOTS
        valid = cnt_ref[e] - j * ch

        def gate_up(s):
            r0 = s * EXPERT_SUB
            w = jnp.concatenate([xbuf[islot, c, r0:r0 + EXPERT_SUB, :] for c in range(PIECES)],
                                axis=1)
            lo, hi = _unpack_halves(w)
            gate = _dot(lo, wg_ref[0, 0, :PACKED, :]) + _dot(hi, wg_ref[0, 0, PACKED:, :])
            up = _dot(lo, wu_ref[0, 0, :PACKED, :]) + _dot(hi, wu_ref[0, 0, PACKED:, :])
            return gate, up

        def down(s, gate_up_pair):
            r0 = s * EXPERT_SUB
            gate, up = gate_up_pair
            y = _pack_halves(_dot(_silu(gate) * up, wd_ref[0, 0]))
            for c in range(PIECES):
                obuf[oslot, c, r0:r0 + EXPERT_SUB, :] = y[:, c * LANES:(c + 1) * LANES]

        n_sub = ch // EXPERT_SUB

        all_subs = valid > ch - EXPERT_SUB

        @pl.when(all_subs)
        def _():
            projected = [gate_up(s) for s in range(n_sub)]
            for s, gu in enumerate(projected):
                down(s, gu)

        @pl.when(jnp.logical_not(all_subs))
        def _():
            for s in range(n_sub - 1):
                @pl.when(s * EXPERT_SUB < valid)
                def _():
                    down(s, gate_up(s))

        out_copy(g).start()
        return carry

    lax.fori_loop(0, nchunk_ref[e], chunk, 0)

    @pl.when(e == pl.num_programs(0) - 1)
    def _():
        for back in range(OUT_SLOTS, 0, -1):
            @pl.when(total >= back)
            def _():
                out_copy(total - back).wait()


def _experts(layer, chunk0, nchunk, counts, total, xs, w_gate, w_up, w_down):
    d = D_MODEL
    ch = EXPERT_ROWS
    wspec_in = pl.BlockSpec((1, 1, d, EXPERT_DIM), lambda e, *_: (layer, e, 0, 0))
    any_spec = pl.BlockSpec(memory_space=pl.ANY)
    grid_spec = pltpu.PrefetchScalarGridSpec(
        num_scalar_prefetch=4,
        grid=(N_EXPERTS,),
        in_specs=[
            any_spec,
            wspec_in,
            wspec_in,
            pl.BlockSpec((1, 1, EXPERT_DIM, d), lambda e, *_: (layer, e, 0, 0)),
        ],
        out_specs=any_spec,
        scratch_shapes=[pltpu.VMEM((IN_SLOTS, PIECES, ch, LANES), _U32),
                        pltpu.VMEM((OUT_SLOTS, PIECES, ch, LANES), _U32),
                        pltpu.SemaphoreType.DMA((IN_SLOTS,)),
                        pltpu.SemaphoreType.DMA((OUT_SLOTS,))],
    )
    return pl.pallas_call(
        _expert_kernel,
        grid_spec=grid_spec,
        out_shape=jax.ShapeDtypeStruct(xs.shape, _U32),
        compiler_params=pltpu.CompilerParams(
            dimension_semantics=("arbitrary",),
            vmem_limit_bytes=VMEM_LIMIT),
        name="moe_experts",
    )(chunk0, nchunk, counts, total, xs, w_gate, w_up, w_down)


def _moe_residual(x, lo_ref, hi_ref, rows, sg_ref, su_ref, sd_ref, g_ref, b_ref):
    xb = x.astype(_BF16)
    routed = jnp.concatenate([lo_ref[c, rows, :] for c in range(PIECES)] +
                             [hi_ref[c, rows, :] for c in range(PIECES)], axis=1)
    h = _silu(_dot(xb, sg_ref[...])) * _dot(xb, su_ref[...])
    shared = _dot(h.astype(_BF16), sd_ref[...])
    return _layer_norm(DEEPNORM_ALPHA * x + (routed + shared), g_ref[...], b_ref[...])


def _combine_kernel(x_ref, lo_ref, hi_ref, sg_ref, su_ref, sd_ref, g_ref, b_ref, *rest):
    o_ref = rest[-1]
    o_ref[...] = _moe_residual(x_ref[...], lo_ref, hi_ref, slice(None),
                               sg_ref, su_ref, sd_ref, g_ref, b_ref)


def _combine_sgu_kernel(x_ref, lo_ref, hi_ref, sg_ref, su_ref, sd_ref, fg_ref, fb_ref,
                        win_ref, bin_ref, lng_ref, lnb_ref, ws_ref, bs_ref, wout_ref,
                        mg_ref, mb_ref, o_ref, op_ref, gated_ref, h_ref):
    for p0 in range(0, x_ref.shape[0], SGU_SUB):
        rows = slice(p0, p0 + SGU_SUB)
        h_ref[rows, :] = _moe_residual(x_ref[rows, :], lo_ref, hi_ref, rows,
                                       sg_ref, su_ref, sd_ref, fg_ref, fb_ref)
    _sgu_kernel(h_ref, win_ref, bin_ref, lng_ref, lnb_ref, ws_ref, bs_ref, wout_ref,
                mg_ref, mb_ref, o_ref, op_ref, gated_ref)


def _combine(x, routed_lo, routed_hi, sh_gate, sh_up, sh_down, ln_g, ln_b, out_parts, part, prev):
    n_part, d = x.shape
    ts = COMBINE_TILE
    steps = n_part // ts
    const2 = lambda i: (0, 0)
    in_specs = [
        pl.BlockSpec((ts, d), lambda i: (i, 0)),
        pl.BlockSpec((PIECES, ts, LANES), lambda i: (0, i, 0)),
        pl.BlockSpec((PIECES, ts, LANES), lambda i: (0, i, 0)),
        pl.BlockSpec((d, EXPERT_DIM), const2),
        pl.BlockSpec((d, EXPERT_DIM), const2),
        pl.BlockSpec((EXPERT_DIM, d), const2),
        pl.BlockSpec((1, d), const2),
        pl.BlockSpec((1, d), const2),
    ]
    args = [x, routed_lo, routed_hi, sh_gate.astype(_BF16), sh_up.astype(_BF16), sh_down.astype(_BF16),
            ln_g.reshape(1, d), ln_b.reshape(1, d)]
    aliases = {}
    if prev is not None:
        in_specs.append(pl.BlockSpec(memory_space=pl.ANY))
        aliases = {len(args): 0}
        args.append(prev)
    return pl.pallas_call(
        _combine_kernel,
        grid=(steps,),
        in_specs=in_specs,
        out_specs=pl.BlockSpec((ts, d), lambda i: (part * steps + i, 0)),
        out_shape=jax.ShapeDtypeStruct((out_parts * n_part, d), _F32),
        input_output_aliases=aliases,
        compiler_params=pltpu.CompilerParams(
            dimension_semantics=("arbitrary",),
            vmem_limit_bytes=VMEM_LIMIT),
        name="moe_combine",
    )(*args)


def _combine_sgu(x, routed_lo, routed_hi, sh_gate, sh_up, sh_down, ffn_g, ffn_b,
                 w_in, b_in, ln_g, ln_b, w_s, b_s, w_out, mix_g, mix_b):
    n_part, d = x.shape
    ts = SGU_TILE
    const2 = lambda i: (0, 0)
    const3 = lambda i: (0, 0, 0)
    causal = jnp.tril(jnp.ones((SGU_CHUNK, SGU_CHUNK), w_s.dtype))
    ws = (w_s * causal[None]).astype(_BF16)
    tile = pl.BlockSpec((ts, d), lambda i: (i, 0))
    ptile = pl.BlockSpec((PIECES, ts, LANES), lambda i: (0, i, 0))
    row = lambda n: pl.BlockSpec((1, n), const2)
    return pl.pallas_call(
        _combine_sgu_kernel,
        grid=(n_part // ts,),
        in_specs=[
            tile, ptile, ptile,
            pl.BlockSpec((d, EXPERT_DIM), const2),
            pl.BlockSpec((d, EXPERT_DIM), const2),
            pl.BlockSpec((EXPERT_DIM, d), const2),
            row(d), row(d),
            pl.BlockSpec((d, 2 * SGU_WIDTH), const2),
            row(2 * SGU_WIDTH), row(SGU_WIDTH), row(SGU_WIDTH),
            pl.BlockSpec((SGU_HEADS, SGU_CHUNK, SGU_CHUNK), const3),
            pl.BlockSpec((SGU_HEADS, SGU_CHUNK, 1), const3),
            pl.BlockSpec((SGU_WIDTH, d), const2),
            row(d), row(d),
        ],
        out_specs=(tile, ptile),
        out_shape=(jax.ShapeDtypeStruct((n_part, d), _F32),
                   jax.ShapeDtypeStruct((PIECES, n_part, LANES), _U32)),
        scratch_shapes=[pltpu.VMEM((ts, SGU_WIDTH), _BF16), pltpu.VMEM((ts, d), _F32)],
        compiler_params=pltpu.CompilerParams(
            dimension_semantics=("arbitrary",),
            vmem_limit_bytes=VMEM_LIMIT),
        name="moe_combine_sgu_mixer",
    )(x, routed_lo, routed_hi, sh_gate.astype(_BF16), sh_up.astype(_BF16), sh_down.astype(_BF16),
      ffn_g.reshape(1, d), ffn_b.reshape(1, d),
      w_in.astype(_BF16), b_in.reshape(1, -1), ln_g.reshape(1, -1), ln_b.reshape(1, -1),
      ws, b_s.reshape(SGU_HEADS, SGU_CHUNK, 1), w_out.astype(_BF16),
      mix_g.reshape(1, d), mix_b.reshape(1, d))


def _moe_dispatch(layer, x, x_packed, w_router, router_bias, w_gate, w_up, w_down):
    n_part, d = x.shape
    bm = EXPERT_ROWS
    n_rows = n_part * TOP_K + N_EXPERTS * bm
    e_k, r_k, g_k, counts = _route(x, w_router, router_bias)
    counts = counts.reshape(N_EXPERTS).astype(jnp.int32)
    padded = (counts + bm - 1) // bm * bm
    padded_end = jnp.cumsum(padded)
    padded_start = padded_end - padded
    idx = _row_indices(padded_start, e_k, r_k, n_rows).reshape(PIECES * TOP_K, n_part)
    xs = _sc_scatter_rows(x_packed.reshape(PIECES * n_part, LANES), idx, PIECES * n_rows)
    rows = _experts(layer, padded_start // bm, padded // bm, counts, padded_end[-1:] // bm,
                    xs.reshape(PIECES, n_rows, LANES), w_gate, w_up, w_down)
    return rows.reshape(PIECES * n_rows, LANES), idx, g_k


def _routed_sums(x, rows, idx, gates):
    n_part = x.shape[0]
    lo, hi = _sc_gather_sum(rows, idx, gates)
    return lo.reshape(PIECES, n_part, LANES), hi.reshape(PIECES, n_part, LANES)


def kernel(x, pool_w_in, pool_w_grp, pool_scale, pool_w_out, sgu_w_in, sgu_b_in, sgu_ln_g, sgu_ln_b, sgu_w_s, sgu_b_s, sgu_w_out, ln_mix_g, ln_mix_b, moe_w_router, moe_router_bias, moe_w_gate, moe_w_up, moe_w_down, moe_sh_gate, moe_sh_up, moe_sh_down, ln_ffn_g, ln_ffn_b):
    bsz, seq, d = x.shape
    parts = range(TOKEN_PARTS)

    def dispatch(i, h, h_packed):
        return _moe_dispatch(i, h, h_packed, moe_w_router[i], moe_router_bias[i],
                             moe_w_gate, moe_w_up, moe_w_down)

    mixed = [_pool_layer(x, p, pool_w_in[0], pool_w_grp[0], pool_scale[0], pool_w_out[0],
                         ln_mix_g[0], ln_mix_b[0]) for p in parts]
    mixed = [(h.reshape(-1, d), h_packed) for h, h_packed in mixed]
    routed = [dispatch(0, h, h_packed) for h, h_packed in mixed]
    mixed = [_combine_sgu(mixed[p][0], *_routed_sums(mixed[p][0], *routed[p]),
                          moe_sh_gate[0], moe_sh_up[0], moe_sh_down[0], ln_ffn_g[0], ln_ffn_b[0],
                          sgu_w_in[0], sgu_b_in[0], sgu_ln_g[0], sgu_ln_b[0], sgu_w_s[0],
                          sgu_b_s[0], sgu_w_out[0], ln_mix_g[1], ln_mix_b[1]) for p in parts]
    routed = [dispatch(1, h, h_packed) for h, h_packed in mixed]
    out = None
    for p in parts:
        h = mixed[p][0]
        out = _combine(h, *_routed_sums(h, *routed[p]), moe_sh_gate[1], moe_sh_up[1],
                       moe_sh_down[1], ln_ffn_g[1], ln_ffn_b[1], TOKEN_PARTS, p, out)
    return out.reshape(bsz, seq, d)
```

```python
import functools

import jax
import jax.numpy as jnp
from jax import lax
from jax.experimental import pallas as pl
from jax.experimental.pallas import tpu as pltpu
from jax.experimental.pallas import tpu_sc as plsc

D_MODEL = 1024
DEPTH = 2
POOL_WINDOWS = (2, 4, 8, 16)
POOL_GROUP_DIM = D_MODEL // len(POOL_WINDOWS)
POOL_HALO = 16
SGU_CHUNK = 128
SGU_HEADS = 4
SGU_WIDTH = 2 * D_MODEL
SGU_HEAD_DIM = SGU_WIDTH // SGU_HEADS
N_EXPERTS = 64
TOP_K = 8
N_GROUPS = 8
GROUP_SIZE = N_EXPERTS // N_GROUPS
TOPK_GROUPS = 4
EXPERT_DIM = D_MODEL // 4
ROUTED_SCALE = 2.5
DEEPNORM_ALPHA = (2 * DEPTH) ** 0.25
LN_EPS = 1e-5

LANES = 128
PACKED = D_MODEL // 2
PIECES = PACKED // LANES

POOL_TILE = 1024
POOL_SUB = 256
SGU_TILE = 512
SGU_SUB = 256
ROUTE_TILE = 1024
ROUTE_SUB = 512
INDEX_TILE = 2048
EXPERT_ROWS = 512
EXPERT_SUB = 256
IN_SLOTS = 6
OUT_SLOTS = 4
COMBINE_TILE = 1024
SC_WINDOW = 128
SC_LANES = 16
SC_BURST = 16
TOKEN_PARTS = 2
TAIL_SLICES = 2
V7X_VMEM_BYTES = 64 * 1024 * 1024
VMEM_LIMIT = V7X_VMEM_BYTES * 7 // 8

_F32 = jnp.float32
_BF16 = jnp.bfloat16
_U32 = jnp.uint32


def _dot(a, b):
    return jnp.dot(a, b, preferred_element_type=_F32)


def _layer_norm(h, g, b):
    mu = jnp.mean(h, axis=-1, keepdims=True)
    hc = h - mu
    var = jnp.mean(hc * hc, axis=-1, keepdims=True)
    return hc * lax.rsqrt(var + LN_EPS) * g + b


def _silu(x):
    return x * jax.nn.sigmoid(x)


def _gelu_tanh(x):
    c = 0.7978845608028654
    return x * (0.5 + 0.5 * jnp.tanh(x * (c + (c * 0.044715) * (x * x))))


def _pack_halves(v):
    half = v.shape[1] // 2
    lo = lax.bitcast_convert_type(v[:, :half].astype(_BF16).astype(_F32), _U32)
    hi = lax.bitcast_convert_type(v[:, half:].astype(_BF16).astype(_F32), _U32)
    return hi | (lo >> 16)


def _unpack_halves(w):
    lo = lax.bitcast_convert_type(w << 16, _F32)
    hi = lax.bitcast_convert_type(w & _U32(0xFFFF0000), _F32)
    return lo, hi


def _store_pieces(ref, r0, w):
    for c in range(PIECES):
        ref[c, r0:r0 + w.shape[0], :] = w[:, c * LANES:(c + 1) * LANES]


def _pool_kernel(x_ref, win_ref, wgrp_ref, scale_ref, wout_ref, g_ref, b_ref,
                 o_ref, op_ref, zs_ref, y_ref):
    s = pl.program_id(1)
    ts = x_ref.shape[1]

    @pl.when(s == 0)
    def _():
        zs_ref[0:POOL_HALO, :] = jnp.zeros((POOL_HALO, D_MODEL), _F32)

    for p0 in range(0, ts, POOL_SUB):
        z = _dot(x_ref[0, p0:p0 + POOL_SUB, :].astype(_BF16), win_ref[...])
        zs_ref[POOL_HALO + p0:POOL_HALO + p0 + POOL_SUB, :] = z
    for p0 in range(0, ts, POOL_SUB):
        x = x_ref[0, p0:p0 + POOL_SUB, :]
        base = POOL_HALO + p0
        pos = s * ts + p0 + lax.broadcasted_iota(jnp.int32, (POOL_SUB, 1), 0)
        for g, w in enumerate(POOL_WINDOWS):
            c0 = g * POOL_GROUP_DIM
            c1 = c0 + POOL_GROUP_DIM
            zg = zs_ref[base:base + POOL_SUB, c0:c1]
            ext = zs_ref[base - POOL_HALO:base + POOL_SUB, c0:c1]
            m = 1
            while m < w:
                ext = ext + pltpu.roll(ext, m, 0)
                m *= 2
            acc = ext[POOL_HALO:, :]
            cnt = jnp.minimum(pos + 1, w).astype(_F32)
            pooled = acc / cnt - zg
            yg = _dot(pooled.astype(_BF16), wgrp_ref[g]) * scale_ref[:, c0:c1]
            y_ref[p0:p0 + POOL_SUB, c0:c1] = yg.astype(_BF16)
        mix = _dot(y_ref[p0:p0 + POOL_SUB, :], wout_ref[...])
        out = _layer_norm(DEEPNORM_ALPHA * x + mix, g_ref[...], b_ref[...])
        o_ref[0, p0:p0 + POOL_SUB, :] = out
        _store_pieces(op_ref, p0, _pack_halves(out))
    zs_ref[0:POOL_HALO, :] = zs_ref[ts:ts + POOL_HALO, :]


def _pool_layer(x, part, w_in, w_grp, scale, w_out, ln_g, ln_b):
    bsz, seq, d = x.shape
    rows = bsz // TOKEN_PARTS
    ts = POOL_TILE
    steps = seq // ts
    const2 = lambda b, s: (0, 0)
    out_shape = (jax.ShapeDtypeStruct((rows, seq, d), _F32),
                 jax.ShapeDtypeStruct((PIECES, rows * seq, LANES), _U32))
    tile = pl.BlockSpec((1, ts, d), lambda b, s: (b, s, 0))
    ptile = pl.BlockSpec((PIECES, ts, LANES), lambda b, s: (0, b * steps + s, 0))
    return pl.pallas_call(
        _pool_kernel,
        grid=(rows, steps),
        in_specs=[
            pl.BlockSpec((1, ts, d), lambda b, s: (part * rows + b, s, 0)),
            pl.BlockSpec((d, d), const2),
            pl.BlockSpec((len(POOL_WINDOWS), POOL_GROUP_DIM, POOL_GROUP_DIM), lambda b, s: (0, 0, 0)),
            pl.BlockSpec((1, d), const2),
            pl.BlockSpec((d, d), const2),
            pl.BlockSpec((1, d), const2),
            pl.BlockSpec((1, d), const2),
        ],
        out_specs=(tile, ptile),
        out_shape=out_shape,
        scratch_shapes=[pltpu.VMEM((POOL_HALO + ts, d), _F32),
                        pltpu.VMEM((ts, d), _BF16)],
        compiler_params=pltpu.CompilerParams(
            dimension_semantics=("arbitrary", "arbitrary"),
            vmem_limit_bytes=VMEM_LIMIT),
        name="pool_mixer",
    )(x, w_in.astype(_BF16), w_grp.astype(_BF16), scale.reshape(1, d),
      w_out.astype(_BF16), ln_g.reshape(1, d), ln_b.reshape(1, d))


def _sgu_kernel(x_ref, win_ref, bin_ref, lng_ref, lnb_ref, ws_ref, bs_ref, wout_ref,
                g_ref, b_ref, o_ref, op_ref, gated_ref):
    ts = x_ref.shape[0]
    projected = []
    for p0 in range(0, ts, SGU_SUB):
        x = x_ref[p0:p0 + SGU_SUB, :]
        xb = x.astype(_BF16)
        v = _gelu_tanh(_dot(xb, win_ref[:, SGU_WIDTH:]) + bin_ref[:, SGU_WIDTH:])
        v = _layer_norm(v, lng_ref[...], lnb_ref[...]).astype(_BF16)
        u = _gelu_tanh(_dot(xb, win_ref[:, :SGU_WIDTH]) + bin_ref[:, :SGU_WIDTH])
        projected.append((p0, x, u, v))
    for p0, x, u, v in projected:
        for r0 in range(0, SGU_SUB, SGU_CHUNK):
            for h in range(SGU_HEADS):
                c0 = h * SGU_HEAD_DIM
                sv = _dot(ws_ref[h], v[r0:r0 + SGU_CHUNK, c0:c0 + SGU_HEAD_DIM]) + bs_ref[h]
                gated_ref[p0 + r0:p0 + r0 + SGU_CHUNK, c0:c0 + SGU_HEAD_DIM] = (
                    u[r0:r0 + SGU_CHUNK, c0:c0 + SGU_HEAD_DIM] * sv).astype(_BF16)
        mix = _dot(gated_ref[p0:p0 + SGU_SUB, :], wout_ref[...])
        out = _layer_norm(DEEPNORM_ALPHA * x + mix, g_ref[...], b_ref[...])
        o_ref[p0:p0 + SGU_SUB, :] = out
        _store_pieces(op_ref, p0, _pack_halves(out))


def _all_max(a):
    return jnp.max(jnp.max(a, axis=0, keepdims=True), axis=1, keepdims=True)


def _all_min(a):
    return jnp.min(jnp.min(a, axis=0, keepdims=True), axis=1, keepdims=True)


def _all_sum(a):
    return jnp.sum(jnp.sum(a, axis=0, keepdims=True), axis=1, keepdims=True)


def _route_kernel(x_ref, whi_ref, wlo_ref, bias_ref, earlier_ref,
                  ek_ref, rk_ref, gk_ref, cnt_ref, carry_ref):
    @pl.when(pl.program_id(0) == 0)
    def _():
        carry_ref[...] = jnp.zeros_like(carry_ref)

    nt = lambda a, b: lax.dot_general(a, b, (((1,), (1,)), ((), ())),
                                      preferred_element_type=_F32)
    logits = []
    for p0 in range(0, x_ref.shape[0], ROUTE_SUB):
        x = x_ref[p0:p0 + ROUTE_SUB, :]
        x_hi = x.astype(_BF16)
        x_lo = (x - x_hi.astype(_F32)).astype(_BF16)
        logits.append(nt(whi_ref[...], x_hi) + (nt(whi_ref[...], x_lo) + nt(wlo_ref[...], x_hi)))
    for n, sub_logits in enumerate(logits):
        _route_sub_tile(sub_logits, n * ROUTE_SUB, bias_ref, earlier_ref,
                        ek_ref, rk_ref, gk_ref, carry_ref)
    cnt_ref[...] = carry_ref[...]


def _route_sub_tile(logits, p0, bias_ref, earlier_ref, ek_ref, rk_ref, gk_ref, carry_ref):
    ts = ROUTE_SUB
    scores = jax.nn.sigmoid(logits).reshape(N_GROUPS, GROUP_SIZE, ts)
    biased = scores + bias_ref[...]
    neg_inf = jnp.float32(-jnp.inf)
    shape3 = (N_GROUPS, GROUP_SIZE, ts)
    in_grp = lax.broadcasted_iota(jnp.int32, shape3, 1)
    grp = lax.broadcasted_iota(jnp.int32, shape3, 0)
    eid = grp * GROUP_SIZE + in_grp

    m1 = jnp.max(biased, axis=1, keepdims=True)
    first1 = jnp.min(jnp.where(biased == m1, in_grp, GROUP_SIZE), axis=1, keepdims=True)
    m2 = jnp.max(jnp.where(in_grp == first1, neg_inf, biased), axis=1, keepdims=True)
    gscore = m1 + m2

    gid = lax.broadcasted_iota(jnp.int32, (N_GROUPS, 1, ts), 0)
    gsel = jnp.zeros((N_GROUPS, 1, ts), jnp.bool_)
    for _ in range(TOPK_GROUPS):
        m = jnp.max(gscore, axis=0, keepdims=True)
        first = jnp.min(jnp.where(gscore == m, gid, N_GROUPS), axis=0, keepdims=True)
        pick = gid == first
        gsel = jnp.logical_or(gsel, pick)
        gscore = jnp.where(pick, neg_inf, gscore)

    masked = jnp.where(gsel, biased, neg_inf)
    picked_any = jnp.zeros(shape3, jnp.bool_)
    e_k, s_k = [], []
    for _ in range(TOP_K):
        m = _all_max(masked)
        first = _all_min(jnp.where(masked == m, eid, N_EXPERTS))
        pick = eid == first
        picked_any = jnp.logical_or(picked_any, pick)
        masked = jnp.where(pick, neg_inf, masked)
        e_k.append(first)
        s_k.append(_all_sum(jnp.where(pick, scores, 0.0)))
    denom = s_k[0]
    for k in range(1, TOP_K):
        denom = denom + s_k[k]

    sel = picked_any.astype(_F32).reshape(N_EXPERTS, ts)
    rank = _dot(sel.astype(_BF16), earlier_ref[...]) + carry_ref[...]
    rank3 = rank.reshape(shape3)
    carry_ref[...] += jnp.sum(sel, axis=1, keepdims=True)

    for k in range(TOP_K):
        r = _all_sum(jnp.where(eid == e_k[k], rank3, 0.0))
        ek_ref[k:k + 1, p0:p0 + ts] = e_k[k].reshape(1, ts)
        rk_ref[k:k + 1, p0:p0 + ts] = r.reshape(1, ts).astype(jnp.int32)
        gk_ref[k:k + 1, p0:p0 + ts] = (s_k[k] / denom * ROUTED_SCALE).reshape(1, ts)


def _route(x, w_router, router_bias):
    n_part, d = x.shape
    ts = ROUTE_TILE
    steps = n_part // ts
    out_shape = (jax.ShapeDtypeStruct((TOP_K, n_part), jnp.int32),
                 jax.ShapeDtypeStruct((TOP_K, n_part), jnp.int32),
                 jax.ShapeDtypeStruct((TOP_K, n_part), _F32),
                 jax.ShapeDtypeStruct((N_EXPERTS, 1), _F32))
    kspec = pl.BlockSpec((TOP_K, ts), lambda i: (0, i))
    const2 = lambda i: (0, 0)
    w_t = w_router.T
    w_hi = w_t.astype(_BF16)
    w_lo = (w_t - w_hi.astype(_F32)).astype(_BF16)
    pos = jnp.arange(ROUTE_SUB, dtype=jnp.int32)
    earlier = (pos[:, None] < pos[None, :]).astype(_BF16)
    return pl.pallas_call(
        _route_kernel,
        grid=(steps,),
        in_specs=[
            pl.BlockSpec((ts, d), lambda i: (i, 0)),
            pl.BlockSpec((N_EXPERTS, d), const2),
            pl.BlockSpec((N_EXPERTS, d), const2),
            pl.BlockSpec((N_GROUPS, GROUP_SIZE, 1), lambda i: (0, 0, 0)),
            pl.BlockSpec((ROUTE_SUB, ROUTE_SUB), const2),
        ],
        out_specs=(kspec, kspec, kspec, pl.BlockSpec((N_EXPERTS, 1), const2)),
        out_shape=out_shape,
        scratch_shapes=[pltpu.VMEM((N_EXPERTS, 1), _F32)],
        compiler_params=pltpu.CompilerParams(
            dimension_semantics=("arbitrary",),
            vmem_limit_bytes=VMEM_LIMIT),
        name="moe_route",
    )(x, w_hi, w_lo, router_bias.reshape(N_GROUPS, GROUP_SIZE, 1), earlier)


def _sc_mesh():
    return plsc.VectorSubcoreMesh(core_axis_name="core", subcore_axis_name="subcore")


def _sc_scatter_rows(src, idx, n_out):
    d = src.shape[1]
    wins = idx.shape[1] // SC_WINDOW

    def body(src_hbm, idx_hbm, out_hbm, sem):
        def step(src_vmem, idx_vmem):
            copies = [pltpu.make_async_copy(src_vmem, out_hbm.at[idx_vmem.at[k]], sem)
                      for k in range(TOP_K)]
            for copy in copies:
                copy.start()
            for copy in copies:
                copy.wait()

        pltpu.emit_pipeline(
            step,
            grid=(src.shape[0] // SC_WINDOW,),
            in_specs=[pl.BlockSpec((SC_WINDOW, d), index_map=lambda i: (i, 0)),
                      pl.BlockSpec((TOP_K, SC_WINDOW), index_map=lambda i: (i // wins, i % wins))],
            out_specs=[],
            core_axis_name=("core", "subcore"),
            dimension_semantics=(pltpu.PARALLEL,),
        )(src_hbm, idx_hbm)

    return pl.kernel(body, out_type=jax.ShapeDtypeStruct((n_out, d), src.dtype),
                     mesh=_sc_mesh(), scratch_types=[pltpu.SemaphoreType.DMA],
                     name="sc_scatter_rows")(src, idx)


def _sc_gather_sum(table, idx, gates):
    d = table.shape[1]
    n_tok = idx.shape[1]
    wins = n_tok // SC_WINDOW
    lanes = SC_LANES
    out_type = jax.ShapeDtypeStruct((PIECES * n_tok, d), _F32)

    def tree_sum(terms):
        while len(terms) > 1:
            terms = [terms[i] + terms[i + 1] for i in range(0, len(terms), 2)]
        return terms[0]

    def body(table_hbm, idx_hbm, gate_hbm, lo_hbm, hi_hbm, rows_vmem, sem):
        def step(idx_vmem, gate_vmem, lo_vmem, hi_vmem):
            def burst_copies(b):
                return [pltpu.make_async_copy(
                    table_hbm.at[idx_vmem.at[k, pl.ds(b * SC_BURST, SC_BURST)]],
                    rows_vmem.at[b % 2, k], sem.at[b % 2]) for k in range(TOP_K)]

            n_bursts = SC_WINDOW // SC_BURST
            for copy in burst_copies(0):
                copy.start()
            for b in range(n_bursts):
                if b + 1 < n_bursts:
                    for copy in burst_copies(b + 1):
                        copy.start()
                for copy in burst_copies(b):
                    copy.wait()
                t0 = b * SC_BURST
                rows = rows_vmem.at[b % 2]

                @plsc.parallel_loop(0, SC_BURST)
                def _(r):
                    t = t0 + r
                    token = jnp.full((lanes,), t, jnp.int32)
                    gate = [plsc.load_gather(gate_vmem, [jnp.full((lanes,), k, jnp.int32), token])
                            for k in range(TOP_K)]
                    for j in range(0, d, lanes):
                        words = [rows[k, r, pl.ds(j, lanes)] for k in range(TOP_K)]
                        lo_vmem[t, pl.ds(j, lanes)] = tree_sum(
                            [gate[k] * lax.bitcast_convert_type(words[k] << 16, _F32)
                             for k in range(TOP_K)])
                        hi_vmem[t, pl.ds(j, lanes)] = tree_sum(
                            [gate[k] * lax.bitcast_convert_type(words[k] & _U32(0xFFFF0000), _F32)
                             for k in range(TOP_K)])

        window = lambda i: (i // wins, i % wins)
        pltpu.emit_pipeline(
            step,
            grid=(PIECES * wins,),
            in_specs=[pl.BlockSpec((TOP_K, SC_WINDOW), index_map=window),
                      pl.BlockSpec((TOP_K, SC_WINDOW), index_map=lambda i: (0, i % wins))],
            out_specs=[pl.BlockSpec((SC_WINDOW, d), index_map=lambda i: (i, 0)),
                       pl.BlockSpec((SC_WINDOW, d), index_map=lambda i: (i, 0))],
            core_axis_name=("core", "subcore"),
            dimension_semantics=(pltpu.PARALLEL,),
        )(idx_hbm, gate_hbm, lo_hbm, hi_hbm)

    return pl.kernel(body, out_type=(out_type, out_type), mesh=_sc_mesh(),
                     scratch_types=[pltpu.VMEM((2, TOP_K, SC_BURST, d), _U32),
                                    pltpu.SemaphoreType.DMA((2,))],
                     compiler_params=pltpu.CompilerParams(needs_layout_passes=False),
                     name="sc_gather_sum")(table, idx, gates)


def _index_kernel(start_ref, ek_ref, rk_ref, idx_ref, *, n_rows):
    e = ek_ref[...]
    start = jnp.zeros(e.shape, jnp.int32)
    for ex in range(N_EXPERTS):
        start = jnp.where(e == ex, start_ref[ex], start)
    dest = start + rk_ref[...]
    for c in range(PIECES):
        idx_ref[c] = dest + c * n_rows


def _row_indices(padded_start, e_k, r_k, n_rows):
    n_tok = e_k.shape[1]
    ts = INDEX_TILE
    kspec = pl.BlockSpec((TOP_K, ts), lambda i, st: (0, i))
    grid_spec = pltpu.PrefetchScalarGridSpec(
        num_scalar_prefetch=1,
        grid=(n_tok // ts,),
        in_specs=[kspec, kspec],
        out_specs=pl.BlockSpec((PIECES, TOP_K, ts), lambda i, st: (0, 0, i)),
    )
    return pl.pallas_call(
        functools.partial(_index_kernel, n_rows=n_rows),
        grid_spec=grid_spec,
        out_shape=jax.ShapeDtypeStruct((PIECES, TOP_K, n_tok), jnp.int32),
        compiler_params=pltpu.CompilerParams(dimension_semantics=("arbitrary",)),
        name="moe_row_indices",
    )(padded_start, e_k, r_k)


def _expert_kernel(chunk0_ref, nchunk_ref, cnt_ref, total_ref,
                   xs_hbm, wg_ref, wu_ref, wd_ref, o_hbm,
                   xbuf, obuf, sem_in, sem_out):
    e = pl.program_id(0)
    total = total_ref[0]
    ch = EXPERT_ROWS
    ahead = IN_SLOTS - 1

    def in_copy(g):
        slot = g % IN_SLOTS
        return pltpu.make_async_copy(xs_hbm.at[:, pl.ds(g * ch, ch), :], xbuf.at[slot],
                                     sem_in.at[slot])

    def out_copy(g):
        slot = g % OUT_SLOTS
        return pltpu.make_async_copy(obuf.at[slot], o_hbm.at[:, pl.ds(g * ch, ch), :],
                                     sem_out.at[slot])

    @pl.when(e == 0)
    def _():
        obuf[...] = jnp.zeros_like(obuf)
        for g in range(ahead):
            @pl.when(g < total)
            def _():
                in_copy(g).start()


    def chunk(j, carry):
        g = chunk0_ref[e] + j
        in_copy(g).wait()

        @pl.when(g + ahead < total)
        def _():
            in_copy(g + ahead).start()

        @pl.when(g >= OUT_SLOTS)
        def _():
            out_copy(g - OUT_SLOTS).wait()

        islot = g % IN_SLOTS
        oslot = g % OUT_SLOTS
        valid = cnt_ref[e] - j * ch

        def gate_up(s):
            r0 = s * EXPERT_SUB
            w = jnp.concatenate([xbuf[islot, c, r0:r0 + EXPERT_SUB, :] for c in range(PIECES)],
                                axis=1)
            lo, hi = _unpack_halves(w)
            gate = _dot(lo, wg_ref[0, 0, :PACKED, :]) + _dot(hi, wg_ref[0, 0, PACKED:, :])
            up = _dot(lo, wu_ref[0, 0, :PACKED, :]) + _dot(hi, wu_ref[0, 0, PACKED:, :])
            return gate, up

        def down(s, gate_up_pair):
            r0 = s * EXPERT_SUB
            gate, up = gate_up_pair
            y = _pack_halves(_dot(_silu(gate) * up, wd_ref[0, 0]))
            for c in range(PIECES):
                obuf[oslot, c, r0:r0 + EXPERT_SUB, :] = y[:, c * LANES:(c + 1) * LANES]

        n_sub = ch // EXPERT_SUB

        all_subs = valid > ch - EXPERT_SUB

        @pl.when(all_subs)
        def _():
            projected = [gate_up(s) for s in range(n_sub)]
            for s, gu in enumerate(projected):
                down(s, gu)

        @pl.when(jnp.logical_not(all_subs))
        def _():
            for s in range(n_sub - 1):
                @pl.when(s * EXPERT_SUB < valid)
                def _():
                    down(s, gate_up(s))

        out_copy(g).start()
        return carry

    lax.fori_loop(0, nchunk_ref[e], chunk, 0)

    @pl.when(e == pl.num_programs(0) - 1)
    def _():
        for back in range(OUT_SLOTS, 0, -1):
            @pl.when(total >= back)
            def _():
                out_copy(total - back).wait()


def _experts(layer, chunk0, nchunk, counts, total, xs, w_gate, w_up, w_down):
    d = D_MODEL
    ch = EXPERT_ROWS
    wspec_in = pl.BlockSpec((1, 1, d, EXPERT_DIM), lambda e, *_: (layer, e, 0, 0))
    any_spec = pl.BlockSpec(memory_space=pl.ANY)
    grid_spec = pltpu.PrefetchScalarGridSpec(
        num_scalar_prefetch=4,
        grid=(N_EXPERTS,),
        in_specs=[
            any_spec,
            wspec_in,
            wspec_in,
            pl.BlockSpec((1, 1, EXPERT_DIM, d), lambda e, *_: (layer, e, 0, 0)),
        ],
        out_specs=any_spec,
        scratch_shapes=[pltpu.VMEM((IN_SLOTS, PIECES, ch, LANES), _U32),
                        pltpu.VMEM((OUT_SLOTS, PIECES, ch, LANES), _U32),
                        pltpu.SemaphoreType.DMA((IN_SLOTS,)),
                        pltpu.SemaphoreType.DMA((OUT_SLOTS,))],
    )
    return pl.pallas_call(
        _expert_kernel,
        grid_spec=grid_spec,
        out_shape=jax.ShapeDtypeStruct(xs.shape, _U32),
        compiler_params=pltpu.CompilerParams(
            dimension_semantics=("arbitrary",),
            vmem_limit_bytes=VMEM_LIMIT),
        name="moe_experts",
    )(chunk0, nchunk, counts, total, xs, w_gate, w_up, w_down)


def _moe_residual(x, lo_ref, hi_ref, rows, sg_ref, su_ref, sd_ref, g_ref, b_ref):
    xb = x.astype(_BF16)
    routed = jnp.concatenate([lo_ref[c, rows, :] for c in range(PIECES)] +
                             [hi_ref[c, rows, :] for c in range(PIECES)], axis=1)
    h = _silu(_dot(xb, sg_ref[...])) * _dot(xb, su_ref[...])
    shared = _dot(h.astype(_BF16), sd_ref[...])
    return _layer_norm(DEEPNORM_ALPHA * x + (routed + shared), g_ref[...], b_ref[...])


def _combine_kernel(x_ref, lo_ref, hi_ref, sg_ref, su_ref, sd_ref, g_ref, b_ref, *rest):
    o_ref = rest[-1]
    o_ref[...] = _moe_residual(x_ref[...], lo_ref, hi_ref, slice(None),
                               sg_ref, su_ref, sd_ref, g_ref, b_ref)


def _combine_sgu_kernel(x_ref, lo_ref, hi_ref, sg_ref, su_ref, sd_ref, fg_ref, fb_ref,
                        win_ref, bin_ref, lng_ref, lnb_ref, ws_ref, bs_ref, wout_ref,
                        mg_ref, mb_ref, o_ref, op_ref, gated_ref, h_ref):
    for p0 in range(0, x_ref.shape[0], SGU_SUB):
        rows = slice(p0, p0 + SGU_SUB)
        h_ref[rows, :] = _moe_residual(x_ref[rows, :], lo_ref, hi_ref, rows,
                                       sg_ref, su_ref, sd_ref, fg_ref, fb_ref)
    _sgu_kernel(h_ref, win_ref, bin_ref, lng_ref, lnb_ref, ws_ref, bs_ref, wout_ref,
                mg_ref, mb_ref, o_ref, op_ref, gated_ref)


def _combine(x, x_row0, routed_lo, routed_hi, sh_gate, sh_up, sh_down, ln_g, ln_b,
             out_rows, out_row0, prev):
    d = x.shape[1]
    n_slice = routed_lo.shape[1]
    ts = COMBINE_TILE
    steps = n_slice // ts
    x_step0 = x_row0 // ts
    out_step0 = out_row0 // ts
    const2 = lambda i: (0, 0)
    in_specs = [
        pl.BlockSpec((ts, d), lambda i: (x_step0 + i, 0)),
        pl.BlockSpec((PIECES, ts, LANES), lambda i: (0, i, 0)),
        pl.BlockSpec((PIECES, ts, LANES), lambda i: (0, i, 0)),
        pl.BlockSpec((d, EXPERT_DIM), const2),
        pl.BlockSpec((d, EXPERT_DIM), const2),
        pl.BlockSpec((EXPERT_DIM, d), const2),
        pl.BlockSpec((1, d), const2),
        pl.BlockSpec((1, d), const2),
    ]
    args = [x, routed_lo, routed_hi, sh_gate.astype(_BF16), sh_up.astype(_BF16), sh_down.astype(_BF16),
            ln_g.reshape(1, d), ln_b.reshape(1, d)]
    aliases = {}
    if prev is not None:
        in_specs.append(pl.BlockSpec(memory_space=pl.ANY))
        aliases = {len(args): 0}
        args.append(prev)
    return pl.pallas_call(
        _combine_kernel,
        grid=(steps,),
        in_specs=in_specs,
        out_specs=pl.BlockSpec((ts, d), lambda i: (out_step0 + i, 0)),
        out_shape=jax.ShapeDtypeStruct((out_rows, d), _F32),
        input_output_aliases=aliases,
        compiler_params=pltpu.CompilerParams(
            dimension_semantics=("arbitrary",),
            vmem_limit_bytes=VMEM_LIMIT),
        name="moe_combine",
    )(*args)


def _combine_sgu(x, routed_lo, routed_hi, sh_gate, sh_up, sh_down, ffn_g, ffn_b,
                 w_in, b_in, ln_g, ln_b, w_s, b_s, w_out, mix_g, mix_b):
    n_part, d = x.shape
    ts = SGU_TILE
    const2 = lambda i: (0, 0)
    const3 = lambda i: (0, 0, 0)
    causal = jnp.tril(jnp.ones((SGU_CHUNK, SGU_CHUNK), w_s.dtype))
    ws = (w_s * causal[None]).astype(_BF16)
    tile = pl.BlockSpec((ts, d), lambda i: (i, 0))
    ptile = pl.BlockSpec((PIECES, ts, LANES), lambda i: (0, i, 0))
    row = lambda n: pl.BlockSpec((1, n), const2)
    return pl.pallas_call(
        _combine_sgu_kernel,
        grid=(n_part // ts,),
        in_specs=[
            tile, ptile, ptile,
            pl.BlockSpec((d, EXPERT_DIM), const2),
            pl.BlockSpec((d, EXPERT_DIM), const2),
            pl.BlockSpec((EXPERT_DIM, d), const2),
            row(d), row(d),
            pl.BlockSpec((d, 2 * SGU_WIDTH), const2),
            row(2 * SGU_WIDTH), row(SGU_WIDTH), row(SGU_WIDTH),
            pl.BlockSpec((SGU_HEADS, SGU_CHUNK, SGU_CHUNK), const3),
            pl.BlockSpec((SGU_HEADS, SGU_CHUNK, 1), const3),
            pl.BlockSpec((SGU_WIDTH, d), const2),
            row(d), row(d),
        ],
        out_specs=(tile, ptile),
        out_shape=(jax.ShapeDtypeStruct((n_part, d), _F32),
                   jax.ShapeDtypeStruct((PIECES, n_part, LANES), _U32)),
        scratch_shapes=[pltpu.VMEM((ts, SGU_WIDTH), _BF16), pltpu.VMEM((ts, d), _F32)],
        compiler_params=pltpu.CompilerParams(
            dimension_semantics=("arbitrary",),
            vmem_limit_bytes=VMEM_LIMIT),
        name="moe_combine_sgu_mixer",
    )(x, routed_lo, routed_hi, sh_gate.astype(_BF16), sh_up.astype(_BF16), sh_down.astype(_BF16),
      ffn_g.reshape(1, d), ffn_b.reshape(1, d),
      w_in.astype(_BF16), b_in.reshape(1, -1), ln_g.reshape(1, -1), ln_b.reshape(1, -1),
      ws, b_s.reshape(SGU_HEADS, SGU_CHUNK, 1), w_out.astype(_BF16),
      mix_g.reshape(1, d), mix_b.reshape(1, d))


def _moe_dispatch(layer, x, x_packed, w_router, router_bias, w_gate, w_up, w_down):
    n_part, d = x.shape
    bm = EXPERT_ROWS
    n_rows = n_part * TOP_K + N_EXPERTS * bm
    e_k, r_k, g_k, counts = _route(x, w_router, router_bias)
    counts = counts.reshape(N_EXPERTS).astype(jnp.int32)
    padded = (counts + bm - 1) // bm * bm
    padded_end = jnp.cumsum(padded)
    padded_start = padded_end - padded
    idx = _row_indices(padded_start, e_k, r_k, n_rows).reshape(PIECES * TOP_K, n_part)
    xs = _sc_scatter_rows(x_packed.reshape(PIECES * n_part, LANES), idx, PIECES * n_rows)
    rows = _experts(layer, padded_start // bm, padded // bm, counts, padded_end[-1:] // bm,
                    xs.reshape(PIECES, n_rows, LANES), w_gate, w_up, w_down)
    return rows.reshape(PIECES * n_rows, LANES), idx, g_k


def _routed_sums(rows, idx, gates):
    n_tok = idx.shape[1]
    lo, hi = _sc_gather_sum(rows, idx, gates)
    return lo.reshape(PIECES, n_tok, LANES), hi.reshape(PIECES, n_tok, LANES)


def kernel(x, pool_w_in, pool_w_grp, pool_scale, pool_w_out, sgu_w_in, sgu_b_in, sgu_ln_g, sgu_ln_b, sgu_w_s, sgu_b_s, sgu_w_out, ln_mix_g, ln_mix_b, moe_w_router, moe_router_bias, moe_w_gate, moe_w_up, moe_w_down, moe_sh_gate, moe_sh_up, moe_sh_down, ln_ffn_g, ln_ffn_b):
    bsz, seq, d = x.shape
    parts = range(TOKEN_PARTS)

    def dispatch(i, h, h_packed):
        return _moe_dispatch(i, h, h_packed, moe_w_router[i], moe_router_bias[i],
                             moe_w_gate, moe_w_up, moe_w_down)

    mixed = [_pool_layer(x, p, pool_w_in[0], pool_w_grp[0], pool_scale[0], pool_w_out[0],
                         ln_mix_g[0], ln_mix_b[0]) for p in parts]
    mixed = [(h.reshape(-1, d), h_packed) for h, h_packed in mixed]
    routed = [dispatch(0, h, h_packed) for h, h_packed in mixed]
    mixed = [_combine_sgu(mixed[p][0], *_routed_sums(*routed[p]),
                          moe_sh_gate[0], moe_sh_up[0], moe_sh_down[0], ln_ffn_g[0], ln_ffn_b[0],
                          sgu_w_in[0], sgu_b_in[0], sgu_ln_g[0], sgu_ln_b[0], sgu_w_s[0],
                          sgu_b_s[0], sgu_w_out[0], ln_mix_g[1], ln_mix_b[1]) for p in parts]
    routed = [dispatch(1, h, h_packed) for h, h_packed in mixed]
    out = None
    for p in parts:
        h = mixed[p][0]
        rows, idx, gates = routed[p]
        n_part = h.shape[0]
        n_slices = TAIL_SLICES if p == TOKEN_PARTS - 1 else 1
        n = n_part // n_slices
        for s in range(n_slices):
            tok = slice(s * n, (s + 1) * n)
            out = _combine(h, s * n, *_routed_sums(rows, idx[:, tok], gates[:, tok]),
                           moe_sh_gate[1], moe_sh_up[1], moe_sh_down[1], ln_ffn_g[1],
                           ln_ffn_b[1], bsz * seq, p * n_part + s * n, out)
    return out.reshape(bsz, seq, d)
```

```python
import functools

import jax
import jax.numpy as jnp
from jax import lax
from jax.experimental import pallas as pl
from jax.experimental.pallas import tpu as pltpu
from jax.experimental.pallas import tpu_sc as plsc

D_MODEL = 1024
DEPTH = 2
POOL_WINDOWS = (2, 4, 8, 16)
POOL_GROUP_DIM = D_MODEL // len(POOL_WINDOWS)
POOL_HALO = 16
SGU_CHUNK = 128
SGU_HEADS = 4
SGU_WIDTH = 2 * D_MODEL
SGU_HEAD_DIM = SGU_WIDTH // SGU_HEADS
N_EXPERTS = 64
TOP_K = 8
N_GROUPS = 8
GROUP_SIZE = N_EXPERTS // N_GROUPS
TOPK_GROUPS = 4
EXPERT_DIM = D_MODEL // 4
ROUTED_SCALE = 2.5
DEEPNORM_ALPHA = (2 * DEPTH) ** 0.25
LN_EPS = 1e-5

LANES = 128
PACKED = D_MODEL // 2
PIECES = PACKED // LANES

POOL_TILE = 1024
POOL_SUB = 256
SGU_TILE = 512
SGU_SUB = 256
ROUTE_TILE = 1024
ROUTE_SUB = 512
INDEX_TILE = 2048
EXPERT_ROWS = 512
EXPERT_SUB = 256
IN_SLOTS = 6
OUT_SLOTS = 4
COMBINE_TILE = 1024
SC_WINDOW = 128
SC_LANES = 16
SC_BURST = 16
TOKEN_PARTS = 2
V7X_VMEM_BYTES = 64 * 1024 * 1024
VMEM_LIMIT = V7X_VMEM_BYTES * 7 // 8

_F32 = jnp.float32
_BF16 = jnp.bfloat16
_U32 = jnp.uint32


def _dot(a, b):
    return jnp.dot(a, b, preferred_element_type=_F32)


def _layer_norm(h, g, b):
    mu = jnp.mean(h, axis=-1, keepdims=True)
    hc = h - mu
    var = jnp.mean(hc * hc, axis=-1, keepdims=True)
    return hc * lax.rsqrt(var + LN_EPS) * g + b


def _silu(x):
    return x * jax.nn.sigmoid(x)


def _gelu_tanh(x):
    c = 0.7978845608028654
    return x * (0.5 + 0.5 * jnp.tanh(x * (c + (c * 0.044715) * (x * x))))


def _pack_halves(v):
    half = v.shape[1] // 2
    lo = lax.bitcast_convert_type(v[:, :half].astype(_BF16).astype(_F32), _U32)
    hi = lax.bitcast_convert_type(v[:, half:].astype(_BF16).astype(_F32), _U32)
    return hi | (lo >> 16)


def _unpack_halves(w):
    lo = lax.bitcast_convert_type(w << 16, _F32)
    hi = lax.bitcast_convert_type(w & _U32(0xFFFF0000), _F32)
    return lo, hi


def _store_pieces(ref, r0, w):
    for c in range(PIECES):
        ref[c, r0:r0 + w.shape[0], :] = w[:, c * LANES:(c + 1) * LANES]


def _pool_kernel(x_ref, win_ref, wgrp_ref, scale_ref, wout_ref, g_ref, b_ref,
                 o_ref, op_ref, zs_ref, y_ref):
    s = pl.program_id(1)
    ts = x_ref.shape[1]

    @pl.when(s == 0)
    def _():
        zs_ref[0:POOL_HALO, :] = jnp.zeros((POOL_HALO, D_MODEL), _F32)

    for p0 in range(0, ts, POOL_SUB):
        z = _dot(x_ref[0, p0:p0 + POOL_SUB, :].astype(_BF16), win_ref[...])
        zs_ref[POOL_HALO + p0:POOL_HALO + p0 + POOL_SUB, :] = z
    for p0 in range(0, ts, POOL_SUB):
        x = x_ref[0, p0:p0 + POOL_SUB, :]
        base = POOL_HALO + p0
        pos = s * ts + p0 + lax.broadcasted_iota(jnp.int32, (POOL_SUB, 1), 0)
        for g, w in enumerate(POOL_WINDOWS):
            c0 = g * POOL_GROUP_DIM
            c1 = c0 + POOL_GROUP_DIM
            zg = zs_ref[base:base + POOL_SUB, c0:c1]
            ext = zs_ref[base - POOL_HALO:base + POOL_SUB, c0:c1]
            m = 1
            while m < w:
                ext = ext + pltpu.roll(ext, m, 0)
                m *= 2
            acc = ext[POOL_HALO:, :]
            cnt = jnp.minimum(pos + 1, w).astype(_F32)
            pooled = acc / cnt - zg
            yg = _dot(pooled.astype(_BF16), wgrp_ref[g]) * scale_ref[:, c0:c1]
            y_ref[p0:p0 + POOL_SUB, c0:c1] = yg.astype(_BF16)
        mix = _dot(y_ref[p0:p0 + POOL_SUB, :], wout_ref[...])
        out = _layer_norm(DEEPNORM_ALPHA * x + mix, g_ref[...], b_ref[...])
        o_ref[0, p0:p0 + POOL_SUB, :] = out
        _store_pieces(op_ref, p0, _pack_halves(out))
    zs_ref[0:POOL_HALO, :] = zs_ref[ts:ts + POOL_HALO, :]


def _pool_layer(x, part, w_in, w_grp, scale, w_out, ln_g, ln_b):
    bsz, seq, d = x.shape
    rows = bsz // TOKEN_PARTS
    ts = POOL_TILE
    steps = seq // ts
    const2 = lambda b, s: (0, 0)
    out_shape = (jax.ShapeDtypeStruct((rows, seq, d), _F32),
                 jax.ShapeDtypeStruct((PIECES, rows * seq, LANES), _U32))
    tile = pl.BlockSpec((1, ts, d), lambda b, s: (b, s, 0))
    ptile = pl.BlockSpec((PIECES, ts, LANES), lambda b, s: (0, b * steps + s, 0))
    return pl.pallas_call(
        _pool_kernel,
        grid=(rows, steps),
        in_specs=[
            pl.BlockSpec((1, ts, d), lambda b, s: (part * rows + b, s, 0)),
            pl.BlockSpec((d, d), const2),
            pl.BlockSpec((len(POOL_WINDOWS), POOL_GROUP_DIM, POOL_GROUP_DIM), lambda b, s: (0, 0, 0)),
            pl.BlockSpec((1, d), const2),
            pl.BlockSpec((d, d), const2),
            pl.BlockSpec((1, d), const2),
            pl.BlockSpec((1, d), const2),
        ],
        out_specs=(tile, ptile),
        out_shape=out_shape,
        scratch_shapes=[pltpu.VMEM((POOL_HALO + ts, d), _F32),
                        pltpu.VMEM((ts, d), _BF16)],
        compiler_params=pltpu.CompilerParams(
            dimension_semantics=("arbitrary", "arbitrary"),
            vmem_limit_bytes=VMEM_LIMIT),
        name="pool_mixer",
    )(x, w_in.astype(_BF16), w_grp.astype(_BF16), scale.reshape(1, d),
      w_out.astype(_BF16), ln_g.reshape(1, d), ln_b.reshape(1, d))


def _sgu_kernel(x_ref, win_ref, bin_ref, lng_ref, lnb_ref, ws_ref, bs_ref, wout_ref,
                g_ref, b_ref, o_ref, op_ref, gated_ref):
    ts = x_ref.shape[0]
    projected = []
    for p0 in range(0, ts, SGU_SUB):
        x = x_ref[p0:p0 + SGU_SUB, :]
        xb = x.astype(_BF16)
        v = _gelu_tanh(_dot(xb, win_ref[:, SGU_WIDTH:]) + bin_ref[:, SGU_WIDTH:])
        v = _layer_norm(v, lng_ref[...], lnb_ref[...]).astype(_BF16)
        u = _gelu_tanh(_dot(xb, win_ref[:, :SGU_WIDTH]) + bin_ref[:, :SGU_WIDTH])
        projected.append((p0, x, u, v))
    for p0, x, u, v in projected:
        for r0 in range(0, SGU_SUB, SGU_CHUNK):
            for h in range(SGU_HEADS):
                c0 = h * SGU_HEAD_DIM
                sv = _dot(ws_ref[h], v[r0:r0 + SGU_CHUNK, c0:c0 + SGU_HEAD_DIM]) + bs_ref[h]
                gated_ref[p0 + r0:p0 + r0 + SGU_CHUNK, c0:c0 + SGU_HEAD_DIM] = (
                    u[r0:r0 + SGU_CHUNK, c0:c0 + SGU_HEAD_DIM] * sv).astype(_BF16)
        mix = _dot(gated_ref[p0:p0 + SGU_SUB, :], wout_ref[...])
        out = _layer_norm(DEEPNORM_ALPHA * x + mix, g_ref[...], b_ref[...])
        o_ref[p0:p0 + SGU_SUB, :] = out
        _store_pieces(op_ref, p0, _pack_halves(out))


def _all_max(a):
    return jnp.max(jnp.max(a, axis=0, keepdims=True), axis=1, keepdims=True)


def _all_min(a):
    return jnp.min(jnp.min(a, axis=0, keepdims=True), axis=1, keepdims=True)


def _all_sum(a):
    return jnp.sum(jnp.sum(a, axis=0, keepdims=True), axis=1, keepdims=True)


def _route_kernel(x_ref, whi_ref, wlo_ref, bias_ref, earlier_ref,
                  ek_ref, rk_ref, gk_ref, cnt_ref, carry_ref):
    @pl.when(pl.program_id(0) == 0)
    def _():
        carry_ref[...] = jnp.zeros_like(carry_ref)

    nt = lambda a, b: lax.dot_general(a, b, (((1,), (1,)), ((), ())),
                                      preferred_element_type=_F32)
    logits = []
    for p0 in range(0, x_ref.shape[0], ROUTE_SUB):
        x = x_ref[p0:p0 + ROUTE_SUB, :]
        x_hi = x.astype(_BF16)
        x_lo = (x - x_hi.astype(_F32)).astype(_BF16)
        logits.append(nt(whi_ref[...], x_hi) + (nt(whi_ref[...], x_lo) + nt(wlo_ref[...], x_hi)))
    for n, sub_logits in enumerate(logits):
        _route_sub_tile(sub_logits, n * ROUTE_SUB, bias_ref, earlier_ref,
                        ek_ref, rk_ref, gk_ref, carry_ref)
    cnt_ref[...] = carry_ref[...]


def _route_sub_tile(logits, p0, bias_ref, earlier_ref, ek_ref, rk_ref, gk_ref, carry_ref):
    ts = ROUTE_SUB
    scores = jax.nn.sigmoid(logits).reshape(N_GROUPS, GROUP_SIZE, ts)
    biased = scores + bias_ref[...]
    neg_inf = jnp.float32(-jnp.inf)
    shape3 = (N_GROUPS, GROUP_SIZE, ts)
    in_grp = lax.broadcasted_iota(jnp.int32, shape3, 1)
    grp = lax.broadcasted_iota(jnp.int32, shape3, 0)
    eid = grp * GROUP_SIZE + in_grp

    m1 = jnp.max(biased, axis=1, keepdims=True)
    first1 = jnp.min(jnp.where(biased == m1, in_grp, GROUP_SIZE), axis=1, keepdims=True)
    m2 = jnp.max(jnp.where(in_grp == first1, neg_inf, biased), axis=1, keepdims=True)
    gscore = m1 + m2

    gid = lax.broadcasted_iota(jnp.int32, (N_GROUPS, 1, ts), 0)
    gsel = jnp.zeros((N_GROUPS, 1, ts), jnp.bool_)
    for _ in range(TOPK_GROUPS):
        m = jnp.max(gscore, axis=0, keepdims=True)
        first = jnp.min(jnp.where(gscore == m, gid, N_GROUPS), axis=0, keepdims=True)
        pick = gid == first
        gsel = jnp.logical_or(gsel, pick)
        gscore = jnp.where(pick, neg_inf, gscore)

    masked = jnp.where(gsel, biased, neg_inf)
    picked_any = jnp.zeros(shape3, jnp.bool_)
    e_k, s_k = [], []
    for _ in range(TOP_K):
        m = _all_max(masked)
        first = _all_min(jnp.where(masked == m, eid, N_EXPERTS))
        pick = eid == first
        picked_any = jnp.logical_or(picked_any, pick)
        masked = jnp.where(pick, neg_inf, masked)
        e_k.append(first)
        s_k.append(_all_sum(jnp.where(pick, scores, 0.0)))
    denom = s_k[0]
    for k in range(1, TOP_K):
        denom = denom + s_k[k]

    sel = picked_any.astype(_F32).reshape(N_EXPERTS, ts)
    rank = _dot(sel.astype(_BF16), earlier_ref[...]) + carry_ref[...]
    rank3 = rank.reshape(shape3)
    carry_ref[...] += jnp.sum(sel, axis=1, keepdims=True)

    for k in range(TOP_K):
        r = _all_sum(jnp.where(eid == e_k[k], rank3, 0.0))
        ek_ref[k:k + 1, p0:p0 + ts] = e_k[k].reshape(1, ts)
        rk_ref[k:k + 1, p0:p0 + ts] = r.reshape(1, ts).astype(jnp.int32)
        gk_ref[k:k + 1, p0:p0 + ts] = (s_k[k] / denom * ROUTED_SCALE).reshape(1, ts)


def _route(x, w_router, router_bias):
    n_part, d = x.shape
    ts = ROUTE_TILE
    steps = n_part // ts
    out_shape = (jax.ShapeDtypeStruct((TOP_K, n_part), jnp.int32),
                 jax.ShapeDtypeStruct((TOP_K, n_part), jnp.int32),
                 jax.ShapeDtypeStruct((TOP_K, n_part), _F32),
                 jax.ShapeDtypeStruct((N_EXPERTS, 1), _F32))
    kspec = pl.BlockSpec((TOP_K, ts), lambda i: (0, i))
    const2 = lambda i: (0, 0)
    w_t = w_router.T
    w_hi = w_t.astype(_BF16)
    w_lo = (w_t - w_hi.astype(_F32)).astype(_BF16)
    pos = jnp.arange(ROUTE_SUB, dtype=jnp.int32)
    earlier = (pos[:, None] < pos[None, :]).astype(_BF16)
    return pl.pallas_call(
        _route_kernel,
        grid=(steps,),
        in_specs=[
            pl.BlockSpec((ts, d), lambda i: (i, 0)),
            pl.BlockSpec((N_EXPERTS, d), const2),
            pl.BlockSpec((N_EXPERTS, d), const2),
            pl.BlockSpec((N_GROUPS, GROUP_SIZE, 1), lambda i: (0, 0, 0)),
            pl.BlockSpec((ROUTE_SUB, ROUTE_SUB), const2),
        ],
        out_specs=(kspec, kspec, kspec, pl.BlockSpec((N_EXPERTS, 1), const2)),
        out_shape=out_shape,
        scratch_shapes=[pltpu.VMEM((N_EXPERTS, 1), _F32)],
        compiler_params=pltpu.CompilerParams(
            dimension_semantics=("arbitrary",),
            vmem_limit_bytes=VMEM_LIMIT),
        name="moe_route",
    )(x, w_hi, w_lo, router_bias.reshape(N_GROUPS, GROUP_SIZE, 1), earlier)


def _sc_mesh():
    return plsc.VectorSubcoreMesh(core_axis_name="core", subcore_axis_name="subcore")


def _sc_scatter_rows(src, idx, n_out):
    d = src.shape[1]
    wins = idx.shape[1] // SC_WINDOW

    def body(src_hbm, idx_hbm, out_hbm, sem):
        def step(src_vmem, idx_vmem):
            copies = [pltpu.make_async_copy(src_vmem, out_hbm.at[idx_vmem.at[k]], sem)
                      for k in range(TOP_K)]
            for copy in copies:
                copy.start()
            for copy in copies:
                copy.wait()

        pltpu.emit_pipeline(
            step,
            grid=(src.shape[0] // SC_WINDOW,),
            in_specs=[pl.BlockSpec((SC_WINDOW, d), index_map=lambda i: (i, 0)),
                      pl.BlockSpec((TOP_K, SC_WINDOW), index_map=lambda i: (i // wins, i % wins))],
            out_specs=[],
            core_axis_name=("core", "subcore"),
            dimension_semantics=(pltpu.PARALLEL,),
        )(src_hbm, idx_hbm)

    return pl.kernel(body, out_type=jax.ShapeDtypeStruct((n_out, d), src.dtype),
                     mesh=_sc_mesh(), scratch_types=[pltpu.SemaphoreType.DMA],
                     name="sc_scatter_rows")(src, idx)


def _sc_gather_sum(table, idx, gates):
    d = table.shape[1]
    n_tok = idx.shape[1]
    wins = n_tok // SC_WINDOW
    lanes = SC_LANES
    out_type = jax.ShapeDtypeStruct((PIECES * n_tok, d), _F32)

    def tree_sum(terms):
        while len(terms) > 1:
            terms = [terms[i] + terms[i + 1] for i in range(0, len(terms), 2)]
        return terms[0]

    def body(table_hbm, idx_hbm, gate_hbm, lo_hbm, hi_hbm, rows_vmem, sem):
        def step(idx_vmem, gate_vmem, lo_vmem, hi_vmem):
            def burst_copies(b):
                return [pltpu.make_async_copy(
                    table_hbm.at[idx_vmem.at[k, pl.ds(b * SC_BURST, SC_BURST)]],
                    rows_vmem.at[b % 2, k], sem.at[b % 2]) for k in range(TOP_K)]

            n_bursts = SC_WINDOW // SC_BURST
            for copy in burst_copies(0):
                copy.start()
            for b in range(n_bursts):
                if b + 1 < n_bursts:
                    for copy in burst_copies(b + 1):
                        copy.start()
                for copy in burst_copies(b):
                    copy.wait()
                t0 = b * SC_BURST
                rows = rows_vmem.at[b % 2]

                @plsc.parallel_loop(0, SC_BURST)
                def _(r):
                    t = t0 + r
                    token = jnp.full((lanes,), t, jnp.int32)
                    gate = [plsc.load_gather(gate_vmem, [jnp.full((lanes,), k, jnp.int32), token])
                            for k in range(TOP_K)]
                    for j in range(0, d, lanes):
                        words = [rows[k, r, pl.ds(j, lanes)] for k in range(TOP_K)]
                        lo_vmem[t, pl.ds(j, lanes)] = tree_sum(
                            [gate[k] * lax.bitcast_convert_type(words[k] << 16, _F32)
                             for k in range(TOP_K)])
                        hi_vmem[t, pl.ds(j, lanes)] = tree_sum(
                            [gate[k] * lax.bitcast_convert_type(words[k] & _U32(0xFFFF0000), _F32)
                             for k in range(TOP_K)])

        window = lambda i: (i // wins, i % wins)
        pltpu.emit_pipeline(
            step,
            grid=(PIECES * wins,),
            in_specs=[pl.BlockSpec((TOP_K, SC_WINDOW), index_map=window),
                      pl.BlockSpec((TOP_K, SC_WINDOW), index_map=lambda i: (0, i % wins))],
            out_specs=[pl.BlockSpec((SC_WINDOW, d), index_map=lambda i: (i, 0)),
                       pl.BlockSpec((SC_WINDOW, d), index_map=lambda i: (i, 0))],
            core_axis_name=("core", "subcore"),
            dimension_semantics=(pltpu.PARALLEL,),
        )(idx_hbm, gate_hbm, lo_hbm, hi_hbm)

    return pl.kernel(body, out_type=(out_type, out_type), mesh=_sc_mesh(),
                     scratch_types=[pltpu.VMEM((2, TOP_K, SC_BURST, d), _U32),
                                    pltpu.SemaphoreType.DMA((2,))],
                     compiler_params=pltpu.CompilerParams(needs_layout_passes=False),
                     name="sc_gather_sum")(table, idx, gates)


def _index_kernel(start_ref, ek_ref, rk_ref, idx_ref, *, n_rows):
    e = ek_ref[...]
    start = jnp.zeros(e.shape, jnp.int32)
    for ex in range(N_EXPERTS):
        start = jnp.where(e == ex, start_ref[ex], start)
    dest = start + rk_ref[...]
    for c in range(PIECES):
        idx_ref[c] = dest + c * n_rows


def _row_indices(padded_start, e_k, r_k, n_rows):
    n_tok = e_k.shape[1]
    ts = INDEX_TILE
    kspec = pl.BlockSpec((TOP_K, ts), lambda i, st: (0, i))
    grid_spec = pltpu.PrefetchScalarGridSpec(
        num_scalar_prefetch=1,
        grid=(n_tok // ts,),
        in_specs=[kspec, kspec],
        out_specs=pl.BlockSpec((PIECES, TOP_K, ts), lambda i, st: (0, 0, i)),
    )
    return pl.pallas_call(
        functools.partial(_index_kernel, n_rows=n_rows),
        grid_spec=grid_spec,
        out_shape=jax.ShapeDtypeStruct((PIECES, TOP_K, n_tok), jnp.int32),
        compiler_params=pltpu.CompilerParams(dimension_semantics=("arbitrary",)),
        name="moe_row_indices",
    )(padded_start, e_k, r_k)


def _expert_kernel(chunk0_ref, nchunk_ref, cnt_ref, total_ref,
                   xs_hbm, wg_ref, wu_ref, wd_ref, o_hbm,
                   xbuf, obuf, sem_in, sem_out):
    e = pl.program_id(0)
    total = total_ref[0]
    ch = EXPERT_ROWS
    ahead = IN_SLOTS - 1

    def in_copy(g):
        slot = g % IN_SLOTS
        return pltpu.make_async_copy(xs_hbm.at[:, pl.ds(g * ch, ch), :], xbuf.at[slot],
                                     sem_in.at[slot])

    def out_copy(g):
        slot = g % OUT_SLOTS
        return pltpu.make_async_copy(obuf.at[slot], o_hbm.at[:, pl.ds(g * ch, ch), :],
                                     sem_out.at[slot])

    @pl.when(e == 0)
    def _():
        obuf[...] = jnp.zeros_like(obuf)
        for g in range(ahead):
            @pl.when(g < total)
            def _():
                in_copy(g).start()


    def chunk(j, carry):
        g = chunk0_ref[e] + j
        in_copy(g).wait()

        @pl.when(g + ahead < total)
        def _():
            in_copy(g + ahead).start()

        @pl.when(g >= OUT_SLOTS)
        def _():
            out_copy(g - OUT_SLOTS).wait()

        islot = g % IN_SLOTS
        oslot = g % OUT_SLOTS
        valid = cnt_ref[e] - j * ch

        def gate_up(s):
            r0 = s * EXPERT_SUB
            w = jnp.concatenate([xbuf[islot, c, r0:r0 + EXPERT_SUB, :] for c in range(PIECES)],
                                axis=1)
            lo, hi = _unpack_halves(w)
            gate = _dot(lo, wg_ref[0, 0, :PACKED, :]) + _dot(hi, wg_ref[0, 0, PACKED:, :])
            up = _dot(lo, wu_ref[0, 0, :PACKED, :]) + _dot(hi, wu_ref[0, 0, PACKED:, :])
            return gate, up

        def down(s, gate_up_pair):
            r0 = s * EXPERT_SUB
            gate, up = gate_up_pair
            y = _pack_halves(_dot(_silu(gate) * up, wd_ref[0, 0]))
            for c in range(PIECES):
                obuf[oslot, c, r0:r0 + EXPERT_SUB, :] = y[:, c * LANES:(c + 1) * LANES]

        n_sub = ch // EXPERT_SUB

        all_subs = valid > ch - EXPERT_SUB

        @pl.when(all_subs)
        def _():
            projected = [gate_up(s) for s in range(n_sub)]
            for s, gu in enumerate(projected):
                down(s, gu)

        @pl.when(jnp.logical_not(all_subs))
        def _():
            for s in range(n_sub - 1):
                @pl.when(s * EXPERT_SUB < valid)
                def _():
                    down(s, gate_up(s))

        out_copy(g).start()
        return carry

    lax.fori_loop(0, nchunk_ref[e], chunk, 0)

    @pl.when(e == pl.num_programs(0) - 1)
    def _():
        for back in range(OUT_SLOTS, 0, -1):
            @pl.when(total >= back)
            def _():
                out_copy(total - back).wait()


def _experts(layer, chunk0, nchunk, counts, total, xs, w_gate, w_up, w_down):
    d = D_MODEL
    ch = EXPERT_ROWS
    wspec_in = pl.BlockSpec((1, 1, d, EXPERT_DIM), lambda e, *_: (layer, e, 0, 0))
    any_spec = pl.BlockSpec(memory_space=pl.ANY)
    grid_spec = pltpu.PrefetchScalarGridSpec(
        num_scalar_prefetch=4,
        grid=(N_EXPERTS,),
        in_specs=[
            any_spec,
            wspec_in,
            wspec_in,
            pl.BlockSpec((1, 1, EXPERT_DIM, d), lambda e, *_: (layer, e, 0, 0)),
        ],
        out_specs=any_spec,
        scratch_shapes=[pltpu.VMEM((IN_SLOTS, PIECES, ch, LANES), _U32),
                        pltpu.VMEM((OUT_SLOTS, PIECES, ch, LANES), _U32),
                        pltpu.SemaphoreType.DMA((IN_SLOTS,)),
                        pltpu.SemaphoreType.DMA((OUT_SLOTS,))],
    )
    return pl.pallas_call(
        _expert_kernel,
        grid_spec=grid_spec,
        out_shape=jax.ShapeDtypeStruct(xs.shape, _U32),
        compiler_params=pltpu.CompilerParams(
            dimension_semantics=("arbitrary",),
            vmem_limit_bytes=VMEM_LIMIT),
        name="moe_experts",
    )(chunk0, nchunk, counts, total, xs, w_gate, w_up, w_down)


def _moe_residual(x, lo_ref, hi_ref, rows, sg_ref, su_ref, sd_ref, g_ref, b_ref):
    xb = x.astype(_BF16)
    routed = jnp.concatenate([lo_ref[c, rows, :] for c in range(PIECES)] +
                             [hi_ref[c, rows, :] for c in range(PIECES)], axis=1)
    h = _silu(_dot(xb, sg_ref[...])) * _dot(xb, su_ref[...])
    shared = _dot(h.astype(_BF16), sd_ref[...])
    return _layer_norm(DEEPNORM_ALPHA * x + (routed + shared), g_ref[...], b_ref[...])


def _cast_shared(f32_refs, bf16_refs):
    @pl.when(pl.program_id(0) == 0)
    def _():
        for src, dst in zip(f32_refs, bf16_refs):
            dst[...] = src[0].astype(_BF16)


def _combine_kernel(x_ref, lo_ref, hi_ref, sg_ref, su_ref, sd_ref, g_ref, b_ref, *rest):
    o_ref, *shared = rest[-4:]
    _cast_shared((sg_ref, su_ref, sd_ref), shared)
    o_ref[...] = _moe_residual(x_ref[...], lo_ref, hi_ref, slice(None),
                               *shared, g_ref, b_ref)


def _combine_sgu_kernel(x_ref, lo_ref, hi_ref, sg_ref, su_ref, sd_ref, fg_ref, fb_ref,
                        win_ref, bin_ref, lng_ref, lnb_ref, ws_ref, bs_ref, wout_ref,
                        mg_ref, mb_ref, o_ref, op_ref, gated_ref, h_ref, *shared):
    _cast_shared((sg_ref, su_ref, sd_ref), shared)
    for p0 in range(0, x_ref.shape[0], SGU_SUB):
        rows = slice(p0, p0 + SGU_SUB)
        h_ref[rows, :] = _moe_residual(x_ref[rows, :], lo_ref, hi_ref, rows,
                                       *shared, fg_ref, fb_ref)
    _sgu_kernel(h_ref, win_ref, bin_ref, lng_ref, lnb_ref, ws_ref, bs_ref, wout_ref,
                mg_ref, mb_ref, o_ref, op_ref, gated_ref)


def _shared_specs(layer):
    d = D_MODEL
    return [pl.BlockSpec((1, d, EXPERT_DIM), lambda i: (layer, 0, 0)),
            pl.BlockSpec((1, d, EXPERT_DIM), lambda i: (layer, 0, 0)),
            pl.BlockSpec((1, EXPERT_DIM, d), lambda i: (layer, 0, 0))]


def _shared_scratch():
    d = D_MODEL
    return [pltpu.VMEM((d, EXPERT_DIM), _BF16), pltpu.VMEM((d, EXPERT_DIM), _BF16),
            pltpu.VMEM((EXPERT_DIM, d), _BF16)]


def _combine(layer, x, routed_lo, routed_hi, sh_gate, sh_up, sh_down, ln_g, ln_b,
             out_parts, part, prev):
    n_part, d = x.shape
    ts = COMBINE_TILE
    steps = n_part // ts
    const2 = lambda i: (0, 0)
    in_specs = [
        pl.BlockSpec((ts, d), lambda i: (i, 0)),
        pl.BlockSpec((PIECES, ts, LANES), lambda i: (0, i, 0)),
        pl.BlockSpec((PIECES, ts, LANES), lambda i: (0, i, 0)),
        *_shared_specs(layer),
        pl.BlockSpec((1, d), const2),
        pl.BlockSpec((1, d), const2),
    ]
    args = [x, routed_lo, routed_hi, sh_gate, sh_up, sh_down,
            ln_g.reshape(1, d), ln_b.reshape(1, d)]
    aliases = {}
    if prev is not None:
        in_specs.append(pl.BlockSpec(memory_space=pl.ANY))
        aliases = {len(args): 0}
        args.append(prev)
    return pl.pallas_call(
        _combine_kernel,
        grid=(steps,),
        in_specs=in_specs,
        out_specs=pl.BlockSpec((ts, d), lambda i: (part * steps + i, 0)),
        out_shape=jax.ShapeDtypeStruct((out_parts * n_part, d), _F32),
        scratch_shapes=_shared_scratch(),
        input_output_aliases=aliases,
        compiler_params=pltpu.CompilerParams(
            dimension_semantics=("arbitrary",),
            vmem_limit_bytes=VMEM_LIMIT),
        name="moe_combine",
    )(*args)


def _combine_sgu(layer, x, routed_lo, routed_hi, sh_gate, sh_up, sh_down, ffn_g, ffn_b,
                 w_in, b_in, ln_g, ln_b, w_s, b_s, w_out, mix_g, mix_b):
    n_part, d = x.shape
    ts = SGU_TILE
    const2 = lambda i: (0, 0)
    const3 = lambda i: (0, 0, 0)
    causal = jnp.tril(jnp.ones((SGU_CHUNK, SGU_CHUNK), w_s.dtype))
    ws = (w_s * causal[None]).astype(_BF16)
    tile = pl.BlockSpec((ts, d), lambda i: (i, 0))
    ptile = pl.BlockSpec((PIECES, ts, LANES), lambda i: (0, i, 0))
    row = lambda n: pl.BlockSpec((1, n), const2)
    return pl.pallas_call(
        _combine_sgu_kernel,
        grid=(n_part // ts,),
        in_specs=[
            tile, ptile, ptile,
            *_shared_specs(layer),
            row(d), row(d),
            pl.BlockSpec((d, 2 * SGU_WIDTH), const2),
            row(2 * SGU_WIDTH), row(SGU_WIDTH), row(SGU_WIDTH),
            pl.BlockSpec((SGU_HEADS, SGU_CHUNK, SGU_CHUNK), const3),
            pl.BlockSpec((SGU_HEADS, SGU_CHUNK, 1), const3),
            pl.BlockSpec((SGU_WIDTH, d), const2),
            row(d), row(d),
        ],
        out_specs=(tile, ptile),
        out_shape=(jax.ShapeDtypeStruct((n_part, d), _F32),
                   jax.ShapeDtypeStruct((PIECES, n_part, LANES), _U32)),
        scratch_shapes=[pltpu.VMEM((ts, SGU_WIDTH), _BF16), pltpu.VMEM((ts, d), _F32),
                        *_shared_scratch()],
        compiler_params=pltpu.CompilerParams(
            dimension_semantics=("arbitrary",),
            vmem_limit_bytes=VMEM_LIMIT),
        name="moe_combine_sgu_mixer",
    )(x, routed_lo, routed_hi, sh_gate, sh_up, sh_down,
      ffn_g.reshape(1, d), ffn_b.reshape(1, d),
      w_in.astype(_BF16), b_in.reshape(1, -1), ln_g.reshape(1, -1), ln_b.reshape(1, -1),
      ws, b_s.reshape(SGU_HEADS, SGU_CHUNK, 1), w_out.astype(_BF16),
      mix_g.reshape(1, d), mix_b.reshape(1, d))


def _moe_dispatch(layer, x, x_packed, w_router, router_bias, w_gate, w_up, w_down):
    n_part, d = x.shape
    bm = EXPERT_ROWS
    n_rows = n_part * TOP_K + N_EXPERTS * bm
    e_k, r_k, g_k, counts = _route(x, w_router, router_bias)
    counts = counts.reshape(N_EXPERTS).astype(jnp.int32)
    padded = (counts + bm - 1) // bm * bm
    padded_end = jnp.cumsum(padded)
    padded_start = padded_end - padded
    idx = _row_indices(padded_start, e_k, r_k, n_rows).reshape(PIECES * TOP_K, n_part)
    xs = _sc_scatter_rows(x_packed.reshape(PIECES * n_part, LANES), idx, PIECES * n_rows)
    rows = _experts(layer, padded_start // bm, padded // bm, counts, padded_end[-1:] // bm,
                    xs.reshape(PIECES, n_rows, LANES), w_gate, w_up, w_down)
    return rows.reshape(PIECES * n_rows, LANES), idx, g_k


def _routed_sums(x, rows, idx, gates):
    n_part = x.shape[0]
    lo, hi = _sc_gather_sum(rows, idx, gates)
    return lo.reshape(PIECES, n_part, LANES), hi.reshape(PIECES, n_part, LANES)


def kernel(x, pool_w_in, pool_w_grp, pool_scale, pool_w_out, sgu_w_in, sgu_b_in, sgu_ln_g, sgu_ln_b, sgu_w_s, sgu_b_s, sgu_w_out, ln_mix_g, ln_mix_b, moe_w_router, moe_router_bias, moe_w_gate, moe_w_up, moe_w_down, moe_sh_gate, moe_sh_up, moe_sh_down, ln_ffn_g, ln_ffn_b):
    bsz, seq, d = x.shape
    parts = range(TOKEN_PARTS)

    def dispatch(i, h, h_packed):
        return _moe_dispatch(i, h, h_packed, moe_w_router[i], moe_router_bias[i],
                             moe_w_gate, moe_w_up, moe_w_down)

    mixed = [_pool_layer(x, p, pool_w_in[0], pool_w_grp[0], pool_scale[0], pool_w_out[0],
                         ln_mix_g[0], ln_mix_b[0]) for p in parts]
    mixed = [(h.reshape(-1, d), h_packed) for h, h_packed in mixed]
    routed = [dispatch(0, h, h_packed) for h, h_packed in mixed]
    mixed = [_combine_sgu(0, mixed[p][0], *_routed_sums(mixed[p][0], *routed[p]),
                          moe_sh_gate, moe_sh_up, moe_sh_down, ln_ffn_g[0], ln_ffn_b[0],
                          sgu_w_in[0], sgu_b_in[0], sgu_ln_g[0], sgu_ln_b[0], sgu_w_s[0],
                          sgu_b_s[0], sgu_w_out[0], ln_mix_g[1], ln_mix_b[1]) for p in parts]
    routed = [dispatch(1, h, h_packed) for h, h_packed in mixed]
    out = None
    for p in parts:
        h = mixed[p][0]
        out = _combine(1, h, *_routed_sums(h, *routed[p]), moe_sh_gate, moe_sh_up,
                       moe_sh_down, ln_ffn_g[1], ln_ffn_b[1], TOKEN_PARTS, p, out)
    return out.reshape(bsz, seq, d)
```
